```python
import math
import jax, jax.numpy as jnp
from jax import lax
import numpy as np

D_MODEL = 1024
BATCH = 8
SEQ = 16384
DEPTH = 2

N_META = 16
NORM_EPS = 1e-6

ATTN_HEAD_DIM = 64
ATTN_HEADS = D_MODEL // ATTN_HEAD_DIM
ATTN_KV_HEADS = ATTN_HEADS // 8
ATTN_GROUPS = ATTN_HEADS // ATTN_KV_HEADS
ATTN_WIDTH = ATTN_HEADS * ATTN_HEAD_DIM
ATTN_KV_WIDTH = ATTN_KV_HEADS * ATTN_HEAD_DIM
ATTN_IN = 2 * ATTN_WIDTH + 2 * ATTN_KV_WIDTH
WINDOW = 128
ATTN_BLOCK = 128

DN_HEAD_DIM_K = 128
DN_HEAD_DIM_V = 128
DN_K_HEADS = D_MODEL // DN_HEAD_DIM_K
DN_V_HEADS = 2 * DN_K_HEADS
DN_KEY_WIDTH = DN_K_HEADS * DN_HEAD_DIM_K
DN_VALUE_WIDTH = DN_V_HEADS * DN_HEAD_DIM_V
DN_CONV = 4
DN_CHUNK = 64
DN_CONV_WIDTH = 2 * DN_KEY_WIDTH + DN_VALUE_WIDTH
DN_IN = DN_CONV_WIDTH + DN_VALUE_WIDTH + 2 * DN_V_HEADS

N_ATTN_LAYERS = (DEPTH + 1) // 2
N_DN_LAYERS = DEPTH // 2

kernel_name = "hybrid_swa_sink_alibi_gated_deltanet_meta"


def rms_norm(x, w):
    xf = x.astype(jnp.float32)
    y = xf * lax.rsqrt(jnp.mean(xf * xf, axis=-1, keepdims=True) + NORM_EPS)
    return (y * w.astype(jnp.float32)).astype(x.dtype)


def l2_norm(x):
    xf = x.astype(jnp.float32)
    return xf * lax.rsqrt(jnp.sum(xf * xf, axis=-1, keepdims=True) + NORM_EPS)


def alibi_slopes(n_heads):
    return jnp.asarray(np.exp2(-8.0 * np.arange(1, n_heads + 1) / n_heads), dtype=jnp.float32)


def banded_sink_attention(q, k, v, sinks):
    B, L = q.shape[:2]
    pad = ATTN_BLOCK - N_META
    Lp = L + pad
    nb = Lp // ATTN_BLOCK
    padt = lambda t: jnp.pad(t, ((0, 0), (pad, 0), (0, 0), (0, 0)))
    qb = padt(q).reshape(B, nb, ATTN_BLOCK, ATTN_KV_HEADS, ATTN_GROUPS, ATTN_HEAD_DIM)
    kb = padt(k).reshape(B, nb, ATTN_BLOCK, ATTN_KV_HEADS, ATTN_HEAD_DIM)
    vb = padt(v).reshape(B, nb, ATTN_BLOCK, ATTN_KV_HEADS, ATTN_HEAD_DIM)
    prev = lambda t: jnp.pad(t, ((0, 0), (1, 0), (0, 0), (0, 0), (0, 0)))[:, :-1]
    k_band = jnp.concatenate([prev(kb), kb], axis=2)
    v_band = jnp.concatenate([prev(vb), vb], axis=2)
    k_meta = k[:, :N_META]
    v_meta = v[:, :N_META]

    scale = ATTN_HEAD_DIM ** -0.5
    s_band = jnp.einsum('bnqhgd,bnkhd->bnhgqk', qb, k_band,
                        preferred_element_type=jnp.float32) * scale
    s_meta = jnp.einsum('bnqhgd,bmhd->bnhgqm', qb, k_meta,
                        preferred_element_type=jnp.float32) * scale

    pos_q = jnp.arange(Lp, dtype=jnp.int32).reshape(nb, ATTN_BLOCK) - pad
    pos_kb = jnp.concatenate([pos_q - ATTN_BLOCK, pos_q], axis=-1)
    pos_meta = jnp.arange(N_META, dtype=jnp.int32)
    dist_band = pos_q[:, :, None] - pos_kb[:, None, :]
    valid_band = (pos_kb[:, None, :] >= N_META) & (dist_band >= 0) & (dist_band < WINDOW)
    dist_meta = pos_q[:, :, None] - pos_meta[None, None, :]
    valid_meta = dist_meta >= 0

    slopes = alibi_slopes(ATTN_HEADS).reshape(1, 1, ATTN_KV_HEADS, ATTN_GROUPS, 1, 1)
    clipdist = lambda d: jnp.minimum(d, WINDOW).astype(jnp.float32)[None, :, None, None]
    s_band = jnp.where(valid_band[None, :, None, None], s_band - slopes * clipdist(dist_band), -jnp.inf)
    s_meta = jnp.where(valid_meta[None, :, None, None], s_meta - slopes * clipdist(dist_meta), -jnp.inf)

    sink = jnp.broadcast_to(
        sinks.astype(jnp.float32).reshape(1, 1, ATTN_KV_HEADS, ATTN_GROUPS, 1, 1),
        s_band.shape[:-1] + (1,))
    p = jax.nn.softmax(jnp.concatenate([s_band, s_meta, sink], axis=-1), axis=-1)
    p_band = p[..., :2 * ATTN_BLOCK].astype(v.dtype)
    p_meta = p[..., 2 * ATTN_BLOCK:2 * ATTN_BLOCK + N_META].astype(v.dtype)
    o = (jnp.einsum('bnhgqk,bnkhd->bnqhgd', p_band, v_band)
         + jnp.einsum('bnhgqm,bmhd->bnqhgd', p_meta, v_meta))
    return o.reshape(B, Lp, ATTN_WIDTH)[:, pad:]


def attention_mixer(h, norm_w, w_in, q_norm_w, k_norm_w, sinks, w_out):
    B, L, _ = h.shape
    u = rms_norm(h, norm_w) @ w_in
    q, k, v, gate = jnp.split(
        u, [ATTN_WIDTH, ATTN_WIDTH + ATTN_KV_WIDTH, ATTN_WIDTH + 2 * ATTN_KV_WIDTH], axis=-1)
    q = rms_norm(q.reshape(B, L, ATTN_HEADS, ATTN_HEAD_DIM), q_norm_w)
    k = rms_norm(k.reshape(B, L, ATTN_KV_HEADS, ATTN_HEAD_DIM), k_norm_w)
    v = v.reshape(B, L, ATTN_KV_HEADS, ATTN_HEAD_DIM)
    o = banded_sink_attention(q, k, v, sinks)
    return (o * jax.nn.silu(gate)) @ w_out


def causal_depthwise_conv(x, w):
    K, C = w.shape
    return lax.conv_general_dilated(
        x, w[:, None, :], window_strides=(1,), padding=[(K - 1, 0)],
        dimension_numbers=('NWC', 'WIO', 'NWC'), feature_group_count=C)


def chunked_gated_delta_rule(q, k, v, beta, g):
    B, L, H, _ = q.shape
    pad = DN_CHUNK - N_META
    Lc = L + pad
    n = Lc // DN_CHUNK

    def to_chunks(t):
        t = jnp.pad(t.astype(jnp.float32), [(0, 0), (pad, 0)] + [(0, 0)] * (t.ndim - 2))
        t = t.reshape((B, n, DN_CHUNK) + t.shape[2:])
        return jnp.swapaxes(jnp.moveaxis(t, 1, 0), 2, 3)

    xs = (to_chunks(q), to_chunks(k), to_chunks(v), to_chunks(beta), to_chunks(g))
    causal = jnp.tril(jnp.ones((DN_CHUNK, DN_CHUNK), dtype=bool))
    strict = jnp.tril(jnp.ones((DN_CHUNK, DN_CHUNK), dtype=bool), -1)
    eye = jnp.eye(DN_CHUNK, dtype=jnp.float32)

    def step(S, inp):
        qc, kc, vc, bc, gc = inp
        gcum = jnp.cumsum(gc, axis=-1)
        decay = jnp.exp(jnp.where(causal, gcum[..., :, None] - gcum[..., None, :], -jnp.inf))
        kb = kc * bc[..., None]
        m = jnp.where(strict, jnp.einsum('bhcd,bhsd->bhcs', kb, kc) * decay, 0.0)
        rhs = jnp.concatenate([vc * bc[..., None], kb * jnp.exp(gcum)[..., None]], axis=-1)
        sol = lax.linalg.triangular_solve(m + eye, rhs, left_side=True, lower=True,
                                          unit_diagonal=True)
        u, w = sol[..., :DN_HEAD_DIM_V], sol[..., DN_HEAD_DIM_V:]
        v_new = u - jnp.einsum('bhcd,bhdv->bhcv', w, S)
        attn = jnp.einsum('bhcd,bhsd->bhcs', qc, kc) * decay
        o = (jnp.einsum('bhcd,bhdv->bhcv', qc * jnp.exp(gcum)[..., None], S)
             + jnp.einsum('bhcs,bhsv->bhcv', attn, v_new))
        g_last = gcum[..., -1]
        k_state = kc * jnp.exp(g_last[..., None] - gcum)[..., None]
        S = S * jnp.exp(g_last)[..., None, None] + jnp.einsum('bhcd,bhcv->bhdv', k_state, v_new)
        return S, o

    S0 = jnp.zeros((B, H, DN_HEAD_DIM_K, DN_HEAD_DIM_V), jnp.float32)
    _, o = lax.scan(step, S0, xs)
    o = jnp.moveaxis(jnp.swapaxes(o, 2, 3), 0, 1).reshape(B, Lc, H, DN_HEAD_DIM_V)
    return o[:, pad:]


def deltanet_mixer(h, norm_w, w_in, conv_w, a_log, dt_bias, o_norm_w, w_out):
    B, L, _ = h.shape
    u = rms_norm(h, norm_w) @ w_in
    qkv, z, b, a = jnp.split(
        u, [DN_CONV_WIDTH, DN_CONV_WIDTH + DN_VALUE_WIDTH,
            DN_CONV_WIDTH + DN_VALUE_WIDTH + DN_V_HEADS], axis=-1)
    qkv = jax.nn.silu(causal_depthwise_conv(qkv, conv_w))
    q, k, v = jnp.split(qkv, [DN_KEY_WIDTH, 2 * DN_KEY_WIDTH], axis=-1)
    rep = DN_V_HEADS // DN_K_HEADS
    q = jnp.repeat(l2_norm(q.reshape(B, L, DN_K_HEADS, DN_HEAD_DIM_K)), rep, axis=2)
    k = jnp.repeat(l2_norm(k.reshape(B, L, DN_K_HEADS, DN_HEAD_DIM_K)), rep, axis=2)
    q = q * (DN_HEAD_DIM_K ** -0.5)
    v = v.reshape(B, L, DN_V_HEADS, DN_HEAD_DIM_V)
    beta = jax.nn.sigmoid(b.astype(jnp.float32))
    g = -jnp.exp(a_log.astype(jnp.float32)) * jax.nn.softplus(
        a.astype(jnp.float32) + dt_bias.astype(jnp.float32))
    o = chunked_gated_delta_rule(q, k, v, beta, g).astype(h.dtype)
    o = rms_norm(o, o_norm_w) * jax.nn.silu(z.reshape(B, L, DN_V_HEADS, DN_HEAD_DIM_V))
    return o.reshape(B, L, DN_VALUE_WIDTH) @ w_out


def _fwd_setup_inputs(seed: int = 0) -> dict:
    key = jax.random.key(seed)
    ks = jax.random.split(key, 20)
    f32 = jnp.float32
    nA, nB = N_ATTN_LAYERS, N_DN_LAYERS
    out_scale = 0.5
    dt = jnp.exp(jax.random.uniform(ks[13], (nB, DN_V_HEADS), f32,
                                    math.log(1e-3), math.log(1e-1)))
    return {
        "x": jax.random.normal(ks[0], (BATCH, SEQ, D_MODEL), f32),
        "meta_tokens": jax.random.normal(ks[1], (N_META, D_MODEL), f32),
        "attn_norm_w": 1.0 + 0.02 * jax.random.normal(ks[2], (nA, D_MODEL), f32),
        "attn_w_in": jax.random.normal(ks[3], (nA, D_MODEL, ATTN_IN), f32) * D_MODEL ** -0.5,
        "attn_q_norm_w": 1.0 + 0.02 * jax.random.normal(ks[4], (nA, ATTN_HEAD_DIM), f32),
        "attn_k_norm_w": 1.0 + 0.02 * jax.random.normal(ks[5], (nA, ATTN_HEAD_DIM), f32),
        "attn_sinks": 0.5 * jax.random.normal(ks[6], (nA, ATTN_HEADS), f32),
        "attn_w_out": jax.random.normal(ks[7], (nA, ATTN_WIDTH, D_MODEL), f32)
                      * ATTN_WIDTH ** -0.5 * out_scale,
        "dn_norm_w": 1.0 + 0.02 * jax.random.normal(ks[8], (nB, D_MODEL), f32),
        "dn_w_in": jax.random.normal(ks[9], (nB, D_MODEL, DN_IN), f32) * D_MODEL ** -0.5,
        "dn_conv_w": jax.random.normal(ks[10], (nB, DN_CONV, DN_CONV_WIDTH), f32) * DN_CONV ** -0.5,
        "dn_a_log": jnp.log(jax.random.uniform(ks[11], (nB, DN_V_HEADS), f32, 1.0, 16.0)),
        "dn_dt_bias": dt + jnp.log(-jnp.expm1(-dt)),
        "dn_o_norm_w": 1.0 + 0.02 * jax.random.normal(ks[12], (nB, DN_HEAD_DIM_V), f32),
        "dn_w_out": jax.random.normal(ks[14], (nB, DN_VALUE_WIDTH, D_MODEL), f32)
                    * DN_VALUE_WIDTH ** -0.5 * out_scale,
    }


def _fwd_reference(x, meta_tokens, attn_norm_w, attn_w_in, attn_q_norm_w, attn_k_norm_w,
              attn_sinks, attn_w_out, dn_norm_w, dn_w_in, dn_conv_w, dn_a_log,
              dn_dt_bias, dn_o_norm_w, dn_w_out):
    B = x.shape[0]
    meta = jnp.broadcast_to(meta_tokens.astype(x.dtype)[None], (B, N_META, x.shape[-1]))
    h = jnp.concatenate([meta, x], axis=1)
    for i in range(DEPTH):
        j = i // 2
        if i % 2 == 0:
            h = h + attention_mixer(h, attn_norm_w[j], attn_w_in[j], attn_q_norm_w[j],
                                    attn_k_norm_w[j], attn_sinks[j], attn_w_out[j])
        else:
            h = h + deltanet_mixer(h, dn_norm_w[j], dn_w_in[j], dn_conv_w[j], dn_a_log[j],
                                   dn_dt_bias[j], dn_o_norm_w[j], dn_w_out[j])
    return h[:, N_META:]


import jax as _jax
import jax.numpy as _jnp

TWIN_FORMAT = 'train_step'
FWD_PARAMS = ['x', 'meta_tokens', 'attn_norm_w', 'attn_w_in', 'attn_q_norm_w', 'attn_k_norm_w', 'attn_sinks', 'attn_w_out', 'dn_norm_w', 'dn_w_in', 'dn_conv_w', 'dn_a_log', 'dn_dt_bias', 'dn_o_norm_w', 'dn_w_out']
TWIN_WEIGHTS = ['meta_tokens', 'attn_norm_w', 'attn_w_in', 'attn_q_norm_w', 'attn_k_norm_w', 'attn_sinks', 'attn_w_out', 'dn_norm_w', 'dn_w_in', 'dn_conv_w', 'dn_a_log', 'dn_dt_bias', 'dn_o_norm_w', 'dn_w_out']
TWIN_DIFF_INPUT = 'x'
TWIN_INPUTS = ['x', 'meta_tokens', 'attn_norm_w', 'attn_w_in', 'attn_q_norm_w', 'attn_k_norm_w', 'attn_sinks', 'attn_w_out', 'dn_norm_w', 'dn_w_in', 'dn_conv_w', 'dn_a_log', 'dn_dt_bias', 'dn_o_norm_w', 'dn_w_out', 'loss_target', 'm_meta_tokens', 'm_attn_norm_w', 'm_attn_w_in', 'm_attn_q_norm_w', 'm_attn_k_norm_w', 'm_attn_sinks', 'm_attn_w_out', 'm_dn_norm_w', 'm_dn_w_in', 'm_dn_conv_w', 'm_dn_a_log', 'm_dn_dt_bias', 'm_dn_o_norm_w', 'm_dn_w_out', 'v_meta_tokens', 'v_attn_norm_w', 'v_attn_w_in', 'v_attn_q_norm_w', 'v_attn_k_norm_w', 'v_attn_sinks', 'v_attn_w_out', 'v_dn_norm_w', 'v_dn_w_in', 'v_dn_conv_w', 'v_dn_a_log', 'v_dn_dt_bias', 'v_dn_o_norm_w', 'v_dn_w_out']
TWIN_OUTPUTS = ['loss', 'grad_x', 'grad_meta_tokens', 'grad_attn_norm_w', 'grad_attn_w_in', 'grad_attn_q_norm_w', 'grad_attn_k_norm_w', 'grad_attn_sinks', 'grad_attn_w_out', 'grad_dn_norm_w', 'grad_dn_w_in', 'grad_dn_conv_w', 'grad_dn_a_log', 'grad_dn_dt_bias', 'grad_dn_o_norm_w', 'grad_dn_w_out', 'delta_meta_tokens', 'delta_attn_norm_w', 'delta_attn_w_in', 'delta_attn_q_norm_w', 'delta_attn_k_norm_w', 'delta_attn_sinks', 'delta_attn_w_out', 'delta_dn_norm_w', 'delta_dn_w_in', 'delta_dn_conv_w', 'delta_dn_a_log', 'delta_dn_dt_bias', 'delta_dn_o_norm_w', 'delta_dn_w_out', 'new_m_meta_tokens', 'new_m_attn_norm_w', 'new_m_attn_w_in', 'new_m_attn_q_norm_w', 'new_m_attn_k_norm_w', 'new_m_attn_sinks', 'new_m_attn_w_out', 'new_m_dn_norm_w', 'new_m_dn_w_in', 'new_m_dn_conv_w', 'new_m_dn_a_log', 'new_m_dn_dt_bias', 'new_m_dn_o_norm_w', 'new_m_dn_w_out', 'new_v_meta_tokens', 'new_v_attn_norm_w', 'new_v_attn_w_in', 'new_v_attn_q_norm_w', 'new_v_attn_k_norm_w', 'new_v_attn_sinks', 'new_v_attn_w_out', 'new_v_dn_norm_w', 'new_v_dn_w_in', 'new_v_dn_conv_w', 'new_v_dn_a_log', 'new_v_dn_dt_bias', 'new_v_dn_o_norm_w', 'new_v_dn_w_out']
TWIN_LEAF_KINDS = {'loss': 'loss', 'grad_x': 'grad_x', 'grad_meta_tokens': 'grad_w', 'grad_attn_norm_w': 'grad_w', 'grad_attn_w_in': 'grad_w', 'grad_attn_q_norm_w': 'grad_w', 'grad_attn_k_norm_w': 'grad_w', 'grad_attn_sinks': 'grad_w', 'grad_attn_w_out': 'grad_w', 'grad_dn_norm_w': 'grad_w', 'grad_dn_w_in': 'grad_w', 'grad_dn_conv_w': 'grad_w', 'grad_dn_a_log': 'grad_w', 'grad_dn_dt_bias': 'grad_w', 'grad_dn_o_norm_w': 'grad_w', 'grad_dn_w_out': 'grad_w', 'delta_meta_tokens': 'delta_w', 'delta_attn_norm_w': 'delta_w', 'delta_attn_w_in': 'delta_w', 'delta_attn_q_norm_w': 'delta_w', 'delta_attn_k_norm_w': 'delta_w', 'delta_attn_sinks': 'delta_w', 'delta_attn_w_out': 'delta_w', 'delta_dn_norm_w': 'delta_w', 'delta_dn_w_in': 'delta_w', 'delta_dn_conv_w': 'delta_w', 'delta_dn_a_log': 'delta_w', 'delta_dn_dt_bias': 'delta_w', 'delta_dn_o_norm_w': 'delta_w', 'delta_dn_w_out': 'delta_w', 'new_m_meta_tokens': 'new_m', 'new_m_attn_norm_w': 'new_m', 'new_m_attn_w_in': 'new_m', 'new_m_attn_q_norm_w': 'new_m', 'new_m_attn_k_norm_w': 'new_m', 'new_m_attn_sinks': 'new_m', 'new_m_attn_w_out': 'new_m', 'new_m_dn_norm_w': 'new_m', 'new_m_dn_w_in': 'new_m', 'new_m_dn_conv_w': 'new_m', 'new_m_dn_a_log': 'new_m', 'new_m_dn_dt_bias': 'new_m', 'new_m_dn_o_norm_w': 'new_m', 'new_m_dn_w_out': 'new_m', 'new_v_meta_tokens': 'new_v', 'new_v_attn_norm_w': 'new_v', 'new_v_attn_w_in': 'new_v', 'new_v_attn_q_norm_w': 'new_v', 'new_v_attn_k_norm_w': 'new_v', 'new_v_attn_sinks': 'new_v', 'new_v_attn_w_out': 'new_v', 'new_v_dn_norm_w': 'new_v', 'new_v_dn_w_in': 'new_v', 'new_v_dn_conv_w': 'new_v', 'new_v_dn_a_log': 'new_v', 'new_v_dn_dt_bias': 'new_v', 'new_v_dn_o_norm_w': 'new_v', 'new_v_dn_w_out': 'new_v'}


def _forward(args):
    return _fwd_reference(*[args[k] for k in FWD_PARAMS])


def _output_shape():
    def fwd():
        inp = _fwd_setup_inputs(0)
        return _fwd_reference(*[inp[k] for k in FWD_PARAMS])
    out = _jax.eval_shape(fwd)
    return out.shape, out.dtype

N_MICROBATCH = 1
ADAM_LR = 0.001
ADAM_B1 = 0.9
ADAM_B2 = 0.999
ADAM_EPS = 1e-08
ADAM_WD = 0.01
ADAM_STEP = 10
PER_EXAMPLE_BATCH_AXIS = {'x': 0, 'loss_target': 0}
SHARED_INPUTS = []
_WEIGHT_DTYPES = {'meta_tokens': _jnp.float32, 'attn_norm_w': _jnp.float32, 'attn_w_in': _jnp.float32, 'attn_q_norm_w': _jnp.float32, 'attn_k_norm_w': _jnp.float32, 'attn_sinks': _jnp.float32, 'attn_w_out': _jnp.float32, 'dn_norm_w': _jnp.float32, 'dn_w_in': _jnp.float32, 'dn_conv_w': _jnp.float32, 'dn_a_log': _jnp.float32, 'dn_dt_bias': _jnp.float32, 'dn_o_norm_w': _jnp.float32, 'dn_w_out': _jnp.float32}
MOMENT_SCALE = {'meta_tokens': 8.480208e-03, 'attn_norm_w': 1.821528e+00, 'attn_w_in': 7.229077e-02, 'attn_q_norm_w': 7.014269e+00, 'attn_k_norm_w': 6.968341e+00, 'attn_sinks': 1.329501e+01, 'attn_w_out': 1.146448e-01, 'dn_norm_w': 1.312010e+01, 'dn_w_in': 1.780207e-01, 'dn_conv_w': 2.450547e-01, 'dn_a_log': 2.108428e+01, 'dn_dt_bias': 2.017208e+01, 'dn_o_norm_w': 8.646244e+01, 'dn_w_out': 9.883514e-01}


def _to_microbatches(a, axis):
    t = _jnp.moveaxis(a, axis, 0)
    t = t.reshape((N_MICROBATCH, t.shape[0] // N_MICROBATCH) + t.shape[1:])
    return _jnp.moveaxis(t, 1, axis + 1)


def setup_inputs(seed: int = 0) -> dict:
    inp = _fwd_setup_inputs(seed)
    key = _jax.random.fold_in(_jax.random.key(seed), 7919)
    shape, _ = _output_shape()
    out = dict(inp)
    out["loss_target"] = _jax.random.normal(_jax.random.fold_in(key, 0), shape, _jnp.float32)
    for i, name in enumerate(TWIN_WEIGHTS):
        w = inp[name].astype(_jnp.float32)
        if MOMENT_SCALE is None:
            s = _jnp.sqrt(_jnp.mean(_jnp.square(w)) + 1e-30)
        else:
            s = MOMENT_SCALE[name]
        km, kv = _jax.random.split(_jax.random.fold_in(key, i + 1))
        out[name] = w
        out["m_" + name] = s * _jax.random.normal(km, w.shape, _jnp.float32)
        out["v_" + name] = (s * s) * _jax.random.uniform(kv, w.shape, _jnp.float32, 0.5, 1.5)
    if N_MICROBATCH > 1:
        for name, axis in PER_EXAMPLE_BATCH_AXIS.items():
            out[name] = _to_microbatches(out[name], axis)
    return {'x': out['x'], 'meta_tokens': out['meta_tokens'], 'attn_norm_w': out['attn_norm_w'], 'attn_w_in': out['attn_w_in'], 'attn_q_norm_w': out['attn_q_norm_w'], 'attn_k_norm_w': out['attn_k_norm_w'], 'attn_sinks': out['attn_sinks'], 'attn_w_out': out['attn_w_out'], 'dn_norm_w': out['dn_norm_w'], 'dn_w_in': out['dn_w_in'], 'dn_conv_w': out['dn_conv_w'], 'dn_a_log': out['dn_a_log'], 'dn_dt_bias': out['dn_dt_bias'], 'dn_o_norm_w': out['dn_o_norm_w'], 'dn_w_out': out['dn_w_out'], 'loss_target': out['loss_target'], 'm_meta_tokens': out['m_meta_tokens'], 'm_attn_norm_w': out['m_attn_norm_w'], 'm_attn_w_in': out['m_attn_w_in'], 'm_attn_q_norm_w': out['m_attn_q_norm_w'], 'm_attn_k_norm_w': out['m_attn_k_norm_w'], 'm_attn_sinks': out['m_attn_sinks'], 'm_attn_w_out': out['m_attn_w_out'], 'm_dn_norm_w': out['m_dn_norm_w'], 'm_dn_w_in': out['m_dn_w_in'], 'm_dn_conv_w': out['m_dn_conv_w'], 'm_dn_a_log': out['m_dn_a_log'], 'm_dn_dt_bias': out['m_dn_dt_bias'], 'm_dn_o_norm_w': out['m_dn_o_norm_w'], 'm_dn_w_out': out['m_dn_w_out'], 'v_meta_tokens': out['v_meta_tokens'], 'v_attn_norm_w': out['v_attn_norm_w'], 'v_attn_w_in': out['v_attn_w_in'], 'v_attn_q_norm_w': out['v_attn_q_norm_w'], 'v_attn_k_norm_w': out['v_attn_k_norm_w'], 'v_attn_sinks': out['v_attn_sinks'], 'v_attn_w_out': out['v_attn_w_out'], 'v_dn_norm_w': out['v_dn_norm_w'], 'v_dn_w_in': out['v_dn_w_in'], 'v_dn_conv_w': out['v_dn_conv_w'], 'v_dn_a_log': out['v_dn_a_log'], 'v_dn_dt_bias': out['v_dn_dt_bias'], 'v_dn_o_norm_w': out['v_dn_o_norm_w'], 'v_dn_w_out': out['v_dn_w_out']}


def _loss(weights, diff, rest, loss_target):
    with _jax.named_scope("forward"):
        args = {**rest, TWIN_DIFF_INPUT: diff, **{k: w.astype(_WEIGHT_DTYPES[k]) for k, w in weights.items()}}
        y = _forward(args)
    with _jax.named_scope("loss_head"):
        err = _jnp.square(y.astype(_jnp.float32) - loss_target)
        return 0.5 * _jnp.sum(_jnp.mean(err, axis=-1)) if err.ndim else 0.5 * err


def _adamw(w, g, m, v):
    m = ADAM_B1 * m + (1.0 - ADAM_B1) * g
    v = ADAM_B2 * v + (1.0 - ADAM_B2) * _jnp.square(g)
    m_hat = m / (1.0 - ADAM_B1 ** ADAM_STEP)
    v_hat = v / (1.0 - ADAM_B2 ** ADAM_STEP)
    delta = -ADAM_LR * (m_hat / (_jnp.sqrt(v_hat) + ADAM_EPS) + ADAM_WD * w)
    return delta, m, v


def reference(x, meta_tokens, attn_norm_w, attn_w_in, attn_q_norm_w, attn_k_norm_w, attn_sinks, attn_w_out, dn_norm_w, dn_w_in, dn_conv_w, dn_a_log, dn_dt_bias, dn_o_norm_w, dn_w_out, loss_target, m_meta_tokens, m_attn_norm_w, m_attn_w_in, m_attn_q_norm_w, m_attn_k_norm_w, m_attn_sinks, m_attn_w_out, m_dn_norm_w, m_dn_w_in, m_dn_conv_w, m_dn_a_log, m_dn_dt_bias, m_dn_o_norm_w, m_dn_w_out, v_meta_tokens, v_attn_norm_w, v_attn_w_in, v_attn_q_norm_w, v_attn_k_norm_w, v_attn_sinks, v_attn_w_out, v_dn_norm_w, v_dn_w_in, v_dn_conv_w, v_dn_a_log, v_dn_dt_bias, v_dn_o_norm_w, v_dn_w_out):
    given = dict(x=x, meta_tokens=meta_tokens, attn_norm_w=attn_norm_w, attn_w_in=attn_w_in, attn_q_norm_w=attn_q_norm_w, attn_k_norm_w=attn_k_norm_w, attn_sinks=attn_sinks, attn_w_out=attn_w_out, dn_norm_w=dn_norm_w, dn_w_in=dn_w_in, dn_conv_w=dn_conv_w, dn_a_log=dn_a_log, dn_dt_bias=dn_dt_bias, dn_o_norm_w=dn_o_norm_w, dn_w_out=dn_w_out, loss_target=loss_target, m_meta_tokens=m_meta_tokens, m_attn_norm_w=m_attn_norm_w, m_attn_w_in=m_attn_w_in, m_attn_q_norm_w=m_attn_q_norm_w, m_attn_k_norm_w=m_attn_k_norm_w, m_attn_sinks=m_attn_sinks, m_attn_w_out=m_attn_w_out, m_dn_norm_w=m_dn_norm_w, m_dn_w_in=m_dn_w_in, m_dn_conv_w=m_dn_conv_w, m_dn_a_log=m_dn_a_log, m_dn_dt_bias=m_dn_dt_bias, m_dn_o_norm_w=m_dn_o_norm_w, m_dn_w_out=m_dn_w_out, v_meta_tokens=v_meta_tokens, v_attn_norm_w=v_attn_norm_w, v_attn_w_in=v_attn_w_in, v_attn_q_norm_w=v_attn_q_norm_w, v_attn_k_norm_w=v_attn_k_norm_w, v_attn_sinks=v_attn_sinks, v_attn_w_out=v_attn_w_out, v_dn_norm_w=v_dn_norm_w, v_dn_w_in=v_dn_w_in, v_dn_conv_w=v_dn_conv_w, v_dn_a_log=v_dn_a_log, v_dn_dt_bias=v_dn_dt_bias, v_dn_o_norm_w=v_dn_o_norm_w, v_dn_w_out=v_dn_w_out)
    weights = {n: given[n] for n in TWIN_WEIGHTS}
    shared = {n: given[n] for n in SHARED_INPUTS}
    per_example = {n: given[n] for n in ['x']}
    grad_fn = _jax.value_and_grad(_loss, argnums=(0, 1))

    def one_microbatch(ex, loss_target):
        ex = dict(ex)
        diff = ex.pop(TWIN_DIFF_INPUT)
        return grad_fn(weights, diff, {**shared, **ex}, loss_target)

    if N_MICROBATCH == 1:
        loss, (grad_w, grad_x) = one_microbatch(per_example, given["loss_target"])
    else:
        def body(carry, xs):
            loss_sum, grad_sum = carry
            l_k, (gw_k, gx_k) = one_microbatch(xs[0], xs[1])
            with _jax.named_scope("update"):
                return (loss_sum + l_k, _jax.tree.map(_jnp.add, grad_sum, gw_k)), gx_k

        init = (_jnp.zeros((), _jnp.float32), _jax.tree.map(_jnp.zeros_like, weights))
        (loss, grad_w), grad_x = _jax.lax.scan(body, init, (per_example, given["loss_target"]))
    with _jax.named_scope("update"):
        delta_w, new_m, new_v = {}, {}, {}
        for n in TWIN_WEIGHTS:
            delta_w[n], new_m[n], new_v[n] = _adamw(weights[n], grad_w[n], given["m_" + n], given["v_" + n])
    return (loss, grad_x, *[grad_w[n] for n in TWIN_WEIGHTS], *[delta_w[n] for n in TWIN_WEIGHTS],
            *[new_m[n] for n in TWIN_WEIGHTS], *[new_v[n] for n in TWIN_WEIGHTS])
```

```python
import functools
import math

import jax
import jax.numpy as jnp
from jax import lax
from jax.experimental import pallas as pl
from jax.experimental.pallas import tpu as pltpu

F32, BF16 = jnp.float32, jnp.bfloat16

D_MODEL = 1024
N_META = 16
NORM_EPS = 1e-6
ATTN_HEADS, ATTN_KV_HEADS, ATTN_GROUPS, ATTN_HD = 16, 2, 8, 64
ATTN_BLOCK = 128
FRONT_PAD = ATTN_BLOCK - N_META
DN_HD, DN_K_HEADS, DN_V_HEADS = 128, 8, 16
DN_CHUNK = 64
DN_KEY_W, DN_VAL_W = 1024, 2048
DN_CONV_W = 2 * DN_KEY_W + DN_VAL_W
DN_CONV_K = 4
N_DEV = 8
ROW_BLOCK = 384
VMEM_LIMIT = 56 * 1024 * 1024
NEG = -1e30

ADAM_LR, ADAM_B1, ADAM_B2, ADAM_EPS, ADAM_WD, ADAM_STEP = 0.001, 0.9, 0.999, 1e-08, 0.01, 10

NT = (((1,), (1,)), ((), ()))
TN = (((0,), (0,)), ((), ()))


def _cparams(sem=("arbitrary",)):
    return pltpu.CompilerParams(dimension_semantics=sem, vmem_limit_bytes=VMEM_LIMIT)


def _rms(x, w):
    return x * lax.rsqrt(jnp.mean(x * x, axis=-1, keepdims=True) + NORM_EPS) * w


def _silu(x):
    return x * jax.nn.sigmoid(x)


def _softplus(x):
    return jnp.maximum(x, 0.0) + jnp.log(1.0 + jnp.exp(-jnp.abs(x)))


def _bdot(a, b, dims=None):
    a, b = a.astype(BF16), b.astype(BF16)
    if dims is None:
        return jnp.dot(a, b, preferred_element_type=F32)
    return lax.dot_general(a, b, dims, preferred_element_type=F32)


def _hdot(a, b):
    return jnp.dot(a, b, preferred_element_type=F32, precision=lax.Precision.HIGHEST)


def _row_call(name, body, n_rows, rb, rows, consts, outs, accs=(), reverse=False, scratch=(), halos=()):
    n = n_rows // rb
    assert n * rb == n_rows
    idx = (lambda i: (n - 1 - i, 0)) if reverse else (lambda i: (i, 0))
    in_specs = [pl.BlockSpec((rb, a.shape[1]), idx) for a in rows]
    in_specs += [pl.BlockSpec((hr, a.shape[1]), fn) for a, hr, fn in halos]
    in_specs += [pl.BlockSpec(c.shape, functools.partial(lambda i, nd: (0,) * nd, nd=c.ndim)) for c in consts]
    out_specs = [pl.BlockSpec((rb, c), idx) for c, _ in outs]
    out_specs += [pl.BlockSpec(s, functools.partial(lambda i, nd: (0,) * nd, nd=len(s))) for s, _ in accs]
    out_shape = [jax.ShapeDtypeStruct((n_rows, c), dt) for c, dt in outs]
    out_shape += [jax.ShapeDtypeStruct(s, dt) for s, dt in accs]
    return pl.pallas_call(
        body, grid=(n,), in_specs=in_specs, out_specs=out_specs, out_shape=out_shape,
        scratch_shapes=list(scratch), name=name, compiler_params=_cparams(),
    )(*rows, *[a for a, _, _ in halos], *consts)


def _attn_in_fwd(h0, norm_w, w_in):
    T = h0.shape[0]

    def body(h_ref, nw_ref, w_ref, xn_ref, q_ref, kv_ref, gate_ref):
        xn = _rms(h_ref[...], nw_ref[...]).astype(BF16)
        xn_ref[...] = xn
        q_ref[...] = jnp.dot(xn, w_ref[:, 0:1024], preferred_element_type=F32)
        kv_ref[...] = jnp.dot(xn, w_ref[:, 1024:1280], preferred_element_type=F32)
        gate_ref[...] = jnp.dot(xn, w_ref[:, 1280:2304], preferred_element_type=F32)

    return _row_call("attn_in_fwd", body, T, ROW_BLOCK, [h0], [norm_w, w_in],
                     [(1024, BF16), (1024, F32), (256, F32), (1024, F32)])


def _attn_mask_bias(n, j):
    R, C = ATTN_GROUPS * ATTN_BLOCK, 2 * ATTN_BLOCK + N_META
    r = lax.broadcasted_iota(jnp.int32, (R, C), 0)
    c = lax.broadcasted_iota(jnp.int32, (R, C), 1)
    ql = r & (ATTN_BLOCK - 1)
    is_meta = c >= 2 * ATTN_BLOCK
    dist_band = ATTN_BLOCK + ql - c
    cmin = jnp.maximum(0, 2 * ATTN_BLOCK - ATTN_BLOCK * n)
    valid_band = (c >= cmin) & (dist_band >= 0) & (dist_band < ATTN_BLOCK)
    dist_meta = ATTN_BLOCK * n + ql - FRONT_PAD - (c - 2 * ATTN_BLOCK)
    valid = (is_meta & (dist_meta >= 0)) | (jnp.logical_not(is_meta) & valid_band)
    dist = jnp.minimum(jnp.where(is_meta, dist_meta, dist_band), ATTN_BLOCK).astype(F32)
    rr = lax.broadcasted_iota(jnp.int32, (R, 1), 0)
    head = (rr >> 7).astype(F32) + float(ATTN_GROUPS * j + 1)
    slope = jnp.exp(head * (-0.5 * math.log(2.0)))
    return valid, slope * dist


def _attn_group(q, k, v, sinkcol, qnw, knw, valid, bias):
    qn = _rms(q, qnw)
    kn = _rms(k, knw)
    s = _bdot(qn, kn, NT) * (ATTN_HD ** -0.5)
    s = jnp.where(valid, s - bias, NEG)
    m = lax.stop_gradient(jnp.maximum(jnp.max(s, axis=1, keepdims=True), sinkcol))
    e = jnp.exp(s - m)
    denom = jnp.sum(e, axis=1, keepdims=True) + jnp.exp(sinkcol - m)
    p = e / denom
    return _bdot(p, v)


def _sink_col(sinks_ref, j):
    rr = lax.broadcasted_iota(jnp.int32, (ATTN_GROUPS * ATTN_BLOCK, 1), 0) >> 7
    col = jnp.zeros((ATTN_GROUPS * ATTN_BLOCK, 1), F32)
    for hl in range(ATTN_GROUPS):
        col = jnp.where(rr == hl, sinks_ref[0, ATTN_GROUPS * j + hl], col)
    return col


def _attn_kv_tiles(kvp_ref, kvc_ref, kvm_ref, j):
    ksl = slice(ATTN_HD * j, ATTN_HD * (j + 1))
    vsl = slice(128 + ATTN_HD * j, 128 + ATTN_HD * (j + 1))
    k = jnp.concatenate([kvp_ref[:, ksl], kvc_ref[:, ksl], kvm_ref[FRONT_PAD:, ksl]], axis=0)
    v = jnp.concatenate([kvp_ref[:, vsl], kvc_ref[:, vsl], kvm_ref[FRONT_PAD:, vsl]], axis=0)
    return k, v


def _attn_core_fwd(q, kv, sinks, qnw, knw):
    T = q.shape[0]
    nb = T // ATTN_BLOCK

    def body(sinks_ref, q_ref, kvc_ref, kvp_ref, kvm_ref, qnw_ref, knw_ref, o_ref):
        n = pl.program_id(0)
        for j in range(ATTN_KV_HEADS):
            valid, bias = _attn_mask_bias(n, j)
            qs = jnp.concatenate(
                [q_ref[:, ATTN_HD * h:ATTN_HD * (h + 1)] for h in range(ATTN_GROUPS * j, ATTN_GROUPS * (j + 1))], axis=0)
            k, v = _attn_kv_tiles(kvp_ref, kvc_ref, kvm_ref, j)
            o = _attn_group(qs, k, v, _sink_col(sinks_ref, j), qnw_ref[...], knw_ref[...], valid, bias)
            for hl in range(ATTN_GROUPS):
                h = ATTN_GROUPS * j + hl
                o_ref[:, ATTN_HD * h:ATTN_HD * (h + 1)] = o[ATTN_BLOCK * hl:ATTN_BLOCK * (hl + 1), :]

    return pl.pallas_call(
        body, grid=(nb,),
        in_specs=[pl.BlockSpec(memory_space=pltpu.SMEM),
                  pl.BlockSpec((ATTN_BLOCK, 1024), lambda i: (i, 0)),
                  pl.BlockSpec((ATTN_BLOCK, 256), lambda i: (i, 0)),
                  pl.BlockSpec((ATTN_BLOCK, 256), lambda i: (jnp.maximum(i - 1, 0), 0)),
                  pl.BlockSpec((ATTN_BLOCK, 256), lambda i: (0, 0)),
                  pl.BlockSpec((1, ATTN_HD), lambda i: (0, 0)),
                  pl.BlockSpec((1, ATTN_HD), lambda i: (0, 0))],
        out_specs=pl.BlockSpec((ATTN_BLOCK, 1024), lambda i: (i, 0)),
        out_shape=jax.ShapeDtypeStruct((T, 1024), F32),
        name="attn_core_fwd", compiler_params=_cparams(),
    )(sinks, q, kv, kv, kv, qnw, knw)


def _attn_out_fwd(o, gate, h0, w_out):
    T = o.shape[0]

    def body(o_ref, g_ref, h_ref, w_ref, h1_ref):
        og = o_ref[...] * _silu(g_ref[...])
        h1_ref[...] = h_ref[...] + _bdot(og, w_ref[...])

    return _row_call("attn_out_fwd", body, T, ROW_BLOCK, [o, gate, h0], [w_out], [(1024, F32)])[0]


def _wgrad(xn, du, cg, name):
    T, kdim = xn.shape
    cdim = du.shape[1]
    nr, nc = T // ROW_BLOCK, cdim // cg
    assert nc * cg == cdim

    def body(x_ref, du_ref, dw_ref):
        @pl.when(pl.program_id(1) == 0)
        def _():
            dw_ref[...] = jnp.zeros_like(dw_ref)
        dw_ref[...] += _bdot(x_ref[...], du_ref[...], TN)

    return pl.pallas_call(
        body, grid=(nc, nr),
        in_specs=[pl.BlockSpec((ROW_BLOCK, kdim), lambda j, i: (i, 0)),
                  pl.BlockSpec((ROW_BLOCK, cg), lambda j, i: (i, j))],
        out_specs=pl.BlockSpec((kdim, cg), lambda j, i: (0, j)),
        out_shape=jax.ShapeDtypeStruct((kdim, cdim), F32),
        name=name, compiler_params=_cparams(("arbitrary", "arbitrary")),
    )(xn, du)


def _attn_out_bwd(dh1, o, gate, w_out):
    T = o.shape[0]

    def body(dh_ref, o_ref, g_ref, w_ref, do_ref, dg_ref, dw_ref):
        @pl.when(pl.program_id(0) == 0)
        def _():
            dw_ref[...] = jnp.zeros_like(dw_ref)
        dh = dh_ref[...]
        dog = _bdot(dh, w_ref[...], NT)
        og, vjp = jax.vjp(lambda o_, g_: o_ * _silu(g_), o_ref[...], g_ref[...])
        do, dg = vjp(dog)
        do_ref[...] = do
        dg_ref[...] = dg
        dw_ref[...] += _bdot(og, dh, TN)

    return _row_call("attn_out_bwd", body, T, ROW_BLOCK, [dh1, o, gate], [w_out],
                     [(1024, F32), (1024, F32)], [((1024, 1024), F32)])


def _attn_core_bwd(do, q, kv, sinks, qnw, knw):
    T = q.shape[0]
    nb = T // ATTN_BLOCK
    rev = lambda i: nb - 1 - i

    def body(sinks_ref, do_ref, q_ref, kvc_ref, kvp_ref, kvm_ref, qnw_ref, knw_ref,
             dq_ref, dkv_ref, dsinks_ref, dqnw_ref, dknw_ref, carry_ref, meta_ref):
        step = pl.program_id(0)
        n = rev(step)

        @pl.when(step == 0)
        def _():
            carry_ref[...] = jnp.zeros_like(carry_ref)
            meta_ref[...] = jnp.zeros_like(meta_ref)
            dsinks_ref[...] = jnp.zeros_like(dsinks_ref)
            dqnw_ref[...] = jnp.zeros_like(dqnw_ref)
            dknw_ref[...] = jnp.zeros_like(dknw_ref)

        lane16 = lax.broadcasted_iota(jnp.int32, (1, ATTN_HEADS), 1)
        dsinks = jnp.zeros((1, ATTN_HEADS), F32)
        for j in range(ATTN_KV_HEADS):
            valid, bias = _attn_mask_bias(n, j)
            heads = range(ATTN_GROUPS * j, ATTN_GROUPS * (j + 1))
            qs = jnp.concatenate([q_ref[:, ATTN_HD * h:ATTN_HD * (h + 1)] for h in heads], axis=0)
            dos = jnp.concatenate([do_ref[:, ATTN_HD * h:ATTN_HD * (h + 1)] for h in heads], axis=0)
            k, v = _attn_kv_tiles(kvp_ref, kvc_ref, kvm_ref, j)
            fn = functools.partial(_attn_group, valid=valid, bias=bias)
            _, vjp = jax.vjp(fn, qs, k, v, _sink_col(sinks_ref, j), qnw_ref[...], knw_ref[...])
            dqs, dk, dv, dsc, dqn, dkn = vjp(dos)
            dqnw_ref[...] += dqn
            dknw_ref[...] += dkn
            for hl in range(ATTN_GROUPS):
                h = ATTN_GROUPS * j + hl
                rows = slice(ATTN_BLOCK * hl, ATTN_BLOCK * (hl + 1))
                dq_ref[:, ATTN_HD * h:ATTN_HD * (h + 1)] = dqs[rows, :]
                dsinks = dsinks + jnp.where(lane16 == h, jnp.sum(dsc[rows, :]), 0.0)
            ksl = slice(ATTN_HD * j, ATTN_HD * (j + 1))
            vsl = slice(128 + ATTN_HD * j, 128 + ATTN_HD * (j + 1))
            for sl, d in ((ksl, dk), (vsl, dv)):
                dkv_ref[:, sl] = d[ATTN_BLOCK:2 * ATTN_BLOCK, :] + carry_ref[:, sl]
                carry_ref[:, sl] = d[0:ATTN_BLOCK, :]
                meta_ref[:, sl] += d[2 * ATTN_BLOCK:, :]
        dsinks_ref[...] += dsinks

        @pl.when(n == 0)
        def _():
            dkv_ref[FRONT_PAD:, :] += meta_ref[...]

    return pl.pallas_call(
        body, grid=(nb,),
        in_specs=[pl.BlockSpec(memory_space=pltpu.SMEM),
                  pl.BlockSpec((ATTN_BLOCK, 1024), lambda i: (rev(i), 0)),
                  pl.BlockSpec((ATTN_BLOCK, 1024), lambda i: (rev(i), 0)),
                  pl.BlockSpec((ATTN_BLOCK, 256), lambda i: (rev(i), 0)),
                  pl.BlockSpec((ATTN_BLOCK, 256), lambda i: (jnp.maximum(rev(i) - 1, 0), 0)),
                  pl.BlockSpec((ATTN_BLOCK, 256), lambda i: (0, 0)),
                  pl.BlockSpec((1, ATTN_HD), lambda i: (0, 0)),
                  pl.BlockSpec((1, ATTN_HD), lambda i: (0, 0))],
        out_specs=[pl.BlockSpec((ATTN_BLOCK, 1024), lambda i: (rev(i), 0)),
                   pl.BlockSpec((ATTN_BLOCK, 256), lambda i: (rev(i), 0)),
                   pl.BlockSpec((1, ATTN_HEADS), lambda i: (0, 0)),
                   pl.BlockSpec((1, ATTN_HD), lambda i: (0, 0)),
                   pl.BlockSpec((1, ATTN_HD), lambda i: (0, 0))],
        out_shape=[jax.ShapeDtypeStruct((T, 1024), F32), jax.ShapeDtypeStruct((T, 256), F32),
                   jax.ShapeDtypeStruct((1, ATTN_HEADS), F32), jax.ShapeDtypeStruct((1, ATTN_HD), F32),
                   jax.ShapeDtypeStruct((1, ATTN_HD), F32)],
        scratch_shapes=[pltpu.VMEM((ATTN_BLOCK, 256), F32), pltpu.VMEM((N_META, 256), F32)],
        name="attn_core_bwd", compiler_params=_cparams(),
    )(sinks, do, q, kv, kv, kv, qnw, knw)


def _attn_in_bwd(dq, dkv, dgate, h0, dh1, norm_w, w_in):
    T = h0.shape[0]

    def body(dq_ref, dkv_ref, dg_ref, h_ref, dh1_ref, nw_ref, w_ref, dh0_ref, dnw_ref):
        @pl.when(pl.program_id(0) == 0)
        def _():
            dnw_ref[...] = jnp.zeros_like(dnw_ref)
        dxn = (_bdot(dq_ref[...], w_ref[:, 0:1024], NT) + _bdot(dkv_ref[...], w_ref[:, 1024:1280], NT)
               + _bdot(dg_ref[...], w_ref[:, 1280:2304], NT))
        _, vjp = jax.vjp(_rms, h_ref[...], nw_ref[...])
        dh, dnw = vjp(dxn)
        dh0_ref[...] = dh1_ref[...] + dh
        dnw_ref[...] += dnw

    return _row_call("attn_in_bwd", body, T, ROW_BLOCK, [dq, dkv, dgate, h0, dh1], [norm_w, w_in],
                     [(1024, F32)], [((1, 1024), F32)])


def _dn_in_fwd(h1, norm_w, w_in):
    T = h1.shape[0]

    def body(h_ref, nw_ref, w_ref, xn_ref, qkv_ref, z_ref, ba_ref):
        xn = _rms(h_ref[...], nw_ref[...]).astype(BF16)
        xn_ref[...] = xn
        qkv_ref[...] = jnp.dot(xn, w_ref[:, 0:4096], preferred_element_type=F32)
        z_ref[...] = jnp.dot(xn, w_ref[:, 4096:6144], preferred_element_type=F32)
        ba_ref[...] = jnp.dot(xn, w_ref[:, 6144:6176], preferred_element_type=F32)

    return _row_call("dn_in_fwd", body, T, ROW_BLOCK, [h1], [norm_w, w_in],
                     [(1024, BF16), (4096, F32), (2048, F32), (32, F32)])


def _shift_down(cur, prev8, s):
    i8 = lax.broadcasted_iota(jnp.int32, (8, cur.shape[1]), 0)
    r = pltpu.roll(cur, s, 0)
    head = jnp.where(i8 < s, pltpu.roll(prev8, s, 0), r[0:8])
    return jnp.concatenate([head, r[8:]], axis=0)


def _shift_up(cur, next8, s):
    n = cur.shape[0]
    i8 = lax.broadcasted_iota(jnp.int32, (8, cur.shape[1]), 0)
    r = pltpu.roll(cur, n - s, 0)
    tail = jnp.where(i8 >= 8 - s, pltpu.roll(next8, 8 - s, 0), r[n - 8:])
    return jnp.concatenate([r[:n - 8], tail], axis=0)


def _conv_tile(cur, prev8, w):
    out = w[3:4, :] * cur
    for s in range(1, DN_CONV_K):
        out = out + w[3 - s:4 - s, :] * _shift_down(cur, prev8, s)
    return out


def _l2n(a, scale):
    return a * (lax.rsqrt(jnp.sum(a * a, axis=-1, keepdims=True) + NORM_EPS) * scale)


def _dn_post_tile(c, t):
    a = _silu(c)
    if t < DN_K_HEADS:
        return _l2n(a, DN_HD ** -0.5)
    if t < 2 * DN_K_HEADS:
        return _l2n(a, 1.0)
    return a


def _dn_beta_g(ba, a_log, dt_bias, live):
    beta = jax.nn.sigmoid(ba[:, 0:DN_V_HEADS]) * live
    g = -jnp.exp(a_log) * _softplus(ba[:, DN_V_HEADS:] + dt_bias) * live
    return beta, g


def _live_rows(i, rb):
    rows = i * rb + lax.broadcasted_iota(jnp.int32, (rb, 1), 0)
    return (rows >= FRONT_PAD).astype(F32)


def _halo_spec_args(x, rb):
    per = rb // 8
    return (x, 8, lambda i: (jnp.maximum(i * per - 1, 0), 0))


def _dn_conv_fwd(qkv, ba, conv_w, a_log, dt_bias):
    T = qkv.shape[0]

    def body(x_ref, ba_ref, halo_ref, cw_ref, al_ref, dtb_ref, q_ref, k_ref, v_ref, bg_ref):
        i = pl.program_id(0)
        first = (i > 0).astype(F32)
        for t in range(DN_CONV_W // 128):
            cols = slice(128 * t, 128 * (t + 1))
            c = _conv_tile(x_ref[:, cols], halo_ref[:, cols] * first, cw_ref[:, cols])
            out = _dn_post_tile(c, t)
            if t < DN_K_HEADS:
                q_ref[:, cols] = out
            elif t < 2 * DN_K_HEADS:
                k_ref[:, 128 * (t - 8):128 * (t - 7)] = out
            else:
                v_ref[:, 128 * (t - 16):128 * (t - 15)] = out
        beta, g = _dn_beta_g(ba_ref[...], al_ref[...], dtb_ref[...], _live_rows(i, ROW_BLOCK))
        bg_ref[:, 0:DN_V_HEADS] = beta
        bg_ref[:, DN_V_HEADS:] = g

    return _row_call("dn_conv_fwd", body, T, ROW_BLOCK, [qkv, ba], [conv_w, a_log, dt_bias],
                     [(1024, F32), (1024, F32), (2048, F32), (32, F32)], halos=[_halo_spec_args(qkv, ROW_BLOCK)])


def _chunk_masks():
    r = lax.broadcasted_iota(jnp.int32, (DN_CHUNK, DN_CHUNK), 0)
    c = lax.broadcasted_iota(jnp.int32, (DN_CHUNK, DN_CHUNK), 1)
    return r >= c, r > c, r == c, r <= c


def _dn_head_step(S, q, k, v, beta, g, masks):
    causal, strict, eye, upper = masks
    g_row = jnp.sum(jnp.where(eye, g, 0.0), axis=0, keepdims=True)
    gc_col = jnp.sum(jnp.where(causal, g_row, 0.0), axis=1, keepdims=True)
    gc_row = jnp.sum(jnp.where(upper, g, 0.0), axis=0, keepdims=True)
    g_last = jnp.sum(g, axis=0, keepdims=True)
    decay = jnp.exp(jnp.where(causal, gc_col - gc_row, NEG))
    eg = jnp.exp(gc_col)
    kb = k * beta
    x = jnp.where(strict, _bdot(kb, k, NT) * decay, 0.0) * -1.0
    ainv = jnp.where(eye, 1.0, 0.0) + x
    p = x
    for _ in range(5):
        p = _hdot(p, p)
        ainv = ainv + _hdot(ainv, p)
    u = _hdot(ainv, v * beta)
    w = _hdot(ainv, kb * eg)
    v_new = u - _bdot(w, S)
    attn = _bdot(q, k, NT) * decay
    o = _bdot(q * eg, S) + _bdot(attn, v_new)
    k_state = k * jnp.exp(g_last - gc_col)
    s_new = S * jnp.exp(g_last) + _bdot(k_state, v_new, TN)
    return s_new, o


def _dn_scan_fwd(qn, kn, v, bg):
    T = qn.shape[0]
    nc = T // DN_CHUNK

    def body(q_ref, k_ref, v_ref, bg_ref, o_ref, ssave_ref, s_ref):
        @pl.when(pl.program_id(0) == 0)
        def _():
            s_ref[...] = jnp.zeros_like(s_ref)
        masks = _chunk_masks()
        for h in range(DN_V_HEADS):
            kc = slice(128 * (h // 2), 128 * (h // 2 + 1))
            vc = slice(128 * h, 128 * (h + 1))
            s_old = s_ref[h]
            ssave_ref[0, h] = s_old
            s_new, o = _dn_head_step(s_old, q_ref[:, kc], k_ref[:, kc], v_ref[:, vc],
                                     bg_ref[:, h:h + 1], bg_ref[:, DN_V_HEADS + h:DN_V_HEADS + h + 1], masks)
            o_ref[:, vc] = o
            s_ref[h] = s_new

    return pl.pallas_call(
        body, grid=(nc,),
        in_specs=[pl.BlockSpec((DN_CHUNK, 1024), lambda i: (i, 0)),
                  pl.BlockSpec((DN_CHUNK, 1024), lambda i: (i, 0)),
                  pl.BlockSpec((DN_CHUNK, 2048), lambda i: (i, 0)),
                  pl.BlockSpec((DN_CHUNK, 32), lambda i: (i, 0))],
        out_specs=[pl.BlockSpec((DN_CHUNK, 2048), lambda i: (i, 0)),
                   pl.BlockSpec((1, DN_V_HEADS, DN_HD, DN_HD), lambda i: (i, 0, 0, 0))],
        out_shape=[jax.ShapeDtypeStruct((T, 2048), F32),
                   jax.ShapeDtypeStruct((nc, DN_V_HEADS, DN_HD, DN_HD), F32)],
        scratch_shapes=[pltpu.VMEM((DN_V_HEADS, DN_HD, DN_HD), F32)],
        name="dn_scan_fwd", compiler_params=_cparams(),
    )(qn, kn, v, bg)


def _dn_gate_tile(o, z, onw):
    return _rms(o, onw) * _silu(z)


def _dn_out_fwd(o, z, h1, target, w_out, onw):
    T = o.shape[0]

    def body(o_ref, z_ref, h_ref, t_ref, w_ref, onw_ref, dy_ref, og_ref, loss_ref):
        i = pl.program_id(0)

        @pl.when(i == 0)
        def _():
            loss_ref[...] = jnp.zeros_like(loss_ref)
        for h in range(DN_V_HEADS):
            cols = slice(128 * h, 128 * (h + 1))
            og_ref[:, cols] = _dn_gate_tile(o_ref[:, cols], z_ref[:, cols], onw_ref[...]).astype(BF16)
        y = h_ref[...] + jnp.dot(og_ref[...], w_ref[...], preferred_element_type=F32)
        rows = i * ROW_BLOCK + lax.broadcasted_iota(jnp.int32, (ROW_BLOCK, 1), 0)
        diff = jnp.where(rows >= FRONT_PAD + N_META, y - t_ref[...], 0.0)
        dy_ref[...] = diff * (1.0 / D_MODEL)
        loss_ref[...] += jnp.sum(diff * diff) * (0.5 / D_MODEL)

    return _row_call("dn_out_fwd", body, T, ROW_BLOCK, [o, z, h1, target], [w_out, onw],
                     [(1024, F32), (2048, BF16)], [((1, 128), F32)])


def _dn_out_bwd(dy, o, z, w_out, onw):
    T = o.shape[0]

    def body(dy_ref, o_ref, z_ref, w_ref, onw_ref, do_ref, dz_ref, donw_ref):
        @pl.when(pl.program_id(0) == 0)
        def _():
            donw_ref[...] = jnp.zeros_like(donw_ref)
        dy = dy_ref[...].astype(BF16)
        donw = jnp.zeros((1, DN_HD), F32)
        for h in range(DN_V_HEADS):
            cols = slice(128 * h, 128 * (h + 1))
            dog = lax.dot_general(dy, w_ref[cols, :], NT, preferred_element_type=F32)
            _, vjp = jax.vjp(_dn_gate_tile, o_ref[:, cols], z_ref[:, cols], onw_ref[...])
            do, dz, dn = vjp(dog)
            do_ref[:, cols] = do
            dz_ref[:, cols] = dz
            donw = donw + dn
        donw_ref[...] += donw

    return _row_call("dn_out_bwd", body, T, ROW_BLOCK, [dy, o, z], [w_out, onw],
                     [(2048, F32), (2048, F32)], [((1, DN_HD), F32)])


def _dn_scan_bwd(do, qn, kn, v, bg, ssave):
    T = qn.shape[0]
    nc = T // DN_CHUNK
    rev = lambda i: nc - 1 - i

    def body(do_ref, q_ref, k_ref, v_ref, bg_ref, ss_ref, dq_ref, dk_ref, dv_ref, dbg_ref, ds_ref):
        @pl.when(pl.program_id(0) == 0)
        def _():
            ds_ref[...] = jnp.zeros_like(ds_ref)
        masks = _chunk_masks()
        lane32 = lax.broadcasted_iota(jnp.int32, (1, 2 * DN_V_HEADS), 1)
        dbg = jnp.zeros((DN_CHUNK, 2 * DN_V_HEADS), F32)
        for h in range(DN_V_HEADS):
            kc = slice(128 * (h // 2), 128 * (h // 2 + 1))
            vc = slice(128 * h, 128 * (h + 1))
            fn = functools.partial(_dn_head_step, masks=masks)
            _, vjp = jax.vjp(fn, ss_ref[0, h], q_ref[:, kc], k_ref[:, kc], v_ref[:, vc],
                             bg_ref[:, h:h + 1], bg_ref[:, DN_V_HEADS + h:DN_V_HEADS + h + 1])
            ds, dq, dk, dv, dbeta, dg = vjp((ds_ref[h], do_ref[:, vc]))
            ds_ref[h] = ds
            dv_ref[:, vc] = dv
            if h % 2 == 0:
                dq_ref[:, kc] = dq
                dk_ref[:, kc] = dk
            else:
                dq_ref[:, kc] += dq
                dk_ref[:, kc] += dk
            dbg = dbg + jnp.where(lane32 == h, dbeta, 0.0) + jnp.where(lane32 == DN_V_HEADS + h, dg, 0.0)
        dbg_ref[...] = dbg

    return pl.pallas_call(
        body, grid=(nc,),
        in_specs=[pl.BlockSpec((DN_CHUNK, 2048), lambda i: (rev(i), 0)),
                  pl.BlockSpec((DN_CHUNK, 1024), lambda i: (rev(i), 0)),
                  pl.BlockSpec((DN_CHUNK, 1024), lambda i: (rev(i), 0)),
                  pl.BlockSpec((DN_CHUNK, 2048), lambda i: (rev(i), 0)),
                  pl.BlockSpec((DN_CHUNK, 32), lambda i: (rev(i), 0)),
                  pl.BlockSpec((1, DN_V_HEADS, DN_HD, DN_HD), lambda i: (rev(i), 0, 0, 0))],
        out_specs=[pl.BlockSpec((DN_CHUNK, 1024), lambda i: (rev(i), 0)),
                   pl.BlockSpec((DN_CHUNK, 1024), lambda i: (rev(i), 0)),
                   pl.BlockSpec((DN_CHUNK, 2048), lambda i: (rev(i), 0)),
                   pl.BlockSpec((DN_CHUNK, 32), lambda i: (rev(i), 0))],
        out_shape=[jax.ShapeDtypeStruct((T, 1024), F32), jax.ShapeDtypeStruct((T, 1024), F32),
                   jax.ShapeDtypeStruct((T, 2048), F32), jax.ShapeDtypeStruct((T, 32), F32)],
        scratch_shapes=[pltpu.VMEM((DN_V_HEADS, DN_HD, DN_HD), F32)],
        name="dn_scan_bwd", compiler_params=_cparams(),
    )(do, qn, kn, v, bg, ssave)


def _dn_conv_bwd(dqn, dkn, dv, dbg, qkv, ba, conv_w, a_log, dt_bias):
    T = qkv.shape[0]
    nr = T // ROW_BLOCK

    def body(dq_ref, dk_ref, dv_ref, dbg_ref, x_ref, ba_ref, halo_ref, cw_ref, al_ref, dtb_ref,
             dx_ref, dba_ref, dcw_ref, dal_ref, ddtb_ref, carry_ref):
        step = pl.program_id(0)
        i = nr - 1 - step

        @pl.when(step == 0)
        def _():
            carry_ref[...] = jnp.zeros_like(carry_ref)
            dcw_ref[...] = jnp.zeros_like(dcw_ref)
            dal_ref[...] = jnp.zeros_like(dal_ref)
            ddtb_ref[...] = jnp.zeros_like(ddtb_ref)
        first = (i > 0).astype(F32)
        for t in range(DN_CONV_W // 128):
            cols = slice(128 * t, 128 * (t + 1))
            cur, prev8, w = x_ref[:, cols], halo_ref[:, cols] * first, cw_ref[:, cols]
            c = _conv_tile(cur, prev8, w)
            if t < DN_K_HEADS:
                dout = dq_ref[:, cols]
            elif t < 2 * DN_K_HEADS:
                dout = dk_ref[:, 128 * (t - 8):128 * (t - 7)]
            else:
                dout = dv_ref[:, 128 * (t - 16):128 * (t - 15)]
            _, vjp = jax.vjp(functools.partial(_dn_post_tile, t=t), c)
            (dc,) = vjp(dout)
            nxt = carry_ref[:, cols]
            dx = w[3:4, :] * dc
            dcw_ref[3:4, cols] += jnp.sum(dc * cur, axis=0, keepdims=True)
            for s in range(1, DN_CONV_K):
                dx = dx + w[3 - s:4 - s, :] * _shift_up(dc, nxt, s)
                dcw_ref[3 - s:4 - s, cols] += jnp.sum(dc * _shift_down(cur, prev8, s), axis=0, keepdims=True)
            dx_ref[:, cols] = dx
            carry_ref[:, cols] = dc[0:8, :]
        fn = functools.partial(_dn_beta_g, live=_live_rows(i, ROW_BLOCK))
        _, vjp = jax.vjp(fn, ba_ref[...], al_ref[...], dtb_ref[...])
        dba, dal, ddtb = vjp((dbg_ref[:, 0:DN_V_HEADS], dbg_ref[:, DN_V_HEADS:]))
        dba_ref[...] = dba
        dal_ref[...] += dal
        ddtb_ref[...] += ddtb

    per = ROW_BLOCK // 8
    halo = (qkv, 8, lambda s: (jnp.maximum((nr - 1 - s) * per - 1, 0), 0))
    return _row_call("dn_conv_bwd", body, T, ROW_BLOCK, [dqn, dkn, dv, dbg, qkv, ba], [conv_w, a_log, dt_bias],
                     [(4096, F32), (32, F32)], [((DN_CONV_K, 4096), F32), ((1, DN_V_HEADS), F32), ((1, DN_V_HEADS), F32)],
                     reverse=True, scratch=[pltpu.VMEM((8, 4096), F32)], halos=[halo])


def _dn_in_bwd(dqkv, dz, dba, h1, dy, norm_w, w_in):
    T = h1.shape[0]

    def body(dqkv_ref, dz_ref, dba_ref, h_ref, dy_ref, nw_ref, w_ref, dh_ref, dnw_ref):
        @pl.when(pl.program_id(0) == 0)
        def _():
            dnw_ref[...] = jnp.zeros_like(dnw_ref)
        dxn = (_bdot(dqkv_ref[...], w_ref[:, 0:4096], NT) + _bdot(dz_ref[...], w_ref[:, 4096:6144], NT)
               + _bdot(dba_ref[...], w_ref[:, 6144:6176], NT))
        _, vjp = jax.vjp(_rms, h_ref[...], nw_ref[...])
        dh, dnw = vjp(dxn)
        dh_ref[...] = (dy_ref[...] + dh) * _live_rows(pl.program_id(0), ROW_BLOCK)
        dnw_ref[...] += dnw

    return _row_call("dn_in_bwd", body, T, ROW_BLOCK, [dqkv, dz, dba, h1, dy], [norm_w, w_in],
                     [(1024, F32)], [((1, 1024), F32)])


def _exchange(parts, scatter, name):
    n = len(parts)
    out_shape = [jax.ShapeDtypeStruct(p.shape if sc else (N_DEV,) + p.shape, p.dtype) for p, sc in zip(parts, scatter)]

    def body(*refs):
        ins, outs = refs[:n], refs[n:2 * n]
        send_sems, recv_sems, local_sems = refs[2 * n:]
        x, y, c = lax.axis_index("x"), lax.axis_index("y"), lax.axis_index("c")
        me = 4 * x + 2 * y + c
        peers = []
        for k in range(1, N_DEV):
            px = 1 - x if k & 4 else x
            py = 1 - y if k & 2 else y
            pc = 1 - c if k & 1 else c
            peers.append(((px, py, pc), 4 * px + 2 * py + pc))

        def src(a, idx):
            return ins[a].at[idx] if scatter[a] else ins[a]

        local = [pltpu.make_async_copy(src(a, me), outs[a].at[me], local_sems.at[a]) for a in range(n)]
        for cp in local:
            cp.start()
        for a in range(n):
            for k, (dev, idx) in enumerate(peers):
                pltpu.make_async_remote_copy(
                    src_ref=src(a, idx), dst_ref=outs[a].at[me], send_sem=send_sems.at[a, k], recv_sem=recv_sems.at[a, k],
                    device_id=dev, device_id_type=pl.DeviceIdType.MESH).start()
        for a in range(n):
            for k, (dev, idx) in enumerate(peers):
                pltpu.make_async_remote_copy(
                    src_ref=src(a, idx), dst_ref=outs[a].at[idx], send_sem=send_sems.at[a, k], recv_sem=recv_sems.at[a, k],
                    device_id=dev, device_id_type=pl.DeviceIdType.MESH).wait()
        for cp in local:
            cp.wait()

    hbm = pl.BlockSpec(memory_space=pltpu.HBM)
    return pl.pallas_call(
        body, out_shape=out_shape, in_specs=[hbm] * n, out_specs=[hbm] * n,
        scratch_shapes=[pltpu.SemaphoreType.DMA((n, N_DEV - 1)), pltpu.SemaphoreType.DMA((n, N_DEV - 1)),
                        pltpu.SemaphoreType.DMA((n,))],
        name=name,
    )(*parts)


def _adam_rows(rows):
    for rb in (128, 64, 40, 16, 8):
        if rows % rb == 0:
            return rb
    return rows


def _adamw(stack, w, m, v, name):
    R, C = w.shape
    rb = _adam_rows(R)

    def body(s_ref, w_ref, m_ref, v_ref, g_ref, d_ref, nm_ref, nv_ref):
        g = s_ref[0]
        for s in range(1, N_DEV):
            g = g + s_ref[s]
        nm = ADAM_B1 * m_ref[...] + (1.0 - ADAM_B1) * g
        nv = ADAM_B2 * v_ref[...] + (1.0 - ADAM_B2) * (g * g)
        m_hat = nm / (1.0 - ADAM_B1 ** ADAM_STEP)
        v_hat = nv / (1.0 - ADAM_B2 ** ADAM_STEP)
        g_ref[...] = g
        d_ref[...] = -ADAM_LR * (m_hat / (jnp.sqrt(v_hat) + ADAM_EPS) + ADAM_WD * w_ref[...])
        nm_ref[...] = nm
        nv_ref[...] = nv

    blk = pl.BlockSpec((rb, C), lambda i: (i, 0))
    return pl.pallas_call(
        body, grid=(R // rb,),
        in_specs=[pl.BlockSpec((N_DEV, rb, C), lambda i: (0, i, 0)), blk, blk, blk],
        out_specs=[blk] * 4, out_shape=[jax.ShapeDtypeStruct((R, C), F32)] * 4,
        name=name, compiler_params=_cparams(),
    )(stack, w, m, v)


def _pad_rows8(a):
    return jnp.concatenate([a, jnp.zeros((8 - a.shape[0], a.shape[1]), a.dtype)], axis=0) if a.shape[0] < 8 else a


def _pack_small(norm_w, qnw, knw, sinks, a_log, dt_bias, onw, extra):
    z = lambda n: jnp.zeros((1, n), F32)
    row = jnp.concatenate([norm_w, qnw, knw, sinks, a_log, dt_bias, z(80), onw, extra, z(512)], axis=1)
    return row.reshape(16, 128)


def _unpack_small(p):
    row = p.reshape(1, 2048)
    cut = lambda a, n: row[:, a:a + n]
    return (cut(0, 1024), cut(1024, 64), cut(1088, 64), cut(1152, 16), cut(1168, 16), cut(1184, 16), cut(1280, 128),
            cut(1408, 128))


def _pack_rows(w_in_a, w_in_d, w_out_a, w_out_d, meta, conv, dn_norm):
    a = jnp.concatenate([w_in_a, w_in_d], axis=1)
    b = jnp.concatenate([w_out_a, w_out_d], axis=0)
    c = jnp.concatenate([meta, conv.reshape(16, 128), _pad_rows8(dn_norm)], axis=0)
    return a, b, c


def _unpack_rows(a, b, c):
    return (a[:, :288], a[:, 288:], b[:128], b[128:], c[:16], c[16:32].reshape(4, 512), c[32:33])


def _local_step(h0, target, w):
    xn0, q, kv, gate = _attn_in_fwd(h0, w["attn_norm_w"], w["attn_w_in"])
    o = _attn_core_fwd(q, kv, w["attn_sinks"], w["attn_q_norm_w"], w["attn_k_norm_w"])
    h1 = _attn_out_fwd(o, gate, h0, w["attn_w_out"])
    xn1, qkv, z, ba = _dn_in_fwd(h1, w["dn_norm_w"], w["dn_w_in"])
    qn, kn, v, bg = _dn_conv_fwd(qkv, ba, w["dn_conv_w"], w["dn_a_log"], w["dn_dt_bias"])
    o_dn, ssave = _dn_scan_fwd(qn, kn, v, bg)
    dy, og_dn, loss = _dn_out_fwd(o_dn, z, h1, target, w["dn_w_out"], w["dn_o_norm_w"])

    g = {}
    do_dn, dz, g["dn_o_norm_w"] = _dn_out_bwd(dy, o_dn, z, w["dn_w_out"], w["dn_o_norm_w"])
    g["dn_w_out"] = _wgrad(og_dn, dy, 1024, "wgrad_dn_out")
    dqn, dkn, dv, dbg = _dn_scan_bwd(do_dn, qn, kn, v, bg, ssave)
    dqkv, dba, g["dn_conv_w"], g["dn_a_log"], g["dn_dt_bias"] = _dn_conv_bwd(
        dqn, dkn, dv, dbg, qkv, ba, w["dn_conv_w"], w["dn_a_log"], w["dn_dt_bias"])
    dh1, g["dn_norm_w"] = _dn_in_bwd(dqkv, dz, dba, h1, dy, w["dn_norm_w"], w["dn_w_in"])
    g["dn_w_in"] = jnp.concatenate([_wgrad(xn1, dqkv, 1024, "wgrad_dn_qkv"), _wgrad(xn1, dz, 1024, "wgrad_dn_z"),
                                    _wgrad(xn1, dba, 32, "wgrad_dn_ba")], axis=1)
    do, dgate, g["attn_w_out"] = _attn_out_bwd(dh1, o, gate, w["attn_w_out"])
    dq, dkv, g["attn_sinks"], g["attn_q_norm_w"], g["attn_k_norm_w"] = _attn_core_bwd(
        do, q, kv, w["attn_sinks"], w["attn_q_norm_w"], w["attn_k_norm_w"])
    dh0, g["attn_norm_w"] = _attn_in_bwd(dq, dkv, dgate, h0, dh1, w["attn_norm_w"], w["attn_w_in"])
    g["attn_w_in"] = jnp.concatenate([_wgrad(xn0, dq, 1024, "wgrad_attn_q"), _wgrad(xn0, dkv, 256, "wgrad_attn_kv"),
                                      _wgrad(xn0, dgate, 1024, "wgrad_attn_gate")], axis=1)
    return loss, dh0, g


WEIGHTS = ['meta_tokens', 'attn_norm_w', 'attn_w_in', 'attn_q_norm_w', 'attn_k_norm_w', 'attn_sinks', 'attn_w_out',
           'dn_norm_w', 'dn_w_in', 'dn_conv_w', 'dn_a_log', 'dn_dt_bias', 'dn_o_norm_w', 'dn_w_out']
SMALL = ['attn_norm_w', 'attn_q_norm_w', 'attn_k_norm_w', 'attn_sinks', 'dn_a_log', 'dn_dt_bias', 'dn_o_norm_w']


def kernel(x, meta_tokens, attn_norm_w, attn_w_in, attn_q_norm_w, attn_k_norm_w, attn_sinks, attn_w_out, dn_norm_w, dn_w_in, dn_conv_w, dn_a_log, dn_dt_bias, dn_o_norm_w, dn_w_out, loss_target, m_meta_tokens, m_attn_norm_w, m_attn_w_in, m_attn_q_norm_w, m_attn_k_norm_w, m_attn_sinks, m_attn_w_out, m_dn_norm_w, m_dn_w_in, m_dn_conv_w, m_dn_a_log, m_dn_dt_bias, m_dn_o_norm_w, m_dn_w_out, v_meta_tokens, v_attn_norm_w, v_attn_w_in, v_attn_q_norm_w, v_attn_k_norm_w, v_attn_sinks, v_attn_w_out, v_dn_norm_w, v_dn_w_in, v_dn_conv_w, v_dn_a_log, v_dn_dt_bias, v_dn_o_norm_w, v_dn_w_out):
    shard = dict(meta_tokens=meta_tokens, attn_norm_w=attn_norm_w, attn_w_in=attn_w_in[0], attn_q_norm_w=attn_q_norm_w,
                 attn_k_norm_w=attn_k_norm_w, attn_sinks=attn_sinks, attn_w_out=attn_w_out[0], dn_norm_w=dn_norm_w,
                 dn_w_in=dn_w_in[0], dn_conv_w=dn_conv_w[0], dn_a_log=dn_a_log, dn_dt_bias=dn_dt_bias,
                 dn_o_norm_w=dn_o_norm_w, dn_w_out=dn_w_out[0])
    mom_m = dict(meta_tokens=m_meta_tokens, attn_norm_w=m_attn_norm_w, attn_w_in=m_attn_w_in[0], attn_q_norm_w=m_attn_q_norm_w,
                 attn_k_norm_w=m_attn_k_norm_w, attn_sinks=m_attn_sinks, attn_w_out=m_attn_w_out[0], dn_norm_w=m_dn_norm_w,
                 dn_w_in=m_dn_w_in[0], dn_conv_w=m_dn_conv_w[0], dn_a_log=m_dn_a_log, dn_dt_bias=m_dn_dt_bias,
                 dn_o_norm_w=m_dn_o_norm_w, dn_w_out=m_dn_w_out[0])
    mom_v = dict(meta_tokens=v_meta_tokens, attn_norm_w=v_attn_norm_w, attn_w_in=v_attn_w_in[0], attn_q_norm_w=v_attn_q_norm_w,
                 attn_k_norm_w=v_attn_k_norm_w, attn_sinks=v_attn_sinks, attn_w_out=v_attn_w_out[0], dn_norm_w=v_dn_norm_w,
                 dn_w_in=v_dn_w_in[0], dn_conv_w=v_dn_conv_w[0], dn_a_log=v_dn_a_log, dn_dt_bias=v_dn_dt_bias,
                 dn_o_norm_w=v_dn_o_norm_w, dn_w_out=v_dn_w_out[0])

    def rows_of(d):
        return _pack_rows(d["attn_w_in"], d["dn_w_in"], d["attn_w_out"], d["dn_w_out"], d["meta_tokens"], d["dn_conv_w"],
                          d["dn_norm_w"])

    def small_of(d, extra):
        return _pack_small(*[d[k] for k in SMALL], extra)

    wa, wb, wc = rows_of(shard)
    ga, gb, gc = _exchange([wa.astype(BF16), wb.astype(BF16), wc], [False, False, False], "gather_weights")
    full = {k: shard[k] for k in SMALL}
    full["attn_w_in"] = ga[:, :, :288].transpose(1, 0, 2).reshape(1024, 2304)
    full["dn_w_in"] = ga[:, :, 288:].transpose(1, 0, 2).reshape(1024, 6176)
    full["attn_w_out"] = gb[:, :128].reshape(1024, 1024)
    full["dn_w_out"] = gb[:, 128:].reshape(2048, 1024)
    meta_full = gc[:, :16].transpose(1, 0, 2).reshape(N_META, 1024)
    full["dn_conv_w"] = gc[:, 16:32].reshape(N_DEV, 4, 512).transpose(1, 0, 2).reshape(4, 4096)
    full["dn_norm_w"] = gc[:, 32].reshape(1, 1024)

    seq = x.shape[1]
    h0 = jnp.concatenate([jnp.zeros((FRONT_PAD, D_MODEL), F32), meta_full, x[0]], axis=0)
    target = jnp.concatenate([jnp.zeros((ATTN_BLOCK, D_MODEL), F32), loss_target[0]], axis=0)
    loss, dh0, g = _local_step(h0, target, full)
    grad_x = dh0[ATTN_BLOCK:ATTN_BLOCK + seq][None]
    g["meta_tokens"] = dh0[FRONT_PAD:ATTN_BLOCK]

    pa = jnp.concatenate([g["attn_w_in"].reshape(1024, N_DEV, 288), g["dn_w_in"].reshape(1024, N_DEV, 772)],
                         axis=2).transpose(1, 0, 2)
    pb = jnp.concatenate([g["attn_w_out"].reshape(N_DEV, 128, 1024), g["dn_w_out"].reshape(N_DEV, 256, 1024)], axis=1)
    dn_norm8 = jnp.concatenate([g["dn_norm_w"].reshape(N_DEV, 1, 128), jnp.zeros((N_DEV, 7, 128), F32)], axis=1)
    pc = jnp.concatenate([g["meta_tokens"].reshape(N_META, N_DEV, 128).transpose(1, 0, 2),
                          g["dn_conv_w"].reshape(4, N_DEV, 512).transpose(1, 0, 2).reshape(N_DEV, 16, 128), dn_norm8], axis=1)
    ps = small_of(g, loss)
    xa, xb, xc, xs = _exchange([pa, pb, pc, ps], [True, True, True, False], "exchange_grads")

    out = {}
    ma, mb, mc = rows_of(mom_m)
    va, vb, vc = rows_of(mom_v)
    ra = _adamw(xa, wa, ma, va, "adamw_a")
    rb = _adamw(xb, wb, mb, vb, "adamw_b")
    rc = _adamw(xc, wc, mc, vc, "adamw_c")
    zero = jnp.zeros((1, 128), F32)
    rs = _adamw(xs, small_of(shard, zero), small_of(mom_m, zero), small_of(mom_v, zero), "adamw_small")
    row_names = ["attn_w_in", "dn_w_in", "attn_w_out", "dn_w_out", "meta_tokens", "dn_conv_w", "dn_norm_w"]
    lead = {"attn_w_in", "dn_w_in", "attn_w_out", "dn_w_out", "dn_conv_w"}
    for kind in range(4):
        vals = dict(zip(row_names, _unpack_rows(ra[kind], rb[kind], rc[kind])))
        small = _unpack_small(rs[kind])
        vals.update(dict(zip(SMALL, small[:7])))
        if kind == 0:
            loss_total = small[7][0, 0]
        out[kind] = [vals[k][None] if k in lead else vals[k] for k in WEIGHTS]
    return (loss_total, grad_x, *out[0], *out[1], *out[2], *out[3])
```

```python
import functools
import math

import jax
import jax.numpy as jnp
from jax import lax
from jax.experimental import pallas as pl
from jax.experimental.pallas import tpu as pltpu

F32, BF16 = jnp.float32, jnp.bfloat16

D_MODEL = 1024
N_META = 16
NORM_EPS = 1e-6
ATTN_HEADS, ATTN_KV_HEADS, ATTN_GROUPS, ATTN_HD = 16, 2, 8, 64
ATTN_BLOCK = 128
FRONT_PAD = ATTN_BLOCK - N_META
DN_HD, DN_K_HEADS, DN_V_HEADS = 128, 8, 16
DN_CHUNK = 64
DN_KEY_W, DN_VAL_W = 1024, 2048
DN_CONV_W = 2 * DN_KEY_W + DN_VAL_W
DN_CONV_K = 4
N_DEV = 8
ROW_BLOCK = 384
VMEM_LIMIT = 56 * 1024 * 1024
NEG = -1e30

ADAM_LR, ADAM_B1, ADAM_B2, ADAM_EPS, ADAM_WD, ADAM_STEP = 0.001, 0.9, 0.999, 1e-08, 0.01, 10

NT = (((1,), (1,)), ((), ()))
TN = (((0,), (0,)), ((), ()))


def _cparams(sem=("arbitrary",)):
    return pltpu.CompilerParams(dimension_semantics=sem, vmem_limit_bytes=VMEM_LIMIT)


def _rms(x, w):
    return x * lax.rsqrt(jnp.mean(x * x, axis=-1, keepdims=True) + NORM_EPS) * w


def _silu(x):
    return x * jax.nn.sigmoid(x)


def _softplus(x):
    return jnp.maximum(x, 0.0) + jnp.log(1.0 + jnp.exp(-jnp.abs(x)))


def _bdot(a, b, dims=None):
    a, b = a.astype(BF16), b.astype(BF16)
    if dims is None:
        return jnp.dot(a, b, preferred_element_type=F32)
    return lax.dot_general(a, b, dims, preferred_element_type=F32)


def _hdot(a, b):
    return jnp.dot(a, b, preferred_element_type=F32, precision=lax.Precision.HIGHEST)


def _row_call(name, body, n_rows, rb, rows, consts, outs, accs=(), reverse=False, scratch=(), halos=()):
    n = n_rows // rb
    assert n * rb == n_rows
    idx = (lambda i: (n - 1 - i, 0)) if reverse else (lambda i: (i, 0))
    in_specs = [pl.BlockSpec((rb, a.shape[1]), idx) for a in rows]
    in_specs += [pl.BlockSpec((hr, a.shape[1]), fn) for a, hr, fn in halos]
    in_specs += [pl.BlockSpec(c.shape, functools.partial(lambda i, nd: (0,) * nd, nd=c.ndim)) for c in consts]
    out_specs = [pl.BlockSpec((rb, c), idx) for c, _ in outs]
    out_specs += [pl.BlockSpec(s, functools.partial(lambda i, nd: (0,) * nd, nd=len(s))) for s, _ in accs]
    out_shape = [jax.ShapeDtypeStruct((n_rows, c), dt) for c, dt in outs]
    out_shape += [jax.ShapeDtypeStruct(s, dt) for s, dt in accs]
    return pl.pallas_call(
        body, grid=(n,), in_specs=in_specs, out_specs=out_specs, out_shape=out_shape,
        scratch_shapes=list(scratch), name=name, compiler_params=_cparams(),
    )(*rows, *[a for a, _, _ in halos], *consts)


def _attn_in_fwd(h0, norm_w, w_in):
    T = h0.shape[0]

    def body(h_ref, nw_ref, w_ref, xn_ref, q_ref, kv_ref, gate_ref):
        xn = _rms(h_ref[...], nw_ref[...]).astype(BF16)
        xn_ref[...] = xn
        q_ref[...] = jnp.dot(xn, w_ref[:, 0:1024], preferred_element_type=F32)
        kv_ref[...] = jnp.dot(xn, w_ref[:, 1024:1280], preferred_element_type=F32)
        gate_ref[...] = jnp.dot(xn, w_ref[:, 1280:2304], preferred_element_type=F32)

    return _row_call("attn_in_fwd", body, T, ROW_BLOCK, [h0], [norm_w, w_in],
                     [(1024, BF16), (1024, F32), (256, F32), (1024, F32)])


def _attn_mask_bias(n, j):
    R, C = ATTN_GROUPS * ATTN_BLOCK, 2 * ATTN_BLOCK + N_META
    r = lax.broadcasted_iota(jnp.int32, (R, C), 0)
    c = lax.broadcasted_iota(jnp.int32, (R, C), 1)
    ql = r & (ATTN_BLOCK - 1)
    is_meta = c >= 2 * ATTN_BLOCK
    dist_band = ATTN_BLOCK + ql - c
    cmin = jnp.maximum(0, 2 * ATTN_BLOCK - ATTN_BLOCK * n)
    valid_band = (c >= cmin) & (dist_band >= 0) & (dist_band < ATTN_BLOCK)
    dist_meta = ATTN_BLOCK * n + ql - FRONT_PAD - (c - 2 * ATTN_BLOCK)
    valid = (is_meta & (dist_meta >= 0)) | (jnp.logical_not(is_meta) & valid_band)
    dist = jnp.minimum(jnp.where(is_meta, dist_meta, dist_band), ATTN_BLOCK).astype(F32)
    rr = lax.broadcasted_iota(jnp.int32, (R, 1), 0)
    head = (rr >> 7).astype(F32) + float(ATTN_GROUPS * j + 1)
    slope = jnp.exp(head * (-0.5 * math.log(2.0)))
    return valid, slope * dist


def _attn_group(q, k, v, sinkcol, qnw, knw, valid, bias):
    qn = _rms(q, qnw)
    kn = _rms(k, knw)
    s = _bdot(qn, kn, NT) * (ATTN_HD ** -0.5)
    s = jnp.where(valid, s - bias, NEG)
    m = lax.stop_gradient(jnp.maximum(jnp.max(s, axis=1, keepdims=True), sinkcol))
    e = jnp.exp(s - m)
    denom = jnp.sum(e, axis=1, keepdims=True) + jnp.exp(sinkcol - m)
    p = e / denom
    return _bdot(p, v)


def _sink_col(sinks_ref, j):
    rr = lax.broadcasted_iota(jnp.int32, (ATTN_GROUPS * ATTN_BLOCK, 1), 0) >> 7
    col = jnp.zeros((ATTN_GROUPS * ATTN_BLOCK, 1), F32)
    for hl in range(ATTN_GROUPS):
        col = jnp.where(rr == hl, sinks_ref[0, ATTN_GROUPS * j + hl], col)
    return col


def _attn_kv_tiles(kvp_ref, kvc_ref, kvm_ref, j):
    ksl = slice(ATTN_HD * j, ATTN_HD * (j + 1))
    vsl = slice(128 + ATTN_HD * j, 128 + ATTN_HD * (j + 1))
    k = jnp.concatenate([kvp_ref[:, ksl], kvc_ref[:, ksl], kvm_ref[FRONT_PAD:, ksl]], axis=0)
    v = jnp.concatenate([kvp_ref[:, vsl], kvc_ref[:, vsl], kvm_ref[FRONT_PAD:, vsl]], axis=0)
    return k, v


def _attn_core_fwd(q, kv, sinks, qnw, knw):
    T = q.shape[0]
    nb = T // ATTN_BLOCK

    def body(sinks_ref, q_ref, kvc_ref, kvp_ref, kvm_ref, qnw_ref, knw_ref, o_ref):
        n = pl.program_id(0)
        for j in range(ATTN_KV_HEADS):
            valid, bias = _attn_mask_bias(n, j)
            qs = jnp.concatenate(
                [q_ref[:, ATTN_HD * h:ATTN_HD * (h + 1)] for h in range(ATTN_GROUPS * j, ATTN_GROUPS * (j + 1))], axis=0)
            k, v = _attn_kv_tiles(kvp_ref, kvc_ref, kvm_ref, j)
            o = _attn_group(qs, k, v, _sink_col(sinks_ref, j), qnw_ref[...], knw_ref[...], valid, bias)
            for hl in range(ATTN_GROUPS):
                h = ATTN_GROUPS * j + hl
                o_ref[:, ATTN_HD * h:ATTN_HD * (h + 1)] = o[ATTN_BLOCK * hl:ATTN_BLOCK * (hl + 1), :]

    return pl.pallas_call(
        body, grid=(nb,),
        in_specs=[pl.BlockSpec(memory_space=pltpu.SMEM),
                  pl.BlockSpec((ATTN_BLOCK, 1024), lambda i: (i, 0)),
                  pl.BlockSpec((ATTN_BLOCK, 256), lambda i: (i, 0)),
                  pl.BlockSpec((ATTN_BLOCK, 256), lambda i: (jnp.maximum(i - 1, 0), 0)),
                  pl.BlockSpec((ATTN_BLOCK, 256), lambda i: (0, 0)),
                  pl.BlockSpec((1, ATTN_HD), lambda i: (0, 0)),
                  pl.BlockSpec((1, ATTN_HD), lambda i: (0, 0))],
        out_specs=pl.BlockSpec((ATTN_BLOCK, 1024), lambda i: (i, 0)),
        out_shape=jax.ShapeDtypeStruct((T, 1024), F32),
        name="attn_core_fwd", compiler_params=_cparams(),
    )(sinks, q, kv, kv, kv, qnw, knw)


def _attn_out_fwd(o, gate, h0, w_out):
    T = o.shape[0]

    def body(o_ref, g_ref, h_ref, w_ref, h1_ref):
        og = o_ref[...] * _silu(g_ref[...])
        h1_ref[...] = h_ref[...] + _bdot(og, w_ref[...])

    return _row_call("attn_out_fwd", body, T, ROW_BLOCK, [o, gate, h0], [w_out], [(1024, F32)])[0]


def _wgrad(xn, du, cg, name):
    T, kdim = xn.shape
    cdim = du.shape[1]
    nr, nc = T // ROW_BLOCK, cdim // cg
    assert nc * cg == cdim

    def body(x_ref, du_ref, dw_ref):
        @pl.when(pl.program_id(1) == 0)
        def _():
            dw_ref[...] = jnp.zeros_like(dw_ref)
        dw_ref[...] += _bdot(x_ref[...], du_ref[...], TN)

    return pl.pallas_call(
        body, grid=(nc, nr),
        in_specs=[pl.BlockSpec((ROW_BLOCK, kdim), lambda j, i: (i, 0)),
                  pl.BlockSpec((ROW_BLOCK, cg), lambda j, i: (i, j))],
        out_specs=pl.BlockSpec((kdim, cg), lambda j, i: (0, j)),
        out_shape=jax.ShapeDtypeStruct((kdim, cdim), F32),
        name=name, compiler_params=_cparams(("arbitrary", "arbitrary")),
    )(xn, du)


def _attn_out_bwd(dh1, o, gate, w_out):
    T = o.shape[0]

    def body(dh_ref, o_ref, g_ref, w_ref, do_ref, dg_ref, dw_ref):
        @pl.when(pl.program_id(0) == 0)
        def _():
            dw_ref[...] = jnp.zeros_like(dw_ref)
        dh = dh_ref[...]
        dog = _bdot(dh, w_ref[...], NT)
        og, vjp = jax.vjp(lambda o_, g_: o_ * _silu(g_), o_ref[...], g_ref[...])
        do, dg = vjp(dog)
        do_ref[...] = do
        dg_ref[...] = dg
        dw_ref[...] += _bdot(og, dh, TN)

    return _row_call("attn_out_bwd", body, T, ROW_BLOCK, [dh1, o, gate], [w_out],
                     [(1024, F32), (1024, F32)], [((1024, 1024), F32)])


def _attn_core_bwd(do, q, kv, sinks, qnw, knw):
    T = q.shape[0]
    nb = T // ATTN_BLOCK
    rev = lambda i: nb - 1 - i

    def body(sinks_ref, do_ref, q_ref, kvc_ref, kvp_ref, kvm_ref, qnw_ref, knw_ref,
             dq_ref, dkv_ref, dsinks_ref, dqnw_ref, dknw_ref, carry_ref, meta_ref):
        step = pl.program_id(0)
        n = rev(step)

        @pl.when(step == 0)
        def _():
            carry_ref[...] = jnp.zeros_like(carry_ref)
            meta_ref[...] = jnp.zeros_like(meta_ref)
            dsinks_ref[...] = jnp.zeros_like(dsinks_ref)
            dqnw_ref[...] = jnp.zeros_like(dqnw_ref)
            dknw_ref[...] = jnp.zeros_like(dknw_ref)

        lane16 = lax.broadcasted_iota(jnp.int32, (1, ATTN_HEADS), 1)
        dsinks = jnp.zeros((1, ATTN_HEADS), F32)
        for j in range(ATTN_KV_HEADS):
            valid, bias = _attn_mask_bias(n, j)
            heads = range(ATTN_GROUPS * j, ATTN_GROUPS * (j + 1))
            qs = jnp.concatenate([q_ref[:, ATTN_HD * h:ATTN_HD * (h + 1)] for h in heads], axis=0)
            dos = jnp.concatenate([do_ref[:, ATTN_HD * h:ATTN_HD * (h + 1)] for h in heads], axis=0)
            k, v = _attn_kv_tiles(kvp_ref, kvc_ref, kvm_ref, j)
            fn = functools.partial(_attn_group, valid=valid, bias=bias)
            _, vjp = jax.vjp(fn, qs, k, v, _sink_col(sinks_ref, j), qnw_ref[...], knw_ref[...])
            dqs, dk, dv, dsc, dqn, dkn = vjp(dos)
            dqnw_ref[...] += dqn
            dknw_ref[...] += dkn
            for hl in range(ATTN_GROUPS):
                h = ATTN_GROUPS * j + hl
                rows = slice(ATTN_BLOCK * hl, ATTN_BLOCK * (hl + 1))
                dq_ref[:, ATTN_HD * h:ATTN_HD * (h + 1)] = dqs[rows, :]
                dsinks = dsinks + jnp.where(lane16 == h, jnp.sum(dsc[rows, :]), 0.0)
            ksl = slice(ATTN_HD * j, ATTN_HD * (j + 1))
            vsl = slice(128 + ATTN_HD * j, 128 + ATTN_HD * (j + 1))
            for sl, d in ((ksl, dk), (vsl, dv)):
                dkv_ref[:, sl] = d[ATTN_BLOCK:2 * ATTN_BLOCK, :] + carry_ref[:, sl]
                carry_ref[:, sl] = d[0:ATTN_BLOCK, :]
                meta_ref[:, sl] += d[2 * ATTN_BLOCK:, :]
        dsinks_ref[...] += dsinks

        @pl.when(n == 0)
        def _():
            dkv_ref[FRONT_PAD:, :] += meta_ref[...]

    return pl.pallas_call(
        body, grid=(nb,),
        in_specs=[pl.BlockSpec(memory_space=pltpu.SMEM),
                  pl.BlockSpec((ATTN_BLOCK, 1024), lambda i: (rev(i), 0)),
                  pl.BlockSpec((ATTN_BLOCK, 1024), lambda i: (rev(i), 0)),
                  pl.BlockSpec((ATTN_BLOCK, 256), lambda i: (rev(i), 0)),
                  pl.BlockSpec((ATTN_BLOCK, 256), lambda i: (jnp.maximum(rev(i) - 1, 0), 0)),
                  pl.BlockSpec((ATTN_BLOCK, 256), lambda i: (0, 0)),
                  pl.BlockSpec((1, ATTN_HD), lambda i: (0, 0)),
                  pl.BlockSpec((1, ATTN_HD), lambda i: (0, 0))],
        out_specs=[pl.BlockSpec((ATTN_BLOCK, 1024), lambda i: (rev(i), 0)),
                   pl.BlockSpec((ATTN_BLOCK, 256), lambda i: (rev(i), 0)),
                   pl.BlockSpec((1, ATTN_HEADS), lambda i: (0, 0)),
                   pl.BlockSpec((1, ATTN_HD), lambda i: (0, 0)),
                   pl.BlockSpec((1, ATTN_HD), lambda i: (0, 0))],
        out_shape=[jax.ShapeDtypeStruct((T, 1024), F32), jax.ShapeDtypeStruct((T, 256), F32),
                   jax.ShapeDtypeStruct((1, ATTN_HEADS), F32), jax.ShapeDtypeStruct((1, ATTN_HD), F32),
                   jax.ShapeDtypeStruct((1, ATTN_HD), F32)],
        scratch_shapes=[pltpu.VMEM((ATTN_BLOCK, 256), F32), pltpu.VMEM((N_META, 256), F32)],
        name="attn_core_bwd", compiler_params=_cparams(),
    )(sinks, do, q, kv, kv, kv, qnw, knw)


def _attn_in_bwd(dq, dkv, dgate, h0, dh1, norm_w, w_in):
    T = h0.shape[0]

    def body(dq_ref, dkv_ref, dg_ref, h_ref, dh1_ref, nw_ref, w_ref, dh0_ref, dnw_ref):
        @pl.when(pl.program_id(0) == 0)
        def _():
            dnw_ref[...] = jnp.zeros_like(dnw_ref)
        dxn = (_bdot(dq_ref[...], w_ref[:, 0:1024], NT) + _bdot(dkv_ref[...], w_ref[:, 1024:1280], NT)
               + _bdot(dg_ref[...], w_ref[:, 1280:2304], NT))
        _, vjp = jax.vjp(_rms, h_ref[...], nw_ref[...])
        dh, dnw = vjp(dxn)
        dh0_ref[...] = dh1_ref[...] + dh
        dnw_ref[...] += dnw

    return _row_call("attn_in_bwd", body, T, ROW_BLOCK, [dq, dkv, dgate, h0, dh1], [norm_w, w_in],
                     [(1024, F32)], [((1, 1024), F32)])


def _dn_in_fwd(h1, norm_w, w_in):
    T = h1.shape[0]

    def body(h_ref, nw_ref, w_ref, xn_ref, qkv_ref, z_ref, ba_ref):
        xn = _rms(h_ref[...], nw_ref[...]).astype(BF16)
        xn_ref[...] = xn
        qkv_ref[...] = jnp.dot(xn, w_ref[:, 0:4096], preferred_element_type=F32)
        z_ref[...] = jnp.dot(xn, w_ref[:, 4096:6144], preferred_element_type=F32)
        ba_ref[...] = jnp.dot(xn, w_ref[:, 6144:6176], preferred_element_type=F32)

    return _row_call("dn_in_fwd", body, T, ROW_BLOCK, [h1], [norm_w, w_in],
                     [(1024, BF16), (4096, F32), (2048, F32), (32, F32)])


def _shift_down(cur, prev8, s):
    i8 = lax.broadcasted_iota(jnp.int32, (8, cur.shape[1]), 0)
    r = pltpu.roll(cur, s, 0)
    head = jnp.where(i8 < s, pltpu.roll(prev8, s, 0), r[0:8])
    return jnp.concatenate([head, r[8:]], axis=0)


def _shift_up(cur, next8, s):
    n = cur.shape[0]
    i8 = lax.broadcasted_iota(jnp.int32, (8, cur.shape[1]), 0)
    r = pltpu.roll(cur, n - s, 0)
    tail = jnp.where(i8 >= 8 - s, pltpu.roll(next8, 8 - s, 0), r[n - 8:])
    return jnp.concatenate([r[:n - 8], tail], axis=0)


def _conv_tile(cur, prev8, w):
    out = w[3:4, :] * cur
    for s in range(1, DN_CONV_K):
        out = out + w[3 - s:4 - s, :] * _shift_down(cur, prev8, s)
    return out


def _l2n(a, scale):
    return a * (lax.rsqrt(jnp.sum(a * a, axis=-1, keepdims=True) + NORM_EPS) * scale)


def _dn_post_tile(c, t):
    a = _silu(c)
    if t < DN_K_HEADS:
        return _l2n(a, DN_HD ** -0.5)
    if t < 2 * DN_K_HEADS:
        return _l2n(a, 1.0)
    return a


def _dn_beta_g(ba, a_log, dt_bias, live):
    beta = jax.nn.sigmoid(ba[:, 0:DN_V_HEADS]) * live
    g = -jnp.exp(a_log) * _softplus(ba[:, DN_V_HEADS:] + dt_bias) * live
    return beta, g


def _live_rows(i, rb):
    rows = i * rb + lax.broadcasted_iota(jnp.int32, (rb, 1), 0)
    return (rows >= FRONT_PAD).astype(F32)


def _halo_spec_args(x, rb):
    per = rb // 8
    return (x, 8, lambda i: (jnp.maximum(i * per - 1, 0), 0))


def _dn_conv_fwd(qkv, ba, conv_w, a_log, dt_bias):
    T = qkv.shape[0]

    def body(x_ref, ba_ref, halo_ref, cw_ref, al_ref, dtb_ref, q_ref, k_ref, v_ref, bg_ref):
        i = pl.program_id(0)
        first = (i > 0).astype(F32)
        for t in range(DN_CONV_W // 128):
            cols = slice(128 * t, 128 * (t + 1))
            c = _conv_tile(x_ref[:, cols], halo_ref[:, cols] * first, cw_ref[:, cols])
            out = _dn_post_tile(c, t)
            if t < DN_K_HEADS:
                q_ref[:, cols] = out
            elif t < 2 * DN_K_HEADS:
                k_ref[:, 128 * (t - 8):128 * (t - 7)] = out
            else:
                v_ref[:, 128 * (t - 16):128 * (t - 15)] = out
        beta, g = _dn_beta_g(ba_ref[...], al_ref[...], dtb_ref[...], _live_rows(i, ROW_BLOCK))
        bg_ref[:, 0:DN_V_HEADS] = beta
        bg_ref[:, DN_V_HEADS:] = g

    return _row_call("dn_conv_fwd", body, T, ROW_BLOCK, [qkv, ba], [conv_w, a_log, dt_bias],
                     [(1024, F32), (1024, F32), (2048, F32), (32, F32)], halos=[_halo_spec_args(qkv, ROW_BLOCK)])


def _chunk_masks():
    r = lax.broadcasted_iota(jnp.int32, (DN_CHUNK, DN_CHUNK), 0)
    c = lax.broadcasted_iota(jnp.int32, (DN_CHUNK, DN_CHUNK), 1)
    return r >= c, r > c, r == c, r <= c


def _dn_chunk_step(S, q, k, v, beta, g, masks):
    causal, strict, eye, upper = masks
    C = DN_CHUNK
    heads = range(DN_V_HEADS)
    qk_kk = [_bdot(jnp.concatenate([q[j], k[j]], axis=0), k[j], NT) for j in range(DN_K_HEADS)]
    g_row = [jnp.sum(jnp.where(eye, g[h], 0.0), axis=0, keepdims=True) for h in heads]
    gc_col = [jnp.sum(jnp.where(causal, g_row[h], 0.0), axis=1, keepdims=True) for h in heads]
    gc_row = [jnp.sum(jnp.where(upper, g[h], 0.0), axis=0, keepdims=True) for h in heads]
    g_last = [jnp.sum(g[h], axis=0, keepdims=True) for h in heads]
    decay = [jnp.exp(jnp.where(causal, gc_col[h] - gc_row[h], NEG)) for h in heads]
    eg = [jnp.exp(gc_col[h]) for h in heads]
    x = [jnp.where(strict, qk_kk[h // 2][C:] * beta[h] * decay[h], 0.0) * -1.0 for h in heads]
    ainv = [jnp.where(eye, 1.0, 0.0) + x[h] for h in heads]
    p = [_bdot(x[h], x[h]) for h in heads]
    for _ in range(4):
        r = [_bdot(jnp.concatenate([p[h], ainv[h]], axis=0), p[h]) for h in heads]
        ainv = [ainv[h] + r[h][C:] for h in heads]
        p = [r[h][:C] for h in heads]
    ainv = [ainv[h] + _bdot(ainv[h], p[h]) for h in heads]
    return _dn_chunk_tail(S, q, k, v, beta, qk_kk, decay, eg, gc_col, g_last, ainv)


def _dn_chunk_tail(S, q, k, v, beta, qk_kk, decay, eg, gc_col, g_last, ainv):
    C = DN_CHUNK
    heads = range(DN_V_HEADS)
    uw = [_bdot(ainv[h], jnp.concatenate([v[h] * beta[h], k[h // 2] * (beta[h] * eg[h])], axis=1)) for h in heads]
    ws_qs = [_bdot(jnp.concatenate([uw[h][:, DN_HD:], q[h // 2] * eg[h]], axis=0), S[h]) for h in heads]
    v_new = [uw[h][:, :DN_HD] - ws_qs[h][:C] for h in heads]
    o = [ws_qs[h][C:] + _bdot(qk_kk[h // 2][:C] * decay[h], v_new[h]) for h in heads]
    s_new = [S[h] * jnp.exp(g_last[h]) + _bdot(k[h // 2] * jnp.exp(g_last[h] - gc_col[h]), v_new[h], TN) for h in heads]
    return s_new, o


def _dn_chunk_tiles(q_ref, k_ref, v_ref, bg_ref):
    q = [q_ref[:, 128 * j:128 * (j + 1)] for j in range(DN_K_HEADS)]
    k = [k_ref[:, 128 * j:128 * (j + 1)] for j in range(DN_K_HEADS)]
    v = [v_ref[:, 128 * h:128 * (h + 1)] for h in range(DN_V_HEADS)]
    beta = [bg_ref[:, h:h + 1] for h in range(DN_V_HEADS)]
    g = [bg_ref[:, DN_V_HEADS + h:DN_V_HEADS + h + 1] for h in range(DN_V_HEADS)]
    return q, k, v, beta, g


def _dn_scan_fwd(qn, kn, v, bg):
    T = qn.shape[0]
    nc = T // DN_CHUNK

    def body(q_ref, k_ref, v_ref, bg_ref, o_ref, ssave_ref, s_ref):
        @pl.when(pl.program_id(0) == 0)
        def _():
            s_ref[...] = jnp.zeros_like(s_ref)
        s_old = [s_ref[h] for h in range(DN_V_HEADS)]
        for h in range(DN_V_HEADS):
            ssave_ref[0, h] = s_old[h]
        s_new, o = _dn_chunk_step(s_old, *_dn_chunk_tiles(q_ref, k_ref, v_ref, bg_ref), _chunk_masks())
        for h in range(DN_V_HEADS):
            o_ref[:, 128 * h:128 * (h + 1)] = o[h]
            s_ref[h] = s_new[h]

    return pl.pallas_call(
        body, grid=(nc,),
        in_specs=[pl.BlockSpec((DN_CHUNK, 1024), lambda i: (i, 0)),
                  pl.BlockSpec((DN_CHUNK, 1024), lambda i: (i, 0)),
                  pl.BlockSpec((DN_CHUNK, 2048), lambda i: (i, 0)),
                  pl.BlockSpec((DN_CHUNK, 32), lambda i: (i, 0))],
        out_specs=[pl.BlockSpec((DN_CHUNK, 2048), lambda i: (i, 0)),
                   pl.BlockSpec((1, DN_V_HEADS, DN_HD, DN_HD), lambda i: (i, 0, 0, 0))],
        out_shape=[jax.ShapeDtypeStruct((T, 2048), F32),
                   jax.ShapeDtypeStruct((nc, DN_V_HEADS, DN_HD, DN_HD), F32)],
        scratch_shapes=[pltpu.VMEM((DN_V_HEADS, DN_HD, DN_HD), F32)],
        name="dn_scan_fwd", compiler_params=_cparams(),
    )(qn, kn, v, bg)


def _dn_gate_tile(o, z, onw):
    return _rms(o, onw) * _silu(z)


def _dn_out_fwd(o, z, h1, target, w_out, onw):
    T = o.shape[0]

    def body(o_ref, z_ref, h_ref, t_ref, w_ref, onw_ref, dy_ref, og_ref, loss_ref):
        i = pl.program_id(0)

        @pl.when(i == 0)
        def _():
            loss_ref[...] = jnp.zeros_like(loss_ref)
        for h in range(DN_V_HEADS):
            cols = slice(128 * h, 128 * (h + 1))
            og_ref[:, cols] = _dn_gate_tile(o_ref[:, cols], z_ref[:, cols], onw_ref[...]).astype(BF16)
        y = h_ref[...] + jnp.dot(og_ref[...], w_ref[...], preferred_element_type=F32)
        rows = i * ROW_BLOCK + lax.broadcasted_iota(jnp.int32, (ROW_BLOCK, 1), 0)
        diff = jnp.where(rows >= FRONT_PAD + N_META, y - t_ref[...], 0.0)
        dy_ref[...] = diff * (1.0 / D_MODEL)
        loss_ref[...] += jnp.sum(diff * diff) * (0.5 / D_MODEL)

    return _row_call("dn_out_fwd", body, T, ROW_BLOCK, [o, z, h1, target], [w_out, onw],
                     [(1024, F32), (2048, BF16)], [((1, 128), F32)])


def _dn_out_bwd(dy, o, z, w_out, onw):
    T = o.shape[0]

    def body(dy_ref, o_ref, z_ref, w_ref, onw_ref, do_ref, dz_ref, donw_ref):
        @pl.when(pl.program_id(0) == 0)
        def _():
            donw_ref[...] = jnp.zeros_like(donw_ref)
        dy = dy_ref[...].astype(BF16)
        donw = jnp.zeros((1, DN_HD), F32)
        for h in range(DN_V_HEADS):
            cols = slice(128 * h, 128 * (h + 1))
            dog = lax.dot_general(dy, w_ref[cols, :], NT, preferred_element_type=F32)
            _, vjp = jax.vjp(_dn_gate_tile, o_ref[:, cols], z_ref[:, cols], onw_ref[...])
            do, dz, dn = vjp(dog)
            do_ref[:, cols] = do
            dz_ref[:, cols] = dz
            donw = donw + dn
        donw_ref[...] += donw

    return _row_call("dn_out_bwd", body, T, ROW_BLOCK, [dy, o, z], [w_out, onw],
                     [(2048, F32), (2048, F32)], [((1, DN_HD), F32)])


def _dn_scan_bwd(do, qn, kn, v, bg, ssave):
    T = qn.shape[0]
    nc = T // DN_CHUNK
    rev = lambda i: nc - 1 - i

    def body(do_ref, q_ref, k_ref, v_ref, bg_ref, ss_ref, dq_ref, dk_ref, dv_ref, dbg_ref, ds_ref):
        @pl.when(pl.program_id(0) == 0)
        def _():
            ds_ref[...] = jnp.zeros_like(ds_ref)
        lane32 = lax.broadcasted_iota(jnp.int32, (1, 2 * DN_V_HEADS), 1)
        heads = range(DN_V_HEADS)
        fn = functools.partial(_dn_chunk_step, masks=_chunk_masks())
        _, vjp = jax.vjp(fn, [ss_ref[0, h] for h in heads], *_dn_chunk_tiles(q_ref, k_ref, v_ref, bg_ref))
        ds, dq, dk, dv, dbeta, dg = vjp(([ds_ref[h] for h in heads], [do_ref[:, 128 * h:128 * (h + 1)] for h in heads]))
        dbg = jnp.zeros((DN_CHUNK, 2 * DN_V_HEADS), F32)
        for h in heads:
            ds_ref[h] = ds[h]
            dv_ref[:, 128 * h:128 * (h + 1)] = dv[h]
            dbg = dbg + jnp.where(lane32 == h, dbeta[h], 0.0) + jnp.where(lane32 == DN_V_HEADS + h, dg[h], 0.0)
        for j in range(DN_K_HEADS):
            dq_ref[:, 128 * j:128 * (j + 1)] = dq[j]
            dk_ref[:, 128 * j:128 * (j + 1)] = dk[j]
        dbg_ref[...] = dbg

    return pl.pallas_call(
        body, grid=(nc,),
        in_specs=[pl.BlockSpec((DN_CHUNK, 2048), lambda i: (rev(i), 0)),
                  pl.BlockSpec((DN_CHUNK, 1024), lambda i: (rev(i), 0)),
                  pl.BlockSpec((DN_CHUNK, 1024), lambda i: (rev(i), 0)),
                  pl.BlockSpec((DN_CHUNK, 2048), lambda i: (rev(i), 0)),
                  pl.BlockSpec((DN_CHUNK, 32), lambda i: (rev(i), 0)),
                  pl.BlockSpec((1, DN_V_HEADS, DN_HD, DN_HD), lambda i: (rev(i), 0, 0, 0))],
        out_specs=[pl.BlockSpec((DN_CHUNK, 1024), lambda i: (rev(i), 0)),
                   pl.BlockSpec((DN_CHUNK, 1024), lambda i: (rev(i), 0)),
                   pl.BlockSpec((DN_CHUNK, 2048), lambda i: (rev(i), 0)),
                   pl.BlockSpec((DN_CHUNK, 32), lambda i: (rev(i), 0))],
        out_shape=[jax.ShapeDtypeStruct((T, 1024), F32), jax.ShapeDtypeStruct((T, 1024), F32),
                   jax.ShapeDtypeStruct((T, 2048), F32), jax.ShapeDtypeStruct((T, 32), F32)],
        scratch_shapes=[pltpu.VMEM((DN_V_HEADS, DN_HD, DN_HD), F32)],
        name="dn_scan_bwd", compiler_params=_cparams(),
    )(do, qn, kn, v, bg, ssave)


def _dn_conv_bwd(dqn, dkn, dv, dbg, qkv, ba, conv_w, a_log, dt_bias):
    T = qkv.shape[0]
    nr = T // ROW_BLOCK

    def body(dq_ref, dk_ref, dv_ref, dbg_ref, x_ref, ba_ref, halo_ref, cw_ref, al_ref, dtb_ref,
             dx_ref, dba_ref, dcw_ref, dal_ref, ddtb_ref, carry_ref):
        step = pl.program_id(0)
        i = nr - 1 - step

        @pl.when(step == 0)
        def _():
            carry_ref[...] = jnp.zeros_like(carry_ref)
            dcw_ref[...] = jnp.zeros_like(dcw_ref)
            dal_ref[...] = jnp.zeros_like(dal_ref)
            ddtb_ref[...] = jnp.zeros_like(ddtb_ref)
        first = (i > 0).astype(F32)
        for t in range(DN_CONV_W // 128):
            cols = slice(128 * t, 128 * (t + 1))
            cur, prev8, w = x_ref[:, cols], halo_ref[:, cols] * first, cw_ref[:, cols]
            c = _conv_tile(cur, prev8, w)
            if t < DN_K_HEADS:
                dout = dq_ref[:, cols]
            elif t < 2 * DN_K_HEADS:
                dout = dk_ref[:, 128 * (t - 8):128 * (t - 7)]
            else:
                dout = dv_ref[:, 128 * (t - 16):128 * (t - 15)]
            _, vjp = jax.vjp(functools.partial(_dn_post_tile, t=t), c)
            (dc,) = vjp(dout)
            nxt = carry_ref[:, cols]
            dx = w[3:4, :] * dc
            dcw_ref[3:4, cols] += jnp.sum(dc * cur, axis=0, keepdims=True)
            for s in range(1, DN_CONV_K):
                dx = dx + w[3 - s:4 - s, :] * _shift_up(dc, nxt, s)
                dcw_ref[3 - s:4 - s, cols] += jnp.sum(dc * _shift_down(cur, prev8, s), axis=0, keepdims=True)
            dx_ref[:, cols] = dx
            carry_ref[:, cols] = dc[0:8, :]
        fn = functools.partial(_dn_beta_g, live=_live_rows(i, ROW_BLOCK))
        _, vjp = jax.vjp(fn, ba_ref[...], al_ref[...], dtb_ref[...])
        dba, dal, ddtb = vjp((dbg_ref[:, 0:DN_V_HEADS], dbg_ref[:, DN_V_HEADS:]))
        dba_ref[...] = dba
        dal_ref[...] += dal
        ddtb_ref[...] += ddtb

    per = ROW_BLOCK // 8
    halo = (qkv, 8, lambda s: (jnp.maximum((nr - 1 - s) * per - 1, 0), 0))
    return _row_call("dn_conv_bwd", body, T, ROW_BLOCK, [dqn, dkn, dv, dbg, qkv, ba], [conv_w, a_log, dt_bias],
                     [(4096, F32), (32, F32)], [((DN_CONV_K, 4096), F32), ((1, DN_V_HEADS), F32), ((1, DN_V_HEADS), F32)],
                     reverse=True, scratch=[pltpu.VMEM((8, 4096), F32)], halos=[halo])


def _dn_in_bwd(dqkv, dz, dba, h1, dy, norm_w, w_in):
    T = h1.shape[0]

    def body(dqkv_ref, dz_ref, dba_ref, h_ref, dy_ref, nw_ref, w_ref, dh_ref, dnw_ref):
        @pl.when(pl.program_id(0) == 0)
        def _():
            dnw_ref[...] = jnp.zeros_like(dnw_ref)
        dxn = (_bdot(dqkv_ref[...], w_ref[:, 0:4096], NT) + _bdot(dz_ref[...], w_ref[:, 4096:6144], NT)
               + _bdot(dba_ref[...], w_ref[:, 6144:6176], NT))
        _, vjp = jax.vjp(_rms, h_ref[...], nw_ref[...])
        dh, dnw = vjp(dxn)
        dh_ref[...] = (dy_ref[...] + dh) * _live_rows(pl.program_id(0), ROW_BLOCK)
        dnw_ref[...] += dnw

    return _row_call("dn_in_bwd", body, T, ROW_BLOCK, [dqkv, dz, dba, h1, dy], [norm_w, w_in],
                     [(1024, F32)], [((1, 1024), F32)])


def _exchange(parts, scatter, name):
    n = len(parts)
    out_shape = [jax.ShapeDtypeStruct(p.shape if sc else (N_DEV,) + p.shape, p.dtype) for p, sc in zip(parts, scatter)]

    def body(*refs):
        ins, outs = refs[:n], refs[n:2 * n]
        send_sems, recv_sems, local_sems = refs[2 * n:]
        x, y, c = lax.axis_index("x"), lax.axis_index("y"), lax.axis_index("c")
        me = 4 * x + 2 * y + c
        peers = []
        for k in range(1, N_DEV):
            px = 1 - x if k & 4 else x
            py = 1 - y if k & 2 else y
            pc = 1 - c if k & 1 else c
            peers.append(((px, py, pc), 4 * px + 2 * py + pc))

        def src(a, idx):
            return ins[a].at[idx] if scatter[a] else ins[a]

        local = [pltpu.make_async_copy(src(a, me), outs[a].at[me], local_sems.at[a]) for a in range(n)]
        for cp in local:
            cp.start()
        for a in range(n):
            for k, (dev, idx) in enumerate(peers):
                pltpu.make_async_remote_copy(
                    src_ref=src(a, idx), dst_ref=outs[a].at[me], send_sem=send_sems.at[a, k], recv_sem=recv_sems.at[a, k],
                    device_id=dev, device_id_type=pl.DeviceIdType.MESH).start()
        for a in range(n):
            for k, (dev, idx) in enumerate(peers):
                pltpu.make_async_remote_copy(
                    src_ref=src(a, idx), dst_ref=outs[a].at[idx], send_sem=send_sems.at[a, k], recv_sem=recv_sems.at[a, k],
                    device_id=dev, device_id_type=pl.DeviceIdType.MESH).wait()
        for cp in local:
            cp.wait()

    hbm = pl.BlockSpec(memory_space=pltpu.HBM)
    return pl.pallas_call(
        body, out_shape=out_shape, in_specs=[hbm] * n, out_specs=[hbm] * n,
        scratch_shapes=[pltpu.SemaphoreType.DMA((n, N_DEV - 1)), pltpu.SemaphoreType.DMA((n, N_DEV - 1)),
                        pltpu.SemaphoreType.DMA((n,))],
        name=name,
    )(*parts)


def _adam_rows(rows):
    for rb in (128, 64, 40, 16, 8):
        if rows % rb == 0:
            return rb
    return rows


def _adamw(stack, w, m, v, name):
    R, C = w.shape
    rb = _adam_rows(R)

    def body(s_ref, w_ref, m_ref, v_ref, g_ref, d_ref, nm_ref, nv_ref):
        g = s_ref[0]
        for s in range(1, N_DEV):
            g = g + s_ref[s]
        nm = ADAM_B1 * m_ref[...] + (1.0 - ADAM_B1) * g
        nv = ADAM_B2 * v_ref[...] + (1.0 - ADAM_B2) * (g * g)
        m_hat = nm / (1.0 - ADAM_B1 ** ADAM_STEP)
        v_hat = nv / (1.0 - ADAM_B2 ** ADAM_STEP)
        g_ref[...] = g
        d_ref[...] = -ADAM_LR * (m_hat / (jnp.sqrt(v_hat) + ADAM_EPS) + ADAM_WD * w_ref[...])
        nm_ref[...] = nm
        nv_ref[...] = nv

    blk = pl.BlockSpec((rb, C), lambda i: (i, 0))
    return pl.pallas_call(
        body, grid=(R // rb,),
        in_specs=[pl.BlockSpec((N_DEV, rb, C), lambda i: (0, i, 0)), blk, blk, blk],
        out_specs=[blk] * 4, out_shape=[jax.ShapeDtypeStruct((R, C), F32)] * 4,
        name=name, compiler_params=_cparams(),
    )(stack, w, m, v)


def _pad_rows8(a):
    return jnp.concatenate([a, jnp.zeros((8 - a.shape[0], a.shape[1]), a.dtype)], axis=0) if a.shape[0] < 8 else a


def _pack_small(norm_w, qnw, knw, sinks, a_log, dt_bias, onw, extra):
    z = lambda n: jnp.zeros((1, n), F32)
    row = jnp.concatenate([norm_w, qnw, knw, sinks, a_log, dt_bias, z(80), onw, extra, z(512)], axis=1)
    return row.reshape(16, 128)


def _unpack_small(p):
    row = p.reshape(1, 2048)
    cut = lambda a, n: row[:, a:a + n]
    return (cut(0, 1024), cut(1024, 64), cut(1088, 64), cut(1152, 16), cut(1168, 16), cut(1184, 16), cut(1280, 128),
            cut(1408, 128))


def _pack_rows(w_in_a, w_in_d, w_out_a, w_out_d, meta, conv, dn_norm):
    a = jnp.concatenate([w_in_a, w_in_d], axis=1)
    b = jnp.concatenate([w_out_a, w_out_d], axis=0)
    c = jnp.concatenate([meta, conv.reshape(16, 128), _pad_rows8(dn_norm)], axis=0)
    return a, b, c


def _unpack_rows(a, b, c):
    return (a[:, :288], a[:, 288:], b[:128], b[128:], c[:16], c[16:32].reshape(4, 512), c[32:33])


def _local_step(h0, target, w):
    xn0, q, kv, gate = _attn_in_fwd(h0, w["attn_norm_w"], w["attn_w_in"])
    o = _attn_core_fwd(q, kv, w["attn_sinks"], w["attn_q_norm_w"], w["attn_k_norm_w"])
    h1 = _attn_out_fwd(o, gate, h0, w["attn_w_out"])
    xn1, qkv, z, ba = _dn_in_fwd(h1, w["dn_norm_w"], w["dn_w_in"])
    qn, kn, v, bg = _dn_conv_fwd(qkv, ba, w["dn_conv_w"], w["dn_a_log"], w["dn_dt_bias"])
    o_dn, ssave = _dn_scan_fwd(qn, kn, v, bg)
    dy, og_dn, loss = _dn_out_fwd(o_dn, z, h1, target, w["dn_w_out"], w["dn_o_norm_w"])

    g = {}
    do_dn, dz, g["dn_o_norm_w"] = _dn_out_bwd(dy, o_dn, z, w["dn_w_out"], w["dn_o_norm_w"])
    g["dn_w_out"] = _wgrad(og_dn, dy, 1024, "wgrad_dn_out")
    dqn, dkn, dv, dbg = _dn_scan_bwd(do_dn, qn, kn, v, bg, ssave)
    dqkv, dba, g["dn_conv_w"], g["dn_a_log"], g["dn_dt_bias"] = _dn_conv_bwd(
        dqn, dkn, dv, dbg, qkv, ba, w["dn_conv_w"], w["dn_a_log"], w["dn_dt_bias"])
    dh1, g["dn_norm_w"] = _dn_in_bwd(dqkv, dz, dba, h1, dy, w["dn_norm_w"], w["dn_w_in"])
    g["dn_w_in"] = jnp.concatenate([_wgrad(xn1, dqkv, 1024, "wgrad_dn_qkv"), _wgrad(xn1, dz, 1024, "wgrad_dn_z"),
                                    _wgrad(xn1, dba, 32, "wgrad_dn_ba")], axis=1)
    do, dgate, g["attn_w_out"] = _attn_out_bwd(dh1, o, gate, w["attn_w_out"])
    dq, dkv, g["attn_sinks"], g["attn_q_norm_w"], g["attn_k_norm_w"] = _attn_core_bwd(
        do, q, kv, w["attn_sinks"], w["attn_q_norm_w"], w["attn_k_norm_w"])
    dh0, g["attn_norm_w"] = _attn_in_bwd(dq, dkv, dgate, h0, dh1, w["attn_norm_w"], w["attn_w_in"])
    g["attn_w_in"] = jnp.concatenate([_wgrad(xn0, dq, 1024, "wgrad_attn_q"), _wgrad(xn0, dkv, 256, "wgrad_attn_kv"),
                                      _wgrad(xn0, dgate, 1024, "wgrad_attn_gate")], axis=1)
    return loss, dh0, g


WEIGHTS = ['meta_tokens', 'attn_norm_w', 'attn_w_in', 'attn_q_norm_w', 'attn_k_norm_w', 'attn_sinks', 'attn_w_out',
           'dn_norm_w', 'dn_w_in', 'dn_conv_w', 'dn_a_log', 'dn_dt_bias', 'dn_o_norm_w', 'dn_w_out']
SMALL = ['attn_norm_w', 'attn_q_norm_w', 'attn_k_norm_w', 'attn_sinks', 'dn_a_log', 'dn_dt_bias', 'dn_o_norm_w']


def kernel(x, meta_tokens, attn_norm_w, attn_w_in, attn_q_norm_w, attn_k_norm_w, attn_sinks, attn_w_out, dn_norm_w, dn_w_in, dn_conv_w, dn_a_log, dn_dt_bias, dn_o_norm_w, dn_w_out, loss_target, m_meta_tokens, m_attn_norm_w, m_attn_w_in, m_attn_q_norm_w, m_attn_k_norm_w, m_attn_sinks, m_attn_w_out, m_dn_norm_w, m_dn_w_in, m_dn_conv_w, m_dn_a_log, m_dn_dt_bias, m_dn_o_norm_w, m_dn_w_out, v_meta_tokens, v_attn_norm_w, v_attn_w_in, v_attn_q_norm_w, v_attn_k_norm_w, v_attn_sinks, v_attn_w_out, v_dn_norm_w, v_dn_w_in, v_dn_conv_w, v_dn_a_log, v_dn_dt_bias, v_dn_o_norm_w, v_dn_w_out):
    shard = dict(meta_tokens=meta_tokens, attn_norm_w=attn_norm_w, attn_w_in=attn_w_in[0], attn_q_norm_w=attn_q_norm_w,
                 attn_k_norm_w=attn_k_norm_w, attn_sinks=attn_sinks, attn_w_out=attn_w_out[0], dn_norm_w=dn_norm_w,
                 dn_w_in=dn_w_in[0], dn_conv_w=dn_conv_w[0], dn_a_log=dn_a_log, dn_dt_bias=dn_dt_bias,
                 dn_o_norm_w=dn_o_norm_w, dn_w_out=dn_w_out[0])
    mom_m = dict(meta_tokens=m_meta_tokens, attn_norm_w=m_attn_norm_w, attn_w_in=m_attn_w_in[0], attn_q_norm_w=m_attn_q_norm_w,
                 attn_k_norm_w=m_attn_k_norm_w, attn_sinks=m_attn_sinks, attn_w_out=m_attn_w_out[0], dn_norm_w=m_dn_norm_w,
                 dn_w_in=m_dn_w_in[0], dn_conv_w=m_dn_conv_w[0], dn_a_log=m_dn_a_log, dn_dt_bias=m_dn_dt_bias,
                 dn_o_norm_w=m_dn_o_norm_w, dn_w_out=m_dn_w_out[0])
    mom_v = dict(meta_tokens=v_meta_tokens, attn_norm_w=v_attn_norm_w, attn_w_in=v_attn_w_in[0], attn_q_norm_w=v_attn_q_norm_w,
                 attn_k_norm_w=v_attn_k_norm_w, attn_sinks=v_attn_sinks, attn_w_out=v_attn_w_out[0], dn_norm_w=v_dn_norm_w,
                 dn_w_in=v_dn_w_in[0], dn_conv_w=v_dn_conv_w[0], dn_a_log=v_dn_a_log, dn_dt_bias=v_dn_dt_bias,
                 dn_o_norm_w=v_dn_o_norm_w, dn_w_out=v_dn_w_out[0])

    def rows_of(d):
        return _pack_rows(d["attn_w_in"], d["dn_w_in"], d["attn_w_out"], d["dn_w_out"], d["meta_tokens"], d["dn_conv_w"],
                          d["dn_norm_w"])

    def small_of(d, extra):
        return _pack_small(*[d[k] for k in SMALL], extra)

    wa, wb, wc = rows_of(shard)
    ga, gb, gc = _exchange([wa.astype(BF16), wb.astype(BF16), wc], [False, False, False], "gather_weights")
    full = {k: shard[k] for k in SMALL}
    full["attn_w_in"] = ga[:, :, :288].transpose(1, 0, 2).reshape(1024, 2304)
    full["dn_w_in"] = ga[:, :, 288:].transpose(1, 0, 2).reshape(1024, 6176)
    full["attn_w_out"] = gb[:, :128].reshape(1024, 1024)
    full["dn_w_out"] = gb[:, 128:].reshape(2048, 1024)
    meta_full = gc[:, :16].transpose(1, 0, 2).reshape(N_META, 1024)
    full["dn_conv_w"] = gc[:, 16:32].reshape(N_DEV, 4, 512).transpose(1, 0, 2).reshape(4, 4096)
    full["dn_norm_w"] = gc[:, 32].reshape(1, 1024)

    seq = x.shape[1]
    h0 = jnp.concatenate([jnp.zeros((FRONT_PAD, D_MODEL), F32), meta_full, x[0]], axis=0)
    target = jnp.concatenate([jnp.zeros((ATTN_BLOCK, D_MODEL), F32), loss_target[0]], axis=0)
    loss, dh0, g = _local_step(h0, target, full)
    grad_x = dh0[ATTN_BLOCK:ATTN_BLOCK + seq][None]
    g["meta_tokens"] = dh0[FRONT_PAD:ATTN_BLOCK]

    pa = jnp.concatenate([g["attn_w_in"].reshape(1024, N_DEV, 288), g["dn_w_in"].reshape(1024, N_DEV, 772)],
                         axis=2).transpose(1, 0, 2)
    pb = jnp.concatenate([g["attn_w_out"].reshape(N_DEV, 128, 1024), g["dn_w_out"].reshape(N_DEV, 256, 1024)], axis=1)
    dn_norm8 = jnp.concatenate([g["dn_norm_w"].reshape(N_DEV, 1, 128), jnp.zeros((N_DEV, 7, 128), F32)], axis=1)
    pc = jnp.concatenate([g["meta_tokens"].reshape(N_META, N_DEV, 128).transpose(1, 0, 2),
                          g["dn_conv_w"].reshape(4, N_DEV, 512).transpose(1, 0, 2).reshape(N_DEV, 16, 128), dn_norm8], axis=1)
    ps = small_of(g, loss)
    xa, xb, xc, xs = _exchange([pa, pb, pc, ps], [True, True, True, False], "exchange_grads")

    out = {}
    ma, mb, mc = rows_of(mom_m)
    va, vb, vc = rows_of(mom_v)
    ra = _adamw(xa, wa, ma, va, "adamw_a")
    rb = _adamw(xb, wb, mb, vb, "adamw_b")
    rc = _adamw(xc, wc, mc, vc, "adamw_c")
    zero = jnp.zeros((1, 128), F32)
    rs = _adamw(xs, small_of(shard, zero), small_of(mom_m, zero), small_of(mom_v, zero), "adamw_small")
    row_names = ["attn_w_in", "dn_w_in", "attn_w_out", "dn_w_out", "meta_tokens", "dn_conv_w", "dn_norm_w"]
    lead = {"attn_w_in", "dn_w_in", "attn_w_out", "dn_w_out", "dn_conv_w"}
    for kind in range(4):
        vals = dict(zip(row_names, _unpack_rows(ra[kind], rb[kind], rc[kind])))
        small = _unpack_small(rs[kind])
        vals.update(dict(zip(SMALL, small[:7])))
        if kind == 0:
            loss_total = small[7][0, 0]
        out[kind] = [vals[k][None] if k in lead else vals[k] for k in WEIGHTS]
    return (loss_total, grad_x, *out[0], *out[1], *out[2], *out[3])
```

```python
import functools
import math

import jax
import jax.numpy as jnp
from jax import lax
from jax.experimental import pallas as pl
from jax.experimental.pallas import tpu as pltpu

F32, BF16 = jnp.float32, jnp.bfloat16

D_MODEL = 1024
N_META = 16
NORM_EPS = 1e-6
ATTN_HEADS, ATTN_KV_HEADS, ATTN_GROUPS, ATTN_HD = 16, 2, 8, 64
ATTN_BLOCK = 128
FRONT_PAD = ATTN_BLOCK - N_META
DN_HD, DN_K_HEADS, DN_V_HEADS = 128, 8, 16
DN_CHUNK = 64
SCAN_BWD_GROUP = 4
DN_KEY_W, DN_VAL_W = 1024, 2048
DN_CONV_W = 2 * DN_KEY_W + DN_VAL_W
DN_CONV_K = 4
N_DEV = 8
ROW_BLOCK = 384
VMEM_LIMIT = 56 * 1024 * 1024
NEG = -1e30

ADAM_LR, ADAM_B1, ADAM_B2, ADAM_EPS, ADAM_WD, ADAM_STEP = 0.001, 0.9, 0.999, 1e-08, 0.01, 10

NT = (((1,), (1,)), ((), ()))
TN = (((0,), (0,)), ((), ()))


def _cparams(sem=("arbitrary",)):
    return pltpu.CompilerParams(dimension_semantics=sem, vmem_limit_bytes=VMEM_LIMIT)


def _rms(x, w):
    return x * lax.rsqrt(jnp.mean(x * x, axis=-1, keepdims=True) + NORM_EPS) * w


def _silu(x):
    return x * jax.nn.sigmoid(x)


def _softplus(x):
    return jnp.maximum(x, 0.0) + jnp.log(1.0 + jnp.exp(-jnp.abs(x)))


def _bdot(a, b, dims=None):
    a, b = a.astype(BF16), b.astype(BF16)
    if dims is None:
        return jnp.dot(a, b, preferred_element_type=F32)
    return lax.dot_general(a, b, dims, preferred_element_type=F32)


def _hdot(a, b):
    return jnp.dot(a, b, preferred_element_type=F32, precision=lax.Precision.HIGHEST)


def _row_call(name, body, n_rows, rb, rows, consts, outs, accs=(), reverse=False, scratch=(), halos=()):
    n = n_rows // rb
    assert n * rb == n_rows
    idx = (lambda i: (n - 1 - i, 0)) if reverse else (lambda i: (i, 0))
    in_specs = [pl.BlockSpec((rb, a.shape[1]), idx) for a in rows]
    in_specs += [pl.BlockSpec((hr, a.shape[1]), fn) for a, hr, fn in halos]
    in_specs += [pl.BlockSpec(c.shape, functools.partial(lambda i, nd: (0,) * nd, nd=c.ndim)) for c in consts]
    out_specs = [pl.BlockSpec((rb, c), idx) for c, _ in outs]
    out_specs += [pl.BlockSpec(s, functools.partial(lambda i, nd: (0,) * nd, nd=len(s))) for s, _ in accs]
    out_shape = [jax.ShapeDtypeStruct((n_rows, c), dt) for c, dt in outs]
    out_shape += [jax.ShapeDtypeStruct(s, dt) for s, dt in accs]
    return pl.pallas_call(
        body, grid=(n,), in_specs=in_specs, out_specs=out_specs, out_shape=out_shape,
        scratch_shapes=list(scratch), name=name, compiler_params=_cparams(),
    )(*rows, *[a for a, _, _ in halos], *consts)


def _attn_in_fwd(h0, norm_w, w_in):
    T = h0.shape[0]

    def body(h_ref, nw_ref, w_ref, xn_ref, q_ref, kv_ref, gate_ref):
        xn = _rms(h_ref[...], nw_ref[...]).astype(BF16)
        xn_ref[...] = xn
        q_ref[...] = jnp.dot(xn, w_ref[:, 0:1024], preferred_element_type=F32)
        kv_ref[...] = jnp.dot(xn, w_ref[:, 1024:1280], preferred_element_type=F32)
        gate_ref[...] = jnp.dot(xn, w_ref[:, 1280:2304], preferred_element_type=F32)

    return _row_call("attn_in_fwd", body, T, ROW_BLOCK, [h0], [norm_w, w_in],
                     [(1024, BF16), (1024, F32), (256, F32), (1024, F32)])


def _attn_bias(n, j):
    R, C = ATTN_GROUPS * ATTN_BLOCK, 2 * ATTN_BLOCK + N_META
    r = lax.broadcasted_iota(jnp.int32, (R, C), 0)
    c = lax.broadcasted_iota(jnp.int32, (R, C), 1)
    ql = r & (ATTN_BLOCK - 1)
    is_meta = c >= 2 * ATTN_BLOCK
    dist_band = ATTN_BLOCK + ql - c
    cmin = jnp.maximum(0, 2 * ATTN_BLOCK - ATTN_BLOCK * n)
    valid_band = (c >= cmin) & (dist_band >= 0) & (dist_band < ATTN_BLOCK)
    dist_meta = ATTN_BLOCK * n + ql - FRONT_PAD - (c - 2 * ATTN_BLOCK)
    valid = (is_meta & (dist_meta >= 0)) | (jnp.logical_not(is_meta) & valid_band)
    dist = jnp.minimum(jnp.where(is_meta, dist_meta, dist_band), ATTN_BLOCK).astype(F32)
    rr = lax.broadcasted_iota(jnp.int32, (R, 1), 0)
    head = (rr >> 7).astype(F32) + float(ATTN_GROUPS * j + 1)
    slope = jnp.exp(head * (-0.5 * math.log(2.0)))
    return jnp.where(valid, slope * dist, -NEG)


def _attn_tables(n, refresh, first, sinks_ref, bias_ref, sink_ref):
    @pl.when(first)
    def _():
        for j in range(ATTN_KV_HEADS):
            sink_ref[pl.ds(j * ATTN_GROUPS * ATTN_BLOCK, ATTN_GROUPS * ATTN_BLOCK), :] = _sink_col(sinks_ref, j)

    @pl.when(refresh)
    def _():
        for j in range(ATTN_KV_HEADS):
            bias_ref[j] = _attn_bias(n, j)


def _attn_table_scratch():
    rows = ATTN_GROUPS * ATTN_BLOCK
    return [pltpu.VMEM((ATTN_KV_HEADS, rows, 2 * ATTN_BLOCK + N_META), F32), pltpu.VMEM((ATTN_KV_HEADS * rows, 1), F32)]


def _attn_group(q, k, v, sinkcol, qnw, knw, bias):
    qn = _rms(q, qnw)
    kn = _rms(k, knw)
    s = _bdot(qn, kn, NT) * (ATTN_HD ** -0.5) - bias
    m = lax.stop_gradient(jnp.maximum(jnp.max(s, axis=1, keepdims=True), sinkcol))
    e = jnp.exp(s - m)
    denom = jnp.sum(e, axis=1, keepdims=True) + jnp.exp(sinkcol - m)
    p = e / denom
    return _bdot(p, v)


def _sink_col(sinks_ref, j):
    rr = lax.broadcasted_iota(jnp.int32, (ATTN_GROUPS * ATTN_BLOCK, 1), 0) >> 7
    col = jnp.zeros((ATTN_GROUPS * ATTN_BLOCK, 1), F32)
    for hl in range(ATTN_GROUPS):
        col = jnp.where(rr == hl, sinks_ref[0, ATTN_GROUPS * j + hl], col)
    return col


def _attn_kv_tiles(kvp_ref, kvc_ref, kvm_ref, j):
    ksl = slice(ATTN_HD * j, ATTN_HD * (j + 1))
    vsl = slice(128 + ATTN_HD * j, 128 + ATTN_HD * (j + 1))
    k = jnp.concatenate([kvp_ref[:, ksl], kvc_ref[:, ksl], kvm_ref[FRONT_PAD:, ksl]], axis=0)
    v = jnp.concatenate([kvp_ref[:, vsl], kvc_ref[:, vsl], kvm_ref[FRONT_PAD:, vsl]], axis=0)
    return k, v


def _attn_core_fwd(q, kv, sinks, qnw, knw):
    T = q.shape[0]
    nb = T // ATTN_BLOCK

    def body(sinks_ref, q_ref, kvc_ref, kvp_ref, kvm_ref, qnw_ref, knw_ref, o_ref, bias_ref, sink_ref):
        n = pl.program_id(0)
        _attn_tables(n, n <= 2, n == 0, sinks_ref, bias_ref, sink_ref)
        for j in range(ATTN_KV_HEADS):
            qs = jnp.concatenate(
                [q_ref[:, ATTN_HD * h:ATTN_HD * (h + 1)] for h in range(ATTN_GROUPS * j, ATTN_GROUPS * (j + 1))], axis=0)
            k, v = _attn_kv_tiles(kvp_ref, kvc_ref, kvm_ref, j)
            sinkcol = sink_ref[pl.ds(j * ATTN_GROUPS * ATTN_BLOCK, ATTN_GROUPS * ATTN_BLOCK), :]
            o = _attn_group(qs, k, v, sinkcol, qnw_ref[...], knw_ref[...], bias_ref[j])
            for hl in range(ATTN_GROUPS):
                h = ATTN_GROUPS * j + hl
                o_ref[:, ATTN_HD * h:ATTN_HD * (h + 1)] = o[ATTN_BLOCK * hl:ATTN_BLOCK * (hl + 1), :]

    return pl.pallas_call(
        body, grid=(nb,),
        in_specs=[pl.BlockSpec(memory_space=pltpu.SMEM),
                  pl.BlockSpec((ATTN_BLOCK, 1024), lambda i: (i, 0)),
                  pl.BlockSpec((ATTN_BLOCK, 256), lambda i: (i, 0)),
                  pl.BlockSpec((ATTN_BLOCK, 256), lambda i: (jnp.maximum(i - 1, 0), 0)),
                  pl.BlockSpec((ATTN_BLOCK, 256), lambda i: (0, 0)),
                  pl.BlockSpec((1, ATTN_HD), lambda i: (0, 0)),
                  pl.BlockSpec((1, ATTN_HD), lambda i: (0, 0))],
        out_specs=pl.BlockSpec((ATTN_BLOCK, 1024), lambda i: (i, 0)),
        out_shape=jax.ShapeDtypeStruct((T, 1024), F32),
        scratch_shapes=_attn_table_scratch(),
        name="attn_core_fwd", compiler_params=_cparams(),
    )(sinks, q, kv, kv, kv, qnw, knw)


def _attn_out_fwd(o, gate, h0, w_out):
    T = o.shape[0]

    def body(o_ref, g_ref, h_ref, w_ref, h1_ref):
        og = o_ref[...] * _silu(g_ref[...])
        h1_ref[...] = h_ref[...] + _bdot(og, w_ref[...])

    return _row_call("attn_out_fwd", body, T, ROW_BLOCK, [o, gate, h0], [w_out], [(1024, F32)])[0]


def _wgrad(xn, du, cg, name):
    T, kdim = xn.shape
    cdim = du.shape[1]
    nr, nc = T // ROW_BLOCK, cdim // cg
    assert nc * cg == cdim

    def body(x_ref, du_ref, dw_ref):
        @pl.when(pl.program_id(1) == 0)
        def _():
            dw_ref[...] = jnp.zeros_like(dw_ref)
        dw_ref[...] += _bdot(x_ref[...], du_ref[...], TN)

    return pl.pallas_call(
        body, grid=(nc, nr),
        in_specs=[pl.BlockSpec((ROW_BLOCK, kdim), lambda j, i: (i, 0)),
                  pl.BlockSpec((ROW_BLOCK, cg), lambda j, i: (i, j))],
        out_specs=pl.BlockSpec((kdim, cg), lambda j, i: (0, j)),
        out_shape=jax.ShapeDtypeStruct((kdim, cdim), F32),
        name=name, compiler_params=_cparams(("arbitrary", "arbitrary")),
    )(xn, du)


def _attn_out_bwd(dh1, o, gate, w_out):
    T = o.shape[0]

    def body(dh_ref, o_ref, g_ref, w_ref, do_ref, dg_ref, dw_ref):
        @pl.when(pl.program_id(0) == 0)
        def _():
            dw_ref[...] = jnp.zeros_like(dw_ref)
        dh = dh_ref[...]
        dog = _bdot(dh, w_ref[...], NT)
        og, vjp = jax.vjp(lambda o_, g_: o_ * _silu(g_), o_ref[...], g_ref[...])
        do, dg = vjp(dog)
        do_ref[...] = do
        dg_ref[...] = dg
        dw_ref[...] += _bdot(og, dh, TN)

    return _row_call("attn_out_bwd", body, T, ROW_BLOCK, [dh1, o, gate], [w_out],
                     [(1024, F32), (1024, F32)], [((1024, 1024), F32)])


def _attn_core_bwd(do, q, kv, sinks, qnw, knw):
    T = q.shape[0]
    nb = T // ATTN_BLOCK
    rev = lambda i: nb - 1 - i

    def body(sinks_ref, do_ref, q_ref, kvc_ref, kvp_ref, kvm_ref, qnw_ref, knw_ref,
             dq_ref, dkv_ref, dsinks_ref, dqnw_ref, dknw_ref, carry_ref, meta_ref, bias_ref, sink_ref):
        step = pl.program_id(0)
        n = rev(step)
        _attn_tables(n, (step == 0) | (n <= 1), step == 0, sinks_ref, bias_ref, sink_ref)

        @pl.when(step == 0)
        def _():
            carry_ref[...] = jnp.zeros_like(carry_ref)
            meta_ref[...] = jnp.zeros_like(meta_ref)
            dsinks_ref[...] = jnp.zeros_like(dsinks_ref)
            dqnw_ref[...] = jnp.zeros_like(dqnw_ref)
            dknw_ref[...] = jnp.zeros_like(dknw_ref)

        lane16 = lax.broadcasted_iota(jnp.int32, (1, ATTN_HEADS), 1)
        dsinks = jnp.zeros((1, ATTN_HEADS), F32)
        for j in range(ATTN_KV_HEADS):
            heads = range(ATTN_GROUPS * j, ATTN_GROUPS * (j + 1))
            qs = jnp.concatenate([q_ref[:, ATTN_HD * h:ATTN_HD * (h + 1)] for h in heads], axis=0)
            dos = jnp.concatenate([do_ref[:, ATTN_HD * h:ATTN_HD * (h + 1)] for h in heads], axis=0)
            k, v = _attn_kv_tiles(kvp_ref, kvc_ref, kvm_ref, j)
            fn = functools.partial(_attn_group, bias=bias_ref[j])
            sinkcol = sink_ref[pl.ds(j * ATTN_GROUPS * ATTN_BLOCK, ATTN_GROUPS * ATTN_BLOCK), :]
            _, vjp = jax.vjp(fn, qs, k, v, sinkcol, qnw_ref[...], knw_ref[...])
            dqs, dk, dv, dsc, dqn, dkn = vjp(dos)
            dqnw_ref[...] += dqn
            dknw_ref[...] += dkn
            for hl in range(ATTN_GROUPS):
                h = ATTN_GROUPS * j + hl
                rows = slice(ATTN_BLOCK * hl, ATTN_BLOCK * (hl + 1))
                dq_ref[:, ATTN_HD * h:ATTN_HD * (h + 1)] = dqs[rows, :]
                dsinks = dsinks + jnp.where(lane16 == h, jnp.sum(dsc[rows, :]), 0.0)
            ksl = slice(ATTN_HD * j, ATTN_HD * (j + 1))
            vsl = slice(128 + ATTN_HD * j, 128 + ATTN_HD * (j + 1))
            for sl, d in ((ksl, dk), (vsl, dv)):
                dkv_ref[:, sl] = d[ATTN_BLOCK:2 * ATTN_BLOCK, :] + carry_ref[:, sl]
                carry_ref[:, sl] = d[0:ATTN_BLOCK, :]
                meta_ref[:, sl] += d[2 * ATTN_BLOCK:, :]
        dsinks_ref[...] += dsinks

        @pl.when(n == 0)
        def _():
            dkv_ref[FRONT_PAD:, :] += meta_ref[...]

    return pl.pallas_call(
        body, grid=(nb,),
        in_specs=[pl.BlockSpec(memory_space=pltpu.SMEM),
                  pl.BlockSpec((ATTN_BLOCK, 1024), lambda i: (rev(i), 0)),
                  pl.BlockSpec((ATTN_BLOCK, 1024), lambda i: (rev(i), 0)),
                  pl.BlockSpec((ATTN_BLOCK, 256), lambda i: (rev(i), 0)),
                  pl.BlockSpec((ATTN_BLOCK, 256), lambda i: (jnp.maximum(rev(i) - 1, 0), 0)),
                  pl.BlockSpec((ATTN_BLOCK, 256), lambda i: (0, 0)),
                  pl.BlockSpec((1, ATTN_HD), lambda i: (0, 0)),
                  pl.BlockSpec((1, ATTN_HD), lambda i: (0, 0))],
        out_specs=[pl.BlockSpec((ATTN_BLOCK, 1024), lambda i: (rev(i), 0)),
                   pl.BlockSpec((ATTN_BLOCK, 256), lambda i: (rev(i), 0)),
                   pl.BlockSpec((1, ATTN_HEADS), lambda i: (0, 0)),
                   pl.BlockSpec((1, ATTN_HD), lambda i: (0, 0)),
                   pl.BlockSpec((1, ATTN_HD), lambda i: (0, 0))],
        out_shape=[jax.ShapeDtypeStruct((T, 1024), F32), jax.ShapeDtypeStruct((T, 256), F32),
                   jax.ShapeDtypeStruct((1, ATTN_HEADS), F32), jax.ShapeDtypeStruct((1, ATTN_HD), F32),
                   jax.ShapeDtypeStruct((1, ATTN_HD), F32)],
        scratch_shapes=[pltpu.VMEM((ATTN_BLOCK, 256), F32), pltpu.VMEM((N_META, 256), F32)] + _attn_table_scratch(),
        name="attn_core_bwd", compiler_params=_cparams(),
    )(sinks, do, q, kv, kv, kv, qnw, knw)


def _attn_in_bwd(dq, dkv, dgate, h0, dh1, norm_w, w_in):
    T = h0.shape[0]

    def body(dq_ref, dkv_ref, dg_ref, h_ref, dh1_ref, nw_ref, w_ref, dh0_ref, dnw_ref):
        @pl.when(pl.program_id(0) == 0)
        def _():
            dnw_ref[...] = jnp.zeros_like(dnw_ref)
        dxn = (_bdot(dq_ref[...], w_ref[:, 0:1024], NT) + _bdot(dkv_ref[...], w_ref[:, 1024:1280], NT)
               + _bdot(dg_ref[...], w_ref[:, 1280:2304], NT))
        _, vjp = jax.vjp(_rms, h_ref[...], nw_ref[...])
        dh, dnw = vjp(dxn)
        dh0_ref[...] = dh1_ref[...] + dh
        dnw_ref[...] += dnw

    return _row_call("attn_in_bwd", body, T, ROW_BLOCK, [dq, dkv, dgate, h0, dh1], [norm_w, w_in],
                     [(1024, F32)], [((1, 1024), F32)])


def _dn_in_fwd(h1, norm_w, w_in):
    T = h1.shape[0]

    def body(h_ref, nw_ref, w_ref, xn_ref, qkv_ref, z_ref, ba_ref):
        xn = _rms(h_ref[...], nw_ref[...]).astype(BF16)
        xn_ref[...] = xn
        qkv_ref[...] = jnp.dot(xn, w_ref[:, 0:4096], preferred_element_type=F32)
        z_ref[...] = jnp.dot(xn, w_ref[:, 4096:6144], preferred_element_type=F32)
        ba_ref[...] = jnp.dot(xn, w_ref[:, 6144:6176], preferred_element_type=F32)

    return _row_call("dn_in_fwd", body, T, ROW_BLOCK, [h1], [norm_w, w_in],
                     [(1024, BF16), (4096, F32), (2048, F32), (32, F32)])


def _shift_down(cur, prev8, s):
    i8 = lax.broadcasted_iota(jnp.int32, (8, cur.shape[1]), 0)
    r = pltpu.roll(cur, s, 0)
    head = jnp.where(i8 < s, pltpu.roll(prev8, s, 0), r[0:8])
    return jnp.concatenate([head, r[8:]], axis=0)


def _shift_up(cur, next8, s):
    n = cur.shape[0]
    i8 = lax.broadcasted_iota(jnp.int32, (8, cur.shape[1]), 0)
    r = pltpu.roll(cur, n - s, 0)
    tail = jnp.where(i8 >= 8 - s, pltpu.roll(next8, 8 - s, 0), r[n - 8:])
    return jnp.concatenate([r[:n - 8], tail], axis=0)


def _conv_tile(cur, prev8, w):
    out = w[3:4, :] * cur
    for s in range(1, DN_CONV_K):
        out = out + w[3 - s:4 - s, :] * _shift_down(cur, prev8, s)
    return out


def _l2n(a, scale):
    return a * (lax.rsqrt(jnp.sum(a * a, axis=-1, keepdims=True) + NORM_EPS) * scale)


def _dn_post_tile(c, t):
    a = _silu(c)
    if t < DN_K_HEADS:
        return _l2n(a, DN_HD ** -0.5)
    if t < 2 * DN_K_HEADS:
        return _l2n(a, 1.0)
    return a


def _dn_beta_g(ba, a_log, dt_bias, live):
    beta = jax.nn.sigmoid(ba[:, 0:DN_V_HEADS]) * live
    g = -jnp.exp(a_log) * _softplus(ba[:, DN_V_HEADS:] + dt_bias) * live
    return beta, g


def _live_rows(i, rb):
    rows = i * rb + lax.broadcasted_iota(jnp.int32, (rb, 1), 0)
    return (rows >= FRONT_PAD).astype(F32)


def _halo_spec_args(x, rb):
    per = rb // 8
    return (x, 8, lambda i: (jnp.maximum(i * per - 1, 0), 0))


def _dn_conv_fwd(qkv, ba, conv_w, a_log, dt_bias):
    T = qkv.shape[0]

    def body(x_ref, ba_ref, halo_ref, cw_ref, al_ref, dtb_ref, q_ref, k_ref, v_ref, bg_ref):
        i = pl.program_id(0)
        first = (i > 0).astype(F32)
        for t in range(DN_CONV_W // 128):
            cols = slice(128 * t, 128 * (t + 1))
            c = _conv_tile(x_ref[:, cols], halo_ref[:, cols] * first, cw_ref[:, cols])
            out = _dn_post_tile(c, t)
            if t < DN_K_HEADS:
                q_ref[:, cols] = out
            elif t < 2 * DN_K_HEADS:
                k_ref[:, 128 * (t - 8):128 * (t - 7)] = out
            else:
                v_ref[:, 128 * (t - 16):128 * (t - 15)] = out
        beta, g = _dn_beta_g(ba_ref[...], al_ref[...], dtb_ref[...], _live_rows(i, ROW_BLOCK))
        bg_ref[:, 0:DN_V_HEADS] = beta
        bg_ref[:, DN_V_HEADS:] = g

    return _row_call("dn_conv_fwd", body, T, ROW_BLOCK, [qkv, ba], [conv_w, a_log, dt_bias],
                     [(1024, F32), (1024, F32), (2048, F32), (32, F32)], halos=[_halo_spec_args(qkv, ROW_BLOCK)])


def _chunk_masks():
    r = lax.broadcasted_iota(jnp.int32, (DN_CHUNK, DN_CHUNK), 0)
    c = lax.broadcasted_iota(jnp.int32, (DN_CHUNK, DN_CHUNK), 1)
    return r >= c, r > c, r == c, r <= c


def _tri_inv(x, eye):
    C = DN_CHUNK
    n = range(len(x))
    ainv = [jnp.where(eye, 1.0, 0.0) + x[h] for h in n]
    p = [_bdot(x[h], x[h]) for h in n]
    for _ in range(4):
        r = [_bdot(jnp.concatenate([p[h], ainv[h]], axis=0), p[h]) for h in n]
        ainv = [ainv[h] + r[h][C:] for h in n]
        p = [r[h][:C] for h in n]
    return [ainv[h] + _bdot(ainv[h], p[h]) for h in n]


@jax.custom_vjp
def _tri_inv_known(x, a):
    return a


def _tri_inv_known_fwd(x, a):
    return a, a


def _tri_inv_known_bwd(a, da):
    return [_bdot(_bdot(a[h], da[h], TN), a[h], NT) for h in range(len(a))], [jnp.zeros_like(t) for t in a]


_tri_inv_known.defvjp(_tri_inv_known_fwd, _tri_inv_known_bwd)


def _dn_chunk_step(S, q, k, v, beta, g, masks, known_inv=None, with_inv=False):
    causal, strict, eye, upper = masks
    C, W = DN_CHUNK, DN_HD
    heads = range(len(v))
    k_t = [k[j].T for j in range(len(k))]
    qk_kk = [_bdot(jnp.concatenate([q[j], k[j]], axis=0), k_t[j]) for j in range(len(q))]
    g_b = [jnp.broadcast_to(g[h], (C, C)) for h in heads]
    beta_b = [jnp.broadcast_to(beta[h], (C, W)) for h in heads]
    g_row = [jnp.sum(jnp.where(eye, g_b[h], 0.0), axis=0, keepdims=True) for h in heads]
    gc_col = [jnp.sum(jnp.where(causal, g_row[h], 0.0), axis=1, keepdims=True) for h in heads]
    gc_row = [jnp.sum(jnp.where(upper, g_b[h], 0.0), axis=0, keepdims=True) for h in heads]
    g_last = [jnp.sum(g_row[h], axis=1, keepdims=True) for h in heads]
    gc_b = [jnp.broadcast_to(gc_col[h], (C, W)) for h in heads]
    decay = [jnp.exp(jnp.where(causal, gc_b[h][:, :C] - gc_row[h], NEG)) for h in heads]
    eg_b = [jnp.exp(gc_b[h]) for h in heads]
    x = [jnp.where(strict, qk_kk[h // 2][C:] * beta_b[h][:, :C] * decay[h], 0.0) * -1.0 for h in heads]
    ainv = _tri_inv(x, eye) if known_inv is None else _tri_inv_known(x, known_inv)
    uw = [_bdot(ainv[h], jnp.concatenate([v[h] * beta_b[h], k[h // 2] * (beta_b[h] * eg_b[h])], axis=1)) for h in heads]
    ws_qs = [_bdot(jnp.concatenate([uw[h][:, W:], q[h // 2] * eg_b[h]], axis=0), S[h]) for h in heads]
    v_new = [uw[h][:, :W] - ws_qs[h][:C] for h in heads]
    o = [ws_qs[h][C:] + _bdot(qk_kk[h // 2][:C] * decay[h], v_new[h]) for h in heads]
    s_new = [S[h] * jnp.exp(g_last[h]) + _bdot(k_t[h // 2] * jnp.exp(g_last[h] - gc_row[h]), v_new[h]) for h in heads]
    return (s_new, o, ainv) if with_inv else (s_new, o)


def _dn_chunk_tiles(q_ref, k_ref, v_ref, bg_ref):
    q = [q_ref[:, 128 * j:128 * (j + 1)] for j in range(DN_K_HEADS)]
    k = [k_ref[:, 128 * j:128 * (j + 1)] for j in range(DN_K_HEADS)]
    v = [v_ref[:, 128 * h:128 * (h + 1)] for h in range(DN_V_HEADS)]
    beta = [bg_ref[:, h:h + 1] for h in range(DN_V_HEADS)]
    g = [bg_ref[:, DN_V_HEADS + h:DN_V_HEADS + h + 1] for h in range(DN_V_HEADS)]
    return q, k, v, beta, g


def _dn_scan_fwd(qn, kn, v, bg):
    T = qn.shape[0]
    nc = T // DN_CHUNK

    def body(q_ref, k_ref, v_ref, bg_ref, o_ref, ssave_ref, inv_ref, s_ref):
        @pl.when(pl.program_id(0) == 0)
        def _():
            s_ref[...] = jnp.zeros_like(s_ref)
        s_old = [s_ref[h] for h in range(DN_V_HEADS)]
        for h in range(DN_V_HEADS):
            ssave_ref[0, h] = s_old[h]
        s_new, o, ainv = _dn_chunk_step(s_old, *_dn_chunk_tiles(q_ref, k_ref, v_ref, bg_ref), _chunk_masks(),
                                        with_inv=True)
        for h in range(DN_V_HEADS):
            o_ref[:, 128 * h:128 * (h + 1)] = o[h]
            inv_ref[0, h] = ainv[h].astype(BF16)
            s_ref[h] = s_new[h]

    return pl.pallas_call(
        body, grid=(nc,),
        in_specs=[pl.BlockSpec((DN_CHUNK, 1024), lambda i: (i, 0)),
                  pl.BlockSpec((DN_CHUNK, 1024), lambda i: (i, 0)),
                  pl.BlockSpec((DN_CHUNK, 2048), lambda i: (i, 0)),
                  pl.BlockSpec((DN_CHUNK, 32), lambda i: (i, 0))],
        out_specs=[pl.BlockSpec((DN_CHUNK, 2048), lambda i: (i, 0)),
                   pl.BlockSpec((1, DN_V_HEADS, DN_HD, DN_HD), lambda i: (i, 0, 0, 0)),
                   pl.BlockSpec((1, DN_V_HEADS, DN_CHUNK, DN_CHUNK), lambda i: (i, 0, 0, 0))],
        out_shape=[jax.ShapeDtypeStruct((T, 2048), F32),
                   jax.ShapeDtypeStruct((nc, DN_V_HEADS, DN_HD, DN_HD), F32),
                   jax.ShapeDtypeStruct((nc, DN_V_HEADS, DN_CHUNK, DN_CHUNK), BF16)],
        scratch_shapes=[pltpu.VMEM((DN_V_HEADS, DN_HD, DN_HD), F32)],
        name="dn_scan_fwd", compiler_params=_cparams(),
    )(qn, kn, v, bg)


def _dn_gate_tile(o, z, onw):
    return _rms(o, onw) * _silu(z)


def _dn_out_fwd(o, z, h1, target, w_out, onw):
    T = o.shape[0]

    def body(o_ref, z_ref, h_ref, t_ref, w_ref, onw_ref, dy_ref, og_ref, loss_ref):
        i = pl.program_id(0)

        @pl.when(i == 0)
        def _():
            loss_ref[...] = jnp.zeros_like(loss_ref)
        for h in range(DN_V_HEADS):
            cols = slice(128 * h, 128 * (h + 1))
            og_ref[:, cols] = _dn_gate_tile(o_ref[:, cols], z_ref[:, cols], onw_ref[...]).astype(BF16)
        y = h_ref[...] + jnp.dot(og_ref[...], w_ref[...], preferred_element_type=F32)
        rows = i * ROW_BLOCK + lax.broadcasted_iota(jnp.int32, (ROW_BLOCK, 1), 0)
        diff = jnp.where(rows >= FRONT_PAD + N_META, y - t_ref[...], 0.0)
        dy_ref[...] = diff * (1.0 / D_MODEL)
        loss_ref[...] += jnp.sum(diff * diff) * (0.5 / D_MODEL)

    return _row_call("dn_out_fwd", body, T, ROW_BLOCK, [o, z, h1, target], [w_out, onw],
                     [(1024, F32), (2048, BF16)], [((1, 128), F32)])


def _dn_out_bwd(dy, o, z, w_out, onw):
    T = o.shape[0]

    def body(dy_ref, o_ref, z_ref, w_ref, onw_ref, do_ref, dz_ref, donw_ref):
        @pl.when(pl.program_id(0) == 0)
        def _():
            donw_ref[...] = jnp.zeros_like(donw_ref)
        dy = dy_ref[...].astype(BF16)
        donw = jnp.zeros((1, DN_HD), F32)
        for h in range(DN_V_HEADS):
            cols = slice(128 * h, 128 * (h + 1))
            dog = lax.dot_general(dy, w_ref[cols, :], NT, preferred_element_type=F32)
            _, vjp = jax.vjp(_dn_gate_tile, o_ref[:, cols], z_ref[:, cols], onw_ref[...])
            do, dz, dn = vjp(dog)
            do_ref[:, cols] = do
            dz_ref[:, cols] = dz
            donw = donw + dn
        donw_ref[...] += donw

    return _row_call("dn_out_bwd", body, T, ROW_BLOCK, [dy, o, z], [w_out, onw],
                     [(2048, F32), (2048, F32)], [((1, DN_HD), F32)])


def _dn_scan_bwd(do, qn, kn, v, bg, ssave, inv):
    T = qn.shape[0]
    nc = T // DN_CHUNK
    rev = lambda i: nc - 1 - i

    def body(do_ref, q_ref, k_ref, v_ref, bg_ref, ss_ref, inv_ref, dq_ref, dk_ref, dv_ref, dbg_ref, ds_ref):
        @pl.when(pl.program_id(0) == 0)
        def _():
            ds_ref[...] = jnp.zeros_like(ds_ref)
        lane32 = lax.broadcasted_iota(jnp.int32, (1, 2 * DN_V_HEADS), 1)
        masks = _chunk_masks()
        q, k, v, beta, g = _dn_chunk_tiles(q_ref, k_ref, v_ref, bg_ref)
        dbg = jnp.zeros((DN_CHUNK, 2 * DN_V_HEADS), F32)
        for first in range(0, DN_V_HEADS, SCAN_BWD_GROUP):
            heads = range(first, first + SCAN_BWD_GROUP)
            pairs = slice(first // 2, (first + SCAN_BWD_GROUP) // 2)
            hs = slice(first, first + SCAN_BWD_GROUP)
            fn = functools.partial(_dn_chunk_step, masks=masks, known_inv=[inv_ref[0, h].astype(F32) for h in heads])
            _, vjp = jax.vjp(fn, [ss_ref[0, h] for h in heads], q[pairs], k[pairs], v[hs], beta[hs], g[hs])
            ds, dq, dk, dv, dbeta, dg = vjp(([ds_ref[h] for h in heads], [do_ref[:, 128 * h:128 * (h + 1)] for h in heads]))
            for i, h in enumerate(heads):
                ds_ref[h] = ds[i]
                dv_ref[:, 128 * h:128 * (h + 1)] = dv[i]
                dbg = dbg + jnp.where(lane32 == h, dbeta[i], 0.0) + jnp.where(lane32 == DN_V_HEADS + h, dg[i], 0.0)
            for i, j in enumerate(range(first // 2, (first + SCAN_BWD_GROUP) // 2)):
                dq_ref[:, 128 * j:128 * (j + 1)] = dq[i]
                dk_ref[:, 128 * j:128 * (j + 1)] = dk[i]
        dbg_ref[...] = dbg

    return pl.pallas_call(
        body, grid=(nc,),
        in_specs=[pl.BlockSpec((DN_CHUNK, 2048), lambda i: (rev(i), 0)),
                  pl.BlockSpec((DN_CHUNK, 1024), lambda i: (rev(i), 0)),
                  pl.BlockSpec((DN_CHUNK, 1024), lambda i: (rev(i), 0)),
                  pl.BlockSpec((DN_CHUNK, 2048), lambda i: (rev(i), 0)),
                  pl.BlockSpec((DN_CHUNK, 32), lambda i: (rev(i), 0)),
                  pl.BlockSpec((1, DN_V_HEADS, DN_HD, DN_HD), lambda i: (rev(i), 0, 0, 0)),
                  pl.BlockSpec((1, DN_V_HEADS, DN_CHUNK, DN_CHUNK), lambda i: (rev(i), 0, 0, 0))],
        out_specs=[pl.BlockSpec((DN_CHUNK, 1024), lambda i: (rev(i), 0)),
                   pl.BlockSpec((DN_CHUNK, 1024), lambda i: (rev(i), 0)),
                   pl.BlockSpec((DN_CHUNK, 2048), lambda i: (rev(i), 0)),
                   pl.BlockSpec((DN_CHUNK, 32), lambda i: (rev(i), 0))],
        out_shape=[jax.ShapeDtypeStruct((T, 1024), F32), jax.ShapeDtypeStruct((T, 1024), F32),
                   jax.ShapeDtypeStruct((T, 2048), F32), jax.ShapeDtypeStruct((T, 32), F32)],
        scratch_shapes=[pltpu.VMEM((DN_V_HEADS, DN_HD, DN_HD), F32)],
        name="dn_scan_bwd", compiler_params=_cparams(),
    )(do, qn, kn, v, bg, ssave, inv)


def _dn_conv_bwd(dqn, dkn, dv, dbg, qkv, ba, conv_w, a_log, dt_bias):
    T = qkv.shape[0]
    nr = T // ROW_BLOCK

    def body(dq_ref, dk_ref, dv_ref, dbg_ref, x_ref, ba_ref, halo_ref, cw_ref, al_ref, dtb_ref,
             dx_ref, dba_ref, dcw_ref, dal_ref, ddtb_ref, carry_ref):
        step = pl.program_id(0)
        i = nr - 1 - step

        @pl.when(step == 0)
        def _():
            carry_ref[...] = jnp.zeros_like(carry_ref)
            dcw_ref[...] = jnp.zeros_like(dcw_ref)
            dal_ref[...] = jnp.zeros_like(dal_ref)
            ddtb_ref[...] = jnp.zeros_like(ddtb_ref)
        first = (i > 0).astype(F32)
        for t in range(DN_CONV_W // 128):
            cols = slice(128 * t, 128 * (t + 1))
            cur, prev8, w = x_ref[:, cols], halo_ref[:, cols] * first, cw_ref[:, cols]
            c = _conv_tile(cur, prev8, w)
            if t < DN_K_HEADS:
                dout = dq_ref[:, cols]
            elif t < 2 * DN_K_HEADS:
                dout = dk_ref[:, 128 * (t - 8):128 * (t - 7)]
            else:
                dout = dv_ref[:, 128 * (t - 16):128 * (t - 15)]
            _, vjp = jax.vjp(functools.partial(_dn_post_tile, t=t), c)
            (dc,) = vjp(dout)
            nxt = carry_ref[:, cols]
            dx = w[3:4, :] * dc
            dcw_ref[3:4, cols] += jnp.sum(dc * cur, axis=0, keepdims=True)
            for s in range(1, DN_CONV_K):
                dx = dx + w[3 - s:4 - s, :] * _shift_up(dc, nxt, s)
                dcw_ref[3 - s:4 - s, cols] += jnp.sum(dc * _shift_down(cur, prev8, s), axis=0, keepdims=True)
            dx_ref[:, cols] = dx
            carry_ref[:, cols] = dc[0:8, :]
        fn = functools.partial(_dn_beta_g, live=_live_rows(i, ROW_BLOCK))
        _, vjp = jax.vjp(fn, ba_ref[...], al_ref[...], dtb_ref[...])
        dba, dal, ddtb = vjp((dbg_ref[:, 0:DN_V_HEADS], dbg_ref[:, DN_V_HEADS:]))
        dba_ref[...] = dba
        dal_ref[...] += dal
        ddtb_ref[...] += ddtb

    per = ROW_BLOCK // 8
    halo = (qkv, 8, lambda s: (jnp.maximum((nr - 1 - s) * per - 1, 0), 0))
    return _row_call("dn_conv_bwd", body, T, ROW_BLOCK, [dqn, dkn, dv, dbg, qkv, ba], [conv_w, a_log, dt_bias],
                     [(4096, F32), (32, F32)], [((DN_CONV_K, 4096), F32), ((1, DN_V_HEADS), F32), ((1, DN_V_HEADS), F32)],
                     reverse=True, scratch=[pltpu.VMEM((8, 4096), F32)], halos=[halo])


def _dn_in_bwd(dqkv, dz, dba, h1, dy, norm_w, w_in):
    T = h1.shape[0]

    def body(dqkv_ref, dz_ref, dba_ref, h_ref, dy_ref, nw_ref, w_ref, dh_ref, dnw_ref):
        @pl.when(pl.program_id(0) == 0)
        def _():
            dnw_ref[...] = jnp.zeros_like(dnw_ref)
        dxn = (_bdot(dqkv_ref[...], w_ref[:, 0:4096], NT) + _bdot(dz_ref[...], w_ref[:, 4096:6144], NT)
               + _bdot(dba_ref[...], w_ref[:, 6144:6176], NT))
        _, vjp = jax.vjp(_rms, h_ref[...], nw_ref[...])
        dh, dnw = vjp(dxn)
        dh_ref[...] = (dy_ref[...] + dh) * _live_rows(pl.program_id(0), ROW_BLOCK)
        dnw_ref[...] += dnw

    return _row_call("dn_in_bwd", body, T, ROW_BLOCK, [dqkv, dz, dba, h1, dy], [norm_w, w_in],
                     [(1024, F32)], [((1, 1024), F32)])


def _exchange(parts, scatter, name):
    n = len(parts)
    out_shape = [jax.ShapeDtypeStruct(p.shape if sc else (N_DEV,) + p.shape, p.dtype) for p, sc in zip(parts, scatter)]

    def body(*refs):
        ins, outs = refs[:n], refs[n:2 * n]
        send_sems, recv_sems, local_sems = refs[2 * n:]
        x, y, c = lax.axis_index("x"), lax.axis_index("y"), lax.axis_index("c")
        me = 4 * x + 2 * y + c
        peers = []
        for k in range(1, N_DEV):
            px = 1 - x if k & 4 else x
            py = 1 - y if k & 2 else y
            pc = 1 - c if k & 1 else c
            peers.append(((px, py, pc), 4 * px + 2 * py + pc))

        def src(a, idx):
            return ins[a].at[idx] if scatter[a] else ins[a]

        local = [pltpu.make_async_copy(src(a, me), outs[a].at[me], local_sems.at[a]) for a in range(n)]
        for cp in local:
            cp.start()
        for a in range(n):
            for k, (dev, idx) in enumerate(peers):
                pltpu.make_async_remote_copy(
                    src_ref=src(a, idx), dst_ref=outs[a].at[me], send_sem=send_sems.at[a, k], recv_sem=recv_sems.at[a, k],
                    device_id=dev, device_id_type=pl.DeviceIdType.MESH).start()
        for a in range(n):
            for k, (dev, idx) in enumerate(peers):
                pltpu.make_async_remote_copy(
                    src_ref=src(a, idx), dst_ref=outs[a].at[idx], send_sem=send_sems.at[a, k], recv_sem=recv_sems.at[a, k],
                    device_id=dev, device_id_type=pl.DeviceIdType.MESH).wait()
        for cp in local:
            cp.wait()

    hbm = pl.BlockSpec(memory_space=pltpu.HBM)
    return pl.pallas_call(
        body, out_shape=out_shape, in_specs=[hbm] * n, out_specs=[hbm] * n,
        scratch_shapes=[pltpu.SemaphoreType.DMA((n, N_DEV - 1)), pltpu.SemaphoreType.DMA((n, N_DEV - 1)),
                        pltpu.SemaphoreType.DMA((n,))],
        name=name,
    )(*parts)


def _adam_rows(rows):
    for rb in (128, 64, 40, 16, 8):
        if rows % rb == 0:
            return rb
    return rows


def _adamw(stack, w, m, v, name):
    R, C = w.shape
    rb = _adam_rows(R)

    def body(s_ref, w_ref, m_ref, v_ref, g_ref, d_ref, nm_ref, nv_ref):
        g = s_ref[0].astype(F32)
        for s in range(1, N_DEV):
            g = g + s_ref[s].astype(F32)
        nm = ADAM_B1 * m_ref[...] + (1.0 - ADAM_B1) * g
        nv = ADAM_B2 * v_ref[...] + (1.0 - ADAM_B2) * (g * g)
        m_hat = nm / (1.0 - ADAM_B1 ** ADAM_STEP)
        v_hat = nv / (1.0 - ADAM_B2 ** ADAM_STEP)
        g_ref[...] = g
        d_ref[...] = -ADAM_LR * (m_hat / (jnp.sqrt(v_hat) + ADAM_EPS) + ADAM_WD * w_ref[...])
        nm_ref[...] = nm
        nv_ref[...] = nv

    blk = pl.BlockSpec((rb, C), lambda i: (i, 0))
    return pl.pallas_call(
        body, grid=(R // rb,),
        in_specs=[pl.BlockSpec((N_DEV, rb, C), lambda i: (0, i, 0)), blk, blk, blk],
        out_specs=[blk] * 4, out_shape=[jax.ShapeDtypeStruct((R, C), F32)] * 4,
        name=name, compiler_params=_cparams(),
    )(stack, w, m, v)


def _pad_rows8(a):
    return jnp.concatenate([a, jnp.zeros((8 - a.shape[0], a.shape[1]), a.dtype)], axis=0) if a.shape[0] < 8 else a


def _pack_small(norm_w, qnw, knw, sinks, a_log, dt_bias, onw, extra):
    z = lambda n: jnp.zeros((1, n), F32)
    row = jnp.concatenate([norm_w, qnw, knw, sinks, a_log, dt_bias, z(80), onw, extra, z(512)], axis=1)
    return row.reshape(16, 128)


def _unpack_small(p):
    row = p.reshape(1, 2048)
    cut = lambda a, n: row[:, a:a + n]
    return (cut(0, 1024), cut(1024, 64), cut(1088, 64), cut(1152, 16), cut(1168, 16), cut(1184, 16), cut(1280, 128),
            cut(1408, 128))


def _pack_rows(w_in_a, w_in_d, w_out_a, w_out_d, meta, conv, dn_norm):
    a = jnp.concatenate([w_in_a, w_in_d], axis=1)
    b = jnp.concatenate([w_out_a, w_out_d], axis=0)
    c = jnp.concatenate([meta, conv.reshape(16, 128), _pad_rows8(dn_norm)], axis=0)
    return a, b, c


def _unpack_rows(a, b, c):
    return (a[:, :288], a[:, 288:], b[:128], b[128:], c[:16], c[16:32].reshape(4, 512), c[32:33])


def _local_step(h0, target, w):
    xn0, q, kv, gate = _attn_in_fwd(h0, w["attn_norm_w"], w["attn_w_in"])
    o = _attn_core_fwd(q, kv, w["attn_sinks"], w["attn_q_norm_w"], w["attn_k_norm_w"])
    h1 = _attn_out_fwd(o, gate, h0, w["attn_w_out"])
    xn1, qkv, z, ba = _dn_in_fwd(h1, w["dn_norm_w"], w["dn_w_in"])
    qn, kn, v, bg = _dn_conv_fwd(qkv, ba, w["dn_conv_w"], w["dn_a_log"], w["dn_dt_bias"])
    o_dn, ssave, inv = _dn_scan_fwd(qn, kn, v, bg)
    dy, og_dn, loss = _dn_out_fwd(o_dn, z, h1, target, w["dn_w_out"], w["dn_o_norm_w"])

    g = {}
    do_dn, dz, g["dn_o_norm_w"] = _dn_out_bwd(dy, o_dn, z, w["dn_w_out"], w["dn_o_norm_w"])
    g["dn_w_out"] = _wgrad(og_dn, dy, 1024, "wgrad_dn_out")
    dqn, dkn, dv, dbg = _dn_scan_bwd(do_dn, qn, kn, v, bg, ssave, inv)
    dqkv, dba, g["dn_conv_w"], g["dn_a_log"], g["dn_dt_bias"] = _dn_conv_bwd(
        dqn, dkn, dv, dbg, qkv, ba, w["dn_conv_w"], w["dn_a_log"], w["dn_dt_bias"])
    dh1, g["dn_norm_w"] = _dn_in_bwd(dqkv, dz, dba, h1, dy, w["dn_norm_w"], w["dn_w_in"])
    g["dn_w_in"] = jnp.concatenate([_wgrad(xn1, dqkv, 1024, "wgrad_dn_qkv"), _wgrad(xn1, dz, 1024, "wgrad_dn_z"),
                                    _wgrad(xn1, dba, 32, "wgrad_dn_ba")], axis=1)
    do, dgate, g["attn_w_out"] = _attn_out_bwd(dh1, o, gate, w["attn_w_out"])
    dq, dkv, g["attn_sinks"], g["attn_q_norm_w"], g["attn_k_norm_w"] = _attn_core_bwd(
        do, q, kv, w["attn_sinks"], w["attn_q_norm_w"], w["attn_k_norm_w"])
    dh0, g["attn_norm_w"] = _attn_in_bwd(dq, dkv, dgate, h0, dh1, w["attn_norm_w"], w["attn_w_in"])
    g["attn_w_in"] = jnp.concatenate([_wgrad(xn0, dq, 1024, "wgrad_attn_q"), _wgrad(xn0, dkv, 256, "wgrad_attn_kv"),
                                      _wgrad(xn0, dgate, 1024, "wgrad_attn_gate")], axis=1)
    return loss, dh0, g


WEIGHTS = ['meta_tokens', 'attn_norm_w', 'attn_w_in', 'attn_q_norm_w', 'attn_k_norm_w', 'attn_sinks', 'attn_w_out',
           'dn_norm_w', 'dn_w_in', 'dn_conv_w', 'dn_a_log', 'dn_dt_bias', 'dn_o_norm_w', 'dn_w_out']
SMALL = ['attn_norm_w', 'attn_q_norm_w', 'attn_k_norm_w', 'attn_sinks', 'dn_a_log', 'dn_dt_bias', 'dn_o_norm_w']


def kernel(x, meta_tokens, attn_norm_w, attn_w_in, attn_q_norm_w, attn_k_norm_w, attn_sinks, attn_w_out, dn_norm_w, dn_w_in, dn_conv_w, dn_a_log, dn_dt_bias, dn_o_norm_w, dn_w_out, loss_target, m_meta_tokens, m_attn_norm_w, m_attn_w_in, m_attn_q_norm_w, m_attn_k_norm_w, m_attn_sinks, m_attn_w_out, m_dn_norm_w, m_dn_w_in, m_dn_conv_w, m_dn_a_log, m_dn_dt_bias, m_dn_o_norm_w, m_dn_w_out, v_meta_tokens, v_attn_norm_w, v_attn_w_in, v_attn_q_norm_w, v_attn_k_norm_w, v_attn_sinks, v_attn_w_out, v_dn_norm_w, v_dn_w_in, v_dn_conv_w, v_dn_a_log, v_dn_dt_bias, v_dn_o_norm_w, v_dn_w_out):
    shard = dict(meta_tokens=meta_tokens, attn_norm_w=attn_norm_w, attn_w_in=attn_w_in[0], attn_q_norm_w=attn_q_norm_w,
                 attn_k_norm_w=attn_k_norm_w, attn_sinks=attn_sinks, attn_w_out=attn_w_out[0], dn_norm_w=dn_norm_w,
                 dn_w_in=dn_w_in[0], dn_conv_w=dn_conv_w[0], dn_a_log=dn_a_log, dn_dt_bias=dn_dt_bias,
                 dn_o_norm_w=dn_o_norm_w, dn_w_out=dn_w_out[0])
    mom_m = dict(meta_tokens=m_meta_tokens, attn_norm_w=m_attn_norm_w, attn_w_in=m_attn_w_in[0], attn_q_norm_w=m_attn_q_norm_w,
                 attn_k_norm_w=m_attn_k_norm_w, attn_sinks=m_attn_sinks, attn_w_out=m_attn_w_out[0], dn_norm_w=m_dn_norm_w,
                 dn_w_in=m_dn_w_in[0], dn_conv_w=m_dn_conv_w[0], dn_a_log=m_dn_a_log, dn_dt_bias=m_dn_dt_bias,
                 dn_o_norm_w=m_dn_o_norm_w, dn_w_out=m_dn_w_out[0])
    mom_v = dict(meta_tokens=v_meta_tokens, attn_norm_w=v_attn_norm_w, attn_w_in=v_attn_w_in[0], attn_q_norm_w=v_attn_q_norm_w,
                 attn_k_norm_w=v_attn_k_norm_w, attn_sinks=v_attn_sinks, attn_w_out=v_attn_w_out[0], dn_norm_w=v_dn_norm_w,
                 dn_w_in=v_dn_w_in[0], dn_conv_w=v_dn_conv_w[0], dn_a_log=v_dn_a_log, dn_dt_bias=v_dn_dt_bias,
                 dn_o_norm_w=v_dn_o_norm_w, dn_w_out=v_dn_w_out[0])

    def rows_of(d):
        return _pack_rows(d["attn_w_in"], d["dn_w_in"], d["attn_w_out"], d["dn_w_out"], d["meta_tokens"], d["dn_conv_w"],
                          d["dn_norm_w"])

    def small_of(d, extra):
        return _pack_small(*[d[k] for k in SMALL], extra)

    wa, wb, wc = rows_of(shard)
    ga, gb, gc = _exchange([wa.astype(BF16), wb.astype(BF16), wc], [False, False, False], "gather_weights")
    full = {k: shard[k] for k in SMALL}
    full["attn_w_in"] = ga[:, :, :288].transpose(1, 0, 2).reshape(1024, 2304)
    full["dn_w_in"] = ga[:, :, 288:].transpose(1, 0, 2).reshape(1024, 6176)
    full["attn_w_out"] = gb[:, :128].reshape(1024, 1024)
    full["dn_w_out"] = gb[:, 128:].reshape(2048, 1024)
    meta_full = gc[:, :16].transpose(1, 0, 2).reshape(N_META, 1024)
    full["dn_conv_w"] = gc[:, 16:32].reshape(N_DEV, 4, 512).transpose(1, 0, 2).reshape(4, 4096)
    full["dn_norm_w"] = gc[:, 32].reshape(1, 1024)

    seq = x.shape[1]
    h0 = jnp.concatenate([jnp.zeros((FRONT_PAD, D_MODEL), F32), meta_full, x[0]], axis=0)
    target = jnp.concatenate([jnp.zeros((ATTN_BLOCK, D_MODEL), F32), loss_target[0]], axis=0)
    loss, dh0, g = _local_step(h0, target, full)
    grad_x = dh0[ATTN_BLOCK:ATTN_BLOCK + seq][None]
    g["meta_tokens"] = dh0[FRONT_PAD:ATTN_BLOCK]

    pa = jnp.concatenate([g["attn_w_in"].reshape(1024, N_DEV, 288), g["dn_w_in"].reshape(1024, N_DEV, 772)],
                         axis=2).transpose(1, 0, 2)
    pb = jnp.concatenate([g["attn_w_out"].reshape(N_DEV, 128, 1024), g["dn_w_out"].reshape(N_DEV, 256, 1024)], axis=1)
    dn_norm8 = jnp.concatenate([g["dn_norm_w"].reshape(N_DEV, 1, 128), jnp.zeros((N_DEV, 7, 128), F32)], axis=1)
    pc = jnp.concatenate([g["meta_tokens"].reshape(N_META, N_DEV, 128).transpose(1, 0, 2),
                          g["dn_conv_w"].reshape(4, N_DEV, 512).transpose(1, 0, 2).reshape(N_DEV, 16, 128), dn_norm8], axis=1)
    ps = small_of(g, loss)
    xa, xb, xc, xs = _exchange([pa.astype(BF16), pb.astype(BF16), pc, ps], [True, True, True, False], "exchange_grads")

    out = {}
    ma, mb, mc = rows_of(mom_m)
    va, vb, vc = rows_of(mom_v)
    ra = _adamw(xa, wa, ma, va, "adamw_a")
    rb = _adamw(xb, wb, mb, vb, "adamw_b")
    rc = _adamw(xc, wc, mc, vc, "adamw_c")
    zero = jnp.zeros((1, 128), F32)
    rs = _adamw(xs, small_of(shard, zero), small_of(mom_m, zero), small_of(mom_v, zero), "adamw_small")
    row_names = ["attn_w_in", "dn_w_in", "attn_w_out", "dn_w_out", "meta_tokens", "dn_conv_w", "dn_norm_w"]
    lead = {"attn_w_in", "dn_w_in", "attn_w_out", "dn_w_out", "dn_conv_w"}
    for kind in range(4):
        vals = dict(zip(row_names, _unpack_rows(ra[kind], rb[kind], rc[kind])))
        small = _unpack_small(rs[kind])
        vals.update(dict(zip(SMALL, small[:7])))
        if kind == 0:
            loss_total = small[7][0, 0]
        out[kind] = [vals[k][None] if k in lead else vals[k] for k in WEIGHTS]
    return (loss_total, grad_x, *out[0], *out[1], *out[2], *out[3])
```

```python
import functools
import math

import jax
import jax.numpy as jnp
from jax import lax
from jax.experimental import pallas as pl
from jax.experimental.pallas import tpu as pltpu

F32, BF16 = jnp.float32, jnp.bfloat16

D_MODEL = 1024
N_META = 16
NORM_EPS = 1e-6
ATTN_HEADS, ATTN_KV_HEADS, ATTN_GROUPS, ATTN_HD = 16, 2, 8, 64
ATTN_BLOCK = 128
FRONT_PAD = ATTN_BLOCK - N_META
DN_HD, DN_K_HEADS, DN_V_HEADS = 128, 8, 16
DN_CHUNK = 64
SCAN_BWD_GROUP = 4
DN_KEY_W, DN_VAL_W = 1024, 2048
DN_CONV_W = 2 * DN_KEY_W + DN_VAL_W
DN_CONV_K = 4
N_DEV = 8
ROW_BLOCK = 384
VMEM_LIMIT = 56 * 1024 * 1024
NEG = -1e30

ADAM_LR, ADAM_B1, ADAM_B2, ADAM_EPS, ADAM_WD, ADAM_STEP = 0.001, 0.9, 0.999, 1e-08, 0.01, 10

NT = (((1,), (1,)), ((), ()))
TN = (((0,), (0,)), ((), ()))


def _cparams(sem=("arbitrary",)):
    return pltpu.CompilerParams(dimension_semantics=sem, vmem_limit_bytes=VMEM_LIMIT)


def _rms(x, w):
    return x * lax.rsqrt(jnp.mean(x * x, axis=-1, keepdims=True) + NORM_EPS) * w


def _silu(x):
    return x * jax.nn.sigmoid(x)


def _softplus(x):
    return jnp.maximum(x, 0.0) + jnp.log(1.0 + jnp.exp(-jnp.abs(x)))


NN = (((1,), (0,)), ((), ()))


def _mm(a, b, dims):
    return lax.dot_general(a.astype(BF16), b.astype(BF16), dims, preferred_element_type=F32)


@functools.partial(jax.custom_vjp, nondiff_argnums=(2,))
def _bdot_vjp(a, b, dims):
    return _mm(a, b, dims)


def _bdot_fwd(a, b, dims):
    a16, b16 = a.astype(BF16), b.astype(BF16)
    return _mm(a16, b16, dims), (a16, b16, jnp.zeros((), a.dtype), jnp.zeros((), b.dtype))


def _bdot_bwd(dims, res, g):
    a16, b16, ta, tb = res
    g16 = g.astype(BF16)
    if dims == NN:
        da, db = _mm(g16, b16, NT), _mm(a16, g16, TN)
    elif dims == NT:
        da, db = _mm(g16, b16, NN), _mm(g16, a16, TN)
    else:
        da, db = _mm(b16, g16, NT), _mm(a16, g16, NN)
    return da.astype(ta.dtype), db.astype(tb.dtype)


_bdot_vjp.defvjp(_bdot_fwd, _bdot_bwd)


def _bdot(a, b, dims=NN):
    return _bdot_vjp(a, b, dims)


def _hdot(a, b):
    return jnp.dot(a, b, preferred_element_type=F32, precision=lax.Precision.HIGHEST)


def _row_call(name, body, n_rows, rb, rows, consts, outs, accs=(), reverse=False, scratch=(), halos=()):
    n = n_rows // rb
    assert n * rb == n_rows
    idx = (lambda i: (n - 1 - i, 0)) if reverse else (lambda i: (i, 0))
    in_specs = [pl.BlockSpec((rb, a.shape[1]), idx) for a in rows]
    in_specs += [pl.BlockSpec((hr, a.shape[1]), fn) for a, hr, fn in halos]
    in_specs += [pl.BlockSpec(c.shape, functools.partial(lambda i, nd: (0,) * nd, nd=c.ndim)) for c in consts]
    out_specs = [pl.BlockSpec((rb, c), idx) for c, _ in outs]
    out_specs += [pl.BlockSpec(s, functools.partial(lambda i, nd: (0,) * nd, nd=len(s))) for s, _ in accs]
    out_shape = [jax.ShapeDtypeStruct((n_rows, c), dt) for c, dt in outs]
    out_shape += [jax.ShapeDtypeStruct(s, dt) for s, dt in accs]
    return pl.pallas_call(
        body, grid=(n,), in_specs=in_specs, out_specs=out_specs, out_shape=out_shape,
        scratch_shapes=list(scratch), name=name, compiler_params=_cparams(),
    )(*rows, *[a for a, _, _ in halos], *consts)


def _attn_in_fwd(h0, norm_w, w_in):
    T = h0.shape[0]

    def body(h_ref, nw_ref, w_ref, xn_ref, q_ref, kv_ref, gate_ref):
        xn = _rms(h_ref[...], nw_ref[...]).astype(BF16)
        xn_ref[...] = xn
        q_ref[...] = jnp.dot(xn, w_ref[:, 0:1024], preferred_element_type=F32)
        kv_ref[...] = jnp.dot(xn, w_ref[:, 1024:1280], preferred_element_type=F32)
        gate_ref[...] = jnp.dot(xn, w_ref[:, 1280:2304], preferred_element_type=F32)

    return _row_call("attn_in_fwd", body, T, ROW_BLOCK, [h0], [norm_w, w_in],
                     [(1024, BF16), (1024, F32), (256, F32), (1024, F32)])


def _attn_bias(n, j):
    C, R = 2 * ATTN_BLOCK + N_META, ATTN_GROUPS * ATTN_BLOCK
    c = lax.broadcasted_iota(jnp.int32, (C, R), 0)
    r = lax.broadcasted_iota(jnp.int32, (C, R), 1)
    ql = r & (ATTN_BLOCK - 1)
    is_meta = c >= 2 * ATTN_BLOCK
    dist_band = ATTN_BLOCK + ql - c
    cmin = jnp.maximum(0, 2 * ATTN_BLOCK - ATTN_BLOCK * n)
    valid_band = (c >= cmin) & (dist_band >= 0) & (dist_band < ATTN_BLOCK)
    dist_meta = ATTN_BLOCK * n + ql - FRONT_PAD - (c - 2 * ATTN_BLOCK)
    valid = (is_meta & (dist_meta >= 0)) | (jnp.logical_not(is_meta) & valid_band)
    dist = jnp.minimum(jnp.where(is_meta, dist_meta, dist_band), ATTN_BLOCK).astype(F32)
    rr = lax.broadcasted_iota(jnp.int32, (1, R), 1)
    head = (rr >> 7).astype(F32) + float(ATTN_GROUPS * j + 1)
    slope = jnp.exp(head * (-0.5 * math.log(2.0)))
    return jnp.where(valid, slope * dist, -NEG)


def _attn_tables(n, refresh, bias_ref):
    @pl.when(refresh)
    def _():
        for j in range(ATTN_KV_HEADS):
            bias_ref[j] = _attn_bias(n, j)


def _attn_table_scratch():
    return [pltpu.VMEM((ATTN_KV_HEADS, 2 * ATTN_BLOCK + N_META, ATTN_GROUPS * ATTN_BLOCK), F32)]


def _attn_group(q_t, k, v, sinkrow, qnw_col, knw, bias):
    qn = q_t * lax.rsqrt(jnp.mean(q_t * q_t, axis=0, keepdims=True) + NORM_EPS) * qnw_col
    kn = _rms(k, knw)
    s = _bdot(kn, qn) * (ATTN_HD ** -0.5) - bias
    m = lax.stop_gradient(jnp.maximum(jnp.max(s, axis=0, keepdims=True), sinkrow))
    e = jnp.exp(s - m)
    denom = jnp.sum(e, axis=0, keepdims=True) + jnp.exp(sinkrow - m)
    p = e * (1.0 / denom)
    return _bdot(v, p, TN)


def _sink_row(sinks_ref, j):
    rr = lax.broadcasted_iota(jnp.int32, (1, ATTN_GROUPS * ATTN_BLOCK), 1) >> 7
    row = jnp.zeros((1, ATTN_GROUPS * ATTN_BLOCK), F32)
    for hl in range(ATTN_GROUPS):
        row = jnp.where(rr == hl, sinks_ref[0, ATTN_GROUPS * j + hl], row)
    return row


def _heads_to_lanes(ref, j):
    return jnp.concatenate([ref[:, ATTN_HD * h:ATTN_HD * (h + 1)].T
                            for h in range(ATTN_GROUPS * j, ATTN_GROUPS * (j + 1))], axis=1)


def _lanes_to_heads(ref, j, x_t):
    for hl in range(ATTN_GROUPS):
        h = ATTN_GROUPS * j + hl
        ref[:, ATTN_HD * h:ATTN_HD * (h + 1)] = x_t[:, ATTN_BLOCK * hl:ATTN_BLOCK * (hl + 1)].T


def _attn_kv_tiles(kvp_ref, kvc_ref, kvm_ref, j):
    ksl = slice(ATTN_HD * j, ATTN_HD * (j + 1))
    vsl = slice(128 + ATTN_HD * j, 128 + ATTN_HD * (j + 1))
    k = jnp.concatenate([kvp_ref[:, ksl], kvc_ref[:, ksl], kvm_ref[FRONT_PAD:, ksl]], axis=0)
    v = jnp.concatenate([kvp_ref[:, vsl], kvc_ref[:, vsl], kvm_ref[FRONT_PAD:, vsl]], axis=0)
    return k, v


def _attn_core_fwd(q, kv, sinks, qnw, knw):
    T = q.shape[0]
    nb = T // ATTN_BLOCK

    def body(sinks_ref, q_ref, kvc_ref, kvp_ref, kvm_ref, qnw_ref, knw_ref, o_ref, bias_ref):
        n = pl.program_id(0)
        _attn_tables(n, n <= 2, bias_ref)
        for j in range(ATTN_KV_HEADS):
            k, v = _attn_kv_tiles(kvp_ref, kvc_ref, kvm_ref, j)
            o_t = _attn_group(_heads_to_lanes(q_ref, j), k, v, _sink_row(sinks_ref, j), qnw_ref[...], knw_ref[...],
                              bias_ref[j])
            _lanes_to_heads(o_ref, j, o_t)

    return pl.pallas_call(
        body, grid=(nb,),
        in_specs=[pl.BlockSpec(memory_space=pltpu.SMEM),
                  pl.BlockSpec((ATTN_BLOCK, 1024), lambda i: (i, 0)),
                  pl.BlockSpec((ATTN_BLOCK, 256), lambda i: (i, 0)),
                  pl.BlockSpec((ATTN_BLOCK, 256), lambda i: (jnp.maximum(i - 1, 0), 0)),
                  pl.BlockSpec((ATTN_BLOCK, 256), lambda i: (0, 0)),
                  pl.BlockSpec((ATTN_HD, 1), lambda i: (0, 0)),
                  pl.BlockSpec((1, ATTN_HD), lambda i: (0, 0))],
        out_specs=pl.BlockSpec((ATTN_BLOCK, 1024), lambda i: (i, 0)),
        out_shape=jax.ShapeDtypeStruct((T, 1024), F32),
        scratch_shapes=_attn_table_scratch(),
        name="attn_core_fwd", compiler_params=_cparams(),
    )(sinks, q, kv, kv, kv, qnw.reshape(ATTN_HD, 1), knw)


def _attn_out_fwd(o, gate, h0, w_out):
    T = o.shape[0]

    def body(o_ref, g_ref, h_ref, w_ref, h1_ref):
        og = o_ref[...] * _silu(g_ref[...])
        h1_ref[...] = h_ref[...] + _bdot(og, w_ref[...])

    return _row_call("attn_out_fwd", body, T, ROW_BLOCK, [o, gate, h0], [w_out], [(1024, F32)])[0]


def _wgrad(xn, du, cg, name):
    T, kdim = xn.shape
    cdim = du.shape[1]
    nr, nc = T // ROW_BLOCK, cdim // cg
    assert nc * cg == cdim

    def body(x_ref, du_ref, dw_ref):
        @pl.when(pl.program_id(1) == 0)
        def _():
            dw_ref[...] = jnp.zeros_like(dw_ref)
        dw_ref[...] += _bdot(x_ref[...], du_ref[...], TN)

    return pl.pallas_call(
        body, grid=(nc, nr),
        in_specs=[pl.BlockSpec((ROW_BLOCK, kdim), lambda j, i: (i, 0)),
                  pl.BlockSpec((ROW_BLOCK, cg), lambda j, i: (i, j))],
        out_specs=pl.BlockSpec((kdim, cg), lambda j, i: (0, j)),
        out_shape=jax.ShapeDtypeStruct((kdim, cdim), F32),
        name=name, compiler_params=_cparams(("arbitrary", "arbitrary")),
    )(xn, du)


def _attn_out_bwd(dh1, o, gate, w_out):
    T = o.shape[0]

    def body(dh_ref, o_ref, g_ref, w_ref, do_ref, dg_ref, dw_ref):
        @pl.when(pl.program_id(0) == 0)
        def _():
            dw_ref[...] = jnp.zeros_like(dw_ref)
        dh = dh_ref[...]
        dog = _bdot(dh, w_ref[...], NT)
        og, vjp = jax.vjp(lambda o_, g_: o_ * _silu(g_), o_ref[...], g_ref[...])
        do, dg = vjp(dog)
        do_ref[...] = do
        dg_ref[...] = dg
        dw_ref[...] += _bdot(og, dh, TN)

    return _row_call("attn_out_bwd", body, T, ROW_BLOCK, [dh1, o, gate], [w_out],
                     [(1024, F32), (1024, F32)], [((1024, 1024), F32)])


def _attn_core_bwd(do, q, kv, sinks, qnw, knw):
    T = q.shape[0]
    nb = T // ATTN_BLOCK
    rev = lambda i: nb - 1 - i

    def body(sinks_ref, do_ref, q_ref, kvc_ref, kvp_ref, kvm_ref, qnw_ref, knw_ref,
             dq_ref, dkv_ref, dsinks_ref, dqnw_ref, dknw_ref, carry_ref, meta_ref, bias_ref):
        step = pl.program_id(0)
        n = rev(step)
        _attn_tables(n, (step == 0) | (n <= 1), bias_ref)

        @pl.when(step == 0)
        def _():
            carry_ref[...] = jnp.zeros_like(carry_ref)
            meta_ref[...] = jnp.zeros_like(meta_ref)
            dsinks_ref[...] = jnp.zeros_like(dsinks_ref)
            dqnw_ref[...] = jnp.zeros_like(dqnw_ref)
            dknw_ref[...] = jnp.zeros_like(dknw_ref)

        lane16 = lax.broadcasted_iota(jnp.int32, (1, ATTN_HEADS), 1)
        dsinks = jnp.zeros((1, ATTN_HEADS), F32)
        for j in range(ATTN_KV_HEADS):
            k, v = _attn_kv_tiles(kvp_ref, kvc_ref, kvm_ref, j)
            fn = functools.partial(_attn_group, bias=bias_ref[j])
            _, vjp = jax.vjp(fn, _heads_to_lanes(q_ref, j), k, v, _sink_row(sinks_ref, j), qnw_ref[...], knw_ref[...])
            dq_t, dk, dv, dsr, dqn, dkn = vjp(_heads_to_lanes(do_ref, j))
            _lanes_to_heads(dq_ref, j, dq_t)
            dqnw_ref[...] += dqn
            dknw_ref[...] += dkn
            for hl in range(ATTN_GROUPS):
                dsinks = dsinks + jnp.where(lane16 == ATTN_GROUPS * j + hl,
                                            jnp.sum(dsr[:, ATTN_BLOCK * hl:ATTN_BLOCK * (hl + 1)]), 0.0)
            ksl = slice(ATTN_HD * j, ATTN_HD * (j + 1))
            vsl = slice(128 + ATTN_HD * j, 128 + ATTN_HD * (j + 1))
            for sl, d in ((ksl, dk), (vsl, dv)):
                dkv_ref[:, sl] = d[ATTN_BLOCK:2 * ATTN_BLOCK, :] + carry_ref[:, sl]
                carry_ref[:, sl] = d[0:ATTN_BLOCK, :]
                meta_ref[:, sl] += d[2 * ATTN_BLOCK:, :]
        dsinks_ref[...] += dsinks

        @pl.when(n == 0)
        def _():
            dkv_ref[FRONT_PAD:, :] += meta_ref[...]

    dq, dkv, dsinks, dqnw, dknw = pl.pallas_call(
        body, grid=(nb,),
        in_specs=[pl.BlockSpec(memory_space=pltpu.SMEM),
                  pl.BlockSpec((ATTN_BLOCK, 1024), lambda i: (rev(i), 0)),
                  pl.BlockSpec((ATTN_BLOCK, 1024), lambda i: (rev(i), 0)),
                  pl.BlockSpec((ATTN_BLOCK, 256), lambda i: (rev(i), 0)),
                  pl.BlockSpec((ATTN_BLOCK, 256), lambda i: (jnp.maximum(rev(i) - 1, 0), 0)),
                  pl.BlockSpec((ATTN_BLOCK, 256), lambda i: (0, 0)),
                  pl.BlockSpec((ATTN_HD, 1), lambda i: (0, 0)),
                  pl.BlockSpec((1, ATTN_HD), lambda i: (0, 0))],
        out_specs=[pl.BlockSpec((ATTN_BLOCK, 1024), lambda i: (rev(i), 0)),
                   pl.BlockSpec((ATTN_BLOCK, 256), lambda i: (rev(i), 0)),
                   pl.BlockSpec((1, ATTN_HEADS), lambda i: (0, 0)),
                   pl.BlockSpec((ATTN_HD, 1), lambda i: (0, 0)),
                   pl.BlockSpec((1, ATTN_HD), lambda i: (0, 0))],
        out_shape=[jax.ShapeDtypeStruct((T, 1024), F32), jax.ShapeDtypeStruct((T, 256), F32),
                   jax.ShapeDtypeStruct((1, ATTN_HEADS), F32), jax.ShapeDtypeStruct((ATTN_HD, 1), F32),
                   jax.ShapeDtypeStruct((1, ATTN_HD), F32)],
        scratch_shapes=[pltpu.VMEM((ATTN_BLOCK, 256), F32), pltpu.VMEM((N_META, 256), F32)] + _attn_table_scratch(),
        name="attn_core_bwd", compiler_params=_cparams(),
    )(sinks, do, q, kv, kv, kv, qnw.reshape(ATTN_HD, 1), knw)
    return dq, dkv, dsinks, dqnw.reshape(1, ATTN_HD), dknw


def _attn_in_bwd(dq, dkv, dgate, h0, dh1, norm_w, w_in):
    T = h0.shape[0]

    def body(dq_ref, dkv_ref, dg_ref, h_ref, dh1_ref, nw_ref, w_ref, dh0_ref, dnw_ref):
        @pl.when(pl.program_id(0) == 0)
        def _():
            dnw_ref[...] = jnp.zeros_like(dnw_ref)
        dxn = (_bdot(dq_ref[...], w_ref[:, 0:1024], NT) + _bdot(dkv_ref[...], w_ref[:, 1024:1280], NT)
               + _bdot(dg_ref[...], w_ref[:, 1280:2304], NT))
        _, vjp = jax.vjp(_rms, h_ref[...], nw_ref[...])
        dh, dnw = vjp(dxn)
        dh0_ref[...] = dh1_ref[...] + dh
        dnw_ref[...] += dnw

    return _row_call("attn_in_bwd", body, T, ROW_BLOCK, [dq, dkv, dgate, h0, dh1], [norm_w, w_in],
                     [(1024, F32)], [((1, 1024), F32)])


def _dn_in_fwd(h1, norm_w, w_in):
    T = h1.shape[0]

    def body(h_ref, nw_ref, w_ref, xn_ref, qkv_ref, z_ref, ba_ref):
        xn = _rms(h_ref[...], nw_ref[...]).astype(BF16)
        xn_ref[...] = xn
        qkv_ref[...] = jnp.dot(xn, w_ref[:, 0:4096], preferred_element_type=F32)
        z_ref[...] = jnp.dot(xn, w_ref[:, 4096:6144], preferred_element_type=F32)
        ba_ref[...] = jnp.dot(xn, w_ref[:, 6144:6176], preferred_element_type=F32)

    return _row_call("dn_in_fwd", body, T, ROW_BLOCK, [h1], [norm_w, w_in],
                     [(1024, BF16), (4096, F32), (2048, F32), (32, F32)])


def _shift_down(cur, prev8, s):
    i8 = lax.broadcasted_iota(jnp.int32, (8, cur.shape[1]), 0)
    r = pltpu.roll(cur, s, 0)
    head = jnp.where(i8 < s, pltpu.roll(prev8, s, 0), r[0:8])
    return jnp.concatenate([head, r[8:]], axis=0)


def _shift_up(cur, next8, s):
    n = cur.shape[0]
    i8 = lax.broadcasted_iota(jnp.int32, (8, cur.shape[1]), 0)
    r = pltpu.roll(cur, n - s, 0)
    tail = jnp.where(i8 >= 8 - s, pltpu.roll(next8, 8 - s, 0), r[n - 8:])
    return jnp.concatenate([r[:n - 8], tail], axis=0)


def _conv_tile(cur, prev8, w):
    out = w[3:4, :] * cur
    for s in range(1, DN_CONV_K):
        out = out + w[3 - s:4 - s, :] * _shift_down(cur, prev8, s)
    return out


def _l2n(a, scale):
    return a * (lax.rsqrt(jnp.sum(a * a, axis=-1, keepdims=True) + NORM_EPS) * scale)


def _dn_post_tile(c, t):
    a = _silu(c)
    if t < DN_K_HEADS:
        return _l2n(a, DN_HD ** -0.5)
    if t < 2 * DN_K_HEADS:
        return _l2n(a, 1.0)
    return a


def _dn_beta_g(ba, a_log, dt_bias, live):
    beta = jax.nn.sigmoid(ba[:, 0:DN_V_HEADS]) * live
    g = -jnp.exp(a_log) * _softplus(ba[:, DN_V_HEADS:] + dt_bias) * live
    return beta, g


def _live_rows(i, rb):
    rows = i * rb + lax.broadcasted_iota(jnp.int32, (rb, 1), 0)
    return (rows >= FRONT_PAD).astype(F32)


def _halo_spec_args(x, rb):
    per = rb // 8
    return (x, 8, lambda i: (jnp.maximum(i * per - 1, 0), 0))


def _dn_conv_fwd(qkv, ba, conv_w, a_log, dt_bias):
    T = qkv.shape[0]

    def body(x_ref, ba_ref, halo_ref, cw_ref, al_ref, dtb_ref, q_ref, k_ref, v_ref, bg_ref):
        i = pl.program_id(0)
        first = (i > 0).astype(F32)
        for t in range(DN_CONV_W // 128):
            cols = slice(128 * t, 128 * (t + 1))
            c = _conv_tile(x_ref[:, cols], halo_ref[:, cols] * first, cw_ref[:, cols])
            out = _dn_post_tile(c, t)
            if t < DN_K_HEADS:
                q_ref[:, cols] = out
            elif t < 2 * DN_K_HEADS:
                k_ref[:, 128 * (t - 8):128 * (t - 7)] = out
            else:
                v_ref[:, 128 * (t - 16):128 * (t - 15)] = out
        beta, g = _dn_beta_g(ba_ref[...], al_ref[...], dtb_ref[...], _live_rows(i, ROW_BLOCK))
        bg_ref[:, 0:DN_V_HEADS] = beta
        bg_ref[:, DN_V_HEADS:] = g

    return _row_call("dn_conv_fwd", body, T, ROW_BLOCK, [qkv, ba], [conv_w, a_log, dt_bias],
                     [(1024, F32), (1024, F32), (2048, F32), (32, F32)], halos=[_halo_spec_args(qkv, ROW_BLOCK)])


def _chunk_masks():
    r = lax.broadcasted_iota(jnp.int32, (DN_CHUNK, DN_CHUNK), 0)
    c = lax.broadcasted_iota(jnp.int32, (DN_CHUNK, DN_CHUNK), 1)
    return r >= c, r > c, r == c, r <= c


def _tri_inv(x, eye):
    C = DN_CHUNK
    n = range(len(x))
    ainv = [jnp.where(eye, 1.0, 0.0) + x[h] for h in n]
    p = [_bdot(x[h], x[h]) for h in n]
    for _ in range(4):
        r = [_bdot(jnp.concatenate([p[h], ainv[h]], axis=0), p[h]) for h in n]
        ainv = [ainv[h] + r[h][C:] for h in n]
        p = [r[h][:C] for h in n]
    return [ainv[h] + _bdot(ainv[h], p[h]) for h in n]


@jax.custom_vjp
def _tri_inv_known(x, a):
    return a


def _tri_inv_known_fwd(x, a):
    return a, a


def _tri_inv_known_bwd(a, da):
    return [_bdot(_bdot(a[h], da[h], TN), a[h], NT) for h in range(len(a))], [jnp.zeros_like(t) for t in a]


_tri_inv_known.defvjp(_tri_inv_known_fwd, _tri_inv_known_bwd)


def _dn_chunk_step(S, q, k, v, beta, g, masks, known_inv=None, with_inv=False):
    causal, strict, eye, upper = masks
    C, W = DN_CHUNK, DN_HD
    heads = range(len(v))
    k_t = [k[j].T for j in range(len(k))]
    qk_kk = [_bdot(jnp.concatenate([q[j], k[j]], axis=0), k_t[j]) for j in range(len(q))]
    g_b = [jnp.broadcast_to(g[h], (C, C)) for h in heads]
    beta_b = [jnp.broadcast_to(beta[h], (C, W)) for h in heads]
    g_row = [jnp.sum(jnp.where(eye, g_b[h], 0.0), axis=0, keepdims=True) for h in heads]
    gc_col = [jnp.sum(jnp.where(causal, g_row[h], 0.0), axis=1, keepdims=True) for h in heads]
    gc_row = [jnp.sum(jnp.where(upper, g_b[h], 0.0), axis=0, keepdims=True) for h in heads]
    g_last = [jnp.sum(g_row[h], axis=1, keepdims=True) for h in heads]
    gc_b = [jnp.broadcast_to(gc_col[h], (C, W)) for h in heads]
    decay = [jnp.exp(jnp.where(causal, gc_b[h][:, :C] - gc_row[h], NEG)) for h in heads]
    eg_b = [jnp.exp(gc_b[h]) for h in heads]
    x = [jnp.where(strict, qk_kk[h // 2][C:] * beta_b[h][:, :C] * decay[h], 0.0) * -1.0 for h in heads]
    ainv = _tri_inv(x, eye) if known_inv is None else _tri_inv_known(x, known_inv)
    uw = [_bdot(ainv[h], jnp.concatenate([v[h] * beta_b[h], k[h // 2] * (beta_b[h] * eg_b[h])], axis=1)) for h in heads]
    ws_qs = [_bdot(jnp.concatenate([uw[h][:, W:], q[h // 2] * eg_b[h]], axis=0), S[h]) for h in heads]
    v_new = [uw[h][:, :W] - ws_qs[h][:C] for h in heads]
    o = [ws_qs[h][C:] + _bdot(qk_kk[h // 2][:C] * decay[h], v_new[h]) for h in heads]
    s_new = [S[h] * jnp.exp(g_last[h]) + _bdot(k_t[h // 2] * jnp.exp(g_last[h] - gc_row[h]), v_new[h]) for h in heads]
    return (s_new, o, ainv) if with_inv else (s_new, o)


def _dn_chunk_tiles(q_ref, k_ref, v_ref, bg_ref):
    q = [q_ref[:, 128 * j:128 * (j + 1)] for j in range(DN_K_HEADS)]
    k = [k_ref[:, 128 * j:128 * (j + 1)] for j in range(DN_K_HEADS)]
    v = [v_ref[:, 128 * h:128 * (h + 1)] for h in range(DN_V_HEADS)]
    beta = [bg_ref[:, h:h + 1] for h in range(DN_V_HEADS)]
    g = [bg_ref[:, DN_V_HEADS + h:DN_V_HEADS + h + 1] for h in range(DN_V_HEADS)]
    return q, k, v, beta, g


def _dn_scan_fwd(qn, kn, v, bg):
    T = qn.shape[0]
    nc = T // DN_CHUNK

    def body(q_ref, k_ref, v_ref, bg_ref, o_ref, ssave_ref, inv_ref, s_ref):
        @pl.when(pl.program_id(0) == 0)
        def _():
            s_ref[...] = jnp.zeros_like(s_ref)
        s_old = [s_ref[h] for h in range(DN_V_HEADS)]
        for h in range(DN_V_HEADS):
            ssave_ref[0, h] = s_old[h]
        s_new, o, ainv = _dn_chunk_step(s_old, *_dn_chunk_tiles(q_ref, k_ref, v_ref, bg_ref), _chunk_masks(),
                                        with_inv=True)
        for h in range(DN_V_HEADS):
            o_ref[:, 128 * h:128 * (h + 1)] = o[h]
            inv_ref[0, h] = ainv[h].astype(BF16)
            s_ref[h] = s_new[h]

    return pl.pallas_call(
        body, grid=(nc,),
        in_specs=[pl.BlockSpec((DN_CHUNK, 1024), lambda i: (i, 0)),
                  pl.BlockSpec((DN_CHUNK, 1024), lambda i: (i, 0)),
                  pl.BlockSpec((DN_CHUNK, 2048), lambda i: (i, 0)),
                  pl.BlockSpec((DN_CHUNK, 32), lambda i: (i, 0))],
        out_specs=[pl.BlockSpec((DN_CHUNK, 2048), lambda i: (i, 0)),
                   pl.BlockSpec((1, DN_V_HEADS, DN_HD, DN_HD), lambda i: (i, 0, 0, 0)),
                   pl.BlockSpec((1, DN_V_HEADS, DN_CHUNK, DN_CHUNK), lambda i: (i, 0, 0, 0))],
        out_shape=[jax.ShapeDtypeStruct((T, 2048), F32),
                   jax.ShapeDtypeStruct((nc, DN_V_HEADS, DN_HD, DN_HD), F32),
                   jax.ShapeDtypeStruct((nc, DN_V_HEADS, DN_CHUNK, DN_CHUNK), BF16)],
        scratch_shapes=[pltpu.VMEM((DN_V_HEADS, DN_HD, DN_HD), F32)],
        name="dn_scan_fwd", compiler_params=_cparams(),
    )(qn, kn, v, bg)


def _dn_gate_tile(o, z, onw):
    return _rms(o, onw) * _silu(z)


def _dn_out_fwd(o, z, h1, target, w_out, onw):
    T = o.shape[0]

    def body(o_ref, z_ref, h_ref, t_ref, w_ref, onw_ref, dy_ref, og_ref, loss_ref):
        i = pl.program_id(0)

        @pl.when(i == 0)
        def _():
            loss_ref[...] = jnp.zeros_like(loss_ref)
        for h in range(DN_V_HEADS):
            cols = slice(128 * h, 128 * (h + 1))
            og_ref[:, cols] = _dn_gate_tile(o_ref[:, cols], z_ref[:, cols], onw_ref[...]).astype(BF16)
        y = h_ref[...] + jnp.dot(og_ref[...], w_ref[...], preferred_element_type=F32)
        rows = i * ROW_BLOCK + lax.broadcasted_iota(jnp.int32, (ROW_BLOCK, 1), 0)
        diff = jnp.where(rows >= FRONT_PAD + N_META, y - t_ref[...], 0.0)
        dy_ref[...] = diff * (1.0 / D_MODEL)
        loss_ref[...] += jnp.sum(diff * diff) * (0.5 / D_MODEL)

    return _row_call("dn_out_fwd", body, T, ROW_BLOCK, [o, z, h1, target], [w_out, onw],
                     [(1024, F32), (2048, BF16)], [((1, 128), F32)])


def _dn_out_bwd(dy, o, z, w_out, onw):
    T = o.shape[0]

    def body(dy_ref, o_ref, z_ref, w_ref, onw_ref, do_ref, dz_ref, donw_ref):
        @pl.when(pl.program_id(0) == 0)
        def _():
            donw_ref[...] = jnp.zeros_like(donw_ref)
        dy = dy_ref[...].astype(BF16)
        donw = jnp.zeros((1, DN_HD), F32)
        for h in range(DN_V_HEADS):
            cols = slice(128 * h, 128 * (h + 1))
            dog = lax.dot_general(dy, w_ref[cols, :], NT, preferred_element_type=F32)
            _, vjp = jax.vjp(_dn_gate_tile, o_ref[:, cols], z_ref[:, cols], onw_ref[...])
            do, dz, dn = vjp(dog)
            do_ref[:, cols] = do
            dz_ref[:, cols] = dz
            donw = donw + dn
        donw_ref[...] += donw

    return _row_call("dn_out_bwd", body, T, ROW_BLOCK, [dy, o, z], [w_out, onw],
                     [(2048, F32), (2048, F32)], [((1, DN_HD), F32)])


def _dn_scan_bwd(do, qn, kn, v, bg, ssave, inv):
    T = qn.shape[0]
    nc = T // DN_CHUNK
    rev = lambda i: nc - 1 - i

    def body(do_ref, q_ref, k_ref, v_ref, bg_ref, ss_ref, inv_ref, dq_ref, dk_ref, dv_ref, dbg_ref, ds_ref):
        @pl.when(pl.program_id(0) == 0)
        def _():
            ds_ref[...] = jnp.zeros_like(ds_ref)
        lane32 = lax.broadcasted_iota(jnp.int32, (1, 2 * DN_V_HEADS), 1)
        masks = _chunk_masks()
        q, k, v, beta, g = _dn_chunk_tiles(q_ref, k_ref, v_ref, bg_ref)
        dbg = jnp.zeros((DN_CHUNK, 2 * DN_V_HEADS), F32)
        for first in range(0, DN_V_HEADS, SCAN_BWD_GROUP):
            heads = range(first, first + SCAN_BWD_GROUP)
            pairs = slice(first // 2, (first + SCAN_BWD_GROUP) // 2)
            hs = slice(first, first + SCAN_BWD_GROUP)
            fn = functools.partial(_dn_chunk_step, masks=masks, known_inv=[inv_ref[0, h].astype(F32) for h in heads])
            _, vjp = jax.vjp(fn, [ss_ref[0, h] for h in heads], q[pairs], k[pairs], v[hs], beta[hs], g[hs])
            ds, dq, dk, dv, dbeta, dg = vjp(([ds_ref[h] for h in heads], [do_ref[:, 128 * h:128 * (h + 1)] for h in heads]))
            for i, h in enumerate(heads):
                ds_ref[h] = ds[i]
                dv_ref[:, 128 * h:128 * (h + 1)] = dv[i]
                dbg = dbg + jnp.where(lane32 == h, dbeta[i], 0.0) + jnp.where(lane32 == DN_V_HEADS + h, dg[i], 0.0)
            for i, j in enumerate(range(first // 2, (first + SCAN_BWD_GROUP) // 2)):
                dq_ref[:, 128 * j:128 * (j + 1)] = dq[i]
                dk_ref[:, 128 * j:128 * (j + 1)] = dk[i]
        dbg_ref[...] = dbg

    return pl.pallas_call(
        body, grid=(nc,),
        in_specs=[pl.BlockSpec((DN_CHUNK, 2048), lambda i: (rev(i), 0)),
                  pl.BlockSpec((DN_CHUNK, 1024), lambda i: (rev(i), 0)),
                  pl.BlockSpec((DN_CHUNK, 1024), lambda i: (rev(i), 0)),
                  pl.BlockSpec((DN_CHUNK, 2048), lambda i: (rev(i), 0)),
                  pl.BlockSpec((DN_CHUNK, 32), lambda i: (rev(i), 0)),
                  pl.BlockSpec((1, DN_V_HEADS, DN_HD, DN_HD), lambda i: (rev(i), 0, 0, 0)),
                  pl.BlockSpec((1, DN_V_HEADS, DN_CHUNK, DN_CHUNK), lambda i: (rev(i), 0, 0, 0))],
        out_specs=[pl.BlockSpec((DN_CHUNK, 1024), lambda i: (rev(i), 0)),
                   pl.BlockSpec((DN_CHUNK, 1024), lambda i: (rev(i), 0)),
                   pl.BlockSpec((DN_CHUNK, 2048), lambda i: (rev(i), 0)),
                   pl.BlockSpec((DN_CHUNK, 32), lambda i: (rev(i), 0))],
        out_shape=[jax.ShapeDtypeStruct((T, 1024), F32), jax.ShapeDtypeStruct((T, 1024), F32),
                   jax.ShapeDtypeStruct((T, 2048), F32), jax.ShapeDtypeStruct((T, 32), F32)],
        scratch_shapes=[pltpu.VMEM((DN_V_HEADS, DN_HD, DN_HD), F32)],
        name="dn_scan_bwd", compiler_params=_cparams(),
    )(do, qn, kn, v, bg, ssave, inv)


def _dn_conv_bwd(dqn, dkn, dv, dbg, qkv, ba, conv_w, a_log, dt_bias):
    T = qkv.shape[0]
    nr = T // ROW_BLOCK

    def body(dq_ref, dk_ref, dv_ref, dbg_ref, x_ref, ba_ref, halo_ref, cw_ref, al_ref, dtb_ref,
             dx_ref, dba_ref, dcw_ref, dal_ref, ddtb_ref, carry_ref):
        step = pl.program_id(0)
        i = nr - 1 - step

        @pl.when(step == 0)
        def _():
            carry_ref[...] = jnp.zeros_like(carry_ref)
            dcw_ref[...] = jnp.zeros_like(dcw_ref)
            dal_ref[...] = jnp.zeros_like(dal_ref)
            ddtb_ref[...] = jnp.zeros_like(ddtb_ref)
        first = (i > 0).astype(F32)
        for t in range(DN_CONV_W // 128):
            cols = slice(128 * t, 128 * (t + 1))
            cur, prev8, w = x_ref[:, cols], halo_ref[:, cols] * first, cw_ref[:, cols]
            c = _conv_tile(cur, prev8, w)
            if t < DN_K_HEADS:
                dout = dq_ref[:, cols]
            elif t < 2 * DN_K_HEADS:
                dout = dk_ref[:, 128 * (t - 8):128 * (t - 7)]
            else:
                dout = dv_ref[:, 128 * (t - 16):128 * (t - 15)]
            _, vjp = jax.vjp(functools.partial(_dn_post_tile, t=t), c)
            (dc,) = vjp(dout)
            nxt = carry_ref[:, cols]
            dx = w[3:4, :] * dc
            dcw_ref[3:4, cols] += jnp.sum(dc * cur, axis=0, keepdims=True)
            for s in range(1, DN_CONV_K):
                dx = dx + w[3 - s:4 - s, :] * _shift_up(dc, nxt, s)
                dcw_ref[3 - s:4 - s, cols] += jnp.sum(dc * _shift_down(cur, prev8, s), axis=0, keepdims=True)
            dx_ref[:, cols] = dx
            carry_ref[:, cols] = dc[0:8, :]
        fn = functools.partial(_dn_beta_g, live=_live_rows(i, ROW_BLOCK))
        _, vjp = jax.vjp(fn, ba_ref[...], al_ref[...], dtb_ref[...])
        dba, dal, ddtb = vjp((dbg_ref[:, 0:DN_V_HEADS], dbg_ref[:, DN_V_HEADS:]))
        dba_ref[...] = dba
        dal_ref[...] += dal
        ddtb_ref[...] += ddtb

    per = ROW_BLOCK // 8
    halo = (qkv, 8, lambda s: (jnp.maximum((nr - 1 - s) * per - 1, 0), 0))
    return _row_call("dn_conv_bwd", body, T, ROW_BLOCK, [dqn, dkn, dv, dbg, qkv, ba], [conv_w, a_log, dt_bias],
                     [(4096, F32), (32, F32)], [((DN_CONV_K, 4096), F32), ((1, DN_V_HEADS), F32), ((1, DN_V_HEADS), F32)],
                     reverse=True, scratch=[pltpu.VMEM((8, 4096), F32)], halos=[halo])


def _dn_in_bwd(dqkv, dz, dba, h1, dy, norm_w, w_in):
    T = h1.shape[0]

    def body(dqkv_ref, dz_ref, dba_ref, h_ref, dy_ref, nw_ref, w_ref, dh_ref, dnw_ref):
        @pl.when(pl.program_id(0) == 0)
        def _():
            dnw_ref[...] = jnp.zeros_like(dnw_ref)
        dxn = (_bdot(dqkv_ref[...], w_ref[:, 0:4096], NT) + _bdot(dz_ref[...], w_ref[:, 4096:6144], NT)
               + _bdot(dba_ref[...], w_ref[:, 6144:6176], NT))
        _, vjp = jax.vjp(_rms, h_ref[...], nw_ref[...])
        dh, dnw = vjp(dxn)
        dh_ref[...] = (dy_ref[...] + dh) * _live_rows(pl.program_id(0), ROW_BLOCK)
        dnw_ref[...] += dnw

    return _row_call("dn_in_bwd", body, T, ROW_BLOCK, [dqkv, dz, dba, h1, dy], [norm_w, w_in],
                     [(1024, F32)], [((1, 1024), F32)])


def _exchange(parts, scatter, name):
    n = len(parts)
    out_shape = [jax.ShapeDtypeStruct(p.shape if sc else (N_DEV,) + p.shape, p.dtype) for p, sc in zip(parts, scatter)]

    def body(*refs):
        ins, outs = refs[:n], refs[n:2 * n]
        send_sems, recv_sems, local_sems = refs[2 * n:]
        x, y, c = lax.axis_index("x"), lax.axis_index("y"), lax.axis_index("c")
        me = 4 * x + 2 * y + c
        peers = []
        for k in range(1, N_DEV):
            px = 1 - x if k & 4 else x
            py = 1 - y if k & 2 else y
            pc = 1 - c if k & 1 else c
            peers.append(((px, py, pc), 4 * px + 2 * py + pc))

        def src(a, idx):
            return ins[a].at[idx] if scatter[a] else ins[a]

        local = [pltpu.make_async_copy(src(a, me), outs[a].at[me], local_sems.at[a]) for a in range(n)]
        for cp in local:
            cp.start()
        for a in range(n):
            for k, (dev, idx) in enumerate(peers):
                pltpu.make_async_remote_copy(
                    src_ref=src(a, idx), dst_ref=outs[a].at[me], send_sem=send_sems.at[a, k], recv_sem=recv_sems.at[a, k],
                    device_id=dev, device_id_type=pl.DeviceIdType.MESH).start()
        for a in range(n):
            for k, (dev, idx) in enumerate(peers):
                pltpu.make_async_remote_copy(
                    src_ref=src(a, idx), dst_ref=outs[a].at[idx], send_sem=send_sems.at[a, k], recv_sem=recv_sems.at[a, k],
                    device_id=dev, device_id_type=pl.DeviceIdType.MESH).wait()
        for cp in local:
            cp.wait()

    hbm = pl.BlockSpec(memory_space=pltpu.HBM)
    return pl.pallas_call(
        body, out_shape=out_shape, in_specs=[hbm] * n, out_specs=[hbm] * n,
        scratch_shapes=[pltpu.SemaphoreType.DMA((n, N_DEV - 1)), pltpu.SemaphoreType.DMA((n, N_DEV - 1)),
                        pltpu.SemaphoreType.DMA((n,))],
        name=name,
    )(*parts)


def _adam_rows(rows):
    for rb in (128, 64, 40, 16, 8):
        if rows % rb == 0:
            return rb
    return rows


def _adamw(stack, w, m, v, name):
    R, C = w.shape
    rb = _adam_rows(R)

    def body(s_ref, w_ref, m_ref, v_ref, g_ref, d_ref, nm_ref, nv_ref):
        g = s_ref[0].astype(F32)
        for s in range(1, N_DEV):
            g = g + s_ref[s].astype(F32)
        nm = ADAM_B1 * m_ref[...] + (1.0 - ADAM_B1) * g
        nv = ADAM_B2 * v_ref[...] + (1.0 - ADAM_B2) * (g * g)
        m_hat = nm / (1.0 - ADAM_B1 ** ADAM_STEP)
        v_hat = nv / (1.0 - ADAM_B2 ** ADAM_STEP)
        g_ref[...] = g
        d_ref[...] = -ADAM_LR * (m_hat / (jnp.sqrt(v_hat) + ADAM_EPS) + ADAM_WD * w_ref[...])
        nm_ref[...] = nm
        nv_ref[...] = nv

    blk = pl.BlockSpec((rb, C), lambda i: (i, 0))
    return pl.pallas_call(
        body, grid=(R // rb,),
        in_specs=[pl.BlockSpec((N_DEV, rb, C), lambda i: (0, i, 0)), blk, blk, blk],
        out_specs=[blk] * 4, out_shape=[jax.ShapeDtypeStruct((R, C), F32)] * 4,
        name=name, compiler_params=_cparams(),
    )(stack, w, m, v)


def _pad_rows8(a):
    return jnp.concatenate([a, jnp.zeros((8 - a.shape[0], a.shape[1]), a.dtype)], axis=0) if a.shape[0] < 8 else a


def _pack_small(norm_w, qnw, knw, sinks, a_log, dt_bias, onw, extra):
    z = lambda n: jnp.zeros((1, n), F32)
    row = jnp.concatenate([norm_w, qnw, knw, sinks, a_log, dt_bias, z(80), onw, extra, z(512)], axis=1)
    return row.reshape(16, 128)


def _unpack_small(p):
    row = p.reshape(1, 2048)
    cut = lambda a, n: row[:, a:a + n]
    return (cut(0, 1024), cut(1024, 64), cut(1088, 64), cut(1152, 16), cut(1168, 16), cut(1184, 16), cut(1280, 128),
            cut(1408, 128))


def _pack_rows(w_in_a, w_in_d, w_out_a, w_out_d, meta, conv, dn_norm):
    a = jnp.concatenate([w_in_a, w_in_d], axis=1)
    b = jnp.concatenate([w_out_a, w_out_d], axis=0)
    c = jnp.concatenate([meta, conv.reshape(16, 128), _pad_rows8(dn_norm)], axis=0)
    return a, b, c


def _unpack_rows(a, b, c):
    return (a[:, :288], a[:, 288:], b[:128], b[128:], c[:16], c[16:32].reshape(4, 512), c[32:33])


def _local_step(h0, target, w):
    xn0, q, kv, gate = _attn_in_fwd(h0, w["attn_norm_w"], w["attn_w_in"])
    o = _attn_core_fwd(q, kv, w["attn_sinks"], w["attn_q_norm_w"], w["attn_k_norm_w"])
    h1 = _attn_out_fwd(o, gate, h0, w["attn_w_out"])
    xn1, qkv, z, ba = _dn_in_fwd(h1, w["dn_norm_w"], w["dn_w_in"])
    qn, kn, v, bg = _dn_conv_fwd(qkv, ba, w["dn_conv_w"], w["dn_a_log"], w["dn_dt_bias"])
    o_dn, ssave, inv = _dn_scan_fwd(qn, kn, v, bg)
    dy, og_dn, loss = _dn_out_fwd(o_dn, z, h1, target, w["dn_w_out"], w["dn_o_norm_w"])

    g = {}
    do_dn, dz, g["dn_o_norm_w"] = _dn_out_bwd(dy, o_dn, z, w["dn_w_out"], w["dn_o_norm_w"])
    g["dn_w_out"] = _wgrad(og_dn, dy, 1024, "wgrad_dn_out")
    dqn, dkn, dv, dbg = _dn_scan_bwd(do_dn, qn, kn, v, bg, ssave, inv)
    dqkv, dba, g["dn_conv_w"], g["dn_a_log"], g["dn_dt_bias"] = _dn_conv_bwd(
        dqn, dkn, dv, dbg, qkv, ba, w["dn_conv_w"], w["dn_a_log"], w["dn_dt_bias"])
    dh1, g["dn_norm_w"] = _dn_in_bwd(dqkv, dz, dba, h1, dy, w["dn_norm_w"], w["dn_w_in"])
    g["dn_w_in"] = jnp.concatenate([_wgrad(xn1, dqkv, 1024, "wgrad_dn_qkv"), _wgrad(xn1, dz, 1024, "wgrad_dn_z"),
                                    _wgrad(xn1, dba, 32, "wgrad_dn_ba")], axis=1)
    do, dgate, g["attn_w_out"] = _attn_out_bwd(dh1, o, gate, w["attn_w_out"])
    dq, dkv, g["attn_sinks"], g["attn_q_norm_w"], g["attn_k_norm_w"] = _attn_core_bwd(
        do, q, kv, w["attn_sinks"], w["attn_q_norm_w"], w["attn_k_norm_w"])
    dh0, g["attn_norm_w"] = _attn_in_bwd(dq, dkv, dgate, h0, dh1, w["attn_norm_w"], w["attn_w_in"])
    g["attn_w_in"] = jnp.concatenate([_wgrad(xn0, dq, 1024, "wgrad_attn_q"), _wgrad(xn0, dkv, 256, "wgrad_attn_kv"),
                                      _wgrad(xn0, dgate, 1024, "wgrad_attn_gate")], axis=1)
    return loss, dh0, g


WEIGHTS = ['meta_tokens', 'attn_norm_w', 'attn_w_in', 'attn_q_norm_w', 'attn_k_norm_w', 'attn_sinks', 'attn_w_out',
           'dn_norm_w', 'dn_w_in', 'dn_conv_w', 'dn_a_log', 'dn_dt_bias', 'dn_o_norm_w', 'dn_w_out']
SMALL = ['attn_norm_w', 'attn_q_norm_w', 'attn_k_norm_w', 'attn_sinks', 'dn_a_log', 'dn_dt_bias', 'dn_o_norm_w']


def kernel(x, meta_tokens, attn_norm_w, attn_w_in, attn_q_norm_w, attn_k_norm_w, attn_sinks, attn_w_out, dn_norm_w, dn_w_in, dn_conv_w, dn_a_log, dn_dt_bias, dn_o_norm_w, dn_w_out, loss_target, m_meta_tokens, m_attn_norm_w, m_attn_w_in, m_attn_q_norm_w, m_attn_k_norm_w, m_attn_sinks, m_attn_w_out, m_dn_norm_w, m_dn_w_in, m_dn_conv_w, m_dn_a_log, m_dn_dt_bias, m_dn_o_norm_w, m_dn_w_out, v_meta_tokens, v_attn_norm_w, v_attn_w_in, v_attn_q_norm_w, v_attn_k_norm_w, v_attn_sinks, v_attn_w_out, v_dn_norm_w, v_dn_w_in, v_dn_conv_w, v_dn_a_log, v_dn_dt_bias, v_dn_o_norm_w, v_dn_w_out):
    shard = dict(meta_tokens=meta_tokens, attn_norm_w=attn_norm_w, attn_w_in=attn_w_in[0], attn_q_norm_w=attn_q_norm_w,
                 attn_k_norm_w=attn_k_norm_w, attn_sinks=attn_sinks, attn_w_out=attn_w_out[0], dn_norm_w=dn_norm_w,
                 dn_w_in=dn_w_in[0], dn_conv_w=dn_conv_w[0], dn_a_log=dn_a_log, dn_dt_bias=dn_dt_bias,
                 dn_o_norm_w=dn_o_norm_w, dn_w_out=dn_w_out[0])
    mom_m = dict(meta_tokens=m_meta_tokens, attn_norm_w=m_attn_norm_w, attn_w_in=m_attn_w_in[0], attn_q_norm_w=m_attn_q_norm_w,
                 attn_k_norm_w=m_attn_k_norm_w, attn_sinks=m_attn_sinks, attn_w_out=m_attn_w_out[0], dn_norm_w=m_dn_norm_w,
                 dn_w_in=m_dn_w_in[0], dn_conv_w=m_dn_conv_w[0], dn_a_log=m_dn_a_log, dn_dt_bias=m_dn_dt_bias,
                 dn_o_norm_w=m_dn_o_norm_w, dn_w_out=m_dn_w_out[0])
    mom_v = dict(meta_tokens=v_meta_tokens, attn_norm_w=v_attn_norm_w, attn_w_in=v_attn_w_in[0], attn_q_norm_w=v_attn_q_norm_w,
                 attn_k_norm_w=v_attn_k_norm_w, attn_sinks=v_attn_sinks, attn_w_out=v_attn_w_out[0], dn_norm_w=v_dn_norm_w,
                 dn_w_in=v_dn_w_in[0], dn_conv_w=v_dn_conv_w[0], dn_a_log=v_dn_a_log, dn_dt_bias=v_dn_dt_bias,
                 dn_o_norm_w=v_dn_o_norm_w, dn_w_out=v_dn_w_out[0])

    def rows_of(d):
        return _pack_rows(d["attn_w_in"], d["dn_w_in"], d["attn_w_out"], d["dn_w_out"], d["meta_tokens"], d["dn_conv_w"],
                          d["dn_norm_w"])

    def small_of(d, extra):
        return _pack_small(*[d[k] for k in SMALL], extra)

    wa, wb, wc = rows_of(shard)
    ga, gb, gc = _exchange([wa.astype(BF16), wb.astype(BF16), wc], [False, False, False], "gather_weights")
    full = {k: shard[k] for k in SMALL}
    full["attn_w_in"] = ga[:, :, :288].transpose(1, 0, 2).reshape(1024, 2304)
    full["dn_w_in"] = ga[:, :, 288:].transpose(1, 0, 2).reshape(1024, 6176)
    full["attn_w_out"] = gb[:, :128].reshape(1024, 1024)
    full["dn_w_out"] = gb[:, 128:].reshape(2048, 1024)
    meta_full = gc[:, :16].transpose(1, 0, 2).reshape(N_META, 1024)
    full["dn_conv_w"] = gc[:, 16:32].reshape(N_DEV, 4, 512).transpose(1, 0, 2).reshape(4, 4096)
    full["dn_norm_w"] = gc[:, 32].reshape(1, 1024)

    seq = x.shape[1]
    h0 = jnp.concatenate([jnp.zeros((FRONT_PAD, D_MODEL), F32), meta_full, x[0]], axis=0)
    target = jnp.concatenate([jnp.zeros((ATTN_BLOCK, D_MODEL), F32), loss_target[0]], axis=0)
    loss, dh0, g = _local_step(h0, target, full)
    grad_x = dh0[ATTN_BLOCK:ATTN_BLOCK + seq][None]
    g["meta_tokens"] = dh0[FRONT_PAD:ATTN_BLOCK]

    pa = jnp.concatenate([g["attn_w_in"].reshape(1024, N_DEV, 288), g["dn_w_in"].reshape(1024, N_DEV, 772)],
                         axis=2).transpose(1, 0, 2)
    pb = jnp.concatenate([g["attn_w_out"].reshape(N_DEV, 128, 1024), g["dn_w_out"].reshape(N_DEV, 256, 1024)], axis=1)
    dn_norm8 = jnp.concatenate([g["dn_norm_w"].reshape(N_DEV, 1, 128), jnp.zeros((N_DEV, 7, 128), F32)], axis=1)
    pc = jnp.concatenate([g["meta_tokens"].reshape(N_META, N_DEV, 128).transpose(1, 0, 2),
                          g["dn_conv_w"].reshape(4, N_DEV, 512).transpose(1, 0, 2).reshape(N_DEV, 16, 128), dn_norm8], axis=1)
    ps = small_of(g, loss)
    xa, xb, xc, xs = _exchange([pa.astype(BF16), pb.astype(BF16), pc, ps], [True, True, True, False], "exchange_grads")

    out = {}
    ma, mb, mc = rows_of(mom_m)
    va, vb, vc = rows_of(mom_v)
    ra = _adamw(xa, wa, ma, va, "adamw_a")
    rb = _adamw(xb, wb, mb, vb, "adamw_b")
    rc = _adamw(xc, wc, mc, vc, "adamw_c")
    zero = jnp.zeros((1, 128), F32)
    rs = _adamw(xs, small_of(shard, zero), small_of(mom_m, zero), small_of(mom_v, zero), "adamw_small")
    row_names = ["attn_w_in", "dn_w_in", "attn_w_out", "dn_w_out", "meta_tokens", "dn_conv_w", "dn_norm_w"]
    lead = {"attn_w_in", "dn_w_in", "attn_w_out", "dn_w_out", "dn_conv_w"}
    for kind in range(4):
        vals = dict(zip(row_names, _unpack_rows(ra[kind], rb[kind], rc[kind])))
        small = _unpack_small(rs[kind])
        vals.update(dict(zip(SMALL, small[:7])))
        if kind == 0:
            loss_total = small[7][0, 0]
        out[kind] = [vals[k][None] if k in lead else vals[k] for k in WEIGHTS]
    return (loss_total, grad_x, *out[0], *out[1], *out[2], *out[3])
```

```python
import functools
import math

import jax
import jax.numpy as jnp
from jax import lax
from jax.experimental import pallas as pl
from jax.experimental.pallas import tpu as pltpu

F32, BF16 = jnp.float32, jnp.bfloat16

D_MODEL = 1024
N_META = 16
NORM_EPS = 1e-6
ATTN_HEADS, ATTN_KV_HEADS, ATTN_GROUPS, ATTN_HD = 16, 2, 8, 64
ATTN_BLOCK = 128
FRONT_PAD = ATTN_BLOCK - N_META
DN_HD, DN_K_HEADS, DN_V_HEADS = 128, 8, 16
DN_CHUNK = 128
TRI_BLOCK = 64
SCAN_BWD_GROUP = 4
DN_KEY_W, DN_VAL_W = 1024, 2048
DN_CONV_W = 2 * DN_KEY_W + DN_VAL_W
DN_CONV_K = 4
N_DEV = 8
ROW_BLOCK = 384
WGRAD_ROWS = 1376
VMEM_LIMIT = 56 * 1024 * 1024
NEG = -1e30

ADAM_LR, ADAM_B1, ADAM_B2, ADAM_EPS, ADAM_WD, ADAM_STEP = 0.001, 0.9, 0.999, 1e-08, 0.01, 10

NT = (((1,), (1,)), ((), ()))
TN = (((0,), (0,)), ((), ()))


def _cparams(sem=("arbitrary",)):
    return pltpu.CompilerParams(dimension_semantics=sem, vmem_limit_bytes=VMEM_LIMIT)


def _rms(x, w):
    return x * lax.rsqrt(jnp.mean(x * x, axis=-1, keepdims=True) + NORM_EPS) * w


def _silu(x):
    return x * jax.nn.sigmoid(x)


def _softplus(x):
    return jnp.maximum(x, 0.0) + jnp.log(1.0 + jnp.exp(-jnp.abs(x)))


NN = (((1,), (0,)), ((), ()))


def _mm(a, b, dims):
    return lax.dot_general(a.astype(BF16), b.astype(BF16), dims, preferred_element_type=F32)


@functools.partial(jax.custom_vjp, nondiff_argnums=(2,))
def _bdot_vjp(a, b, dims):
    return _mm(a, b, dims)


def _bdot_fwd(a, b, dims):
    a16, b16 = a.astype(BF16), b.astype(BF16)
    return _mm(a16, b16, dims), (a16, b16, jnp.zeros((), a.dtype), jnp.zeros((), b.dtype))


def _bdot_bwd(dims, res, g):
    a16, b16, ta, tb = res
    g16 = g.astype(BF16)
    if dims == NN:
        da, db = _mm(g16, b16, NT), _mm(a16, g16, TN)
    elif dims == NT:
        da, db = _mm(g16, b16, NN), _mm(g16, a16, TN)
    else:
        da, db = _mm(b16, g16, NT), _mm(a16, g16, NN)
    return da.astype(ta.dtype), db.astype(tb.dtype)


_bdot_vjp.defvjp(_bdot_fwd, _bdot_bwd)


def _bdot(a, b, dims=NN):
    return _bdot_vjp(a, b, dims)


def _hdot(a, b):
    return jnp.dot(a, b, preferred_element_type=F32, precision=lax.Precision.HIGHEST)


def _row_call(name, body, n_rows, rb, rows, consts, outs, accs=(), reverse=False, scratch=(), halos=()):
    n = n_rows // rb
    assert n * rb == n_rows
    idx = (lambda i: (n - 1 - i, 0)) if reverse else (lambda i: (i, 0))
    in_specs = [pl.BlockSpec((rb, a.shape[1]), idx) for a in rows]
    in_specs += [pl.BlockSpec((hr, a.shape[1]), fn) for a, hr, fn in halos]
    in_specs += [pl.BlockSpec(c.shape, functools.partial(lambda i, nd: (0,) * nd, nd=c.ndim)) for c in consts]
    out_specs = [pl.BlockSpec((rb, c), idx) for c, _ in outs]
    out_specs += [pl.BlockSpec(s, functools.partial(lambda i, nd: (0,) * nd, nd=len(s))) for s, _ in accs]
    out_shape = [jax.ShapeDtypeStruct((n_rows, c), dt) for c, dt in outs]
    out_shape += [jax.ShapeDtypeStruct(s, dt) for s, dt in accs]
    return pl.pallas_call(
        body, grid=(n,), in_specs=in_specs, out_specs=out_specs, out_shape=out_shape,
        scratch_shapes=list(scratch), name=name, compiler_params=_cparams(),
    )(*rows, *[a for a, _, _ in halos], *consts)


def _attn_in_fwd(h0, norm_w, w_in):
    T = h0.shape[0]

    def body(h_ref, nw_ref, w_ref, xn_ref, q_ref, kv_ref, gate_ref):
        xn = _rms(h_ref[...], nw_ref[...]).astype(BF16)
        xn_ref[...] = xn
        q_ref[...] = jnp.dot(xn, w_ref[:, 0:1024], preferred_element_type=F32)
        kv_ref[...] = jnp.dot(xn, w_ref[:, 1024:1280], preferred_element_type=F32)
        gate_ref[...] = jnp.dot(xn, w_ref[:, 1280:2304], preferred_element_type=F32)

    return _row_call("attn_in_fwd", body, T, ROW_BLOCK, [h0], [norm_w, w_in],
                     [(1024, BF16), (1024, F32), (256, F32), (1024, F32)])


def _attn_bias(n, j):
    C, R = 2 * ATTN_BLOCK + N_META, ATTN_GROUPS * ATTN_BLOCK
    c = lax.broadcasted_iota(jnp.int32, (C, R), 0)
    r = lax.broadcasted_iota(jnp.int32, (C, R), 1)
    ql = r & (ATTN_BLOCK - 1)
    is_meta = c >= 2 * ATTN_BLOCK
    dist_band = ATTN_BLOCK + ql - c
    cmin = jnp.maximum(0, 2 * ATTN_BLOCK - ATTN_BLOCK * n)
    valid_band = (c >= cmin) & (dist_band >= 0) & (dist_band < ATTN_BLOCK)
    dist_meta = ATTN_BLOCK * n + ql - FRONT_PAD - (c - 2 * ATTN_BLOCK)
    valid = (is_meta & (dist_meta >= 0)) | (jnp.logical_not(is_meta) & valid_band)
    dist = jnp.minimum(jnp.where(is_meta, dist_meta, dist_band), ATTN_BLOCK).astype(F32)
    rr = lax.broadcasted_iota(jnp.int32, (1, R), 1)
    head = (rr >> 7).astype(F32) + float(ATTN_GROUPS * j + 1)
    slope = jnp.exp(head * (-0.5 * math.log(2.0)))
    return jnp.where(valid, slope * dist, -NEG)


def _attn_tables(n, refresh, bias_ref):
    @pl.when(refresh)
    def _():
        for j in range(ATTN_KV_HEADS):
            bias_ref[j] = _attn_bias(n, j)


def _attn_table_scratch():
    return [pltpu.VMEM((ATTN_KV_HEADS, 2 * ATTN_BLOCK + N_META, ATTN_GROUPS * ATTN_BLOCK), F32)]


def _attn_group(q_t, k, v, sinkrow, qnw_col, knw, bias):
    qn = q_t * lax.rsqrt(jnp.mean(q_t * q_t, axis=0, keepdims=True) + NORM_EPS) * qnw_col
    kn = _rms(k, knw)
    s = _bdot(kn, qn) * (ATTN_HD ** -0.5) - bias
    m = lax.stop_gradient(jnp.maximum(jnp.max(s, axis=0, keepdims=True), sinkrow))
    e = jnp.exp(s - m)
    denom = jnp.sum(e, axis=0, keepdims=True) + jnp.exp(sinkrow - m)
    p = e * (1.0 / denom)
    return _bdot(v, p, TN)


def _sink_row(sinks_ref, j):
    rr = lax.broadcasted_iota(jnp.int32, (1, ATTN_GROUPS * ATTN_BLOCK), 1) >> 7
    row = jnp.zeros((1, ATTN_GROUPS * ATTN_BLOCK), F32)
    for hl in range(ATTN_GROUPS):
        row = jnp.where(rr == hl, sinks_ref[0, ATTN_GROUPS * j + hl], row)
    return row


def _heads_to_lanes(ref, j):
    return jnp.concatenate([ref[:, ATTN_HD * h:ATTN_HD * (h + 1)].T
                            for h in range(ATTN_GROUPS * j, ATTN_GROUPS * (j + 1))], axis=1)


def _lanes_to_heads(ref, j, x_t):
    for hl in range(ATTN_GROUPS):
        h = ATTN_GROUPS * j + hl
        ref[:, ATTN_HD * h:ATTN_HD * (h + 1)] = x_t[:, ATTN_BLOCK * hl:ATTN_BLOCK * (hl + 1)].T


def _attn_kv_tiles(kvp_ref, kvc_ref, kvm_ref, j):
    ksl = slice(ATTN_HD * j, ATTN_HD * (j + 1))
    vsl = slice(128 + ATTN_HD * j, 128 + ATTN_HD * (j + 1))
    k = jnp.concatenate([kvp_ref[:, ksl], kvc_ref[:, ksl], kvm_ref[FRONT_PAD:, ksl]], axis=0)
    v = jnp.concatenate([kvp_ref[:, vsl], kvc_ref[:, vsl], kvm_ref[FRONT_PAD:, vsl]], axis=0)
    return k, v


def _attn_core_fwd(q, kv, sinks, qnw, knw):
    T = q.shape[0]
    nb = T // ATTN_BLOCK

    def body(sinks_ref, q_ref, kvc_ref, kvp_ref, kvm_ref, qnw_ref, knw_ref, o_ref, bias_ref):
        n = pl.program_id(0)
        _attn_tables(n, n <= 2, bias_ref)
        for j in range(ATTN_KV_HEADS):
            k, v = _attn_kv_tiles(kvp_ref, kvc_ref, kvm_ref, j)
            o_t = _attn_group(_heads_to_lanes(q_ref, j), k, v, _sink_row(sinks_ref, j), qnw_ref[...], knw_ref[...],
                              bias_ref[j])
            _lanes_to_heads(o_ref, j, o_t)

    return pl.pallas_call(
        body, grid=(nb,),
        in_specs=[pl.BlockSpec(memory_space=pltpu.SMEM),
                  pl.BlockSpec((ATTN_BLOCK, 1024), lambda i: (i, 0)),
                  pl.BlockSpec((ATTN_BLOCK, 256), lambda i: (i, 0)),
                  pl.BlockSpec((ATTN_BLOCK, 256), lambda i: (jnp.maximum(i - 1, 0), 0)),
                  pl.BlockSpec((ATTN_BLOCK, 256), lambda i: (0, 0)),
                  pl.BlockSpec((ATTN_HD, 1), lambda i: (0, 0)),
                  pl.BlockSpec((1, ATTN_HD), lambda i: (0, 0))],
        out_specs=pl.BlockSpec((ATTN_BLOCK, 1024), lambda i: (i, 0)),
        out_shape=jax.ShapeDtypeStruct((T, 1024), F32),
        scratch_shapes=_attn_table_scratch(),
        name="attn_core_fwd", compiler_params=_cparams(),
    )(sinks, q, kv, kv, kv, qnw.reshape(ATTN_HD, 1), knw)


def _attn_out_fwd(o, gate, h0, w_out):
    T = o.shape[0]

    def body(o_ref, g_ref, h_ref, w_ref, h1_ref):
        og = o_ref[...] * _silu(g_ref[...])
        h1_ref[...] = h_ref[...] + _bdot(og, w_ref[...])

    return _row_call("attn_out_fwd", body, T, ROW_BLOCK, [o, gate, h0], [w_out], [(1024, F32)])[0]


def _wgrad(xn, du, cg, name):
    T, kdim = xn.shape
    cdim = du.shape[1]
    rows = WGRAD_ROWS if T % WGRAD_ROWS == 0 else ROW_BLOCK
    nr, nc = T // rows, cdim // cg
    assert nc * cg == cdim

    def body(x_ref, du_ref, dw_ref):
        @pl.when(pl.program_id(1) == 0)
        def _():
            dw_ref[...] = jnp.zeros_like(dw_ref)
        dw_ref[...] += _bdot(x_ref[...], du_ref[...], TN)

    return pl.pallas_call(
        body, grid=(nc, nr),
        in_specs=[pl.BlockSpec((rows, kdim), lambda j, i: (i, 0)),
                  pl.BlockSpec((rows, cg), lambda j, i: (i, j))],
        out_specs=pl.BlockSpec((kdim, cg), lambda j, i: (0, j)),
        out_shape=jax.ShapeDtypeStruct((kdim, cdim), F32),
        name=name, compiler_params=_cparams(("arbitrary", "arbitrary")),
    )(xn, du)


def _attn_out_bwd(dh1, o, gate, w_out):
    T = o.shape[0]

    def body(dh_ref, o_ref, g_ref, w_ref, do_ref, dg_ref, dw_ref):
        @pl.when(pl.program_id(0) == 0)
        def _():
            dw_ref[...] = jnp.zeros_like(dw_ref)
        dh = dh_ref[...]
        dog = _bdot(dh, w_ref[...], NT)
        og, vjp = jax.vjp(lambda o_, g_: o_ * _silu(g_), o_ref[...], g_ref[...])
        do, dg = vjp(dog)
        do_ref[...] = do
        dg_ref[...] = dg
        dw_ref[...] += _bdot(og, dh, TN)

    return _row_call("attn_out_bwd", body, T, ROW_BLOCK, [dh1, o, gate], [w_out],
                     [(1024, F32), (1024, F32)], [((1024, 1024), F32)])


def _attn_core_bwd(do, q, kv, sinks, qnw, knw):
    T = q.shape[0]
    nb = T // ATTN_BLOCK
    rev = lambda i: nb - 1 - i

    def body(sinks_ref, do_ref, q_ref, kvc_ref, kvp_ref, kvm_ref, qnw_ref, knw_ref,
             dq_ref, dkv_ref, dsinks_ref, dqnw_ref, dknw_ref, carry_ref, meta_ref, bias_ref):
        step = pl.program_id(0)
        n = rev(step)
        _attn_tables(n, (step == 0) | (n <= 1), bias_ref)

        @pl.when(step == 0)
        def _():
            carry_ref[...] = jnp.zeros_like(carry_ref)
            meta_ref[...] = jnp.zeros_like(meta_ref)
            dsinks_ref[...] = jnp.zeros_like(dsinks_ref)
            dqnw_ref[...] = jnp.zeros_like(dqnw_ref)
            dknw_ref[...] = jnp.zeros_like(dknw_ref)

        lane16 = lax.broadcasted_iota(jnp.int32, (1, ATTN_HEADS), 1)
        dsinks = jnp.zeros((1, ATTN_HEADS), F32)
        for j in range(ATTN_KV_HEADS):
            k, v = _attn_kv_tiles(kvp_ref, kvc_ref, kvm_ref, j)
            fn = functools.partial(_attn_group, bias=bias_ref[j])
            _, vjp = jax.vjp(fn, _heads_to_lanes(q_ref, j), k, v, _sink_row(sinks_ref, j), qnw_ref[...], knw_ref[...])
            dq_t, dk, dv, dsr, dqn, dkn = vjp(_heads_to_lanes(do_ref, j))
            _lanes_to_heads(dq_ref, j, dq_t)
            dqnw_ref[...] += dqn
            dknw_ref[...] += dkn
            for hl in range(ATTN_GROUPS):
                dsinks = dsinks + jnp.where(lane16 == ATTN_GROUPS * j + hl,
                                            jnp.sum(dsr[:, ATTN_BLOCK * hl:ATTN_BLOCK * (hl + 1)]), 0.0)
            ksl = slice(ATTN_HD * j, ATTN_HD * (j + 1))
            vsl = slice(128 + ATTN_HD * j, 128 + ATTN_HD * (j + 1))
            for sl, d in ((ksl, dk), (vsl, dv)):
                dkv_ref[:, sl] = d[ATTN_BLOCK:2 * ATTN_BLOCK, :] + carry_ref[:, sl]
                carry_ref[:, sl] = d[0:ATTN_BLOCK, :]
                meta_ref[:, sl] += d[2 * ATTN_BLOCK:, :]
        dsinks_ref[...] += dsinks

        @pl.when(n == 0)
        def _():
            dkv_ref[FRONT_PAD:, :] += meta_ref[...]

    dq, dkv, dsinks, dqnw, dknw = pl.pallas_call(
        body, grid=(nb,),
        in_specs=[pl.BlockSpec(memory_space=pltpu.SMEM),
                  pl.BlockSpec((ATTN_BLOCK, 1024), lambda i: (rev(i), 0)),
                  pl.BlockSpec((ATTN_BLOCK, 1024), lambda i: (rev(i), 0)),
                  pl.BlockSpec((ATTN_BLOCK, 256), lambda i: (rev(i), 0)),
                  pl.BlockSpec((ATTN_BLOCK, 256), lambda i: (jnp.maximum(rev(i) - 1, 0), 0)),
                  pl.BlockSpec((ATTN_BLOCK, 256), lambda i: (0, 0)),
                  pl.BlockSpec((ATTN_HD, 1), lambda i: (0, 0)),
                  pl.BlockSpec((1, ATTN_HD), lambda i: (0, 0))],
        out_specs=[pl.BlockSpec((ATTN_BLOCK, 1024), lambda i: (rev(i), 0)),
                   pl.BlockSpec((ATTN_BLOCK, 256), lambda i: (rev(i), 0)),
                   pl.BlockSpec((1, ATTN_HEADS), lambda i: (0, 0)),
                   pl.BlockSpec((ATTN_HD, 1), lambda i: (0, 0)),
                   pl.BlockSpec((1, ATTN_HD), lambda i: (0, 0))],
        out_shape=[jax.ShapeDtypeStruct((T, 1024), F32), jax.ShapeDtypeStruct((T, 256), F32),
                   jax.ShapeDtypeStruct((1, ATTN_HEADS), F32), jax.ShapeDtypeStruct((ATTN_HD, 1), F32),
                   jax.ShapeDtypeStruct((1, ATTN_HD), F32)],
        scratch_shapes=[pltpu.VMEM((ATTN_BLOCK, 256), F32), pltpu.VMEM((N_META, 256), F32)] + _attn_table_scratch(),
        name="attn_core_bwd", compiler_params=_cparams(),
    )(sinks, do, q, kv, kv, kv, qnw.reshape(ATTN_HD, 1), knw)
    return dq, dkv, dsinks, dqnw.reshape(1, ATTN_HD), dknw


def _attn_in_bwd(dq, dkv, dgate, h0, dh1, norm_w, w_in):
    T = h0.shape[0]

    def body(dq_ref, dkv_ref, dg_ref, h_ref, dh1_ref, nw_ref, w_ref, dh0_ref, dnw_ref):
        @pl.when(pl.program_id(0) == 0)
        def _():
            dnw_ref[...] = jnp.zeros_like(dnw_ref)
        dxn = (_bdot(dq_ref[...], w_ref[:, 0:1024], NT) + _bdot(dkv_ref[...], w_ref[:, 1024:1280], NT)
               + _bdot(dg_ref[...], w_ref[:, 1280:2304], NT))
        _, vjp = jax.vjp(_rms, h_ref[...], nw_ref[...])
        dh, dnw = vjp(dxn)
        dh0_ref[...] = dh1_ref[...] + dh
        dnw_ref[...] += dnw

    return _row_call("attn_in_bwd", body, T, ROW_BLOCK, [dq, dkv, dgate, h0, dh1], [norm_w, w_in],
                     [(1024, F32)], [((1, 1024), F32)])


def _dn_in_fwd(h1, norm_w, w_in):
    T = h1.shape[0]

    def body(h_ref, nw_ref, w_ref, xn_ref, qkv_ref, z_ref, ba_ref):
        xn = _rms(h_ref[...], nw_ref[...]).astype(BF16)
        xn_ref[...] = xn
        qkv_ref[...] = jnp.dot(xn, w_ref[:, 0:4096], preferred_element_type=F32)
        z_ref[...] = jnp.dot(xn, w_ref[:, 4096:6144], preferred_element_type=F32)
        ba_ref[...] = jnp.dot(xn, w_ref[:, 6144:6176], preferred_element_type=F32)

    return _row_call("dn_in_fwd", body, T, ROW_BLOCK, [h1], [norm_w, w_in],
                     [(1024, BF16), (4096, F32), (2048, F32), (32, F32)])


def _shift_down(cur, prev8, s):
    i8 = lax.broadcasted_iota(jnp.int32, (8, cur.shape[1]), 0)
    r = pltpu.roll(cur, s, 0)
    head = jnp.where(i8 < s, pltpu.roll(prev8, s, 0), r[0:8])
    return jnp.concatenate([head, r[8:]], axis=0)


def _shift_up(cur, next8, s):
    n = cur.shape[0]
    i8 = lax.broadcasted_iota(jnp.int32, (8, cur.shape[1]), 0)
    r = pltpu.roll(cur, n - s, 0)
    tail = jnp.where(i8 >= 8 - s, pltpu.roll(next8, 8 - s, 0), r[n - 8:])
    return jnp.concatenate([r[:n - 8], tail], axis=0)


def _conv_taps(cur, prev8):
    return [cur] + [_shift_down(cur, prev8, s) for s in range(1, DN_CONV_K)]


def _conv_tile(taps, w):
    out = w[3:4, :] * taps[0]
    for s in range(1, DN_CONV_K):
        out = out + w[3 - s:4 - s, :] * taps[s]
    return out


def _l2n(a, scale):
    return a * (lax.rsqrt(jnp.sum(a * a, axis=-1, keepdims=True) + NORM_EPS) * scale)


def _dn_post_tile(c, t):
    a = _silu(c)
    if t < DN_K_HEADS:
        return _l2n(a, DN_HD ** -0.5)
    if t < 2 * DN_K_HEADS:
        return _l2n(a, 1.0)
    return a


def _dn_beta_g(ba, a_log, dt_bias, live):
    beta = jax.nn.sigmoid(ba[:, 0:DN_V_HEADS]) * live
    g = -jnp.exp(a_log) * _softplus(ba[:, DN_V_HEADS:] + dt_bias) * live
    return beta, g


def _live_rows(i, rb):
    rows = i * rb + lax.broadcasted_iota(jnp.int32, (rb, 1), 0)
    return (rows >= FRONT_PAD).astype(F32)


def _halo_spec_args(x, rb):
    per = rb // 8
    return (x, 8, lambda i: (jnp.maximum(i * per - 1, 0), 0))


def _dn_conv_fwd(qkv, ba, conv_w, a_log, dt_bias):
    T = qkv.shape[0]

    def body(x_ref, ba_ref, halo_ref, cw_ref, al_ref, dtb_ref, q_ref, k_ref, v_ref, bg_ref):
        i = pl.program_id(0)
        first = (i > 0).astype(F32)
        for t in range(DN_CONV_W // 128):
            cols = slice(128 * t, 128 * (t + 1))
            c = _conv_tile(_conv_taps(x_ref[:, cols], halo_ref[:, cols] * first), cw_ref[:, cols])
            out = _dn_post_tile(c, t)
            if t < DN_K_HEADS:
                q_ref[:, cols] = out
            elif t < 2 * DN_K_HEADS:
                k_ref[:, 128 * (t - 8):128 * (t - 7)] = out
            else:
                v_ref[:, 128 * (t - 16):128 * (t - 15)] = out
        beta, g = _dn_beta_g(ba_ref[...], al_ref[...], dtb_ref[...], _live_rows(i, ROW_BLOCK))
        bg_ref[:, 0:DN_V_HEADS] = beta
        bg_ref[:, DN_V_HEADS:] = g

    return _row_call("dn_conv_fwd", body, T, ROW_BLOCK, [qkv, ba], [conv_w, a_log, dt_bias],
                     [(1024, F32), (1024, F32), (2048, F32), (32, F32)], halos=[_halo_spec_args(qkv, ROW_BLOCK)])


def _chunk_masks():
    r = lax.broadcasted_iota(jnp.int32, (DN_CHUNK, DN_CHUNK), 0)
    c = lax.broadcasted_iota(jnp.int32, (DN_CHUNK, DN_CHUNK), 1)
    return r >= c, r > c, r == c, r <= c


def _tri_inv_block(x):
    B = TRI_BLOCK
    n = range(len(x))
    r_, c_ = lax.broadcasted_iota(jnp.int32, (B, B), 0), lax.broadcasted_iota(jnp.int32, (B, B), 1)
    ainv = [jnp.where(r_ == c_, 1.0, 0.0) + x[h] for h in n]
    p = [_bdot(x[h], x[h]) for h in n]
    for _ in range(B.bit_length() - 3):
        r = [_bdot(jnp.concatenate([p[h], ainv[h]], axis=0), p[h]) for h in n]
        ainv = [ainv[h] + r[h][B:] for h in n]
        p = [r[h][:B] for h in n]
    return [ainv[h] + _bdot(ainv[h], p[h]) for h in n]


def _tri_inv(x):
    B = TRI_BLOCK
    assert DN_CHUNK == 2 * B
    n = len(x)
    diag = _tri_inv_block([x[h][:B, :B] for h in range(n)] + [x[h][B:, B:] for h in range(n)])
    a11, a22 = diag[:n], diag[n:]
    a21 = [_bdot(_bdot(a22[h], x[h][B:, :B]), a11[h]) for h in range(n)]
    zero = jnp.zeros((B, B), F32)
    return [jnp.concatenate([jnp.concatenate([a11[h], zero], axis=1), jnp.concatenate([a21[h], a22[h]], axis=1)], axis=0)
            for h in range(n)]


@jax.custom_vjp
def _tri_inv_known(x, a):
    return a


def _tri_inv_known_fwd(x, a):
    return a, a


def _tri_inv_known_bwd(a, da):
    return [_bdot(_bdot(a[h], da[h], TN), a[h], NT) for h in range(len(a))], [jnp.zeros_like(t) for t in a]


_tri_inv_known.defvjp(_tri_inv_known_fwd, _tri_inv_known_bwd)


def _dn_chunk_step(S, q, k, v, beta, g, masks, known_inv=None, with_inv=False):
    causal, strict, eye, upper = masks
    C, W = DN_CHUNK, DN_HD
    heads = range(len(v))
    k_t = [k[j].T for j in range(len(k))]
    qk_kk = [_bdot(jnp.concatenate([q[j], k[j]], axis=0), k_t[j]) for j in range(len(q))]
    g_b = [jnp.broadcast_to(g[h], (C, C)) for h in heads]
    beta_b = [jnp.broadcast_to(beta[h], (C, W)) for h in heads]
    g_row = [jnp.sum(jnp.where(eye, g_b[h], 0.0), axis=0, keepdims=True) for h in heads]
    gc_col = [jnp.sum(jnp.where(causal, g_row[h], 0.0), axis=1, keepdims=True) for h in heads]
    gc_row = [jnp.sum(jnp.where(upper, g_b[h], 0.0), axis=0, keepdims=True) for h in heads]
    g_last = [jnp.sum(g_row[h], axis=1, keepdims=True) for h in heads]
    gc_b = [jnp.broadcast_to(gc_col[h], (C, W)) for h in heads]
    decay = [jnp.exp(jnp.where(causal, gc_b[h][:, :C] - gc_row[h], NEG)) for h in heads]
    eg_b = [jnp.exp(gc_b[h]) for h in heads]
    x = [jnp.where(strict, qk_kk[h // 2][C:] * beta_b[h][:, :C] * decay[h], 0.0) * -1.0 for h in heads]
    ainv = _tri_inv(x) if known_inv is None else _tri_inv_known(x, known_inv)
    uw = [_bdot(ainv[h], jnp.concatenate([v[h] * beta_b[h], k[h // 2] * (beta_b[h] * eg_b[h])], axis=1)) for h in heads]
    ws_qs = [_bdot(jnp.concatenate([uw[h][:, W:], q[h // 2] * eg_b[h]], axis=0), S[h]) for h in heads]
    v_new = [uw[h][:, :W] - ws_qs[h][:C] for h in heads]
    o = [ws_qs[h][C:] + _bdot(qk_kk[h // 2][:C] * decay[h], v_new[h]) for h in heads]
    s_new = [S[h] * jnp.exp(g_last[h]) + _bdot(k_t[h // 2] * jnp.exp(g_last[h] - gc_row[h]), v_new[h]) for h in heads]
    return (s_new, o, ainv) if with_inv else (s_new, o)


def _dn_chunk_tiles(q_ref, k_ref, v_ref, bg_ref):
    q = [q_ref[:, 128 * j:128 * (j + 1)] for j in range(DN_K_HEADS)]
    k = [k_ref[:, 128 * j:128 * (j + 1)] for j in range(DN_K_HEADS)]
    v = [v_ref[:, 128 * h:128 * (h + 1)] for h in range(DN_V_HEADS)]
    beta = [bg_ref[:, h:h + 1] for h in range(DN_V_HEADS)]
    g = [bg_ref[:, DN_V_HEADS + h:DN_V_HEADS + h + 1] for h in range(DN_V_HEADS)]
    return q, k, v, beta, g


def _dn_scan_fwd(qn, kn, v, bg):
    T = qn.shape[0]
    nc = T // DN_CHUNK

    def body(q_ref, k_ref, v_ref, bg_ref, o_ref, ssave_ref, inv_ref, s_ref):
        @pl.when(pl.program_id(0) == 0)
        def _():
            s_ref[...] = jnp.zeros_like(s_ref)
        s_old = [s_ref[h] for h in range(DN_V_HEADS)]
        for h in range(DN_V_HEADS):
            ssave_ref[0, h] = s_old[h]
        s_new, o, ainv = _dn_chunk_step(s_old, *_dn_chunk_tiles(q_ref, k_ref, v_ref, bg_ref), _chunk_masks(),
                                        with_inv=True)
        for h in range(DN_V_HEADS):
            o_ref[:, 128 * h:128 * (h + 1)] = o[h]
            inv_ref[0, h] = ainv[h].astype(BF16)
            s_ref[h] = s_new[h]

    return pl.pallas_call(
        body, grid=(nc,),
        in_specs=[pl.BlockSpec((DN_CHUNK, 1024), lambda i: (i, 0)),
                  pl.BlockSpec((DN_CHUNK, 1024), lambda i: (i, 0)),
                  pl.BlockSpec((DN_CHUNK, 2048), lambda i: (i, 0)),
                  pl.BlockSpec((DN_CHUNK, 32), lambda i: (i, 0))],
        out_specs=[pl.BlockSpec((DN_CHUNK, 2048), lambda i: (i, 0)),
                   pl.BlockSpec((1, DN_V_HEADS, DN_HD, DN_HD), lambda i: (i, 0, 0, 0)),
                   pl.BlockSpec((1, DN_V_HEADS, DN_CHUNK, DN_CHUNK), lambda i: (i, 0, 0, 0))],
        out_shape=[jax.ShapeDtypeStruct((T, 2048), F32),
                   jax.ShapeDtypeStruct((nc, DN_V_HEADS, DN_HD, DN_HD), F32),
                   jax.ShapeDtypeStruct((nc, DN_V_HEADS, DN_CHUNK, DN_CHUNK), BF16)],
        scratch_shapes=[pltpu.VMEM((DN_V_HEADS, DN_HD, DN_HD), F32)],
        name="dn_scan_fwd", compiler_params=_cparams(),
    )(qn, kn, v, bg)


def _dn_gate_tile(o, z, onw):
    return _rms(o, onw) * _silu(z)


def _dn_out_fwd(o, z, h1, target, w_out, onw):
    T = o.shape[0]

    def body(o_ref, z_ref, h_ref, t_ref, w_ref, onw_ref, dy_ref, og_ref, loss_ref):
        i = pl.program_id(0)

        @pl.when(i == 0)
        def _():
            loss_ref[...] = jnp.zeros_like(loss_ref)
        for h in range(DN_V_HEADS):
            cols = slice(128 * h, 128 * (h + 1))
            og_ref[:, cols] = _dn_gate_tile(o_ref[:, cols], z_ref[:, cols], onw_ref[...]).astype(BF16)
        y = h_ref[...] + jnp.dot(og_ref[...], w_ref[...], preferred_element_type=F32)
        rows = i * ROW_BLOCK + lax.broadcasted_iota(jnp.int32, (ROW_BLOCK, 1), 0)
        diff = jnp.where(rows >= FRONT_PAD + N_META, y - t_ref[...], 0.0)
        dy_ref[...] = diff * (1.0 / D_MODEL)
        loss_ref[...] += jnp.sum(diff * diff) * (0.5 / D_MODEL)

    return _row_call("dn_out_fwd", body, T, ROW_BLOCK, [o, z, h1, target], [w_out, onw],
                     [(1024, F32), (2048, BF16)], [((1, 128), F32)])


def _dn_out_bwd(dy, o, z, w_out, onw):
    T = o.shape[0]

    def body(dy_ref, o_ref, z_ref, w_ref, onw_ref, do_ref, dz_ref, donw_ref):
        @pl.when(pl.program_id(0) == 0)
        def _():
            donw_ref[...] = jnp.zeros_like(donw_ref)
        dy = dy_ref[...].astype(BF16)
        donw = jnp.zeros((1, DN_HD), F32)
        for h in range(DN_V_HEADS):
            cols = slice(128 * h, 128 * (h + 1))
            dog = lax.dot_general(dy, w_ref[cols, :], NT, preferred_element_type=F32)
            _, vjp = jax.vjp(_dn_gate_tile, o_ref[:, cols], z_ref[:, cols], onw_ref[...])
            do, dz, dn = vjp(dog)
            do_ref[:, cols] = do
            dz_ref[:, cols] = dz
            donw = donw + dn
        donw_ref[...] += donw

    return _row_call("dn_out_bwd", body, T, ROW_BLOCK, [dy, o, z], [w_out, onw],
                     [(2048, F32), (2048, F32)], [((1, DN_HD), F32)])


def _dn_scan_bwd(do, qn, kn, v, bg, ssave, inv):
    T = qn.shape[0]
    nc = T // DN_CHUNK
    rev = lambda i: nc - 1 - i

    def body(do_ref, q_ref, k_ref, v_ref, bg_ref, ss_ref, inv_ref, dq_ref, dk_ref, dv_ref, dbg_ref, ds_ref):
        @pl.when(pl.program_id(0) == 0)
        def _():
            ds_ref[...] = jnp.zeros_like(ds_ref)
        lane32 = lax.broadcasted_iota(jnp.int32, (1, 2 * DN_V_HEADS), 1)
        masks = _chunk_masks()
        q, k, v, beta, g = _dn_chunk_tiles(q_ref, k_ref, v_ref, bg_ref)
        dbg = jnp.zeros((DN_CHUNK, 2 * DN_V_HEADS), F32)
        for first in range(0, DN_V_HEADS, SCAN_BWD_GROUP):
            heads = range(first, first + SCAN_BWD_GROUP)
            pairs = slice(first // 2, (first + SCAN_BWD_GROUP) // 2)
            hs = slice(first, first + SCAN_BWD_GROUP)
            fn = functools.partial(_dn_chunk_step, masks=masks, known_inv=[inv_ref[0, h].astype(F32) for h in heads])
            _, vjp = jax.vjp(fn, [ss_ref[0, h] for h in heads], q[pairs], k[pairs], v[hs], beta[hs], g[hs])
            ds, dq, dk, dv, dbeta, dg = vjp(([ds_ref[h] for h in heads], [do_ref[:, 128 * h:128 * (h + 1)] for h in heads]))
            for i, h in enumerate(heads):
                ds_ref[h] = ds[i]
                dv_ref[:, 128 * h:128 * (h + 1)] = dv[i]
                dbg = dbg + jnp.where(lane32 == h, dbeta[i], 0.0) + jnp.where(lane32 == DN_V_HEADS + h, dg[i], 0.0)
            for i, j in enumerate(range(first // 2, (first + SCAN_BWD_GROUP) // 2)):
                dq_ref[:, 128 * j:128 * (j + 1)] = dq[i]
                dk_ref[:, 128 * j:128 * (j + 1)] = dk[i]
        dbg_ref[...] = dbg

    return pl.pallas_call(
        body, grid=(nc,),
        in_specs=[pl.BlockSpec((DN_CHUNK, 2048), lambda i: (rev(i), 0)),
                  pl.BlockSpec((DN_CHUNK, 1024), lambda i: (rev(i), 0)),
                  pl.BlockSpec((DN_CHUNK, 1024), lambda i: (rev(i), 0)),
                  pl.BlockSpec((DN_CHUNK, 2048), lambda i: (rev(i), 0)),
                  pl.BlockSpec((DN_CHUNK, 32), lambda i: (rev(i), 0)),
                  pl.BlockSpec((1, DN_V_HEADS, DN_HD, DN_HD), lambda i: (rev(i), 0, 0, 0)),
                  pl.BlockSpec((1, DN_V_HEADS, DN_CHUNK, DN_CHUNK), lambda i: (rev(i), 0, 0, 0))],
        out_specs=[pl.BlockSpec((DN_CHUNK, 1024), lambda i: (rev(i), 0)),
                   pl.BlockSpec((DN_CHUNK, 1024), lambda i: (rev(i), 0)),
                   pl.BlockSpec((DN_CHUNK, 2048), lambda i: (rev(i), 0)),
                   pl.BlockSpec((DN_CHUNK, 32), lambda i: (rev(i), 0))],
        out_shape=[jax.ShapeDtypeStruct((T, 1024), F32), jax.ShapeDtypeStruct((T, 1024), F32),
                   jax.ShapeDtypeStruct((T, 2048), F32), jax.ShapeDtypeStruct((T, 32), F32)],
        scratch_shapes=[pltpu.VMEM((DN_V_HEADS, DN_HD, DN_HD), F32)],
        name="dn_scan_bwd", compiler_params=_cparams(),
    )(do, qn, kn, v, bg, ssave, inv)


def _dn_conv_bwd(dqn, dkn, dv, dbg, qkv, ba, conv_w, a_log, dt_bias):
    T = qkv.shape[0]
    nr = T // ROW_BLOCK

    def body(dq_ref, dk_ref, dv_ref, dbg_ref, x_ref, ba_ref, halo_ref, cw_ref, al_ref, dtb_ref,
             dx_ref, dba_ref, dcw_ref, dal_ref, ddtb_ref, carry_ref):
        step = pl.program_id(0)
        i = nr - 1 - step

        @pl.when(step == 0)
        def _():
            carry_ref[...] = jnp.zeros_like(carry_ref)
            dcw_ref[...] = jnp.zeros_like(dcw_ref)
            dal_ref[...] = jnp.zeros_like(dal_ref)
            ddtb_ref[...] = jnp.zeros_like(ddtb_ref)
        first = (i > 0).astype(F32)
        for t in range(DN_CONV_W // 128):
            cols = slice(128 * t, 128 * (t + 1))
            w = cw_ref[:, cols]
            taps = _conv_taps(x_ref[:, cols], halo_ref[:, cols] * first)
            c = _conv_tile(taps, w)
            if t < DN_K_HEADS:
                dout = dq_ref[:, cols]
            elif t < 2 * DN_K_HEADS:
                dout = dk_ref[:, 128 * (t - 8):128 * (t - 7)]
            else:
                dout = dv_ref[:, 128 * (t - 16):128 * (t - 15)]
            _, vjp = jax.vjp(functools.partial(_dn_post_tile, t=t), c)
            (dc,) = vjp(dout)
            nxt = carry_ref[:, cols]
            dx = w[3:4, :] * dc
            dcw_ref[3:4, cols] += jnp.sum(dc * taps[0], axis=0, keepdims=True)
            for s in range(1, DN_CONV_K):
                dx = dx + w[3 - s:4 - s, :] * _shift_up(dc, nxt, s)
                dcw_ref[3 - s:4 - s, cols] += jnp.sum(dc * taps[s], axis=0, keepdims=True)
            dx_ref[:, cols] = dx
            carry_ref[:, cols] = dc[0:8, :]
        fn = functools.partial(_dn_beta_g, live=_live_rows(i, ROW_BLOCK))
        _, vjp = jax.vjp(fn, ba_ref[...], al_ref[...], dtb_ref[...])
        dba, dal, ddtb = vjp((dbg_ref[:, 0:DN_V_HEADS], dbg_ref[:, DN_V_HEADS:]))
        dba_ref[...] = dba
        dal_ref[...] += dal
        ddtb_ref[...] += ddtb

    per = ROW_BLOCK // 8
    halo = (qkv, 8, lambda s: (jnp.maximum((nr - 1 - s) * per - 1, 0), 0))
    return _row_call("dn_conv_bwd", body, T, ROW_BLOCK, [dqn, dkn, dv, dbg, qkv, ba], [conv_w, a_log, dt_bias],
                     [(4096, F32), (32, F32)], [((DN_CONV_K, 4096), F32), ((1, DN_V_HEADS), F32), ((1, DN_V_HEADS), F32)],
                     reverse=True, scratch=[pltpu.VMEM((8, 4096), F32)], halos=[halo])


def _dn_in_bwd(dqkv, dz, dba, h1, dy, norm_w, w_in):
    T = h1.shape[0]

    def body(dqkv_ref, dz_ref, dba_ref, h_ref, dy_ref, nw_ref, w_ref, dh_ref, dnw_ref):
        @pl.when(pl.program_id(0) == 0)
        def _():
            dnw_ref[...] = jnp.zeros_like(dnw_ref)
        dxn = (_bdot(dqkv_ref[...], w_ref[:, 0:4096], NT) + _bdot(dz_ref[...], w_ref[:, 4096:6144], NT)
               + _bdot(dba_ref[...], w_ref[:, 6144:6176], NT))
        _, vjp = jax.vjp(_rms, h_ref[...], nw_ref[...])
        dh, dnw = vjp(dxn)
        dh_ref[...] = (dy_ref[...] + dh) * _live_rows(pl.program_id(0), ROW_BLOCK)
        dnw_ref[...] += dnw

    return _row_call("dn_in_bwd", body, T, ROW_BLOCK, [dqkv, dz, dba, h1, dy], [norm_w, w_in],
                     [(1024, F32)], [((1, 1024), F32)])


def _exchange(parts, scatter, name):
    n = len(parts)
    out_shape = [jax.ShapeDtypeStruct(p.shape if sc else (N_DEV,) + p.shape, p.dtype) for p, sc in zip(parts, scatter)]

    def body(*refs):
        ins, outs = refs[:n], refs[n:2 * n]
        send_sems, recv_sems, local_sems = refs[2 * n:]
        x, y, c = lax.axis_index("x"), lax.axis_index("y"), lax.axis_index("c")
        me = 4 * x + 2 * y + c
        peers = []
        for k in range(1, N_DEV):
            px = 1 - x if k & 4 else x
            py = 1 - y if k & 2 else y
            pc = 1 - c if k & 1 else c
            peers.append(((px, py, pc), 4 * px + 2 * py + pc))

        def src(a, idx):
            return ins[a].at[idx] if scatter[a] else ins[a]

        local = [pltpu.make_async_copy(src(a, me), outs[a].at[me], local_sems.at[a]) for a in range(n)]
        for cp in local:
            cp.start()
        for a in range(n):
            for k, (dev, idx) in enumerate(peers):
                pltpu.make_async_remote_copy(
                    src_ref=src(a, idx), dst_ref=outs[a].at[me], send_sem=send_sems.at[a, k], recv_sem=recv_sems.at[a, k],
                    device_id=dev, device_id_type=pl.DeviceIdType.MESH).start()
        for a in range(n):
            for k, (dev, idx) in enumerate(peers):
                pltpu.make_async_remote_copy(
                    src_ref=src(a, idx), dst_ref=outs[a].at[idx], send_sem=send_sems.at[a, k], recv_sem=recv_sems.at[a, k],
                    device_id=dev, device_id_type=pl.DeviceIdType.MESH).wait()
        for cp in local:
            cp.wait()

    hbm = pl.BlockSpec(memory_space=pltpu.HBM)
    return pl.pallas_call(
        body, out_shape=out_shape, in_specs=[hbm] * n, out_specs=[hbm] * n,
        scratch_shapes=[pltpu.SemaphoreType.DMA((n, N_DEV - 1)), pltpu.SemaphoreType.DMA((n, N_DEV - 1)),
                        pltpu.SemaphoreType.DMA((n,))],
        name=name,
    )(*parts)


def _adam_rows(rows):
    for rb in (128, 64, 40, 16, 8):
        if rows % rb == 0:
            return rb
    return rows


def _adamw(stack, w, m, v, name):
    R, C = w.shape
    rb = _adam_rows(R)

    def body(s_ref, w_ref, m_ref, v_ref, g_ref, d_ref, nm_ref, nv_ref):
        g = s_ref[0].astype(F32)
        for s in range(1, N_DEV):
            g = g + s_ref[s].astype(F32)
        nm = ADAM_B1 * m_ref[...] + (1.0 - ADAM_B1) * g
        nv = ADAM_B2 * v_ref[...] + (1.0 - ADAM_B2) * (g * g)
        m_hat = nm / (1.0 - ADAM_B1 ** ADAM_STEP)
        v_hat = nv / (1.0 - ADAM_B2 ** ADAM_STEP)
        g_ref[...] = g
        d_ref[...] = -ADAM_LR * (m_hat / (jnp.sqrt(v_hat) + ADAM_EPS) + ADAM_WD * w_ref[...])
        nm_ref[...] = nm
        nv_ref[...] = nv

    blk = pl.BlockSpec((rb, C), lambda i: (i, 0))
    return pl.pallas_call(
        body, grid=(R // rb,),
        in_specs=[pl.BlockSpec((N_DEV, rb, C), lambda i: (0, i, 0)), blk, blk, blk],
        out_specs=[blk] * 4, out_shape=[jax.ShapeDtypeStruct((R, C), F32)] * 4,
        name=name, compiler_params=_cparams(),
    )(stack, w, m, v)


def _pad_rows8(a):
    return jnp.concatenate([a, jnp.zeros((8 - a.shape[0], a.shape[1]), a.dtype)], axis=0) if a.shape[0] < 8 else a


def _pack_small(norm_w, qnw, knw, sinks, a_log, dt_bias, onw, extra):
    z = lambda n: jnp.zeros((1, n), F32)
    row = jnp.concatenate([norm_w, qnw, knw, sinks, a_log, dt_bias, z(80), onw, extra, z(512)], axis=1)
    return row.reshape(16, 128)


def _unpack_small(p):
    row = p.reshape(1, 2048)
    cut = lambda a, n: row[:, a:a + n]
    return (cut(0, 1024), cut(1024, 64), cut(1088, 64), cut(1152, 16), cut(1168, 16), cut(1184, 16), cut(1280, 128),
            cut(1408, 128))


def _pack_rows(w_in_a, w_in_d, w_out_a, w_out_d, meta, conv, dn_norm):
    a = jnp.concatenate([w_in_a, w_in_d], axis=1)
    b = jnp.concatenate([w_out_a, w_out_d], axis=0)
    c = jnp.concatenate([meta, conv.reshape(16, 128), _pad_rows8(dn_norm)], axis=0)
    return a, b, c


def _unpack_rows(a, b, c):
    return (a[:, :288], a[:, 288:], b[:128], b[128:], c[:16], c[16:32].reshape(4, 512), c[32:33])


def _local_step(h0, target, w):
    xn0, q, kv, gate = _attn_in_fwd(h0, w["attn_norm_w"], w["attn_w_in"])
    o = _attn_core_fwd(q, kv, w["attn_sinks"], w["attn_q_norm_w"], w["attn_k_norm_w"])
    h1 = _attn_out_fwd(o, gate, h0, w["attn_w_out"])
    xn1, qkv, z, ba = _dn_in_fwd(h1, w["dn_norm_w"], w["dn_w_in"])
    qn, kn, v, bg = _dn_conv_fwd(qkv, ba, w["dn_conv_w"], w["dn_a_log"], w["dn_dt_bias"])
    o_dn, ssave, inv = _dn_scan_fwd(qn, kn, v, bg)
    dy, og_dn, loss = _dn_out_fwd(o_dn, z, h1, target, w["dn_w_out"], w["dn_o_norm_w"])

    g = {}
    do_dn, dz, g["dn_o_norm_w"] = _dn_out_bwd(dy, o_dn, z, w["dn_w_out"], w["dn_o_norm_w"])
    g["dn_w_out"] = _wgrad(og_dn, dy, 1024, "wgrad_dn_out")
    dqn, dkn, dv, dbg = _dn_scan_bwd(do_dn, qn, kn, v, bg, ssave, inv)
    dqkv, dba, g["dn_conv_w"], g["dn_a_log"], g["dn_dt_bias"] = _dn_conv_bwd(
        dqn, dkn, dv, dbg, qkv, ba, w["dn_conv_w"], w["dn_a_log"], w["dn_dt_bias"])
    dh1, g["dn_norm_w"] = _dn_in_bwd(dqkv, dz, dba, h1, dy, w["dn_norm_w"], w["dn_w_in"])
    g["dn_w_in"] = jnp.concatenate([_wgrad(xn1, dqkv, 1024, "wgrad_dn_qkv"), _wgrad(xn1, dz, 1024, "wgrad_dn_z"),
                                    _wgrad(xn1, dba, 32, "wgrad_dn_ba")], axis=1)
    do, dgate, g["attn_w_out"] = _attn_out_bwd(dh1, o, gate, w["attn_w_out"])
    dq, dkv, g["attn_sinks"], g["attn_q_norm_w"], g["attn_k_norm_w"] = _attn_core_bwd(
        do, q, kv, w["attn_sinks"], w["attn_q_norm_w"], w["attn_k_norm_w"])
    dh0, g["attn_norm_w"] = _attn_in_bwd(dq, dkv, dgate, h0, dh1, w["attn_norm_w"], w["attn_w_in"])
    g["attn_w_in"] = jnp.concatenate([_wgrad(xn0, dq, 1024, "wgrad_attn_q"), _wgrad(xn0, dkv, 256, "wgrad_attn_kv"),
                                      _wgrad(xn0, dgate, 1024, "wgrad_attn_gate")], axis=1)
    return loss, dh0, g


WEIGHTS = ['meta_tokens', 'attn_norm_w', 'attn_w_in', 'attn_q_norm_w', 'attn_k_norm_w', 'attn_sinks', 'attn_w_out',
           'dn_norm_w', 'dn_w_in', 'dn_conv_w', 'dn_a_log', 'dn_dt_bias', 'dn_o_norm_w', 'dn_w_out']
SMALL = ['attn_norm_w', 'attn_q_norm_w', 'attn_k_norm_w', 'attn_sinks', 'dn_a_log', 'dn_dt_bias', 'dn_o_norm_w']


def kernel(x, meta_tokens, attn_norm_w, attn_w_in, attn_q_norm_w, attn_k_norm_w, attn_sinks, attn_w_out, dn_norm_w, dn_w_in, dn_conv_w, dn_a_log, dn_dt_bias, dn_o_norm_w, dn_w_out, loss_target, m_meta_tokens, m_attn_norm_w, m_attn_w_in, m_attn_q_norm_w, m_attn_k_norm_w, m_attn_sinks, m_attn_w_out, m_dn_norm_w, m_dn_w_in, m_dn_conv_w, m_dn_a_log, m_dn_dt_bias, m_dn_o_norm_w, m_dn_w_out, v_meta_tokens, v_attn_norm_w, v_attn_w_in, v_attn_q_norm_w, v_attn_k_norm_w, v_attn_sinks, v_attn_w_out, v_dn_norm_w, v_dn_w_in, v_dn_conv_w, v_dn_a_log, v_dn_dt_bias, v_dn_o_norm_w, v_dn_w_out):
    shard = dict(meta_tokens=meta_tokens, attn_norm_w=attn_norm_w, attn_w_in=attn_w_in[0], attn_q_norm_w=attn_q_norm_w,
                 attn_k_norm_w=attn_k_norm_w, attn_sinks=attn_sinks, attn_w_out=attn_w_out[0], dn_norm_w=dn_norm_w,
                 dn_w_in=dn_w_in[0], dn_conv_w=dn_conv_w[0], dn_a_log=dn_a_log, dn_dt_bias=dn_dt_bias,
                 dn_o_norm_w=dn_o_norm_w, dn_w_out=dn_w_out[0])
    mom_m = dict(meta_tokens=m_meta_tokens, attn_norm_w=m_attn_norm_w, attn_w_in=m_attn_w_in[0], attn_q_norm_w=m_attn_q_norm_w,
                 attn_k_norm_w=m_attn_k_norm_w, attn_sinks=m_attn_sinks, attn_w_out=m_attn_w_out[0], dn_norm_w=m_dn_norm_w,
                 dn_w_in=m_dn_w_in[0], dn_conv_w=m_dn_conv_w[0], dn_a_log=m_dn_a_log, dn_dt_bias=m_dn_dt_bias,
                 dn_o_norm_w=m_dn_o_norm_w, dn_w_out=m_dn_w_out[0])
    mom_v = dict(meta_tokens=v_meta_tokens, attn_norm_w=v_attn_norm_w, attn_w_in=v_attn_w_in[0], attn_q_norm_w=v_attn_q_norm_w,
                 attn_k_norm_w=v_attn_k_norm_w, attn_sinks=v_attn_sinks, attn_w_out=v_attn_w_out[0], dn_norm_w=v_dn_norm_w,
                 dn_w_in=v_dn_w_in[0], dn_conv_w=v_dn_conv_w[0], dn_a_log=v_dn_a_log, dn_dt_bias=v_dn_dt_bias,
                 dn_o_norm_w=v_dn_o_norm_w, dn_w_out=v_dn_w_out[0])

    def rows_of(d):
        return _pack_rows(d["attn_w_in"], d["dn_w_in"], d["attn_w_out"], d["dn_w_out"], d["meta_tokens"], d["dn_conv_w"],
                          d["dn_norm_w"])

    def small_of(d, extra):
        return _pack_small(*[d[k] for k in SMALL], extra)

    wa, wb, wc = rows_of(shard)
    ga, gb, gc = _exchange([wa.astype(BF16), wb.astype(BF16), wc], [False, False, False], "gather_weights")
    full = {k: shard[k] for k in SMALL}
    full["attn_w_in"] = ga[:, :, :288].transpose(1, 0, 2).reshape(1024, 2304)
    full["dn_w_in"] = ga[:, :, 288:].transpose(1, 0, 2).reshape(1024, 6176)
    full["attn_w_out"] = gb[:, :128].reshape(1024, 1024)
    full["dn_w_out"] = gb[:, 128:].reshape(2048, 1024)
    meta_full = gc[:, :16].transpose(1, 0, 2).reshape(N_META, 1024)
    full["dn_conv_w"] = gc[:, 16:32].reshape(N_DEV, 4, 512).transpose(1, 0, 2).reshape(4, 4096)
    full["dn_norm_w"] = gc[:, 32].reshape(1, 1024)

    seq = x.shape[1]
    h0 = jnp.concatenate([jnp.zeros((FRONT_PAD, D_MODEL), F32), meta_full, x[0]], axis=0)
    target = jnp.concatenate([jnp.zeros((ATTN_BLOCK, D_MODEL), F32), loss_target[0]], axis=0)
    loss, dh0, g = _local_step(h0, target, full)
    grad_x = dh0[ATTN_BLOCK:ATTN_BLOCK + seq][None]
    g["meta_tokens"] = dh0[FRONT_PAD:ATTN_BLOCK]

    pa = jnp.concatenate([g["attn_w_in"].reshape(1024, N_DEV, 288), g["dn_w_in"].reshape(1024, N_DEV, 772)],
                         axis=2).transpose(1, 0, 2)
    pb = jnp.concatenate([g["attn_w_out"].reshape(N_DEV, 128, 1024), g["dn_w_out"].reshape(N_DEV, 256, 1024)], axis=1)
    dn_norm8 = jnp.concatenate([g["dn_norm_w"].reshape(N_DEV, 1, 128), jnp.zeros((N_DEV, 7, 128), F32)], axis=1)
    pc = jnp.concatenate([g["meta_tokens"].reshape(N_META, N_DEV, 128).transpose(1, 0, 2),
                          g["dn_conv_w"].reshape(4, N_DEV, 512).transpose(1, 0, 2).reshape(N_DEV, 16, 128), dn_norm8], axis=1)
    ps = small_of(g, loss)
    xa, xb, xc, xs = _exchange([pa.astype(BF16), pb.astype(BF16), pc, ps], [True, True, True, False], "exchange_grads")

    out = {}
    ma, mb, mc = rows_of(mom_m)
    va, vb, vc = rows_of(mom_v)
    ra = _adamw(xa, wa, ma, va, "adamw_a")
    rb = _adamw(xb, wb, mb, vb, "adamw_b")
    rc = _adamw(xc, wc, mc, vc, "adamw_c")
    zero = jnp.zeros((1, 128), F32)
    rs = _adamw(xs, small_of(shard, zero), small_of(mom_m, zero), small_of(mom_v, zero), "adamw_small")
    row_names = ["attn_w_in", "dn_w_in", "attn_w_out", "dn_w_out", "meta_tokens", "dn_conv_w", "dn_norm_w"]
    lead = {"attn_w_in", "dn_w_in", "attn_w_out", "dn_w_out", "dn_conv_w"}
    for kind in range(4):
        vals = dict(zip(row_names, _unpack_rows(ra[kind], rb[kind], rc[kind])))
        small = _unpack_small(rs[kind])
        vals.update(dict(zip(SMALL, small[:7])))
        if kind == 0:
            loss_total = small[7][0, 0]
        out[kind] = [vals[k][None] if k in lead else vals[k] for k in WEIGHTS]
    return (loss_total, grad_x, *out[0], *out[1], *out[2], *out[3])
```

```python
import functools
import math

import jax
import jax.numpy as jnp
from jax import lax
from jax.experimental import pallas as pl
from jax.experimental.pallas import tpu as pltpu

F32, BF16 = jnp.float32, jnp.bfloat16

D_MODEL = 1024
N_META = 16
NORM_EPS = 1e-6
ATTN_HEADS, ATTN_KV_HEADS, ATTN_GROUPS, ATTN_HD = 16, 2, 8, 64
ATTN_BLOCK = 128
FRONT_PAD = ATTN_BLOCK - N_META
DN_HD, DN_K_HEADS, DN_V_HEADS = 128, 8, 16
DN_CHUNK = 128
TRI_BLOCK = 64
SCAN_BWD_GROUP = 4
DN_KEY_W, DN_VAL_W = 1024, 2048
DN_CONV_W = 2 * DN_KEY_W + DN_VAL_W
DN_CONV_K = 4
N_DEV = 8
ROW_BLOCK = 384
WGRAD_ROWS = 1376
VMEM_LIMIT = 56 * 1024 * 1024
NEG = -1e30

ADAM_LR, ADAM_B1, ADAM_B2, ADAM_EPS, ADAM_WD, ADAM_STEP = 0.001, 0.9, 0.999, 1e-08, 0.01, 10

NT = (((1,), (1,)), ((), ()))
TN = (((0,), (0,)), ((), ()))


def _cparams(sem=("arbitrary",)):
    return pltpu.CompilerParams(dimension_semantics=sem, vmem_limit_bytes=VMEM_LIMIT)


def _rms(x, w):
    return x * lax.rsqrt(jnp.mean(x * x, axis=-1, keepdims=True) + NORM_EPS) * w


def _silu(x):
    return x * jax.nn.sigmoid(x)


def _softplus(x):
    return jnp.maximum(x, 0.0) + jnp.log(1.0 + jnp.exp(-jnp.abs(x)))


NN = (((1,), (0,)), ((), ()))


def _mm(a, b, dims):
    return lax.dot_general(a.astype(BF16), b.astype(BF16), dims, preferred_element_type=F32)


@functools.partial(jax.custom_vjp, nondiff_argnums=(2,))
def _bdot_vjp(a, b, dims):
    return _mm(a, b, dims)


def _bdot_fwd(a, b, dims):
    a16, b16 = a.astype(BF16), b.astype(BF16)
    return _mm(a16, b16, dims), (a16, b16, jnp.zeros((), a.dtype), jnp.zeros((), b.dtype))


def _bdot_bwd(dims, res, g):
    a16, b16, ta, tb = res
    g16 = g.astype(BF16)
    if dims == NN:
        da, db = _mm(g16, b16, NT), _mm(a16, g16, TN)
    elif dims == NT:
        da, db = _mm(g16, b16, NN), _mm(g16, a16, TN)
    else:
        da, db = _mm(b16, g16, NT), _mm(a16, g16, NN)
    return da.astype(ta.dtype), db.astype(tb.dtype)


_bdot_vjp.defvjp(_bdot_fwd, _bdot_bwd)


def _bdot(a, b, dims=NN):
    return _bdot_vjp(a, b, dims)


def _hdot(a, b):
    return jnp.dot(a, b, preferred_element_type=F32, precision=lax.Precision.HIGHEST)


def _row_call(name, body, n_rows, rb, rows, consts, outs, accs=(), reverse=False, scratch=(), halos=()):
    n = n_rows // rb
    assert n * rb == n_rows
    idx = (lambda i: (n - 1 - i, 0)) if reverse else (lambda i: (i, 0))
    in_specs = [pl.BlockSpec((rb, a.shape[1]), idx) for a in rows]
    in_specs += [pl.BlockSpec((hr, a.shape[1]), fn) for a, hr, fn in halos]
    in_specs += [pl.BlockSpec(c.shape, functools.partial(lambda i, nd: (0,) * nd, nd=c.ndim)) for c in consts]
    out_specs = [pl.BlockSpec((rb, c), idx) for c, _ in outs]
    out_specs += [pl.BlockSpec(s, functools.partial(lambda i, nd: (0,) * nd, nd=len(s))) for s, _ in accs]
    out_shape = [jax.ShapeDtypeStruct((n_rows, c), dt) for c, dt in outs]
    out_shape += [jax.ShapeDtypeStruct(s, dt) for s, dt in accs]
    return pl.pallas_call(
        body, grid=(n,), in_specs=in_specs, out_specs=out_specs, out_shape=out_shape,
        scratch_shapes=list(scratch), name=name, compiler_params=_cparams(),
    )(*rows, *[a for a, _, _ in halos], *consts)


def _attn_in_fwd(h0, norm_w, w_in):
    T = h0.shape[0]

    def body(h_ref, nw_ref, w_ref, xn_ref, q_ref, kv_ref, gate_ref):
        xn = _rms(h_ref[...], nw_ref[...]).astype(BF16)
        xn_ref[...] = xn
        q_ref[...] = jnp.dot(xn, w_ref[:, 0:1024], preferred_element_type=F32)
        kv_ref[...] = jnp.dot(xn, w_ref[:, 1024:1280], preferred_element_type=F32)
        gate_ref[...] = jnp.dot(xn, w_ref[:, 1280:2304], preferred_element_type=F32)

    return _row_call("attn_in_fwd", body, T, ROW_BLOCK, [h0], [norm_w, w_in],
                     [(1024, BF16), (1024, F32), (256, F32), (1024, F32)])


def _attn_bias(n, j):
    C, R = 2 * ATTN_BLOCK + N_META, ATTN_GROUPS * ATTN_BLOCK
    c = lax.broadcasted_iota(jnp.int32, (C, R), 0)
    r = lax.broadcasted_iota(jnp.int32, (C, R), 1)
    ql = r & (ATTN_BLOCK - 1)
    is_meta = c >= 2 * ATTN_BLOCK
    dist_band = ATTN_BLOCK + ql - c
    cmin = jnp.maximum(0, 2 * ATTN_BLOCK - ATTN_BLOCK * n)
    valid_band = (c >= cmin) & (dist_band >= 0) & (dist_band < ATTN_BLOCK)
    dist_meta = ATTN_BLOCK * n + ql - FRONT_PAD - (c - 2 * ATTN_BLOCK)
    valid = (is_meta & (dist_meta >= 0)) | (jnp.logical_not(is_meta) & valid_band)
    dist = jnp.minimum(jnp.where(is_meta, dist_meta, dist_band), ATTN_BLOCK).astype(F32)
    rr = lax.broadcasted_iota(jnp.int32, (1, R), 1)
    head = (rr >> 7).astype(F32) + float(ATTN_GROUPS * j + 1)
    slope = jnp.exp(head * (-0.5 * math.log(2.0)))
    return jnp.where(valid, slope * dist, -NEG)


def _attn_tables(n, refresh, bias_ref):
    @pl.when(refresh)
    def _():
        for j in range(ATTN_KV_HEADS):
            bias_ref[j] = _attn_bias(n, j)


def _attn_table_scratch():
    return [pltpu.VMEM((ATTN_KV_HEADS, 2 * ATTN_BLOCK + N_META, ATTN_GROUPS * ATTN_BLOCK), F32)]


def _attn_group(q_t, k, v, sinkrow, qnw_col, knw, bias):
    qn = q_t * lax.rsqrt(jnp.mean(q_t * q_t, axis=0, keepdims=True) + NORM_EPS) * qnw_col
    kn = _rms(k, knw)
    s = _bdot(kn, qn) * (ATTN_HD ** -0.5) - bias
    m = lax.stop_gradient(jnp.maximum(jnp.max(s, axis=0, keepdims=True), sinkrow))
    e = jnp.exp(s - m)
    denom = jnp.sum(e, axis=0, keepdims=True) + jnp.exp(sinkrow - m)
    p = e * (1.0 / denom)
    return _bdot(v, p, TN)


def _sink_row(sinks_ref, j):
    rr = lax.broadcasted_iota(jnp.int32, (1, ATTN_GROUPS * ATTN_BLOCK), 1) >> 7
    row = jnp.zeros((1, ATTN_GROUPS * ATTN_BLOCK), F32)
    for hl in range(ATTN_GROUPS):
        row = jnp.where(rr == hl, sinks_ref[0, ATTN_GROUPS * j + hl], row)
    return row


def _heads_to_lanes(ref, j):
    return jnp.concatenate([ref[:, ATTN_HD * h:ATTN_HD * (h + 1)].T
                            for h in range(ATTN_GROUPS * j, ATTN_GROUPS * (j + 1))], axis=1)


def _lanes_to_heads(ref, j, x_t):
    for hl in range(ATTN_GROUPS):
        h = ATTN_GROUPS * j + hl
        ref[:, ATTN_HD * h:ATTN_HD * (h + 1)] = x_t[:, ATTN_BLOCK * hl:ATTN_BLOCK * (hl + 1)].T


def _attn_kv_tiles(kvp_ref, kvc_ref, kvm_ref, j):
    ksl = slice(ATTN_HD * j, ATTN_HD * (j + 1))
    vsl = slice(128 + ATTN_HD * j, 128 + ATTN_HD * (j + 1))
    k = jnp.concatenate([kvp_ref[:, ksl], kvc_ref[:, ksl], kvm_ref[FRONT_PAD:, ksl]], axis=0)
    v = jnp.concatenate([kvp_ref[:, vsl], kvc_ref[:, vsl], kvm_ref[FRONT_PAD:, vsl]], axis=0)
    return k, v


def _attn_core_fwd(q, kv, sinks, qnw, knw):
    T = q.shape[0]
    nb = T // ATTN_BLOCK

    def body(sinks_ref, q_ref, kvc_ref, kvp_ref, kvm_ref, qnw_ref, knw_ref, o_ref, bias_ref):
        n = pl.program_id(0)
        _attn_tables(n, n <= 2, bias_ref)
        for j in range(ATTN_KV_HEADS):
            k, v = _attn_kv_tiles(kvp_ref, kvc_ref, kvm_ref, j)
            o_t = _attn_group(_heads_to_lanes(q_ref, j), k, v, _sink_row(sinks_ref, j), qnw_ref[...], knw_ref[...],
                              bias_ref[j])
            _lanes_to_heads(o_ref, j, o_t)

    return pl.pallas_call(
        body, grid=(nb,),
        in_specs=[pl.BlockSpec(memory_space=pltpu.SMEM),
                  pl.BlockSpec((ATTN_BLOCK, 1024), lambda i: (i, 0)),
                  pl.BlockSpec((ATTN_BLOCK, 256), lambda i: (i, 0)),
                  pl.BlockSpec((ATTN_BLOCK, 256), lambda i: (jnp.maximum(i - 1, 0), 0)),
                  pl.BlockSpec((ATTN_BLOCK, 256), lambda i: (0, 0)),
                  pl.BlockSpec((ATTN_HD, 1), lambda i: (0, 0)),
                  pl.BlockSpec((1, ATTN_HD), lambda i: (0, 0))],
        out_specs=pl.BlockSpec((ATTN_BLOCK, 1024), lambda i: (i, 0)),
        out_shape=jax.ShapeDtypeStruct((T, 1024), F32),
        scratch_shapes=_attn_table_scratch(),
        name="attn_core_fwd", compiler_params=_cparams(),
    )(sinks, q, kv, kv, kv, qnw.reshape(ATTN_HD, 1), knw)


def _attn_out_fwd(o, gate, h0, w_out):
    T = o.shape[0]

    def body(o_ref, g_ref, h_ref, w_ref, h1_ref):
        og = o_ref[...] * _silu(g_ref[...])
        h1_ref[...] = h_ref[...] + _bdot(og, w_ref[...])

    return _row_call("attn_out_fwd", body, T, ROW_BLOCK, [o, gate, h0], [w_out], [(1024, F32)])[0]


def _wgrad(xn, du, cg, name):
    T, kdim = xn.shape
    cdim = du.shape[1]
    rows = WGRAD_ROWS if T % WGRAD_ROWS == 0 else ROW_BLOCK
    nr, nc = T // rows, cdim // cg
    assert nc * cg == cdim

    def body(x_ref, du_ref, dw_ref):
        @pl.when(pl.program_id(1) == 0)
        def _():
            dw_ref[...] = jnp.zeros_like(dw_ref)
        dw_ref[...] += _bdot(x_ref[...], du_ref[...], TN)

    return pl.pallas_call(
        body, grid=(nc, nr),
        in_specs=[pl.BlockSpec((rows, kdim), lambda j, i: (i, 0)),
                  pl.BlockSpec((rows, cg), lambda j, i: (i, j))],
        out_specs=pl.BlockSpec((kdim, cg), lambda j, i: (0, j)),
        out_shape=jax.ShapeDtypeStruct((kdim, cdim), F32),
        name=name, compiler_params=_cparams(("arbitrary", "arbitrary")),
    )(xn, du)


def _attn_out_bwd(dh1, o, gate, w_out):
    T = o.shape[0]

    def body(dh_ref, o_ref, g_ref, w_ref, do_ref, dg_ref, dw_ref):
        @pl.when(pl.program_id(0) == 0)
        def _():
            dw_ref[...] = jnp.zeros_like(dw_ref)
        dh = dh_ref[...]
        dog = _bdot(dh, w_ref[...], NT)
        og, vjp = jax.vjp(lambda o_, g_: o_ * _silu(g_), o_ref[...], g_ref[...])
        do, dg = vjp(dog)
        do_ref[...] = do
        dg_ref[...] = dg
        dw_ref[...] += _bdot(og, dh, TN)

    return _row_call("attn_out_bwd", body, T, ROW_BLOCK, [dh1, o, gate], [w_out],
                     [(1024, F32), (1024, F32)], [((1024, 1024), F32)])


def _attn_core_bwd(do, q, kv, sinks, qnw, knw):
    T = q.shape[0]
    nb = T // ATTN_BLOCK
    rev = lambda i: nb - 1 - i

    def body(sinks_ref, do_ref, q_ref, kvc_ref, kvp_ref, kvm_ref, qnw_ref, knw_ref,
             dq_ref, dkv_ref, dsinks_ref, dqnw_ref, dknw_ref, carry_ref, meta_ref, bias_ref):
        step = pl.program_id(0)
        n = rev(step)
        _attn_tables(n, (step == 0) | (n <= 1), bias_ref)

        @pl.when(step == 0)
        def _():
            carry_ref[...] = jnp.zeros_like(carry_ref)
            meta_ref[...] = jnp.zeros_like(meta_ref)
            dsinks_ref[...] = jnp.zeros_like(dsinks_ref)
            dqnw_ref[...] = jnp.zeros_like(dqnw_ref)
            dknw_ref[...] = jnp.zeros_like(dknw_ref)

        lane16 = lax.broadcasted_iota(jnp.int32, (1, ATTN_HEADS), 1)
        dsinks = jnp.zeros((1, ATTN_HEADS), F32)
        for j in range(ATTN_KV_HEADS):
            k, v = _attn_kv_tiles(kvp_ref, kvc_ref, kvm_ref, j)
            fn = functools.partial(_attn_group, bias=bias_ref[j])
            _, vjp = jax.vjp(fn, _heads_to_lanes(q_ref, j), k, v, _sink_row(sinks_ref, j), qnw_ref[...], knw_ref[...])
            dq_t, dk, dv, dsr, dqn, dkn = vjp(_heads_to_lanes(do_ref, j))
            _lanes_to_heads(dq_ref, j, dq_t)
            dqnw_ref[...] += dqn
            dknw_ref[...] += dkn
            for hl in range(ATTN_GROUPS):
                dsinks = dsinks + jnp.where(lane16 == ATTN_GROUPS * j + hl,
                                            jnp.sum(dsr[:, ATTN_BLOCK * hl:ATTN_BLOCK * (hl + 1)]), 0.0)
            ksl = slice(ATTN_HD * j, ATTN_HD * (j + 1))
            vsl = slice(128 + ATTN_HD * j, 128 + ATTN_HD * (j + 1))
            for sl, d in ((ksl, dk), (vsl, dv)):
                dkv_ref[:, sl] = d[ATTN_BLOCK:2 * ATTN_BLOCK, :] + carry_ref[:, sl]
                carry_ref[:, sl] = d[0:ATTN_BLOCK, :]
                meta_ref[:, sl] += d[2 * ATTN_BLOCK:, :]
        dsinks_ref[...] += dsinks

        @pl.when(n == 0)
        def _():
            dkv_ref[FRONT_PAD:, :] += meta_ref[...]

    dq, dkv, dsinks, dqnw, dknw = pl.pallas_call(
        body, grid=(nb,),
        in_specs=[pl.BlockSpec(memory_space=pltpu.SMEM),
                  pl.BlockSpec((ATTN_BLOCK, 1024), lambda i: (rev(i), 0)),
                  pl.BlockSpec((ATTN_BLOCK, 1024), lambda i: (rev(i), 0)),
                  pl.BlockSpec((ATTN_BLOCK, 256), lambda i: (rev(i), 0)),
                  pl.BlockSpec((ATTN_BLOCK, 256), lambda i: (jnp.maximum(rev(i) - 1, 0), 0)),
                  pl.BlockSpec((ATTN_BLOCK, 256), lambda i: (0, 0)),
                  pl.BlockSpec((ATTN_HD, 1), lambda i: (0, 0)),
                  pl.BlockSpec((1, ATTN_HD), lambda i: (0, 0))],
        out_specs=[pl.BlockSpec((ATTN_BLOCK, 1024), lambda i: (rev(i), 0)),
                   pl.BlockSpec((ATTN_BLOCK, 256), lambda i: (rev(i), 0)),
                   pl.BlockSpec((1, ATTN_HEADS), lambda i: (0, 0)),
                   pl.BlockSpec((ATTN_HD, 1), lambda i: (0, 0)),
                   pl.BlockSpec((1, ATTN_HD), lambda i: (0, 0))],
        out_shape=[jax.ShapeDtypeStruct((T, 1024), F32), jax.ShapeDtypeStruct((T, 256), F32),
                   jax.ShapeDtypeStruct((1, ATTN_HEADS), F32), jax.ShapeDtypeStruct((ATTN_HD, 1), F32),
                   jax.ShapeDtypeStruct((1, ATTN_HD), F32)],
        scratch_shapes=[pltpu.VMEM((ATTN_BLOCK, 256), F32), pltpu.VMEM((N_META, 256), F32)] + _attn_table_scratch(),
        name="attn_core_bwd", compiler_params=_cparams(),
    )(sinks, do, q, kv, kv, kv, qnw.reshape(ATTN_HD, 1), knw)
    return dq, dkv, dsinks, dqnw.reshape(1, ATTN_HD), dknw


def _attn_in_bwd(dq, dkv, dgate, h0, dh1, norm_w, w_in):
    T = h0.shape[0]

    def body(dq_ref, dkv_ref, dg_ref, h_ref, dh1_ref, nw_ref, w_ref, dh0_ref, dnw_ref):
        @pl.when(pl.program_id(0) == 0)
        def _():
            dnw_ref[...] = jnp.zeros_like(dnw_ref)
        dxn = (_bdot(dq_ref[...], w_ref[:, 0:1024], NT) + _bdot(dkv_ref[...], w_ref[:, 1024:1280], NT)
               + _bdot(dg_ref[...], w_ref[:, 1280:2304], NT))
        _, vjp = jax.vjp(_rms, h_ref[...], nw_ref[...])
        dh, dnw = vjp(dxn)
        dh0_ref[...] = dh1_ref[...] + dh
        dnw_ref[...] += dnw

    return _row_call("attn_in_bwd", body, T, ROW_BLOCK, [dq, dkv, dgate, h0, dh1], [norm_w, w_in],
                     [(1024, F32)], [((1, 1024), F32)])


def _dn_in_fwd(h1, norm_w, w_in):
    T = h1.shape[0]

    def body(h_ref, nw_ref, w_ref, xn_ref, qkv_ref, z_ref, ba_ref):
        xn = _rms(h_ref[...], nw_ref[...]).astype(BF16)
        xn_ref[...] = xn
        qkv_ref[...] = jnp.dot(xn, w_ref[:, 0:4096], preferred_element_type=F32)
        z_ref[...] = jnp.dot(xn, w_ref[:, 4096:6144], preferred_element_type=F32)
        ba_ref[...] = jnp.dot(xn, w_ref[:, 6144:6176], preferred_element_type=F32)

    return _row_call("dn_in_fwd", body, T, ROW_BLOCK, [h1], [norm_w, w_in],
                     [(1024, BF16), (4096, F32), (2048, F32), (32, F32)])


def _shift_down(cur, prev8, s):
    i8 = lax.broadcasted_iota(jnp.int32, (8, cur.shape[1]), 0)
    r = pltpu.roll(cur, s, 0)
    head = jnp.where(i8 < s, pltpu.roll(prev8, s, 0), r[0:8])
    return jnp.concatenate([head, r[8:]], axis=0)


def _shift_up(cur, next8, s):
    n = cur.shape[0]
    i8 = lax.broadcasted_iota(jnp.int32, (8, cur.shape[1]), 0)
    r = pltpu.roll(cur, n - s, 0)
    tail = jnp.where(i8 >= 8 - s, pltpu.roll(next8, 8 - s, 0), r[n - 8:])
    return jnp.concatenate([r[:n - 8], tail], axis=0)


def _conv_taps(cur, prev8):
    return [cur] + [_shift_down(cur, prev8, s) for s in range(1, DN_CONV_K)]


def _conv_tile(taps, w):
    out = w[3:4, :] * taps[0]
    for s in range(1, DN_CONV_K):
        out = out + w[3 - s:4 - s, :] * taps[s]
    return out


def _l2n(a, scale):
    return a * (lax.rsqrt(jnp.sum(a * a, axis=-1, keepdims=True) + NORM_EPS) * scale)


def _dn_post_tile(c, t):
    a = _silu(c)
    if t < DN_K_HEADS:
        return _l2n(a, DN_HD ** -0.5)
    if t < 2 * DN_K_HEADS:
        return _l2n(a, 1.0)
    return a


def _dn_beta_g(ba, a_log, dt_bias, live):
    beta = jax.nn.sigmoid(ba[:, 0:DN_V_HEADS]) * live
    g = -jnp.exp(a_log) * _softplus(ba[:, DN_V_HEADS:] + dt_bias) * live
    return beta, g


def _live_rows(i, rb):
    rows = i * rb + lax.broadcasted_iota(jnp.int32, (rb, 1), 0)
    return (rows >= FRONT_PAD).astype(F32)


def _halo_spec_args(x, rb):
    per = rb // 8
    return (x, 8, lambda i: (jnp.maximum(i * per - 1, 0), 0))


def _dn_conv_fwd(qkv, ba, conv_w, a_log, dt_bias):
    T = qkv.shape[0]

    def body(x_ref, ba_ref, halo_ref, cw_ref, al_ref, dtb_ref, q_ref, k_ref, v_ref, bg_ref):
        i = pl.program_id(0)
        first = (i > 0).astype(F32)
        for t in range(DN_CONV_W // 128):
            cols = slice(128 * t, 128 * (t + 1))
            c = _conv_tile(_conv_taps(x_ref[:, cols], halo_ref[:, cols] * first), cw_ref[:, cols])
            out = _dn_post_tile(c, t)
            if t < DN_K_HEADS:
                q_ref[:, cols] = out
            elif t < 2 * DN_K_HEADS:
                k_ref[:, 128 * (t - 8):128 * (t - 7)] = out
            else:
                v_ref[:, 128 * (t - 16):128 * (t - 15)] = out
        beta, g = _dn_beta_g(ba_ref[...], al_ref[...], dtb_ref[...], _live_rows(i, ROW_BLOCK))
        bg_ref[:, 0:DN_V_HEADS] = beta
        bg_ref[:, DN_V_HEADS:] = g

    return _row_call("dn_conv_fwd", body, T, ROW_BLOCK, [qkv, ba], [conv_w, a_log, dt_bias],
                     [(1024, F32), (1024, F32), (2048, F32), (32, F32)], halos=[_halo_spec_args(qkv, ROW_BLOCK)])


def _chunk_masks():
    r = lax.broadcasted_iota(jnp.int32, (DN_CHUNK, DN_CHUNK), 0)
    c = lax.broadcasted_iota(jnp.int32, (DN_CHUNK, DN_CHUNK), 1)
    return r >= c, r > c, r == c, r <= c


def _tri_inv_block(x):
    B = TRI_BLOCK
    n = range(len(x))
    r_, c_ = lax.broadcasted_iota(jnp.int32, (B, B), 0), lax.broadcasted_iota(jnp.int32, (B, B), 1)
    ainv = [jnp.where(r_ == c_, 1.0, 0.0) + x[h] for h in n]
    p = [_bdot(x[h], x[h]) for h in n]
    for _ in range(B.bit_length() - 3):
        r = [_bdot(jnp.concatenate([p[h], ainv[h]], axis=0), p[h]) for h in n]
        ainv = [ainv[h] + r[h][B:] for h in n]
        p = [r[h][:B] for h in n]
    return [ainv[h] + _bdot(ainv[h], p[h]) for h in n]


def _tri_inv(x):
    B = TRI_BLOCK
    assert DN_CHUNK == 2 * B
    n = len(x)
    diag = _tri_inv_block([x[h][:B, :B] for h in range(n)] + [x[h][B:, B:] for h in range(n)])
    a11, a22 = diag[:n], diag[n:]
    a21 = [_bdot(_bdot(a22[h], x[h][B:, :B]), a11[h]) for h in range(n)]
    zero = jnp.zeros((B, B), F32)
    return [jnp.concatenate([jnp.concatenate([a11[h], zero], axis=1), jnp.concatenate([a21[h], a22[h]], axis=1)], axis=0)
            for h in range(n)]


@jax.custom_vjp
def _tri_inv_known(x, a):
    return a


def _tri_inv_known_fwd(x, a):
    return a, a


def _tri_inv_known_bwd(a, da):
    return [_bdot(_bdot(a[h], da[h], TN), a[h], NT) for h in range(len(a))], [jnp.zeros_like(t) for t in a]


_tri_inv_known.defvjp(_tri_inv_known_fwd, _tri_inv_known_bwd)


def _dn_chunk_step(S, q, k, v, beta, g, masks, known_inv=None, with_inv=False):
    causal, strict, eye, upper = masks
    C, W = DN_CHUNK, DN_HD
    heads = range(len(v))
    k_t = [k[j].T for j in range(len(k))]
    qk_kk = [_bdot(jnp.concatenate([q[j], k[j]], axis=0), k_t[j]) for j in range(len(q))]
    g_b = [jnp.broadcast_to(g[h], (C, C)) for h in heads]
    beta_b = [jnp.broadcast_to(beta[h], (C, W)) for h in heads]
    g_row = [jnp.sum(jnp.where(eye, g_b[h], 0.0), axis=0, keepdims=True) for h in heads]
    gc_col = [jnp.sum(jnp.where(causal, g_row[h], 0.0), axis=1, keepdims=True) for h in heads]
    gc_row = [jnp.sum(jnp.where(upper, g_b[h], 0.0), axis=0, keepdims=True) for h in heads]
    g_last = [jnp.sum(g_row[h], axis=1, keepdims=True) for h in heads]
    gc_b = [jnp.broadcast_to(gc_col[h], (C, W)) for h in heads]
    decay = [jnp.exp(jnp.where(causal, gc_b[h][:, :C] - gc_row[h], NEG)) for h in heads]
    eg_b = [jnp.exp(gc_b[h]) for h in heads]
    x = [jnp.where(strict, qk_kk[h // 2][C:] * beta_b[h][:, :C] * decay[h], 0.0) * -1.0 for h in heads]
    ainv = _tri_inv(x) if known_inv is None else _tri_inv_known(x, known_inv)
    uw = [_bdot(ainv[h], jnp.concatenate([v[h] * beta_b[h], k[h // 2] * (beta_b[h] * eg_b[h])], axis=1)) for h in heads]
    ws_qs = [_bdot(jnp.concatenate([uw[h][:, W:], q[h // 2] * eg_b[h]], axis=0), S[h]) for h in heads]
    v_new = [uw[h][:, :W] - ws_qs[h][:C] for h in heads]
    o = [ws_qs[h][C:] + _bdot(qk_kk[h // 2][:C] * decay[h], v_new[h]) for h in heads]
    s_new = [S[h] * jnp.exp(g_last[h]) + _bdot(k_t[h // 2] * jnp.exp(g_last[h] - gc_row[h]), v_new[h]) for h in heads]
    return (s_new, o, ainv) if with_inv else (s_new, o)


def _dn_chunk_tiles(q_ref, k_ref, v_ref, bg_ref):
    q = [q_ref[:, 128 * j:128 * (j + 1)] for j in range(DN_K_HEADS)]
    k = [k_ref[:, 128 * j:128 * (j + 1)] for j in range(DN_K_HEADS)]
    v = [v_ref[:, 128 * h:128 * (h + 1)] for h in range(DN_V_HEADS)]
    beta = [bg_ref[:, h:h + 1] for h in range(DN_V_HEADS)]
    g = [bg_ref[:, DN_V_HEADS + h:DN_V_HEADS + h + 1] for h in range(DN_V_HEADS)]
    return q, k, v, beta, g


def _dn_scan_fwd(qn, kn, v, bg):
    T = qn.shape[0]
    nc = T // DN_CHUNK

    def body(q_ref, k_ref, v_ref, bg_ref, o_ref, ssave_ref, inv_ref, s_ref):
        @pl.when(pl.program_id(0) == 0)
        def _():
            s_ref[...] = jnp.zeros_like(s_ref)
        s_old = [s_ref[h] for h in range(DN_V_HEADS)]
        for h in range(DN_V_HEADS):
            ssave_ref[0, h] = s_old[h]
        s_new, o, ainv = _dn_chunk_step(s_old, *_dn_chunk_tiles(q_ref, k_ref, v_ref, bg_ref), _chunk_masks(),
                                        with_inv=True)
        for h in range(DN_V_HEADS):
            o_ref[:, 128 * h:128 * (h + 1)] = o[h]
            inv_ref[0, h] = ainv[h].astype(BF16)
            s_ref[h] = s_new[h]

    return pl.pallas_call(
        body, grid=(nc,),
        in_specs=[pl.BlockSpec((DN_CHUNK, 1024), lambda i: (i, 0)),
                  pl.BlockSpec((DN_CHUNK, 1024), lambda i: (i, 0)),
                  pl.BlockSpec((DN_CHUNK, 2048), lambda i: (i, 0)),
                  pl.BlockSpec((DN_CHUNK, 32), lambda i: (i, 0))],
        out_specs=[pl.BlockSpec((DN_CHUNK, 2048), lambda i: (i, 0)),
                   pl.BlockSpec((1, DN_V_HEADS, DN_HD, DN_HD), lambda i: (i, 0, 0, 0)),
                   pl.BlockSpec((1, DN_V_HEADS, DN_CHUNK, DN_CHUNK), lambda i: (i, 0, 0, 0))],
        out_shape=[jax.ShapeDtypeStruct((T, 2048), F32),
                   jax.ShapeDtypeStruct((nc, DN_V_HEADS, DN_HD, DN_HD), F32),
                   jax.ShapeDtypeStruct((nc, DN_V_HEADS, DN_CHUNK, DN_CHUNK), BF16)],
        scratch_shapes=[pltpu.VMEM((DN_V_HEADS, DN_HD, DN_HD), F32)],
        name="dn_scan_fwd", compiler_params=_cparams(),
    )(qn, kn, v, bg)


def _dn_gate_tile(o, z, onw):
    return _rms(o, onw) * _silu(z)


def _dn_out_fwd(o, z, h1, target, w_out, onw):
    T = o.shape[0]

    def body(o_ref, z_ref, h_ref, t_ref, w_ref, onw_ref, dy_ref, og_ref, loss_ref):
        i = pl.program_id(0)

        @pl.when(i == 0)
        def _():
            loss_ref[...] = jnp.zeros_like(loss_ref)
        for h in range(DN_V_HEADS):
            cols = slice(128 * h, 128 * (h + 1))
            og_ref[:, cols] = _dn_gate_tile(o_ref[:, cols], z_ref[:, cols], onw_ref[...]).astype(BF16)
        y = h_ref[...] + jnp.dot(og_ref[...], w_ref[...], preferred_element_type=F32)
        rows = i * ROW_BLOCK + lax.broadcasted_iota(jnp.int32, (ROW_BLOCK, 1), 0)
        diff = jnp.where(rows >= FRONT_PAD + N_META, y - t_ref[...], 0.0)
        dy_ref[...] = diff * (1.0 / D_MODEL)
        loss_ref[...] += jnp.sum(diff * diff) * (0.5 / D_MODEL)

    return _row_call("dn_out_fwd", body, T, ROW_BLOCK, [o, z, h1, target], [w_out, onw],
                     [(1024, F32), (2048, BF16)], [((1, 128), F32)])


def _dn_out_bwd(dy, o, z, w_out, onw):
    T = o.shape[0]

    def body(dy_ref, o_ref, z_ref, w_ref, onw_ref, do_ref, dz_ref, donw_ref, dog_ref):
        @pl.when(pl.program_id(0) == 0)
        def _():
            donw_ref[...] = jnp.zeros_like(donw_ref)
        dy = dy_ref[...].astype(BF16)
        donw = jnp.zeros((1, DN_HD), F32)
        for half in range(2):
            hcols = slice(1024 * half, 1024 * (half + 1))
            dog_ref[:, hcols] = lax.dot_general(dy, w_ref[hcols, :], NT, preferred_element_type=F32)
        for h in range(DN_V_HEADS):
            cols = slice(128 * h, 128 * (h + 1))
            _, vjp = jax.vjp(_dn_gate_tile, o_ref[:, cols], z_ref[:, cols], onw_ref[...])
            do, dz, dn = vjp(dog_ref[:, cols])
            do_ref[:, cols] = do
            dz_ref[:, cols] = dz
            donw = donw + dn
        donw_ref[...] += donw

    return _row_call("dn_out_bwd", body, T, ROW_BLOCK, [dy, o, z], [w_out, onw],
                     [(2048, F32), (2048, F32)], [((1, DN_HD), F32)], scratch=[pltpu.VMEM((ROW_BLOCK, 2048), F32)])


def _dn_scan_bwd(do, qn, kn, v, bg, ssave, inv):
    T = qn.shape[0]
    nc = T // DN_CHUNK
    rev = lambda i: nc - 1 - i

    def body(do_ref, q_ref, k_ref, v_ref, bg_ref, ss_ref, inv_ref, dq_ref, dk_ref, dv_ref, dbg_ref, ds_ref):
        @pl.when(pl.program_id(0) == 0)
        def _():
            ds_ref[...] = jnp.zeros_like(ds_ref)
        lane32 = lax.broadcasted_iota(jnp.int32, (1, 2 * DN_V_HEADS), 1)
        masks = _chunk_masks()
        q, k, v, beta, g = _dn_chunk_tiles(q_ref, k_ref, v_ref, bg_ref)
        dbg = jnp.zeros((DN_CHUNK, 2 * DN_V_HEADS), F32)
        for first in range(0, DN_V_HEADS, SCAN_BWD_GROUP):
            heads = range(first, first + SCAN_BWD_GROUP)
            pairs = slice(first // 2, (first + SCAN_BWD_GROUP) // 2)
            hs = slice(first, first + SCAN_BWD_GROUP)
            fn = functools.partial(_dn_chunk_step, masks=masks, known_inv=[inv_ref[0, h].astype(F32) for h in heads])
            _, vjp = jax.vjp(fn, [ss_ref[0, h] for h in heads], q[pairs], k[pairs], v[hs], beta[hs], g[hs])
            ds, dq, dk, dv, dbeta, dg = vjp(([ds_ref[h] for h in heads], [do_ref[:, 128 * h:128 * (h + 1)] for h in heads]))
            for i, h in enumerate(heads):
                ds_ref[h] = ds[i]
                dv_ref[:, 128 * h:128 * (h + 1)] = dv[i]
                dbg = dbg + jnp.where(lane32 == h, dbeta[i], 0.0) + jnp.where(lane32 == DN_V_HEADS + h, dg[i], 0.0)
            for i, j in enumerate(range(first // 2, (first + SCAN_BWD_GROUP) // 2)):
                dq_ref[:, 128 * j:128 * (j + 1)] = dq[i]
                dk_ref[:, 128 * j:128 * (j + 1)] = dk[i]
        dbg_ref[...] = dbg

    return pl.pallas_call(
        body, grid=(nc,),
        in_specs=[pl.BlockSpec((DN_CHUNK, 2048), lambda i: (rev(i), 0)),
                  pl.BlockSpec((DN_CHUNK, 1024), lambda i: (rev(i), 0)),
                  pl.BlockSpec((DN_CHUNK, 1024), lambda i: (rev(i), 0)),
                  pl.BlockSpec((DN_CHUNK, 2048), lambda i: (rev(i), 0)),
                  pl.BlockSpec((DN_CHUNK, 32), lambda i: (rev(i), 0)),
                  pl.BlockSpec((1, DN_V_HEADS, DN_HD, DN_HD), lambda i: (rev(i), 0, 0, 0)),
                  pl.BlockSpec((1, DN_V_HEADS, DN_CHUNK, DN_CHUNK), lambda i: (rev(i), 0, 0, 0))],
        out_specs=[pl.BlockSpec((DN_CHUNK, 1024), lambda i: (rev(i), 0)),
                   pl.BlockSpec((DN_CHUNK, 1024), lambda i: (rev(i), 0)),
                   pl.BlockSpec((DN_CHUNK, 2048), lambda i: (rev(i), 0)),
                   pl.BlockSpec((DN_CHUNK, 32), lambda i: (rev(i), 0))],
        out_shape=[jax.ShapeDtypeStruct((T, 1024), F32), jax.ShapeDtypeStruct((T, 1024), F32),
                   jax.ShapeDtypeStruct((T, 2048), F32), jax.ShapeDtypeStruct((T, 32), F32)],
        scratch_shapes=[pltpu.VMEM((DN_V_HEADS, DN_HD, DN_HD), F32)],
        name="dn_scan_bwd", compiler_params=_cparams(),
    )(do, qn, kn, v, bg, ssave, inv)


def _dn_conv_bwd(dqn, dkn, dv, dbg, qkv, ba, conv_w, a_log, dt_bias):
    T = qkv.shape[0]
    nr = T // ROW_BLOCK

    def body(dq_ref, dk_ref, dv_ref, dbg_ref, x_ref, ba_ref, halo_ref, cw_ref, al_ref, dtb_ref,
             dx_ref, dba_ref, dcw_ref, dal_ref, ddtb_ref, carry_ref):
        step = pl.program_id(0)
        i = nr - 1 - step

        @pl.when(step == 0)
        def _():
            carry_ref[...] = jnp.zeros_like(carry_ref)
            dcw_ref[...] = jnp.zeros_like(dcw_ref)
            dal_ref[...] = jnp.zeros_like(dal_ref)
            ddtb_ref[...] = jnp.zeros_like(ddtb_ref)
        first = (i > 0).astype(F32)
        for t in range(DN_CONV_W // 128):
            cols = slice(128 * t, 128 * (t + 1))
            w = cw_ref[:, cols]
            taps = _conv_taps(x_ref[:, cols], halo_ref[:, cols] * first)
            c = _conv_tile(taps, w)
            if t < DN_K_HEADS:
                dout = dq_ref[:, cols]
            elif t < 2 * DN_K_HEADS:
                dout = dk_ref[:, 128 * (t - 8):128 * (t - 7)]
            else:
                dout = dv_ref[:, 128 * (t - 16):128 * (t - 15)]
            _, vjp = jax.vjp(functools.partial(_dn_post_tile, t=t), c)
            (dc,) = vjp(dout)
            nxt = carry_ref[:, cols]
            dx = w[3:4, :] * dc
            dcw_ref[3:4, cols] += jnp.sum(dc * taps[0], axis=0, keepdims=True)
            for s in range(1, DN_CONV_K):
                dx = dx + w[3 - s:4 - s, :] * _shift_up(dc, nxt, s)
                dcw_ref[3 - s:4 - s, cols] += jnp.sum(dc * taps[s], axis=0, keepdims=True)
            dx_ref[:, cols] = dx
            carry_ref[:, cols] = dc[0:8, :]
        fn = functools.partial(_dn_beta_g, live=_live_rows(i, ROW_BLOCK))
        _, vjp = jax.vjp(fn, ba_ref[...], al_ref[...], dtb_ref[...])
        dba, dal, ddtb = vjp((dbg_ref[:, 0:DN_V_HEADS], dbg_ref[:, DN_V_HEADS:]))
        dba_ref[...] = dba
        dal_ref[...] += dal
        ddtb_ref[...] += ddtb

    per = ROW_BLOCK // 8
    halo = (qkv, 8, lambda s: (jnp.maximum((nr - 1 - s) * per - 1, 0), 0))
    return _row_call("dn_conv_bwd", body, T, ROW_BLOCK, [dqn, dkn, dv, dbg, qkv, ba], [conv_w, a_log, dt_bias],
                     [(4096, F32), (32, F32)], [((DN_CONV_K, 4096), F32), ((1, DN_V_HEADS), F32), ((1, DN_V_HEADS), F32)],
                     reverse=True, scratch=[pltpu.VMEM((8, 4096), F32)], halos=[halo])


def _dn_in_bwd(dqkv, dz, dba, h1, dy, norm_w, w_in):
    T = h1.shape[0]

    def body(dqkv_ref, dz_ref, dba_ref, h_ref, dy_ref, nw_ref, w_ref, dh_ref, dnw_ref):
        @pl.when(pl.program_id(0) == 0)
        def _():
            dnw_ref[...] = jnp.zeros_like(dnw_ref)
        dxn = (_bdot(dqkv_ref[...], w_ref[:, 0:4096], NT) + _bdot(dz_ref[...], w_ref[:, 4096:6144], NT)
               + _bdot(dba_ref[...], w_ref[:, 6144:6176], NT))
        _, vjp = jax.vjp(_rms, h_ref[...], nw_ref[...])
        dh, dnw = vjp(dxn)
        dh_ref[...] = (dy_ref[...] + dh) * _live_rows(pl.program_id(0), ROW_BLOCK)
        dnw_ref[...] += dnw

    return _row_call("dn_in_bwd", body, T, ROW_BLOCK, [dqkv, dz, dba, h1, dy], [norm_w, w_in],
                     [(1024, F32)], [((1, 1024), F32)])


def _exchange(parts, scatter, name):
    n = len(parts)
    out_shape = [jax.ShapeDtypeStruct(p.shape if sc else (N_DEV,) + p.shape, p.dtype) for p, sc in zip(parts, scatter)]

    def body(*refs):
        ins, outs = refs[:n], refs[n:2 * n]
        send_sems, recv_sems, local_sems = refs[2 * n:]
        x, y, c = lax.axis_index("x"), lax.axis_index("y"), lax.axis_index("c")
        me = 4 * x + 2 * y + c
        peers = []
        for k in range(1, N_DEV):
            px = 1 - x if k & 4 else x
            py = 1 - y if k & 2 else y
            pc = 1 - c if k & 1 else c
            peers.append(((px, py, pc), 4 * px + 2 * py + pc))

        def src(a, idx):
            return ins[a].at[idx] if scatter[a] else ins[a]

        local = [pltpu.make_async_copy(src(a, me), outs[a].at[me], local_sems.at[a]) for a in range(n)]
        for cp in local:
            cp.start()
        for a in range(n):
            for k, (dev, idx) in enumerate(peers):
                pltpu.make_async_remote_copy(
                    src_ref=src(a, idx), dst_ref=outs[a].at[me], send_sem=send_sems.at[a, k], recv_sem=recv_sems.at[a, k],
                    device_id=dev, device_id_type=pl.DeviceIdType.MESH).start()
        for a in range(n):
            for k, (dev, idx) in enumerate(peers):
                pltpu.make_async_remote_copy(
                    src_ref=src(a, idx), dst_ref=outs[a].at[idx], send_sem=send_sems.at[a, k], recv_sem=recv_sems.at[a, k],
                    device_id=dev, device_id_type=pl.DeviceIdType.MESH).wait()
        for cp in local:
            cp.wait()

    hbm = pl.BlockSpec(memory_space=pltpu.HBM)
    return pl.pallas_call(
        body, out_shape=out_shape, in_specs=[hbm] * n, out_specs=[hbm] * n,
        scratch_shapes=[pltpu.SemaphoreType.DMA((n, N_DEV - 1)), pltpu.SemaphoreType.DMA((n, N_DEV - 1)),
                        pltpu.SemaphoreType.DMA((n,))],
        name=name,
    )(*parts)


def _gather_two_level(parts, name):
    n = len(parts)
    out_shape = [jax.ShapeDtypeStruct((N_DEV,) + p.shape, p.dtype) for p in parts]

    def body(*refs):
        ins, outs = refs[:n], refs[n:2 * n]
        send_sems, recv_sems, local_sems = refs[2 * n:]
        x, y, c = lax.axis_index("x"), lax.axis_index("y"), lax.axis_index("c")
        idx = lambda px, py, pc: 4 * px + 2 * py + pc
        me, sibling = (x, y, c), (x, y, 1 - c)
        chips = [(1 - x, y), (x, 1 - y), (1 - x, 1 - y)]

        def copy(a, k, block, to, src=None):
            slot = outs[a].at[idx(*block)]
            return pltpu.make_async_remote_copy(
                src_ref=slot if src is None else src, dst_ref=slot, send_sem=send_sems.at[a, k], recv_sem=recv_sems.at[a, k],
                device_id=to, device_id_type=pl.DeviceIdType.MESH)

        local = [pltpu.make_async_copy(ins[a], outs[a].at[idx(*me)], local_sems.at[a]) for a in range(n)]
        for cp in local:
            cp.start()
        sent = []
        for a in range(n):
            sent.append(copy(a, 0, me, sibling, src=ins[a]))
            sent += [copy(a, 1 + j, me, (*chip, c), src=ins[a]) for j, chip in enumerate(chips)]
        for cp in sent:
            cp.start()
        for a in range(n):
            for j, chip in enumerate(chips):
                copy(a, 1 + j, (*chip, c), me).wait_recv()
                passed = copy(a, 4 + j, (*chip, c), sibling)
                passed.start()
                sent.append(passed)
        for a in range(n):
            copy(a, 0, sibling, me).wait_recv()
            for j, chip in enumerate(chips):
                copy(a, 4 + j, (*chip, 1 - c), me).wait_recv()
        for cp in sent:
            cp.wait_send()
        for cp in local:
            cp.wait()

    hbm = pl.BlockSpec(memory_space=pltpu.HBM)
    return pl.pallas_call(
        body, out_shape=out_shape, in_specs=[hbm] * n, out_specs=[hbm] * n,
        scratch_shapes=[pltpu.SemaphoreType.DMA((n, N_DEV - 1)), pltpu.SemaphoreType.DMA((n, N_DEV - 1)),
                        pltpu.SemaphoreType.DMA((n,))],
        name=name,
    )(*parts)


def _swap_with_sibling(parts, name):
    n = len(parts)

    def body(*refs):
        ins, outs = refs[:n], refs[n:2 * n]
        send_sems, recv_sems = refs[2 * n:]
        x, y, c = lax.axis_index("x"), lax.axis_index("y"), lax.axis_index("c")
        copies = [pltpu.make_async_remote_copy(
            src_ref=ins[a].at[1 - c], dst_ref=outs[a], send_sem=send_sems.at[a], recv_sem=recv_sems.at[a],
            device_id=(x, y, 1 - c), device_id_type=pl.DeviceIdType.MESH) for a in range(n)]
        for cp in copies:
            cp.start()
        for cp in copies:
            cp.wait()

    hbm = pl.BlockSpec(memory_space=pltpu.HBM)
    return pl.pallas_call(
        body, out_shape=[jax.ShapeDtypeStruct(p.shape[1:], p.dtype) for p in parts], in_specs=[hbm] * n, out_specs=[hbm] * n,
        scratch_shapes=[pltpu.SemaphoreType.DMA((n,)), pltpu.SemaphoreType.DMA((n,))],
        name=name,
    )(*parts)


def _pair_sum(a, b, name):
    R, C = a.shape
    rb = _adam_rows(R)

    def body(a_ref, b_ref, o_ref):
        o_ref[...] = (a_ref[...].astype(F32) + b_ref[...].astype(F32)).astype(BF16)

    blk = pl.BlockSpec((rb, C), lambda i: (i, 0))
    return pl.pallas_call(body, grid=(R // rb,), in_specs=[blk, blk], out_specs=blk,
                          out_shape=jax.ShapeDtypeStruct((R, C), BF16), name=name, compiler_params=_cparams())(a, b)


def _exchange_chips(parts, name):
    n = len(parts)
    n_chips = N_DEV // 2

    def body(*refs):
        ins, outs = refs[:n], refs[n:2 * n]
        send_sems, recv_sems, local_sems = refs[2 * n:]
        x, y, c = lax.axis_index("x"), lax.axis_index("y"), lax.axis_index("c")
        mine = 2 * x + y
        chips = [(1 - x, y), (x, 1 - y), (1 - x, 1 - y)]
        local = [pltpu.make_async_copy(ins[a].at[mine], outs[a].at[mine], local_sems.at[a]) for a in range(n)]
        for cp in local:
            cp.start()
        for a in range(n):
            for k, (px, py) in enumerate(chips):
                pltpu.make_async_remote_copy(
                    src_ref=ins[a].at[2 * px + py], dst_ref=outs[a].at[mine], send_sem=send_sems.at[a, k],
                    recv_sem=recv_sems.at[a, k], device_id=(px, py, c), device_id_type=pl.DeviceIdType.MESH).start()
        for a in range(n):
            for k, (px, py) in enumerate(chips):
                pltpu.make_async_remote_copy(
                    src_ref=ins[a].at[2 * px + py], dst_ref=outs[a].at[2 * px + py], send_sem=send_sems.at[a, k],
                    recv_sem=recv_sems.at[a, k], device_id=(px, py, c), device_id_type=pl.DeviceIdType.MESH).wait()
        for cp in local:
            cp.wait()

    hbm = pl.BlockSpec(memory_space=pltpu.HBM)
    return pl.pallas_call(
        body, out_shape=[jax.ShapeDtypeStruct(p.shape, p.dtype) for p in parts], in_specs=[hbm] * n, out_specs=[hbm] * n,
        scratch_shapes=[pltpu.SemaphoreType.DMA((n, n_chips - 1)), pltpu.SemaphoreType.DMA((n, n_chips - 1)),
                        pltpu.SemaphoreType.DMA((n,))],
        name=name,
    )(*parts)


def _adam_rows(rows):
    for rb in (128, 64, 40, 16, 8):
        if rows % rb == 0:
            return rb
    return rows


def _adamw(stack, w, m, v, name):
    R, C = w.shape
    rb = _adam_rows(R)
    slots = stack.shape[0]

    def body(s_ref, w_ref, m_ref, v_ref, g_ref, d_ref, nm_ref, nv_ref):
        g = s_ref[0].astype(F32)
        for s in range(1, slots):
            g = g + s_ref[s].astype(F32)
        nm = ADAM_B1 * m_ref[...] + (1.0 - ADAM_B1) * g
        nv = ADAM_B2 * v_ref[...] + (1.0 - ADAM_B2) * (g * g)
        m_hat = nm / (1.0 - ADAM_B1 ** ADAM_STEP)
        v_hat = nv / (1.0 - ADAM_B2 ** ADAM_STEP)
        g_ref[...] = g
        d_ref[...] = -ADAM_LR * (m_hat / (jnp.sqrt(v_hat) + ADAM_EPS) + ADAM_WD * w_ref[...])
        nm_ref[...] = nm
        nv_ref[...] = nv

    blk = pl.BlockSpec((rb, C), lambda i: (i, 0))
    return pl.pallas_call(
        body, grid=(R // rb,),
        in_specs=[pl.BlockSpec((slots, rb, C), lambda i: (0, i, 0)), blk, blk, blk],
        out_specs=[blk] * 4, out_shape=[jax.ShapeDtypeStruct((R, C), F32)] * 4,
        name=name, compiler_params=_cparams(),
    )(stack, w, m, v)


def _pad_rows8(a):
    return jnp.concatenate([a, jnp.zeros((8 - a.shape[0], a.shape[1]), a.dtype)], axis=0) if a.shape[0] < 8 else a


def _pack_small(norm_w, qnw, knw, sinks, a_log, dt_bias, onw, extra):
    z = lambda n: jnp.zeros((1, n), F32)
    row = jnp.concatenate([norm_w, qnw, knw, sinks, a_log, dt_bias, z(80), onw, extra, z(512)], axis=1)
    return row.reshape(16, 128)


def _unpack_small(p):
    row = p.reshape(1, 2048)
    cut = lambda a, n: row[:, a:a + n]
    return (cut(0, 1024), cut(1024, 64), cut(1088, 64), cut(1152, 16), cut(1168, 16), cut(1184, 16), cut(1280, 128),
            cut(1408, 128))


def _pack_rows(w_in_a, w_in_d, w_out_a, w_out_d, meta, conv, dn_norm):
    a = jnp.concatenate([w_in_a, w_in_d], axis=1)
    b = jnp.concatenate([w_out_a, w_out_d], axis=0)
    c = jnp.concatenate([meta, conv.reshape(16, 128), _pad_rows8(dn_norm)], axis=0)
    return a, b, c


def _unpack_rows(a, b, c):
    return (a[:, :288], a[:, 288:], b[:128], b[128:], c[:16], c[16:32].reshape(4, 512), c[32:33])


def _local_step(h0, target, w):
    xn0, q, kv, gate = _attn_in_fwd(h0, w["attn_norm_w"], w["attn_w_in"])
    o = _attn_core_fwd(q, kv, w["attn_sinks"], w["attn_q_norm_w"], w["attn_k_norm_w"])
    h1 = _attn_out_fwd(o, gate, h0, w["attn_w_out"])
    xn1, qkv, z, ba = _dn_in_fwd(h1, w["dn_norm_w"], w["dn_w_in"])
    qn, kn, v, bg = _dn_conv_fwd(qkv, ba, w["dn_conv_w"], w["dn_a_log"], w["dn_dt_bias"])
    o_dn, ssave, inv = _dn_scan_fwd(qn, kn, v, bg)
    dy, og_dn, loss = _dn_out_fwd(o_dn, z, h1, target, w["dn_w_out"], w["dn_o_norm_w"])

    g = {}
    do_dn, dz, g["dn_o_norm_w"] = _dn_out_bwd(dy, o_dn, z, w["dn_w_out"], w["dn_o_norm_w"])
    g["dn_w_out"] = _wgrad(og_dn, dy, 1024, "wgrad_dn_out")
    dqn, dkn, dv, dbg = _dn_scan_bwd(do_dn, qn, kn, v, bg, ssave, inv)
    dqkv, dba, g["dn_conv_w"], g["dn_a_log"], g["dn_dt_bias"] = _dn_conv_bwd(
        dqn, dkn, dv, dbg, qkv, ba, w["dn_conv_w"], w["dn_a_log"], w["dn_dt_bias"])
    dh1, g["dn_norm_w"] = _dn_in_bwd(dqkv, dz, dba, h1, dy, w["dn_norm_w"], w["dn_w_in"])
    g["dn_w_in"] = jnp.concatenate([_wgrad(xn1, dqkv, 1024, "wgrad_dn_qkv"), _wgrad(xn1, dz, 1024, "wgrad_dn_z"),
                                    _wgrad(xn1, dba, 32, "wgrad_dn_ba")], axis=1)
    do, dgate, g["attn_w_out"] = _attn_out_bwd(dh1, o, gate, w["attn_w_out"])
    dq, dkv, g["attn_sinks"], g["attn_q_norm_w"], g["attn_k_norm_w"] = _attn_core_bwd(
        do, q, kv, w["attn_sinks"], w["attn_q_norm_w"], w["attn_k_norm_w"])
    dh0, g["attn_norm_w"] = _attn_in_bwd(dq, dkv, dgate, h0, dh1, w["attn_norm_w"], w["attn_w_in"])
    g["attn_w_in"] = jnp.concatenate([_wgrad(xn0, dq, 1024, "wgrad_attn_q"), _wgrad(xn0, dkv, 256, "wgrad_attn_kv"),
                                      _wgrad(xn0, dgate, 1024, "wgrad_attn_gate")], axis=1)
    return loss, dh0, g


WEIGHTS = ['meta_tokens', 'attn_norm_w', 'attn_w_in', 'attn_q_norm_w', 'attn_k_norm_w', 'attn_sinks', 'attn_w_out',
           'dn_norm_w', 'dn_w_in', 'dn_conv_w', 'dn_a_log', 'dn_dt_bias', 'dn_o_norm_w', 'dn_w_out']
SMALL = ['attn_norm_w', 'attn_q_norm_w', 'attn_k_norm_w', 'attn_sinks', 'dn_a_log', 'dn_dt_bias', 'dn_o_norm_w']


def kernel(x, meta_tokens, attn_norm_w, attn_w_in, attn_q_norm_w, attn_k_norm_w, attn_sinks, attn_w_out, dn_norm_w, dn_w_in, dn_conv_w, dn_a_log, dn_dt_bias, dn_o_norm_w, dn_w_out, loss_target, m_meta_tokens, m_attn_norm_w, m_attn_w_in, m_attn_q_norm_w, m_attn_k_norm_w, m_attn_sinks, m_attn_w_out, m_dn_norm_w, m_dn_w_in, m_dn_conv_w, m_dn_a_log, m_dn_dt_bias, m_dn_o_norm_w, m_dn_w_out, v_meta_tokens, v_attn_norm_w, v_attn_w_in, v_attn_q_norm_w, v_attn_k_norm_w, v_attn_sinks, v_attn_w_out, v_dn_norm_w, v_dn_w_in, v_dn_conv_w, v_dn_a_log, v_dn_dt_bias, v_dn_o_norm_w, v_dn_w_out):
    shard = dict(meta_tokens=meta_tokens, attn_norm_w=attn_norm_w, attn_w_in=attn_w_in[0], attn_q_norm_w=attn_q_norm_w,
                 attn_k_norm_w=attn_k_norm_w, attn_sinks=attn_sinks, attn_w_out=attn_w_out[0], dn_norm_w=dn_norm_w,
                 dn_w_in=dn_w_in[0], dn_conv_w=dn_conv_w[0], dn_a_log=dn_a_log, dn_dt_bias=dn_dt_bias,
                 dn_o_norm_w=dn_o_norm_w, dn_w_out=dn_w_out[0])
    mom_m = dict(meta_tokens=m_meta_tokens, attn_norm_w=m_attn_norm_w, attn_w_in=m_attn_w_in[0], attn_q_norm_w=m_attn_q_norm_w,
                 attn_k_norm_w=m_attn_k_norm_w, attn_sinks=m_attn_sinks, attn_w_out=m_attn_w_out[0], dn_norm_w=m_dn_norm_w,
                 dn_w_in=m_dn_w_in[0], dn_conv_w=m_dn_conv_w[0], dn_a_log=m_dn_a_log, dn_dt_bias=m_dn_dt_bias,
                 dn_o_norm_w=m_dn_o_norm_w, dn_w_out=m_dn_w_out[0])
    mom_v = dict(meta_tokens=v_meta_tokens, attn_norm_w=v_attn_norm_w, attn_w_in=v_attn_w_in[0], attn_q_norm_w=v_attn_q_norm_w,
                 attn_k_norm_w=v_attn_k_norm_w, attn_sinks=v_attn_sinks, attn_w_out=v_attn_w_out[0], dn_norm_w=v_dn_norm_w,
                 dn_w_in=v_dn_w_in[0], dn_conv_w=v_dn_conv_w[0], dn_a_log=v_dn_a_log, dn_dt_bias=v_dn_dt_bias,
                 dn_o_norm_w=v_dn_o_norm_w, dn_w_out=v_dn_w_out[0])

    def rows_of(d):
        return _pack_rows(d["attn_w_in"], d["dn_w_in"], d["attn_w_out"], d["dn_w_out"], d["meta_tokens"], d["dn_conv_w"],
                          d["dn_norm_w"])

    def small_of(d, extra):
        return _pack_small(*[d[k] for k in SMALL], extra)

    wa, wb, wc = rows_of(shard)
    ga, gb, gc = _gather_two_level([wa.astype(BF16), wb.astype(BF16), wc], "gather_weights")
    full = {k: shard[k] for k in SMALL}
    full["attn_w_in"] = ga[:, :, :288].transpose(1, 0, 2).reshape(1024, 2304)
    full["dn_w_in"] = ga[:, :, 288:].transpose(1, 0, 2).reshape(1024, 6176)
    full["attn_w_out"] = gb[:, :128].reshape(1024, 1024)
    full["dn_w_out"] = gb[:, 128:].reshape(2048, 1024)
    meta_full = gc[:, :16].transpose(1, 0, 2).reshape(N_META, 1024)
    full["dn_conv_w"] = gc[:, 16:32].reshape(N_DEV, 4, 512).transpose(1, 0, 2).reshape(4, 4096)
    full["dn_norm_w"] = gc[:, 32].reshape(1, 1024)

    seq = x.shape[1]
    h0 = jnp.concatenate([jnp.zeros((FRONT_PAD, D_MODEL), F32), meta_full, x[0]], axis=0)
    target = jnp.concatenate([jnp.zeros((ATTN_BLOCK, D_MODEL), F32), loss_target[0]], axis=0)
    loss, dh0, g = _local_step(h0, target, full)
    grad_x = dh0[ATTN_BLOCK:ATTN_BLOCK + seq][None]
    g["meta_tokens"] = dh0[FRONT_PAD:ATTN_BLOCK]

    pa = jnp.concatenate([g["attn_w_in"].reshape(1024, N_DEV, 288), g["dn_w_in"].reshape(1024, N_DEV, 772)],
                         axis=2).transpose(1, 0, 2)
    pb = jnp.concatenate([g["attn_w_out"].reshape(N_DEV, 128, 1024), g["dn_w_out"].reshape(N_DEV, 256, 1024)], axis=1)
    dn_norm8 = jnp.concatenate([g["dn_norm_w"].reshape(N_DEV, 1, 128), jnp.zeros((N_DEV, 7, 128), F32)], axis=1)
    pc = jnp.concatenate([g["meta_tokens"].reshape(N_META, N_DEV, 128).transpose(1, 0, 2),
                          g["dn_conv_w"].reshape(4, N_DEV, 512).transpose(1, 0, 2).reshape(N_DEV, 16, 128), dn_norm8], axis=1)
    ps = small_of(g, loss)
    c = lax.axis_index("c")
    by_core = lambda p: p.astype(BF16).reshape((N_DEV // 2, 2) + p.shape[1:]).swapaxes(0, 1)
    pa2, pb2 = by_core(pa), by_core(pb)
    ra, rb_ = _swap_with_sibling([pa2, pb2], "swap_grads")
    own = lambda p2: lax.dynamic_index_in_dim(p2, c, axis=0, keepdims=False)
    flat = lambda t: t.reshape((-1,) + t.shape[2:])
    sa = _pair_sum(flat(own(pa2)), flat(ra), "pair_sum_a").reshape(ra.shape)
    sb = _pair_sum(flat(own(pb2)), flat(rb_), "pair_sum_b").reshape(rb_.shape)
    xa, xb = _exchange_chips([sa, sb], "exchange_grads")
    xc, xs = _exchange([pc, ps], [True, False], "exchange_small")

    out = {}
    ma, mb, mc = rows_of(mom_m)
    va, vb, vc = rows_of(mom_v)
    ra = _adamw(xa, wa, ma, va, "adamw_a")
    rb = _adamw(xb, wb, mb, vb, "adamw_b")
    rc = _adamw(xc, wc, mc, vc, "adamw_c")
    zero = jnp.zeros((1, 128), F32)
    rs = _adamw(xs, small_of(shard, zero), small_of(mom_m, zero), small_of(mom_v, zero), "adamw_small")
    row_names = ["attn_w_in", "dn_w_in", "attn_w_out", "dn_w_out", "meta_tokens", "dn_conv_w", "dn_norm_w"]
    lead = {"attn_w_in", "dn_w_in", "attn_w_out", "dn_w_out", "dn_conv_w"}
    for kind in range(4):
        vals = dict(zip(row_names, _unpack_rows(ra[kind], rb[kind], rc[kind])))
        small = _unpack_small(rs[kind])
        vals.update(dict(zip(SMALL, small[:7])))
        if kind == 0:
            loss_total = small[7][0, 0]
        out[kind] = [vals[k][None] if k in lead else vals[k] for k in WEIGHTS]
    return (loss_total, grad_x, *out[0], *out[1], *out[2], *out[3])
```

```python
import functools
import math

import jax
import jax.numpy as jnp
from jax import lax
from jax.experimental import pallas as pl
from jax.experimental.pallas import tpu as pltpu

F32, BF16 = jnp.float32, jnp.bfloat16

D_MODEL = 1024
N_META = 16
NORM_EPS = 1e-6
ATTN_HEADS, ATTN_KV_HEADS, ATTN_GROUPS, ATTN_HD = 16, 2, 8, 64
ATTN_BLOCK = 128
FRONT_PAD = ATTN_BLOCK - N_META
DN_HD, DN_K_HEADS, DN_V_HEADS = 128, 8, 16
DN_CHUNK = 128
TRI_BLOCK = 64
SCAN_FWD_GROUP = 8
SCAN_BWD_GROUP = 8
DN_KEY_W, DN_VAL_W = 1024, 2048
DN_CONV_W = 2 * DN_KEY_W + DN_VAL_W
DN_CONV_K = 4
N_DEV = 8
ROW_BLOCK = 384
WGRAD_ROWS = 1376
VMEM_LIMIT = 56 * 1024 * 1024
NEG = -1e30

ADAM_LR, ADAM_B1, ADAM_B2, ADAM_EPS, ADAM_WD, ADAM_STEP = 0.001, 0.9, 0.999, 1e-08, 0.01, 10

NT = (((1,), (1,)), ((), ()))
TN = (((0,), (0,)), ((), ()))


def _cparams(sem=("arbitrary",)):
    return pltpu.CompilerParams(dimension_semantics=sem, vmem_limit_bytes=VMEM_LIMIT)


def _rms(x, w):
    return x * lax.rsqrt(jnp.mean(x * x, axis=-1, keepdims=True) + NORM_EPS) * w


def _silu(x):
    return x * jax.nn.sigmoid(x)


def _softplus(x):
    return jnp.maximum(x, 0.0) + jnp.log(1.0 + jnp.exp(-jnp.abs(x)))


NN = (((1,), (0,)), ((), ()))


def _mm(a, b, dims):
    return lax.dot_general(a.astype(BF16), b.astype(BF16), dims, preferred_element_type=F32)


@functools.partial(jax.custom_vjp, nondiff_argnums=(2,))
def _bdot_vjp(a, b, dims):
    return _mm(a, b, dims)


def _bdot_fwd(a, b, dims):
    a16, b16 = a.astype(BF16), b.astype(BF16)
    return _mm(a16, b16, dims), (a16, b16, jnp.zeros((), a.dtype), jnp.zeros((), b.dtype))


def _bdot_bwd(dims, res, g):
    a16, b16, ta, tb = res
    g16 = g.astype(BF16)
    if dims == NN:
        da, db = _mm(g16, b16, NT), _mm(a16, g16, TN)
    elif dims == NT:
        da, db = _mm(g16, b16, NN), _mm(g16, a16, TN)
    else:
        da, db = _mm(b16, g16, NT), _mm(a16, g16, NN)
    return da.astype(ta.dtype), db.astype(tb.dtype)


_bdot_vjp.defvjp(_bdot_fwd, _bdot_bwd)


def _bdot(a, b, dims=NN):
    return _bdot_vjp(a, b, dims)


def _hdot(a, b):
    return jnp.dot(a, b, preferred_element_type=F32, precision=lax.Precision.HIGHEST)


def _row_call(name, body, n_rows, rb, rows, consts, outs, accs=(), reverse=False, scratch=(), halos=()):
    n = n_rows // rb
    assert n * rb == n_rows
    idx = (lambda i: (n - 1 - i, 0)) if reverse else (lambda i: (i, 0))
    in_specs = [pl.BlockSpec((rb, a.shape[1]), idx) for a in rows]
    in_specs += [pl.BlockSpec((hr, a.shape[1]), fn) for a, hr, fn in halos]
    in_specs += [pl.BlockSpec(c.shape, functools.partial(lambda i, nd: (0,) * nd, nd=c.ndim)) for c in consts]
    out_specs = [pl.BlockSpec((rb, c), idx) for c, _ in outs]
    out_specs += [pl.BlockSpec(s, functools.partial(lambda i, nd: (0,) * nd, nd=len(s))) for s, _ in accs]
    out_shape = [jax.ShapeDtypeStruct((n_rows, c), dt) for c, dt in outs]
    out_shape += [jax.ShapeDtypeStruct(s, dt) for s, dt in accs]
    return pl.pallas_call(
        body, grid=(n,), in_specs=in_specs, out_specs=out_specs, out_shape=out_shape,
        scratch_shapes=list(scratch), name=name, compiler_params=_cparams(),
    )(*rows, *[a for a, _, _ in halos], *consts)


def _token_views(x):
    per = ROW_BLOCK // ATTN_BLOCK
    return [(x, ATTN_BLOCK, functools.partial(lambda i, k: (jnp.maximum(per * i - 1 + k, 0), 0), k=k)) for k in range(per)]


def _padded_block(i, front, views):
    first = jnp.where(i == 0, front, views[0][...]) if front is not None else views[0][...]
    return jnp.concatenate([first] + [v[...] for v in views[1:]], axis=0)


def _attn_in_fwd(x, front, norm_w, w_in):
    T = x.shape[0] + ATTN_BLOCK

    def body(xa_ref, xb_ref, xc_ref, front_ref, nw_ref, w_ref, xn_ref, q_ref, kv_ref, gate_ref):
        h = _padded_block(pl.program_id(0), front_ref[...], (xa_ref, xb_ref, xc_ref))
        xn = _rms(h, nw_ref[...]).astype(BF16)
        xn_ref[...] = xn
        q_ref[...] = jnp.dot(xn, w_ref[:, 0:1024], preferred_element_type=F32)
        kv_ref[...] = jnp.dot(xn, w_ref[:, 1024:1280], preferred_element_type=F32)
        gate_ref[...] = jnp.dot(xn, w_ref[:, 1280:2304], preferred_element_type=F32)

    return _row_call("attn_in_fwd", body, T, ROW_BLOCK, [], [front, norm_w, w_in],
                     [(1024, BF16), (1024, F32), (256, F32), (1024, F32)], halos=_token_views(x))


def _attn_bias(n, j):
    C, R = 2 * ATTN_BLOCK + N_META, ATTN_GROUPS * ATTN_BLOCK
    c = lax.broadcasted_iota(jnp.int32, (C, R), 0)
    r = lax.broadcasted_iota(jnp.int32, (C, R), 1)
    ql = r & (ATTN_BLOCK - 1)
    is_meta = c >= 2 * ATTN_BLOCK
    dist_band = ATTN_BLOCK + ql - c
    cmin = jnp.maximum(0, 2 * ATTN_BLOCK - ATTN_BLOCK * n)
    valid_band = (c >= cmin) & (dist_band >= 0) & (dist_band < ATTN_BLOCK)
    dist_meta = ATTN_BLOCK * n + ql - FRONT_PAD - (c - 2 * ATTN_BLOCK)
    valid = (is_meta & (dist_meta >= 0)) | (jnp.logical_not(is_meta) & valid_band)
    dist = jnp.minimum(jnp.where(is_meta, dist_meta, dist_band), ATTN_BLOCK).astype(F32)
    rr = lax.broadcasted_iota(jnp.int32, (1, R), 1)
    head = (rr >> 7).astype(F32) + float(ATTN_GROUPS * j + 1)
    slope = jnp.exp(head * (-0.5 * math.log(2.0)))
    return jnp.where(valid, slope * dist, -NEG)


def _attn_tables(n, refresh, bias_ref):
    @pl.when(refresh)
    def _():
        for j in range(ATTN_KV_HEADS):
            bias_ref[j] = _attn_bias(n, j)


def _attn_table_scratch():
    return [pltpu.VMEM((ATTN_KV_HEADS, 2 * ATTN_BLOCK + N_META, ATTN_GROUPS * ATTN_BLOCK), F32)]


def _attn_group(q_t, k, v, sinkrow, qnw_col, knw, bias):
    qn = q_t * lax.rsqrt(jnp.mean(q_t * q_t, axis=0, keepdims=True) + NORM_EPS) * qnw_col
    kn = _rms(k, knw)
    s = _bdot(kn, qn) * (ATTN_HD ** -0.5) - bias
    m = lax.stop_gradient(jnp.maximum(jnp.max(s, axis=0, keepdims=True), sinkrow))
    e = jnp.exp(s - m)
    denom = jnp.sum(e, axis=0, keepdims=True) + jnp.exp(sinkrow - m)
    p = e * (1.0 / denom)
    return _bdot(v, p, TN)


def _sink_row(sinks_ref, j):
    rr = lax.broadcasted_iota(jnp.int32, (1, ATTN_GROUPS * ATTN_BLOCK), 1) >> 7
    row = jnp.zeros((1, ATTN_GROUPS * ATTN_BLOCK), F32)
    for hl in range(ATTN_GROUPS):
        row = jnp.where(rr == hl, sinks_ref[0, ATTN_GROUPS * j + hl], row)
    return row


def _heads_to_lanes(ref, j):
    return jnp.concatenate([ref[:, ATTN_HD * h:ATTN_HD * (h + 1)].T
                            for h in range(ATTN_GROUPS * j, ATTN_GROUPS * (j + 1))], axis=1)


def _lanes_to_heads(ref, j, x_t):
    for hl in range(ATTN_GROUPS):
        h = ATTN_GROUPS * j + hl
        ref[:, ATTN_HD * h:ATTN_HD * (h + 1)] = x_t[:, ATTN_BLOCK * hl:ATTN_BLOCK * (hl + 1)].T


def _attn_kv_tiles(kvp_ref, kvc_ref, kvm_ref, j):
    ksl = slice(ATTN_HD * j, ATTN_HD * (j + 1))
    vsl = slice(128 + ATTN_HD * j, 128 + ATTN_HD * (j + 1))
    k = jnp.concatenate([kvp_ref[:, ksl], kvc_ref[:, ksl], kvm_ref[FRONT_PAD:, ksl]], axis=0)
    v = jnp.concatenate([kvp_ref[:, vsl], kvc_ref[:, vsl], kvm_ref[FRONT_PAD:, vsl]], axis=0)
    return k, v


def _attn_core_fwd(q, kv, sinks, qnw, knw):
    T = q.shape[0]
    nb = T // ATTN_BLOCK

    def body(sinks_ref, q_ref, kvc_ref, kvp_ref, kvm_ref, qnw_ref, knw_ref, o_ref, bias_ref):
        n = pl.program_id(0)
        _attn_tables(n, n <= 2, bias_ref)
        for j in range(ATTN_KV_HEADS):
            k, v = _attn_kv_tiles(kvp_ref, kvc_ref, kvm_ref, j)
            o_t = _attn_group(_heads_to_lanes(q_ref, j), k, v, _sink_row(sinks_ref, j), qnw_ref[...], knw_ref[...],
                              bias_ref[j])
            _lanes_to_heads(o_ref, j, o_t)

    return pl.pallas_call(
        body, grid=(nb,),
        in_specs=[pl.BlockSpec(memory_space=pltpu.SMEM),
                  pl.BlockSpec((ATTN_BLOCK, 1024), lambda i: (i, 0)),
                  pl.BlockSpec((ATTN_BLOCK, 256), lambda i: (i, 0)),
                  pl.BlockSpec((ATTN_BLOCK, 256), lambda i: (jnp.maximum(i - 1, 0), 0)),
                  pl.BlockSpec((ATTN_BLOCK, 256), lambda i: (0, 0)),
                  pl.BlockSpec((ATTN_HD, 1), lambda i: (0, 0)),
                  pl.BlockSpec((1, ATTN_HD), lambda i: (0, 0))],
        out_specs=pl.BlockSpec((ATTN_BLOCK, 1024), lambda i: (i, 0)),
        out_shape=jax.ShapeDtypeStruct((T, 1024), F32),
        scratch_shapes=_attn_table_scratch(),
        name="attn_core_fwd", compiler_params=_cparams(),
    )(sinks, q, kv, kv, kv, qnw.reshape(ATTN_HD, 1), knw)


def _attn_out_fwd(o, gate, x, front, w_out):
    T = o.shape[0]

    def body(o_ref, g_ref, xa_ref, xb_ref, xc_ref, front_ref, w_ref, h1_ref):
        h = _padded_block(pl.program_id(0), front_ref[...], (xa_ref, xb_ref, xc_ref))
        og = o_ref[...] * _silu(g_ref[...])
        h1_ref[...] = h + _bdot(og, w_ref[...])

    return _row_call("attn_out_fwd", body, T, ROW_BLOCK, [o, gate], [front, w_out], [(1024, F32)], halos=_token_views(x))[0]


def _wgrad(xn, du, cg, name):
    T, kdim = xn.shape
    cdim = du.shape[1]
    rows = WGRAD_ROWS if T % WGRAD_ROWS == 0 else ROW_BLOCK
    nr, nc = T // rows, cdim // cg
    assert nc * cg == cdim

    def body(x_ref, du_ref, dw_ref):
        @pl.when(pl.program_id(1) == 0)
        def _():
            dw_ref[...] = jnp.zeros_like(dw_ref)
        dw_ref[...] += _bdot(x_ref[...], du_ref[...], TN)

    return pl.pallas_call(
        body, grid=(nc, nr),
        in_specs=[pl.BlockSpec((rows, kdim), lambda j, i: (i, 0)),
                  pl.BlockSpec((rows, cg), lambda j, i: (i, j))],
        out_specs=pl.BlockSpec((kdim, cg), lambda j, i: (0, j)),
        out_shape=jax.ShapeDtypeStruct((kdim, cdim), F32),
        name=name, compiler_params=_cparams(("arbitrary", "arbitrary")),
    )(xn, du)


def _attn_out_bwd(dh1, o, gate, w_out):
    T = o.shape[0]

    def body(dh_ref, o_ref, g_ref, w_ref, do_ref, dg_ref, dw_ref):
        @pl.when(pl.program_id(0) == 0)
        def _():
            dw_ref[...] = jnp.zeros_like(dw_ref)
        dh = dh_ref[...]
        dog = _bdot(dh, w_ref[...], NT)
        og, vjp = jax.vjp(lambda o_, g_: o_ * _silu(g_), o_ref[...], g_ref[...])
        do, dg = vjp(dog)
        do_ref[...] = do
        dg_ref[...] = dg
        dw_ref[...] += _bdot(og, dh, TN)

    return _row_call("attn_out_bwd", body, T, ROW_BLOCK, [dh1, o, gate], [w_out],
                     [(1024, F32), (1024, F32)], [((1024, 1024), F32)])


def _attn_core_bwd(do, q, kv, sinks, qnw, knw):
    T = q.shape[0]
    nb = T // ATTN_BLOCK
    rev = lambda i: nb - 1 - i

    def body(sinks_ref, do_ref, q_ref, kvc_ref, kvp_ref, kvm_ref, qnw_ref, knw_ref,
             dq_ref, dkv_ref, dsinks_ref, dqnw_ref, dknw_ref, carry_ref, meta_ref, bias_ref):
        step = pl.program_id(0)
        n = rev(step)
        _attn_tables(n, (step == 0) | (n <= 1), bias_ref)

        @pl.when(step == 0)
        def _():
            carry_ref[...] = jnp.zeros_like(carry_ref)
            meta_ref[...] = jnp.zeros_like(meta_ref)
            dsinks_ref[...] = jnp.zeros_like(dsinks_ref)
            dqnw_ref[...] = jnp.zeros_like(dqnw_ref)
            dknw_ref[...] = jnp.zeros_like(dknw_ref)

        lane16 = lax.broadcasted_iota(jnp.int32, (1, ATTN_HEADS), 1)
        dsinks = jnp.zeros((1, ATTN_HEADS), F32)
        for j in range(ATTN_KV_HEADS):
            k, v = _attn_kv_tiles(kvp_ref, kvc_ref, kvm_ref, j)
            fn = functools.partial(_attn_group, bias=bias_ref[j])
            _, vjp = jax.vjp(fn, _heads_to_lanes(q_ref, j), k, v, _sink_row(sinks_ref, j), qnw_ref[...], knw_ref[...])
            dq_t, dk, dv, dsr, dqn, dkn = vjp(_heads_to_lanes(do_ref, j))
            _lanes_to_heads(dq_ref, j, dq_t)
            dqnw_ref[...] += dqn
            dknw_ref[...] += dkn
            for hl in range(ATTN_GROUPS):
                dsinks = dsinks + jnp.where(lane16 == ATTN_GROUPS * j + hl,
                                            jnp.sum(dsr[:, ATTN_BLOCK * hl:ATTN_BLOCK * (hl + 1)]), 0.0)
            ksl = slice(ATTN_HD * j, ATTN_HD * (j + 1))
            vsl = slice(128 + ATTN_HD * j, 128 + ATTN_HD * (j + 1))
            for sl, d in ((ksl, dk), (vsl, dv)):
                dkv_ref[:, sl] = d[ATTN_BLOCK:2 * ATTN_BLOCK, :] + carry_ref[:, sl]
                carry_ref[:, sl] = d[0:ATTN_BLOCK, :]
                meta_ref[:, sl] += d[2 * ATTN_BLOCK:, :]
        dsinks_ref[...] += dsinks

        @pl.when(n == 0)
        def _():
            dkv_ref[FRONT_PAD:, :] += meta_ref[...]

    dq, dkv, dsinks, dqnw, dknw = pl.pallas_call(
        body, grid=(nb,),
        in_specs=[pl.BlockSpec(memory_space=pltpu.SMEM),
                  pl.BlockSpec((ATTN_BLOCK, 1024), lambda i: (rev(i), 0)),
                  pl.BlockSpec((ATTN_BLOCK, 1024), lambda i: (rev(i), 0)),
                  pl.BlockSpec((ATTN_BLOCK, 256), lambda i: (rev(i), 0)),
                  pl.BlockSpec((ATTN_BLOCK, 256), lambda i: (jnp.maximum(rev(i) - 1, 0), 0)),
                  pl.BlockSpec((ATTN_BLOCK, 256), lambda i: (0, 0)),
                  pl.BlockSpec((ATTN_HD, 1), lambda i: (0, 0)),
                  pl.BlockSpec((1, ATTN_HD), lambda i: (0, 0))],
        out_specs=[pl.BlockSpec((ATTN_BLOCK, 1024), lambda i: (rev(i), 0)),
                   pl.BlockSpec((ATTN_BLOCK, 256), lambda i: (rev(i), 0)),
                   pl.BlockSpec((1, ATTN_HEADS), lambda i: (0, 0)),
                   pl.BlockSpec((ATTN_HD, 1), lambda i: (0, 0)),
                   pl.BlockSpec((1, ATTN_HD), lambda i: (0, 0))],
        out_shape=[jax.ShapeDtypeStruct((T, 1024), F32), jax.ShapeDtypeStruct((T, 256), F32),
                   jax.ShapeDtypeStruct((1, ATTN_HEADS), F32), jax.ShapeDtypeStruct((ATTN_HD, 1), F32),
                   jax.ShapeDtypeStruct((1, ATTN_HD), F32)],
        scratch_shapes=[pltpu.VMEM((ATTN_BLOCK, 256), F32), pltpu.VMEM((N_META, 256), F32)] + _attn_table_scratch(),
        name="attn_core_bwd", compiler_params=_cparams(),
    )(sinks, do, q, kv, kv, kv, qnw.reshape(ATTN_HD, 1), knw)
    return dq, dkv, dsinks, dqnw.reshape(1, ATTN_HD), dknw


def _attn_in_bwd(dq, dkv, dgate, x, front, dh1, norm_w, w_in):
    T = dq.shape[0]

    def body(dq_ref, dkv_ref, dg_ref, dh1_ref, xa_ref, xb_ref, xc_ref, front_ref, nw_ref, w_ref, dh0_ref, dnw_ref):
        @pl.when(pl.program_id(0) == 0)
        def _():
            dnw_ref[...] = jnp.zeros_like(dnw_ref)
        h = _padded_block(pl.program_id(0), front_ref[...], (xa_ref, xb_ref, xc_ref))
        dxn = (_bdot(dq_ref[...], w_ref[:, 0:1024], NT) + _bdot(dkv_ref[...], w_ref[:, 1024:1280], NT)
               + _bdot(dg_ref[...], w_ref[:, 1280:2304], NT))
        _, vjp = jax.vjp(_rms, h, nw_ref[...])
        dh, dnw = vjp(dxn)
        dh0_ref[...] = dh1_ref[...] + dh
        dnw_ref[...] += dnw

    return _row_call("attn_in_bwd", body, T, ROW_BLOCK, [dq, dkv, dgate, dh1], [front, norm_w, w_in],
                     [(1024, F32)], [((1, 1024), F32)], halos=_token_views(x))


def _dn_in_fwd(h1, norm_w, w_in):
    T = h1.shape[0]

    def body(h_ref, nw_ref, w_ref, xn_ref, qkv_ref, z_ref, ba_ref):
        xn = _rms(h_ref[...], nw_ref[...]).astype(BF16)
        xn_ref[...] = xn
        qkv_ref[...] = jnp.dot(xn, w_ref[:, 0:4096], preferred_element_type=F32)
        z_ref[...] = jnp.dot(xn, w_ref[:, 4096:6144], preferred_element_type=F32)
        ba_ref[...] = jnp.dot(xn, w_ref[:, 6144:6176], preferred_element_type=F32)

    return _row_call("dn_in_fwd", body, T, ROW_BLOCK, [h1], [norm_w, w_in],
                     [(1024, BF16), (4096, F32), (2048, F32), (32, F32)])


def _shift_down(cur, prev8, s):
    i8 = lax.broadcasted_iota(jnp.int32, (8, cur.shape[1]), 0)
    r = pltpu.roll(cur, s, 0)
    head = jnp.where(i8 < s, pltpu.roll(prev8, s, 0), r[0:8])
    return jnp.concatenate([head, r[8:]], axis=0)


def _shift_up(cur, next8, s):
    n = cur.shape[0]
    i8 = lax.broadcasted_iota(jnp.int32, (8, cur.shape[1]), 0)
    r = pltpu.roll(cur, n - s, 0)
    tail = jnp.where(i8 >= 8 - s, pltpu.roll(next8, 8 - s, 0), r[n - 8:])
    return jnp.concatenate([r[:n - 8], tail], axis=0)


def _conv_taps(cur, prev8):
    return [cur] + [_shift_down(cur, prev8, s) for s in range(1, DN_CONV_K)]


def _conv_tile(taps, w):
    out = w[3:4, :] * taps[0]
    for s in range(1, DN_CONV_K):
        out = out + w[3 - s:4 - s, :] * taps[s]
    return out


def _l2n(a, scale):
    return a * (lax.rsqrt(jnp.sum(a * a, axis=-1, keepdims=True) + NORM_EPS) * scale)


def _dn_post_tile(c, t):
    a = _silu(c)
    if t < DN_K_HEADS:
        return _l2n(a, DN_HD ** -0.5)
    if t < 2 * DN_K_HEADS:
        return _l2n(a, 1.0)
    return a


def _dn_beta_g(ba, a_log, dt_bias, live):
    beta = jax.nn.sigmoid(ba[:, 0:DN_V_HEADS]) * live
    g = -jnp.exp(a_log) * _softplus(ba[:, DN_V_HEADS:] + dt_bias) * live
    return beta, g


def _live_rows(i, rb):
    rows = i * rb + lax.broadcasted_iota(jnp.int32, (rb, 1), 0)
    return (rows >= FRONT_PAD).astype(F32)


def _halo_spec_args(x, rb):
    per = rb // 8
    return (x, 8, lambda i: (jnp.maximum(i * per - 1, 0), 0))


def _dn_conv_fwd(qkv, ba, conv_w, a_log, dt_bias):
    T = qkv.shape[0]

    def body(x_ref, ba_ref, halo_ref, cw_ref, al_ref, dtb_ref, q_ref, k_ref, v_ref, bg_ref):
        i = pl.program_id(0)
        first = (i > 0).astype(F32)
        for t in range(DN_CONV_W // 128):
            cols = slice(128 * t, 128 * (t + 1))
            c = _conv_tile(_conv_taps(x_ref[:, cols], halo_ref[:, cols] * first), cw_ref[:, cols])
            out = _dn_post_tile(c, t)
            if t < DN_K_HEADS:
                q_ref[:, cols] = out
            elif t < 2 * DN_K_HEADS:
                k_ref[:, 128 * (t - 8):128 * (t - 7)] = out
            else:
                v_ref[:, 128 * (t - 16):128 * (t - 15)] = out
        beta, g = _dn_beta_g(ba_ref[...], al_ref[...], dtb_ref[...], _live_rows(i, ROW_BLOCK))
        bg_ref[:, 0:DN_V_HEADS] = beta
        bg_ref[:, DN_V_HEADS:] = g

    return _row_call("dn_conv_fwd", body, T, ROW_BLOCK, [qkv, ba], [conv_w, a_log, dt_bias],
                     [(1024, F32), (1024, F32), (2048, F32), (32, F32)], halos=[_halo_spec_args(qkv, ROW_BLOCK)])


def _chunk_masks():
    r = lax.broadcasted_iota(jnp.int32, (DN_CHUNK, DN_CHUNK), 0)
    c = lax.broadcasted_iota(jnp.int32, (DN_CHUNK, DN_CHUNK), 1)
    return r >= c, r > c, r == c, r <= c


def _tri_inv_block(x):
    B = TRI_BLOCK
    n = range(len(x))
    r_, c_ = lax.broadcasted_iota(jnp.int32, (B, B), 0), lax.broadcasted_iota(jnp.int32, (B, B), 1)
    ainv = [jnp.where(r_ == c_, 1.0, 0.0) + x[h] for h in n]
    p = [_bdot(x[h], x[h]) for h in n]
    for _ in range(B.bit_length() - 3):
        r = [_bdot(jnp.concatenate([p[h], ainv[h]], axis=0), p[h]) for h in n]
        ainv = [ainv[h] + r[h][B:] for h in n]
        p = [r[h][:B] for h in n]
    return [ainv[h] + _bdot(ainv[h], p[h]) for h in n]


def _tri_inv(x):
    B = TRI_BLOCK
    assert DN_CHUNK == 2 * B
    n = len(x)
    diag = _tri_inv_block([x[h][:B, :B] for h in range(n)] + [x[h][B:, B:] for h in range(n)])
    a11, a22 = diag[:n], diag[n:]
    a21 = [_bdot(_bdot(a22[h], x[h][B:, :B]), a11[h]) for h in range(n)]
    zero = jnp.zeros((B, B), F32)
    return [jnp.concatenate([jnp.concatenate([a11[h], zero], axis=1), jnp.concatenate([a21[h], a22[h]], axis=1)], axis=0)
            for h in range(n)]


@jax.custom_vjp
def _tri_inv_known(x, a):
    return a


def _tri_inv_known_fwd(x, a):
    return a, a


def _tri_inv_known_bwd(a, da):
    return [_bdot(_bdot(a[h], da[h], TN), a[h], NT) for h in range(len(a))], [jnp.zeros_like(t) for t in a]


_tri_inv_known.defvjp(_tri_inv_known_fwd, _tri_inv_known_bwd)


def _dn_chunk_step(S, q, k, v, beta, g, masks, known_inv=None, with_inv=False):
    causal, strict, eye, upper = masks
    C, W = DN_CHUNK, DN_HD
    heads = range(len(v))
    k_t = [k[j].T for j in range(len(k))]
    qk_kk = [_bdot(jnp.concatenate([q[j], k[j]], axis=0), k_t[j]) for j in range(len(q))]
    g_b = [jnp.broadcast_to(g[h], (C, C)) for h in heads]
    beta_b = [jnp.broadcast_to(beta[h], (C, W)) for h in heads]
    g_row = [jnp.sum(jnp.where(eye, g_b[h], 0.0), axis=0, keepdims=True) for h in heads]
    gc_col = [jnp.sum(jnp.where(causal, g_row[h], 0.0), axis=1, keepdims=True) for h in heads]
    gc_row = [jnp.sum(jnp.where(upper, g_b[h], 0.0), axis=0, keepdims=True) for h in heads]
    g_last = [jnp.sum(g_row[h], axis=1, keepdims=True) for h in heads]
    gc_b = [jnp.broadcast_to(gc_col[h], (C, W)) for h in heads]
    decay = [jnp.exp(jnp.where(causal, gc_b[h][:, :C] - gc_row[h], NEG)) for h in heads]
    eg_b = [jnp.exp(gc_b[h]) for h in heads]
    x = [jnp.where(strict, qk_kk[h // 2][C:] * beta_b[h][:, :C] * decay[h], 0.0) * -1.0 for h in heads]
    ainv = _tri_inv(x) if known_inv is None else _tri_inv_known(x, known_inv)
    uw = [_bdot(ainv[h], jnp.concatenate([v[h] * beta_b[h], k[h // 2] * (beta_b[h] * eg_b[h])], axis=1)) for h in heads]
    ws_qs = [_bdot(jnp.concatenate([uw[h][:, W:], q[h // 2] * eg_b[h]], axis=0), S[h]) for h in heads]
    v_new = [uw[h][:, :W] - ws_qs[h][:C] for h in heads]
    o = [ws_qs[h][C:] + _bdot(qk_kk[h // 2][:C] * decay[h], v_new[h]) for h in heads]
    s_new = [S[h] * jnp.exp(g_last[h]) + _bdot(k_t[h // 2] * jnp.exp(g_last[h] - gc_row[h]), v_new[h]) for h in heads]
    return (s_new, o, ainv) if with_inv else (s_new, o)


def _dn_chunk_tiles(q_ref, k_ref, v_ref, bg_ref):
    q = [q_ref[:, 128 * j:128 * (j + 1)] for j in range(DN_K_HEADS)]
    k = [k_ref[:, 128 * j:128 * (j + 1)] for j in range(DN_K_HEADS)]
    v = [v_ref[:, 128 * h:128 * (h + 1)] for h in range(DN_V_HEADS)]
    beta = [bg_ref[:, h:h + 1] for h in range(DN_V_HEADS)]
    g = [bg_ref[:, DN_V_HEADS + h:DN_V_HEADS + h + 1] for h in range(DN_V_HEADS)]
    return q, k, v, beta, g


def _dn_scan_fwd(qn, kn, v, bg):
    T = qn.shape[0]
    nc = T // DN_CHUNK

    def body(q_ref, k_ref, v_ref, bg_ref, o_ref, ssave_ref, inv_ref, s_ref):
        @pl.when(pl.program_id(0) == 0)
        def _():
            s_ref[...] = jnp.zeros_like(s_ref)
        masks = _chunk_masks()
        q, k, v, beta, g = _dn_chunk_tiles(q_ref, k_ref, v_ref, bg_ref)
        for first in range(0, DN_V_HEADS, SCAN_FWD_GROUP):
            heads = range(first, first + SCAN_FWD_GROUP)
            pairs = slice(first // 2, (first + SCAN_FWD_GROUP) // 2)
            hs = slice(first, first + SCAN_FWD_GROUP)
            s_old = [s_ref[h] for h in heads]
            for i, h in enumerate(heads):
                ssave_ref[0, h] = s_old[i]
            s_new, o, ainv = _dn_chunk_step(s_old, q[pairs], k[pairs], v[hs], beta[hs], g[hs], masks, with_inv=True)
            for i, h in enumerate(heads):
                o_ref[:, 128 * h:128 * (h + 1)] = o[i]
                inv_ref[0, h] = ainv[i].astype(BF16)
                s_ref[h] = s_new[i]

    return pl.pallas_call(
        body, grid=(nc,),
        in_specs=[pl.BlockSpec((DN_CHUNK, 1024), lambda i: (i, 0)),
                  pl.BlockSpec((DN_CHUNK, 1024), lambda i: (i, 0)),
                  pl.BlockSpec((DN_CHUNK, 2048), lambda i: (i, 0)),
                  pl.BlockSpec((DN_CHUNK, 32), lambda i: (i, 0))],
        out_specs=[pl.BlockSpec((DN_CHUNK, 2048), lambda i: (i, 0)),
                   pl.BlockSpec((1, DN_V_HEADS, DN_HD, DN_HD), lambda i: (i, 0, 0, 0)),
                   pl.BlockSpec((1, DN_V_HEADS, DN_CHUNK, DN_CHUNK), lambda i: (i, 0, 0, 0))],
        out_shape=[jax.ShapeDtypeStruct((T, 2048), F32),
                   jax.ShapeDtypeStruct((nc, DN_V_HEADS, DN_HD, DN_HD), F32),
                   jax.ShapeDtypeStruct((nc, DN_V_HEADS, DN_CHUNK, DN_CHUNK), BF16)],
        scratch_shapes=[pltpu.VMEM((DN_V_HEADS, DN_HD, DN_HD), F32)],
        name="dn_scan_fwd", compiler_params=_cparams(),
    )(qn, kn, v, bg)


def _dn_gate_tile(o, z, onw):
    return _rms(o, onw) * _silu(z)


def _dn_out_fwd(o, z, h1, target, w_out, onw):
    T = o.shape[0]

    def body(o_ref, z_ref, h_ref, ta_ref, tb_ref, tc_ref, w_ref, onw_ref, dy_ref, og_ref, loss_ref):
        i = pl.program_id(0)

        @pl.when(i == 0)
        def _():
            loss_ref[...] = jnp.zeros_like(loss_ref)
        for h in range(DN_V_HEADS):
            cols = slice(128 * h, 128 * (h + 1))
            og_ref[:, cols] = _dn_gate_tile(o_ref[:, cols], z_ref[:, cols], onw_ref[...]).astype(BF16)
        y = h_ref[...] + jnp.dot(og_ref[...], w_ref[...], preferred_element_type=F32)
        rows = i * ROW_BLOCK + lax.broadcasted_iota(jnp.int32, (ROW_BLOCK, 1), 0)
        diff = jnp.where(rows >= FRONT_PAD + N_META, y - _padded_block(i, None, (ta_ref, tb_ref, tc_ref)), 0.0)
        dy_ref[...] = diff * (1.0 / D_MODEL)
        loss_ref[...] += jnp.sum(diff * diff) * (0.5 / D_MODEL)

    return _row_call("dn_out_fwd", body, T, ROW_BLOCK, [o, z, h1], [w_out, onw],
                     [(1024, F32), (2048, BF16)], [((1, 128), F32)], halos=_token_views(target))


def _dn_out_bwd(dy, o, z, w_out, onw):
    T = o.shape[0]

    def body(dy_ref, o_ref, z_ref, w_ref, onw_ref, do_ref, dz_ref, donw_ref, dog_ref):
        @pl.when(pl.program_id(0) == 0)
        def _():
            donw_ref[...] = jnp.zeros_like(donw_ref)
        dy = dy_ref[...].astype(BF16)
        donw = jnp.zeros((1, DN_HD), F32)
        for half in range(2):
            hcols = slice(1024 * half, 1024 * (half + 1))
            dog_ref[:, hcols] = lax.dot_general(dy, w_ref[hcols, :], NT, preferred_element_type=F32)
        for h in range(DN_V_HEADS):
            cols = slice(128 * h, 128 * (h + 1))
            _, vjp = jax.vjp(_dn_gate_tile, o_ref[:, cols], z_ref[:, cols], onw_ref[...])
            do, dz, dn = vjp(dog_ref[:, cols])
            do_ref[:, cols] = do
            dz_ref[:, cols] = dz
            donw = donw + dn
        donw_ref[...] += donw

    return _row_call("dn_out_bwd", body, T, ROW_BLOCK, [dy, o, z], [w_out, onw],
                     [(2048, F32), (2048, F32)], [((1, DN_HD), F32)], scratch=[pltpu.VMEM((ROW_BLOCK, 2048), F32)])


def _dn_scan_bwd(do, qn, kn, v, bg, ssave, inv):
    T = qn.shape[0]
    nc = T // DN_CHUNK
    rev = lambda i: nc - 1 - i

    def body(do_ref, q_ref, k_ref, v_ref, bg_ref, ss_ref, inv_ref, dq_ref, dk_ref, dv_ref, dbg_ref, ds_ref):
        @pl.when(pl.program_id(0) == 0)
        def _():
            ds_ref[...] = jnp.zeros_like(ds_ref)
        lane32 = lax.broadcasted_iota(jnp.int32, (1, 2 * DN_V_HEADS), 1)
        masks = _chunk_masks()
        q, k, v, beta, g = _dn_chunk_tiles(q_ref, k_ref, v_ref, bg_ref)
        dbg = jnp.zeros((DN_CHUNK, 2 * DN_V_HEADS), F32)
        for first in range(0, DN_V_HEADS, SCAN_BWD_GROUP):
            heads = range(first, first + SCAN_BWD_GROUP)
            pairs = slice(first // 2, (first + SCAN_BWD_GROUP) // 2)
            hs = slice(first, first + SCAN_BWD_GROUP)
            fn = functools.partial(_dn_chunk_step, masks=masks, known_inv=[inv_ref[0, h].astype(F32) for h in heads])
            _, vjp = jax.vjp(fn, [ss_ref[0, h] for h in heads], q[pairs], k[pairs], v[hs], beta[hs], g[hs])
            ds, dq, dk, dv, dbeta, dg = vjp(([ds_ref[h] for h in heads], [do_ref[:, 128 * h:128 * (h + 1)] for h in heads]))
            for i, h in enumerate(heads):
                ds_ref[h] = ds[i]
                dv_ref[:, 128 * h:128 * (h + 1)] = dv[i]
                dbg = dbg + jnp.where(lane32 == h, dbeta[i], 0.0) + jnp.where(lane32 == DN_V_HEADS + h, dg[i], 0.0)
            for i, j in enumerate(range(first // 2, (first + SCAN_BWD_GROUP) // 2)):
                dq_ref[:, 128 * j:128 * (j + 1)] = dq[i]
                dk_ref[:, 128 * j:128 * (j + 1)] = dk[i]
        dbg_ref[...] = dbg

    return pl.pallas_call(
        body, grid=(nc,),
        in_specs=[pl.BlockSpec((DN_CHUNK, 2048), lambda i: (rev(i), 0)),
                  pl.BlockSpec((DN_CHUNK, 1024), lambda i: (rev(i), 0)),
                  pl.BlockSpec((DN_CHUNK, 1024), lambda i: (rev(i), 0)),
                  pl.BlockSpec((DN_CHUNK, 2048), lambda i: (rev(i), 0)),
                  pl.BlockSpec((DN_CHUNK, 32), lambda i: (rev(i), 0)),
                  pl.BlockSpec((1, DN_V_HEADS, DN_HD, DN_HD), lambda i: (rev(i), 0, 0, 0)),
                  pl.BlockSpec((1, DN_V_HEADS, DN_CHUNK, DN_CHUNK), lambda i: (rev(i), 0, 0, 0))],
        out_specs=[pl.BlockSpec((DN_CHUNK, 1024), lambda i: (rev(i), 0)),
                   pl.BlockSpec((DN_CHUNK, 1024), lambda i: (rev(i), 0)),
                   pl.BlockSpec((DN_CHUNK, 2048), lambda i: (rev(i), 0)),
                   pl.BlockSpec((DN_CHUNK, 32), lambda i: (rev(i), 0))],
        out_shape=[jax.ShapeDtypeStruct((T, 1024), F32), jax.ShapeDtypeStruct((T, 1024), F32),
                   jax.ShapeDtypeStruct((T, 2048), F32), jax.ShapeDtypeStruct((T, 32), F32)],
        scratch_shapes=[pltpu.VMEM((DN_V_HEADS, DN_HD, DN_HD), F32)],
        name="dn_scan_bwd", compiler_params=_cparams(),
    )(do, qn, kn, v, bg, ssave, inv)


def _dn_conv_bwd(dqn, dkn, dv, dbg, qkv, ba, conv_w, a_log, dt_bias):
    T = qkv.shape[0]
    nr = T // ROW_BLOCK

    def body(dq_ref, dk_ref, dv_ref, dbg_ref, x_ref, ba_ref, halo_ref, cw_ref, al_ref, dtb_ref,
             dx_ref, dba_ref, dcw_ref, dal_ref, ddtb_ref, carry_ref):
        step = pl.program_id(0)
        i = nr - 1 - step

        @pl.when(step == 0)
        def _():
            carry_ref[...] = jnp.zeros_like(carry_ref)
            dcw_ref[...] = jnp.zeros_like(dcw_ref)
            dal_ref[...] = jnp.zeros_like(dal_ref)
            ddtb_ref[...] = jnp.zeros_like(ddtb_ref)
        first = (i > 0).astype(F32)
        for t in range(DN_CONV_W // 128):
            cols = slice(128 * t, 128 * (t + 1))
            w = cw_ref[:, cols]
            taps = _conv_taps(x_ref[:, cols], halo_ref[:, cols] * first)
            c = _conv_tile(taps, w)
            if t < DN_K_HEADS:
                dout = dq_ref[:, cols]
            elif t < 2 * DN_K_HEADS:
                dout = dk_ref[:, 128 * (t - 8):128 * (t - 7)]
            else:
                dout = dv_ref[:, 128 * (t - 16):128 * (t - 15)]
            _, vjp = jax.vjp(functools.partial(_dn_post_tile, t=t), c)
            (dc,) = vjp(dout)
            nxt = carry_ref[:, cols]
            dx = w[3:4, :] * dc
            dcw_ref[3:4, cols] += jnp.sum(dc * taps[0], axis=0, keepdims=True)
            for s in range(1, DN_CONV_K):
                dx = dx + w[3 - s:4 - s, :] * _shift_up(dc, nxt, s)
                dcw_ref[3 - s:4 - s, cols] += jnp.sum(dc * taps[s], axis=0, keepdims=True)
            dx_ref[:, cols] = dx
            carry_ref[:, cols] = dc[0:8, :]
        fn = functools.partial(_dn_beta_g, live=_live_rows(i, ROW_BLOCK))
        _, vjp = jax.vjp(fn, ba_ref[...], al_ref[...], dtb_ref[...])
        dba, dal, ddtb = vjp((dbg_ref[:, 0:DN_V_HEADS], dbg_ref[:, DN_V_HEADS:]))
        dba_ref[...] = dba
        dal_ref[...] += dal
        ddtb_ref[...] += ddtb

    per = ROW_BLOCK // 8
    halo = (qkv, 8, lambda s: (jnp.maximum((nr - 1 - s) * per - 1, 0), 0))
    return _row_call("dn_conv_bwd", body, T, ROW_BLOCK, [dqn, dkn, dv, dbg, qkv, ba], [conv_w, a_log, dt_bias],
                     [(4096, F32), (32, F32)], [((DN_CONV_K, 4096), F32), ((1, DN_V_HEADS), F32), ((1, DN_V_HEADS), F32)],
                     reverse=True, scratch=[pltpu.VMEM((8, 4096), F32)], halos=[halo])


def _dn_in_bwd(dqkv, dz, dba, h1, dy, norm_w, w_in):
    T = h1.shape[0]

    def body(dqkv_ref, dz_ref, dba_ref, h_ref, dy_ref, nw_ref, w_ref, dh_ref, dnw_ref):
        @pl.when(pl.program_id(0) == 0)
        def _():
            dnw_ref[...] = jnp.zeros_like(dnw_ref)
        dxn = (_bdot(dqkv_ref[...], w_ref[:, 0:4096], NT) + _bdot(dz_ref[...], w_ref[:, 4096:6144], NT)
               + _bdot(dba_ref[...], w_ref[:, 6144:6176], NT))
        _, vjp = jax.vjp(_rms, h_ref[...], nw_ref[...])
        dh, dnw = vjp(dxn)
        dh_ref[...] = (dy_ref[...] + dh) * _live_rows(pl.program_id(0), ROW_BLOCK)
        dnw_ref[...] += dnw

    return _row_call("dn_in_bwd", body, T, ROW_BLOCK, [dqkv, dz, dba, h1, dy], [norm_w, w_in],
                     [(1024, F32)], [((1, 1024), F32)])


def _exchange(parts, scatter, name):
    n = len(parts)
    out_shape = [jax.ShapeDtypeStruct(p.shape if sc else (N_DEV,) + p.shape, p.dtype) for p, sc in zip(parts, scatter)]

    def body(*refs):
        ins, outs = refs[:n], refs[n:2 * n]
        send_sems, recv_sems, local_sems = refs[2 * n:]
        x, y, c = lax.axis_index("x"), lax.axis_index("y"), lax.axis_index("c")
        me = 4 * x + 2 * y + c
        peers = []
        for k in range(1, N_DEV):
            px = 1 - x if k & 4 else x
            py = 1 - y if k & 2 else y
            pc = 1 - c if k & 1 else c
            peers.append(((px, py, pc), 4 * px + 2 * py + pc))

        def src(a, idx):
            return ins[a].at[idx] if scatter[a] else ins[a]

        local = [pltpu.make_async_copy(src(a, me), outs[a].at[me], local_sems.at[a]) for a in range(n)]
        for cp in local:
            cp.start()
        for a in range(n):
            for k, (dev, idx) in enumerate(peers):
                pltpu.make_async_remote_copy(
                    src_ref=src(a, idx), dst_ref=outs[a].at[me], send_sem=send_sems.at[a, k], recv_sem=recv_sems.at[a, k],
                    device_id=dev, device_id_type=pl.DeviceIdType.MESH).start()
        for a in range(n):
            for k, (dev, idx) in enumerate(peers):
                pltpu.make_async_remote_copy(
                    src_ref=src(a, idx), dst_ref=outs[a].at[idx], send_sem=send_sems.at[a, k], recv_sem=recv_sems.at[a, k],
                    device_id=dev, device_id_type=pl.DeviceIdType.MESH).wait()
        for cp in local:
            cp.wait()

    hbm = pl.BlockSpec(memory_space=pltpu.HBM)
    return pl.pallas_call(
        body, out_shape=out_shape, in_specs=[hbm] * n, out_specs=[hbm] * n,
        scratch_shapes=[pltpu.SemaphoreType.DMA((n, N_DEV - 1)), pltpu.SemaphoreType.DMA((n, N_DEV - 1)),
                        pltpu.SemaphoreType.DMA((n,))],
        name=name,
    )(*parts)


def _gather_two_level(parts, name):
    n = len(parts)
    out_shape = [jax.ShapeDtypeStruct((N_DEV,) + p.shape, p.dtype) for p in parts]

    def body(*refs):
        ins, outs = refs[:n], refs[n:2 * n]
        send_sems, recv_sems, local_sems = refs[2 * n:]
        x, y, c = lax.axis_index("x"), lax.axis_index("y"), lax.axis_index("c")
        idx = lambda px, py, pc: 4 * px + 2 * py + pc
        me, sibling = (x, y, c), (x, y, 1 - c)
        chips = [(1 - x, y), (x, 1 - y), (1 - x, 1 - y)]

        def copy(a, k, block, to, src=None):
            slot = outs[a].at[idx(*block)]
            return pltpu.make_async_remote_copy(
                src_ref=slot if src is None else src, dst_ref=slot, send_sem=send_sems.at[a, k], recv_sem=recv_sems.at[a, k],
                device_id=to, device_id_type=pl.DeviceIdType.MESH)

        local = [pltpu.make_async_copy(ins[a], outs[a].at[idx(*me)], local_sems.at[a]) for a in range(n)]
        for cp in local:
            cp.start()
        sent = []
        for a in range(n):
            sent.append(copy(a, 0, me, sibling, src=ins[a]))
            sent += [copy(a, 1 + j, me, (*chip, c), src=ins[a]) for j, chip in enumerate(chips)]
        for cp in sent:
            cp.start()
        for a in range(n):
            for j, chip in enumerate(chips):
                copy(a, 1 + j, (*chip, c), me).wait_recv()
                passed = copy(a, 4 + j, (*chip, c), sibling)
                passed.start()
                sent.append(passed)
        for a in range(n):
            copy(a, 0, sibling, me).wait_recv()
            for j, chip in enumerate(chips):
                copy(a, 4 + j, (*chip, 1 - c), me).wait_recv()
        for cp in sent:
            cp.wait_send()
        for cp in local:
            cp.wait()

    hbm = pl.BlockSpec(memory_space=pltpu.HBM)
    return pl.pallas_call(
        body, out_shape=out_shape, in_specs=[hbm] * n, out_specs=[hbm] * n,
        scratch_shapes=[pltpu.SemaphoreType.DMA((n, N_DEV - 1)), pltpu.SemaphoreType.DMA((n, N_DEV - 1)),
                        pltpu.SemaphoreType.DMA((n,))],
        name=name,
    )(*parts)


def _swap_with_sibling(parts, name):
    n = len(parts)

    def body(*refs):
        ins, outs = refs[:n], refs[n:2 * n]
        send_sems, recv_sems = refs[2 * n:]
        x, y, c = lax.axis_index("x"), lax.axis_index("y"), lax.axis_index("c")
        copies = [pltpu.make_async_remote_copy(
            src_ref=ins[a].at[1 - c], dst_ref=outs[a], send_sem=send_sems.at[a], recv_sem=recv_sems.at[a],
            device_id=(x, y, 1 - c), device_id_type=pl.DeviceIdType.MESH) for a in range(n)]
        for cp in copies:
            cp.start()
        for cp in copies:
            cp.wait()

    hbm = pl.BlockSpec(memory_space=pltpu.HBM)
    return pl.pallas_call(
        body, out_shape=[jax.ShapeDtypeStruct(p.shape[1:], p.dtype) for p in parts], in_specs=[hbm] * n, out_specs=[hbm] * n,
        scratch_shapes=[pltpu.SemaphoreType.DMA((n,)), pltpu.SemaphoreType.DMA((n,))],
        name=name,
    )(*parts)


def _pair_sum(a, b, name):
    R, C = a.shape
    rb = _adam_rows(R)

    def body(a_ref, b_ref, o_ref):
        o_ref[...] = (a_ref[...].astype(F32) + b_ref[...].astype(F32)).astype(BF16)

    blk = pl.BlockSpec((rb, C), lambda i: (i, 0))
    return pl.pallas_call(body, grid=(R // rb,), in_specs=[blk, blk], out_specs=blk,
                          out_shape=jax.ShapeDtypeStruct((R, C), BF16), name=name, compiler_params=_cparams())(a, b)


def _exchange_chips(parts, name):
    n = len(parts)
    n_chips = N_DEV // 2

    def body(*refs):
        ins, outs = refs[:n], refs[n:2 * n]
        send_sems, recv_sems, local_sems = refs[2 * n:]
        x, y, c = lax.axis_index("x"), lax.axis_index("y"), lax.axis_index("c")
        mine = 2 * x + y
        chips = [(1 - x, y), (x, 1 - y), (1 - x, 1 - y)]
        local = [pltpu.make_async_copy(ins[a].at[mine], outs[a].at[mine], local_sems.at[a]) for a in range(n)]
        for cp in local:
            cp.start()
        for a in range(n):
            for k, (px, py) in enumerate(chips):
                pltpu.make_async_remote_copy(
                    src_ref=ins[a].at[2 * px + py], dst_ref=outs[a].at[mine], send_sem=send_sems.at[a, k],
                    recv_sem=recv_sems.at[a, k], device_id=(px, py, c), device_id_type=pl.DeviceIdType.MESH).start()
        for a in range(n):
            for k, (px, py) in enumerate(chips):
                pltpu.make_async_remote_copy(
                    src_ref=ins[a].at[2 * px + py], dst_ref=outs[a].at[2 * px + py], send_sem=send_sems.at[a, k],
                    recv_sem=recv_sems.at[a, k], device_id=(px, py, c), device_id_type=pl.DeviceIdType.MESH).wait()
        for cp in local:
            cp.wait()

    hbm = pl.BlockSpec(memory_space=pltpu.HBM)
    return pl.pallas_call(
        body, out_shape=[jax.ShapeDtypeStruct(p.shape, p.dtype) for p in parts], in_specs=[hbm] * n, out_specs=[hbm] * n,
        scratch_shapes=[pltpu.SemaphoreType.DMA((n, n_chips - 1)), pltpu.SemaphoreType.DMA((n, n_chips - 1)),
                        pltpu.SemaphoreType.DMA((n,))],
        name=name,
    )(*parts)


def _adam_rows(rows):
    for rb in (128, 64, 40, 16, 8):
        if rows % rb == 0:
            return rb
    return rows


def _adamw(stack, w, m, v, name):
    R, C = w.shape
    rb = _adam_rows(R)
    slots = stack.shape[0]

    def body(s_ref, w_ref, m_ref, v_ref, g_ref, d_ref, nm_ref, nv_ref):
        g = s_ref[0].astype(F32)
        for s in range(1, slots):
            g = g + s_ref[s].astype(F32)
        nm = ADAM_B1 * m_ref[...] + (1.0 - ADAM_B1) * g
        nv = ADAM_B2 * v_ref[...] + (1.0 - ADAM_B2) * (g * g)
        m_hat = nm / (1.0 - ADAM_B1 ** ADAM_STEP)
        v_hat = nv / (1.0 - ADAM_B2 ** ADAM_STEP)
        g_ref[...] = g
        d_ref[...] = -ADAM_LR * (m_hat / (jnp.sqrt(v_hat) + ADAM_EPS) + ADAM_WD * w_ref[...])
        nm_ref[...] = nm
        nv_ref[...] = nv

    blk = pl.BlockSpec((rb, C), lambda i: (i, 0))
    return pl.pallas_call(
        body, grid=(R // rb,),
        in_specs=[pl.BlockSpec((slots, rb, C), lambda i: (0, i, 0)), blk, blk, blk],
        out_specs=[blk] * 4, out_shape=[jax.ShapeDtypeStruct((R, C), F32)] * 4,
        name=name, compiler_params=_cparams(),
    )(stack, w, m, v)


def _pad_rows8(a):
    return jnp.concatenate([a, jnp.zeros((8 - a.shape[0], a.shape[1]), a.dtype)], axis=0) if a.shape[0] < 8 else a


def _pack_small(norm_w, qnw, knw, sinks, a_log, dt_bias, onw, extra):
    z = lambda n: jnp.zeros((1, n), F32)
    row = jnp.concatenate([norm_w, qnw, knw, sinks, a_log, dt_bias, z(80), onw, extra, z(512)], axis=1)
    return row.reshape(16, 128)


def _unpack_small(p):
    row = p.reshape(1, 2048)
    cut = lambda a, n: row[:, a:a + n]
    return (cut(0, 1024), cut(1024, 64), cut(1088, 64), cut(1152, 16), cut(1168, 16), cut(1184, 16), cut(1280, 128),
            cut(1408, 128))


def _pack_rows(w_in_a, w_in_d, w_out_a, w_out_d, meta, conv, dn_norm):
    a = jnp.concatenate([w_in_a, w_in_d], axis=1)
    b = jnp.concatenate([w_out_a, w_out_d], axis=0)
    c = jnp.concatenate([meta, conv.reshape(16, 128), _pad_rows8(dn_norm)], axis=0)
    return a, b, c


def _unpack_rows(a, b, c):
    return (a[:, :288], a[:, 288:], b[:128], b[128:], c[:16], c[16:32].reshape(4, 512), c[32:33])


def _local_step(x, front, target, w):
    xn0, q, kv, gate = _attn_in_fwd(x, front, w["attn_norm_w"], w["attn_w_in"])
    o = _attn_core_fwd(q, kv, w["attn_sinks"], w["attn_q_norm_w"], w["attn_k_norm_w"])
    h1 = _attn_out_fwd(o, gate, x, front, w["attn_w_out"])
    xn1, qkv, z, ba = _dn_in_fwd(h1, w["dn_norm_w"], w["dn_w_in"])
    qn, kn, v, bg = _dn_conv_fwd(qkv, ba, w["dn_conv_w"], w["dn_a_log"], w["dn_dt_bias"])
    o_dn, ssave, inv = _dn_scan_fwd(qn, kn, v, bg)
    dy, og_dn, loss = _dn_out_fwd(o_dn, z, h1, target, w["dn_w_out"], w["dn_o_norm_w"])

    g = {}
    do_dn, dz, g["dn_o_norm_w"] = _dn_out_bwd(dy, o_dn, z, w["dn_w_out"], w["dn_o_norm_w"])
    g["dn_w_out"] = _wgrad(og_dn, dy, 1024, "wgrad_dn_out")
    dqn, dkn, dv, dbg = _dn_scan_bwd(do_dn, qn, kn, v, bg, ssave, inv)
    dqkv, dba, g["dn_conv_w"], g["dn_a_log"], g["dn_dt_bias"] = _dn_conv_bwd(
        dqn, dkn, dv, dbg, qkv, ba, w["dn_conv_w"], w["dn_a_log"], w["dn_dt_bias"])
    dh1, g["dn_norm_w"] = _dn_in_bwd(dqkv, dz, dba, h1, dy, w["dn_norm_w"], w["dn_w_in"])
    g["dn_w_in"] = jnp.concatenate([_wgrad(xn1, dqkv, 1024, "wgrad_dn_qkv"), _wgrad(xn1, dz, 1024, "wgrad_dn_z"),
                                    _wgrad(xn1, dba, 32, "wgrad_dn_ba")], axis=1)
    do, dgate, g["attn_w_out"] = _attn_out_bwd(dh1, o, gate, w["attn_w_out"])
    dq, dkv, g["attn_sinks"], g["attn_q_norm_w"], g["attn_k_norm_w"] = _attn_core_bwd(
        do, q, kv, w["attn_sinks"], w["attn_q_norm_w"], w["attn_k_norm_w"])
    dh0, g["attn_norm_w"] = _attn_in_bwd(dq, dkv, dgate, x, front, dh1, w["attn_norm_w"], w["attn_w_in"])
    g["attn_w_in"] = jnp.concatenate([_wgrad(xn0, dq, 1024, "wgrad_attn_q"), _wgrad(xn0, dkv, 256, "wgrad_attn_kv"),
                                      _wgrad(xn0, dgate, 1024, "wgrad_attn_gate")], axis=1)
    return loss, dh0, g


WEIGHTS = ['meta_tokens', 'attn_norm_w', 'attn_w_in', 'attn_q_norm_w', 'attn_k_norm_w', 'attn_sinks', 'attn_w_out',
           'dn_norm_w', 'dn_w_in', 'dn_conv_w', 'dn_a_log', 'dn_dt_bias', 'dn_o_norm_w', 'dn_w_out']
SMALL = ['attn_norm_w', 'attn_q_norm_w', 'attn_k_norm_w', 'attn_sinks', 'dn_a_log', 'dn_dt_bias', 'dn_o_norm_w']


def kernel(x, meta_tokens, attn_norm_w, attn_w_in, attn_q_norm_w, attn_k_norm_w, attn_sinks, attn_w_out, dn_norm_w, dn_w_in, dn_conv_w, dn_a_log, dn_dt_bias, dn_o_norm_w, dn_w_out, loss_target, m_meta_tokens, m_attn_norm_w, m_attn_w_in, m_attn_q_norm_w, m_attn_k_norm_w, m_attn_sinks, m_attn_w_out, m_dn_norm_w, m_dn_w_in, m_dn_conv_w, m_dn_a_log, m_dn_dt_bias, m_dn_o_norm_w, m_dn_w_out, v_meta_tokens, v_attn_norm_w, v_attn_w_in, v_attn_q_norm_w, v_attn_k_norm_w, v_attn_sinks, v_attn_w_out, v_dn_norm_w, v_dn_w_in, v_dn_conv_w, v_dn_a_log, v_dn_dt_bias, v_dn_o_norm_w, v_dn_w_out):
    shard = dict(meta_tokens=meta_tokens, attn_norm_w=attn_norm_w, attn_w_in=attn_w_in[0], attn_q_norm_w=attn_q_norm_w,
                 attn_k_norm_w=attn_k_norm_w, attn_sinks=attn_sinks, attn_w_out=attn_w_out[0], dn_norm_w=dn_norm_w,
                 dn_w_in=dn_w_in[0], dn_conv_w=dn_conv_w[0], dn_a_log=dn_a_log, dn_dt_bias=dn_dt_bias,
                 dn_o_norm_w=dn_o_norm_w, dn_w_out=dn_w_out[0])
    mom_m = dict(meta_tokens=m_meta_tokens, attn_norm_w=m_attn_norm_w, attn_w_in=m_attn_w_in[0], attn_q_norm_w=m_attn_q_norm_w,
                 attn_k_norm_w=m_attn_k_norm_w, attn_sinks=m_attn_sinks, attn_w_out=m_attn_w_out[0], dn_norm_w=m_dn_norm_w,
                 dn_w_in=m_dn_w_in[0], dn_conv_w=m_dn_conv_w[0], dn_a_log=m_dn_a_log, dn_dt_bias=m_dn_dt_bias,
                 dn_o_norm_w=m_dn_o_norm_w, dn_w_out=m_dn_w_out[0])
    mom_v = dict(meta_tokens=v_meta_tokens, attn_norm_w=v_attn_norm_w, attn_w_in=v_attn_w_in[0], attn_q_norm_w=v_attn_q_norm_w,
                 attn_k_norm_w=v_attn_k_norm_w, attn_sinks=v_attn_sinks, attn_w_out=v_attn_w_out[0], dn_norm_w=v_dn_norm_w,
                 dn_w_in=v_dn_w_in[0], dn_conv_w=v_dn_conv_w[0], dn_a_log=v_dn_a_log, dn_dt_bias=v_dn_dt_bias,
                 dn_o_norm_w=v_dn_o_norm_w, dn_w_out=v_dn_w_out[0])

    def rows_of(d):
        return _pack_rows(d["attn_w_in"], d["dn_w_in"], d["attn_w_out"], d["dn_w_out"], d["meta_tokens"], d["dn_conv_w"],
                          d["dn_norm_w"])

    def small_of(d, extra):
        return _pack_small(*[d[k] for k in SMALL], extra)

    wa, wb, wc = rows_of(shard)
    ga, gb, gc = _gather_two_level([wa.astype(BF16), wb.astype(BF16), wc], "gather_weights")
    full = {k: shard[k] for k in SMALL}
    full["attn_w_in"] = ga[:, :, :288].transpose(1, 0, 2).reshape(1024, 2304)
    full["dn_w_in"] = ga[:, :, 288:].transpose(1, 0, 2).reshape(1024, 6176)
    full["attn_w_out"] = gb[:, :128].reshape(1024, 1024)
    full["dn_w_out"] = gb[:, 128:].reshape(2048, 1024)
    meta_full = gc[:, :16].transpose(1, 0, 2).reshape(N_META, 1024)
    full["dn_conv_w"] = gc[:, 16:32].reshape(N_DEV, 4, 512).transpose(1, 0, 2).reshape(4, 4096)
    full["dn_norm_w"] = gc[:, 32].reshape(1, 1024)

    seq = x.shape[1]
    front = jnp.concatenate([jnp.zeros((FRONT_PAD, D_MODEL), F32), meta_full], axis=0)
    loss, dh0, g = _local_step(x[0], front, loss_target[0], full)
    grad_x = dh0[ATTN_BLOCK:ATTN_BLOCK + seq][None]
    g["meta_tokens"] = dh0[FRONT_PAD:ATTN_BLOCK]

    pa = jnp.concatenate([g["attn_w_in"].reshape(1024, N_DEV, 288), g["dn_w_in"].reshape(1024, N_DEV, 772)],
                         axis=2).transpose(1, 0, 2)
    pb = jnp.concatenate([g["attn_w_out"].reshape(N_DEV, 128, 1024), g["dn_w_out"].reshape(N_DEV, 256, 1024)], axis=1)
    dn_norm8 = jnp.concatenate([g["dn_norm_w"].reshape(N_DEV, 1, 128), jnp.zeros((N_DEV, 7, 128), F32)], axis=1)
    pc = jnp.concatenate([g["meta_tokens"].reshape(N_META, N_DEV, 128).transpose(1, 0, 2),
                          g["dn_conv_w"].reshape(4, N_DEV, 512).transpose(1, 0, 2).reshape(N_DEV, 16, 128), dn_norm8], axis=1)
    ps = small_of(g, loss)
    c = lax.axis_index("c")
    by_core = lambda p: p.astype(BF16).reshape((N_DEV // 2, 2) + p.shape[1:]).swapaxes(0, 1)
    pa2, pb2 = by_core(pa), by_core(pb)
    ra, rb_ = _swap_with_sibling([pa2, pb2], "swap_grads")
    own = lambda p2: lax.dynamic_index_in_dim(p2, c, axis=0, keepdims=False)
    flat = lambda t: t.reshape((-1,) + t.shape[2:])
    sa = _pair_sum(flat(own(pa2)), flat(ra), "pair_sum_a").reshape(ra.shape)
    sb = _pair_sum(flat(own(pb2)), flat(rb_), "pair_sum_b").reshape(rb_.shape)
    xa, xb = _exchange_chips([sa, sb], "exchange_grads")
    xc, xs = _exchange([pc, ps], [True, False], "exchange_small")

    out = {}
    ma, mb, mc = rows_of(mom_m)
    va, vb, vc = rows_of(mom_v)
    ra = _adamw(xa, wa, ma, va, "adamw_a")
    rb = _adamw(xb, wb, mb, vb, "adamw_b")
    rc = _adamw(xc, wc, mc, vc, "adamw_c")
    zero = jnp.zeros((1, 128), F32)
    rs = _adamw(xs, small_of(shard, zero), small_of(mom_m, zero), small_of(mom_v, zero), "adamw_small")
    row_names = ["attn_w_in", "dn_w_in", "attn_w_out", "dn_w_out", "meta_tokens", "dn_conv_w", "dn_norm_w"]
    lead = {"attn_w_in", "dn_w_in", "attn_w_out", "dn_w_out", "dn_conv_w"}
    for kind in range(4):
        vals = dict(zip(row_names, _unpack_rows(ra[kind], rb[kind], rc[kind])))
        small = _unpack_small(rs[kind])
        vals.update(dict(zip(SMALL, small[:7])))
        if kind == 0:
            loss_total = small[7][0, 0]
        out[kind] = [vals[k][None] if k in lead else vals[k] for k in WEIGHTS]
    return (loss_total, grad_x, *out[0], *out[1], *out[2], *out[3])
```

```python
import functools
import math

import jax
import jax.numpy as jnp
from jax import lax
from jax.experimental import pallas as pl
from jax.experimental.pallas import tpu as pltpu

F32, BF16 = jnp.float32, jnp.bfloat16

D_MODEL = 1024
N_META = 16
NORM_EPS = 1e-6
ATTN_HEADS, ATTN_KV_HEADS, ATTN_GROUPS, ATTN_HD = 16, 2, 8, 64
ATTN_BLOCK = 128
FRONT_PAD = ATTN_BLOCK - N_META
DN_HD, DN_K_HEADS, DN_V_HEADS = 128, 8, 16
DN_CHUNK = 128
TRI_BLOCK = 64
SCAN_FWD_GROUP = 8
SCAN_BWD_GROUP = 8
DN_KEY_W, DN_VAL_W = 1024, 2048
DN_CONV_W = 2 * DN_KEY_W + DN_VAL_W
DN_CONV_K = 4
N_DEV = 8
ROW_BLOCK = 384
WGRAD_ROWS = 1376
VMEM_LIMIT = 56 * 1024 * 1024
NEG = -1e30

ADAM_LR, ADAM_B1, ADAM_B2, ADAM_EPS, ADAM_WD, ADAM_STEP = 0.001, 0.9, 0.999, 1e-08, 0.01, 10

NT = (((1,), (1,)), ((), ()))
TN = (((0,), (0,)), ((), ()))


def _cparams(sem=("arbitrary",)):
    return pltpu.CompilerParams(dimension_semantics=sem, vmem_limit_bytes=VMEM_LIMIT)


def _rms(x, w):
    return x * lax.rsqrt(jnp.mean(x * x, axis=-1, keepdims=True) + NORM_EPS) * w


def _silu(x):
    return x * jax.nn.sigmoid(x)


def _softplus(x):
    return jnp.maximum(x, 0.0) + jnp.log(1.0 + jnp.exp(-jnp.abs(x)))


NN = (((1,), (0,)), ((), ()))


def _mm(a, b, dims):
    return lax.dot_general(a.astype(BF16), b.astype(BF16), dims, preferred_element_type=F32)


@functools.partial(jax.custom_vjp, nondiff_argnums=(2,))
def _bdot_vjp(a, b, dims):
    return _mm(a, b, dims)


def _bdot_fwd(a, b, dims):
    a16, b16 = a.astype(BF16), b.astype(BF16)
    return _mm(a16, b16, dims), (a16, b16, jnp.zeros((), a.dtype), jnp.zeros((), b.dtype))


def _bdot_bwd(dims, res, g):
    a16, b16, ta, tb = res
    g16 = g.astype(BF16)
    if dims == NN:
        da, db = _mm(g16, b16, NT), _mm(a16, g16, TN)
    elif dims == NT:
        da, db = _mm(g16, b16, NN), _mm(g16, a16, TN)
    else:
        da, db = _mm(b16, g16, NT), _mm(a16, g16, NN)
    return da.astype(ta.dtype), db.astype(tb.dtype)


_bdot_vjp.defvjp(_bdot_fwd, _bdot_bwd)


def _bdot(a, b, dims=NN):
    return _bdot_vjp(a, b, dims)


def _hdot(a, b):
    return jnp.dot(a, b, preferred_element_type=F32, precision=lax.Precision.HIGHEST)


def _row_call(name, body, n_rows, rb, rows, consts, outs, accs=(), reverse=False, scratch=(), halos=()):
    n = n_rows // rb
    assert n * rb == n_rows
    idx = (lambda i: (n - 1 - i, 0)) if reverse else (lambda i: (i, 0))
    in_specs = [pl.BlockSpec((rb, a.shape[1]), idx) for a in rows]
    in_specs += [pl.BlockSpec((hr, a.shape[1]), fn) for a, hr, fn in halos]
    in_specs += [pl.BlockSpec(c.shape, functools.partial(lambda i, nd: (0,) * nd, nd=c.ndim)) for c in consts]
    out_specs = [pl.BlockSpec((rb, c), idx) for c, _ in outs]
    out_specs += [pl.BlockSpec(s, functools.partial(lambda i, nd: (0,) * nd, nd=len(s))) for s, _ in accs]
    out_shape = [jax.ShapeDtypeStruct((n_rows, c), dt) for c, dt in outs]
    out_shape += [jax.ShapeDtypeStruct(s, dt) for s, dt in accs]
    return pl.pallas_call(
        body, grid=(n,), in_specs=in_specs, out_specs=out_specs, out_shape=out_shape,
        scratch_shapes=list(scratch), name=name, compiler_params=_cparams(),
    )(*rows, *[a for a, _, _ in halos], *consts)


def _token_views(x):
    per = ROW_BLOCK // ATTN_BLOCK
    return [(x, ATTN_BLOCK, functools.partial(lambda i, k: (jnp.maximum(per * i - 1 + k, 0), 0), k=k)) for k in range(per)]


def _padded_block(i, front, views):
    first = jnp.where(i == 0, front, views[0][...]) if front is not None else views[0][...]
    return jnp.concatenate([first] + [v[...] for v in views[1:]], axis=0)


def _attn_in_fwd(x, front, norm_w, w_in):
    T = x.shape[0] + ATTN_BLOCK

    def body(xa_ref, xb_ref, xc_ref, front_ref, nw_ref, w_ref, xn_ref, q_ref, kv_ref, gate_ref):
        h = _padded_block(pl.program_id(0), front_ref[...], (xa_ref, xb_ref, xc_ref))
        xn = _rms(h, nw_ref[...]).astype(BF16)
        xn_ref[...] = xn
        q_ref[...] = jnp.dot(xn, w_ref[:, 0:1024], preferred_element_type=F32)
        kv_ref[...] = jnp.dot(xn, w_ref[:, 1024:1280], preferred_element_type=F32)
        gate_ref[...] = jnp.dot(xn, w_ref[:, 1280:2304], preferred_element_type=F32)

    return _row_call("attn_in_fwd", body, T, ROW_BLOCK, [], [front, norm_w, w_in],
                     [(1024, BF16), (1024, F32), (256, F32), (1024, F32)], halos=_token_views(x))


def _attn_bias(n, j):
    C, R = 2 * ATTN_BLOCK + N_META, ATTN_GROUPS * ATTN_BLOCK
    c = lax.broadcasted_iota(jnp.int32, (C, R), 0)
    r = lax.broadcasted_iota(jnp.int32, (C, R), 1)
    ql = r & (ATTN_BLOCK - 1)
    is_meta = c >= 2 * ATTN_BLOCK
    dist_band = ATTN_BLOCK + ql - c
    cmin = jnp.maximum(0, 2 * ATTN_BLOCK - ATTN_BLOCK * n)
    valid_band = (c >= cmin) & (dist_band >= 0) & (dist_band < ATTN_BLOCK)
    dist_meta = ATTN_BLOCK * n + ql - FRONT_PAD - (c - 2 * ATTN_BLOCK)
    valid = (is_meta & (dist_meta >= 0)) | (jnp.logical_not(is_meta) & valid_band)
    dist = jnp.minimum(jnp.where(is_meta, dist_meta, dist_band), ATTN_BLOCK).astype(F32)
    rr = lax.broadcasted_iota(jnp.int32, (1, R), 1)
    head = (rr >> 7).astype(F32) + float(ATTN_GROUPS * j + 1)
    slope = jnp.exp(head * (-0.5 * math.log(2.0)))
    return jnp.where(valid, slope * dist, -NEG)


def _attn_tables(n, refresh, bias_ref):
    @pl.when(refresh)
    def _():
        for j in range(ATTN_KV_HEADS):
            bias_ref[j] = _attn_bias(n, j)


def _attn_table_scratch():
    return [pltpu.VMEM((ATTN_KV_HEADS, 2 * ATTN_BLOCK + N_META, ATTN_GROUPS * ATTN_BLOCK), F32)]


def _attn_group(q_t, k, v, sinkrow, qnw_col, knw, bias):
    qn = q_t * lax.rsqrt(jnp.mean(q_t * q_t, axis=0, keepdims=True) + NORM_EPS) * qnw_col
    kn = _rms(k, knw)
    s = _bdot(kn, qn) * (ATTN_HD ** -0.5) - bias
    m = lax.stop_gradient(jnp.maximum(jnp.max(s, axis=0, keepdims=True), sinkrow))
    e = jnp.exp(s - m)
    denom = jnp.sum(e, axis=0, keepdims=True) + jnp.exp(sinkrow - m)
    p = e * (1.0 / denom)
    return _bdot(v, p, TN)


def _sink_row(sinks_ref, j):
    rr = lax.broadcasted_iota(jnp.int32, (1, ATTN_GROUPS * ATTN_BLOCK), 1) >> 7
    row = jnp.zeros((1, ATTN_GROUPS * ATTN_BLOCK), F32)
    for hl in range(ATTN_GROUPS):
        row = jnp.where(rr == hl, sinks_ref[0, ATTN_GROUPS * j + hl], row)
    return row


def _heads_to_lanes(ref, j):
    return jnp.concatenate([ref[:, ATTN_HD * h:ATTN_HD * (h + 1)].T
                            for h in range(ATTN_GROUPS * j, ATTN_GROUPS * (j + 1))], axis=1)


def _lanes_to_heads(ref, j, x_t):
    for hl in range(ATTN_GROUPS):
        h = ATTN_GROUPS * j + hl
        ref[:, ATTN_HD * h:ATTN_HD * (h + 1)] = x_t[:, ATTN_BLOCK * hl:ATTN_BLOCK * (hl + 1)].T


def _attn_kv_tiles(kvp_ref, kvc_ref, kvm_ref, j):
    ksl = slice(ATTN_HD * j, ATTN_HD * (j + 1))
    vsl = slice(128 + ATTN_HD * j, 128 + ATTN_HD * (j + 1))
    k = jnp.concatenate([kvp_ref[:, ksl], kvc_ref[:, ksl], kvm_ref[FRONT_PAD:, ksl]], axis=0)
    v = jnp.concatenate([kvp_ref[:, vsl], kvc_ref[:, vsl], kvm_ref[FRONT_PAD:, vsl]], axis=0)
    return k, v


def _attn_core_fwd(q, kv, sinks, qnw, knw):
    T = q.shape[0]
    nb = T // ATTN_BLOCK

    def body(sinks_ref, q_ref, kvc_ref, kvp_ref, kvm_ref, qnw_ref, knw_ref, o_ref, bias_ref):
        n = pl.program_id(0)
        _attn_tables(n, n <= 2, bias_ref)
        for j in range(ATTN_KV_HEADS):
            k, v = _attn_kv_tiles(kvp_ref, kvc_ref, kvm_ref, j)
            o_t = _attn_group(_heads_to_lanes(q_ref, j), k, v, _sink_row(sinks_ref, j), qnw_ref[...], knw_ref[...],
                              bias_ref[j])
            _lanes_to_heads(o_ref, j, o_t)

    return pl.pallas_call(
        body, grid=(nb,),
        in_specs=[pl.BlockSpec(memory_space=pltpu.SMEM),
                  pl.BlockSpec((ATTN_BLOCK, 1024), lambda i: (i, 0)),
                  pl.BlockSpec((ATTN_BLOCK, 256), lambda i: (i, 0)),
                  pl.BlockSpec((ATTN_BLOCK, 256), lambda i: (jnp.maximum(i - 1, 0), 0)),
                  pl.BlockSpec((ATTN_BLOCK, 256), lambda i: (0, 0)),
                  pl.BlockSpec((ATTN_HD, 1), lambda i: (0, 0)),
                  pl.BlockSpec((1, ATTN_HD), lambda i: (0, 0))],
        out_specs=pl.BlockSpec((ATTN_BLOCK, 1024), lambda i: (i, 0)),
        out_shape=jax.ShapeDtypeStruct((T, 1024), F32),
        scratch_shapes=_attn_table_scratch(),
        name="attn_core_fwd", compiler_params=_cparams(),
    )(sinks, q, kv, kv, kv, qnw.reshape(ATTN_HD, 1), knw)


def _attn_out_fwd(o, gate, x, front, w_out):
    T = o.shape[0]

    def body(o_ref, g_ref, xa_ref, xb_ref, xc_ref, front_ref, w_ref, h1_ref):
        h = _padded_block(pl.program_id(0), front_ref[...], (xa_ref, xb_ref, xc_ref))
        og = o_ref[...] * _silu(g_ref[...])
        h1_ref[...] = h + _bdot(og, w_ref[...])

    return _row_call("attn_out_fwd", body, T, ROW_BLOCK, [o, gate], [front, w_out], [(1024, F32)], halos=_token_views(x))[0]


def _wgrad(xn, du, cg, name):
    T, kdim = xn.shape
    cdim = du.shape[1]
    rows = WGRAD_ROWS if T % WGRAD_ROWS == 0 else ROW_BLOCK
    nr, nc = T // rows, cdim // cg
    assert nc * cg == cdim

    def body(x_ref, du_ref, dw_ref):
        @pl.when(pl.program_id(1) == 0)
        def _():
            dw_ref[...] = jnp.zeros_like(dw_ref)
        dw_ref[...] += _bdot(x_ref[...], du_ref[...], TN)

    return pl.pallas_call(
        body, grid=(nc, nr),
        in_specs=[pl.BlockSpec((rows, kdim), lambda j, i: (i, 0)),
                  pl.BlockSpec((rows, cg), lambda j, i: (i, j))],
        out_specs=pl.BlockSpec((kdim, cg), lambda j, i: (0, j)),
        out_shape=jax.ShapeDtypeStruct((kdim, cdim), F32),
        name=name, compiler_params=_cparams(("arbitrary", "arbitrary")),
    )(xn, du)


def _attn_out_bwd(dh1, o, gate, w_out):
    T = o.shape[0]

    def body(dh_ref, o_ref, g_ref, w_ref, do_ref, dg_ref, dw_ref):
        @pl.when(pl.program_id(0) == 0)
        def _():
            dw_ref[...] = jnp.zeros_like(dw_ref)
        dh = dh_ref[...]
        dog = _bdot(dh, w_ref[...], NT)
        og, vjp = jax.vjp(lambda o_, g_: o_ * _silu(g_), o_ref[...], g_ref[...])
        do, dg = vjp(dog)
        do_ref[...] = do
        dg_ref[...] = dg
        dw_ref[...] += _bdot(og, dh, TN)

    return _row_call("attn_out_bwd", body, T, ROW_BLOCK, [dh1, o, gate], [w_out],
                     [(1024, F32), (1024, F32)], [((1024, 1024), F32)])


def _attn_core_bwd(do, q, kv, sinks, qnw, knw):
    T = q.shape[0]
    nb = T // ATTN_BLOCK
    rev = lambda i: nb - 1 - i

    def body(sinks_ref, do_ref, q_ref, kvc_ref, kvp_ref, kvm_ref, qnw_ref, knw_ref,
             dq_ref, dkv_ref, dsinks_ref, dqnw_ref, dknw_ref, carry_ref, meta_ref, bias_ref):
        step = pl.program_id(0)
        n = rev(step)
        _attn_tables(n, (step == 0) | (n <= 1), bias_ref)

        @pl.when(step == 0)
        def _():
            carry_ref[...] = jnp.zeros_like(carry_ref)
            meta_ref[...] = jnp.zeros_like(meta_ref)
            dsinks_ref[...] = jnp.zeros_like(dsinks_ref)
            dqnw_ref[...] = jnp.zeros_like(dqnw_ref)
            dknw_ref[...] = jnp.zeros_like(dknw_ref)

        lane16 = lax.broadcasted_iota(jnp.int32, (1, ATTN_HEADS), 1)
        dsinks = jnp.zeros((1, ATTN_HEADS), F32)
        for j in range(ATTN_KV_HEADS):
            k, v = _attn_kv_tiles(kvp_ref, kvc_ref, kvm_ref, j)
            fn = functools.partial(_attn_group, bias=bias_ref[j])
            _, vjp = jax.vjp(fn, _heads_to_lanes(q_ref, j), k, v, _sink_row(sinks_ref, j), qnw_ref[...], knw_ref[...])
            dq_t, dk, dv, dsr, dqn, dkn = vjp(_heads_to_lanes(do_ref, j))
            _lanes_to_heads(dq_ref, j, dq_t)
            dqnw_ref[...] += dqn
            dknw_ref[...] += dkn
            for hl in range(ATTN_GROUPS):
                dsinks = dsinks + jnp.where(lane16 == ATTN_GROUPS * j + hl,
                                            jnp.sum(dsr[:, ATTN_BLOCK * hl:ATTN_BLOCK * (hl + 1)]), 0.0)
            ksl = slice(ATTN_HD * j, ATTN_HD * (j + 1))
            vsl = slice(128 + ATTN_HD * j, 128 + ATTN_HD * (j + 1))
            for sl, d in ((ksl, dk), (vsl, dv)):
                dkv_ref[:, sl] = d[ATTN_BLOCK:2 * ATTN_BLOCK, :] + carry_ref[:, sl]
                carry_ref[:, sl] = d[0:ATTN_BLOCK, :]
                meta_ref[:, sl] += d[2 * ATTN_BLOCK:, :]
        dsinks_ref[...] += dsinks

        @pl.when(n == 0)
        def _():
            dkv_ref[FRONT_PAD:, :] += meta_ref[...]

    dq, dkv, dsinks, dqnw, dknw = pl.pallas_call(
        body, grid=(nb,),
        in_specs=[pl.BlockSpec(memory_space=pltpu.SMEM),
                  pl.BlockSpec((ATTN_BLOCK, 1024), lambda i: (rev(i), 0)),
                  pl.BlockSpec((ATTN_BLOCK, 1024), lambda i: (rev(i), 0)),
                  pl.BlockSpec((ATTN_BLOCK, 256), lambda i: (rev(i), 0)),
                  pl.BlockSpec((ATTN_BLOCK, 256), lambda i: (jnp.maximum(rev(i) - 1, 0), 0)),
                  pl.BlockSpec((ATTN_BLOCK, 256), lambda i: (0, 0)),
                  pl.BlockSpec((ATTN_HD, 1), lambda i: (0, 0)),
                  pl.BlockSpec((1, ATTN_HD), lambda i: (0, 0))],
        out_specs=[pl.BlockSpec((ATTN_BLOCK, 1024), lambda i: (rev(i), 0)),
                   pl.BlockSpec((ATTN_BLOCK, 256), lambda i: (rev(i), 0)),
                   pl.BlockSpec((1, ATTN_HEADS), lambda i: (0, 0)),
                   pl.BlockSpec((ATTN_HD, 1), lambda i: (0, 0)),
                   pl.BlockSpec((1, ATTN_HD), lambda i: (0, 0))],
        out_shape=[jax.ShapeDtypeStruct((T, 1024), F32), jax.ShapeDtypeStruct((T, 256), F32),
                   jax.ShapeDtypeStruct((1, ATTN_HEADS), F32), jax.ShapeDtypeStruct((ATTN_HD, 1), F32),
                   jax.ShapeDtypeStruct((1, ATTN_HD), F32)],
        scratch_shapes=[pltpu.VMEM((ATTN_BLOCK, 256), F32), pltpu.VMEM((N_META, 256), F32)] + _attn_table_scratch(),
        name="attn_core_bwd", compiler_params=_cparams(),
    )(sinks, do, q, kv, kv, kv, qnw.reshape(ATTN_HD, 1), knw)
    return dq, dkv, dsinks, dqnw.reshape(1, ATTN_HD), dknw


def _attn_in_bwd(dq, dkv, dgate, x, front, dh1, norm_w, w_in):
    T = dq.shape[0]

    def body(dq_ref, dkv_ref, dg_ref, dh1_ref, xa_ref, xb_ref, xc_ref, front_ref, nw_ref, w_ref, dh0_ref, dnw_ref):
        @pl.when(pl.program_id(0) == 0)
        def _():
            dnw_ref[...] = jnp.zeros_like(dnw_ref)
        h = _padded_block(pl.program_id(0), front_ref[...], (xa_ref, xb_ref, xc_ref))
        dxn = (_bdot(dq_ref[...], w_ref[:, 0:1024], NT) + _bdot(dkv_ref[...], w_ref[:, 1024:1280], NT)
               + _bdot(dg_ref[...], w_ref[:, 1280:2304], NT))
        _, vjp = jax.vjp(_rms, h, nw_ref[...])
        dh, dnw = vjp(dxn)
        dh0_ref[...] = dh1_ref[...] + dh
        dnw_ref[...] += dnw

    return _row_call("attn_in_bwd", body, T, ROW_BLOCK, [dq, dkv, dgate, dh1], [front, norm_w, w_in],
                     [(1024, F32)], [((1, 1024), F32)], halos=_token_views(x))


def _dn_in_fwd(h1, norm_w, w_in):
    T = h1.shape[0]

    def body(h_ref, nw_ref, w_ref, xn_ref, qkv_ref, z_ref, ba_ref):
        xn = _rms(h_ref[...], nw_ref[...]).astype(BF16)
        xn_ref[...] = xn
        qkv_ref[...] = jnp.dot(xn, w_ref[:, 0:4096], preferred_element_type=F32)
        z_ref[...] = jnp.dot(xn, w_ref[:, 4096:6144], preferred_element_type=F32)
        ba_ref[...] = jnp.dot(xn, w_ref[:, 6144:6176], preferred_element_type=F32)

    return _row_call("dn_in_fwd", body, T, ROW_BLOCK, [h1], [norm_w, w_in],
                     [(1024, BF16), (4096, F32), (2048, F32), (32, F32)])


def _shift_down(cur, prev8, s):
    i8 = lax.broadcasted_iota(jnp.int32, (8, cur.shape[1]), 0)
    r = pltpu.roll(cur, s, 0)
    head = jnp.where(i8 < s, pltpu.roll(prev8, s, 0), r[0:8])
    return jnp.concatenate([head, r[8:]], axis=0)


def _shift_up(cur, next8, s):
    n = cur.shape[0]
    i8 = lax.broadcasted_iota(jnp.int32, (8, cur.shape[1]), 0)
    r = pltpu.roll(cur, n - s, 0)
    tail = jnp.where(i8 >= 8 - s, pltpu.roll(next8, 8 - s, 0), r[n - 8:])
    return jnp.concatenate([r[:n - 8], tail], axis=0)


def _conv_taps(cur, prev8):
    return [cur] + [_shift_down(cur, prev8, s) for s in range(1, DN_CONV_K)]


def _conv_tile(taps, w):
    out = w[3:4, :] * taps[0]
    for s in range(1, DN_CONV_K):
        out = out + w[3 - s:4 - s, :] * taps[s]
    return out


def _l2n(a, scale):
    return a * (lax.rsqrt(jnp.sum(a * a, axis=-1, keepdims=True) + NORM_EPS) * scale)


def _dn_post_tile(c, t):
    a = _silu(c)
    if t < DN_K_HEADS:
        return _l2n(a, DN_HD ** -0.5)
    if t < 2 * DN_K_HEADS:
        return _l2n(a, 1.0)
    return a


def _dn_beta_g(ba, a_log, dt_bias, live):
    beta = jax.nn.sigmoid(ba[:, 0:DN_V_HEADS]) * live
    g = -jnp.exp(a_log) * _softplus(ba[:, DN_V_HEADS:] + dt_bias) * live
    return beta, g


def _live_rows(i, rb):
    rows = i * rb + lax.broadcasted_iota(jnp.int32, (rb, 1), 0)
    return (rows >= FRONT_PAD).astype(F32)


def _halo_spec_args(x, rb):
    per = rb // 8
    return (x, 8, lambda i: (jnp.maximum(i * per - 1, 0), 0))


def _dn_conv_fwd(qkv, ba, conv_w, a_log, dt_bias):
    T = qkv.shape[0]

    def body(x_ref, ba_ref, halo_ref, cw_ref, al_ref, dtb_ref, q_ref, k_ref, v_ref, bg_ref, c_ref):
        i = pl.program_id(0)
        first = (i > 0).astype(F32)
        for t in range(DN_CONV_W // 128):
            cols = slice(128 * t, 128 * (t + 1))
            c = _conv_tile(_conv_taps(x_ref[:, cols], halo_ref[:, cols] * first), cw_ref[:, cols])
            c_ref[:, cols] = c
            out = _dn_post_tile(c, t)
            if t < DN_K_HEADS:
                q_ref[:, cols] = out
            elif t < 2 * DN_K_HEADS:
                k_ref[:, 128 * (t - 8):128 * (t - 7)] = out
            else:
                v_ref[:, 128 * (t - 16):128 * (t - 15)] = out
        beta, g = _dn_beta_g(ba_ref[...], al_ref[...], dtb_ref[...], _live_rows(i, ROW_BLOCK))
        bg_ref[:, 0:DN_V_HEADS] = beta
        bg_ref[:, DN_V_HEADS:] = g

    return _row_call("dn_conv_fwd", body, T, ROW_BLOCK, [qkv, ba], [conv_w, a_log, dt_bias],
                     [(1024, F32), (1024, F32), (2048, F32), (32, F32), (4096, F32)], halos=[_halo_spec_args(qkv, ROW_BLOCK)])


def _chunk_masks():
    r = lax.broadcasted_iota(jnp.int32, (DN_CHUNK, DN_CHUNK), 0)
    c = lax.broadcasted_iota(jnp.int32, (DN_CHUNK, DN_CHUNK), 1)
    return r >= c, r > c, r == c, r <= c


def _tri_inv_block(x):
    B = TRI_BLOCK
    n = range(len(x))
    r_, c_ = lax.broadcasted_iota(jnp.int32, (B, B), 0), lax.broadcasted_iota(jnp.int32, (B, B), 1)
    ainv = [jnp.where(r_ == c_, 1.0, 0.0) + x[h] for h in n]
    p = [_bdot(x[h], x[h]) for h in n]
    for _ in range(B.bit_length() - 3):
        r = [_bdot(jnp.concatenate([p[h], ainv[h]], axis=0), p[h]) for h in n]
        ainv = [ainv[h] + r[h][B:] for h in n]
        p = [r[h][:B] for h in n]
    return [ainv[h] + _bdot(ainv[h], p[h]) for h in n]


def _tri_inv(x):
    B = TRI_BLOCK
    assert DN_CHUNK == 2 * B
    n = len(x)
    diag = _tri_inv_block([x[h][:B, :B] for h in range(n)] + [x[h][B:, B:] for h in range(n)])
    a11, a22 = diag[:n], diag[n:]
    a21 = [_bdot(_bdot(a22[h], x[h][B:, :B]), a11[h]) for h in range(n)]
    zero = jnp.zeros((B, B), F32)
    return [jnp.concatenate([jnp.concatenate([a11[h], zero], axis=1), jnp.concatenate([a21[h], a22[h]], axis=1)], axis=0)
            for h in range(n)]


@jax.custom_vjp
def _tri_inv_known(x, a):
    return a


def _tri_inv_known_fwd(x, a):
    return a, a


def _tri_inv_known_bwd(a, da):
    return [_bdot(_bdot(a[h], da[h], TN), a[h], NT) for h in range(len(a))], [jnp.zeros_like(t) for t in a]


_tri_inv_known.defvjp(_tri_inv_known_fwd, _tri_inv_known_bwd)


def _dn_chunk_step(S, q, k, v, beta, g, masks, known_inv=None, with_inv=False):
    causal, strict, eye, upper = masks
    C, W = DN_CHUNK, DN_HD
    heads = range(len(v))
    k_t = [k[j].T for j in range(len(k))]
    qk_kk = [_bdot(jnp.concatenate([q[j], k[j]], axis=0), k_t[j]) for j in range(len(q))]
    g_b = [jnp.broadcast_to(g[h], (C, C)) for h in heads]
    beta_b = [jnp.broadcast_to(beta[h], (C, W)) for h in heads]
    g_row = [jnp.sum(jnp.where(eye, g_b[h], 0.0), axis=0, keepdims=True) for h in heads]
    gc_col = [jnp.sum(jnp.where(causal, g_row[h], 0.0), axis=1, keepdims=True) for h in heads]
    gc_row = [jnp.sum(jnp.where(upper, g_b[h], 0.0), axis=0, keepdims=True) for h in heads]
    g_last = [jnp.sum(g_row[h], axis=1, keepdims=True) for h in heads]
    gc_b = [jnp.broadcast_to(gc_col[h], (C, W)) for h in heads]
    decay = [jnp.exp(jnp.where(causal, gc_b[h][:, :C] - gc_row[h], NEG)) for h in heads]
    eg_b = [jnp.exp(gc_b[h]) for h in heads]
    x = [jnp.where(strict, qk_kk[h // 2][C:] * beta_b[h][:, :C] * decay[h], 0.0) * -1.0 for h in heads]
    ainv = _tri_inv(x) if known_inv is None else _tri_inv_known(x, known_inv)
    uw = [_bdot(ainv[h], jnp.concatenate([v[h] * beta_b[h], k[h // 2] * (beta_b[h] * eg_b[h])], axis=1)) for h in heads]
    ws_qs = [_bdot(jnp.concatenate([uw[h][:, W:], q[h // 2] * eg_b[h]], axis=0), S[h]) for h in heads]
    v_new = [uw[h][:, :W] - ws_qs[h][:C] for h in heads]
    o = [ws_qs[h][C:] + _bdot(qk_kk[h // 2][:C] * decay[h], v_new[h]) for h in heads]
    s_new = [S[h] * jnp.exp(g_last[h]) + _bdot(k_t[h // 2] * jnp.exp(g_last[h] - gc_row[h]), v_new[h]) for h in heads]
    return (s_new, o, ainv) if with_inv else (s_new, o)


def _dn_chunk_tiles(q_ref, k_ref, v_ref, bg_ref):
    q = [q_ref[:, 128 * j:128 * (j + 1)] for j in range(DN_K_HEADS)]
    k = [k_ref[:, 128 * j:128 * (j + 1)] for j in range(DN_K_HEADS)]
    v = [v_ref[:, 128 * h:128 * (h + 1)] for h in range(DN_V_HEADS)]
    beta = [bg_ref[:, h:h + 1] for h in range(DN_V_HEADS)]
    g = [bg_ref[:, DN_V_HEADS + h:DN_V_HEADS + h + 1] for h in range(DN_V_HEADS)]
    return q, k, v, beta, g


def _dn_scan_fwd(qn, kn, v, bg):
    T = qn.shape[0]
    nc = T // DN_CHUNK

    def body(q_ref, k_ref, v_ref, bg_ref, o_ref, ssave_ref, inv_ref, s_ref):
        @pl.when(pl.program_id(0) == 0)
        def _():
            s_ref[...] = jnp.zeros_like(s_ref)
        masks = _chunk_masks()
        q, k, v, beta, g = _dn_chunk_tiles(q_ref, k_ref, v_ref, bg_ref)
        for first in range(0, DN_V_HEADS, SCAN_FWD_GROUP):
            heads = range(first, first + SCAN_FWD_GROUP)
            pairs = slice(first // 2, (first + SCAN_FWD_GROUP) // 2)
            hs = slice(first, first + SCAN_FWD_GROUP)
            s_old = [s_ref[h] for h in heads]
            for i, h in enumerate(heads):
                ssave_ref[0, h] = s_old[i]
            s_new, o, ainv = _dn_chunk_step(s_old, q[pairs], k[pairs], v[hs], beta[hs], g[hs], masks, with_inv=True)
            for i, h in enumerate(heads):
                o_ref[:, 128 * h:128 * (h + 1)] = o[i]
                inv_ref[0, h] = ainv[i].astype(BF16)
                s_ref[h] = s_new[i]

    return pl.pallas_call(
        body, grid=(nc,),
        in_specs=[pl.BlockSpec((DN_CHUNK, 1024), lambda i: (i, 0)),
                  pl.BlockSpec((DN_CHUNK, 1024), lambda i: (i, 0)),
                  pl.BlockSpec((DN_CHUNK, 2048), lambda i: (i, 0)),
                  pl.BlockSpec((DN_CHUNK, 32), lambda i: (i, 0))],
        out_specs=[pl.BlockSpec((DN_CHUNK, 2048), lambda i: (i, 0)),
                   pl.BlockSpec((1, DN_V_HEADS, DN_HD, DN_HD), lambda i: (i, 0, 0, 0)),
                   pl.BlockSpec((1, DN_V_HEADS, DN_CHUNK, DN_CHUNK), lambda i: (i, 0, 0, 0))],
        out_shape=[jax.ShapeDtypeStruct((T, 2048), F32),
                   jax.ShapeDtypeStruct((nc, DN_V_HEADS, DN_HD, DN_HD), F32),
                   jax.ShapeDtypeStruct((nc, DN_V_HEADS, DN_CHUNK, DN_CHUNK), BF16)],
        scratch_shapes=[pltpu.VMEM((DN_V_HEADS, DN_HD, DN_HD), F32)],
        name="dn_scan_fwd", compiler_params=_cparams(),
    )(qn, kn, v, bg)


def _dn_gate_tile(o, z, onw):
    return _rms(o, onw) * _silu(z)


def _dn_out_fwd(o, z, h1, target, w_out, onw):
    T = o.shape[0]

    def body(o_ref, z_ref, h_ref, ta_ref, tb_ref, tc_ref, w_ref, onw_ref, dy_ref, og_ref, loss_ref):
        i = pl.program_id(0)

        @pl.when(i == 0)
        def _():
            loss_ref[...] = jnp.zeros_like(loss_ref)
        for h in range(DN_V_HEADS):
            cols = slice(128 * h, 128 * (h + 1))
            og_ref[:, cols] = _dn_gate_tile(o_ref[:, cols], z_ref[:, cols], onw_ref[...]).astype(BF16)
        y = h_ref[...] + jnp.dot(og_ref[...], w_ref[...], preferred_element_type=F32)
        rows = i * ROW_BLOCK + lax.broadcasted_iota(jnp.int32, (ROW_BLOCK, 1), 0)
        diff = jnp.where(rows >= FRONT_PAD + N_META, y - _padded_block(i, None, (ta_ref, tb_ref, tc_ref)), 0.0)
        dy_ref[...] = diff * (1.0 / D_MODEL)
        loss_ref[...] += jnp.sum(diff * diff) * (0.5 / D_MODEL)

    return _row_call("dn_out_fwd", body, T, ROW_BLOCK, [o, z, h1], [w_out, onw],
                     [(1024, F32), (2048, BF16)], [((1, 128), F32)], halos=_token_views(target))


def _dn_out_bwd(dy, o, z, w_out, onw):
    T = o.shape[0]

    def body(dy_ref, o_ref, z_ref, w_ref, onw_ref, do_ref, dz_ref, donw_ref, dog_ref):
        @pl.when(pl.program_id(0) == 0)
        def _():
            donw_ref[...] = jnp.zeros_like(donw_ref)
        dy = dy_ref[...].astype(BF16)
        donw = jnp.zeros((1, DN_HD), F32)
        for half in range(2):
            hcols = slice(1024 * half, 1024 * (half + 1))
            dog_ref[:, hcols] = lax.dot_general(dy, w_ref[hcols, :], NT, preferred_element_type=F32)
        for h in range(DN_V_HEADS):
            cols = slice(128 * h, 128 * (h + 1))
            _, vjp = jax.vjp(_dn_gate_tile, o_ref[:, cols], z_ref[:, cols], onw_ref[...])
            do, dz, dn = vjp(dog_ref[:, cols])
            do_ref[:, cols] = do
            dz_ref[:, cols] = dz
            donw = donw + dn
        donw_ref[...] += donw

    return _row_call("dn_out_bwd", body, T, ROW_BLOCK, [dy, o, z], [w_out, onw],
                     [(2048, F32), (2048, F32)], [((1, DN_HD), F32)], scratch=[pltpu.VMEM((ROW_BLOCK, 2048), F32)])


def _dn_scan_bwd(do, qn, kn, v, bg, ssave, inv):
    T = qn.shape[0]
    nc = T // DN_CHUNK
    rev = lambda i: nc - 1 - i

    def body(do_ref, q_ref, k_ref, v_ref, bg_ref, ss_ref, inv_ref, dq_ref, dk_ref, dv_ref, dbg_ref, ds_ref):
        @pl.when(pl.program_id(0) == 0)
        def _():
            ds_ref[...] = jnp.zeros_like(ds_ref)
        lane32 = lax.broadcasted_iota(jnp.int32, (1, 2 * DN_V_HEADS), 1)
        masks = _chunk_masks()
        q, k, v, beta, g = _dn_chunk_tiles(q_ref, k_ref, v_ref, bg_ref)
        dbg = jnp.zeros((DN_CHUNK, 2 * DN_V_HEADS), F32)
        for first in range(0, DN_V_HEADS, SCAN_BWD_GROUP):
            heads = range(first, first + SCAN_BWD_GROUP)
            pairs = slice(first // 2, (first + SCAN_BWD_GROUP) // 2)
            hs = slice(first, first + SCAN_BWD_GROUP)
            fn = functools.partial(_dn_chunk_step, masks=masks, known_inv=[inv_ref[0, h].astype(F32) for h in heads])
            _, vjp = jax.vjp(fn, [ss_ref[0, h] for h in heads], q[pairs], k[pairs], v[hs], beta[hs], g[hs])
            ds, dq, dk, dv, dbeta, dg = vjp(([ds_ref[h] for h in heads], [do_ref[:, 128 * h:128 * (h + 1)] for h in heads]))
            for i, h in enumerate(heads):
                ds_ref[h] = ds[i]
                dv_ref[:, 128 * h:128 * (h + 1)] = dv[i]
                dbg = dbg + jnp.where(lane32 == h, dbeta[i], 0.0) + jnp.where(lane32 == DN_V_HEADS + h, dg[i], 0.0)
            for i, j in enumerate(range(first // 2, (first + SCAN_BWD_GROUP) // 2)):
                dq_ref[:, 128 * j:128 * (j + 1)] = dq[i]
                dk_ref[:, 128 * j:128 * (j + 1)] = dk[i]
        dbg_ref[...] = dbg

    return pl.pallas_call(
        body, grid=(nc,),
        in_specs=[pl.BlockSpec((DN_CHUNK, 2048), lambda i: (rev(i), 0)),
                  pl.BlockSpec((DN_CHUNK, 1024), lambda i: (rev(i), 0)),
                  pl.BlockSpec((DN_CHUNK, 1024), lambda i: (rev(i), 0)),
                  pl.BlockSpec((DN_CHUNK, 2048), lambda i: (rev(i), 0)),
                  pl.BlockSpec((DN_CHUNK, 32), lambda i: (rev(i), 0)),
                  pl.BlockSpec((1, DN_V_HEADS, DN_HD, DN_HD), lambda i: (rev(i), 0, 0, 0)),
                  pl.BlockSpec((1, DN_V_HEADS, DN_CHUNK, DN_CHUNK), lambda i: (rev(i), 0, 0, 0))],
        out_specs=[pl.BlockSpec((DN_CHUNK, 1024), lambda i: (rev(i), 0)),
                   pl.BlockSpec((DN_CHUNK, 1024), lambda i: (rev(i), 0)),
                   pl.BlockSpec((DN_CHUNK, 2048), lambda i: (rev(i), 0)),
                   pl.BlockSpec((DN_CHUNK, 32), lambda i: (rev(i), 0))],
        out_shape=[jax.ShapeDtypeStruct((T, 1024), F32), jax.ShapeDtypeStruct((T, 1024), F32),
                   jax.ShapeDtypeStruct((T, 2048), F32), jax.ShapeDtypeStruct((T, 32), F32)],
        scratch_shapes=[pltpu.VMEM((DN_V_HEADS, DN_HD, DN_HD), F32)],
        name="dn_scan_bwd", compiler_params=_cparams(),
    )(do, qn, kn, v, bg, ssave, inv)


def _dn_conv_bwd(dqn, dkn, dv, dbg, qkv, conv_out, ba, conv_w, a_log, dt_bias):
    T = qkv.shape[0]
    rb = ROW_BLOCK // 2
    nr = T // rb

    def body(dq_ref, dk_ref, dv_ref, dbg_ref, x_ref, c_ref, ba_ref, cw_ref, al_ref, dtb_ref,
             dx_ref, dba_ref, dcw_ref, dal_ref, ddtb_ref, carry_ref):
        step = pl.program_id(0)
        i = nr - 1 - step

        @pl.when(step == 0)
        def _():
            carry_ref[...] = jnp.zeros_like(carry_ref)
            dcw_ref[...] = jnp.zeros_like(dcw_ref)
            dal_ref[...] = jnp.zeros_like(dal_ref)
            ddtb_ref[...] = jnp.zeros_like(ddtb_ref)
        for t in range(DN_CONV_W // 128):
            cols = slice(128 * t, 128 * (t + 1))
            w, x = cw_ref[:, cols], x_ref[:, cols]
            if t < DN_K_HEADS:
                dout = dq_ref[:, cols]
            elif t < 2 * DN_K_HEADS:
                dout = dk_ref[:, 128 * (t - 8):128 * (t - 7)]
            else:
                dout = dv_ref[:, 128 * (t - 16):128 * (t - 15)]
            _, vjp = jax.vjp(functools.partial(_dn_post_tile, t=t), c_ref[:, cols])
            (dc,) = vjp(dout)
            nxt = carry_ref[:, cols]
            dx = w[3:4, :] * dc
            dcw_ref[3:4, cols] += jnp.sum(dc * x, axis=0, keepdims=True)
            for s in range(1, DN_CONV_K):
                up = _shift_up(dc, nxt, s)
                dx = dx + w[3 - s:4 - s, :] * up
                dcw_ref[3 - s:4 - s, cols] += jnp.sum(up * x, axis=0, keepdims=True)
            dx_ref[:, cols] = dx
            carry_ref[:, cols] = dc[0:8, :]
        fn = functools.partial(_dn_beta_g, live=_live_rows(i, rb))
        _, vjp = jax.vjp(fn, ba_ref[...], al_ref[...], dtb_ref[...])
        dba, dal, ddtb = vjp((dbg_ref[:, 0:DN_V_HEADS], dbg_ref[:, DN_V_HEADS:]))
        dba_ref[...] = dba
        dal_ref[...] += dal
        ddtb_ref[...] += ddtb

    return _row_call("dn_conv_bwd", body, T, rb, [dqn, dkn, dv, dbg, qkv, conv_out, ba], [conv_w, a_log, dt_bias],
                     [(4096, F32), (32, F32)], [((DN_CONV_K, 4096), F32), ((1, DN_V_HEADS), F32), ((1, DN_V_HEADS), F32)],
                     reverse=True, scratch=[pltpu.VMEM((8, 4096), F32)])


def _dn_in_bwd(dqkv, dz, dba, h1, dy, norm_w, w_in):
    T = h1.shape[0]

    def body(dqkv_ref, dz_ref, dba_ref, h_ref, dy_ref, nw_ref, w_ref, dh_ref, dnw_ref):
        @pl.when(pl.program_id(0) == 0)
        def _():
            dnw_ref[...] = jnp.zeros_like(dnw_ref)
        dxn = (_bdot(dqkv_ref[...], w_ref[:, 0:4096], NT) + _bdot(dz_ref[...], w_ref[:, 4096:6144], NT)
               + _bdot(dba_ref[...], w_ref[:, 6144:6176], NT))
        _, vjp = jax.vjp(_rms, h_ref[...], nw_ref[...])
        dh, dnw = vjp(dxn)
        dh_ref[...] = (dy_ref[...] + dh) * _live_rows(pl.program_id(0), ROW_BLOCK)
        dnw_ref[...] += dnw

    return _row_call("dn_in_bwd", body, T, ROW_BLOCK, [dqkv, dz, dba, h1, dy], [norm_w, w_in],
                     [(1024, F32)], [((1, 1024), F32)])


def _exchange(parts, scatter, name):
    n = len(parts)
    out_shape = [jax.ShapeDtypeStruct(p.shape if sc else (N_DEV,) + p.shape, p.dtype) for p, sc in zip(parts, scatter)]

    def body(*refs):
        ins, outs = refs[:n], refs[n:2 * n]
        send_sems, recv_sems, local_sems = refs[2 * n:]
        x, y, c = lax.axis_index("x"), lax.axis_index("y"), lax.axis_index("c")
        me = 4 * x + 2 * y + c
        peers = []
        for k in range(1, N_DEV):
            px = 1 - x if k & 4 else x
            py = 1 - y if k & 2 else y
            pc = 1 - c if k & 1 else c
            peers.append(((px, py, pc), 4 * px + 2 * py + pc))

        def src(a, idx):
            return ins[a].at[idx] if scatter[a] else ins[a]

        local = [pltpu.make_async_copy(src(a, me), outs[a].at[me], local_sems.at[a]) for a in range(n)]
        for cp in local:
            cp.start()
        for a in range(n):
            for k, (dev, idx) in enumerate(peers):
                pltpu.make_async_remote_copy(
                    src_ref=src(a, idx), dst_ref=outs[a].at[me], send_sem=send_sems.at[a, k], recv_sem=recv_sems.at[a, k],
                    device_id=dev, device_id_type=pl.DeviceIdType.MESH).start()
        for a in range(n):
            for k, (dev, idx) in enumerate(peers):
                pltpu.make_async_remote_copy(
                    src_ref=src(a, idx), dst_ref=outs[a].at[idx], send_sem=send_sems.at[a, k], recv_sem=recv_sems.at[a, k],
                    device_id=dev, device_id_type=pl.DeviceIdType.MESH).wait()
        for cp in local:
            cp.wait()

    hbm = pl.BlockSpec(memory_space=pltpu.HBM)
    return pl.pallas_call(
        body, out_shape=out_shape, in_specs=[hbm] * n, out_specs=[hbm] * n,
        scratch_shapes=[pltpu.SemaphoreType.DMA((n, N_DEV - 1)), pltpu.SemaphoreType.DMA((n, N_DEV - 1)),
                        pltpu.SemaphoreType.DMA((n,))],
        name=name,
    )(*parts)


def _gather_two_level(parts, name):
    n = len(parts)
    out_shape = [jax.ShapeDtypeStruct((N_DEV,) + p.shape, p.dtype) for p in parts]

    def body(*refs):
        ins, outs = refs[:n], refs[n:2 * n]
        send_sems, recv_sems, local_sems = refs[2 * n:]
        x, y, c = lax.axis_index("x"), lax.axis_index("y"), lax.axis_index("c")
        idx = lambda px, py, pc: 4 * px + 2 * py + pc
        me, sibling = (x, y, c), (x, y, 1 - c)
        chips = [(1 - x, y), (x, 1 - y), (1 - x, 1 - y)]

        def copy(a, k, block, to, src=None):
            slot = outs[a].at[idx(*block)]
            return pltpu.make_async_remote_copy(
                src_ref=slot if src is None else src, dst_ref=slot, send_sem=send_sems.at[a, k], recv_sem=recv_sems.at[a, k],
                device_id=to, device_id_type=pl.DeviceIdType.MESH)

        local = [pltpu.make_async_copy(ins[a], outs[a].at[idx(*me)], local_sems.at[a]) for a in range(n)]
        for cp in local:
            cp.start()
        sent = []
        for a in range(n):
            sent.append(copy(a, 0, me, sibling, src=ins[a]))
            sent += [copy(a, 1 + j, me, (*chip, c), src=ins[a]) for j, chip in enumerate(chips)]
        for cp in sent:
            cp.start()
        for a in range(n):
            for j, chip in enumerate(chips):
                copy(a, 1 + j, (*chip, c), me).wait_recv()
                passed = copy(a, 4 + j, (*chip, c), sibling)
                passed.start()
                sent.append(passed)
        for a in range(n):
            copy(a, 0, sibling, me).wait_recv()
            for j, chip in enumerate(chips):
                copy(a, 4 + j, (*chip, 1 - c), me).wait_recv()
        for cp in sent:
            cp.wait_send()
        for cp in local:
            cp.wait()

    hbm = pl.BlockSpec(memory_space=pltpu.HBM)
    return pl.pallas_call(
        body, out_shape=out_shape, in_specs=[hbm] * n, out_specs=[hbm] * n,
        scratch_shapes=[pltpu.SemaphoreType.DMA((n, N_DEV - 1)), pltpu.SemaphoreType.DMA((n, N_DEV - 1)),
                        pltpu.SemaphoreType.DMA((n,))],
        name=name,
    )(*parts)


def _swap_with_sibling(parts, name):
    n = len(parts)

    def body(*refs):
        ins, outs = refs[:n], refs[n:2 * n]
        send_sems, recv_sems = refs[2 * n:]
        x, y, c = lax.axis_index("x"), lax.axis_index("y"), lax.axis_index("c")
        copies = [pltpu.make_async_remote_copy(
            src_ref=ins[a].at[1 - c], dst_ref=outs[a], send_sem=send_sems.at[a], recv_sem=recv_sems.at[a],
            device_id=(x, y, 1 - c), device_id_type=pl.DeviceIdType.MESH) for a in range(n)]
        for cp in copies:
            cp.start()
        for cp in copies:
            cp.wait()

    hbm = pl.BlockSpec(memory_space=pltpu.HBM)
    return pl.pallas_call(
        body, out_shape=[jax.ShapeDtypeStruct(p.shape[1:], p.dtype) for p in parts], in_specs=[hbm] * n, out_specs=[hbm] * n,
        scratch_shapes=[pltpu.SemaphoreType.DMA((n,)), pltpu.SemaphoreType.DMA((n,))],
        name=name,
    )(*parts)


def _pair_sum(a, b, name):
    R, C = a.shape
    rb = _adam_rows(R)

    def body(a_ref, b_ref, o_ref):
        o_ref[...] = (a_ref[...].astype(F32) + b_ref[...].astype(F32)).astype(BF16)

    blk = pl.BlockSpec((rb, C), lambda i: (i, 0))
    return pl.pallas_call(body, grid=(R // rb,), in_specs=[blk, blk], out_specs=blk,
                          out_shape=jax.ShapeDtypeStruct((R, C), BF16), name=name, compiler_params=_cparams())(a, b)


def _exchange_chips(parts, name):
    n = len(parts)
    n_chips = N_DEV // 2

    def body(*refs):
        ins, outs = refs[:n], refs[n:2 * n]
        send_sems, recv_sems, local_sems = refs[2 * n:]
        x, y, c = lax.axis_index("x"), lax.axis_index("y"), lax.axis_index("c")
        mine = 2 * x + y
        chips = [(1 - x, y), (x, 1 - y), (1 - x, 1 - y)]
        local = [pltpu.make_async_copy(ins[a].at[mine], outs[a].at[mine], local_sems.at[a]) for a in range(n)]
        for cp in local:
            cp.start()
        for a in range(n):
            for k, (px, py) in enumerate(chips):
                pltpu.make_async_remote_copy(
                    src_ref=ins[a].at[2 * px + py], dst_ref=outs[a].at[mine], send_sem=send_sems.at[a, k],
                    recv_sem=recv_sems.at[a, k], device_id=(px, py, c), device_id_type=pl.DeviceIdType.MESH).start()
        for a in range(n):
            for k, (px, py) in enumerate(chips):
                pltpu.make_async_remote_copy(
                    src_ref=ins[a].at[2 * px + py], dst_ref=outs[a].at[2 * px + py], send_sem=send_sems.at[a, k],
                    recv_sem=recv_sems.at[a, k], device_id=(px, py, c), device_id_type=pl.DeviceIdType.MESH).wait()
        for cp in local:
            cp.wait()

    hbm = pl.BlockSpec(memory_space=pltpu.HBM)
    return pl.pallas_call(
        body, out_shape=[jax.ShapeDtypeStruct(p.shape, p.dtype) for p in parts], in_specs=[hbm] * n, out_specs=[hbm] * n,
        scratch_shapes=[pltpu.SemaphoreType.DMA((n, n_chips - 1)), pltpu.SemaphoreType.DMA((n, n_chips - 1)),
                        pltpu.SemaphoreType.DMA((n,))],
        name=name,
    )(*parts)


def _adam_rows(rows):
    for rb in (128, 64, 40, 16, 8):
        if rows % rb == 0:
            return rb
    return rows


def _adamw(stack, w, m, v, name):
    R, C = w.shape
    rb = _adam_rows(R)
    slots = stack.shape[0]

    def body(s_ref, w_ref, m_ref, v_ref, g_ref, d_ref, nm_ref, nv_ref):
        g = s_ref[0].astype(F32)
        for s in range(1, slots):
            g = g + s_ref[s].astype(F32)
        nm = ADAM_B1 * m_ref[...] + (1.0 - ADAM_B1) * g
        nv = ADAM_B2 * v_ref[...] + (1.0 - ADAM_B2) * (g * g)
        m_hat = nm / (1.0 - ADAM_B1 ** ADAM_STEP)
        v_hat = nv / (1.0 - ADAM_B2 ** ADAM_STEP)
        g_ref[...] = g
        d_ref[...] = -ADAM_LR * (m_hat / (jnp.sqrt(v_hat) + ADAM_EPS) + ADAM_WD * w_ref[...])
        nm_ref[...] = nm
        nv_ref[...] = nv

    blk = pl.BlockSpec((rb, C), lambda i: (i, 0))
    return pl.pallas_call(
        body, grid=(R // rb,),
        in_specs=[pl.BlockSpec((slots, rb, C), lambda i: (0, i, 0)), blk, blk, blk],
        out_specs=[blk] * 4, out_shape=[jax.ShapeDtypeStruct((R, C), F32)] * 4,
        name=name, compiler_params=_cparams(),
    )(stack, w, m, v)


def _pad_rows8(a):
    return jnp.concatenate([a, jnp.zeros((8 - a.shape[0], a.shape[1]), a.dtype)], axis=0) if a.shape[0] < 8 else a


def _pack_small(norm_w, qnw, knw, sinks, a_log, dt_bias, onw, extra):
    z = lambda n: jnp.zeros((1, n), F32)
    row = jnp.concatenate([norm_w, qnw, knw, sinks, a_log, dt_bias, z(80), onw, extra, z(512)], axis=1)
    return row.reshape(16, 128)


def _unpack_small(p):
    row = p.reshape(1, 2048)
    cut = lambda a, n: row[:, a:a + n]
    return (cut(0, 1024), cut(1024, 64), cut(1088, 64), cut(1152, 16), cut(1168, 16), cut(1184, 16), cut(1280, 128),
            cut(1408, 128))


def _pack_rows(w_in_a, w_in_d, w_out_a, w_out_d, meta, conv, dn_norm):
    a = jnp.concatenate([w_in_a, w_in_d], axis=1)
    b = jnp.concatenate([w_out_a, w_out_d], axis=0)
    c = jnp.concatenate([meta, conv.reshape(16, 128), _pad_rows8(dn_norm)], axis=0)
    return a, b, c


def _unpack_rows(a, b, c):
    return (a[:, :288], a[:, 288:], b[:128], b[128:], c[:16], c[16:32].reshape(4, 512), c[32:33])


def _local_step(x, front, target, w):
    xn0, q, kv, gate = _attn_in_fwd(x, front, w["attn_norm_w"], w["attn_w_in"])
    o = _attn_core_fwd(q, kv, w["attn_sinks"], w["attn_q_norm_w"], w["attn_k_norm_w"])
    h1 = _attn_out_fwd(o, gate, x, front, w["attn_w_out"])
    xn1, qkv, z, ba = _dn_in_fwd(h1, w["dn_norm_w"], w["dn_w_in"])
    qn, kn, v, bg, conv_out = _dn_conv_fwd(qkv, ba, w["dn_conv_w"], w["dn_a_log"], w["dn_dt_bias"])
    o_dn, ssave, inv = _dn_scan_fwd(qn, kn, v, bg)
    dy, og_dn, loss = _dn_out_fwd(o_dn, z, h1, target, w["dn_w_out"], w["dn_o_norm_w"])

    g = {}
    do_dn, dz, g["dn_o_norm_w"] = _dn_out_bwd(dy, o_dn, z, w["dn_w_out"], w["dn_o_norm_w"])
    g["dn_w_out"] = _wgrad(og_dn, dy, 1024, "wgrad_dn_out")
    dqn, dkn, dv, dbg = _dn_scan_bwd(do_dn, qn, kn, v, bg, ssave, inv)
    dqkv, dba, g["dn_conv_w"], g["dn_a_log"], g["dn_dt_bias"] = _dn_conv_bwd(
        dqn, dkn, dv, dbg, qkv, conv_out, ba, w["dn_conv_w"], w["dn_a_log"], w["dn_dt_bias"])
    dh1, g["dn_norm_w"] = _dn_in_bwd(dqkv, dz, dba, h1, dy, w["dn_norm_w"], w["dn_w_in"])
    g["dn_w_in"] = jnp.concatenate([_wgrad(xn1, dqkv, 1024, "wgrad_dn_qkv"), _wgrad(xn1, dz, 1024, "wgrad_dn_z"),
                                    _wgrad(xn1, dba, 32, "wgrad_dn_ba")], axis=1)
    do, dgate, g["attn_w_out"] = _attn_out_bwd(dh1, o, gate, w["attn_w_out"])
    dq, dkv, g["attn_sinks"], g["attn_q_norm_w"], g["attn_k_norm_w"] = _attn_core_bwd(
        do, q, kv, w["attn_sinks"], w["attn_q_norm_w"], w["attn_k_norm_w"])
    dh0, g["attn_norm_w"] = _attn_in_bwd(dq, dkv, dgate, x, front, dh1, w["attn_norm_w"], w["attn_w_in"])
    g["attn_w_in"] = jnp.concatenate([_wgrad(xn0, dq, 1024, "wgrad_attn_q"), _wgrad(xn0, dkv, 256, "wgrad_attn_kv"),
                                      _wgrad(xn0, dgate, 1024, "wgrad_attn_gate")], axis=1)
    return loss, dh0, g


WEIGHTS = ['meta_tokens', 'attn_norm_w', 'attn_w_in', 'attn_q_norm_w', 'attn_k_norm_w', 'attn_sinks', 'attn_w_out',
           'dn_norm_w', 'dn_w_in', 'dn_conv_w', 'dn_a_log', 'dn_dt_bias', 'dn_o_norm_w', 'dn_w_out']
SMALL = ['attn_norm_w', 'attn_q_norm_w', 'attn_k_norm_w', 'attn_sinks', 'dn_a_log', 'dn_dt_bias', 'dn_o_norm_w']


def kernel(x, meta_tokens, attn_norm_w, attn_w_in, attn_q_norm_w, attn_k_norm_w, attn_sinks, attn_w_out, dn_norm_w, dn_w_in, dn_conv_w, dn_a_log, dn_dt_bias, dn_o_norm_w, dn_w_out, loss_target, m_meta_tokens, m_attn_norm_w, m_attn_w_in, m_attn_q_norm_w, m_attn_k_norm_w, m_attn_sinks, m_attn_w_out, m_dn_norm_w, m_dn_w_in, m_dn_conv_w, m_dn_a_log, m_dn_dt_bias, m_dn_o_norm_w, m_dn_w_out, v_meta_tokens, v_attn_norm_w, v_attn_w_in, v_attn_q_norm_w, v_attn_k_norm_w, v_attn_sinks, v_attn_w_out, v_dn_norm_w, v_dn_w_in, v_dn_conv_w, v_dn_a_log, v_dn_dt_bias, v_dn_o_norm_w, v_dn_w_out):
    shard = dict(meta_tokens=meta_tokens, attn_norm_w=attn_norm_w, attn_w_in=attn_w_in[0], attn_q_norm_w=attn_q_norm_w,
                 attn_k_norm_w=attn_k_norm_w, attn_sinks=attn_sinks, attn_w_out=attn_w_out[0], dn_norm_w=dn_norm_w,
                 dn_w_in=dn_w_in[0], dn_conv_w=dn_conv_w[0], dn_a_log=dn_a_log, dn_dt_bias=dn_dt_bias,
                 dn_o_norm_w=dn_o_norm_w, dn_w_out=dn_w_out[0])
    mom_m = dict(meta_tokens=m_meta_tokens, attn_norm_w=m_attn_norm_w, attn_w_in=m_attn_w_in[0], attn_q_norm_w=m_attn_q_norm_w,
                 attn_k_norm_w=m_attn_k_norm_w, attn_sinks=m_attn_sinks, attn_w_out=m_attn_w_out[0], dn_norm_w=m_dn_norm_w,
                 dn_w_in=m_dn_w_in[0], dn_conv_w=m_dn_conv_w[0], dn_a_log=m_dn_a_log, dn_dt_bias=m_dn_dt_bias,
                 dn_o_norm_w=m_dn_o_norm_w, dn_w_out=m_dn_w_out[0])
    mom_v = dict(meta_tokens=v_meta_tokens, attn_norm_w=v_attn_norm_w, attn_w_in=v_attn_w_in[0], attn_q_norm_w=v_attn_q_norm_w,
                 attn_k_norm_w=v_attn_k_norm_w, attn_sinks=v_attn_sinks, attn_w_out=v_attn_w_out[0], dn_norm_w=v_dn_norm_w,
                 dn_w_in=v_dn_w_in[0], dn_conv_w=v_dn_conv_w[0], dn_a_log=v_dn_a_log, dn_dt_bias=v_dn_dt_bias,
                 dn_o_norm_w=v_dn_o_norm_w, dn_w_out=v_dn_w_out[0])

    def rows_of(d):
        return _pack_rows(d["attn_w_in"], d["dn_w_in"], d["attn_w_out"], d["dn_w_out"], d["meta_tokens"], d["dn_conv_w"],
                          d["dn_norm_w"])

    def small_of(d, extra):
        return _pack_small(*[d[k] for k in SMALL], extra)

    wa, wb, wc = rows_of(shard)
    ga, gb, gc = _gather_two_level([wa.astype(BF16), wb.astype(BF16), wc], "gather_weights")
    full = {k: shard[k] for k in SMALL}
    full["attn_w_in"] = ga[:, :, :288].transpose(1, 0, 2).reshape(1024, 2304)
    full["dn_w_in"] = ga[:, :, 288:].transpose(1, 0, 2).reshape(1024, 6176)
    full["attn_w_out"] = gb[:, :128].reshape(1024, 1024)
    full["dn_w_out"] = gb[:, 128:].reshape(2048, 1024)
    meta_full = gc[:, :16].transpose(1, 0, 2).reshape(N_META, 1024)
    full["dn_conv_w"] = gc[:, 16:32].reshape(N_DEV, 4, 512).transpose(1, 0, 2).reshape(4, 4096)
    full["dn_norm_w"] = gc[:, 32].reshape(1, 1024)

    seq = x.shape[1]
    front = jnp.concatenate([jnp.zeros((FRONT_PAD, D_MODEL), F32), meta_full], axis=0)
    loss, dh0, g = _local_step(x[0], front, loss_target[0], full)
    grad_x = dh0[ATTN_BLOCK:ATTN_BLOCK + seq][None]
    g["meta_tokens"] = dh0[FRONT_PAD:ATTN_BLOCK]

    pa = jnp.concatenate([g["attn_w_in"].reshape(1024, N_DEV, 288), g["dn_w_in"].reshape(1024, N_DEV, 772)],
                         axis=2).transpose(1, 0, 2)
    pb = jnp.concatenate([g["attn_w_out"].reshape(N_DEV, 128, 1024), g["dn_w_out"].reshape(N_DEV, 256, 1024)], axis=1)
    dn_norm8 = jnp.concatenate([g["dn_norm_w"].reshape(N_DEV, 1, 128), jnp.zeros((N_DEV, 7, 128), F32)], axis=1)
    pc = jnp.concatenate([g["meta_tokens"].reshape(N_META, N_DEV, 128).transpose(1, 0, 2),
                          g["dn_conv_w"].reshape(4, N_DEV, 512).transpose(1, 0, 2).reshape(N_DEV, 16, 128), dn_norm8], axis=1)
    ps = small_of(g, loss)
    c = lax.axis_index("c")
    by_core = lambda p: p.astype(BF16).reshape((N_DEV // 2, 2) + p.shape[1:]).swapaxes(0, 1)
    pa2, pb2 = by_core(pa), by_core(pb)
    ra, rb_ = _swap_with_sibling([pa2, pb2], "swap_grads")
    own = lambda p2: lax.dynamic_index_in_dim(p2, c, axis=0, keepdims=False)
    flat = lambda t: t.reshape((-1,) + t.shape[2:])
    sa = _pair_sum(flat(own(pa2)), flat(ra), "pair_sum_a").reshape(ra.shape)
    sb = _pair_sum(flat(own(pb2)), flat(rb_), "pair_sum_b").reshape(rb_.shape)
    xa, xb = _exchange_chips([sa, sb], "exchange_grads")
    xc, xs = _exchange([pc, ps], [True, False], "exchange_small")

    out = {}
    ma, mb, mc = rows_of(mom_m)
    va, vb, vc = rows_of(mom_v)
    ra = _adamw(xa, wa, ma, va, "adamw_a")
    rb = _adamw(xb, wb, mb, vb, "adamw_b")
    rc = _adamw(xc, wc, mc, vc, "adamw_c")
    zero = jnp.zeros((1, 128), F32)
    rs = _adamw(xs, small_of(shard, zero), small_of(mom_m, zero), small_of(mom_v, zero), "adamw_small")
    row_names = ["attn_w_in", "dn_w_in", "attn_w_out", "dn_w_out", "meta_tokens", "dn_conv_w", "dn_norm_w"]
    lead = {"attn_w_in", "dn_w_in", "attn_w_out", "dn_w_out", "dn_conv_w"}
    for kind in range(4):
        vals = dict(zip(row_names, _unpack_rows(ra[kind], rb[kind], rc[kind])))
        small = _unpack_small(rs[kind])
        vals.update(dict(zip(SMALL, small[:7])))
        if kind == 0:
            loss_total = small[7][0, 0]
        out[kind] = [vals[k][None] if k in lead else vals[k] for k in WEIGHTS]
    return (loss_total, grad_x, *out[0], *out[1], *out[2], *out[3])
```

```python
import functools
import math

import jax
import jax.numpy as jnp
from jax import lax
from jax.experimental import pallas as pl
from jax.experimental.pallas import tpu as pltpu

F32, BF16 = jnp.float32, jnp.bfloat16

D_MODEL = 1024
N_META = 16
NORM_EPS = 1e-6
ATTN_HEADS, ATTN_KV_HEADS, ATTN_GROUPS, ATTN_HD = 16, 2, 8, 64
ATTN_BLOCK = 128
FRONT_PAD = ATTN_BLOCK - N_META
DN_HD, DN_K_HEADS, DN_V_HEADS = 128, 8, 16
DN_CHUNK = 128
TRI_BLOCK = 64
SCAN_FWD_GROUP = 8
SCAN_BWD_GROUP = 8
DN_KEY_W, DN_VAL_W = 1024, 2048
DN_CONV_W = 2 * DN_KEY_W + DN_VAL_W
DN_CONV_K = 4
N_DEV = 8
ROW_BLOCK = 384
WGRAD_ROWS = 1376
VMEM_LIMIT = 56 * 1024 * 1024
NEG = -1e30

ADAM_LR, ADAM_B1, ADAM_B2, ADAM_EPS, ADAM_WD, ADAM_STEP = 0.001, 0.9, 0.999, 1e-08, 0.01, 10

NT = (((1,), (1,)), ((), ()))
TN = (((0,), (0,)), ((), ()))


def _cparams(sem=("arbitrary",)):
    return pltpu.CompilerParams(dimension_semantics=sem, vmem_limit_bytes=VMEM_LIMIT)


def _rms(x, w):
    return x * lax.rsqrt(jnp.mean(x * x, axis=-1, keepdims=True) + NORM_EPS) * w


def _silu(x):
    return x * jax.nn.sigmoid(x)


def _softplus(x):
    return jnp.maximum(x, 0.0) + jnp.log(1.0 + jnp.exp(-jnp.abs(x)))


NN = (((1,), (0,)), ((), ()))


def _mm(a, b, dims):
    return lax.dot_general(a.astype(BF16), b.astype(BF16), dims, preferred_element_type=F32)


@functools.partial(jax.custom_vjp, nondiff_argnums=(2,))
def _bdot_vjp(a, b, dims):
    return _mm(a, b, dims)


def _bdot_fwd(a, b, dims):
    a16, b16 = a.astype(BF16), b.astype(BF16)
    return _mm(a16, b16, dims), (a16, b16, jnp.zeros((), a.dtype), jnp.zeros((), b.dtype))


def _bdot_bwd(dims, res, g):
    a16, b16, ta, tb = res
    g16 = g.astype(BF16)
    if dims == NN:
        da, db = _mm(g16, b16, NT), _mm(a16, g16, TN)
    elif dims == NT:
        da, db = _mm(g16, b16, NN), _mm(g16, a16, TN)
    else:
        da, db = _mm(b16, g16, NT), _mm(a16, g16, NN)
    return da.astype(ta.dtype), db.astype(tb.dtype)


_bdot_vjp.defvjp(_bdot_fwd, _bdot_bwd)


def _bdot(a, b, dims=NN):
    return _bdot_vjp(a, b, dims)


def _hdot(a, b):
    return jnp.dot(a, b, preferred_element_type=F32, precision=lax.Precision.HIGHEST)


def _row_call(name, body, n_rows, rb, rows, consts, outs, accs=(), reverse=False, scratch=(), halos=()):
    n = n_rows // rb
    assert n * rb == n_rows
    idx = (lambda i: (n - 1 - i, 0)) if reverse else (lambda i: (i, 0))
    in_specs = [pl.BlockSpec((rb, a.shape[1]), idx) for a in rows]
    in_specs += [pl.BlockSpec((hr, a.shape[1]), fn) for a, hr, fn in halos]
    in_specs += [pl.BlockSpec(c.shape, functools.partial(lambda i, nd: (0,) * nd, nd=c.ndim)) for c in consts]
    out_specs = [pl.BlockSpec((rb, c), idx) for c, _ in outs]
    out_specs += [pl.BlockSpec(s, functools.partial(lambda i, nd: (0,) * nd, nd=len(s))) for s, _ in accs]
    out_shape = [jax.ShapeDtypeStruct((n_rows, c), dt) for c, dt in outs]
    out_shape += [jax.ShapeDtypeStruct(s, dt) for s, dt in accs]
    return pl.pallas_call(
        body, grid=(n,), in_specs=in_specs, out_specs=out_specs, out_shape=out_shape,
        scratch_shapes=list(scratch), name=name, compiler_params=_cparams(),
    )(*rows, *[a for a, _, _ in halos], *consts)


def _token_views(x):
    per = ROW_BLOCK // ATTN_BLOCK
    return [(x, ATTN_BLOCK, functools.partial(lambda i, k: (jnp.maximum(per * i - 1 + k, 0), 0), k=k)) for k in range(per)]


def _padded_block(i, front, views):
    first = jnp.where(i == 0, front, views[0][...]) if front is not None else views[0][...]
    return jnp.concatenate([first] + [v[...] for v in views[1:]], axis=0)


def _attn_in_fwd(x, front, norm_w, w_in):
    T = x.shape[0] + ATTN_BLOCK

    def body(xa_ref, xb_ref, xc_ref, front_ref, nw_ref, w_ref, xn_ref, q_ref, kv_ref, gate_ref):
        h = _padded_block(pl.program_id(0), front_ref[...], (xa_ref, xb_ref, xc_ref))
        xn = _rms(h, nw_ref[...]).astype(BF16)
        xn_ref[...] = xn
        q_ref[...] = jnp.dot(xn, w_ref[:, 0:1024], preferred_element_type=F32)
        kv_ref[...] = jnp.dot(xn, w_ref[:, 1024:1280], preferred_element_type=F32)
        gate_ref[...] = jnp.dot(xn, w_ref[:, 1280:2304], preferred_element_type=F32)

    return _row_call("attn_in_fwd", body, T, ROW_BLOCK, [], [front, norm_w, w_in],
                     [(1024, BF16), (1024, F32), (256, F32), (1024, F32)], halos=_token_views(x))


def _attn_bias(n, j):
    C, R = 2 * ATTN_BLOCK + N_META, ATTN_GROUPS * ATTN_BLOCK
    c = lax.broadcasted_iota(jnp.int32, (C, R), 0)
    r = lax.broadcasted_iota(jnp.int32, (C, R), 1)
    ql = r & (ATTN_BLOCK - 1)
    is_meta = c >= 2 * ATTN_BLOCK
    dist_band = ATTN_BLOCK + ql - c
    cmin = jnp.maximum(0, 2 * ATTN_BLOCK - ATTN_BLOCK * n)
    valid_band = (c >= cmin) & (dist_band >= 0) & (dist_band < ATTN_BLOCK)
    dist_meta = ATTN_BLOCK * n + ql - FRONT_PAD - (c - 2 * ATTN_BLOCK)
    valid = (is_meta & (dist_meta >= 0)) | (jnp.logical_not(is_meta) & valid_band)
    dist = jnp.minimum(jnp.where(is_meta, dist_meta, dist_band), ATTN_BLOCK).astype(F32)
    rr = lax.broadcasted_iota(jnp.int32, (1, R), 1)
    head = (rr >> 7).astype(F32) + float(ATTN_GROUPS * j + 1)
    slope = jnp.exp(head * (-0.5 * math.log(2.0)))
    return jnp.where(valid, slope * dist, -NEG)


def _attn_tables(n, refresh, bias_ref):
    @pl.when(refresh)
    def _():
        for j in range(ATTN_KV_HEADS):
            bias_ref[j] = _attn_bias(n, j)


def _attn_table_scratch():
    return [pltpu.VMEM((ATTN_KV_HEADS, 2 * ATTN_BLOCK + N_META, ATTN_GROUPS * ATTN_BLOCK), F32)]


def _attn_groups(q_t, k, v, sinkrow, qnw_col, knw, bias, late_norm=True):
    n = range(len(q_t))
    qn = [q_t[j] * (lax.rsqrt(jnp.mean(q_t[j] * q_t[j], axis=0, keepdims=True) + NORM_EPS) * (ATTN_HD ** -0.5)) * qnw_col
          for j in n]
    kn = [_rms(k[j], knw) for j in n]
    s = [_bdot(kn[j], qn[j]) - bias[j] for j in n]
    m = [lax.stop_gradient(jnp.maximum(jnp.max(s[j], axis=0, keepdims=True), sinkrow[j])) for j in n]
    e = [jnp.exp(s[j] - m[j]) for j in n]
    inv = [1.0 / (jnp.sum(e[j], axis=0, keepdims=True) + jnp.exp(sinkrow[j] - m[j])) for j in n]
    if late_norm:
        return [_bdot(v[j], e[j], TN) * inv[j] for j in n]
    return [_bdot(v[j], e[j] * inv[j], TN) for j in n]


def _sink_row(sinks_ref, j):
    rr = lax.broadcasted_iota(jnp.int32, (1, ATTN_GROUPS * ATTN_BLOCK), 1) >> 7
    row = jnp.zeros((1, ATTN_GROUPS * ATTN_BLOCK), F32)
    for hl in range(ATTN_GROUPS):
        row = jnp.where(rr == hl, sinks_ref[0, ATTN_GROUPS * j + hl], row)
    return row


def _heads_to_lanes(ref, j):
    return jnp.concatenate([ref[:, ATTN_HD * h:ATTN_HD * (h + 1)].T
                            for h in range(ATTN_GROUPS * j, ATTN_GROUPS * (j + 1))], axis=1)


def _lanes_to_heads(ref, j, x_t):
    for hl in range(ATTN_GROUPS):
        h = ATTN_GROUPS * j + hl
        ref[:, ATTN_HD * h:ATTN_HD * (h + 1)] = x_t[:, ATTN_BLOCK * hl:ATTN_BLOCK * (hl + 1)].T


def _attn_kv_tiles(kvp_ref, kvc_ref, kvm_ref, j):
    ksl = slice(ATTN_HD * j, ATTN_HD * (j + 1))
    vsl = slice(128 + ATTN_HD * j, 128 + ATTN_HD * (j + 1))
    k = jnp.concatenate([kvp_ref[:, ksl], kvc_ref[:, ksl], kvm_ref[FRONT_PAD:, ksl]], axis=0)
    v = jnp.concatenate([kvp_ref[:, vsl], kvc_ref[:, vsl], kvm_ref[FRONT_PAD:, vsl]], axis=0)
    return k, v


def _attn_core_fwd(q, kv, sinks, qnw, knw):
    T = q.shape[0]
    nb = T // ATTN_BLOCK

    def body(sinks_ref, q_ref, kvc_ref, kvp_ref, kvm_ref, qnw_ref, knw_ref, o_ref, bias_ref):
        n = pl.program_id(0)
        _attn_tables(n, n <= 2, bias_ref)
        kvh = range(ATTN_KV_HEADS)
        kv_tiles = [_attn_kv_tiles(kvp_ref, kvc_ref, kvm_ref, j) for j in kvh]
        o_t = _attn_groups([_heads_to_lanes(q_ref, j) for j in kvh], [t[0] for t in kv_tiles], [t[1] for t in kv_tiles],
                           [_sink_row(sinks_ref, j) for j in kvh], qnw_ref[...], knw_ref[...], [bias_ref[j] for j in kvh])
        for j in kvh:
            _lanes_to_heads(o_ref, j, o_t[j])

    return pl.pallas_call(
        body, grid=(nb,),
        in_specs=[pl.BlockSpec(memory_space=pltpu.SMEM),
                  pl.BlockSpec((ATTN_BLOCK, 1024), lambda i: (i, 0)),
                  pl.BlockSpec((ATTN_BLOCK, 256), lambda i: (i, 0)),
                  pl.BlockSpec((ATTN_BLOCK, 256), lambda i: (jnp.maximum(i - 1, 0), 0)),
                  pl.BlockSpec((ATTN_BLOCK, 256), lambda i: (0, 0)),
                  pl.BlockSpec((ATTN_HD, 1), lambda i: (0, 0)),
                  pl.BlockSpec((1, ATTN_HD), lambda i: (0, 0))],
        out_specs=pl.BlockSpec((ATTN_BLOCK, 1024), lambda i: (i, 0)),
        out_shape=jax.ShapeDtypeStruct((T, 1024), F32),
        scratch_shapes=_attn_table_scratch(),
        name="attn_core_fwd", compiler_params=_cparams(),
    )(sinks, q, kv, kv, kv, qnw.reshape(ATTN_HD, 1), knw)


def _attn_out_fwd(o, gate, x, front, w_out):
    T = o.shape[0]

    def body(o_ref, g_ref, xa_ref, xb_ref, xc_ref, front_ref, w_ref, h1_ref):
        h = _padded_block(pl.program_id(0), front_ref[...], (xa_ref, xb_ref, xc_ref))
        og = o_ref[...] * _silu(g_ref[...])
        h1_ref[...] = h + _bdot(og, w_ref[...])

    return _row_call("attn_out_fwd", body, T, ROW_BLOCK, [o, gate], [front, w_out], [(1024, F32)], halos=_token_views(x))[0]


def _wgrad(xn, du, cg, name):
    T, kdim = xn.shape
    cdim = du.shape[1]
    rows = WGRAD_ROWS if T % WGRAD_ROWS == 0 else ROW_BLOCK
    nr, nc = T // rows, cdim // cg
    assert nc * cg == cdim

    def body(x_ref, du_ref, dw_ref):
        @pl.when(pl.program_id(1) == 0)
        def _():
            dw_ref[...] = jnp.zeros_like(dw_ref)
        dw_ref[...] += _bdot(x_ref[...], du_ref[...], TN)

    return pl.pallas_call(
        body, grid=(nc, nr),
        in_specs=[pl.BlockSpec((rows, kdim), lambda j, i: (i, 0)),
                  pl.BlockSpec((rows, cg), lambda j, i: (i, j))],
        out_specs=pl.BlockSpec((kdim, cg), lambda j, i: (0, j)),
        out_shape=jax.ShapeDtypeStruct((kdim, cdim), F32),
        name=name, compiler_params=_cparams(("arbitrary", "arbitrary")),
    )(xn, du)


def _attn_out_bwd(dh1, o, gate, w_out):
    T = o.shape[0]

    def body(dh_ref, o_ref, g_ref, w_ref, do_ref, dg_ref, dw_ref):
        @pl.when(pl.program_id(0) == 0)
        def _():
            dw_ref[...] = jnp.zeros_like(dw_ref)
        dh = dh_ref[...]
        dog = _bdot(dh, w_ref[...], NT)
        og, vjp = jax.vjp(lambda o_, g_: o_ * _silu(g_), o_ref[...], g_ref[...])
        do, dg = vjp(dog)
        do_ref[...] = do
        dg_ref[...] = dg
        dw_ref[...] += _bdot(og, dh, TN)

    return _row_call("attn_out_bwd", body, T, ROW_BLOCK, [dh1, o, gate], [w_out],
                     [(1024, F32), (1024, F32)], [((1024, 1024), F32)])


def _attn_core_bwd(do, q, kv, sinks, qnw, knw):
    T = q.shape[0]
    nb = T // ATTN_BLOCK
    rev = lambda i: nb - 1 - i

    def body(sinks_ref, do_ref, q_ref, kvc_ref, kvp_ref, kvm_ref, qnw_ref, knw_ref,
             dq_ref, dkv_ref, dsinks_ref, dqnw_ref, dknw_ref, carry_ref, meta_ref, bias_ref):
        step = pl.program_id(0)
        n = rev(step)
        _attn_tables(n, (step == 0) | (n <= 1), bias_ref)

        @pl.when(step == 0)
        def _():
            carry_ref[...] = jnp.zeros_like(carry_ref)
            meta_ref[...] = jnp.zeros_like(meta_ref)
            dsinks_ref[...] = jnp.zeros_like(dsinks_ref)
            dqnw_ref[...] = jnp.zeros_like(dqnw_ref)
            dknw_ref[...] = jnp.zeros_like(dknw_ref)

        lane16 = lax.broadcasted_iota(jnp.int32, (1, ATTN_HEADS), 1)
        dsinks = jnp.zeros((1, ATTN_HEADS), F32)
        kvh = range(ATTN_KV_HEADS)
        kv_tiles = [_attn_kv_tiles(kvp_ref, kvc_ref, kvm_ref, j) for j in kvh]
        fn = functools.partial(_attn_groups, bias=[bias_ref[j] for j in kvh], late_norm=False)
        _, vjp = jax.vjp(fn, [_heads_to_lanes(q_ref, j) for j in kvh], [t[0] for t in kv_tiles], [t[1] for t in kv_tiles],
                         [_sink_row(sinks_ref, j) for j in kvh], qnw_ref[...], knw_ref[...])
        dq_t, dks, dvs, dsr, dqn, dkn = vjp([_heads_to_lanes(do_ref, j) for j in kvh])
        dqnw_ref[...] += dqn
        dknw_ref[...] += dkn
        for j in kvh:
            _lanes_to_heads(dq_ref, j, dq_t[j])
            for hl in range(ATTN_GROUPS):
                dsinks = dsinks + jnp.where(lane16 == ATTN_GROUPS * j + hl,
                                            jnp.sum(dsr[j][:, ATTN_BLOCK * hl:ATTN_BLOCK * (hl + 1)]), 0.0)
            ksl = slice(ATTN_HD * j, ATTN_HD * (j + 1))
            vsl = slice(128 + ATTN_HD * j, 128 + ATTN_HD * (j + 1))
            for sl, d in ((ksl, dks[j]), (vsl, dvs[j])):
                dkv_ref[:, sl] = d[ATTN_BLOCK:2 * ATTN_BLOCK, :] + carry_ref[:, sl]
                carry_ref[:, sl] = d[0:ATTN_BLOCK, :]
                meta_ref[:, sl] += d[2 * ATTN_BLOCK:, :]
        dsinks_ref[...] += dsinks

        @pl.when(n == 0)
        def _():
            dkv_ref[FRONT_PAD:, :] += meta_ref[...]

    dq, dkv, dsinks, dqnw, dknw = pl.pallas_call(
        body, grid=(nb,),
        in_specs=[pl.BlockSpec(memory_space=pltpu.SMEM),
                  pl.BlockSpec((ATTN_BLOCK, 1024), lambda i: (rev(i), 0)),
                  pl.BlockSpec((ATTN_BLOCK, 1024), lambda i: (rev(i), 0)),
                  pl.BlockSpec((ATTN_BLOCK, 256), lambda i: (rev(i), 0)),
                  pl.BlockSpec((ATTN_BLOCK, 256), lambda i: (jnp.maximum(rev(i) - 1, 0), 0)),
                  pl.BlockSpec((ATTN_BLOCK, 256), lambda i: (0, 0)),
                  pl.BlockSpec((ATTN_HD, 1), lambda i: (0, 0)),
                  pl.BlockSpec((1, ATTN_HD), lambda i: (0, 0))],
        out_specs=[pl.BlockSpec((ATTN_BLOCK, 1024), lambda i: (rev(i), 0)),
                   pl.BlockSpec((ATTN_BLOCK, 256), lambda i: (rev(i), 0)),
                   pl.BlockSpec((1, ATTN_HEADS), lambda i: (0, 0)),
                   pl.BlockSpec((ATTN_HD, 1), lambda i: (0, 0)),
                   pl.BlockSpec((1, ATTN_HD), lambda i: (0, 0))],
        out_shape=[jax.ShapeDtypeStruct((T, 1024), F32), jax.ShapeDtypeStruct((T, 256), F32),
                   jax.ShapeDtypeStruct((1, ATTN_HEADS), F32), jax.ShapeDtypeStruct((ATTN_HD, 1), F32),
                   jax.ShapeDtypeStruct((1, ATTN_HD), F32)],
        scratch_shapes=[pltpu.VMEM((ATTN_BLOCK, 256), F32), pltpu.VMEM((N_META, 256), F32)] + _attn_table_scratch(),
        name="attn_core_bwd", compiler_params=_cparams(),
    )(sinks, do, q, kv, kv, kv, qnw.reshape(ATTN_HD, 1), knw)
    return dq, dkv, dsinks, dqnw.reshape(1, ATTN_HD), dknw


def _attn_in_bwd(dq, dkv, dgate, x, front, dh1, norm_w, w_in):
    T = dq.shape[0]

    def body(dq_ref, dkv_ref, dg_ref, dh1_ref, xa_ref, xb_ref, xc_ref, front_ref, nw_ref, w_ref, dh0_ref, dnw_ref):
        @pl.when(pl.program_id(0) == 0)
        def _():
            dnw_ref[...] = jnp.zeros_like(dnw_ref)
        h = _padded_block(pl.program_id(0), front_ref[...], (xa_ref, xb_ref, xc_ref))
        dxn = (_bdot(dq_ref[...], w_ref[:, 0:1024], NT) + _bdot(dkv_ref[...], w_ref[:, 1024:1280], NT)
               + _bdot(dg_ref[...], w_ref[:, 1280:2304], NT))
        _, vjp = jax.vjp(_rms, h, nw_ref[...])
        dh, dnw = vjp(dxn)
        dh0_ref[...] = dh1_ref[...] + dh
        dnw_ref[...] += dnw

    return _row_call("attn_in_bwd", body, T, ROW_BLOCK, [dq, dkv, dgate, dh1], [front, norm_w, w_in],
                     [(1024, F32)], [((1, 1024), F32)], halos=_token_views(x))


def _dn_in_fwd(h1, norm_w, w_in):
    T = h1.shape[0]

    def body(h_ref, nw_ref, w_ref, xn_ref, qkv_ref, z_ref, ba_ref):
        xn = _rms(h_ref[...], nw_ref[...]).astype(BF16)
        xn_ref[...] = xn
        qkv_ref[...] = jnp.dot(xn, w_ref[:, 0:4096], preferred_element_type=F32)
        z_ref[...] = jnp.dot(xn, w_ref[:, 4096:6144], preferred_element_type=F32)
        ba_ref[...] = jnp.dot(xn, w_ref[:, 6144:6176], preferred_element_type=F32)

    return _row_call("dn_in_fwd", body, T, ROW_BLOCK, [h1], [norm_w, w_in],
                     [(1024, BF16), (4096, F32), (2048, F32), (32, F32)])


def _shift_down(cur, prev8, s):
    i8 = lax.broadcasted_iota(jnp.int32, (8, cur.shape[1]), 0)
    r = pltpu.roll(cur, s, 0)
    head = jnp.where(i8 < s, pltpu.roll(prev8, s, 0), r[0:8])
    return jnp.concatenate([head, r[8:]], axis=0)


def _shift_up(cur, next8, s):
    n = cur.shape[0]
    i8 = lax.broadcasted_iota(jnp.int32, (8, cur.shape[1]), 0)
    r = pltpu.roll(cur, n - s, 0)
    tail = jnp.where(i8 >= 8 - s, pltpu.roll(next8, 8 - s, 0), r[n - 8:])
    return jnp.concatenate([r[:n - 8], tail], axis=0)


def _conv_taps(cur, prev8):
    return [cur] + [_shift_down(cur, prev8, s) for s in range(1, DN_CONV_K)]


def _conv_tile(taps, w):
    out = w[3:4, :] * taps[0]
    for s in range(1, DN_CONV_K):
        out = out + w[3 - s:4 - s, :] * taps[s]
    return out


def _l2n(a, scale):
    return a * (lax.rsqrt(jnp.sum(a * a, axis=-1, keepdims=True) + NORM_EPS) * scale)


def _dn_post_tile(c, t):
    a = _silu(c)
    if t < DN_K_HEADS:
        return _l2n(a, DN_HD ** -0.5)
    if t < 2 * DN_K_HEADS:
        return _l2n(a, 1.0)
    return a


def _dn_beta_g(ba, a_log, dt_bias, live):
    beta = jax.nn.sigmoid(ba[:, 0:DN_V_HEADS]) * live
    g = -jnp.exp(a_log) * _softplus(ba[:, DN_V_HEADS:] + dt_bias) * live
    return beta, g


def _live_rows(i, rb):
    rows = i * rb + lax.broadcasted_iota(jnp.int32, (rb, 1), 0)
    return (rows >= FRONT_PAD).astype(F32)


def _halo_spec_args(x, rb):
    per = rb // 8
    return (x, 8, lambda i: (jnp.maximum(i * per - 1, 0), 0))


def _dn_conv_fwd(qkv, ba, conv_w, a_log, dt_bias):
    T = qkv.shape[0]

    def body(x_ref, ba_ref, halo_ref, cw_ref, al_ref, dtb_ref, q_ref, k_ref, v_ref, bg_ref, c_ref):
        i = pl.program_id(0)
        first = (i > 0).astype(F32)
        for t in range(DN_CONV_W // 128):
            cols = slice(128 * t, 128 * (t + 1))
            c = _conv_tile(_conv_taps(x_ref[:, cols], halo_ref[:, cols] * first), cw_ref[:, cols])
            c_ref[:, cols] = c
            out = _dn_post_tile(c, t)
            if t < DN_K_HEADS:
                q_ref[:, cols] = out
            elif t < 2 * DN_K_HEADS:
                k_ref[:, 128 * (t - 8):128 * (t - 7)] = out
            else:
                v_ref[:, 128 * (t - 16):128 * (t - 15)] = out
        beta, g = _dn_beta_g(ba_ref[...], al_ref[...], dtb_ref[...], _live_rows(i, ROW_BLOCK))
        bg_ref[:, 0:DN_V_HEADS] = beta
        bg_ref[:, DN_V_HEADS:] = g

    return _row_call("dn_conv_fwd", body, T, ROW_BLOCK, [qkv, ba], [conv_w, a_log, dt_bias],
                     [(1024, F32), (1024, F32), (2048, F32), (32, F32), (4096, F32)], halos=[_halo_spec_args(qkv, ROW_BLOCK)])


def _chunk_masks():
    r = lax.broadcasted_iota(jnp.int32, (DN_CHUNK, DN_CHUNK), 0)
    c = lax.broadcasted_iota(jnp.int32, (DN_CHUNK, DN_CHUNK), 1)
    return r >= c, r > c, r == c, r <= c


def _tri_inv_block(x):
    B = TRI_BLOCK
    n = range(len(x))
    r_, c_ = lax.broadcasted_iota(jnp.int32, (B, B), 0), lax.broadcasted_iota(jnp.int32, (B, B), 1)
    ainv = [jnp.where(r_ == c_, 1.0, 0.0) + x[h] for h in n]
    p = [_bdot(x[h], x[h]) for h in n]
    for _ in range(B.bit_length() - 3):
        r = [_bdot(jnp.concatenate([p[h], ainv[h]], axis=0), p[h]) for h in n]
        ainv = [ainv[h] + r[h][B:] for h in n]
        p = [r[h][:B] for h in n]
    return [ainv[h] + _bdot(ainv[h], p[h]) for h in n]


def _tri_inv(x):
    B = TRI_BLOCK
    assert DN_CHUNK == 2 * B
    n = len(x)
    diag = _tri_inv_block([x[h][:B, :B] for h in range(n)] + [x[h][B:, B:] for h in range(n)])
    a11, a22 = diag[:n], diag[n:]
    a21 = [_bdot(_bdot(a22[h], x[h][B:, :B]), a11[h]) for h in range(n)]
    zero = jnp.zeros((B, B), F32)
    return [jnp.concatenate([jnp.concatenate([a11[h], zero], axis=1), jnp.concatenate([a21[h], a22[h]], axis=1)], axis=0)
            for h in range(n)]


@jax.custom_vjp
def _tri_inv_known(x, a):
    return a


def _tri_inv_known_fwd(x, a):
    return a, a


def _tri_inv_known_bwd(a, da):
    return [_bdot(_bdot(a[h], da[h], TN), a[h], NT) for h in range(len(a))], [jnp.zeros_like(t) for t in a]


_tri_inv_known.defvjp(_tri_inv_known_fwd, _tri_inv_known_bwd)


def _dn_chunk_step(S, q, k, v, beta, g, masks, known_inv=None, with_inv=False):
    causal, strict, eye, upper = masks
    C, W = DN_CHUNK, DN_HD
    heads = range(len(v))
    k_t = [k[j].T for j in range(len(k))]
    qk_kk = [_bdot(jnp.concatenate([q[j], k[j]], axis=0), k_t[j]) for j in range(len(q))]
    g_b = [jnp.broadcast_to(g[h], (C, C)) for h in heads]
    beta_b = [jnp.broadcast_to(beta[h], (C, W)) for h in heads]
    g_row = [jnp.sum(jnp.where(eye, g_b[h], 0.0), axis=0, keepdims=True) for h in heads]
    gc_col = [jnp.sum(jnp.where(causal, g_row[h], 0.0), axis=1, keepdims=True) for h in heads]
    gc_row = [jnp.sum(jnp.where(upper, g_b[h], 0.0), axis=0, keepdims=True) for h in heads]
    g_last = [jnp.sum(g_row[h], axis=1, keepdims=True) for h in heads]
    gc_b = [jnp.broadcast_to(gc_col[h], (C, W)) for h in heads]
    decay = [jnp.exp(jnp.where(causal, gc_b[h][:, :C] - gc_row[h], NEG)) for h in heads]
    eg_b = [jnp.exp(gc_b[h]) for h in heads]
    x = [jnp.where(strict, qk_kk[h // 2][C:] * beta_b[h][:, :C] * decay[h], 0.0) * -1.0 for h in heads]
    ainv = _tri_inv(x) if known_inv is None else _tri_inv_known(x, known_inv)
    uw = [_bdot(ainv[h], jnp.concatenate([v[h] * beta_b[h], k[h // 2] * (beta_b[h] * eg_b[h])], axis=1)) for h in heads]
    ws_qs = [_bdot(jnp.concatenate([uw[h][:, W:], q[h // 2] * eg_b[h]], axis=0), S[h]) for h in heads]
    v_new = [uw[h][:, :W] - ws_qs[h][:C] for h in heads]
    o = [ws_qs[h][C:] + _bdot(qk_kk[h // 2][:C] * decay[h], v_new[h]) for h in heads]
    s_new = [S[h] * jnp.exp(g_last[h]) + _bdot(k_t[h // 2] * jnp.exp(g_last[h] - gc_row[h]), v_new[h]) for h in heads]
    return (s_new, o, ainv) if with_inv else (s_new, o)


def _dn_chunk_tiles(q_ref, k_ref, v_ref, bg_ref):
    q = [q_ref[:, 128 * j:128 * (j + 1)] for j in range(DN_K_HEADS)]
    k = [k_ref[:, 128 * j:128 * (j + 1)] for j in range(DN_K_HEADS)]
    v = [v_ref[:, 128 * h:128 * (h + 1)] for h in range(DN_V_HEADS)]
    beta = [bg_ref[:, h:h + 1] for h in range(DN_V_HEADS)]
    g = [bg_ref[:, DN_V_HEADS + h:DN_V_HEADS + h + 1] for h in range(DN_V_HEADS)]
    return q, k, v, beta, g


def _dn_scan_fwd(qn, kn, v, bg):
    T = qn.shape[0]
    nc = T // DN_CHUNK

    def body(q_ref, k_ref, v_ref, bg_ref, o_ref, ssave_ref, inv_ref, s_ref):
        @pl.when(pl.program_id(0) == 0)
        def _():
            s_ref[...] = jnp.zeros_like(s_ref)
        masks = _chunk_masks()
        q, k, v, beta, g = _dn_chunk_tiles(q_ref, k_ref, v_ref, bg_ref)
        for first in range(0, DN_V_HEADS, SCAN_FWD_GROUP):
            heads = range(first, first + SCAN_FWD_GROUP)
            pairs = slice(first // 2, (first + SCAN_FWD_GROUP) // 2)
            hs = slice(first, first + SCAN_FWD_GROUP)
            s_old = [s_ref[h] for h in heads]
            for i, h in enumerate(heads):
                ssave_ref[0, h] = s_old[i]
            s_new, o, ainv = _dn_chunk_step(s_old, q[pairs], k[pairs], v[hs], beta[hs], g[hs], masks, with_inv=True)
            for i, h in enumerate(heads):
                o_ref[:, 128 * h:128 * (h + 1)] = o[i]
                inv_ref[0, h] = ainv[i].astype(BF16)
                s_ref[h] = s_new[i]

    return pl.pallas_call(
        body, grid=(nc,),
        in_specs=[pl.BlockSpec((DN_CHUNK, 1024), lambda i: (i, 0)),
                  pl.BlockSpec((DN_CHUNK, 1024), lambda i: (i, 0)),
                  pl.BlockSpec((DN_CHUNK, 2048), lambda i: (i, 0)),
                  pl.BlockSpec((DN_CHUNK, 32), lambda i: (i, 0))],
        out_specs=[pl.BlockSpec((DN_CHUNK, 2048), lambda i: (i, 0)),
                   pl.BlockSpec((1, DN_V_HEADS, DN_HD, DN_HD), lambda i: (i, 0, 0, 0)),
                   pl.BlockSpec((1, DN_V_HEADS, DN_CHUNK, DN_CHUNK), lambda i: (i, 0, 0, 0))],
        out_shape=[jax.ShapeDtypeStruct((T, 2048), F32),
                   jax.ShapeDtypeStruct((nc, DN_V_HEADS, DN_HD, DN_HD), F32),
                   jax.ShapeDtypeStruct((nc, DN_V_HEADS, DN_CHUNK, DN_CHUNK), BF16)],
        scratch_shapes=[pltpu.VMEM((DN_V_HEADS, DN_HD, DN_HD), F32)],
        name="dn_scan_fwd", compiler_params=_cparams(),
    )(qn, kn, v, bg)


def _dn_gate_tile(o, z, onw):
    return _rms(o, onw) * _silu(z)


def _dn_out_fwd(o, z, h1, target, w_out, onw):
    T = o.shape[0]

    def body(o_ref, z_ref, h_ref, ta_ref, tb_ref, tc_ref, w_ref, onw_ref, dy_ref, og_ref, loss_ref):
        i = pl.program_id(0)

        @pl.when(i == 0)
        def _():
            loss_ref[...] = jnp.zeros_like(loss_ref)
        for h in range(DN_V_HEADS):
            cols = slice(128 * h, 128 * (h + 1))
            og_ref[:, cols] = _dn_gate_tile(o_ref[:, cols], z_ref[:, cols], onw_ref[...]).astype(BF16)
        y = h_ref[...] + jnp.dot(og_ref[...], w_ref[...], preferred_element_type=F32)
        rows = i * ROW_BLOCK + lax.broadcasted_iota(jnp.int32, (ROW_BLOCK, 1), 0)
        diff = jnp.where(rows >= FRONT_PAD + N_META, y - _padded_block(i, None, (ta_ref, tb_ref, tc_ref)), 0.0)
        dy_ref[...] = diff * (1.0 / D_MODEL)
        loss_ref[...] += jnp.sum(diff * diff) * (0.5 / D_MODEL)

    return _row_call("dn_out_fwd", body, T, ROW_BLOCK, [o, z, h1], [w_out, onw],
                     [(1024, F32), (2048, BF16)], [((1, 128), F32)], halos=_token_views(target))


def _dn_out_bwd(dy, o, z, w_out, onw):
    T = o.shape[0]

    def body(dy_ref, o_ref, z_ref, w_ref, onw_ref, do_ref, dz_ref, donw_ref, dog_ref):
        @pl.when(pl.program_id(0) == 0)
        def _():
            donw_ref[...] = jnp.zeros_like(donw_ref)
        dy = dy_ref[...].astype(BF16)
        donw = jnp.zeros((1, DN_HD), F32)
        for half in range(2):
            hcols = slice(1024 * half, 1024 * (half + 1))
            dog_ref[:, hcols] = lax.dot_general(dy, w_ref[hcols, :], NT, preferred_element_type=F32)
        for h in range(DN_V_HEADS):
            cols = slice(128 * h, 128 * (h + 1))
            _, vjp = jax.vjp(_dn_gate_tile, o_ref[:, cols], z_ref[:, cols], onw_ref[...])
            do, dz, dn = vjp(dog_ref[:, cols])
            do_ref[:, cols] = do
            dz_ref[:, cols] = dz
            donw = donw + dn
        donw_ref[...] += donw

    return _row_call("dn_out_bwd", body, T, ROW_BLOCK, [dy, o, z], [w_out, onw],
                     [(2048, F32), (2048, F32)], [((1, DN_HD), F32)], scratch=[pltpu.VMEM((ROW_BLOCK, 2048), F32)])


def _dn_scan_bwd(do, qn, kn, v, bg, ssave, inv):
    T = qn.shape[0]
    nc = T // DN_CHUNK
    rev = lambda i: nc - 1 - i

    def body(do_ref, q_ref, k_ref, v_ref, bg_ref, ss_ref, inv_ref, dq_ref, dk_ref, dv_ref, dbg_ref, ds_ref):
        @pl.when(pl.program_id(0) == 0)
        def _():
            ds_ref[...] = jnp.zeros_like(ds_ref)
        lane32 = lax.broadcasted_iota(jnp.int32, (1, 2 * DN_V_HEADS), 1)
        masks = _chunk_masks()
        q, k, v, beta, g = _dn_chunk_tiles(q_ref, k_ref, v_ref, bg_ref)
        dbg = jnp.zeros((DN_CHUNK, 2 * DN_V_HEADS), F32)
        for first in range(0, DN_V_HEADS, SCAN_BWD_GROUP):
            heads = range(first, first + SCAN_BWD_GROUP)
            pairs = slice(first // 2, (first + SCAN_BWD_GROUP) // 2)
            hs = slice(first, first + SCAN_BWD_GROUP)
            fn = functools.partial(_dn_chunk_step, masks=masks, known_inv=[inv_ref[0, h].astype(F32) for h in heads])
            _, vjp = jax.vjp(fn, [ss_ref[0, h] for h in heads], q[pairs], k[pairs], v[hs], beta[hs], g[hs])
            ds, dq, dk, dv, dbeta, dg = vjp(([ds_ref[h] for h in heads], [do_ref[:, 128 * h:128 * (h + 1)] for h in heads]))
            for i, h in enumerate(heads):
                ds_ref[h] = ds[i]
                dv_ref[:, 128 * h:128 * (h + 1)] = dv[i]
                dbg = dbg + jnp.where(lane32 == h, dbeta[i], 0.0) + jnp.where(lane32 == DN_V_HEADS + h, dg[i], 0.0)
            for i, j in enumerate(range(first // 2, (first + SCAN_BWD_GROUP) // 2)):
                dq_ref[:, 128 * j:128 * (j + 1)] = dq[i]
                dk_ref[:, 128 * j:128 * (j + 1)] = dk[i]
        dbg_ref[...] = dbg

    return pl.pallas_call(
        body, grid=(nc,),
        in_specs=[pl.BlockSpec((DN_CHUNK, 2048), lambda i: (rev(i), 0)),
                  pl.BlockSpec((DN_CHUNK, 1024), lambda i: (rev(i), 0)),
                  pl.BlockSpec((DN_CHUNK, 1024), lambda i: (rev(i), 0)),
                  pl.BlockSpec((DN_CHUNK, 2048), lambda i: (rev(i), 0)),
                  pl.BlockSpec((DN_CHUNK, 32), lambda i: (rev(i), 0)),
                  pl.BlockSpec((1, DN_V_HEADS, DN_HD, DN_HD), lambda i: (rev(i), 0, 0, 0)),
                  pl.BlockSpec((1, DN_V_HEADS, DN_CHUNK, DN_CHUNK), lambda i: (rev(i), 0, 0, 0))],
        out_specs=[pl.BlockSpec((DN_CHUNK, 1024), lambda i: (rev(i), 0)),
                   pl.BlockSpec((DN_CHUNK, 1024), lambda i: (rev(i), 0)),
                   pl.BlockSpec((DN_CHUNK, 2048), lambda i: (rev(i), 0)),
                   pl.BlockSpec((DN_CHUNK, 32), lambda i: (rev(i), 0))],
        out_shape=[jax.ShapeDtypeStruct((T, 1024), F32), jax.ShapeDtypeStruct((T, 1024), F32),
                   jax.ShapeDtypeStruct((T, 2048), F32), jax.ShapeDtypeStruct((T, 32), F32)],
        scratch_shapes=[pltpu.VMEM((DN_V_HEADS, DN_HD, DN_HD), F32)],
        name="dn_scan_bwd", compiler_params=_cparams(),
    )(do, qn, kn, v, bg, ssave, inv)


def _dn_conv_bwd(dqn, dkn, dv, dbg, qkv, conv_out, ba, conv_w, a_log, dt_bias):
    T = qkv.shape[0]
    rb = ROW_BLOCK // 2
    nr = T // rb

    def body(dq_ref, dk_ref, dv_ref, dbg_ref, x_ref, c_ref, ba_ref, cw_ref, al_ref, dtb_ref,
             dx_ref, dba_ref, dcw_ref, dal_ref, ddtb_ref, carry_ref):
        step = pl.program_id(0)
        i = nr - 1 - step

        @pl.when(step == 0)
        def _():
            carry_ref[...] = jnp.zeros_like(carry_ref)
            dcw_ref[...] = jnp.zeros_like(dcw_ref)
            dal_ref[...] = jnp.zeros_like(dal_ref)
            ddtb_ref[...] = jnp.zeros_like(ddtb_ref)
        for t in range(DN_CONV_W // 128):
            cols = slice(128 * t, 128 * (t + 1))
            w, x = cw_ref[:, cols], x_ref[:, cols]
            if t < DN_K_HEADS:
                dout = dq_ref[:, cols]
            elif t < 2 * DN_K_HEADS:
                dout = dk_ref[:, 128 * (t - 8):128 * (t - 7)]
            else:
                dout = dv_ref[:, 128 * (t - 16):128 * (t - 15)]
            _, vjp = jax.vjp(functools.partial(_dn_post_tile, t=t), c_ref[:, cols])
            (dc,) = vjp(dout)
            nxt = carry_ref[:, cols]
            dx = w[3:4, :] * dc
            dcw_ref[3:4, cols] += jnp.sum(dc * x, axis=0, keepdims=True)
            for s in range(1, DN_CONV_K):
                up = _shift_up(dc, nxt, s)
                dx = dx + w[3 - s:4 - s, :] * up
                dcw_ref[3 - s:4 - s, cols] += jnp.sum(up * x, axis=0, keepdims=True)
            dx_ref[:, cols] = dx
            carry_ref[:, cols] = dc[0:8, :]
        fn = functools.partial(_dn_beta_g, live=_live_rows(i, rb))
        _, vjp = jax.vjp(fn, ba_ref[...], al_ref[...], dtb_ref[...])
        dba, dal, ddtb = vjp((dbg_ref[:, 0:DN_V_HEADS], dbg_ref[:, DN_V_HEADS:]))
        dba_ref[...] = dba
        dal_ref[...] += dal
        ddtb_ref[...] += ddtb

    return _row_call("dn_conv_bwd", body, T, rb, [dqn, dkn, dv, dbg, qkv, conv_out, ba], [conv_w, a_log, dt_bias],
                     [(4096, F32), (32, F32)], [((DN_CONV_K, 4096), F32), ((1, DN_V_HEADS), F32), ((1, DN_V_HEADS), F32)],
                     reverse=True, scratch=[pltpu.VMEM((8, 4096), F32)])


def _dn_in_bwd(dqkv, dz, dba, h1, dy, norm_w, w_in):
    T = h1.shape[0]

    def body(dqkv_ref, dz_ref, dba_ref, h_ref, dy_ref, nw_ref, w_ref, dh_ref, dnw_ref):
        @pl.when(pl.program_id(0) == 0)
        def _():
            dnw_ref[...] = jnp.zeros_like(dnw_ref)
        dxn = (_bdot(dqkv_ref[...], w_ref[:, 0:4096], NT) + _bdot(dz_ref[...], w_ref[:, 4096:6144], NT)
               + _bdot(dba_ref[...], w_ref[:, 6144:6176], NT))
        _, vjp = jax.vjp(_rms, h_ref[...], nw_ref[...])
        dh, dnw = vjp(dxn)
        dh_ref[...] = (dy_ref[...] + dh) * _live_rows(pl.program_id(0), ROW_BLOCK)
        dnw_ref[...] += dnw

    return _row_call("dn_in_bwd", body, T, ROW_BLOCK, [dqkv, dz, dba, h1, dy], [norm_w, w_in],
                     [(1024, F32)], [((1, 1024), F32)])


def _exchange(parts, scatter, name):
    n = len(parts)
    out_shape = [jax.ShapeDtypeStruct(p.shape if sc else (N_DEV,) + p.shape, p.dtype) for p, sc in zip(parts, scatter)]

    def body(*refs):
        ins, outs = refs[:n], refs[n:2 * n]
        send_sems, recv_sems, local_sems = refs[2 * n:]
        x, y, c = lax.axis_index("x"), lax.axis_index("y"), lax.axis_index("c")
        me = 4 * x + 2 * y + c
        peers = []
        for k in range(1, N_DEV):
            px = 1 - x if k & 4 else x
            py = 1 - y if k & 2 else y
            pc = 1 - c if k & 1 else c
            peers.append(((px, py, pc), 4 * px + 2 * py + pc))

        def src(a, idx):
            return ins[a].at[idx] if scatter[a] else ins[a]

        local = [pltpu.make_async_copy(src(a, me), outs[a].at[me], local_sems.at[a]) for a in range(n)]
        for cp in local:
            cp.start()
        for a in range(n):
            for k, (dev, idx) in enumerate(peers):
                pltpu.make_async_remote_copy(
                    src_ref=src(a, idx), dst_ref=outs[a].at[me], send_sem=send_sems.at[a, k], recv_sem=recv_sems.at[a, k],
                    device_id=dev, device_id_type=pl.DeviceIdType.MESH).start()
        for a in range(n):
            for k, (dev, idx) in enumerate(peers):
                pltpu.make_async_remote_copy(
                    src_ref=src(a, idx), dst_ref=outs[a].at[idx], send_sem=send_sems.at[a, k], recv_sem=recv_sems.at[a, k],
                    device_id=dev, device_id_type=pl.DeviceIdType.MESH).wait()
        for cp in local:
            cp.wait()

    hbm = pl.BlockSpec(memory_space=pltpu.HBM)
    return pl.pallas_call(
        body, out_shape=out_shape, in_specs=[hbm] * n, out_specs=[hbm] * n,
        scratch_shapes=[pltpu.SemaphoreType.DMA((n, N_DEV - 1)), pltpu.SemaphoreType.DMA((n, N_DEV - 1)),
                        pltpu.SemaphoreType.DMA((n,))],
        name=name,
    )(*parts)


def _gather_two_level(parts, name):
    n = len(parts)
    out_shape = [jax.ShapeDtypeStruct((N_DEV,) + p.shape, p.dtype) for p in parts]

    def body(*refs):
        ins, outs = refs[:n], refs[n:2 * n]
        send_sems, recv_sems, local_sems = refs[2 * n:]
        x, y, c = lax.axis_index("x"), lax.axis_index("y"), lax.axis_index("c")
        idx = lambda px, py, pc: 4 * px + 2 * py + pc
        me, sibling = (x, y, c), (x, y, 1 - c)
        chips = [(1 - x, y), (x, 1 - y), (1 - x, 1 - y)]

        def copy(a, k, block, to, src=None):
            slot = outs[a].at[idx(*block)]
            return pltpu.make_async_remote_copy(
                src_ref=slot if src is None else src, dst_ref=slot, send_sem=send_sems.at[a, k], recv_sem=recv_sems.at[a, k],
                device_id=to, device_id_type=pl.DeviceIdType.MESH)

        local = [pltpu.make_async_copy(ins[a], outs[a].at[idx(*me)], local_sems.at[a]) for a in range(n)]
        for cp in local:
            cp.start()
        sent = []
        for a in range(n):
            sent.append(copy(a, 0, me, sibling, src=ins[a]))
            sent += [copy(a, 1 + j, me, (*chip, c), src=ins[a]) for j, chip in enumerate(chips)]
        for cp in sent:
            cp.start()
        for a in range(n):
            for j, chip in enumerate(chips):
                copy(a, 1 + j, (*chip, c), me).wait_recv()
                passed = copy(a, 4 + j, (*chip, c), sibling)
                passed.start()
                sent.append(passed)
        for a in range(n):
            copy(a, 0, sibling, me).wait_recv()
            for j, chip in enumerate(chips):
                copy(a, 4 + j, (*chip, 1 - c), me).wait_recv()
        for cp in sent:
            cp.wait_send()
        for cp in local:
            cp.wait()

    hbm = pl.BlockSpec(memory_space=pltpu.HBM)
    return pl.pallas_call(
        body, out_shape=out_shape, in_specs=[hbm] * n, out_specs=[hbm] * n,
        scratch_shapes=[pltpu.SemaphoreType.DMA((n, N_DEV - 1)), pltpu.SemaphoreType.DMA((n, N_DEV - 1)),
                        pltpu.SemaphoreType.DMA((n,))],
        name=name,
    )(*parts)


def _swap_with_sibling(parts, name):
    n = len(parts)

    def body(*refs):
        ins, outs = refs[:n], refs[n:2 * n]
        send_sems, recv_sems = refs[2 * n:]
        x, y, c = lax.axis_index("x"), lax.axis_index("y"), lax.axis_index("c")
        copies = [pltpu.make_async_remote_copy(
            src_ref=ins[a].at[1 - c], dst_ref=outs[a], send_sem=send_sems.at[a], recv_sem=recv_sems.at[a],
            device_id=(x, y, 1 - c), device_id_type=pl.DeviceIdType.MESH) for a in range(n)]
        for cp in copies:
            cp.start()
        for cp in copies:
            cp.wait()

    hbm = pl.BlockSpec(memory_space=pltpu.HBM)
    return pl.pallas_call(
        body, out_shape=[jax.ShapeDtypeStruct(p.shape[1:], p.dtype) for p in parts], in_specs=[hbm] * n, out_specs=[hbm] * n,
        scratch_shapes=[pltpu.SemaphoreType.DMA((n,)), pltpu.SemaphoreType.DMA((n,))],
        name=name,
    )(*parts)


def _pair_sum(a, b, name):
    R, C = a.shape
    rb = _adam_rows(R)

    def body(a_ref, b_ref, o_ref):
        o_ref[...] = (a_ref[...].astype(F32) + b_ref[...].astype(F32)).astype(BF16)

    blk = pl.BlockSpec((rb, C), lambda i: (i, 0))
    return pl.pallas_call(body, grid=(R // rb,), in_specs=[blk, blk], out_specs=blk,
                          out_shape=jax.ShapeDtypeStruct((R, C), BF16), name=name, compiler_params=_cparams())(a, b)


def _exchange_chips(parts, name):
    n = len(parts)
    n_chips = N_DEV // 2

    def body(*refs):
        ins, outs = refs[:n], refs[n:2 * n]
        send_sems, recv_sems, local_sems = refs[2 * n:]
        x, y, c = lax.axis_index("x"), lax.axis_index("y"), lax.axis_index("c")
        mine = 2 * x + y
        chips = [(1 - x, y), (x, 1 - y), (1 - x, 1 - y)]
        local = [pltpu.make_async_copy(ins[a].at[mine], outs[a].at[mine], local_sems.at[a]) for a in range(n)]
        for cp in local:
            cp.start()
        for a in range(n):
            for k, (px, py) in enumerate(chips):
                pltpu.make_async_remote_copy(
                    src_ref=ins[a].at[2 * px + py], dst_ref=outs[a].at[mine], send_sem=send_sems.at[a, k],
                    recv_sem=recv_sems.at[a, k], device_id=(px, py, c), device_id_type=pl.DeviceIdType.MESH).start()
        for a in range(n):
            for k, (px, py) in enumerate(chips):
                pltpu.make_async_remote_copy(
                    src_ref=ins[a].at[2 * px + py], dst_ref=outs[a].at[2 * px + py], send_sem=send_sems.at[a, k],
                    recv_sem=recv_sems.at[a, k], device_id=(px, py, c), device_id_type=pl.DeviceIdType.MESH).wait()
        for cp in local:
            cp.wait()

    hbm = pl.BlockSpec(memory_space=pltpu.HBM)
    return pl.pallas_call(
        body, out_shape=[jax.ShapeDtypeStruct(p.shape, p.dtype) for p in parts], in_specs=[hbm] * n, out_specs=[hbm] * n,
        scratch_shapes=[pltpu.SemaphoreType.DMA((n, n_chips - 1)), pltpu.SemaphoreType.DMA((n, n_chips - 1)),
                        pltpu.SemaphoreType.DMA((n,))],
        name=name,
    )(*parts)


def _adam_rows(rows):
    for rb in (128, 64, 40, 16, 8):
        if rows % rb == 0:
            return rb
    return rows


def _adamw(stack, w, m, v, name):
    R, C = w.shape
    rb = _adam_rows(R)
    slots = stack.shape[0]

    def body(s_ref, w_ref, m_ref, v_ref, g_ref, d_ref, nm_ref, nv_ref):
        g = s_ref[0].astype(F32)
        for s in range(1, slots):
            g = g + s_ref[s].astype(F32)
        nm = ADAM_B1 * m_ref[...] + (1.0 - ADAM_B1) * g
        nv = ADAM_B2 * v_ref[...] + (1.0 - ADAM_B2) * (g * g)
        m_hat = nm / (1.0 - ADAM_B1 ** ADAM_STEP)
        v_hat = nv / (1.0 - ADAM_B2 ** ADAM_STEP)
        g_ref[...] = g
        d_ref[...] = -ADAM_LR * (m_hat / (jnp.sqrt(v_hat) + ADAM_EPS) + ADAM_WD * w_ref[...])
        nm_ref[...] = nm
        nv_ref[...] = nv

    blk = pl.BlockSpec((rb, C), lambda i: (i, 0))
    return pl.pallas_call(
        body, grid=(R // rb,),
        in_specs=[pl.BlockSpec((slots, rb, C), lambda i: (0, i, 0)), blk, blk, blk],
        out_specs=[blk] * 4, out_shape=[jax.ShapeDtypeStruct((R, C), F32)] * 4,
        name=name, compiler_params=_cparams(),
    )(stack, w, m, v)


def _pad_rows8(a):
    return jnp.concatenate([a, jnp.zeros((8 - a.shape[0], a.shape[1]), a.dtype)], axis=0) if a.shape[0] < 8 else a


def _pack_small(norm_w, qnw, knw, sinks, a_log, dt_bias, onw, extra):
    z = lambda n: jnp.zeros((1, n), F32)
    row = jnp.concatenate([norm_w, qnw, knw, sinks, a_log, dt_bias, z(80), onw, extra, z(512)], axis=1)
    return row.reshape(16, 128)


def _unpack_small(p):
    row = p.reshape(1, 2048)
    cut = lambda a, n: row[:, a:a + n]
    return (cut(0, 1024), cut(1024, 64), cut(1088, 64), cut(1152, 16), cut(1168, 16), cut(1184, 16), cut(1280, 128),
            cut(1408, 128))


def _pack_rows(w_in_a, w_in_d, w_out_a, w_out_d, meta, conv, dn_norm):
    a = jnp.concatenate([w_in_a, w_in_d], axis=1)
    b = jnp.concatenate([w_out_a, w_out_d], axis=0)
    c = jnp.concatenate([meta, conv.reshape(16, 128), _pad_rows8(dn_norm)], axis=0)
    return a, b, c


def _unpack_rows(a, b, c):
    return (a[:, :288], a[:, 288:], b[:128], b[128:], c[:16], c[16:32].reshape(4, 512), c[32:33])


def _local_step(x, front, target, w):
    xn0, q, kv, gate = _attn_in_fwd(x, front, w["attn_norm_w"], w["attn_w_in"])
    o = _attn_core_fwd(q, kv, w["attn_sinks"], w["attn_q_norm_w"], w["attn_k_norm_w"])
    h1 = _attn_out_fwd(o, gate, x, front, w["attn_w_out"])
    xn1, qkv, z, ba = _dn_in_fwd(h1, w["dn_norm_w"], w["dn_w_in"])
    qn, kn, v, bg, conv_out = _dn_conv_fwd(qkv, ba, w["dn_conv_w"], w["dn_a_log"], w["dn_dt_bias"])
    o_dn, ssave, inv = _dn_scan_fwd(qn, kn, v, bg)
    dy, og_dn, loss = _dn_out_fwd(o_dn, z, h1, target, w["dn_w_out"], w["dn_o_norm_w"])

    g = {}
    do_dn, dz, g["dn_o_norm_w"] = _dn_out_bwd(dy, o_dn, z, w["dn_w_out"], w["dn_o_norm_w"])
    g["dn_w_out"] = _wgrad(og_dn, dy, 1024, "wgrad_dn_out")
    dqn, dkn, dv, dbg = _dn_scan_bwd(do_dn, qn, kn, v, bg, ssave, inv)
    dqkv, dba, g["dn_conv_w"], g["dn_a_log"], g["dn_dt_bias"] = _dn_conv_bwd(
        dqn, dkn, dv, dbg, qkv, conv_out, ba, w["dn_conv_w"], w["dn_a_log"], w["dn_dt_bias"])
    dh1, g["dn_norm_w"] = _dn_in_bwd(dqkv, dz, dba, h1, dy, w["dn_norm_w"], w["dn_w_in"])
    g["dn_w_in"] = jnp.concatenate([_wgrad(xn1, dqkv, 1024, "wgrad_dn_qkv"), _wgrad(xn1, dz, 1024, "wgrad_dn_z"),
                                    _wgrad(xn1, dba, 32, "wgrad_dn_ba")], axis=1)
    do, dgate, g["attn_w_out"] = _attn_out_bwd(dh1, o, gate, w["attn_w_out"])
    dq, dkv, g["attn_sinks"], g["attn_q_norm_w"], g["attn_k_norm_w"] = _attn_core_bwd(
        do, q, kv, w["attn_sinks"], w["attn_q_norm_w"], w["attn_k_norm_w"])
    dh0, g["attn_norm_w"] = _attn_in_bwd(dq, dkv, dgate, x, front, dh1, w["attn_norm_w"], w["attn_w_in"])
    g["attn_w_in"] = jnp.concatenate([_wgrad(xn0, dq, 1024, "wgrad_attn_q"), _wgrad(xn0, dkv, 256, "wgrad_attn_kv"),
                                      _wgrad(xn0, dgate, 1024, "wgrad_attn_gate")], axis=1)
    return loss, dh0, g


WEIGHTS = ['meta_tokens', 'attn_norm_w', 'attn_w_in', 'attn_q_norm_w', 'attn_k_norm_w', 'attn_sinks', 'attn_w_out',
           'dn_norm_w', 'dn_w_in', 'dn_conv_w', 'dn_a_log', 'dn_dt_bias', 'dn_o_norm_w', 'dn_w_out']
SMALL = ['attn_norm_w', 'attn_q_norm_w', 'attn_k_norm_w', 'attn_sinks', 'dn_a_log', 'dn_dt_bias', 'dn_o_norm_w']


def kernel(x, meta_tokens, attn_norm_w, attn_w_in, attn_q_norm_w, attn_k_norm_w, attn_sinks, attn_w_out, dn_norm_w, dn_w_in, dn_conv_w, dn_a_log, dn_dt_bias, dn_o_norm_w, dn_w_out, loss_target, m_meta_tokens, m_attn_norm_w, m_attn_w_in, m_attn_q_norm_w, m_attn_k_norm_w, m_attn_sinks, m_attn_w_out, m_dn_norm_w, m_dn_w_in, m_dn_conv_w, m_dn_a_log, m_dn_dt_bias, m_dn_o_norm_w, m_dn_w_out, v_meta_tokens, v_attn_norm_w, v_attn_w_in, v_attn_q_norm_w, v_attn_k_norm_w, v_attn_sinks, v_attn_w_out, v_dn_norm_w, v_dn_w_in, v_dn_conv_w, v_dn_a_log, v_dn_dt_bias, v_dn_o_norm_w, v_dn_w_out):
    shard = dict(meta_tokens=meta_tokens, attn_norm_w=attn_norm_w, attn_w_in=attn_w_in[0], attn_q_norm_w=attn_q_norm_w,
                 attn_k_norm_w=attn_k_norm_w, attn_sinks=attn_sinks, attn_w_out=attn_w_out[0], dn_norm_w=dn_norm_w,
                 dn_w_in=dn_w_in[0], dn_conv_w=dn_conv_w[0], dn_a_log=dn_a_log, dn_dt_bias=dn_dt_bias,
                 dn_o_norm_w=dn_o_norm_w, dn_w_out=dn_w_out[0])
    mom_m = dict(meta_tokens=m_meta_tokens, attn_norm_w=m_attn_norm_w, attn_w_in=m_attn_w_in[0], attn_q_norm_w=m_attn_q_norm_w,
                 attn_k_norm_w=m_attn_k_norm_w, attn_sinks=m_attn_sinks, attn_w_out=m_attn_w_out[0], dn_norm_w=m_dn_norm_w,
                 dn_w_in=m_dn_w_in[0], dn_conv_w=m_dn_conv_w[0], dn_a_log=m_dn_a_log, dn_dt_bias=m_dn_dt_bias,
                 dn_o_norm_w=m_dn_o_norm_w, dn_w_out=m_dn_w_out[0])
    mom_v = dict(meta_tokens=v_meta_tokens, attn_norm_w=v_attn_norm_w, attn_w_in=v_attn_w_in[0], attn_q_norm_w=v_attn_q_norm_w,
                 attn_k_norm_w=v_attn_k_norm_w, attn_sinks=v_attn_sinks, attn_w_out=v_attn_w_out[0], dn_norm_w=v_dn_norm_w,
                 dn_w_in=v_dn_w_in[0], dn_conv_w=v_dn_conv_w[0], dn_a_log=v_dn_a_log, dn_dt_bias=v_dn_dt_bias,
                 dn_o_norm_w=v_dn_o_norm_w, dn_w_out=v_dn_w_out[0])

    def rows_of(d):
        return _pack_rows(d["attn_w_in"], d["dn_w_in"], d["attn_w_out"], d["dn_w_out"], d["meta_tokens"], d["dn_conv_w"],
                          d["dn_norm_w"])

    def small_of(d, extra):
        return _pack_small(*[d[k] for k in SMALL], extra)

    wa, wb, wc = rows_of(shard)
    ga, gb, gc = _gather_two_level([wa.astype(BF16), wb.astype(BF16), wc], "gather_weights")
    full = {k: shard[k] for k in SMALL}
    full["attn_w_in"] = ga[:, :, :288].transpose(1, 0, 2).reshape(1024, 2304)
    full["dn_w_in"] = ga[:, :, 288:].transpose(1, 0, 2).reshape(1024, 6176)
    full["attn_w_out"] = gb[:, :128].reshape(1024, 1024)
    full["dn_w_out"] = gb[:, 128:].reshape(2048, 1024)
    meta_full = gc[:, :16].transpose(1, 0, 2).reshape(N_META, 1024)
    full["dn_conv_w"] = gc[:, 16:32].reshape(N_DEV, 4, 512).transpose(1, 0, 2).reshape(4, 4096)
    full["dn_norm_w"] = gc[:, 32].reshape(1, 1024)

    seq = x.shape[1]
    front = jnp.concatenate([jnp.zeros((FRONT_PAD, D_MODEL), F32), meta_full], axis=0)
    loss, dh0, g = _local_step(x[0], front, loss_target[0], full)
    grad_x = dh0[ATTN_BLOCK:ATTN_BLOCK + seq][None]
    g["meta_tokens"] = dh0[FRONT_PAD:ATTN_BLOCK]

    pa = jnp.concatenate([g["attn_w_in"].reshape(1024, N_DEV, 288), g["dn_w_in"].reshape(1024, N_DEV, 772)],
                         axis=2).transpose(1, 0, 2)
    pb = jnp.concatenate([g["attn_w_out"].reshape(N_DEV, 128, 1024), g["dn_w_out"].reshape(N_DEV, 256, 1024)], axis=1)
    dn_norm8 = jnp.concatenate([g["dn_norm_w"].reshape(N_DEV, 1, 128), jnp.zeros((N_DEV, 7, 128), F32)], axis=1)
    pc = jnp.concatenate([g["meta_tokens"].reshape(N_META, N_DEV, 128).transpose(1, 0, 2),
                          g["dn_conv_w"].reshape(4, N_DEV, 512).transpose(1, 0, 2).reshape(N_DEV, 16, 128), dn_norm8], axis=1)
    ps = small_of(g, loss)
    c = lax.axis_index("c")
    by_core = lambda p: p.astype(BF16).reshape((N_DEV // 2, 2) + p.shape[1:]).swapaxes(0, 1)
    pa2, pb2 = by_core(pa), by_core(pb)
    ra, rb_ = _swap_with_sibling([pa2, pb2], "swap_grads")
    own = lambda p2: lax.dynamic_index_in_dim(p2, c, axis=0, keepdims=False)
    flat = lambda t: t.reshape((-1,) + t.shape[2:])
    sa = _pair_sum(flat(own(pa2)), flat(ra), "pair_sum_a").reshape(ra.shape)
    sb = _pair_sum(flat(own(pb2)), flat(rb_), "pair_sum_b").reshape(rb_.shape)
    xa, xb = _exchange_chips([sa, sb], "exchange_grads")
    xc, xs = _exchange([pc, ps], [True, False], "exchange_small")

    out = {}
    ma, mb, mc = rows_of(mom_m)
    va, vb, vc = rows_of(mom_v)
    ra = _adamw(xa, wa, ma, va, "adamw_a")
    rb = _adamw(xb, wb, mb, vb, "adamw_b")
    rc = _adamw(xc, wc, mc, vc, "adamw_c")
    zero = jnp.zeros((1, 128), F32)
    rs = _adamw(xs, small_of(shard, zero), small_of(mom_m, zero), small_of(mom_v, zero), "adamw_small")
    row_names = ["attn_w_in", "dn_w_in", "attn_w_out", "dn_w_out", "meta_tokens", "dn_conv_w", "dn_norm_w"]
    lead = {"attn_w_in", "dn_w_in", "attn_w_out", "dn_w_out", "dn_conv_w"}
    for kind in range(4):
        vals = dict(zip(row_names, _unpack_rows(ra[kind], rb[kind], rc[kind])))
        small = _unpack_small(rs[kind])
        vals.update(dict(zip(SMALL, small[:7])))
        if kind == 0:
            loss_total = small[7][0, 0]
        out[kind] = [vals[k][None] if k in lead else vals[k] for k in WEIGHTS]
    return (loss_total, grad_x, *out[0], *out[1], *out[2], *out[3])
```

```python
import functools
import math

import jax
import jax.numpy as jnp
from jax import lax
from jax.experimental import pallas as pl
from jax.experimental.pallas import tpu as pltpu

F32, BF16 = jnp.float32, jnp.bfloat16

D_MODEL = 1024
N_META = 16
NORM_EPS = 1e-6
ATTN_HEADS, ATTN_KV_HEADS, ATTN_GROUPS, ATTN_HD = 16, 2, 8, 64
ATTN_BLOCK = 128
FRONT_PAD = ATTN_BLOCK - N_META
DN_HD, DN_K_HEADS, DN_V_HEADS = 128, 8, 16
DN_CHUNK = 128
TRI_BLOCK = 64
SCAN_CHUNKS = 3
SCAN_BWD_CHUNKS = 1
SCAN_FWD_GROUP = 8
SCAN_BWD_GROUP = 8
DN_KEY_W, DN_VAL_W = 1024, 2048
DN_CONV_W = 2 * DN_KEY_W + DN_VAL_W
DN_CONV_K = 4
N_DEV = 8
ROW_BLOCK = 384
WGRAD_ROWS = 1376
VMEM_LIMIT = 56 * 1024 * 1024
NEG = -1e30

ADAM_LR, ADAM_B1, ADAM_B2, ADAM_EPS, ADAM_WD, ADAM_STEP = 0.001, 0.9, 0.999, 1e-08, 0.01, 10

NT = (((1,), (1,)), ((), ()))
TN = (((0,), (0,)), ((), ()))


def _cparams(sem=("arbitrary",)):
    return pltpu.CompilerParams(dimension_semantics=sem, vmem_limit_bytes=VMEM_LIMIT)


def _rms(x, w):
    return x * lax.rsqrt(jnp.mean(x * x, axis=-1, keepdims=True) + NORM_EPS) * w


def _silu(x):
    return x * jax.nn.sigmoid(x)


def _softplus(x):
    return jnp.maximum(x, 0.0) + jnp.log(1.0 + jnp.exp(-jnp.abs(x)))


NN = (((1,), (0,)), ((), ()))


def _mm(a, b, dims):
    return lax.dot_general(a.astype(BF16), b.astype(BF16), dims, preferred_element_type=F32)


@functools.partial(jax.custom_vjp, nondiff_argnums=(2,))
def _bdot_vjp(a, b, dims):
    return _mm(a, b, dims)


def _bdot_fwd(a, b, dims):
    a16, b16 = a.astype(BF16), b.astype(BF16)
    return _mm(a16, b16, dims), (a16, b16, jnp.zeros((), a.dtype), jnp.zeros((), b.dtype))


def _bdot_bwd(dims, res, g):
    a16, b16, ta, tb = res
    g16 = g.astype(BF16)
    if dims == NN:
        da, db = _mm(g16, b16, NT), _mm(a16, g16, TN)
    elif dims == NT:
        da, db = _mm(g16, b16, NN), _mm(g16, a16, TN)
    else:
        da, db = _mm(b16, g16, NT), _mm(a16, g16, NN)
    return da.astype(ta.dtype), db.astype(tb.dtype)


_bdot_vjp.defvjp(_bdot_fwd, _bdot_bwd)


def _bdot(a, b, dims=NN):
    return _bdot_vjp(a, b, dims)


def _hdot(a, b):
    return jnp.dot(a, b, preferred_element_type=F32, precision=lax.Precision.HIGHEST)


def _row_call(name, body, n_rows, rb, rows, consts, outs, accs=(), reverse=False, scratch=(), halos=()):
    n = n_rows // rb
    assert n * rb == n_rows
    idx = (lambda i: (n - 1 - i, 0)) if reverse else (lambda i: (i, 0))
    in_specs = [pl.BlockSpec((rb, a.shape[1]), idx) for a in rows]
    in_specs += [pl.BlockSpec((hr, a.shape[1]), fn) for a, hr, fn in halos]
    in_specs += [pl.BlockSpec(c.shape, functools.partial(lambda i, nd: (0,) * nd, nd=c.ndim)) for c in consts]
    out_specs = [pl.BlockSpec((rb, c), idx) for c, _ in outs]
    out_specs += [pl.BlockSpec(s, functools.partial(lambda i, nd: (0,) * nd, nd=len(s))) for s, _ in accs]
    out_shape = [jax.ShapeDtypeStruct((n_rows, c), dt) for c, dt in outs]
    out_shape += [jax.ShapeDtypeStruct(s, dt) for s, dt in accs]
    return pl.pallas_call(
        body, grid=(n,), in_specs=in_specs, out_specs=out_specs, out_shape=out_shape,
        scratch_shapes=list(scratch), name=name, compiler_params=_cparams(),
    )(*rows, *[a for a, _, _ in halos], *consts)


def _token_views(x):
    per = ROW_BLOCK // ATTN_BLOCK
    return [(x, ATTN_BLOCK, functools.partial(lambda i, k: (jnp.maximum(per * i - 1 + k, 0), 0), k=k)) for k in range(per)]


def _padded_block(i, front, views):
    first = jnp.where(i == 0, front, views[0][...]) if front is not None else views[0][...]
    return jnp.concatenate([first] + [v[...] for v in views[1:]], axis=0)


def _attn_in_fwd(x, front, norm_w, w_in):
    T = x.shape[0] + ATTN_BLOCK

    def body(xa_ref, xb_ref, xc_ref, front_ref, nw_ref, w_ref, xn_ref, q_ref, kv_ref, gate_ref):
        h = _padded_block(pl.program_id(0), front_ref[...], (xa_ref, xb_ref, xc_ref))
        xn = _rms(h, nw_ref[...]).astype(BF16)
        xn_ref[...] = xn
        q_ref[...] = jnp.dot(xn, w_ref[:, 0:1024], preferred_element_type=F32)
        kv_ref[...] = jnp.dot(xn, w_ref[:, 1024:1280], preferred_element_type=F32)
        gate_ref[...] = jnp.dot(xn, w_ref[:, 1280:2304], preferred_element_type=F32)

    return _row_call("attn_in_fwd", body, T, ROW_BLOCK, [], [front, norm_w, w_in],
                     [(1024, BF16), (1024, F32), (256, F32), (1024, F32)], halos=_token_views(x))


def _attn_bias(n, j):
    C, R = 2 * ATTN_BLOCK + N_META, ATTN_GROUPS * ATTN_BLOCK
    c = lax.broadcasted_iota(jnp.int32, (C, R), 0)
    r = lax.broadcasted_iota(jnp.int32, (C, R), 1)
    ql = r & (ATTN_BLOCK - 1)
    is_meta = c >= 2 * ATTN_BLOCK
    dist_band = ATTN_BLOCK + ql - c
    cmin = jnp.maximum(0, 2 * ATTN_BLOCK - ATTN_BLOCK * n)
    valid_band = (c >= cmin) & (dist_band >= 0) & (dist_band < ATTN_BLOCK)
    dist_meta = ATTN_BLOCK * n + ql - FRONT_PAD - (c - 2 * ATTN_BLOCK)
    valid = (is_meta & (dist_meta >= 0)) | (jnp.logical_not(is_meta) & valid_band)
    dist = jnp.minimum(jnp.where(is_meta, dist_meta, dist_band), ATTN_BLOCK).astype(F32)
    rr = lax.broadcasted_iota(jnp.int32, (1, R), 1)
    head = (rr >> 7).astype(F32) + float(ATTN_GROUPS * j + 1)
    slope = jnp.exp(head * (-0.5 * math.log(2.0)))
    return jnp.where(valid, slope * dist, -NEG)


def _attn_tables(n, refresh, bias_ref):
    @pl.when(refresh)
    def _():
        for j in range(ATTN_KV_HEADS):
            bias_ref[j] = _attn_bias(n, j)


def _attn_table_scratch():
    return [pltpu.VMEM((ATTN_KV_HEADS, 2 * ATTN_BLOCK + N_META, ATTN_GROUPS * ATTN_BLOCK), F32)]


def _attn_groups(q_t, k, v, sinkrow, qnw_col, knw, bias, late_norm=True):
    n = range(len(q_t))
    qn = [q_t[j] * (lax.rsqrt(jnp.mean(q_t[j] * q_t[j], axis=0, keepdims=True) + NORM_EPS) * (ATTN_HD ** -0.5)) * qnw_col
          for j in n]
    kn = [_rms(k[j], knw) for j in n]
    s = [_bdot(kn[j], qn[j]) - bias[j] for j in n]
    m = [lax.stop_gradient(jnp.maximum(jnp.max(s[j], axis=0, keepdims=True), sinkrow[j])) for j in n]
    e = [jnp.exp(s[j] - m[j]) for j in n]
    inv = [1.0 / (jnp.sum(e[j], axis=0, keepdims=True) + jnp.exp(sinkrow[j] - m[j])) for j in n]
    if late_norm:
        return [_bdot(v[j], e[j], TN) * inv[j] for j in n]
    return [_bdot(v[j], e[j] * inv[j], TN) for j in n]


def _sink_row(sinks_ref, j):
    rr = lax.broadcasted_iota(jnp.int32, (1, ATTN_GROUPS * ATTN_BLOCK), 1) >> 7
    row = jnp.zeros((1, ATTN_GROUPS * ATTN_BLOCK), F32)
    for hl in range(ATTN_GROUPS):
        row = jnp.where(rr == hl, sinks_ref[0, ATTN_GROUPS * j + hl], row)
    return row


def _heads_to_lanes(ref, j):
    return jnp.concatenate([ref[:, ATTN_HD * h:ATTN_HD * (h + 1)].T
                            for h in range(ATTN_GROUPS * j, ATTN_GROUPS * (j + 1))], axis=1)


def _lanes_to_heads(ref, j, x_t):
    for hl in range(ATTN_GROUPS):
        h = ATTN_GROUPS * j + hl
        ref[:, ATTN_HD * h:ATTN_HD * (h + 1)] = x_t[:, ATTN_BLOCK * hl:ATTN_BLOCK * (hl + 1)].T


def _attn_kv_tiles(kvp_ref, kvc_ref, kvm_ref, j):
    ksl = slice(ATTN_HD * j, ATTN_HD * (j + 1))
    vsl = slice(128 + ATTN_HD * j, 128 + ATTN_HD * (j + 1))
    k = jnp.concatenate([kvp_ref[:, ksl], kvc_ref[:, ksl], kvm_ref[FRONT_PAD:, ksl]], axis=0)
    v = jnp.concatenate([kvp_ref[:, vsl], kvc_ref[:, vsl], kvm_ref[FRONT_PAD:, vsl]], axis=0)
    return k, v


def _attn_core_fwd(q, kv, sinks, qnw, knw):
    T = q.shape[0]
    nb = T // ATTN_BLOCK

    def body(sinks_ref, q_ref, kvc_ref, kvp_ref, kvm_ref, qnw_ref, knw_ref, o_ref, bias_ref):
        n = pl.program_id(0)
        _attn_tables(n, n <= 2, bias_ref)
        kvh = range(ATTN_KV_HEADS)
        kv_tiles = [_attn_kv_tiles(kvp_ref, kvc_ref, kvm_ref, j) for j in kvh]
        o_t = _attn_groups([_heads_to_lanes(q_ref, j) for j in kvh], [t[0] for t in kv_tiles], [t[1] for t in kv_tiles],
                           [_sink_row(sinks_ref, j) for j in kvh], qnw_ref[...], knw_ref[...], [bias_ref[j] for j in kvh])
        for j in kvh:
            _lanes_to_heads(o_ref, j, o_t[j])

    return pl.pallas_call(
        body, grid=(nb,),
        in_specs=[pl.BlockSpec(memory_space=pltpu.SMEM),
                  pl.BlockSpec((ATTN_BLOCK, 1024), lambda i: (i, 0)),
                  pl.BlockSpec((ATTN_BLOCK, 256), lambda i: (i, 0)),
                  pl.BlockSpec((ATTN_BLOCK, 256), lambda i: (jnp.maximum(i - 1, 0), 0)),
                  pl.BlockSpec((ATTN_BLOCK, 256), lambda i: (0, 0)),
                  pl.BlockSpec((ATTN_HD, 1), lambda i: (0, 0)),
                  pl.BlockSpec((1, ATTN_HD), lambda i: (0, 0))],
        out_specs=pl.BlockSpec((ATTN_BLOCK, 1024), lambda i: (i, 0)),
        out_shape=jax.ShapeDtypeStruct((T, 1024), F32),
        scratch_shapes=_attn_table_scratch(),
        name="attn_core_fwd", compiler_params=_cparams(),
    )(sinks, q, kv, kv, kv, qnw.reshape(ATTN_HD, 1), knw)


def _attn_out_fwd(o, gate, x, front, w_out):
    T = o.shape[0]

    def body(o_ref, g_ref, xa_ref, xb_ref, xc_ref, front_ref, w_ref, h1_ref):
        h = _padded_block(pl.program_id(0), front_ref[...], (xa_ref, xb_ref, xc_ref))
        og = o_ref[...] * _silu(g_ref[...])
        h1_ref[...] = h + _bdot(og, w_ref[...])

    return _row_call("attn_out_fwd", body, T, ROW_BLOCK, [o, gate], [front, w_out], [(1024, F32)], halos=_token_views(x))[0]


def _wgrad(xn, du, cg, name):
    T, kdim = xn.shape
    cdim = du.shape[1]
    rows = WGRAD_ROWS if T % WGRAD_ROWS == 0 else ROW_BLOCK
    nr, nc = T // rows, cdim // cg
    assert nc * cg == cdim

    def body(x_ref, du_ref, dw_ref):
        @pl.when(pl.program_id(1) == 0)
        def _():
            dw_ref[...] = jnp.zeros_like(dw_ref)
        dw_ref[...] += _bdot(x_ref[...], du_ref[...], TN)

    return pl.pallas_call(
        body, grid=(nc, nr),
        in_specs=[pl.BlockSpec((rows, kdim), lambda j, i: (i, 0)),
                  pl.BlockSpec((rows, cg), lambda j, i: (i, j))],
        out_specs=pl.BlockSpec((kdim, cg), lambda j, i: (0, j)),
        out_shape=jax.ShapeDtypeStruct((kdim, cdim), F32),
        name=name, compiler_params=_cparams(("arbitrary", "arbitrary")),
    )(xn, du)


def _attn_out_bwd(dh1, o, gate, w_out):
    T = o.shape[0]

    def body(dh_ref, o_ref, g_ref, w_ref, do_ref, dg_ref, dw_ref):
        @pl.when(pl.program_id(0) == 0)
        def _():
            dw_ref[...] = jnp.zeros_like(dw_ref)
        dh = dh_ref[...]
        dog = _bdot(dh, w_ref[...], NT)
        og, vjp = jax.vjp(lambda o_, g_: o_ * _silu(g_), o_ref[...], g_ref[...])
        do, dg = vjp(dog)
        do_ref[...] = do
        dg_ref[...] = dg
        dw_ref[...] += _bdot(og, dh, TN)

    return _row_call("attn_out_bwd", body, T, ROW_BLOCK, [dh1, o, gate], [w_out],
                     [(1024, F32), (1024, F32)], [((1024, 1024), F32)])


def _attn_core_bwd(do, q, kv, sinks, qnw, knw):
    T = q.shape[0]
    nb = T // ATTN_BLOCK
    rev = lambda i: nb - 1 - i

    def body(sinks_ref, do_ref, q_ref, kvc_ref, kvp_ref, kvm_ref, qnw_ref, knw_ref,
             dq_ref, dkv_ref, dsinks_ref, dqnw_ref, dknw_ref, carry_ref, meta_ref, bias_ref):
        step = pl.program_id(0)
        n = rev(step)
        _attn_tables(n, (step == 0) | (n <= 1), bias_ref)

        @pl.when(step == 0)
        def _():
            carry_ref[...] = jnp.zeros_like(carry_ref)
            meta_ref[...] = jnp.zeros_like(meta_ref)
            dsinks_ref[...] = jnp.zeros_like(dsinks_ref)
            dqnw_ref[...] = jnp.zeros_like(dqnw_ref)
            dknw_ref[...] = jnp.zeros_like(dknw_ref)

        lane16 = lax.broadcasted_iota(jnp.int32, (1, ATTN_HEADS), 1)
        dsinks = jnp.zeros((1, ATTN_HEADS), F32)
        kvh = range(ATTN_KV_HEADS)
        kv_tiles = [_attn_kv_tiles(kvp_ref, kvc_ref, kvm_ref, j) for j in kvh]
        fn = functools.partial(_attn_groups, bias=[bias_ref[j] for j in kvh], late_norm=False)
        _, vjp = jax.vjp(fn, [_heads_to_lanes(q_ref, j) for j in kvh], [t[0] for t in kv_tiles], [t[1] for t in kv_tiles],
                         [_sink_row(sinks_ref, j) for j in kvh], qnw_ref[...], knw_ref[...])
        dq_t, dks, dvs, dsr, dqn, dkn = vjp([_heads_to_lanes(do_ref, j) for j in kvh])
        dqnw_ref[...] += dqn
        dknw_ref[...] += dkn
        for j in kvh:
            _lanes_to_heads(dq_ref, j, dq_t[j])
            for hl in range(ATTN_GROUPS):
                dsinks = dsinks + jnp.where(lane16 == ATTN_GROUPS * j + hl,
                                            jnp.sum(dsr[j][:, ATTN_BLOCK * hl:ATTN_BLOCK * (hl + 1)]), 0.0)
            ksl = slice(ATTN_HD * j, ATTN_HD * (j + 1))
            vsl = slice(128 + ATTN_HD * j, 128 + ATTN_HD * (j + 1))
            for sl, d in ((ksl, dks[j]), (vsl, dvs[j])):
                dkv_ref[:, sl] = d[ATTN_BLOCK:2 * ATTN_BLOCK, :] + carry_ref[:, sl]
                carry_ref[:, sl] = d[0:ATTN_BLOCK, :]
                meta_ref[:, sl] += d[2 * ATTN_BLOCK:, :]
        dsinks_ref[...] += dsinks

        @pl.when(n == 0)
        def _():
            dkv_ref[FRONT_PAD:, :] += meta_ref[...]

    dq, dkv, dsinks, dqnw, dknw = pl.pallas_call(
        body, grid=(nb,),
        in_specs=[pl.BlockSpec(memory_space=pltpu.SMEM),
                  pl.BlockSpec((ATTN_BLOCK, 1024), lambda i: (rev(i), 0)),
                  pl.BlockSpec((ATTN_BLOCK, 1024), lambda i: (rev(i), 0)),
                  pl.BlockSpec((ATTN_BLOCK, 256), lambda i: (rev(i), 0)),
                  pl.BlockSpec((ATTN_BLOCK, 256), lambda i: (jnp.maximum(rev(i) - 1, 0), 0)),
                  pl.BlockSpec((ATTN_BLOCK, 256), lambda i: (0, 0)),
                  pl.BlockSpec((ATTN_HD, 1), lambda i: (0, 0)),
                  pl.BlockSpec((1, ATTN_HD), lambda i: (0, 0))],
        out_specs=[pl.BlockSpec((ATTN_BLOCK, 1024), lambda i: (rev(i), 0)),
                   pl.BlockSpec((ATTN_BLOCK, 256), lambda i: (rev(i), 0)),
                   pl.BlockSpec((1, ATTN_HEADS), lambda i: (0, 0)),
                   pl.BlockSpec((ATTN_HD, 1), lambda i: (0, 0)),
                   pl.BlockSpec((1, ATTN_HD), lambda i: (0, 0))],
        out_shape=[jax.ShapeDtypeStruct((T, 1024), F32), jax.ShapeDtypeStruct((T, 256), F32),
                   jax.ShapeDtypeStruct((1, ATTN_HEADS), F32), jax.ShapeDtypeStruct((ATTN_HD, 1), F32),
                   jax.ShapeDtypeStruct((1, ATTN_HD), F32)],
        scratch_shapes=[pltpu.VMEM((ATTN_BLOCK, 256), F32), pltpu.VMEM((N_META, 256), F32)] + _attn_table_scratch(),
        name="attn_core_bwd", compiler_params=_cparams(),
    )(sinks, do, q, kv, kv, kv, qnw.reshape(ATTN_HD, 1), knw)
    return dq, dkv, dsinks, dqnw.reshape(1, ATTN_HD), dknw


def _attn_in_bwd(dq, dkv, dgate, x, front, dh1, norm_w, w_in):
    T = dq.shape[0]

    def body(dq_ref, dkv_ref, dg_ref, dh1_ref, xa_ref, xb_ref, xc_ref, front_ref, nw_ref, w_ref, dh0_ref, dnw_ref):
        @pl.when(pl.program_id(0) == 0)
        def _():
            dnw_ref[...] = jnp.zeros_like(dnw_ref)
        h = _padded_block(pl.program_id(0), front_ref[...], (xa_ref, xb_ref, xc_ref))
        dxn = (_bdot(dq_ref[...], w_ref[:, 0:1024], NT) + _bdot(dkv_ref[...], w_ref[:, 1024:1280], NT)
               + _bdot(dg_ref[...], w_ref[:, 1280:2304], NT))
        _, vjp = jax.vjp(_rms, h, nw_ref[...])
        dh, dnw = vjp(dxn)
        dh0_ref[...] = dh1_ref[...] + dh
        dnw_ref[...] += dnw

    return _row_call("attn_in_bwd", body, T, ROW_BLOCK, [dq, dkv, dgate, dh1], [front, norm_w, w_in],
                     [(1024, F32)], [((1, 1024), F32)], halos=_token_views(x))


def _dn_in_fwd(h1, norm_w, w_in):
    T = h1.shape[0]

    def body(h_ref, nw_ref, w_ref, xn_ref, qkv_ref, z_ref, ba_ref):
        xn = _rms(h_ref[...], nw_ref[...]).astype(BF16)
        xn_ref[...] = xn
        qkv_ref[...] = jnp.dot(xn, w_ref[:, 0:4096], preferred_element_type=F32)
        z_ref[...] = jnp.dot(xn, w_ref[:, 4096:6144], preferred_element_type=F32)
        ba_ref[...] = jnp.dot(xn, w_ref[:, 6144:6176], preferred_element_type=F32)

    return _row_call("dn_in_fwd", body, T, ROW_BLOCK, [h1], [norm_w, w_in],
                     [(1024, BF16), (4096, F32), (2048, F32), (32, F32)])


def _shift_down(cur, prev8, s):
    i8 = lax.broadcasted_iota(jnp.int32, (8, cur.shape[1]), 0)
    r = pltpu.roll(cur, s, 0)
    head = jnp.where(i8 < s, pltpu.roll(prev8, s, 0), r[0:8])
    return jnp.concatenate([head, r[8:]], axis=0)


def _shift_up(cur, next8, s):
    n = cur.shape[0]
    i8 = lax.broadcasted_iota(jnp.int32, (8, cur.shape[1]), 0)
    r = pltpu.roll(cur, n - s, 0)
    tail = jnp.where(i8 >= 8 - s, pltpu.roll(next8, 8 - s, 0), r[n - 8:])
    return jnp.concatenate([r[:n - 8], tail], axis=0)


def _conv_taps(cur, prev8):
    return [cur] + [_shift_down(cur, prev8, s) for s in range(1, DN_CONV_K)]


def _conv_tile(taps, w):
    out = w[3:4, :] * taps[0]
    for s in range(1, DN_CONV_K):
        out = out + w[3 - s:4 - s, :] * taps[s]
    return out


def _l2n(a, scale):
    return a * (lax.rsqrt(jnp.sum(a * a, axis=-1, keepdims=True) + NORM_EPS) * scale)


def _dn_post_tile(c, t):
    a = _silu(c)
    if t < DN_K_HEADS:
        return _l2n(a, DN_HD ** -0.5)
    if t < 2 * DN_K_HEADS:
        return _l2n(a, 1.0)
    return a


def _dn_beta_g(ba, a_log, dt_bias, live):
    beta = jax.nn.sigmoid(ba[:, 0:DN_V_HEADS]) * live
    g = -jnp.exp(a_log) * _softplus(ba[:, DN_V_HEADS:] + dt_bias) * live
    return beta, g


def _live_rows(i, rb):
    rows = i * rb + lax.broadcasted_iota(jnp.int32, (rb, 1), 0)
    return (rows >= FRONT_PAD).astype(F32)


def _halo_spec_args(x, rb):
    per = rb // 8
    return (x, 8, lambda i: (jnp.maximum(i * per - 1, 0), 0))


def _dn_conv_fwd(qkv, ba, conv_w, a_log, dt_bias):
    T = qkv.shape[0]

    def body(x_ref, ba_ref, halo_ref, cw_ref, al_ref, dtb_ref, q_ref, k_ref, v_ref, bg_ref, c_ref):
        i = pl.program_id(0)
        first = (i > 0).astype(F32)
        for t in range(DN_CONV_W // 128):
            cols = slice(128 * t, 128 * (t + 1))
            c = _conv_tile(_conv_taps(x_ref[:, cols], halo_ref[:, cols] * first), cw_ref[:, cols])
            c_ref[:, cols] = c
            out = _dn_post_tile(c, t)
            if t < DN_K_HEADS:
                q_ref[:, cols] = out
            elif t < 2 * DN_K_HEADS:
                k_ref[:, 128 * (t - 8):128 * (t - 7)] = out
            else:
                v_ref[:, 128 * (t - 16):128 * (t - 15)] = out
        beta, g = _dn_beta_g(ba_ref[...], al_ref[...], dtb_ref[...], _live_rows(i, ROW_BLOCK))
        bg_ref[:, 0:DN_V_HEADS] = beta
        bg_ref[:, DN_V_HEADS:] = g

    return _row_call("dn_conv_fwd", body, T, ROW_BLOCK, [qkv, ba], [conv_w, a_log, dt_bias],
                     [(1024, F32), (1024, F32), (2048, F32), (32, F32), (4096, F32)], halos=[_halo_spec_args(qkv, ROW_BLOCK)])


def _chunk_masks():
    r = lax.broadcasted_iota(jnp.int32, (DN_CHUNK, DN_CHUNK), 0)
    c = lax.broadcasted_iota(jnp.int32, (DN_CHUNK, DN_CHUNK), 1)
    return r >= c, r > c, r == c, r <= c


def _tri_inv_block(x):
    B = TRI_BLOCK
    n = range(len(x))
    r_, c_ = lax.broadcasted_iota(jnp.int32, (B, B), 0), lax.broadcasted_iota(jnp.int32, (B, B), 1)
    ainv = [jnp.where(r_ == c_, 1.0, 0.0) + x[h] for h in n]
    p = [_bdot(x[h], x[h]) for h in n]
    for _ in range(B.bit_length() - 3):
        r = [_bdot(jnp.concatenate([p[h], ainv[h]], axis=0), p[h]) for h in n]
        ainv = [ainv[h] + r[h][B:] for h in n]
        p = [r[h][:B] for h in n]
    return [ainv[h] + _bdot(ainv[h], p[h]) for h in n]


def _tri_inv(x):
    B = TRI_BLOCK
    assert DN_CHUNK == 2 * B
    n = len(x)
    diag = _tri_inv_block([x[h][:B, :B] for h in range(n)] + [x[h][B:, B:] for h in range(n)])
    a11, a22 = diag[:n], diag[n:]
    a21 = [_bdot(_bdot(a22[h], x[h][B:, :B]), a11[h]) for h in range(n)]
    zero = jnp.zeros((B, B), F32)
    return [jnp.concatenate([jnp.concatenate([a11[h], zero], axis=1), jnp.concatenate([a21[h], a22[h]], axis=1)], axis=0)
            for h in range(n)]


@jax.custom_vjp
def _tri_inv_known(x, a):
    return a


def _tri_inv_known_fwd(x, a):
    return a, a


def _tri_inv_known_bwd(a, da):
    return [_bdot(_bdot(a[h], da[h], TN), a[h], NT) for h in range(len(a))], [jnp.zeros_like(t) for t in a]


_tri_inv_known.defvjp(_tri_inv_known_fwd, _tri_inv_known_bwd)


@jax.custom_vjp
def _known(computed, value):
    return value


def _known_fwd(computed, value):
    return value, None


def _known_bwd(_, g):
    return g, jax.tree.map(jnp.zeros_like, g)


_known.defvjp(_known_fwd, _known_bwd)


def _dn_chunk_step(S, q, k, v, beta, g, masks, known=None, with_saved=False):
    causal, strict, eye, upper = masks
    C, W = DN_CHUNK, DN_HD
    heads = range(len(v))
    k_t = [k[j].T for j in range(len(k))]
    qk_kk = [_bdot(jnp.concatenate([q[j], k[j]], axis=0), k_t[j]) for j in range(len(q))]
    if known is not None:
        qk_kk = _known(qk_kk, known["qk_kk"])
    g_b = [jnp.broadcast_to(g[h], (C, C)) for h in heads]
    beta_b = [jnp.broadcast_to(beta[h], (C, W)) for h in heads]
    g_row = [jnp.sum(jnp.where(eye, g_b[h], 0.0), axis=0, keepdims=True) for h in heads]
    gc_col = [jnp.sum(jnp.where(causal, g_row[h], 0.0), axis=1, keepdims=True) for h in heads]
    gc_row = [jnp.sum(jnp.where(upper, g_b[h], 0.0), axis=0, keepdims=True) for h in heads]
    g_last = [jnp.sum(g_row[h], axis=1, keepdims=True) for h in heads]
    gc_b = [jnp.broadcast_to(gc_col[h], (C, W)) for h in heads]
    decay = [jnp.exp(jnp.where(causal, gc_b[h][:, :C] - gc_row[h], NEG)) for h in heads]
    eg_b = [jnp.exp(gc_b[h]) for h in heads]
    x = [jnp.where(strict, qk_kk[h // 2][C:] * beta_b[h][:, :C] * decay[h], 0.0) * -1.0 for h in heads]
    ainv = _tri_inv(x) if known is None else _tri_inv_known(x, known["inv"])
    uw = [_bdot(ainv[h], jnp.concatenate([v[h] * beta_b[h], k[h // 2] * (beta_b[h] * eg_b[h])], axis=1)) for h in heads]
    if known is not None:
        uw = _known(uw, known["uw"])
    q_eg = [q[h // 2] * eg_b[h] for h in heads]
    attn = [qk_kk[h // 2][:C] * decay[h] for h in heads]
    k_st = [k_t[h // 2] * jnp.exp(g_last[h] - gc_row[h]) for h in heads]
    s_dec = [jnp.exp(g_last[h]) for h in heads]
    prep = (uw, q_eg, attn, k_st, s_dec)
    if S is None:
        return prep, dict(inv=ainv, uw=uw, qk_kk=qk_kk)
    s_new, o, _ = _dn_chunk_tail(S, prep, None if known is None else known["v_new"])
    return s_new, o


def _dn_chunk_tail(S, prep, known_v_new=None):
    uw, q_eg, attn, k_st, s_dec = prep
    C, W = DN_CHUNK, DN_HD
    heads = range(len(uw))
    ws_qs = [_bdot(jnp.concatenate([uw[h][:, W:], q_eg[h]], axis=0), S[h]) for h in heads]
    v_new = [uw[h][:, :W] - ws_qs[h][:C] for h in heads]
    if known_v_new is not None:
        v_new = _known(v_new, known_v_new)
    o = [ws_qs[h][C:] + _bdot(attn[h], v_new[h]) for h in heads]
    s_new = [S[h] * s_dec[h] + _bdot(k_st[h], v_new[h]) for h in heads]
    return s_new, o, v_new


def _dn_chunk_tiles(q_ref, k_ref, v_ref, bg_ref, c, first, count):
    rows = slice(DN_CHUNK * c, DN_CHUNK * (c + 1))
    q = [q_ref[rows, 128 * j:128 * (j + 1)] for j in range(first // 2, (first + count) // 2)]
    k = [k_ref[rows, 128 * j:128 * (j + 1)] for j in range(first // 2, (first + count) // 2)]
    v = [v_ref[rows, 128 * h:128 * (h + 1)] for h in range(first, first + count)]
    beta = [bg_ref[rows, h:h + 1] for h in range(first, first + count)]
    g = [bg_ref[rows, DN_V_HEADS + h:DN_V_HEADS + h + 1] for h in range(first, first + count)]
    return q, k, v, beta, g


def _dn_scan_fwd(qn, kn, v, bg):
    T = qn.shape[0]
    nc = T // DN_CHUNK
    rows = SCAN_CHUNKS * DN_CHUNK
    assert nc % SCAN_CHUNKS == 0

    def body(q_ref, k_ref, v_ref, bg_ref, o_ref, ssave_ref, inv_ref, uw_ref, vn_ref, qk_ref, s_ref):
        @pl.when(pl.program_id(0) == 0)
        def _():
            s_ref[...] = jnp.zeros_like(s_ref)
        masks = _chunk_masks()
        for first in range(0, DN_V_HEADS, SCAN_FWD_GROUP):
            heads = range(first, first + SCAN_FWD_GROUP)
            preps = [_dn_chunk_step(None, *_dn_chunk_tiles(q_ref, k_ref, v_ref, bg_ref, c, first, SCAN_FWD_GROUP), masks)
                     for c in range(SCAN_CHUNKS)]
            state = [s_ref[h] for h in heads]
            for c, (prep, saved) in enumerate(preps):
                for i, h in enumerate(heads):
                    ssave_ref[c, h] = state[i]
                    inv_ref[c, h] = saved["inv"][i].astype(BF16)
                    uw_ref[c, h] = saved["uw"][i].astype(BF16)
                for i, j in enumerate(range(first // 2, (first + SCAN_FWD_GROUP) // 2)):
                    qk_ref[c, j] = saved["qk_kk"][i].astype(BF16)
                state, o, v_new = _dn_chunk_tail(state, prep)
                for i, h in enumerate(heads):
                    o_ref[DN_CHUNK * c:DN_CHUNK * (c + 1), 128 * h:128 * (h + 1)] = o[i]
                    vn_ref[c, h] = v_new[i].astype(BF16)
            for i, h in enumerate(heads):
                s_ref[h] = state[i]

    return pl.pallas_call(
        body, grid=(nc // SCAN_CHUNKS,),
        in_specs=[pl.BlockSpec((rows, 1024), lambda i: (i, 0)),
                  pl.BlockSpec((rows, 1024), lambda i: (i, 0)),
                  pl.BlockSpec((rows, 2048), lambda i: (i, 0)),
                  pl.BlockSpec((rows, 32), lambda i: (i, 0))],
        out_specs=[pl.BlockSpec((rows, 2048), lambda i: (i, 0)),
                   pl.BlockSpec((SCAN_CHUNKS, DN_V_HEADS, DN_HD, DN_HD), lambda i: (i, 0, 0, 0)),
                   pl.BlockSpec((SCAN_CHUNKS, DN_V_HEADS, DN_CHUNK, DN_CHUNK), lambda i: (i, 0, 0, 0)),
                   pl.BlockSpec((SCAN_CHUNKS, DN_V_HEADS, DN_CHUNK, 2 * DN_HD), lambda i: (i, 0, 0, 0)),
                   pl.BlockSpec((SCAN_CHUNKS, DN_V_HEADS, DN_CHUNK, DN_HD), lambda i: (i, 0, 0, 0)),
                   pl.BlockSpec((SCAN_CHUNKS, DN_K_HEADS, 2 * DN_CHUNK, DN_CHUNK), lambda i: (i, 0, 0, 0))],
        out_shape=[jax.ShapeDtypeStruct((T, 2048), F32),
                   jax.ShapeDtypeStruct((nc, DN_V_HEADS, DN_HD, DN_HD), F32),
                   jax.ShapeDtypeStruct((nc, DN_V_HEADS, DN_CHUNK, DN_CHUNK), BF16),
                   jax.ShapeDtypeStruct((nc, DN_V_HEADS, DN_CHUNK, 2 * DN_HD), BF16),
                   jax.ShapeDtypeStruct((nc, DN_V_HEADS, DN_CHUNK, DN_HD), BF16),
                   jax.ShapeDtypeStruct((nc, DN_K_HEADS, 2 * DN_CHUNK, DN_CHUNK), BF16)],
        scratch_shapes=[pltpu.VMEM((DN_V_HEADS, DN_HD, DN_HD), F32)],
        name="dn_scan_fwd", compiler_params=_cparams(),
    )(qn, kn, v, bg)


def _dn_gate_tile(o, z, onw):
    return _rms(o, onw) * _silu(z)


def _dn_out_fwd(o, z, h1, target, w_out, onw):
    T = o.shape[0]

    def body(o_ref, z_ref, h_ref, ta_ref, tb_ref, tc_ref, w_ref, onw_ref, dy_ref, og_ref, loss_ref):
        i = pl.program_id(0)

        @pl.when(i == 0)
        def _():
            loss_ref[...] = jnp.zeros_like(loss_ref)
        for h in range(DN_V_HEADS):
            cols = slice(128 * h, 128 * (h + 1))
            og_ref[:, cols] = _dn_gate_tile(o_ref[:, cols], z_ref[:, cols], onw_ref[...]).astype(BF16)
        y = h_ref[...] + jnp.dot(og_ref[...], w_ref[...], preferred_element_type=F32)
        rows = i * ROW_BLOCK + lax.broadcasted_iota(jnp.int32, (ROW_BLOCK, 1), 0)
        diff = jnp.where(rows >= FRONT_PAD + N_META, y - _padded_block(i, None, (ta_ref, tb_ref, tc_ref)), 0.0)
        dy_ref[...] = diff * (1.0 / D_MODEL)
        loss_ref[...] += jnp.sum(diff * diff) * (0.5 / D_MODEL)

    return _row_call("dn_out_fwd", body, T, ROW_BLOCK, [o, z, h1], [w_out, onw],
                     [(1024, F32), (2048, BF16)], [((1, 128), F32)], halos=_token_views(target))


def _dn_out_bwd(dy, o, z, w_out, onw):
    T = o.shape[0]

    def body(dy_ref, o_ref, z_ref, w_ref, onw_ref, do_ref, dz_ref, donw_ref, dog_ref):
        @pl.when(pl.program_id(0) == 0)
        def _():
            donw_ref[...] = jnp.zeros_like(donw_ref)
        dy = dy_ref[...].astype(BF16)
        donw = jnp.zeros((1, DN_HD), F32)
        for half in range(2):
            hcols = slice(1024 * half, 1024 * (half + 1))
            dog_ref[:, hcols] = lax.dot_general(dy, w_ref[hcols, :], NT, preferred_element_type=F32)
        for h in range(DN_V_HEADS):
            cols = slice(128 * h, 128 * (h + 1))
            _, vjp = jax.vjp(_dn_gate_tile, o_ref[:, cols], z_ref[:, cols], onw_ref[...])
            do, dz, dn = vjp(dog_ref[:, cols])
            do_ref[:, cols] = do
            dz_ref[:, cols] = dz
            donw = donw + dn
        donw_ref[...] += donw

    return _row_call("dn_out_bwd", body, T, ROW_BLOCK, [dy, o, z], [w_out, onw],
                     [(2048, F32), (2048, F32)], [((1, DN_HD), F32)], scratch=[pltpu.VMEM((ROW_BLOCK, 2048), F32)])


def _dn_scan_bwd(do, qn, kn, v, bg, ssave, saved):
    T = qn.shape[0]
    nc = T // DN_CHUNK
    chunks = SCAN_BWD_CHUNKS
    ns = nc // chunks
    rows = chunks * DN_CHUNK
    rev = lambda i: ns - 1 - i

    def body(do_ref, q_ref, k_ref, v_ref, bg_ref, ss_ref, inv_ref, uw_ref, vn_ref, qk_ref,
             dq_ref, dk_ref, dv_ref, dbg_ref, ds_ref):
        @pl.when(pl.program_id(0) == 0)
        def _():
            ds_ref[...] = jnp.zeros_like(ds_ref)
        lane32 = lax.broadcasted_iota(jnp.int32, (1, 2 * DN_V_HEADS), 1)
        masks = _chunk_masks()
        dbg = [jnp.zeros((DN_CHUNK, 2 * DN_V_HEADS), F32) for _ in range(chunks)]
        for first in range(0, DN_V_HEADS, SCAN_BWD_GROUP):
            heads = range(first, first + SCAN_BWD_GROUP)
            vjps = []
            for c in range(chunks):
                known = dict(inv=[inv_ref[c, h].astype(F32) for h in heads], uw=[uw_ref[c, h].astype(F32) for h in heads],
                             v_new=[vn_ref[c, h].astype(F32) for h in heads],
                             qk_kk=[qk_ref[c, j].astype(F32) for j in range(first // 2, (first + SCAN_BWD_GROUP) // 2)])
                fn = functools.partial(_dn_chunk_step, masks=masks, known=known)
                vjps.append(jax.vjp(fn, [ss_ref[c, h] for h in heads],
                                    *_dn_chunk_tiles(q_ref, k_ref, v_ref, bg_ref, c, first, SCAN_BWD_GROUP))[1])
            ds = [ds_ref[h] for h in heads]
            for c in reversed(range(chunks)):
                crows = slice(DN_CHUNK * c, DN_CHUNK * (c + 1))
                ds, dq, dk, dv, dbeta, dg = vjps[c]((ds, [do_ref[crows, 128 * h:128 * (h + 1)] for h in heads]))
                for i, h in enumerate(heads):
                    dv_ref[crows, 128 * h:128 * (h + 1)] = dv[i]
                    dbg[c] = dbg[c] + jnp.where(lane32 == h, dbeta[i], 0.0) + jnp.where(lane32 == DN_V_HEADS + h, dg[i], 0.0)
                for i, j in enumerate(range(first // 2, (first + SCAN_BWD_GROUP) // 2)):
                    dq_ref[crows, 128 * j:128 * (j + 1)] = dq[i]
                    dk_ref[crows, 128 * j:128 * (j + 1)] = dk[i]
            for i, h in enumerate(heads):
                ds_ref[h] = ds[i]
        for c in range(chunks):
            dbg_ref[DN_CHUNK * c:DN_CHUNK * (c + 1), :] = dbg[c]

    return pl.pallas_call(
        body, grid=(ns,),
        in_specs=[pl.BlockSpec((rows, 2048), lambda i: (rev(i), 0)),
                  pl.BlockSpec((rows, 1024), lambda i: (rev(i), 0)),
                  pl.BlockSpec((rows, 1024), lambda i: (rev(i), 0)),
                  pl.BlockSpec((rows, 2048), lambda i: (rev(i), 0)),
                  pl.BlockSpec((rows, 32), lambda i: (rev(i), 0)),
                  pl.BlockSpec((chunks, DN_V_HEADS, DN_HD, DN_HD), lambda i: (rev(i), 0, 0, 0)),
                  pl.BlockSpec((chunks, DN_V_HEADS, DN_CHUNK, DN_CHUNK), lambda i: (rev(i), 0, 0, 0)),
                  pl.BlockSpec((chunks, DN_V_HEADS, DN_CHUNK, 2 * DN_HD), lambda i: (rev(i), 0, 0, 0)),
                  pl.BlockSpec((chunks, DN_V_HEADS, DN_CHUNK, DN_HD), lambda i: (rev(i), 0, 0, 0)),
                  pl.BlockSpec((chunks, DN_K_HEADS, 2 * DN_CHUNK, DN_CHUNK), lambda i: (rev(i), 0, 0, 0))],
        out_specs=[pl.BlockSpec((rows, 1024), lambda i: (rev(i), 0)),
                   pl.BlockSpec((rows, 1024), lambda i: (rev(i), 0)),
                   pl.BlockSpec((rows, 2048), lambda i: (rev(i), 0)),
                   pl.BlockSpec((rows, 32), lambda i: (rev(i), 0))],
        out_shape=[jax.ShapeDtypeStruct((T, 1024), F32), jax.ShapeDtypeStruct((T, 1024), F32),
                   jax.ShapeDtypeStruct((T, 2048), F32), jax.ShapeDtypeStruct((T, 32), F32)],
        scratch_shapes=[pltpu.VMEM((DN_V_HEADS, DN_HD, DN_HD), F32)],
        name="dn_scan_bwd", compiler_params=_cparams(),
    )(do, qn, kn, v, bg, ssave, *saved)


def _dn_conv_bwd(dqn, dkn, dv, dbg, qkv, conv_out, ba, conv_w, a_log, dt_bias):
    T = qkv.shape[0]
    rb = ROW_BLOCK // 2
    nr = T // rb

    def body(dq_ref, dk_ref, dv_ref, dbg_ref, x_ref, c_ref, ba_ref, cw_ref, al_ref, dtb_ref,
             dx_ref, dba_ref, dcw_ref, dal_ref, ddtb_ref, carry_ref):
        step = pl.program_id(0)
        i = nr - 1 - step

        @pl.when(step == 0)
        def _():
            carry_ref[...] = jnp.zeros_like(carry_ref)
            dcw_ref[...] = jnp.zeros_like(dcw_ref)
            dal_ref[...] = jnp.zeros_like(dal_ref)
            ddtb_ref[...] = jnp.zeros_like(ddtb_ref)
        for t in range(DN_CONV_W // 128):
            cols = slice(128 * t, 128 * (t + 1))
            w, x = cw_ref[:, cols], x_ref[:, cols]
            if t < DN_K_HEADS:
                dout = dq_ref[:, cols]
            elif t < 2 * DN_K_HEADS:
                dout = dk_ref[:, 128 * (t - 8):128 * (t - 7)]
            else:
                dout = dv_ref[:, 128 * (t - 16):128 * (t - 15)]
            _, vjp = jax.vjp(functools.partial(_dn_post_tile, t=t), c_ref[:, cols])
            (dc,) = vjp(dout)
            nxt = carry_ref[:, cols]
            dx = w[3:4, :] * dc
            dcw_ref[3:4, cols] += jnp.sum(dc * x, axis=0, keepdims=True)
            for s in range(1, DN_CONV_K):
                up = _shift_up(dc, nxt, s)
                dx = dx + w[3 - s:4 - s, :] * up
                dcw_ref[3 - s:4 - s, cols] += jnp.sum(up * x, axis=0, keepdims=True)
            dx_ref[:, cols] = dx
            carry_ref[:, cols] = dc[0:8, :]
        fn = functools.partial(_dn_beta_g, live=_live_rows(i, rb))
        _, vjp = jax.vjp(fn, ba_ref[...], al_ref[...], dtb_ref[...])
        dba, dal, ddtb = vjp((dbg_ref[:, 0:DN_V_HEADS], dbg_ref[:, DN_V_HEADS:]))
        dba_ref[...] = dba
        dal_ref[...] += dal
        ddtb_ref[...] += ddtb

    return _row_call("dn_conv_bwd", body, T, rb, [dqn, dkn, dv, dbg, qkv, conv_out, ba], [conv_w, a_log, dt_bias],
                     [(4096, F32), (32, F32)], [((DN_CONV_K, 4096), F32), ((1, DN_V_HEADS), F32), ((1, DN_V_HEADS), F32)],
                     reverse=True, scratch=[pltpu.VMEM((8, 4096), F32)])


def _dn_in_bwd(dqkv, dz, dba, h1, dy, norm_w, w_in):
    T = h1.shape[0]

    def body(dqkv_ref, dz_ref, dba_ref, h_ref, dy_ref, nw_ref, w_ref, dh_ref, dnw_ref):
        @pl.when(pl.program_id(0) == 0)
        def _():
            dnw_ref[...] = jnp.zeros_like(dnw_ref)
        dxn = (_bdot(dqkv_ref[...], w_ref[:, 0:4096], NT) + _bdot(dz_ref[...], w_ref[:, 4096:6144], NT)
               + _bdot(dba_ref[...], w_ref[:, 6144:6176], NT))
        _, vjp = jax.vjp(_rms, h_ref[...], nw_ref[...])
        dh, dnw = vjp(dxn)
        dh_ref[...] = (dy_ref[...] + dh) * _live_rows(pl.program_id(0), ROW_BLOCK)
        dnw_ref[...] += dnw

    return _row_call("dn_in_bwd", body, T, ROW_BLOCK, [dqkv, dz, dba, h1, dy], [norm_w, w_in],
                     [(1024, F32)], [((1, 1024), F32)])


def _exchange(parts, scatter, name):
    n = len(parts)
    out_shape = [jax.ShapeDtypeStruct(p.shape if sc else (N_DEV,) + p.shape, p.dtype) for p, sc in zip(parts, scatter)]

    def body(*refs):
        ins, outs = refs[:n], refs[n:2 * n]
        send_sems, recv_sems, local_sems = refs[2 * n:]
        x, y, c = lax.axis_index("x"), lax.axis_index("y"), lax.axis_index("c")
        me = 4 * x + 2 * y + c
        peers = []
        for k in range(1, N_DEV):
            px = 1 - x if k & 4 else x
            py = 1 - y if k & 2 else y
            pc = 1 - c if k & 1 else c
            peers.append(((px, py, pc), 4 * px + 2 * py + pc))

        def src(a, idx):
            return ins[a].at[idx] if scatter[a] else ins[a]

        local = [pltpu.make_async_copy(src(a, me), outs[a].at[me], local_sems.at[a]) for a in range(n)]
        for cp in local:
            cp.start()
        for a in range(n):
            for k, (dev, idx) in enumerate(peers):
                pltpu.make_async_remote_copy(
                    src_ref=src(a, idx), dst_ref=outs[a].at[me], send_sem=send_sems.at[a, k], recv_sem=recv_sems.at[a, k],
                    device_id=dev, device_id_type=pl.DeviceIdType.MESH).start()
        for a in range(n):
            for k, (dev, idx) in enumerate(peers):
                pltpu.make_async_remote_copy(
                    src_ref=src(a, idx), dst_ref=outs[a].at[idx], send_sem=send_sems.at[a, k], recv_sem=recv_sems.at[a, k],
                    device_id=dev, device_id_type=pl.DeviceIdType.MESH).wait()
        for cp in local:
            cp.wait()

    hbm = pl.BlockSpec(memory_space=pltpu.HBM)
    return pl.pallas_call(
        body, out_shape=out_shape, in_specs=[hbm] * n, out_specs=[hbm] * n,
        scratch_shapes=[pltpu.SemaphoreType.DMA((n, N_DEV - 1)), pltpu.SemaphoreType.DMA((n, N_DEV - 1)),
                        pltpu.SemaphoreType.DMA((n,))],
        name=name,
    )(*parts)


def _gather_two_level(parts, name):
    n = len(parts)
    out_shape = [jax.ShapeDtypeStruct((N_DEV,) + p.shape, p.dtype) for p in parts]

    def body(*refs):
        ins, outs = refs[:n], refs[n:2 * n]
        send_sems, recv_sems, local_sems = refs[2 * n:]
        x, y, c = lax.axis_index("x"), lax.axis_index("y"), lax.axis_index("c")
        idx = lambda px, py, pc: 4 * px + 2 * py + pc
        me, sibling = (x, y, c), (x, y, 1 - c)
        chips = [(1 - x, y), (x, 1 - y), (1 - x, 1 - y)]

        def copy(a, k, block, to, src=None):
            slot = outs[a].at[idx(*block)]
            return pltpu.make_async_remote_copy(
                src_ref=slot if src is None else src, dst_ref=slot, send_sem=send_sems.at[a, k], recv_sem=recv_sems.at[a, k],
                device_id=to, device_id_type=pl.DeviceIdType.MESH)

        local = [pltpu.make_async_copy(ins[a], outs[a].at[idx(*me)], local_sems.at[a]) for a in range(n)]
        for cp in local:
            cp.start()
        sent = []
        for a in range(n):
            sent.append(copy(a, 0, me, sibling, src=ins[a]))
            sent += [copy(a, 1 + j, me, (*chip, c), src=ins[a]) for j, chip in enumerate(chips)]
        for cp in sent:
            cp.start()
        for a in range(n):
            for j, chip in enumerate(chips):
                copy(a, 1 + j, (*chip, c), me).wait_recv()
                passed = copy(a, 4 + j, (*chip, c), sibling)
                passed.start()
                sent.append(passed)
        for a in range(n):
            copy(a, 0, sibling, me).wait_recv()
            for j, chip in enumerate(chips):
                copy(a, 4 + j, (*chip, 1 - c), me).wait_recv()
        for cp in sent:
            cp.wait_send()
        for cp in local:
            cp.wait()

    hbm = pl.BlockSpec(memory_space=pltpu.HBM)
    return pl.pallas_call(
        body, out_shape=out_shape, in_specs=[hbm] * n, out_specs=[hbm] * n,
        scratch_shapes=[pltpu.SemaphoreType.DMA((n, N_DEV - 1)), pltpu.SemaphoreType.DMA((n, N_DEV - 1)),
                        pltpu.SemaphoreType.DMA((n,))],
        name=name,
    )(*parts)


def _swap_with_sibling(parts, name):
    n = len(parts)

    def body(*refs):
        ins, outs = refs[:n], refs[n:2 * n]
        send_sems, recv_sems = refs[2 * n:]
        x, y, c = lax.axis_index("x"), lax.axis_index("y"), lax.axis_index("c")
        copies = [pltpu.make_async_remote_copy(
            src_ref=ins[a].at[1 - c], dst_ref=outs[a], send_sem=send_sems.at[a], recv_sem=recv_sems.at[a],
            device_id=(x, y, 1 - c), device_id_type=pl.DeviceIdType.MESH) for a in range(n)]
        for cp in copies:
            cp.start()
        for cp in copies:
            cp.wait()

    hbm = pl.BlockSpec(memory_space=pltpu.HBM)
    return pl.pallas_call(
        body, out_shape=[jax.ShapeDtypeStruct(p.shape[1:], p.dtype) for p in parts], in_specs=[hbm] * n, out_specs=[hbm] * n,
        scratch_shapes=[pltpu.SemaphoreType.DMA((n,)), pltpu.SemaphoreType.DMA((n,))],
        name=name,
    )(*parts)


def _pair_sum(a, b, name):
    R, C = a.shape
    rb = _adam_rows(R)

    def body(a_ref, b_ref, o_ref):
        o_ref[...] = (a_ref[...].astype(F32) + b_ref[...].astype(F32)).astype(BF16)

    blk = pl.BlockSpec((rb, C), lambda i: (i, 0))
    return pl.pallas_call(body, grid=(R // rb,), in_specs=[blk, blk], out_specs=blk,
                          out_shape=jax.ShapeDtypeStruct((R, C), BF16), name=name, compiler_params=_cparams())(a, b)


def _exchange_chips(parts, name):
    n = len(parts)
    n_chips = N_DEV // 2

    def body(*refs):
        ins, outs = refs[:n], refs[n:2 * n]
        send_sems, recv_sems, local_sems = refs[2 * n:]
        x, y, c = lax.axis_index("x"), lax.axis_index("y"), lax.axis_index("c")
        mine = 2 * x + y
        chips = [(1 - x, y), (x, 1 - y), (1 - x, 1 - y)]
        local = [pltpu.make_async_copy(ins[a].at[mine], outs[a].at[mine], local_sems.at[a]) for a in range(n)]
        for cp in local:
            cp.start()
        for a in range(n):
            for k, (px, py) in enumerate(chips):
                pltpu.make_async_remote_copy(
                    src_ref=ins[a].at[2 * px + py], dst_ref=outs[a].at[mine], send_sem=send_sems.at[a, k],
                    recv_sem=recv_sems.at[a, k], device_id=(px, py, c), device_id_type=pl.DeviceIdType.MESH).start()
        for a in range(n):
            for k, (px, py) in enumerate(chips):
                pltpu.make_async_remote_copy(
                    src_ref=ins[a].at[2 * px + py], dst_ref=outs[a].at[2 * px + py], send_sem=send_sems.at[a, k],
                    recv_sem=recv_sems.at[a, k], device_id=(px, py, c), device_id_type=pl.DeviceIdType.MESH).wait()
        for cp in local:
            cp.wait()

    hbm = pl.BlockSpec(memory_space=pltpu.HBM)
    return pl.pallas_call(
        body, out_shape=[jax.ShapeDtypeStruct(p.shape, p.dtype) for p in parts], in_specs=[hbm] * n, out_specs=[hbm] * n,
        scratch_shapes=[pltpu.SemaphoreType.DMA((n, n_chips - 1)), pltpu.SemaphoreType.DMA((n, n_chips - 1)),
                        pltpu.SemaphoreType.DMA((n,))],
        name=name,
    )(*parts)


def _adam_rows(rows):
    for rb in (128, 64, 40, 16, 8):
        if rows % rb == 0:
            return rb
    return rows


def _adamw(stack, w, m, v, name):
    R, C = w.shape
    rb = _adam_rows(R)
    slots = stack.shape[0]

    def body(s_ref, w_ref, m_ref, v_ref, g_ref, d_ref, nm_ref, nv_ref):
        g = s_ref[0].astype(F32)
        for s in range(1, slots):
            g = g + s_ref[s].astype(F32)
        nm = ADAM_B1 * m_ref[...] + (1.0 - ADAM_B1) * g
        nv = ADAM_B2 * v_ref[...] + (1.0 - ADAM_B2) * (g * g)
        m_hat = nm / (1.0 - ADAM_B1 ** ADAM_STEP)
        v_hat = nv / (1.0 - ADAM_B2 ** ADAM_STEP)
        g_ref[...] = g
        d_ref[...] = -ADAM_LR * (m_hat / (jnp.sqrt(v_hat) + ADAM_EPS) + ADAM_WD * w_ref[...])
        nm_ref[...] = nm
        nv_ref[...] = nv

    blk = pl.BlockSpec((rb, C), lambda i: (i, 0))
    return pl.pallas_call(
        body, grid=(R // rb,),
        in_specs=[pl.BlockSpec((slots, rb, C), lambda i: (0, i, 0)), blk, blk, blk],
        out_specs=[blk] * 4, out_shape=[jax.ShapeDtypeStruct((R, C), F32)] * 4,
        name=name, compiler_params=_cparams(),
    )(stack, w, m, v)


def _pad_rows8(a):
    return jnp.concatenate([a, jnp.zeros((8 - a.shape[0], a.shape[1]), a.dtype)], axis=0) if a.shape[0] < 8 else a


def _pack_small(norm_w, qnw, knw, sinks, a_log, dt_bias, onw, extra):
    z = lambda n: jnp.zeros((1, n), F32)
    row = jnp.concatenate([norm_w, qnw, knw, sinks, a_log, dt_bias, z(80), onw, extra, z(512)], axis=1)
    return row.reshape(16, 128)


def _unpack_small(p):
    row = p.reshape(1, 2048)
    cut = lambda a, n: row[:, a:a + n]
    return (cut(0, 1024), cut(1024, 64), cut(1088, 64), cut(1152, 16), cut(1168, 16), cut(1184, 16), cut(1280, 128),
            cut(1408, 128))


def _pack_rows(w_in_a, w_in_d, w_out_a, w_out_d, meta, conv, dn_norm):
    a = jnp.concatenate([w_in_a, w_in_d], axis=1)
    b = jnp.concatenate([w_out_a, w_out_d], axis=0)
    c = jnp.concatenate([meta, conv.reshape(16, 128), _pad_rows8(dn_norm)], axis=0)
    return a, b, c


def _unpack_rows(a, b, c):
    return (a[:, :288], a[:, 288:], b[:128], b[128:], c[:16], c[16:32].reshape(4, 512), c[32:33])


def _local_step(x, front, target, w):
    xn0, q, kv, gate = _attn_in_fwd(x, front, w["attn_norm_w"], w["attn_w_in"])
    o = _attn_core_fwd(q, kv, w["attn_sinks"], w["attn_q_norm_w"], w["attn_k_norm_w"])
    h1 = _attn_out_fwd(o, gate, x, front, w["attn_w_out"])
    xn1, qkv, z, ba = _dn_in_fwd(h1, w["dn_norm_w"], w["dn_w_in"])
    qn, kn, v, bg, conv_out = _dn_conv_fwd(qkv, ba, w["dn_conv_w"], w["dn_a_log"], w["dn_dt_bias"])
    o_dn, ssave, *saved = _dn_scan_fwd(qn, kn, v, bg)
    dy, og_dn, loss = _dn_out_fwd(o_dn, z, h1, target, w["dn_w_out"], w["dn_o_norm_w"])

    g = {}
    do_dn, dz, g["dn_o_norm_w"] = _dn_out_bwd(dy, o_dn, z, w["dn_w_out"], w["dn_o_norm_w"])
    g["dn_w_out"] = _wgrad(og_dn, dy, 1024, "wgrad_dn_out")
    dqn, dkn, dv, dbg = _dn_scan_bwd(do_dn, qn, kn, v, bg, ssave, saved)
    dqkv, dba, g["dn_conv_w"], g["dn_a_log"], g["dn_dt_bias"] = _dn_conv_bwd(
        dqn, dkn, dv, dbg, qkv, conv_out, ba, w["dn_conv_w"], w["dn_a_log"], w["dn_dt_bias"])
    dh1, g["dn_norm_w"] = _dn_in_bwd(dqkv, dz, dba, h1, dy, w["dn_norm_w"], w["dn_w_in"])
    g["dn_w_in"] = jnp.concatenate([_wgrad(xn1, dqkv, 1024, "wgrad_dn_qkv"), _wgrad(xn1, dz, 1024, "wgrad_dn_z"),
                                    _wgrad(xn1, dba, 32, "wgrad_dn_ba")], axis=1)
    do, dgate, g["attn_w_out"] = _attn_out_bwd(dh1, o, gate, w["attn_w_out"])
    dq, dkv, g["attn_sinks"], g["attn_q_norm_w"], g["attn_k_norm_w"] = _attn_core_bwd(
        do, q, kv, w["attn_sinks"], w["attn_q_norm_w"], w["attn_k_norm_w"])
    dh0, g["attn_norm_w"] = _attn_in_bwd(dq, dkv, dgate, x, front, dh1, w["attn_norm_w"], w["attn_w_in"])
    g["attn_w_in"] = jnp.concatenate([_wgrad(xn0, dq, 1024, "wgrad_attn_q"), _wgrad(xn0, dkv, 256, "wgrad_attn_kv"),
                                      _wgrad(xn0, dgate, 1024, "wgrad_attn_gate")], axis=1)
    return loss, dh0, g


WEIGHTS = ['meta_tokens', 'attn_norm_w', 'attn_w_in', 'attn_q_norm_w', 'attn_k_norm_w', 'attn_sinks', 'attn_w_out',
           'dn_norm_w', 'dn_w_in', 'dn_conv_w', 'dn_a_log', 'dn_dt_bias', 'dn_o_norm_w', 'dn_w_out']
SMALL = ['attn_norm_w', 'attn_q_norm_w', 'attn_k_norm_w', 'attn_sinks', 'dn_a_log', 'dn_dt_bias', 'dn_o_norm_w']


def kernel(x, meta_tokens, attn_norm_w, attn_w_in, attn_q_norm_w, attn_k_norm_w, attn_sinks, attn_w_out, dn_norm_w, dn_w_in, dn_conv_w, dn_a_log, dn_dt_bias, dn_o_norm_w, dn_w_out, loss_target, m_meta_tokens, m_attn_norm_w, m_attn_w_in, m_attn_q_norm_w, m_attn_k_norm_w, m_attn_sinks, m_attn_w_out, m_dn_norm_w, m_dn_w_in, m_dn_conv_w, m_dn_a_log, m_dn_dt_bias, m_dn_o_norm_w, m_dn_w_out, v_meta_tokens, v_attn_norm_w, v_attn_w_in, v_attn_q_norm_w, v_attn_k_norm_w, v_attn_sinks, v_attn_w_out, v_dn_norm_w, v_dn_w_in, v_dn_conv_w, v_dn_a_log, v_dn_dt_bias, v_dn_o_norm_w, v_dn_w_out):
    shard = dict(meta_tokens=meta_tokens, attn_norm_w=attn_norm_w, attn_w_in=attn_w_in[0], attn_q_norm_w=attn_q_norm_w,
                 attn_k_norm_w=attn_k_norm_w, attn_sinks=attn_sinks, attn_w_out=attn_w_out[0], dn_norm_w=dn_norm_w,
                 dn_w_in=dn_w_in[0], dn_conv_w=dn_conv_w[0], dn_a_log=dn_a_log, dn_dt_bias=dn_dt_bias,
                 dn_o_norm_w=dn_o_norm_w, dn_w_out=dn_w_out[0])
    mom_m = dict(meta_tokens=m_meta_tokens, attn_norm_w=m_attn_norm_w, attn_w_in=m_attn_w_in[0], attn_q_norm_w=m_attn_q_norm_w,
                 attn_k_norm_w=m_attn_k_norm_w, attn_sinks=m_attn_sinks, attn_w_out=m_attn_w_out[0], dn_norm_w=m_dn_norm_w,
                 dn_w_in=m_dn_w_in[0], dn_conv_w=m_dn_conv_w[0], dn_a_log=m_dn_a_log, dn_dt_bias=m_dn_dt_bias,
                 dn_o_norm_w=m_dn_o_norm_w, dn_w_out=m_dn_w_out[0])
    mom_v = dict(meta_tokens=v_meta_tokens, attn_norm_w=v_attn_norm_w, attn_w_in=v_attn_w_in[0], attn_q_norm_w=v_attn_q_norm_w,
                 attn_k_norm_w=v_attn_k_norm_w, attn_sinks=v_attn_sinks, attn_w_out=v_attn_w_out[0], dn_norm_w=v_dn_norm_w,
                 dn_w_in=v_dn_w_in[0], dn_conv_w=v_dn_conv_w[0], dn_a_log=v_dn_a_log, dn_dt_bias=v_dn_dt_bias,
                 dn_o_norm_w=v_dn_o_norm_w, dn_w_out=v_dn_w_out[0])

    def rows_of(d):
        return _pack_rows(d["attn_w_in"], d["dn_w_in"], d["attn_w_out"], d["dn_w_out"], d["meta_tokens"], d["dn_conv_w"],
                          d["dn_norm_w"])

    def small_of(d, extra):
        return _pack_small(*[d[k] for k in SMALL], extra)

    wa, wb, wc = rows_of(shard)
    ga, gb, gc = _gather_two_level([wa.astype(BF16), wb.astype(BF16), wc], "gather_weights")
    full = {k: shard[k] for k in SMALL}
    full["attn_w_in"] = ga[:, :, :288].transpose(1, 0, 2).reshape(1024, 2304)
    full["dn_w_in"] = ga[:, :, 288:].transpose(1, 0, 2).reshape(1024, 6176)
    full["attn_w_out"] = gb[:, :128].reshape(1024, 1024)
    full["dn_w_out"] = gb[:, 128:].reshape(2048, 1024)
    meta_full = gc[:, :16].transpose(1, 0, 2).reshape(N_META, 1024)
    full["dn_conv_w"] = gc[:, 16:32].reshape(N_DEV, 4, 512).transpose(1, 0, 2).reshape(4, 4096)
    full["dn_norm_w"] = gc[:, 32].reshape(1, 1024)

    seq = x.shape[1]
    front = jnp.concatenate([jnp.zeros((FRONT_PAD, D_MODEL), F32), meta_full], axis=0)
    loss, dh0, g = _local_step(x[0], front, loss_target[0], full)
    grad_x = dh0[ATTN_BLOCK:ATTN_BLOCK + seq][None]
    g["meta_tokens"] = dh0[FRONT_PAD:ATTN_BLOCK]

    pa = jnp.concatenate([g["attn_w_in"].reshape(1024, N_DEV, 288), g["dn_w_in"].reshape(1024, N_DEV, 772)],
                         axis=2).transpose(1, 0, 2)
    pb = jnp.concatenate([g["attn_w_out"].reshape(N_DEV, 128, 1024), g["dn_w_out"].reshape(N_DEV, 256, 1024)], axis=1)
    dn_norm8 = jnp.concatenate([g["dn_norm_w"].reshape(N_DEV, 1, 128), jnp.zeros((N_DEV, 7, 128), F32)], axis=1)
    pc = jnp.concatenate([g["meta_tokens"].reshape(N_META, N_DEV, 128).transpose(1, 0, 2),
                          g["dn_conv_w"].reshape(4, N_DEV, 512).transpose(1, 0, 2).reshape(N_DEV, 16, 128), dn_norm8], axis=1)
    ps = small_of(g, loss)
    c = lax.axis_index("c")
    by_core = lambda p: p.astype(BF16).reshape((N_DEV // 2, 2) + p.shape[1:]).swapaxes(0, 1)
    pa2, pb2 = by_core(pa), by_core(pb)
    ra, rb_ = _swap_with_sibling([pa2, pb2], "swap_grads")
    own = lambda p2: lax.dynamic_index_in_dim(p2, c, axis=0, keepdims=False)
    flat = lambda t: t.reshape((-1,) + t.shape[2:])
    sa = _pair_sum(flat(own(pa2)), flat(ra), "pair_sum_a").reshape(ra.shape)
    sb = _pair_sum(flat(own(pb2)), flat(rb_), "pair_sum_b").reshape(rb_.shape)
    xa, xb = _exchange_chips([sa, sb], "exchange_grads")
    xc, xs = _exchange([pc, ps], [True, False], "exchange_small")

    out = {}
    ma, mb, mc = rows_of(mom_m)
    va, vb, vc = rows_of(mom_v)
    ra = _adamw(xa, wa, ma, va, "adamw_a")
    rb = _adamw(xb, wb, mb, vb, "adamw_b")
    rc = _adamw(xc, wc, mc, vc, "adamw_c")
    zero = jnp.zeros((1, 128), F32)
    rs = _adamw(xs, small_of(shard, zero), small_of(mom_m, zero), small_of(mom_v, zero), "adamw_small")
    row_names = ["attn_w_in", "dn_w_in", "attn_w_out", "dn_w_out", "meta_tokens", "dn_conv_w", "dn_norm_w"]
    lead = {"attn_w_in", "dn_w_in", "attn_w_out", "dn_w_out", "dn_conv_w"}
    for kind in range(4):
        vals = dict(zip(row_names, _unpack_rows(ra[kind], rb[kind], rc[kind])))
        small = _unpack_small(rs[kind])
        vals.update(dict(zip(SMALL, small[:7])))
        if kind == 0:
            loss_total = small[7][0, 0]
        out[kind] = [vals[k][None] if k in lead else vals[k] for k in WEIGHTS]
    return (loss_total, grad_x, *out[0], *out[1], *out[2], *out[3])
```

```python
import functools
import math

import jax
import jax.numpy as jnp
from jax import lax
from jax.experimental import pallas as pl
from jax.experimental.pallas import tpu as pltpu

F32, BF16 = jnp.float32, jnp.bfloat16

D_MODEL = 1024
N_META = 16
NORM_EPS = 1e-6
ATTN_HEADS, ATTN_KV_HEADS, ATTN_GROUPS, ATTN_HD = 16, 2, 8, 64
ATTN_BLOCK = 128
FRONT_PAD = ATTN_BLOCK - N_META
DN_HD, DN_K_HEADS, DN_V_HEADS = 128, 8, 16
DN_CHUNK = 128
TRI_BLOCK = 64
SCAN_CHUNKS = 3
SCAN_BWD_CHUNKS = 1
SCAN_FWD_GROUP = 8
SCAN_BWD_GROUP = 8
DN_KEY_W, DN_VAL_W = 1024, 2048
DN_CONV_W = 2 * DN_KEY_W + DN_VAL_W
DN_CONV_K = 4
N_DEV = 8
ROW_BLOCK = 384
WGRAD_ROWS = 1376
VMEM_LIMIT = 56 * 1024 * 1024
NEG = -1e30

ADAM_LR, ADAM_B1, ADAM_B2, ADAM_EPS, ADAM_WD, ADAM_STEP = 0.001, 0.9, 0.999, 1e-08, 0.01, 10

NT = (((1,), (1,)), ((), ()))
TN = (((0,), (0,)), ((), ()))


def _cparams(sem=("arbitrary",)):
    return pltpu.CompilerParams(dimension_semantics=sem, vmem_limit_bytes=VMEM_LIMIT)


def _rms(x, w):
    return x * lax.rsqrt(jnp.mean(x * x, axis=-1, keepdims=True) + NORM_EPS) * w


def _silu(x):
    return x * jax.nn.sigmoid(x)


def _softplus(x):
    return jnp.maximum(x, 0.0) + jnp.log(1.0 + jnp.exp(-jnp.abs(x)))


NN = (((1,), (0,)), ((), ()))


def _mm(a, b, dims):
    return lax.dot_general(a.astype(BF16), b.astype(BF16), dims, preferred_element_type=F32)


@functools.partial(jax.custom_vjp, nondiff_argnums=(2,))
def _bdot_vjp(a, b, dims):
    return _mm(a, b, dims)


def _bdot_fwd(a, b, dims):
    a16, b16 = a.astype(BF16), b.astype(BF16)
    return _mm(a16, b16, dims), (a16, b16, jnp.zeros((), a.dtype), jnp.zeros((), b.dtype))


def _bdot_bwd(dims, res, g):
    a16, b16, ta, tb = res
    g16 = g.astype(BF16)
    if dims == NN:
        da, db = _mm(g16, b16, NT), _mm(a16, g16, TN)
    elif dims == NT:
        da, db = _mm(g16, b16, NN), _mm(g16, a16, TN)
    else:
        da, db = _mm(b16, g16, NT), _mm(a16, g16, NN)
    return da.astype(ta.dtype), db.astype(tb.dtype)


_bdot_vjp.defvjp(_bdot_fwd, _bdot_bwd)


def _bdot(a, b, dims=NN):
    return _bdot_vjp(a, b, dims)


def _row_call(name, body, n_rows, rb, rows, consts, outs, accs=(), reverse=False, scratch=(), halos=()):
    n = n_rows // rb
    assert n * rb == n_rows
    idx = (lambda i: (n - 1 - i, 0)) if reverse else (lambda i: (i, 0))
    in_specs = [pl.BlockSpec((rb, a.shape[1]), idx) for a in rows]
    in_specs += [pl.BlockSpec((hr, a.shape[1]), fn) for a, hr, fn in halos]
    in_specs += [pl.BlockSpec(c.shape, functools.partial(lambda i, nd: (0,) * nd, nd=c.ndim)) for c in consts]
    out_specs = [pl.BlockSpec((rb, c), idx) for c, _ in outs]
    out_specs += [pl.BlockSpec(s, functools.partial(lambda i, nd: (0,) * nd, nd=len(s))) for s, _ in accs]
    out_shape = [jax.ShapeDtypeStruct((n_rows, c), dt) for c, dt in outs]
    out_shape += [jax.ShapeDtypeStruct(s, dt) for s, dt in accs]
    return pl.pallas_call(
        body, grid=(n,), in_specs=in_specs, out_specs=out_specs, out_shape=out_shape,
        scratch_shapes=list(scratch), name=name, compiler_params=_cparams(),
    )(*rows, *[a for a, _, _ in halos], *consts)


def _token_views(x):
    per = ROW_BLOCK // ATTN_BLOCK
    return [(x, ATTN_BLOCK, functools.partial(lambda i, k: (jnp.maximum(per * i - 1 + k, 0), 0), k=k)) for k in range(per)]


def _padded_block(i, front, views):
    first = jnp.where(i == 0, front, views[0][...]) if front is not None else views[0][...]
    return jnp.concatenate([first] + [v[...] for v in views[1:]], axis=0)


def _attn_in_fwd(x, front, norm_w, w_in):
    T = x.shape[0] + ATTN_BLOCK

    def body(xa_ref, xb_ref, xc_ref, front_ref, nw_ref, w_ref, xn_ref, q_ref, kv_ref, gate_ref):
        h = _padded_block(pl.program_id(0), front_ref[...], (xa_ref, xb_ref, xc_ref))
        xn = _rms(h, nw_ref[...]).astype(BF16)
        xn_ref[...] = xn
        q_ref[...] = jnp.dot(xn, w_ref[:, 0:1024], preferred_element_type=F32)
        kv_ref[...] = jnp.dot(xn, w_ref[:, 1024:1280], preferred_element_type=F32)
        gate_ref[...] = jnp.dot(xn, w_ref[:, 1280:2304], preferred_element_type=F32)

    return _row_call("attn_in_fwd", body, T, ROW_BLOCK, [], [front, norm_w, w_in],
                     [(1024, BF16), (1024, F32), (256, F32), (1024, F32)], halos=_token_views(x))


def _attn_bias(n, j):
    C, R = 2 * ATTN_BLOCK + N_META, ATTN_GROUPS * ATTN_BLOCK
    c = lax.broadcasted_iota(jnp.int32, (C, R), 0)
    r = lax.broadcasted_iota(jnp.int32, (C, R), 1)
    ql = r & (ATTN_BLOCK - 1)
    is_meta = c >= 2 * ATTN_BLOCK
    dist_band = ATTN_BLOCK + ql - c
    cmin = jnp.maximum(0, 2 * ATTN_BLOCK - ATTN_BLOCK * n)
    valid_band = (c >= cmin) & (dist_band >= 0) & (dist_band < ATTN_BLOCK)
    dist_meta = ATTN_BLOCK * n + ql - FRONT_PAD - (c - 2 * ATTN_BLOCK)
    valid = (is_meta & (dist_meta >= 0)) | (jnp.logical_not(is_meta) & valid_band)
    dist = jnp.minimum(jnp.where(is_meta, dist_meta, dist_band), ATTN_BLOCK).astype(F32)
    rr = lax.broadcasted_iota(jnp.int32, (1, R), 1)
    head = (rr >> 7).astype(F32) + float(ATTN_GROUPS * j + 1)
    slope = jnp.exp(head * (-0.5 * math.log(2.0)))
    return jnp.where(valid, slope * dist, -NEG)


def _attn_tables(n, refresh, bias_ref):
    @pl.when(refresh)
    def _():
        for j in range(ATTN_KV_HEADS):
            bias_ref[j] = _attn_bias(n, j)


def _attn_table_scratch():
    return [pltpu.VMEM((ATTN_KV_HEADS, 2 * ATTN_BLOCK + N_META, ATTN_GROUPS * ATTN_BLOCK), F32)]


def _attn_groups(q_t, k, v, sinkrow, qnw_col, knw, bias, late_norm=True):
    n = range(len(q_t))
    qn = [q_t[j] * (lax.rsqrt(jnp.mean(q_t[j] * q_t[j], axis=0, keepdims=True) + NORM_EPS) * (ATTN_HD ** -0.5)) * qnw_col
          for j in n]
    kn = [_rms(k[j], knw) for j in n]
    s = [_bdot(kn[j], qn[j]) - bias[j] for j in n]
    m = [lax.stop_gradient(jnp.maximum(jnp.max(s[j], axis=0, keepdims=True), sinkrow[j])) for j in n]
    e = [jnp.exp(s[j] - m[j]) for j in n]
    inv = [1.0 / (jnp.sum(e[j], axis=0, keepdims=True) + jnp.exp(sinkrow[j] - m[j])) for j in n]
    if late_norm:
        return [_bdot(v[j], e[j], TN) * inv[j] for j in n]
    return [_bdot(v[j], e[j] * inv[j], TN) for j in n]


def _sink_row(sinks_ref, j):
    rr = lax.broadcasted_iota(jnp.int32, (1, ATTN_GROUPS * ATTN_BLOCK), 1) >> 7
    row = jnp.zeros((1, ATTN_GROUPS * ATTN_BLOCK), F32)
    for hl in range(ATTN_GROUPS):
        row = jnp.where(rr == hl, sinks_ref[0, ATTN_GROUPS * j + hl], row)
    return row


def _heads_to_lanes(ref, j):
    return jnp.concatenate([ref[:, ATTN_HD * h:ATTN_HD * (h + 1)].T
                            for h in range(ATTN_GROUPS * j, ATTN_GROUPS * (j + 1))], axis=1)


def _lanes_to_heads(ref, j, x_t):
    for hl in range(ATTN_GROUPS):
        h = ATTN_GROUPS * j + hl
        ref[:, ATTN_HD * h:ATTN_HD * (h + 1)] = x_t[:, ATTN_BLOCK * hl:ATTN_BLOCK * (hl + 1)].T


def _attn_kv_tiles(kvp_ref, kvc_ref, kvm_ref, j):
    ksl = slice(ATTN_HD * j, ATTN_HD * (j + 1))
    vsl = slice(128 + ATTN_HD * j, 128 + ATTN_HD * (j + 1))
    k = jnp.concatenate([kvp_ref[:, ksl], kvc_ref[:, ksl], kvm_ref[FRONT_PAD:, ksl]], axis=0)
    v = jnp.concatenate([kvp_ref[:, vsl], kvc_ref[:, vsl], kvm_ref[FRONT_PAD:, vsl]], axis=0)
    return k, v


def _attn_core_fwd(q, kv, sinks, qnw, knw):
    T = q.shape[0]
    nb = T // ATTN_BLOCK

    def body(sinks_ref, q_ref, kvc_ref, kvp_ref, kvm_ref, qnw_ref, knw_ref, o_ref, bias_ref):
        n = pl.program_id(0)
        _attn_tables(n, n <= 2, bias_ref)
        kvh = range(ATTN_KV_HEADS)
        kv_tiles = [_attn_kv_tiles(kvp_ref, kvc_ref, kvm_ref, j) for j in kvh]
        o_t = _attn_groups([_heads_to_lanes(q_ref, j) for j in kvh], [t[0] for t in kv_tiles], [t[1] for t in kv_tiles],
                           [_sink_row(sinks_ref, j) for j in kvh], qnw_ref[...], knw_ref[...], [bias_ref[j] for j in kvh])
        for j in kvh:
            _lanes_to_heads(o_ref, j, o_t[j])

    return pl.pallas_call(
        body, grid=(nb,),
        in_specs=[pl.BlockSpec(memory_space=pltpu.SMEM),
                  pl.BlockSpec((ATTN_BLOCK, 1024), lambda i: (i, 0)),
                  pl.BlockSpec((ATTN_BLOCK, 256), lambda i: (i, 0)),
                  pl.BlockSpec((ATTN_BLOCK, 256), lambda i: (jnp.maximum(i - 1, 0), 0)),
                  pl.BlockSpec((ATTN_BLOCK, 256), lambda i: (0, 0)),
                  pl.BlockSpec((ATTN_HD, 1), lambda i: (0, 0)),
                  pl.BlockSpec((1, ATTN_HD), lambda i: (0, 0))],
        out_specs=pl.BlockSpec((ATTN_BLOCK, 1024), lambda i: (i, 0)),
        out_shape=jax.ShapeDtypeStruct((T, 1024), F32),
        scratch_shapes=_attn_table_scratch(),
        name="attn_core_fwd", compiler_params=_cparams(),
    )(sinks, q, kv, kv, kv, qnw.reshape(ATTN_HD, 1), knw)


def _attn_out_fwd(o, gate, x, front, w_out):
    T = o.shape[0]

    def body(o_ref, g_ref, xa_ref, xb_ref, xc_ref, front_ref, w_ref, h1_ref):
        h = _padded_block(pl.program_id(0), front_ref[...], (xa_ref, xb_ref, xc_ref))
        og = o_ref[...] * _silu(g_ref[...])
        h1_ref[...] = h + _bdot(og, w_ref[...])

    return _row_call("attn_out_fwd", body, T, ROW_BLOCK, [o, gate], [front, w_out], [(1024, F32)], halos=_token_views(x))[0]


def _wgrad(xn, du, cg, name):
    T, kdim = xn.shape
    cdim = du.shape[1]
    rows = WGRAD_ROWS if T % WGRAD_ROWS == 0 else ROW_BLOCK
    nr, nc = T // rows, cdim // cg
    assert nc * cg == cdim

    def body(x_ref, du_ref, dw_ref):
        @pl.when(pl.program_id(1) == 0)
        def _():
            dw_ref[...] = jnp.zeros_like(dw_ref)
        dw_ref[...] += _bdot(x_ref[...], du_ref[...], TN)

    return pl.pallas_call(
        body, grid=(nc, nr),
        in_specs=[pl.BlockSpec((rows, kdim), lambda j, i: (i, 0)),
                  pl.BlockSpec((rows, cg), lambda j, i: (i, j))],
        out_specs=pl.BlockSpec((kdim, cg), lambda j, i: (0, j)),
        out_shape=jax.ShapeDtypeStruct((kdim, cdim), F32),
        name=name, compiler_params=_cparams(("arbitrary", "arbitrary")),
    )(xn, du)


def _attn_out_bwd(dh1, o, gate, w_out):
    T = o.shape[0]

    def body(dh_ref, o_ref, g_ref, w_ref, do_ref, dg_ref, dw_ref):
        @pl.when(pl.program_id(0) == 0)
        def _():
            dw_ref[...] = jnp.zeros_like(dw_ref)
        dh = dh_ref[...]
        dog = _bdot(dh, w_ref[...], NT)
        og, vjp = jax.vjp(lambda o_, g_: o_ * _silu(g_), o_ref[...], g_ref[...])
        do, dg = vjp(dog)
        do_ref[...] = do
        dg_ref[...] = dg
        dw_ref[...] += _bdot(og, dh, TN)

    return _row_call("attn_out_bwd", body, T, ROW_BLOCK, [dh1, o, gate], [w_out],
                     [(1024, F32), (1024, F32)], [((1024, 1024), F32)])


def _attn_core_bwd(do, q, kv, sinks, qnw, knw):
    T = q.shape[0]
    nb = T // ATTN_BLOCK
    rev = lambda i: nb - 1 - i

    def body(sinks_ref, do_ref, q_ref, kvc_ref, kvp_ref, kvm_ref, qnw_ref, knw_ref,
             dq_ref, dkv_ref, dsinks_ref, dqnw_ref, dknw_ref, carry_ref, meta_ref, bias_ref):
        step = pl.program_id(0)
        n = rev(step)
        _attn_tables(n, (step == 0) | (n <= 1), bias_ref)

        @pl.when(step == 0)
        def _():
            carry_ref[...] = jnp.zeros_like(carry_ref)
            meta_ref[...] = jnp.zeros_like(meta_ref)
            dsinks_ref[...] = jnp.zeros_like(dsinks_ref)
            dqnw_ref[...] = jnp.zeros_like(dqnw_ref)
            dknw_ref[...] = jnp.zeros_like(dknw_ref)

        lane16 = lax.broadcasted_iota(jnp.int32, (1, ATTN_HEADS), 1)
        dsinks = jnp.zeros((1, ATTN_HEADS), F32)
        kvh = range(ATTN_KV_HEADS)
        kv_tiles = [_attn_kv_tiles(kvp_ref, kvc_ref, kvm_ref, j) for j in kvh]
        fn = functools.partial(_attn_groups, bias=[bias_ref[j] for j in kvh], late_norm=False)
        _, vjp = jax.vjp(fn, [_heads_to_lanes(q_ref, j) for j in kvh], [t[0] for t in kv_tiles], [t[1] for t in kv_tiles],
                         [_sink_row(sinks_ref, j) for j in kvh], qnw_ref[...], knw_ref[...])
        dq_t, dks, dvs, dsr, dqn, dkn = vjp([_heads_to_lanes(do_ref, j) for j in kvh])
        dqnw_ref[...] += dqn
        dknw_ref[...] += dkn
        for j in kvh:
            _lanes_to_heads(dq_ref, j, dq_t[j])
            for hl in range(ATTN_GROUPS):
                dsinks = dsinks + jnp.where(lane16 == ATTN_GROUPS * j + hl,
                                            jnp.sum(dsr[j][:, ATTN_BLOCK * hl:ATTN_BLOCK * (hl + 1)]), 0.0)
            ksl = slice(ATTN_HD * j, ATTN_HD * (j + 1))
            vsl = slice(128 + ATTN_HD * j, 128 + ATTN_HD * (j + 1))
            for sl, d in ((ksl, dks[j]), (vsl, dvs[j])):
                dkv_ref[:, sl] = d[ATTN_BLOCK:2 * ATTN_BLOCK, :] + carry_ref[:, sl]
                carry_ref[:, sl] = d[0:ATTN_BLOCK, :]
                meta_ref[:, sl] += d[2 * ATTN_BLOCK:, :]
        dsinks_ref[...] += dsinks

        @pl.when(n == 0)
        def _():
            dkv_ref[FRONT_PAD:, :] += meta_ref[...]

    dq, dkv, dsinks, dqnw, dknw = pl.pallas_call(
        body, grid=(nb,),
        in_specs=[pl.BlockSpec(memory_space=pltpu.SMEM),
                  pl.BlockSpec((ATTN_BLOCK, 1024), lambda i: (rev(i), 0)),
                  pl.BlockSpec((ATTN_BLOCK, 1024), lambda i: (rev(i), 0)),
                  pl.BlockSpec((ATTN_BLOCK, 256), lambda i: (rev(i), 0)),
                  pl.BlockSpec((ATTN_BLOCK, 256), lambda i: (jnp.maximum(rev(i) - 1, 0), 0)),
                  pl.BlockSpec((ATTN_BLOCK, 256), lambda i: (0, 0)),
                  pl.BlockSpec((ATTN_HD, 1), lambda i: (0, 0)),
                  pl.BlockSpec((1, ATTN_HD), lambda i: (0, 0))],
        out_specs=[pl.BlockSpec((ATTN_BLOCK, 1024), lambda i: (rev(i), 0)),
                   pl.BlockSpec((ATTN_BLOCK, 256), lambda i: (rev(i), 0)),
                   pl.BlockSpec((1, ATTN_HEADS), lambda i: (0, 0)),
                   pl.BlockSpec((ATTN_HD, 1), lambda i: (0, 0)),
                   pl.BlockSpec((1, ATTN_HD), lambda i: (0, 0))],
        out_shape=[jax.ShapeDtypeStruct((T, 1024), F32), jax.ShapeDtypeStruct((T, 256), F32),
                   jax.ShapeDtypeStruct((1, ATTN_HEADS), F32), jax.ShapeDtypeStruct((ATTN_HD, 1), F32),
                   jax.ShapeDtypeStruct((1, ATTN_HD), F32)],
        scratch_shapes=[pltpu.VMEM((ATTN_BLOCK, 256), F32), pltpu.VMEM((N_META, 256), F32)] + _attn_table_scratch(),
        name="attn_core_bwd", compiler_params=_cparams(),
    )(sinks, do, q, kv, kv, kv, qnw.reshape(ATTN_HD, 1), knw)
    return dq, dkv, dsinks, dqnw.reshape(1, ATTN_HD), dknw


def _attn_in_bwd(dq, dkv, dgate, x, front, dh1, norm_w, w_in):
    T = dq.shape[0]

    def body(dq_ref, dkv_ref, dg_ref, dh1_ref, xa_ref, xb_ref, xc_ref, front_ref, nw_ref, w_ref, dh0_ref, dnw_ref):
        @pl.when(pl.program_id(0) == 0)
        def _():
            dnw_ref[...] = jnp.zeros_like(dnw_ref)
        h = _padded_block(pl.program_id(0), front_ref[...], (xa_ref, xb_ref, xc_ref))
        dxn = (_bdot(dq_ref[...], w_ref[:, 0:1024], NT) + _bdot(dkv_ref[...], w_ref[:, 1024:1280], NT)
               + _bdot(dg_ref[...], w_ref[:, 1280:2304], NT))
        _, vjp = jax.vjp(_rms, h, nw_ref[...])
        dh, dnw = vjp(dxn)
        dh0_ref[...] = dh1_ref[...] + dh
        dnw_ref[...] += dnw

    return _row_call("attn_in_bwd", body, T, ROW_BLOCK, [dq, dkv, dgate, dh1], [front, norm_w, w_in],
                     [(1024, F32)], [((1, 1024), F32)], halos=_token_views(x))


def _dn_in_fwd(h1, norm_w, w_in):
    T = h1.shape[0]

    def body(h_ref, nw_ref, w_ref, xn_ref, qkv_ref, z_ref, ba_ref):
        xn = _rms(h_ref[...], nw_ref[...]).astype(BF16)
        xn_ref[...] = xn
        qkv_ref[...] = jnp.dot(xn, w_ref[:, 0:4096], preferred_element_type=F32)
        z_ref[...] = jnp.dot(xn, w_ref[:, 4096:6144], preferred_element_type=F32)
        ba_ref[...] = jnp.dot(xn, w_ref[:, 6144:6176], preferred_element_type=F32)

    return _row_call("dn_in_fwd", body, T, ROW_BLOCK, [h1], [norm_w, w_in],
                     [(1024, BF16), (4096, F32), (2048, F32), (32, F32)])


def _shift_down(cur, prev8, s):
    i8 = lax.broadcasted_iota(jnp.int32, (8, cur.shape[1]), 0)
    r = pltpu.roll(cur, s, 0)
    head = jnp.where(i8 < s, pltpu.roll(prev8, s, 0), r[0:8])
    return jnp.concatenate([head, r[8:]], axis=0)


def _shift_up(cur, next8, s):
    n = cur.shape[0]
    i8 = lax.broadcasted_iota(jnp.int32, (8, cur.shape[1]), 0)
    r = pltpu.roll(cur, n - s, 0)
    tail = jnp.where(i8 >= 8 - s, pltpu.roll(next8, 8 - s, 0), r[n - 8:])
    return jnp.concatenate([r[:n - 8], tail], axis=0)


def _conv_taps(cur, prev8):
    return [cur] + [_shift_down(cur, prev8, s) for s in range(1, DN_CONV_K)]


def _conv_tile(taps, w):
    out = w[3:4, :] * taps[0]
    for s in range(1, DN_CONV_K):
        out = out + w[3 - s:4 - s, :] * taps[s]
    return out


def _l2n(a, scale):
    return a * (lax.rsqrt(jnp.sum(a * a, axis=-1, keepdims=True) + NORM_EPS) * scale)


def _dn_post_tile(c, t):
    a = _silu(c)
    if t < DN_K_HEADS:
        return _l2n(a, DN_HD ** -0.5)
    if t < 2 * DN_K_HEADS:
        return _l2n(a, 1.0)
    return a


def _dn_beta_g(ba, a_log, dt_bias, live):
    beta = jax.nn.sigmoid(ba[:, 0:DN_V_HEADS]) * live
    g = -jnp.exp(a_log) * _softplus(ba[:, DN_V_HEADS:] + dt_bias) * live
    return beta, g


def _live_rows(i, rb):
    rows = i * rb + lax.broadcasted_iota(jnp.int32, (rb, 1), 0)
    return (rows >= FRONT_PAD).astype(F32)


def _halo_spec_args(x, rb):
    per = rb // 8
    return (x, 8, lambda i: (jnp.maximum(i * per - 1, 0), 0))


def _dn_conv_fwd(qkv, ba, conv_w, a_log, dt_bias):
    T = qkv.shape[0]

    def body(x_ref, ba_ref, halo_ref, cw_ref, al_ref, dtb_ref, q_ref, k_ref, v_ref, bg_ref, c_ref):
        i = pl.program_id(0)
        first = (i > 0).astype(F32)
        for t in range(DN_CONV_W // 128):
            cols = slice(128 * t, 128 * (t + 1))
            c = _conv_tile(_conv_taps(x_ref[:, cols], halo_ref[:, cols] * first), cw_ref[:, cols])
            c_ref[:, cols] = c.astype(BF16)
            out = _dn_post_tile(c, t)
            if t < DN_K_HEADS:
                q_ref[:, cols] = out
            elif t < 2 * DN_K_HEADS:
                k_ref[:, 128 * (t - 8):128 * (t - 7)] = out
            else:
                v_ref[:, 128 * (t - 16):128 * (t - 15)] = out
        beta, g = _dn_beta_g(ba_ref[...], al_ref[...], dtb_ref[...], _live_rows(i, ROW_BLOCK))
        bg_ref[:, 0:DN_V_HEADS] = beta
        bg_ref[:, DN_V_HEADS:] = g

    return _row_call("dn_conv_fwd", body, T, ROW_BLOCK, [qkv, ba], [conv_w, a_log, dt_bias],
                     [(1024, F32), (1024, F32), (2048, F32), (32, F32), (4096, BF16)], halos=[_halo_spec_args(qkv, ROW_BLOCK)])


def _chunk_masks():
    r = lax.broadcasted_iota(jnp.int32, (DN_CHUNK, DN_CHUNK), 0)
    c = lax.broadcasted_iota(jnp.int32, (DN_CHUNK, DN_CHUNK), 1)
    return r >= c, r > c, r == c, r <= c


def _tri_inv_block(x):
    B = TRI_BLOCK
    n = range(len(x))
    r_, c_ = lax.broadcasted_iota(jnp.int32, (B, B), 0), lax.broadcasted_iota(jnp.int32, (B, B), 1)
    ainv = [jnp.where(r_ == c_, 1.0, 0.0) + x[h] for h in n]
    p = [_bdot(x[h], x[h]) for h in n]
    for _ in range(B.bit_length() - 3):
        r = [_bdot(jnp.concatenate([p[h], ainv[h]], axis=0), p[h]) for h in n]
        ainv = [ainv[h] + r[h][B:] for h in n]
        p = [r[h][:B] for h in n]
    return [ainv[h] + _bdot(ainv[h], p[h]) for h in n]


def _tri_inv(x):
    B = TRI_BLOCK
    assert DN_CHUNK == 2 * B
    n = len(x)
    diag = _tri_inv_block([x[h][:B, :B] for h in range(n)] + [x[h][B:, B:] for h in range(n)])
    a11, a22 = diag[:n], diag[n:]
    a21 = [_bdot(_bdot(a22[h], x[h][B:, :B]), a11[h]) for h in range(n)]
    zero = jnp.zeros((B, B), F32)
    return [jnp.concatenate([jnp.concatenate([a11[h], zero], axis=1), jnp.concatenate([a21[h], a22[h]], axis=1)], axis=0)
            for h in range(n)]


@jax.custom_vjp
def _tri_inv_known(x, a):
    return a


def _tri_inv_known_fwd(x, a):
    return a, a


def _tri_inv_known_bwd(a, da):
    return [_bdot(_bdot(a[h], da[h], TN), a[h], NT) for h in range(len(a))], [jnp.zeros_like(t) for t in a]


_tri_inv_known.defvjp(_tri_inv_known_fwd, _tri_inv_known_bwd)


@jax.custom_vjp
def _known(computed, value):
    return value


def _known_fwd(computed, value):
    return value, None


def _known_bwd(_, g):
    return g, jax.tree.map(jnp.zeros_like, g)


_known.defvjp(_known_fwd, _known_bwd)


def _dn_chunk_step(S, q, k, v, beta, g, masks, known=None):
    causal, strict, eye, upper = masks
    C, W = DN_CHUNK, DN_HD
    heads = range(len(v))
    k_t = [k[j].T for j in range(len(k))]
    qk_kk = [_bdot(jnp.concatenate([q[j], k[j]], axis=0), k_t[j]) for j in range(len(q))]
    if known is not None:
        qk_kk = _known(qk_kk, known["qk_kk"])
    g_b = [jnp.broadcast_to(g[h], (C, C)) for h in heads]
    beta_b = [jnp.broadcast_to(beta[h], (C, W)) for h in heads]
    g_row = [jnp.sum(jnp.where(eye, g_b[h], 0.0), axis=0, keepdims=True) for h in heads]
    gc_col = [jnp.sum(jnp.where(causal, g_row[h], 0.0), axis=1, keepdims=True) for h in heads]
    gc_row = [jnp.sum(jnp.where(upper, g_b[h], 0.0), axis=0, keepdims=True) for h in heads]
    g_last = [jnp.sum(g_row[h], axis=1, keepdims=True) for h in heads]
    gc_b = [jnp.broadcast_to(gc_col[h], (C, W)) for h in heads]
    decay = [jnp.exp(jnp.where(causal, gc_b[h][:, :C] - gc_row[h], NEG)) for h in heads]
    eg_b = [jnp.exp(gc_b[h]) for h in heads]
    x = [jnp.where(strict, qk_kk[h // 2][C:] * beta_b[h][:, :C] * decay[h], 0.0) * -1.0 for h in heads]
    ainv = _tri_inv(x) if known is None else _tri_inv_known(x, known["inv"])
    uw = [_bdot(ainv[h], jnp.concatenate([v[h] * beta_b[h], k[h // 2] * (beta_b[h] * eg_b[h])], axis=1)) for h in heads]
    if known is not None:
        uw = _known(uw, known["uw"])
    q_eg = [q[h // 2] * eg_b[h] for h in heads]
    attn = [qk_kk[h // 2][:C] * decay[h] for h in heads]
    k_st = [k_t[h // 2] * jnp.exp(g_last[h] - gc_row[h]) for h in heads]
    s_dec = [jnp.exp(g_last[h]) for h in heads]
    prep = (uw, q_eg, attn, k_st, s_dec)
    if S is None:
        return prep, dict(inv=ainv, uw=uw, qk_kk=qk_kk)
    s_new, o, _ = _dn_chunk_tail(S, prep, None if known is None else known["v_new"])
    return s_new, o


def _dn_chunk_tail(S, prep, known_v_new=None):
    uw, q_eg, attn, k_st, s_dec = prep
    C, W = DN_CHUNK, DN_HD
    heads = range(len(uw))
    ws_qs = [_bdot(jnp.concatenate([uw[h][:, W:], q_eg[h]], axis=0), S[h]) for h in heads]
    v_new = [uw[h][:, :W] - ws_qs[h][:C] for h in heads]
    if known_v_new is not None:
        v_new = _known(v_new, known_v_new)
    o = [ws_qs[h][C:] + _bdot(attn[h], v_new[h]) for h in heads]
    s_new = [S[h] * s_dec[h] + _bdot(k_st[h], v_new[h]) for h in heads]
    return s_new, o, v_new


def _dn_chunk_tiles(q_ref, k_ref, v_ref, bg_ref, c, first, count):
    rows = slice(DN_CHUNK * c, DN_CHUNK * (c + 1))
    q = [q_ref[rows, 128 * j:128 * (j + 1)] for j in range(first // 2, (first + count) // 2)]
    k = [k_ref[rows, 128 * j:128 * (j + 1)] for j in range(first // 2, (first + count) // 2)]
    v = [v_ref[rows, 128 * h:128 * (h + 1)] for h in range(first, first + count)]
    beta = [bg_ref[rows, h:h + 1] for h in range(first, first + count)]
    g = [bg_ref[rows, DN_V_HEADS + h:DN_V_HEADS + h + 1] for h in range(first, first + count)]
    return q, k, v, beta, g


def _dn_scan_fwd(qn, kn, v, bg):
    T = qn.shape[0]
    nc = T // DN_CHUNK
    rows = SCAN_CHUNKS * DN_CHUNK
    assert nc % SCAN_CHUNKS == 0

    def body(q_ref, k_ref, v_ref, bg_ref, o_ref, ssave_ref, inv_ref, uw_ref, vn_ref, qk_ref, s_ref):
        @pl.when(pl.program_id(0) == 0)
        def _():
            s_ref[...] = jnp.zeros_like(s_ref)
        masks = _chunk_masks()
        for first in range(0, DN_V_HEADS, SCAN_FWD_GROUP):
            heads = range(first, first + SCAN_FWD_GROUP)
            preps = [_dn_chunk_step(None, *_dn_chunk_tiles(q_ref, k_ref, v_ref, bg_ref, c, first, SCAN_FWD_GROUP), masks)
                     for c in range(SCAN_CHUNKS)]
            state = [s_ref[h] for h in heads]
            for c, (prep, saved) in enumerate(preps):
                for i, h in enumerate(heads):
                    ssave_ref[c, h] = state[i]
                    inv_ref[c, h] = saved["inv"][i].astype(BF16)
                    uw_ref[c, h] = saved["uw"][i].astype(BF16)
                for i, j in enumerate(range(first // 2, (first + SCAN_FWD_GROUP) // 2)):
                    qk_ref[c, j] = saved["qk_kk"][i].astype(BF16)
                state, o, v_new = _dn_chunk_tail(state, prep)
                for i, h in enumerate(heads):
                    o_ref[DN_CHUNK * c:DN_CHUNK * (c + 1), 128 * h:128 * (h + 1)] = o[i]
                    vn_ref[c, h] = v_new[i].astype(BF16)
            for i, h in enumerate(heads):
                s_ref[h] = state[i]

    return pl.pallas_call(
        body, grid=(nc // SCAN_CHUNKS,),
        in_specs=[pl.BlockSpec((rows, 1024), lambda i: (i, 0)),
                  pl.BlockSpec((rows, 1024), lambda i: (i, 0)),
                  pl.BlockSpec((rows, 2048), lambda i: (i, 0)),
                  pl.BlockSpec((rows, 32), lambda i: (i, 0))],
        out_specs=[pl.BlockSpec((rows, 2048), lambda i: (i, 0)),
                   pl.BlockSpec((SCAN_CHUNKS, DN_V_HEADS, DN_HD, DN_HD), lambda i: (i, 0, 0, 0)),
                   pl.BlockSpec((SCAN_CHUNKS, DN_V_HEADS, DN_CHUNK, DN_CHUNK), lambda i: (i, 0, 0, 0)),
                   pl.BlockSpec((SCAN_CHUNKS, DN_V_HEADS, DN_CHUNK, 2 * DN_HD), lambda i: (i, 0, 0, 0)),
                   pl.BlockSpec((SCAN_CHUNKS, DN_V_HEADS, DN_CHUNK, DN_HD), lambda i: (i, 0, 0, 0)),
                   pl.BlockSpec((SCAN_CHUNKS, DN_K_HEADS, 2 * DN_CHUNK, DN_CHUNK), lambda i: (i, 0, 0, 0))],
        out_shape=[jax.ShapeDtypeStruct((T, 2048), F32),
                   jax.ShapeDtypeStruct((nc, DN_V_HEADS, DN_HD, DN_HD), F32),
                   jax.ShapeDtypeStruct((nc, DN_V_HEADS, DN_CHUNK, DN_CHUNK), BF16),
                   jax.ShapeDtypeStruct((nc, DN_V_HEADS, DN_CHUNK, 2 * DN_HD), BF16),
                   jax.ShapeDtypeStruct((nc, DN_V_HEADS, DN_CHUNK, DN_HD), BF16),
                   jax.ShapeDtypeStruct((nc, DN_K_HEADS, 2 * DN_CHUNK, DN_CHUNK), BF16)],
        scratch_shapes=[pltpu.VMEM((DN_V_HEADS, DN_HD, DN_HD), F32)],
        name="dn_scan_fwd", compiler_params=_cparams(),
    )(qn, kn, v, bg)


def _dn_gate_tile(o, z, onw):
    return _rms(o, onw) * _silu(z)


def _dn_out_fwd(o, z, h1, target, w_out, onw):
    T = o.shape[0]

    def body(o_ref, z_ref, h_ref, ta_ref, tb_ref, tc_ref, w_ref, onw_ref, dy_ref, og_ref, loss_ref):
        i = pl.program_id(0)

        @pl.when(i == 0)
        def _():
            loss_ref[...] = jnp.zeros_like(loss_ref)
        for h in range(DN_V_HEADS):
            cols = slice(128 * h, 128 * (h + 1))
            og_ref[:, cols] = _dn_gate_tile(o_ref[:, cols], z_ref[:, cols], onw_ref[...]).astype(BF16)
        y = h_ref[...] + jnp.dot(og_ref[...], w_ref[...], preferred_element_type=F32)
        rows = i * ROW_BLOCK + lax.broadcasted_iota(jnp.int32, (ROW_BLOCK, 1), 0)
        diff = jnp.where(rows >= FRONT_PAD + N_META, y - _padded_block(i, None, (ta_ref, tb_ref, tc_ref)), 0.0)
        dy_ref[...] = diff * (1.0 / D_MODEL)
        loss_ref[...] += jnp.sum(diff * diff) * (0.5 / D_MODEL)

    return _row_call("dn_out_fwd", body, T, ROW_BLOCK, [o, z, h1], [w_out, onw],
                     [(1024, F32), (2048, BF16)], [((1, 128), F32)], halos=_token_views(target))


def _dn_out_bwd(dy, o, z, w_out, onw):
    T = o.shape[0]

    def body(dy_ref, o_ref, z_ref, w_ref, onw_ref, do_ref, dz_ref, donw_ref, dog_ref):
        @pl.when(pl.program_id(0) == 0)
        def _():
            donw_ref[...] = jnp.zeros_like(donw_ref)
        dy = dy_ref[...].astype(BF16)
        donw = jnp.zeros((1, DN_HD), F32)
        for half in range(2):
            hcols = slice(1024 * half, 1024 * (half + 1))
            dog_ref[:, hcols] = lax.dot_general(dy, w_ref[hcols, :], NT, preferred_element_type=F32)
        for h in range(DN_V_HEADS):
            cols = slice(128 * h, 128 * (h + 1))
            _, vjp = jax.vjp(_dn_gate_tile, o_ref[:, cols], z_ref[:, cols], onw_ref[...])
            do, dz, dn = vjp(dog_ref[:, cols])
            do_ref[:, cols] = do
            dz_ref[:, cols] = dz
            donw = donw + dn
        donw_ref[...] += donw

    return _row_call("dn_out_bwd", body, T, ROW_BLOCK, [dy, o, z], [w_out, onw],
                     [(2048, F32), (2048, F32)], [((1, DN_HD), F32)], scratch=[pltpu.VMEM((ROW_BLOCK, 2048), F32)])


def _dn_scan_bwd(do, qn, kn, v, bg, ssave, saved):
    T = qn.shape[0]
    nc = T // DN_CHUNK
    chunks = SCAN_BWD_CHUNKS
    ns = nc // chunks
    rows = chunks * DN_CHUNK
    rev = lambda i: ns - 1 - i

    def body(do_ref, q_ref, k_ref, v_ref, bg_ref, ss_ref, inv_ref, uw_ref, vn_ref, qk_ref,
             dq_ref, dk_ref, dv_ref, dbg_ref, ds_ref):
        @pl.when(pl.program_id(0) == 0)
        def _():
            ds_ref[...] = jnp.zeros_like(ds_ref)
        lane32 = lax.broadcasted_iota(jnp.int32, (1, 2 * DN_V_HEADS), 1)
        masks = _chunk_masks()
        dbg = [jnp.zeros((DN_CHUNK, 2 * DN_V_HEADS), F32) for _ in range(chunks)]
        for first in range(0, DN_V_HEADS, SCAN_BWD_GROUP):
            heads = range(first, first + SCAN_BWD_GROUP)
            vjps = []
            for c in range(chunks):
                known = dict(inv=[inv_ref[c, h].astype(F32) for h in heads], uw=[uw_ref[c, h].astype(F32) for h in heads],
                             v_new=[vn_ref[c, h].astype(F32) for h in heads],
                             qk_kk=[qk_ref[c, j].astype(F32) for j in range(first // 2, (first + SCAN_BWD_GROUP) // 2)])
                fn = functools.partial(_dn_chunk_step, masks=masks, known=known)
                vjps.append(jax.vjp(fn, [ss_ref[c, h] for h in heads],
                                    *_dn_chunk_tiles(q_ref, k_ref, v_ref, bg_ref, c, first, SCAN_BWD_GROUP))[1])
            ds = [ds_ref[h] for h in heads]
            for c in reversed(range(chunks)):
                crows = slice(DN_CHUNK * c, DN_CHUNK * (c + 1))
                ds, dq, dk, dv, dbeta, dg = vjps[c]((ds, [do_ref[crows, 128 * h:128 * (h + 1)] for h in heads]))
                for i, h in enumerate(heads):
                    dv_ref[crows, 128 * h:128 * (h + 1)] = dv[i]
                    dbg[c] = dbg[c] + jnp.where(lane32 == h, dbeta[i], 0.0) + jnp.where(lane32 == DN_V_HEADS + h, dg[i], 0.0)
                for i, j in enumerate(range(first // 2, (first + SCAN_BWD_GROUP) // 2)):
                    dq_ref[crows, 128 * j:128 * (j + 1)] = dq[i]
                    dk_ref[crows, 128 * j:128 * (j + 1)] = dk[i]
            for i, h in enumerate(heads):
                ds_ref[h] = ds[i]
        for c in range(chunks):
            dbg_ref[DN_CHUNK * c:DN_CHUNK * (c + 1), :] = dbg[c]

    return pl.pallas_call(
        body, grid=(ns,),
        in_specs=[pl.BlockSpec((rows, 2048), lambda i: (rev(i), 0)),
                  pl.BlockSpec((rows, 1024), lambda i: (rev(i), 0)),
                  pl.BlockSpec((rows, 1024), lambda i: (rev(i), 0)),
                  pl.BlockSpec((rows, 2048), lambda i: (rev(i), 0)),
                  pl.BlockSpec((rows, 32), lambda i: (rev(i), 0)),
                  pl.BlockSpec((chunks, DN_V_HEADS, DN_HD, DN_HD), lambda i: (rev(i), 0, 0, 0)),
                  pl.BlockSpec((chunks, DN_V_HEADS, DN_CHUNK, DN_CHUNK), lambda i: (rev(i), 0, 0, 0)),
                  pl.BlockSpec((chunks, DN_V_HEADS, DN_CHUNK, 2 * DN_HD), lambda i: (rev(i), 0, 0, 0)),
                  pl.BlockSpec((chunks, DN_V_HEADS, DN_CHUNK, DN_HD), lambda i: (rev(i), 0, 0, 0)),
                  pl.BlockSpec((chunks, DN_K_HEADS, 2 * DN_CHUNK, DN_CHUNK), lambda i: (rev(i), 0, 0, 0))],
        out_specs=[pl.BlockSpec((rows, 1024), lambda i: (rev(i), 0)),
                   pl.BlockSpec((rows, 1024), lambda i: (rev(i), 0)),
                   pl.BlockSpec((rows, 2048), lambda i: (rev(i), 0)),
                   pl.BlockSpec((rows, 32), lambda i: (rev(i), 0))],
        out_shape=[jax.ShapeDtypeStruct((T, 1024), F32), jax.ShapeDtypeStruct((T, 1024), F32),
                   jax.ShapeDtypeStruct((T, 2048), F32), jax.ShapeDtypeStruct((T, 32), F32)],
        scratch_shapes=[pltpu.VMEM((DN_V_HEADS, DN_HD, DN_HD), F32)],
        name="dn_scan_bwd", compiler_params=_cparams(),
    )(do, qn, kn, v, bg, ssave, *saved)


def _dn_conv_bwd(dqn, dkn, dv, dbg, qkv, conv_out, ba, conv_w, a_log, dt_bias):
    T = qkv.shape[0]
    rb = ROW_BLOCK // 2
    nr = T // rb

    def body(dq_ref, dk_ref, dv_ref, dbg_ref, x_ref, c_ref, ba_ref, cw_ref, al_ref, dtb_ref,
             dx_ref, dba_ref, dcw_ref, dal_ref, ddtb_ref, carry_ref):
        step = pl.program_id(0)
        i = nr - 1 - step

        @pl.when(step == 0)
        def _():
            carry_ref[...] = jnp.zeros_like(carry_ref)
            dcw_ref[...] = jnp.zeros_like(dcw_ref)
            dal_ref[...] = jnp.zeros_like(dal_ref)
            ddtb_ref[...] = jnp.zeros_like(ddtb_ref)
        for t in range(DN_CONV_W // 128):
            cols = slice(128 * t, 128 * (t + 1))
            w, x = cw_ref[:, cols], x_ref[:, cols]
            if t < DN_K_HEADS:
                dout = dq_ref[:, cols]
            elif t < 2 * DN_K_HEADS:
                dout = dk_ref[:, 128 * (t - 8):128 * (t - 7)]
            else:
                dout = dv_ref[:, 128 * (t - 16):128 * (t - 15)]
            _, vjp = jax.vjp(functools.partial(_dn_post_tile, t=t), c_ref[:, cols].astype(F32))
            (dc,) = vjp(dout)
            nxt = carry_ref[:, cols]
            dx = w[3:4, :] * dc
            dcw_ref[3:4, cols] += jnp.sum(dc * x, axis=0, keepdims=True)
            for s in range(1, DN_CONV_K):
                up = _shift_up(dc, nxt, s)
                dx = dx + w[3 - s:4 - s, :] * up
                dcw_ref[3 - s:4 - s, cols] += jnp.sum(up * x, axis=0, keepdims=True)
            dx_ref[:, cols] = dx
            carry_ref[:, cols] = dc[0:8, :]
        fn = functools.partial(_dn_beta_g, live=_live_rows(i, rb))
        _, vjp = jax.vjp(fn, ba_ref[...], al_ref[...], dtb_ref[...])
        dba, dal, ddtb = vjp((dbg_ref[:, 0:DN_V_HEADS], dbg_ref[:, DN_V_HEADS:]))
        dba_ref[...] = dba
        dal_ref[...] += dal
        ddtb_ref[...] += ddtb

    return _row_call("dn_conv_bwd", body, T, rb, [dqn, dkn, dv, dbg, qkv, conv_out, ba], [conv_w, a_log, dt_bias],
                     [(4096, F32), (32, F32)], [((DN_CONV_K, 4096), F32), ((1, DN_V_HEADS), F32), ((1, DN_V_HEADS), F32)],
                     reverse=True, scratch=[pltpu.VMEM((8, 4096), F32)])


def _dn_in_bwd(dqkv, dz, dba, h1, dy, norm_w, w_in):
    T = h1.shape[0]

    def body(dqkv_ref, dz_ref, dba_ref, h_ref, dy_ref, nw_ref, w_ref, dh_ref, dnw_ref):
        @pl.when(pl.program_id(0) == 0)
        def _():
            dnw_ref[...] = jnp.zeros_like(dnw_ref)
        dxn = (_bdot(dqkv_ref[...], w_ref[:, 0:4096], NT) + _bdot(dz_ref[...], w_ref[:, 4096:6144], NT)
               + _bdot(dba_ref[...], w_ref[:, 6144:6176], NT))
        _, vjp = jax.vjp(_rms, h_ref[...], nw_ref[...])
        dh, dnw = vjp(dxn)
        dh_ref[...] = (dy_ref[...] + dh) * _live_rows(pl.program_id(0), ROW_BLOCK)
        dnw_ref[...] += dnw

    return _row_call("dn_in_bwd", body, T, ROW_BLOCK, [dqkv, dz, dba, h1, dy], [norm_w, w_in],
                     [(1024, F32)], [((1, 1024), F32)])


def _exchange(parts, scatter, name):
    n = len(parts)
    out_shape = [jax.ShapeDtypeStruct(p.shape if sc else (N_DEV,) + p.shape, p.dtype) for p, sc in zip(parts, scatter)]

    def body(*refs):
        ins, outs = refs[:n], refs[n:2 * n]
        send_sems, recv_sems, local_sems = refs[2 * n:]
        x, y, c = lax.axis_index("x"), lax.axis_index("y"), lax.axis_index("c")
        me = 4 * x + 2 * y + c
        peers = []
        for k in range(1, N_DEV):
            px = 1 - x if k & 4 else x
            py = 1 - y if k & 2 else y
            pc = 1 - c if k & 1 else c
            peers.append(((px, py, pc), 4 * px + 2 * py + pc))

        def src(a, idx):
            return ins[a].at[idx] if scatter[a] else ins[a]

        local = [pltpu.make_async_copy(src(a, me), outs[a].at[me], local_sems.at[a]) for a in range(n)]
        for cp in local:
            cp.start()
        for a in range(n):
            for k, (dev, idx) in enumerate(peers):
                pltpu.make_async_remote_copy(
                    src_ref=src(a, idx), dst_ref=outs[a].at[me], send_sem=send_sems.at[a, k], recv_sem=recv_sems.at[a, k],
                    device_id=dev, device_id_type=pl.DeviceIdType.MESH).start()
        for a in range(n):
            for k, (dev, idx) in enumerate(peers):
                pltpu.make_async_remote_copy(
                    src_ref=src(a, idx), dst_ref=outs[a].at[idx], send_sem=send_sems.at[a, k], recv_sem=recv_sems.at[a, k],
                    device_id=dev, device_id_type=pl.DeviceIdType.MESH).wait()
        for cp in local:
            cp.wait()

    hbm = pl.BlockSpec(memory_space=pltpu.HBM)
    return pl.pallas_call(
        body, out_shape=out_shape, in_specs=[hbm] * n, out_specs=[hbm] * n,
        scratch_shapes=[pltpu.SemaphoreType.DMA((n, N_DEV - 1)), pltpu.SemaphoreType.DMA((n, N_DEV - 1)),
                        pltpu.SemaphoreType.DMA((n,))],
        name=name,
    )(*parts)


def _gather_two_level(parts, name):
    n = len(parts)
    out_shape = [jax.ShapeDtypeStruct((N_DEV,) + p.shape, p.dtype) for p in parts]

    def body(*refs):
        ins, outs = refs[:n], refs[n:2 * n]
        send_sems, recv_sems, local_sems = refs[2 * n:]
        x, y, c = lax.axis_index("x"), lax.axis_index("y"), lax.axis_index("c")
        idx = lambda px, py, pc: 4 * px + 2 * py + pc
        me, sibling = (x, y, c), (x, y, 1 - c)
        chips = [(1 - x, y), (x, 1 - y), (1 - x, 1 - y)]

        def copy(a, k, block, to, src=None):
            slot = outs[a].at[idx(*block)]
            return pltpu.make_async_remote_copy(
                src_ref=slot if src is None else src, dst_ref=slot, send_sem=send_sems.at[a, k], recv_sem=recv_sems.at[a, k],
                device_id=to, device_id_type=pl.DeviceIdType.MESH)

        local = [pltpu.make_async_copy(ins[a], outs[a].at[idx(*me)], local_sems.at[a]) for a in range(n)]
        for cp in local:
            cp.start()
        sent = []
        for a in range(n):
            sent.append(copy(a, 0, me, sibling, src=ins[a]))
            sent += [copy(a, 1 + j, me, (*chip, c), src=ins[a]) for j, chip in enumerate(chips)]
        for cp in sent:
            cp.start()
        for a in range(n):
            for j, chip in enumerate(chips):
                copy(a, 1 + j, (*chip, c), me).wait_recv()
                passed = copy(a, 4 + j, (*chip, c), sibling)
                passed.start()
                sent.append(passed)
        for a in range(n):
            copy(a, 0, sibling, me).wait_recv()
            for j, chip in enumerate(chips):
                copy(a, 4 + j, (*chip, 1 - c), me).wait_recv()
        for cp in sent:
            cp.wait_send()
        for cp in local:
            cp.wait()

    hbm = pl.BlockSpec(memory_space=pltpu.HBM)
    return pl.pallas_call(
        body, out_shape=out_shape, in_specs=[hbm] * n, out_specs=[hbm] * n,
        scratch_shapes=[pltpu.SemaphoreType.DMA((n, N_DEV - 1)), pltpu.SemaphoreType.DMA((n, N_DEV - 1)),
                        pltpu.SemaphoreType.DMA((n,))],
        name=name,
    )(*parts)


def _swap_with_sibling(parts, name):
    n = len(parts)

    def body(*refs):
        ins, outs = refs[:n], refs[n:2 * n]
        send_sems, recv_sems = refs[2 * n:]
        x, y, c = lax.axis_index("x"), lax.axis_index("y"), lax.axis_index("c")
        copies = [pltpu.make_async_remote_copy(
            src_ref=ins[a].at[1 - c], dst_ref=outs[a], send_sem=send_sems.at[a], recv_sem=recv_sems.at[a],
            device_id=(x, y, 1 - c), device_id_type=pl.DeviceIdType.MESH) for a in range(n)]
        for cp in copies:
            cp.start()
        for cp in copies:
            cp.wait()

    hbm = pl.BlockSpec(memory_space=pltpu.HBM)
    return pl.pallas_call(
        body, out_shape=[jax.ShapeDtypeStruct(p.shape[1:], p.dtype) for p in parts], in_specs=[hbm] * n, out_specs=[hbm] * n,
        scratch_shapes=[pltpu.SemaphoreType.DMA((n,)), pltpu.SemaphoreType.DMA((n,))],
        name=name,
    )(*parts)


def _pair_sum(a, b, name):
    R, C = a.shape
    rb = _adam_rows(R)

    def body(a_ref, b_ref, o_ref):
        o_ref[...] = (a_ref[...].astype(F32) + b_ref[...].astype(F32)).astype(BF16)

    blk = pl.BlockSpec((rb, C), lambda i: (i, 0))
    return pl.pallas_call(body, grid=(R // rb,), in_specs=[blk, blk], out_specs=blk,
                          out_shape=jax.ShapeDtypeStruct((R, C), BF16), name=name, compiler_params=_cparams())(a, b)


def _exchange_chips(parts, name):
    n = len(parts)
    n_chips = N_DEV // 2

    def body(*refs):
        ins, outs = refs[:n], refs[n:2 * n]
        send_sems, recv_sems, local_sems = refs[2 * n:]
        x, y, c = lax.axis_index("x"), lax.axis_index("y"), lax.axis_index("c")
        mine = 2 * x + y
        chips = [(1 - x, y), (x, 1 - y), (1 - x, 1 - y)]
        local = [pltpu.make_async_copy(ins[a].at[mine], outs[a].at[mine], local_sems.at[a]) for a in range(n)]
        for cp in local:
            cp.start()
        for a in range(n):
            for k, (px, py) in enumerate(chips):
                pltpu.make_async_remote_copy(
                    src_ref=ins[a].at[2 * px + py], dst_ref=outs[a].at[mine], send_sem=send_sems.at[a, k],
                    recv_sem=recv_sems.at[a, k], device_id=(px, py, c), device_id_type=pl.DeviceIdType.MESH).start()
        for a in range(n):
            for k, (px, py) in enumerate(chips):
                pltpu.make_async_remote_copy(
                    src_ref=ins[a].at[2 * px + py], dst_ref=outs[a].at[2 * px + py], send_sem=send_sems.at[a, k],
                    recv_sem=recv_sems.at[a, k], device_id=(px, py, c), device_id_type=pl.DeviceIdType.MESH).wait()
        for cp in local:
            cp.wait()

    hbm = pl.BlockSpec(memory_space=pltpu.HBM)
    return pl.pallas_call(
        body, out_shape=[jax.ShapeDtypeStruct(p.shape, p.dtype) for p in parts], in_specs=[hbm] * n, out_specs=[hbm] * n,
        scratch_shapes=[pltpu.SemaphoreType.DMA((n, n_chips - 1)), pltpu.SemaphoreType.DMA((n, n_chips - 1)),
                        pltpu.SemaphoreType.DMA((n,))],
        name=name,
    )(*parts)


def _adam_rows(rows):
    for rb in (128, 64, 40, 16, 8):
        if rows % rb == 0:
            return rb
    return rows


def _adamw(stack, w, m, v, name):
    R, C = w.shape
    rb = _adam_rows(R)
    slots = stack.shape[0]

    def body(s_ref, w_ref, m_ref, v_ref, g_ref, d_ref, nm_ref, nv_ref):
        g = s_ref[0].astype(F32)
        for s in range(1, slots):
            g = g + s_ref[s].astype(F32)
        nm = ADAM_B1 * m_ref[...] + (1.0 - ADAM_B1) * g
        nv = ADAM_B2 * v_ref[...] + (1.0 - ADAM_B2) * (g * g)
        m_hat = nm / (1.0 - ADAM_B1 ** ADAM_STEP)
        v_hat = nv / (1.0 - ADAM_B2 ** ADAM_STEP)
        g_ref[...] = g
        d_ref[...] = -ADAM_LR * (m_hat / (jnp.sqrt(v_hat) + ADAM_EPS) + ADAM_WD * w_ref[...])
        nm_ref[...] = nm
        nv_ref[...] = nv

    blk = pl.BlockSpec((rb, C), lambda i: (i, 0))
    return pl.pallas_call(
        body, grid=(R // rb,),
        in_specs=[pl.BlockSpec((slots, rb, C), lambda i: (0, i, 0)), blk, blk, blk],
        out_specs=[blk] * 4, out_shape=[jax.ShapeDtypeStruct((R, C), F32)] * 4,
        name=name, compiler_params=_cparams(),
    )(stack, w, m, v)


def _pad_rows8(a):
    return jnp.concatenate([a, jnp.zeros((8 - a.shape[0], a.shape[1]), a.dtype)], axis=0) if a.shape[0] < 8 else a


def _pack_small(norm_w, qnw, knw, sinks, a_log, dt_bias, onw, extra):
    z = lambda n: jnp.zeros((1, n), F32)
    row = jnp.concatenate([norm_w, qnw, knw, sinks, a_log, dt_bias, z(80), onw, extra, z(512)], axis=1)
    return row.reshape(16, 128)


def _unpack_small(p):
    row = p.reshape(1, 2048)
    cut = lambda a, n: row[:, a:a + n]
    return (cut(0, 1024), cut(1024, 64), cut(1088, 64), cut(1152, 16), cut(1168, 16), cut(1184, 16), cut(1280, 128),
            cut(1408, 128))


def _pack_rows(w_in_a, w_in_d, w_out_a, w_out_d, meta, conv, dn_norm):
    a = jnp.concatenate([w_in_a, w_in_d], axis=1)
    b = jnp.concatenate([w_out_a, w_out_d], axis=0)
    c = jnp.concatenate([meta, conv.reshape(16, 128), _pad_rows8(dn_norm)], axis=0)
    return a, b, c


def _unpack_rows(a, b, c):
    return (a[:, :288], a[:, 288:], b[:128], b[128:], c[:16], c[16:32].reshape(4, 512), c[32:33])


def _local_step(x, front, target, w):
    xn0, q, kv, gate = _attn_in_fwd(x, front, w["attn_norm_w"], w["attn_w_in"])
    o = _attn_core_fwd(q, kv, w["attn_sinks"], w["attn_q_norm_w"], w["attn_k_norm_w"])
    h1 = _attn_out_fwd(o, gate, x, front, w["attn_w_out"])
    xn1, qkv, z, ba = _dn_in_fwd(h1, w["dn_norm_w"], w["dn_w_in"])
    qn, kn, v, bg, conv_out = _dn_conv_fwd(qkv, ba, w["dn_conv_w"], w["dn_a_log"], w["dn_dt_bias"])
    o_dn, ssave, *saved = _dn_scan_fwd(qn, kn, v, bg)
    dy, og_dn, loss = _dn_out_fwd(o_dn, z, h1, target, w["dn_w_out"], w["dn_o_norm_w"])

    g = {}
    do_dn, dz, g["dn_o_norm_w"] = _dn_out_bwd(dy, o_dn, z, w["dn_w_out"], w["dn_o_norm_w"])
    g["dn_w_out"] = _wgrad(og_dn, dy, 1024, "wgrad_dn_out")
    dqn, dkn, dv, dbg = _dn_scan_bwd(do_dn, qn, kn, v, bg, ssave, saved)
    dqkv, dba, g["dn_conv_w"], g["dn_a_log"], g["dn_dt_bias"] = _dn_conv_bwd(
        dqn, dkn, dv, dbg, qkv, conv_out, ba, w["dn_conv_w"], w["dn_a_log"], w["dn_dt_bias"])
    dh1, g["dn_norm_w"] = _dn_in_bwd(dqkv, dz, dba, h1, dy, w["dn_norm_w"], w["dn_w_in"])
    g["dn_w_in"] = jnp.concatenate([_wgrad(xn1, dqkv, 1024, "wgrad_dn_qkv"), _wgrad(xn1, dz, 1024, "wgrad_dn_z"),
                                    _wgrad(xn1, dba, 32, "wgrad_dn_ba")], axis=1)
    do, dgate, g["attn_w_out"] = _attn_out_bwd(dh1, o, gate, w["attn_w_out"])
    dq, dkv, g["attn_sinks"], g["attn_q_norm_w"], g["attn_k_norm_w"] = _attn_core_bwd(
        do, q, kv, w["attn_sinks"], w["attn_q_norm_w"], w["attn_k_norm_w"])
    dh0, g["attn_norm_w"] = _attn_in_bwd(dq, dkv, dgate, x, front, dh1, w["attn_norm_w"], w["attn_w_in"])
    g["attn_w_in"] = jnp.concatenate([_wgrad(xn0, dq, 1024, "wgrad_attn_q"), _wgrad(xn0, dkv, 256, "wgrad_attn_kv"),
                                      _wgrad(xn0, dgate, 1024, "wgrad_attn_gate")], axis=1)
    return loss, dh0, g


WEIGHTS = ['meta_tokens', 'attn_norm_w', 'attn_w_in', 'attn_q_norm_w', 'attn_k_norm_w', 'attn_sinks', 'attn_w_out',
           'dn_norm_w', 'dn_w_in', 'dn_conv_w', 'dn_a_log', 'dn_dt_bias', 'dn_o_norm_w', 'dn_w_out']
SMALL = ['attn_norm_w', 'attn_q_norm_w', 'attn_k_norm_w', 'attn_sinks', 'dn_a_log', 'dn_dt_bias', 'dn_o_norm_w']


def kernel(x, meta_tokens, attn_norm_w, attn_w_in, attn_q_norm_w, attn_k_norm_w, attn_sinks, attn_w_out, dn_norm_w, dn_w_in, dn_conv_w, dn_a_log, dn_dt_bias, dn_o_norm_w, dn_w_out, loss_target, m_meta_tokens, m_attn_norm_w, m_attn_w_in, m_attn_q_norm_w, m_attn_k_norm_w, m_attn_sinks, m_attn_w_out, m_dn_norm_w, m_dn_w_in, m_dn_conv_w, m_dn_a_log, m_dn_dt_bias, m_dn_o_norm_w, m_dn_w_out, v_meta_tokens, v_attn_norm_w, v_attn_w_in, v_attn_q_norm_w, v_attn_k_norm_w, v_attn_sinks, v_attn_w_out, v_dn_norm_w, v_dn_w_in, v_dn_conv_w, v_dn_a_log, v_dn_dt_bias, v_dn_o_norm_w, v_dn_w_out):
    shard = dict(meta_tokens=meta_tokens, attn_norm_w=attn_norm_w, attn_w_in=attn_w_in[0], attn_q_norm_w=attn_q_norm_w,
                 attn_k_norm_w=attn_k_norm_w, attn_sinks=attn_sinks, attn_w_out=attn_w_out[0], dn_norm_w=dn_norm_w,
                 dn_w_in=dn_w_in[0], dn_conv_w=dn_conv_w[0], dn_a_log=dn_a_log, dn_dt_bias=dn_dt_bias,
                 dn_o_norm_w=dn_o_norm_w, dn_w_out=dn_w_out[0])
    mom_m = dict(meta_tokens=m_meta_tokens, attn_norm_w=m_attn_norm_w, attn_w_in=m_attn_w_in[0], attn_q_norm_w=m_attn_q_norm_w,
                 attn_k_norm_w=m_attn_k_norm_w, attn_sinks=m_attn_sinks, attn_w_out=m_attn_w_out[0], dn_norm_w=m_dn_norm_w,
                 dn_w_in=m_dn_w_in[0], dn_conv_w=m_dn_conv_w[0], dn_a_log=m_dn_a_log, dn_dt_bias=m_dn_dt_bias,
                 dn_o_norm_w=m_dn_o_norm_w, dn_w_out=m_dn_w_out[0])
    mom_v = dict(meta_tokens=v_meta_tokens, attn_norm_w=v_attn_norm_w, attn_w_in=v_attn_w_in[0], attn_q_norm_w=v_attn_q_norm_w,
                 attn_k_norm_w=v_attn_k_norm_w, attn_sinks=v_attn_sinks, attn_w_out=v_attn_w_out[0], dn_norm_w=v_dn_norm_w,
                 dn_w_in=v_dn_w_in[0], dn_conv_w=v_dn_conv_w[0], dn_a_log=v_dn_a_log, dn_dt_bias=v_dn_dt_bias,
                 dn_o_norm_w=v_dn_o_norm_w, dn_w_out=v_dn_w_out[0])

    def rows_of(d):
        return _pack_rows(d["attn_w_in"], d["dn_w_in"], d["attn_w_out"], d["dn_w_out"], d["meta_tokens"], d["dn_conv_w"],
                          d["dn_norm_w"])

    def small_of(d, extra):
        return _pack_small(*[d[k] for k in SMALL], extra)

    wa, wb, wc = rows_of(shard)
    ga, gb, gc = _gather_two_level([wa.astype(BF16), wb.astype(BF16), wc], "gather_weights")
    full = {k: shard[k] for k in SMALL}
    full["attn_w_in"] = ga[:, :, :288].transpose(1, 0, 2).reshape(1024, 2304)
    full["dn_w_in"] = ga[:, :, 288:].transpose(1, 0, 2).reshape(1024, 6176)
    full["attn_w_out"] = gb[:, :128].reshape(1024, 1024)
    full["dn_w_out"] = gb[:, 128:].reshape(2048, 1024)
    meta_full = gc[:, :16].transpose(1, 0, 2).reshape(N_META, 1024)
    full["dn_conv_w"] = gc[:, 16:32].reshape(N_DEV, 4, 512).transpose(1, 0, 2).reshape(4, 4096)
    full["dn_norm_w"] = gc[:, 32].reshape(1, 1024)

    seq = x.shape[1]
    front = jnp.concatenate([jnp.zeros((FRONT_PAD, D_MODEL), F32), meta_full], axis=0)
    loss, dh0, g = _local_step(x[0], front, loss_target[0], full)
    grad_x = dh0[ATTN_BLOCK:ATTN_BLOCK + seq][None]
    g["meta_tokens"] = dh0[FRONT_PAD:ATTN_BLOCK]

    pa = jnp.concatenate([g["attn_w_in"].reshape(1024, N_DEV, 288), g["dn_w_in"].reshape(1024, N_DEV, 772)],
                         axis=2).transpose(1, 0, 2)
    pb = jnp.concatenate([g["attn_w_out"].reshape(N_DEV, 128, 1024), g["dn_w_out"].reshape(N_DEV, 256, 1024)], axis=1)
    dn_norm8 = jnp.concatenate([g["dn_norm_w"].reshape(N_DEV, 1, 128), jnp.zeros((N_DEV, 7, 128), F32)], axis=1)
    pc = jnp.concatenate([g["meta_tokens"].reshape(N_META, N_DEV, 128).transpose(1, 0, 2),
                          g["dn_conv_w"].reshape(4, N_DEV, 512).transpose(1, 0, 2).reshape(N_DEV, 16, 128), dn_norm8], axis=1)
    ps = small_of(g, loss)
    c = lax.axis_index("c")
    by_core = lambda p: p.astype(BF16).reshape((N_DEV // 2, 2) + p.shape[1:]).swapaxes(0, 1)
    pa2, pb2 = by_core(pa), by_core(pb)
    ra, rb_ = _swap_with_sibling([pa2, pb2], "swap_grads")
    own = lambda p2: lax.dynamic_index_in_dim(p2, c, axis=0, keepdims=False)
    flat = lambda t: t.reshape((-1,) + t.shape[2:])
    sa = _pair_sum(flat(own(pa2)), flat(ra), "pair_sum_a").reshape(ra.shape)
    sb = _pair_sum(flat(own(pb2)), flat(rb_), "pair_sum_b").reshape(rb_.shape)
    xa, xb = _exchange_chips([sa, sb], "exchange_grads")
    xc, xs = _exchange([pc, ps], [True, False], "exchange_small")

    out = {}
    ma, mb, mc = rows_of(mom_m)
    va, vb, vc = rows_of(mom_v)
    ra = _adamw(xa, wa, ma, va, "adamw_a")
    rb = _adamw(xb, wb, mb, vb, "adamw_b")
    rc = _adamw(xc, wc, mc, vc, "adamw_c")
    zero = jnp.zeros((1, 128), F32)
    rs = _adamw(xs, small_of(shard, zero), small_of(mom_m, zero), small_of(mom_v, zero), "adamw_small")
    row_names = ["attn_w_in", "dn_w_in", "attn_w_out", "dn_w_out", "meta_tokens", "dn_conv_w", "dn_norm_w"]
    lead = {"attn_w_in", "dn_w_in", "attn_w_out", "dn_w_out", "dn_conv_w"}
    for kind in range(4):
        vals = dict(zip(row_names, _unpack_rows(ra[kind], rb[kind], rc[kind])))
        small = _unpack_small(rs[kind])
        vals.update(dict(zip(SMALL, small[:7])))
        if kind == 0:
            loss_total = small[7][0, 0]
        out[kind] = [vals[k][None] if k in lead else vals[k] for k in WEIGHTS]
    return (loss_total, grad_x, *out[0], *out[1], *out[2], *out[3])
```

```python
import functools
import math

import jax
import jax.numpy as jnp
from jax import lax
from jax.experimental import pallas as pl
from jax.experimental.pallas import tpu as pltpu

F32, BF16 = jnp.float32, jnp.bfloat16

D_MODEL = 1024
N_META = 16
NORM_EPS = 1e-6
ATTN_HEADS, ATTN_KV_HEADS, ATTN_GROUPS, ATTN_HD = 16, 2, 8, 64
ATTN_BLOCK = 128
FRONT_PAD = ATTN_BLOCK - N_META
DN_HD, DN_K_HEADS, DN_V_HEADS = 128, 8, 16
DN_CHUNK = 128
TRI_BLOCK = 64
SCAN_CHUNKS = 3
SCAN_BWD_CHUNKS = 1
SCAN_FWD_GROUP = 8
SCAN_BWD_GROUP = 8
DN_KEY_W, DN_VAL_W = 1024, 2048
DN_CONV_W = 2 * DN_KEY_W + DN_VAL_W
DN_CONV_K = 4
N_DEV = 8
ROW_BLOCK = 384
WGRAD_ROWS = 1376
VMEM_LIMIT = 56 * 1024 * 1024
NEG = -1e30

ADAM_LR, ADAM_B1, ADAM_B2, ADAM_EPS, ADAM_WD, ADAM_STEP = 0.001, 0.9, 0.999, 1e-08, 0.01, 10

NT = (((1,), (1,)), ((), ()))
TN = (((0,), (0,)), ((), ()))


def _cparams(sem=("arbitrary",)):
    return pltpu.CompilerParams(dimension_semantics=sem, vmem_limit_bytes=VMEM_LIMIT)


def _rms(x, w):
    return x * lax.rsqrt(jnp.mean(x * x, axis=-1, keepdims=True) + NORM_EPS) * w


def _silu(x):
    return x * jax.nn.sigmoid(x)


def _softplus(x):
    return jnp.maximum(x, 0.0) + jnp.log(1.0 + jnp.exp(-jnp.abs(x)))


NN = (((1,), (0,)), ((), ()))


def _mm(a, b, dims):
    return lax.dot_general(a.astype(BF16), b.astype(BF16), dims, preferred_element_type=F32)


@functools.partial(jax.custom_vjp, nondiff_argnums=(2,))
def _bdot_vjp(a, b, dims):
    return _mm(a, b, dims)


def _bdot_fwd(a, b, dims):
    a16, b16 = a.astype(BF16), b.astype(BF16)
    return _mm(a16, b16, dims), (a16, b16, jnp.zeros((), a.dtype), jnp.zeros((), b.dtype))


def _bdot_bwd(dims, res, g):
    a16, b16, ta, tb = res
    g16 = g.astype(BF16)
    if dims == NN:
        da, db = _mm(g16, b16, NT), _mm(a16, g16, TN)
    elif dims == NT:
        da, db = _mm(g16, b16, NN), _mm(g16, a16, TN)
    else:
        da, db = _mm(b16, g16, NT), _mm(a16, g16, NN)
    return da.astype(ta.dtype), db.astype(tb.dtype)


_bdot_vjp.defvjp(_bdot_fwd, _bdot_bwd)


def _bdot(a, b, dims=NN):
    return _bdot_vjp(a, b, dims)


def _row_call(name, body, n_rows, rb, rows, consts, outs, accs=(), reverse=False, scratch=(), halos=()):
    n = n_rows // rb
    assert n * rb == n_rows
    idx = (lambda i: (n - 1 - i, 0)) if reverse else (lambda i: (i, 0))
    in_specs = [pl.BlockSpec((rb, a.shape[1]), idx) for a in rows]
    in_specs += [pl.BlockSpec((hr, a.shape[1]), fn) for a, hr, fn in halos]
    in_specs += [pl.BlockSpec(c.shape, functools.partial(lambda i, nd: (0,) * nd, nd=c.ndim)) for c in consts]
    out_specs = [pl.BlockSpec((rb, c), idx) for c, _ in outs]
    out_specs += [pl.BlockSpec(s, functools.partial(lambda i, nd: (0,) * nd, nd=len(s))) for s, _ in accs]
    out_shape = [jax.ShapeDtypeStruct((n_rows, c), dt) for c, dt in outs]
    out_shape += [jax.ShapeDtypeStruct(s, dt) for s, dt in accs]
    return pl.pallas_call(
        body, grid=(n,), in_specs=in_specs, out_specs=out_specs, out_shape=out_shape,
        scratch_shapes=list(scratch), name=name, compiler_params=_cparams(),
    )(*rows, *[a for a, _, _ in halos], *consts)


def _token_views(x):
    per = ROW_BLOCK // ATTN_BLOCK
    return [(x, ATTN_BLOCK, functools.partial(lambda i, k: (jnp.maximum(per * i - 1 + k, 0), 0), k=k)) for k in range(per)]


def _padded_block(i, front, views):
    first = jnp.where(i == 0, front, views[0][...]) if front is not None else views[0][...]
    return jnp.concatenate([first] + [v[...] for v in views[1:]], axis=0)


def _attn_in_fwd(x, front, norm_w, w_in):
    T = x.shape[0] + ATTN_BLOCK

    def body(xa_ref, xb_ref, xc_ref, front_ref, nw_ref, w_ref, xn_ref, q_ref, kv_ref, gate_ref):
        h = _padded_block(pl.program_id(0), front_ref[...], (xa_ref, xb_ref, xc_ref))
        xn = _rms(h, nw_ref[...]).astype(BF16)
        xn_ref[...] = xn
        q_ref[...] = jnp.dot(xn, w_ref[:, 0:1024], preferred_element_type=F32)
        kv_ref[...] = jnp.dot(xn, w_ref[:, 1024:1280], preferred_element_type=F32)
        gate_ref[...] = jnp.dot(xn, w_ref[:, 1280:2304], preferred_element_type=F32)

    return _row_call("attn_in_fwd", body, T, ROW_BLOCK, [], [front, norm_w, w_in],
                     [(1024, BF16), (1024, F32), (256, F32), (1024, F32)], halos=_token_views(x))


def _attn_bias(n, j):
    C, R = 2 * ATTN_BLOCK + N_META, ATTN_GROUPS * ATTN_BLOCK
    c = lax.broadcasted_iota(jnp.int32, (C, R), 0)
    r = lax.broadcasted_iota(jnp.int32, (C, R), 1)
    ql = r & (ATTN_BLOCK - 1)
    is_meta = c >= 2 * ATTN_BLOCK
    dist_band = ATTN_BLOCK + ql - c
    cmin = jnp.maximum(0, 2 * ATTN_BLOCK - ATTN_BLOCK * n)
    valid_band = (c >= cmin) & (dist_band >= 0) & (dist_band < ATTN_BLOCK)
    dist_meta = ATTN_BLOCK * n + ql - FRONT_PAD - (c - 2 * ATTN_BLOCK)
    valid = (is_meta & (dist_meta >= 0)) | (jnp.logical_not(is_meta) & valid_band)
    dist = jnp.minimum(jnp.where(is_meta, dist_meta, dist_band), ATTN_BLOCK).astype(F32)
    rr = lax.broadcasted_iota(jnp.int32, (1, R), 1)
    head = (rr >> 7).astype(F32) + float(ATTN_GROUPS * j + 1)
    slope = jnp.exp(head * (-0.5 * math.log(2.0)))
    return jnp.where(valid, slope * dist, -NEG)


def _attn_tables(n, refresh, bias_ref):
    @pl.when(refresh)
    def _():
        for j in range(ATTN_KV_HEADS):
            bias_ref[j] = _attn_bias(n, j)


def _attn_table_scratch():
    return [pltpu.VMEM((ATTN_KV_HEADS, 2 * ATTN_BLOCK + N_META, ATTN_GROUPS * ATTN_BLOCK), F32)]


def _attn_groups(q_t, k, v, sinkrow, qnw_col, knw, bias, late_norm=True):
    n = range(len(q_t))
    qn = [q_t[j] * (lax.rsqrt(jnp.mean(q_t[j] * q_t[j], axis=0, keepdims=True) + NORM_EPS) * (ATTN_HD ** -0.5)) * qnw_col
          for j in n]
    kn = [_rms(k[j], knw) for j in n]
    s = [_bdot(kn[j], qn[j]) - bias[j] for j in n]
    m = [lax.stop_gradient(jnp.maximum(jnp.max(s[j], axis=0, keepdims=True), sinkrow[j])) for j in n]
    e = [jnp.exp(s[j] - m[j]) for j in n]
    inv = [1.0 / (jnp.sum(e[j], axis=0, keepdims=True) + jnp.exp(sinkrow[j] - m[j])) for j in n]
    if late_norm:
        return [_bdot(v[j], e[j], TN) * inv[j] for j in n]
    return [_bdot(v[j], e[j] * inv[j], TN) for j in n]


def _sink_row(sinks_ref, j):
    rr = lax.broadcasted_iota(jnp.int32, (1, ATTN_GROUPS * ATTN_BLOCK), 1) >> 7
    row = jnp.zeros((1, ATTN_GROUPS * ATTN_BLOCK), F32)
    for hl in range(ATTN_GROUPS):
        row = jnp.where(rr == hl, sinks_ref[0, ATTN_GROUPS * j + hl], row)
    return row


def _heads_to_lanes(ref, j):
    return jnp.concatenate([ref[:, ATTN_HD * h:ATTN_HD * (h + 1)].T
                            for h in range(ATTN_GROUPS * j, ATTN_GROUPS * (j + 1))], axis=1)


def _lanes_to_heads(ref, j, x_t):
    for hl in range(ATTN_GROUPS):
        h = ATTN_GROUPS * j + hl
        ref[:, ATTN_HD * h:ATTN_HD * (h + 1)] = x_t[:, ATTN_BLOCK * hl:ATTN_BLOCK * (hl + 1)].T


def _attn_kv_tiles(kvp_ref, kvc_ref, kvm_ref, j):
    ksl = slice(ATTN_HD * j, ATTN_HD * (j + 1))
    vsl = slice(128 + ATTN_HD * j, 128 + ATTN_HD * (j + 1))
    k = jnp.concatenate([kvp_ref[:, ksl], kvc_ref[:, ksl], kvm_ref[FRONT_PAD:, ksl]], axis=0)
    v = jnp.concatenate([kvp_ref[:, vsl], kvc_ref[:, vsl], kvm_ref[FRONT_PAD:, vsl]], axis=0)
    return k, v


def _attn_core_fwd(q, kv, sinks, qnw, knw):
    T = q.shape[0]
    nb = T // ATTN_BLOCK

    def body(sinks_ref, q_ref, kvc_ref, kvp_ref, kvm_ref, qnw_ref, knw_ref, o_ref, bias_ref):
        n = pl.program_id(0)
        _attn_tables(n, n <= 2, bias_ref)
        kvh = range(ATTN_KV_HEADS)
        kv_tiles = [_attn_kv_tiles(kvp_ref, kvc_ref, kvm_ref, j) for j in kvh]
        o_t = _attn_groups([_heads_to_lanes(q_ref, j) for j in kvh], [t[0] for t in kv_tiles], [t[1] for t in kv_tiles],
                           [_sink_row(sinks_ref, j) for j in kvh], qnw_ref[...], knw_ref[...], [bias_ref[j] for j in kvh])
        for j in kvh:
            _lanes_to_heads(o_ref, j, o_t[j])

    return pl.pallas_call(
        body, grid=(nb,),
        in_specs=[pl.BlockSpec(memory_space=pltpu.SMEM),
                  pl.BlockSpec((ATTN_BLOCK, 1024), lambda i: (i, 0)),
                  pl.BlockSpec((ATTN_BLOCK, 256), lambda i: (i, 0)),
                  pl.BlockSpec((ATTN_BLOCK, 256), lambda i: (jnp.maximum(i - 1, 0), 0)),
                  pl.BlockSpec((ATTN_BLOCK, 256), lambda i: (0, 0)),
                  pl.BlockSpec((ATTN_HD, 1), lambda i: (0, 0)),
                  pl.BlockSpec((1, ATTN_HD), lambda i: (0, 0))],
        out_specs=pl.BlockSpec((ATTN_BLOCK, 1024), lambda i: (i, 0)),
        out_shape=jax.ShapeDtypeStruct((T, 1024), F32),
        scratch_shapes=_attn_table_scratch(),
        name="attn_core_fwd", compiler_params=_cparams(),
    )(sinks, q, kv, kv, kv, qnw.reshape(ATTN_HD, 1), knw)


def _attn_out_fwd(o, gate, x, front, w_out):
    T = o.shape[0]

    def body(o_ref, g_ref, xa_ref, xb_ref, xc_ref, front_ref, w_ref, h1_ref):
        h = _padded_block(pl.program_id(0), front_ref[...], (xa_ref, xb_ref, xc_ref))
        og = o_ref[...] * _silu(g_ref[...])
        h1_ref[...] = h + _bdot(og, w_ref[...])

    return _row_call("attn_out_fwd", body, T, ROW_BLOCK, [o, gate], [front, w_out], [(1024, F32)], halos=_token_views(x))[0]


def _wgrad(xn, du, cg, name):
    T, kdim = xn.shape
    cdim = du.shape[1]
    rows = WGRAD_ROWS if T % WGRAD_ROWS == 0 else ROW_BLOCK
    nr, nc = T // rows, cdim // cg
    assert nc * cg == cdim

    def body(x_ref, du_ref, dw_ref):
        @pl.when(pl.program_id(1) == 0)
        def _():
            dw_ref[...] = jnp.zeros_like(dw_ref)
        dw_ref[...] += _bdot(x_ref[...], du_ref[...], TN)

    return pl.pallas_call(
        body, grid=(nc, nr),
        in_specs=[pl.BlockSpec((rows, kdim), lambda j, i: (i, 0)),
                  pl.BlockSpec((rows, cg), lambda j, i: (i, j))],
        out_specs=pl.BlockSpec((kdim, cg), lambda j, i: (0, j)),
        out_shape=jax.ShapeDtypeStruct((kdim, cdim), F32),
        name=name, compiler_params=_cparams(("arbitrary", "arbitrary")),
    )(xn, du)


def _attn_out_bwd(dh1, o, gate, w_out):
    T = o.shape[0]

    def body(dh_ref, o_ref, g_ref, w_ref, do_ref, dg_ref, dw_ref):
        @pl.when(pl.program_id(0) == 0)
        def _():
            dw_ref[...] = jnp.zeros_like(dw_ref)
        dh = dh_ref[...]
        dog = _bdot(dh, w_ref[...], NT)
        og, vjp = jax.vjp(lambda o_, g_: o_ * _silu(g_), o_ref[...], g_ref[...])
        do, dg = vjp(dog)
        do_ref[...] = do
        dg_ref[...] = dg
        dw_ref[...] += _bdot(og, dh, TN)

    return _row_call("attn_out_bwd", body, T, ROW_BLOCK, [dh1, o, gate], [w_out],
                     [(1024, F32), (1024, F32)], [((1024, 1024), F32)])


def _attn_core_bwd(do, q, kv, sinks, qnw, knw):
    T = q.shape[0]
    nb = T // ATTN_BLOCK
    rev = lambda i: nb - 1 - i

    def body(sinks_ref, do_ref, q_ref, kvc_ref, kvp_ref, kvm_ref, qnw_ref, knw_ref,
             dq_ref, dkv_ref, dsinks_ref, dqnw_ref, dknw_ref, carry_ref, meta_ref, bias_ref):
        step = pl.program_id(0)
        n = rev(step)
        _attn_tables(n, (step == 0) | (n <= 1), bias_ref)

        @pl.when(step == 0)
        def _():
            carry_ref[...] = jnp.zeros_like(carry_ref)
            meta_ref[...] = jnp.zeros_like(meta_ref)
            dsinks_ref[...] = jnp.zeros_like(dsinks_ref)
            dqnw_ref[...] = jnp.zeros_like(dqnw_ref)
            dknw_ref[...] = jnp.zeros_like(dknw_ref)

        lane16 = lax.broadcasted_iota(jnp.int32, (1, ATTN_HEADS), 1)
        dsinks = jnp.zeros((1, ATTN_HEADS), F32)
        kvh = range(ATTN_KV_HEADS)
        kv_tiles = [_attn_kv_tiles(kvp_ref, kvc_ref, kvm_ref, j) for j in kvh]
        fn = functools.partial(_attn_groups, bias=[bias_ref[j] for j in kvh], late_norm=False)
        _, vjp = jax.vjp(fn, [_heads_to_lanes(q_ref, j) for j in kvh], [t[0] for t in kv_tiles], [t[1] for t in kv_tiles],
                         [_sink_row(sinks_ref, j) for j in kvh], qnw_ref[...], knw_ref[...])
        dq_t, dks, dvs, dsr, dqn, dkn = vjp([_heads_to_lanes(do_ref, j) for j in kvh])
        dqnw_ref[...] += dqn
        dknw_ref[...] += dkn
        for j in kvh:
            _lanes_to_heads(dq_ref, j, dq_t[j])
            for hl in range(ATTN_GROUPS):
                dsinks = dsinks + jnp.where(lane16 == ATTN_GROUPS * j + hl,
                                            jnp.sum(dsr[j][:, ATTN_BLOCK * hl:ATTN_BLOCK * (hl + 1)]), 0.0)
            ksl = slice(ATTN_HD * j, ATTN_HD * (j + 1))
            vsl = slice(128 + ATTN_HD * j, 128 + ATTN_HD * (j + 1))
            for sl, d in ((ksl, dks[j]), (vsl, dvs[j])):
                dkv_ref[:, sl] = d[ATTN_BLOCK:2 * ATTN_BLOCK, :] + carry_ref[:, sl]
                carry_ref[:, sl] = d[0:ATTN_BLOCK, :]
                meta_ref[:, sl] += d[2 * ATTN_BLOCK:, :]
        dsinks_ref[...] += dsinks

        @pl.when(n == 0)
        def _():
            dkv_ref[FRONT_PAD:, :] += meta_ref[...]

    dq, dkv, dsinks, dqnw, dknw = pl.pallas_call(
        body, grid=(nb,),
        in_specs=[pl.BlockSpec(memory_space=pltpu.SMEM),
                  pl.BlockSpec((ATTN_BLOCK, 1024), lambda i: (rev(i), 0)),
                  pl.BlockSpec((ATTN_BLOCK, 1024), lambda i: (rev(i), 0)),
                  pl.BlockSpec((ATTN_BLOCK, 256), lambda i: (rev(i), 0)),
                  pl.BlockSpec((ATTN_BLOCK, 256), lambda i: (jnp.maximum(rev(i) - 1, 0), 0)),
                  pl.BlockSpec((ATTN_BLOCK, 256), lambda i: (0, 0)),
                  pl.BlockSpec((ATTN_HD, 1), lambda i: (0, 0)),
                  pl.BlockSpec((1, ATTN_HD), lambda i: (0, 0))],
        out_specs=[pl.BlockSpec((ATTN_BLOCK, 1024), lambda i: (rev(i), 0)),
                   pl.BlockSpec((ATTN_BLOCK, 256), lambda i: (rev(i), 0)),
                   pl.BlockSpec((1, ATTN_HEADS), lambda i: (0, 0)),
                   pl.BlockSpec((ATTN_HD, 1), lambda i: (0, 0)),
                   pl.BlockSpec((1, ATTN_HD), lambda i: (0, 0))],
        out_shape=[jax.ShapeDtypeStruct((T, 1024), F32), jax.ShapeDtypeStruct((T, 256), F32),
                   jax.ShapeDtypeStruct((1, ATTN_HEADS), F32), jax.ShapeDtypeStruct((ATTN_HD, 1), F32),
                   jax.ShapeDtypeStruct((1, ATTN_HD), F32)],
        scratch_shapes=[pltpu.VMEM((ATTN_BLOCK, 256), F32), pltpu.VMEM((N_META, 256), F32)] + _attn_table_scratch(),
        name="attn_core_bwd", compiler_params=_cparams(),
    )(sinks, do, q, kv, kv, kv, qnw.reshape(ATTN_HD, 1), knw)
    return dq, dkv, dsinks, dqnw.reshape(1, ATTN_HD), dknw


def _attn_in_bwd(dq, dkv, dgate, x, front, dh1, norm_w, w_in):
    T = dq.shape[0]
    n = T // ROW_BLOCK
    per = ROW_BLOCK // ATTN_BLOCK
    assert n >= 3

    def body(dq_ref, dkv_ref, dg_ref, dh1_ref, xa_ref, xb_ref, xc_ref, front_ref, nw_ref, w_ref,
             gx_ref, dfront_ref, dnw_ref, buf_ref, sems):
        i = pl.program_id(0)
        slot = i % 2

        def piece(step, k, s):
            return pltpu.make_async_copy(buf_ref.at[s, pl.ds(ATTN_BLOCK * k, ATTN_BLOCK)],
                                         gx_ref.at[pl.ds((per * step - 1 + k) * ATTN_BLOCK, ATTN_BLOCK)], sems.at[s, k])

        @pl.when(i == 0)
        def _():
            dnw_ref[...] = jnp.zeros_like(dnw_ref)
        for k in range(per):
            @pl.when((i >= 2) & ((k > 0) | (i > 2)))
            def _():
                piece(i - 2, k, slot).wait()
        h = _padded_block(i, front_ref[...], (xa_ref, xb_ref, xc_ref))
        dxn = (_bdot(dq_ref[...], w_ref[:, 0:1024], NT) + _bdot(dkv_ref[...], w_ref[:, 1024:1280], NT)
               + _bdot(dg_ref[...], w_ref[:, 1280:2304], NT))
        _, vjp = jax.vjp(_rms, h, nw_ref[...])
        dh, dnw = vjp(dxn)
        dnw_ref[...] += dnw
        buf_ref[slot] = dh1_ref[...] + dh

        @pl.when(i == 0)
        def _():
            dfront_ref[...] = buf_ref[0, 0:ATTN_BLOCK, :]
        for k in range(per):
            @pl.when((k > 0) | (i > 0))
            def _():
                piece(i, k, slot).start()

        @pl.when(i == n - 1)
        def _():
            for k in range(per):
                piece(i, k, slot).wait()
                piece(i - 1, k, 1 - slot).wait()

    idx = lambda i: (i, 0)
    const = lambda a: pl.BlockSpec(a.shape, functools.partial(lambda i, nd: (0,) * nd, nd=a.ndim))
    rows = [dq, dkv, dgate, dh1]
    views = _token_views(x)
    return pl.pallas_call(
        body, grid=(n,),
        in_specs=[pl.BlockSpec((ROW_BLOCK, a.shape[1]), idx) for a in rows]
        + [pl.BlockSpec((hr, a.shape[1]), fn) for a, hr, fn in views] + [const(front), const(norm_w), const(w_in)],
        out_specs=[pl.BlockSpec(memory_space=pltpu.HBM), pl.BlockSpec((ATTN_BLOCK, D_MODEL), lambda i: (0, 0)),
                   pl.BlockSpec((1, D_MODEL), lambda i: (0, 0))],
        out_shape=[jax.ShapeDtypeStruct(x.shape, F32), jax.ShapeDtypeStruct((ATTN_BLOCK, D_MODEL), F32),
                   jax.ShapeDtypeStruct((1, D_MODEL), F32)],
        scratch_shapes=[pltpu.VMEM((2, ROW_BLOCK, D_MODEL), F32), pltpu.SemaphoreType.DMA((2, per))],
        name="attn_in_bwd", compiler_params=_cparams(),
    )(*rows, *[a for a, _, _ in views], front, norm_w, w_in)


def _dn_in_fwd(h1, norm_w, w_in):
    T = h1.shape[0]

    def body(h_ref, nw_ref, w_ref, xn_ref, qkv_ref, z_ref, ba_ref):
        xn = _rms(h_ref[...], nw_ref[...]).astype(BF16)
        xn_ref[...] = xn
        qkv_ref[...] = jnp.dot(xn, w_ref[:, 0:4096], preferred_element_type=F32)
        z_ref[...] = jnp.dot(xn, w_ref[:, 4096:6144], preferred_element_type=F32)
        ba_ref[...] = jnp.dot(xn, w_ref[:, 6144:6176], preferred_element_type=F32)

    return _row_call("dn_in_fwd", body, T, ROW_BLOCK, [h1], [norm_w, w_in],
                     [(1024, BF16), (4096, F32), (2048, F32), (32, F32)])


def _shift_down(cur, prev8, s):
    i8 = lax.broadcasted_iota(jnp.int32, (8, cur.shape[1]), 0)
    r = pltpu.roll(cur, s, 0)
    head = jnp.where(i8 < s, pltpu.roll(prev8, s, 0), r[0:8])
    return jnp.concatenate([head, r[8:]], axis=0)


def _shift_up(cur, next8, s):
    n = cur.shape[0]
    i8 = lax.broadcasted_iota(jnp.int32, (8, cur.shape[1]), 0)
    r = pltpu.roll(cur, n - s, 0)
    tail = jnp.where(i8 >= 8 - s, pltpu.roll(next8, 8 - s, 0), r[n - 8:])
    return jnp.concatenate([r[:n - 8], tail], axis=0)


def _conv_taps(cur, prev8):
    return [cur] + [_shift_down(cur, prev8, s) for s in range(1, DN_CONV_K)]


def _conv_tile(taps, w):
    out = w[3:4, :] * taps[0]
    for s in range(1, DN_CONV_K):
        out = out + w[3 - s:4 - s, :] * taps[s]
    return out


def _l2n(a, scale):
    return a * (lax.rsqrt(jnp.sum(a * a, axis=-1, keepdims=True) + NORM_EPS) * scale)


def _dn_post_tile(c, t):
    a = _silu(c)
    if t < DN_K_HEADS:
        return _l2n(a, DN_HD ** -0.5)
    if t < 2 * DN_K_HEADS:
        return _l2n(a, 1.0)
    return a


def _dn_beta_g(ba, a_log, dt_bias, live):
    beta = jax.nn.sigmoid(ba[:, 0:DN_V_HEADS]) * live
    g = -jnp.exp(a_log) * _softplus(ba[:, DN_V_HEADS:] + dt_bias) * live
    return beta, g


def _live_rows(i, rb):
    rows = i * rb + lax.broadcasted_iota(jnp.int32, (rb, 1), 0)
    return (rows >= FRONT_PAD).astype(F32)


def _halo_spec_args(x, rb):
    per = rb // 8
    return (x, 8, lambda i: (jnp.maximum(i * per - 1, 0), 0))


def _dn_conv_fwd(qkv, ba, conv_w, a_log, dt_bias):
    T = qkv.shape[0]

    def body(x_ref, ba_ref, halo_ref, cw_ref, al_ref, dtb_ref, q_ref, k_ref, v_ref, bg_ref, c_ref):
        i = pl.program_id(0)
        first = (i > 0).astype(F32)
        for t in range(DN_CONV_W // 128):
            cols = slice(128 * t, 128 * (t + 1))
            c = _conv_tile(_conv_taps(x_ref[:, cols], halo_ref[:, cols] * first), cw_ref[:, cols])
            c_ref[:, cols] = c.astype(BF16)
            out = _dn_post_tile(c, t)
            if t < DN_K_HEADS:
                q_ref[:, cols] = out
            elif t < 2 * DN_K_HEADS:
                k_ref[:, 128 * (t - 8):128 * (t - 7)] = out
            else:
                v_ref[:, 128 * (t - 16):128 * (t - 15)] = out
        beta, g = _dn_beta_g(ba_ref[...], al_ref[...], dtb_ref[...], _live_rows(i, ROW_BLOCK))
        bg_ref[:, 0:DN_V_HEADS] = beta
        bg_ref[:, DN_V_HEADS:] = g

    return _row_call("dn_conv_fwd", body, T, ROW_BLOCK, [qkv, ba], [conv_w, a_log, dt_bias],
                     [(1024, F32), (1024, F32), (2048, F32), (32, F32), (4096, BF16)], halos=[_halo_spec_args(qkv, ROW_BLOCK)])


def _chunk_masks():
    r = lax.broadcasted_iota(jnp.int32, (DN_CHUNK, DN_CHUNK), 0)
    c = lax.broadcasted_iota(jnp.int32, (DN_CHUNK, DN_CHUNK), 1)
    return r >= c, r > c, r == c, r <= c


def _tri_inv_block(x):
    B = TRI_BLOCK
    n = range(len(x))
    r_, c_ = lax.broadcasted_iota(jnp.int32, (B, B), 0), lax.broadcasted_iota(jnp.int32, (B, B), 1)
    ainv = [jnp.where(r_ == c_, 1.0, 0.0) + x[h] for h in n]
    p = [_bdot(x[h], x[h]) for h in n]
    for _ in range(B.bit_length() - 3):
        r = [_bdot(jnp.concatenate([p[h], ainv[h]], axis=0), p[h]) for h in n]
        ainv = [ainv[h] + r[h][B:] for h in n]
        p = [r[h][:B] for h in n]
    return [ainv[h] + _bdot(ainv[h], p[h]) for h in n]


def _tri_inv(x):
    B = TRI_BLOCK
    assert DN_CHUNK == 2 * B
    n = len(x)
    diag = _tri_inv_block([x[h][:B, :B] for h in range(n)] + [x[h][B:, B:] for h in range(n)])
    a11, a22 = diag[:n], diag[n:]
    a21 = [_bdot(_bdot(a22[h], x[h][B:, :B]), a11[h]) for h in range(n)]
    zero = jnp.zeros((B, B), F32)
    return [jnp.concatenate([jnp.concatenate([a11[h], zero], axis=1), jnp.concatenate([a21[h], a22[h]], axis=1)], axis=0)
            for h in range(n)]


@jax.custom_vjp
def _tri_inv_known(x, a):
    return a


def _tri_inv_known_fwd(x, a):
    return a, a


def _tri_inv_known_bwd(a, da):
    return [_bdot(_bdot(a[h], da[h], TN), a[h], NT) for h in range(len(a))], [jnp.zeros_like(t) for t in a]


_tri_inv_known.defvjp(_tri_inv_known_fwd, _tri_inv_known_bwd)


@jax.custom_vjp
def _known(computed, value):
    return value


def _known_fwd(computed, value):
    return value, None


def _known_bwd(_, g):
    return g, jax.tree.map(jnp.zeros_like, g)


_known.defvjp(_known_fwd, _known_bwd)


def _dn_chunk_step(S, q, k, v, beta, g, masks, known=None):
    causal, strict, eye, upper = masks
    C, W = DN_CHUNK, DN_HD
    heads = range(len(v))
    k_t = [k[j].T for j in range(len(k))]
    qk_kk = [_bdot(jnp.concatenate([q[j], k[j]], axis=0), k_t[j]) for j in range(len(q))]
    if known is not None:
        qk_kk = _known(qk_kk, known["qk_kk"])
    g_b = [jnp.broadcast_to(g[h], (C, C)) for h in heads]
    beta_b = [jnp.broadcast_to(beta[h], (C, W)) for h in heads]
    g_row = [jnp.sum(jnp.where(eye, g_b[h], 0.0), axis=0, keepdims=True) for h in heads]
    gc_col = [jnp.sum(jnp.where(causal, g_row[h], 0.0), axis=1, keepdims=True) for h in heads]
    gc_row = [jnp.sum(jnp.where(upper, g_b[h], 0.0), axis=0, keepdims=True) for h in heads]
    g_last = [jnp.sum(g_row[h], axis=1, keepdims=True) for h in heads]
    gc_b = [jnp.broadcast_to(gc_col[h], (C, W)) for h in heads]
    decay = [jnp.exp(jnp.where(causal, gc_b[h][:, :C] - gc_row[h], NEG)) for h in heads]
    eg_b = [jnp.exp(gc_b[h]) for h in heads]
    x = [jnp.where(strict, qk_kk[h // 2][C:] * beta_b[h][:, :C] * decay[h], 0.0) * -1.0 for h in heads]
    ainv = _tri_inv(x) if known is None else _tri_inv_known(x, known["inv"])
    uw = [_bdot(ainv[h], jnp.concatenate([v[h] * beta_b[h], k[h // 2] * (beta_b[h] * eg_b[h])], axis=1)) for h in heads]
    if known is not None:
        uw = _known(uw, known["uw"])
    q_eg = [q[h // 2] * eg_b[h] for h in heads]
    attn = [qk_kk[h // 2][:C] * decay[h] for h in heads]
    k_st = [k_t[h // 2] * jnp.exp(g_last[h] - gc_row[h]) for h in heads]
    s_dec = [jnp.exp(g_last[h]) for h in heads]
    prep = (uw, q_eg, attn, k_st, s_dec)
    if S is None:
        return prep, dict(inv=ainv, uw=uw, qk_kk=qk_kk)
    s_new, o, _ = _dn_chunk_tail(S, prep, None if known is None else known["v_new"])
    return s_new, o


def _dn_chunk_tail(S, prep, known_v_new=None):
    uw, q_eg, attn, k_st, s_dec = prep
    C, W = DN_CHUNK, DN_HD
    heads = range(len(uw))
    ws_qs = [_bdot(jnp.concatenate([uw[h][:, W:], q_eg[h]], axis=0), S[h]) for h in heads]
    v_new = [uw[h][:, :W] - ws_qs[h][:C] for h in heads]
    if known_v_new is not None:
        v_new = _known(v_new, known_v_new)
    o = [ws_qs[h][C:] + _bdot(attn[h], v_new[h]) for h in heads]
    s_new = [S[h] * s_dec[h] + _bdot(k_st[h], v_new[h]) for h in heads]
    return s_new, o, v_new


def _dn_chunk_tiles(q_ref, k_ref, v_ref, bg_ref, c, first, count):
    rows = slice(DN_CHUNK * c, DN_CHUNK * (c + 1))
    q = [q_ref[rows, 128 * j:128 * (j + 1)] for j in range(first // 2, (first + count) // 2)]
    k = [k_ref[rows, 128 * j:128 * (j + 1)] for j in range(first // 2, (first + count) // 2)]
    v = [v_ref[rows, 128 * h:128 * (h + 1)] for h in range(first, first + count)]
    beta = [bg_ref[rows, h:h + 1] for h in range(first, first + count)]
    g = [bg_ref[rows, DN_V_HEADS + h:DN_V_HEADS + h + 1] for h in range(first, first + count)]
    return q, k, v, beta, g


def _dn_scan_fwd(qn, kn, v, bg):
    T = qn.shape[0]
    nc = T // DN_CHUNK
    rows = SCAN_CHUNKS * DN_CHUNK
    assert nc % SCAN_CHUNKS == 0

    def body(q_ref, k_ref, v_ref, bg_ref, o_ref, ssave_ref, inv_ref, uw_ref, vn_ref, qk_ref, s_ref):
        @pl.when(pl.program_id(0) == 0)
        def _():
            s_ref[...] = jnp.zeros_like(s_ref)
        masks = _chunk_masks()
        for first in range(0, DN_V_HEADS, SCAN_FWD_GROUP):
            heads = range(first, first + SCAN_FWD_GROUP)
            preps = [_dn_chunk_step(None, *_dn_chunk_tiles(q_ref, k_ref, v_ref, bg_ref, c, first, SCAN_FWD_GROUP), masks)
                     for c in range(SCAN_CHUNKS)]
            state = [s_ref[h] for h in heads]
            for c, (prep, saved) in enumerate(preps):
                for i, h in enumerate(heads):
                    ssave_ref[c, h] = state[i]
                    inv_ref[c, h] = saved["inv"][i].astype(BF16)
                    uw_ref[c, h] = saved["uw"][i].astype(BF16)
                for i, j in enumerate(range(first // 2, (first + SCAN_FWD_GROUP) // 2)):
                    qk_ref[c, j] = saved["qk_kk"][i].astype(BF16)
                state, o, v_new = _dn_chunk_tail(state, prep)
                for i, h in enumerate(heads):
                    o_ref[DN_CHUNK * c:DN_CHUNK * (c + 1), 128 * h:128 * (h + 1)] = o[i]
                    vn_ref[c, h] = v_new[i].astype(BF16)
            for i, h in enumerate(heads):
                s_ref[h] = state[i]

    return pl.pallas_call(
        body, grid=(nc // SCAN_CHUNKS,),
        in_specs=[pl.BlockSpec((rows, 1024), lambda i: (i, 0)),
                  pl.BlockSpec((rows, 1024), lambda i: (i, 0)),
                  pl.BlockSpec((rows, 2048), lambda i: (i, 0)),
                  pl.BlockSpec((rows, 32), lambda i: (i, 0))],
        out_specs=[pl.BlockSpec((rows, 2048), lambda i: (i, 0)),
                   pl.BlockSpec((SCAN_CHUNKS, DN_V_HEADS, DN_HD, DN_HD), lambda i: (i, 0, 0, 0)),
                   pl.BlockSpec((SCAN_CHUNKS, DN_V_HEADS, DN_CHUNK, DN_CHUNK), lambda i: (i, 0, 0, 0)),
                   pl.BlockSpec((SCAN_CHUNKS, DN_V_HEADS, DN_CHUNK, 2 * DN_HD), lambda i: (i, 0, 0, 0)),
                   pl.BlockSpec((SCAN_CHUNKS, DN_V_HEADS, DN_CHUNK, DN_HD), lambda i: (i, 0, 0, 0)),
                   pl.BlockSpec((SCAN_CHUNKS, DN_K_HEADS, 2 * DN_CHUNK, DN_CHUNK), lambda i: (i, 0, 0, 0))],
        out_shape=[jax.ShapeDtypeStruct((T, 2048), F32),
                   jax.ShapeDtypeStruct((nc, DN_V_HEADS, DN_HD, DN_HD), F32),
                   jax.ShapeDtypeStruct((nc, DN_V_HEADS, DN_CHUNK, DN_CHUNK), BF16),
                   jax.ShapeDtypeStruct((nc, DN_V_HEADS, DN_CHUNK, 2 * DN_HD), BF16),
                   jax.ShapeDtypeStruct((nc, DN_V_HEADS, DN_CHUNK, DN_HD), BF16),
                   jax.ShapeDtypeStruct((nc, DN_K_HEADS, 2 * DN_CHUNK, DN_CHUNK), BF16)],
        scratch_shapes=[pltpu.VMEM((DN_V_HEADS, DN_HD, DN_HD), F32)],
        name="dn_scan_fwd", compiler_params=_cparams(),
    )(qn, kn, v, bg)


def _dn_gate_tile(o, z, onw):
    return _rms(o, onw) * _silu(z)


def _dn_out_fwd(o, z, h1, target, w_out, onw):
    T = o.shape[0]

    def body(o_ref, z_ref, h_ref, ta_ref, tb_ref, tc_ref, w_ref, onw_ref, dy_ref, og_ref, loss_ref):
        i = pl.program_id(0)

        @pl.when(i == 0)
        def _():
            loss_ref[...] = jnp.zeros_like(loss_ref)
        for h in range(DN_V_HEADS):
            cols = slice(128 * h, 128 * (h + 1))
            og_ref[:, cols] = _dn_gate_tile(o_ref[:, cols], z_ref[:, cols], onw_ref[...]).astype(BF16)
        y = h_ref[...] + jnp.dot(og_ref[...], w_ref[...], preferred_element_type=F32)
        rows = i * ROW_BLOCK + lax.broadcasted_iota(jnp.int32, (ROW_BLOCK, 1), 0)
        diff = jnp.where(rows >= FRONT_PAD + N_META, y - _padded_block(i, None, (ta_ref, tb_ref, tc_ref)), 0.0)
        dy_ref[...] = diff * (1.0 / D_MODEL)
        loss_ref[...] += jnp.sum(diff * diff) * (0.5 / D_MODEL)

    return _row_call("dn_out_fwd", body, T, ROW_BLOCK, [o, z, h1], [w_out, onw],
                     [(1024, F32), (2048, BF16)], [((1, 128), F32)], halos=_token_views(target))


def _dn_out_bwd(dy, o, z, w_out, onw):
    T = o.shape[0]

    def body(dy_ref, o_ref, z_ref, w_ref, onw_ref, do_ref, dz_ref, donw_ref, dog_ref):
        @pl.when(pl.program_id(0) == 0)
        def _():
            donw_ref[...] = jnp.zeros_like(donw_ref)
        dy = dy_ref[...].astype(BF16)
        donw = jnp.zeros((1, DN_HD), F32)
        for half in range(2):
            hcols = slice(1024 * half, 1024 * (half + 1))
            dog_ref[:, hcols] = lax.dot_general(dy, w_ref[hcols, :], NT, preferred_element_type=F32)
        for h in range(DN_V_HEADS):
            cols = slice(128 * h, 128 * (h + 1))
            _, vjp = jax.vjp(_dn_gate_tile, o_ref[:, cols], z_ref[:, cols], onw_ref[...])
            do, dz, dn = vjp(dog_ref[:, cols])
            do_ref[:, cols] = do
            dz_ref[:, cols] = dz
            donw = donw + dn
        donw_ref[...] += donw

    return _row_call("dn_out_bwd", body, T, ROW_BLOCK, [dy, o, z], [w_out, onw],
                     [(2048, F32), (2048, F32)], [((1, DN_HD), F32)], scratch=[pltpu.VMEM((ROW_BLOCK, 2048), F32)])


def _dn_scan_bwd(do, qn, kn, v, bg, ssave, saved):
    T = qn.shape[0]
    nc = T // DN_CHUNK
    chunks = SCAN_BWD_CHUNKS
    ns = nc // chunks
    rows = chunks * DN_CHUNK
    rev = lambda i: ns - 1 - i

    def body(do_ref, q_ref, k_ref, v_ref, bg_ref, ss_ref, inv_ref, uw_ref, vn_ref, qk_ref,
             dq_ref, dk_ref, dv_ref, dbg_ref, ds_ref):
        @pl.when(pl.program_id(0) == 0)
        def _():
            ds_ref[...] = jnp.zeros_like(ds_ref)
        lane32 = lax.broadcasted_iota(jnp.int32, (1, 2 * DN_V_HEADS), 1)
        masks = _chunk_masks()
        dbg = [jnp.zeros((DN_CHUNK, 2 * DN_V_HEADS), F32) for _ in range(chunks)]
        for first in range(0, DN_V_HEADS, SCAN_BWD_GROUP):
            heads = range(first, first + SCAN_BWD_GROUP)
            vjps = []
            for c in range(chunks):
                known = dict(inv=[inv_ref[c, h].astype(F32) for h in heads], uw=[uw_ref[c, h].astype(F32) for h in heads],
                             v_new=[vn_ref[c, h].astype(F32) for h in heads],
                             qk_kk=[qk_ref[c, j].astype(F32) for j in range(first // 2, (first + SCAN_BWD_GROUP) // 2)])
                fn = functools.partial(_dn_chunk_step, masks=masks, known=known)
                vjps.append(jax.vjp(fn, [ss_ref[c, h] for h in heads],
                                    *_dn_chunk_tiles(q_ref, k_ref, v_ref, bg_ref, c, first, SCAN_BWD_GROUP))[1])
            ds = [ds_ref[h] for h in heads]
            for c in reversed(range(chunks)):
                crows = slice(DN_CHUNK * c, DN_CHUNK * (c + 1))
                ds, dq, dk, dv, dbeta, dg = vjps[c]((ds, [do_ref[crows, 128 * h:128 * (h + 1)] for h in heads]))
                for i, h in enumerate(heads):
                    dv_ref[crows, 128 * h:128 * (h + 1)] = dv[i]
                    dbg[c] = dbg[c] + jnp.where(lane32 == h, dbeta[i], 0.0) + jnp.where(lane32 == DN_V_HEADS + h, dg[i], 0.0)
                for i, j in enumerate(range(first // 2, (first + SCAN_BWD_GROUP) // 2)):
                    dq_ref[crows, 128 * j:128 * (j + 1)] = dq[i]
                    dk_ref[crows, 128 * j:128 * (j + 1)] = dk[i]
            for i, h in enumerate(heads):
                ds_ref[h] = ds[i]
        for c in range(chunks):
            dbg_ref[DN_CHUNK * c:DN_CHUNK * (c + 1), :] = dbg[c]

    return pl.pallas_call(
        body, grid=(ns,),
        in_specs=[pl.BlockSpec((rows, 2048), lambda i: (rev(i), 0)),
                  pl.BlockSpec((rows, 1024), lambda i: (rev(i), 0)),
                  pl.BlockSpec((rows, 1024), lambda i: (rev(i), 0)),
                  pl.BlockSpec((rows, 2048), lambda i: (rev(i), 0)),
                  pl.BlockSpec((rows, 32), lambda i: (rev(i), 0)),
                  pl.BlockSpec((chunks, DN_V_HEADS, DN_HD, DN_HD), lambda i: (rev(i), 0, 0, 0)),
                  pl.BlockSpec((chunks, DN_V_HEADS, DN_CHUNK, DN_CHUNK), lambda i: (rev(i), 0, 0, 0)),
                  pl.BlockSpec((chunks, DN_V_HEADS, DN_CHUNK, 2 * DN_HD), lambda i: (rev(i), 0, 0, 0)),
                  pl.BlockSpec((chunks, DN_V_HEADS, DN_CHUNK, DN_HD), lambda i: (rev(i), 0, 0, 0)),
                  pl.BlockSpec((chunks, DN_K_HEADS, 2 * DN_CHUNK, DN_CHUNK), lambda i: (rev(i), 0, 0, 0))],
        out_specs=[pl.BlockSpec((rows, 1024), lambda i: (rev(i), 0)),
                   pl.BlockSpec((rows, 1024), lambda i: (rev(i), 0)),
                   pl.BlockSpec((rows, 2048), lambda i: (rev(i), 0)),
                   pl.BlockSpec((rows, 32), lambda i: (rev(i), 0))],
        out_shape=[jax.ShapeDtypeStruct((T, 1024), F32), jax.ShapeDtypeStruct((T, 1024), F32),
                   jax.ShapeDtypeStruct((T, 2048), F32), jax.ShapeDtypeStruct((T, 32), F32)],
        scratch_shapes=[pltpu.VMEM((DN_V_HEADS, DN_HD, DN_HD), F32)],
        name="dn_scan_bwd", compiler_params=_cparams(),
    )(do, qn, kn, v, bg, ssave, *saved)


def _dn_conv_bwd(dqn, dkn, dv, dbg, qkv, conv_out, ba, conv_w, a_log, dt_bias):
    T = qkv.shape[0]
    rb = ROW_BLOCK // 2
    nr = T // rb

    def body(dq_ref, dk_ref, dv_ref, dbg_ref, x_ref, c_ref, ba_ref, cw_ref, al_ref, dtb_ref,
             dx_ref, dba_ref, dcw_ref, dal_ref, ddtb_ref, carry_ref):
        step = pl.program_id(0)
        i = nr - 1 - step

        @pl.when(step == 0)
        def _():
            carry_ref[...] = jnp.zeros_like(carry_ref)
            dcw_ref[...] = jnp.zeros_like(dcw_ref)
            dal_ref[...] = jnp.zeros_like(dal_ref)
            ddtb_ref[...] = jnp.zeros_like(ddtb_ref)
        for t in range(DN_CONV_W // 128):
            cols = slice(128 * t, 128 * (t + 1))
            w, x = cw_ref[:, cols], x_ref[:, cols]
            if t < DN_K_HEADS:
                dout = dq_ref[:, cols]
            elif t < 2 * DN_K_HEADS:
                dout = dk_ref[:, 128 * (t - 8):128 * (t - 7)]
            else:
                dout = dv_ref[:, 128 * (t - 16):128 * (t - 15)]
            _, vjp = jax.vjp(functools.partial(_dn_post_tile, t=t), c_ref[:, cols].astype(F32))
            (dc,) = vjp(dout)
            nxt = carry_ref[:, cols]
            dx = w[3:4, :] * dc
            dcw_ref[3:4, cols] += jnp.sum(dc * x, axis=0, keepdims=True)
            for s in range(1, DN_CONV_K):
                up = _shift_up(dc, nxt, s)
                dx = dx + w[3 - s:4 - s, :] * up
                dcw_ref[3 - s:4 - s, cols] += jnp.sum(up * x, axis=0, keepdims=True)
            dx_ref[:, cols] = dx
            carry_ref[:, cols] = dc[0:8, :]
        fn = functools.partial(_dn_beta_g, live=_live_rows(i, rb))
        _, vjp = jax.vjp(fn, ba_ref[...], al_ref[...], dtb_ref[...])
        dba, dal, ddtb = vjp((dbg_ref[:, 0:DN_V_HEADS], dbg_ref[:, DN_V_HEADS:]))
        dba_ref[...] = dba
        dal_ref[...] += dal
        ddtb_ref[...] += ddtb

    return _row_call("dn_conv_bwd", body, T, rb, [dqn, dkn, dv, dbg, qkv, conv_out, ba], [conv_w, a_log, dt_bias],
                     [(4096, F32), (32, F32)], [((DN_CONV_K, 4096), F32), ((1, DN_V_HEADS), F32), ((1, DN_V_HEADS), F32)],
                     reverse=True, scratch=[pltpu.VMEM((8, 4096), F32)])


def _dn_in_bwd(dqkv, dz, dba, h1, dy, norm_w, w_in):
    T = h1.shape[0]

    def body(dqkv_ref, dz_ref, dba_ref, h_ref, dy_ref, nw_ref, w_ref, dh_ref, dnw_ref):
        @pl.when(pl.program_id(0) == 0)
        def _():
            dnw_ref[...] = jnp.zeros_like(dnw_ref)
        dxn = (_bdot(dqkv_ref[...], w_ref[:, 0:4096], NT) + _bdot(dz_ref[...], w_ref[:, 4096:6144], NT)
               + _bdot(dba_ref[...], w_ref[:, 6144:6176], NT))
        _, vjp = jax.vjp(_rms, h_ref[...], nw_ref[...])
        dh, dnw = vjp(dxn)
        dh_ref[...] = (dy_ref[...] + dh) * _live_rows(pl.program_id(0), ROW_BLOCK)
        dnw_ref[...] += dnw

    return _row_call("dn_in_bwd", body, T, ROW_BLOCK, [dqkv, dz, dba, h1, dy], [norm_w, w_in],
                     [(1024, F32)], [((1, 1024), F32)])


def _exchange(parts, scatter, name):
    n = len(parts)
    out_shape = [jax.ShapeDtypeStruct(p.shape if sc else (N_DEV,) + p.shape, p.dtype) for p, sc in zip(parts, scatter)]

    def body(*refs):
        ins, outs = refs[:n], refs[n:2 * n]
        send_sems, recv_sems, local_sems = refs[2 * n:]
        x, y, c = lax.axis_index("x"), lax.axis_index("y"), lax.axis_index("c")
        me = 4 * x + 2 * y + c
        peers = []
        for k in range(1, N_DEV):
            px = 1 - x if k & 4 else x
            py = 1 - y if k & 2 else y
            pc = 1 - c if k & 1 else c
            peers.append(((px, py, pc), 4 * px + 2 * py + pc))

        def src(a, idx):
            return ins[a].at[idx] if scatter[a] else ins[a]

        local = [pltpu.make_async_copy(src(a, me), outs[a].at[me], local_sems.at[a]) for a in range(n)]
        for cp in local:
            cp.start()
        for a in range(n):
            for k, (dev, idx) in enumerate(peers):
                pltpu.make_async_remote_copy(
                    src_ref=src(a, idx), dst_ref=outs[a].at[me], send_sem=send_sems.at[a, k], recv_sem=recv_sems.at[a, k],
                    device_id=dev, device_id_type=pl.DeviceIdType.MESH).start()
        for a in range(n):
            for k, (dev, idx) in enumerate(peers):
                pltpu.make_async_remote_copy(
                    src_ref=src(a, idx), dst_ref=outs[a].at[idx], send_sem=send_sems.at[a, k], recv_sem=recv_sems.at[a, k],
                    device_id=dev, device_id_type=pl.DeviceIdType.MESH).wait()
        for cp in local:
            cp.wait()

    hbm = pl.BlockSpec(memory_space=pltpu.HBM)
    return pl.pallas_call(
        body, out_shape=out_shape, in_specs=[hbm] * n, out_specs=[hbm] * n,
        scratch_shapes=[pltpu.SemaphoreType.DMA((n, N_DEV - 1)), pltpu.SemaphoreType.DMA((n, N_DEV - 1)),
                        pltpu.SemaphoreType.DMA((n,))],
        name=name,
    )(*parts)


def _gather_two_level(parts, name):
    n = len(parts)
    out_shape = [jax.ShapeDtypeStruct((N_DEV,) + p.shape, p.dtype) for p in parts]

    def body(*refs):
        ins, outs = refs[:n], refs[n:2 * n]
        send_sems, recv_sems, local_sems = refs[2 * n:]
        x, y, c = lax.axis_index("x"), lax.axis_index("y"), lax.axis_index("c")
        idx = lambda px, py, pc: 4 * px + 2 * py + pc
        me, sibling = (x, y, c), (x, y, 1 - c)
        chips = [(1 - x, y), (x, 1 - y), (1 - x, 1 - y)]

        def copy(a, k, block, to, src=None):
            slot = outs[a].at[idx(*block)]
            return pltpu.make_async_remote_copy(
                src_ref=slot if src is None else src, dst_ref=slot, send_sem=send_sems.at[a, k], recv_sem=recv_sems.at[a, k],
                device_id=to, device_id_type=pl.DeviceIdType.MESH)

        local = [pltpu.make_async_copy(ins[a], outs[a].at[idx(*me)], local_sems.at[a]) for a in range(n)]
        for cp in local:
            cp.start()
        sent = []
        for a in range(n):
            sent.append(copy(a, 0, me, sibling, src=ins[a]))
            sent += [copy(a, 1 + j, me, (*chip, c), src=ins[a]) for j, chip in enumerate(chips)]
        for cp in sent:
            cp.start()
        for a in range(n):
            for j, chip in enumerate(chips):
                copy(a, 1 + j, (*chip, c), me).wait_recv()
                passed = copy(a, 4 + j, (*chip, c), sibling)
                passed.start()
                sent.append(passed)
        for a in range(n):
            copy(a, 0, sibling, me).wait_recv()
            for j, chip in enumerate(chips):
                copy(a, 4 + j, (*chip, 1 - c), me).wait_recv()
        for cp in sent:
            cp.wait_send()
        for cp in local:
            cp.wait()

    hbm = pl.BlockSpec(memory_space=pltpu.HBM)
    return pl.pallas_call(
        body, out_shape=out_shape, in_specs=[hbm] * n, out_specs=[hbm] * n,
        scratch_shapes=[pltpu.SemaphoreType.DMA((n, N_DEV - 1)), pltpu.SemaphoreType.DMA((n, N_DEV - 1)),
                        pltpu.SemaphoreType.DMA((n,))],
        name=name,
    )(*parts)


def _swap_with_sibling(parts, name):
    n = len(parts)

    def body(*refs):
        ins, outs = refs[:n], refs[n:2 * n]
        send_sems, recv_sems = refs[2 * n:]
        x, y, c = lax.axis_index("x"), lax.axis_index("y"), lax.axis_index("c")
        copies = [pltpu.make_async_remote_copy(
            src_ref=ins[a].at[1 - c], dst_ref=outs[a], send_sem=send_sems.at[a], recv_sem=recv_sems.at[a],
            device_id=(x, y, 1 - c), device_id_type=pl.DeviceIdType.MESH) for a in range(n)]
        for cp in copies:
            cp.start()
        for cp in copies:
            cp.wait()

    hbm = pl.BlockSpec(memory_space=pltpu.HBM)
    return pl.pallas_call(
        body, out_shape=[jax.ShapeDtypeStruct(p.shape[1:], p.dtype) for p in parts], in_specs=[hbm] * n, out_specs=[hbm] * n,
        scratch_shapes=[pltpu.SemaphoreType.DMA((n,)), pltpu.SemaphoreType.DMA((n,))],
        name=name,
    )(*parts)


def _pair_sum(a, b, name):
    R, C = a.shape
    rb = _adam_rows(R)

    def body(a_ref, b_ref, o_ref):
        o_ref[...] = (a_ref[...].astype(F32) + b_ref[...].astype(F32)).astype(BF16)

    blk = pl.BlockSpec((rb, C), lambda i: (i, 0))
    return pl.pallas_call(body, grid=(R // rb,), in_specs=[blk, blk], out_specs=blk,
                          out_shape=jax.ShapeDtypeStruct((R, C), BF16), name=name, compiler_params=_cparams())(a, b)


def _exchange_chips(parts, name):
    n = len(parts)
    n_chips = N_DEV // 2

    def body(*refs):
        ins, outs = refs[:n], refs[n:2 * n]
        send_sems, recv_sems, local_sems = refs[2 * n:]
        x, y, c = lax.axis_index("x"), lax.axis_index("y"), lax.axis_index("c")
        mine = 2 * x + y
        chips = [(1 - x, y), (x, 1 - y), (1 - x, 1 - y)]
        local = [pltpu.make_async_copy(ins[a].at[mine], outs[a].at[mine], local_sems.at[a]) for a in range(n)]
        for cp in local:
            cp.start()
        for a in range(n):
            for k, (px, py) in enumerate(chips):
                pltpu.make_async_remote_copy(
                    src_ref=ins[a].at[2 * px + py], dst_ref=outs[a].at[mine], send_sem=send_sems.at[a, k],
                    recv_sem=recv_sems.at[a, k], device_id=(px, py, c), device_id_type=pl.DeviceIdType.MESH).start()
        for a in range(n):
            for k, (px, py) in enumerate(chips):
                pltpu.make_async_remote_copy(
                    src_ref=ins[a].at[2 * px + py], dst_ref=outs[a].at[2 * px + py], send_sem=send_sems.at[a, k],
                    recv_sem=recv_sems.at[a, k], device_id=(px, py, c), device_id_type=pl.DeviceIdType.MESH).wait()
        for cp in local:
            cp.wait()

    hbm = pl.BlockSpec(memory_space=pltpu.HBM)
    return pl.pallas_call(
        body, out_shape=[jax.ShapeDtypeStruct(p.shape, p.dtype) for p in parts], in_specs=[hbm] * n, out_specs=[hbm] * n,
        scratch_shapes=[pltpu.SemaphoreType.DMA((n, n_chips - 1)), pltpu.SemaphoreType.DMA((n, n_chips - 1)),
                        pltpu.SemaphoreType.DMA((n,))],
        name=name,
    )(*parts)


def _adam_rows(rows):
    for rb in (128, 64, 40, 16, 8):
        if rows % rb == 0:
            return rb
    return rows


def _adamw(stack, w, m, v, name):
    R, C = w.shape
    rb = _adam_rows(R)
    slots = stack.shape[0]

    def body(s_ref, w_ref, m_ref, v_ref, g_ref, d_ref, nm_ref, nv_ref):
        g = s_ref[0].astype(F32)
        for s in range(1, slots):
            g = g + s_ref[s].astype(F32)
        nm = ADAM_B1 * m_ref[...] + (1.0 - ADAM_B1) * g
        nv = ADAM_B2 * v_ref[...] + (1.0 - ADAM_B2) * (g * g)
        m_hat = nm / (1.0 - ADAM_B1 ** ADAM_STEP)
        v_hat = nv / (1.0 - ADAM_B2 ** ADAM_STEP)
        g_ref[...] = g
        d_ref[...] = -ADAM_LR * (m_hat / (jnp.sqrt(v_hat) + ADAM_EPS) + ADAM_WD * w_ref[...])
        nm_ref[...] = nm
        nv_ref[...] = nv

    blk = pl.BlockSpec((rb, C), lambda i: (i, 0))
    return pl.pallas_call(
        body, grid=(R // rb,),
        in_specs=[pl.BlockSpec((slots, rb, C), lambda i: (0, i, 0)), blk, blk, blk],
        out_specs=[blk] * 4, out_shape=[jax.ShapeDtypeStruct((R, C), F32)] * 4,
        name=name, compiler_params=_cparams(),
    )(stack, w, m, v)


def _pad_rows8(a):
    return jnp.concatenate([a, jnp.zeros((8 - a.shape[0], a.shape[1]), a.dtype)], axis=0) if a.shape[0] < 8 else a


def _pack_small(norm_w, qnw, knw, sinks, a_log, dt_bias, onw, extra):
    z = lambda n: jnp.zeros((1, n), F32)
    row = jnp.concatenate([norm_w, qnw, knw, sinks, a_log, dt_bias, z(80), onw, extra, z(512)], axis=1)
    return row.reshape(16, 128)


def _unpack_small(p):
    row = p.reshape(1, 2048)
    cut = lambda a, n: row[:, a:a + n]
    return (cut(0, 1024), cut(1024, 64), cut(1088, 64), cut(1152, 16), cut(1168, 16), cut(1184, 16), cut(1280, 128),
            cut(1408, 128))


def _pack_rows(w_in_a, w_in_d, w_out_a, w_out_d, meta, conv, dn_norm):
    a = jnp.concatenate([w_in_a, w_in_d], axis=1)
    b = jnp.concatenate([w_out_a, w_out_d], axis=0)
    c = jnp.concatenate([meta, conv.reshape(16, 128), _pad_rows8(dn_norm)], axis=0)
    return a, b, c


def _unpack_rows(a, b, c):
    return (a[:, :288], a[:, 288:], b[:128], b[128:], c[:16], c[16:32].reshape(4, 512), c[32:33])


def _local_step(x, front, target, w):
    xn0, q, kv, gate = _attn_in_fwd(x, front, w["attn_norm_w"], w["attn_w_in"])
    o = _attn_core_fwd(q, kv, w["attn_sinks"], w["attn_q_norm_w"], w["attn_k_norm_w"])
    h1 = _attn_out_fwd(o, gate, x, front, w["attn_w_out"])
    xn1, qkv, z, ba = _dn_in_fwd(h1, w["dn_norm_w"], w["dn_w_in"])
    qn, kn, v, bg, conv_out = _dn_conv_fwd(qkv, ba, w["dn_conv_w"], w["dn_a_log"], w["dn_dt_bias"])
    o_dn, ssave, *saved = _dn_scan_fwd(qn, kn, v, bg)
    dy, og_dn, loss = _dn_out_fwd(o_dn, z, h1, target, w["dn_w_out"], w["dn_o_norm_w"])

    g = {}
    do_dn, dz, g["dn_o_norm_w"] = _dn_out_bwd(dy, o_dn, z, w["dn_w_out"], w["dn_o_norm_w"])
    g["dn_w_out"] = _wgrad(og_dn, dy, 1024, "wgrad_dn_out")
    dqn, dkn, dv, dbg = _dn_scan_bwd(do_dn, qn, kn, v, bg, ssave, saved)
    dqkv, dba, g["dn_conv_w"], g["dn_a_log"], g["dn_dt_bias"] = _dn_conv_bwd(
        dqn, dkn, dv, dbg, qkv, conv_out, ba, w["dn_conv_w"], w["dn_a_log"], w["dn_dt_bias"])
    dh1, g["dn_norm_w"] = _dn_in_bwd(dqkv, dz, dba, h1, dy, w["dn_norm_w"], w["dn_w_in"])
    g["dn_w_in"] = jnp.concatenate([_wgrad(xn1, dqkv, 1024, "wgrad_dn_qkv"), _wgrad(xn1, dz, 1024, "wgrad_dn_z"),
                                    _wgrad(xn1, dba, 32, "wgrad_dn_ba")], axis=1)
    do, dgate, g["attn_w_out"] = _attn_out_bwd(dh1, o, gate, w["attn_w_out"])
    dq, dkv, g["attn_sinks"], g["attn_q_norm_w"], g["attn_k_norm_w"] = _attn_core_bwd(
        do, q, kv, w["attn_sinks"], w["attn_q_norm_w"], w["attn_k_norm_w"])
    grad_x, dfront, g["attn_norm_w"] = _attn_in_bwd(dq, dkv, dgate, x, front, dh1, w["attn_norm_w"], w["attn_w_in"])
    g["meta_tokens"] = dfront[FRONT_PAD:]
    g["attn_w_in"] = jnp.concatenate([_wgrad(xn0, dq, 1024, "wgrad_attn_q"), _wgrad(xn0, dkv, 256, "wgrad_attn_kv"),
                                      _wgrad(xn0, dgate, 1024, "wgrad_attn_gate")], axis=1)
    return loss, grad_x, g


WEIGHTS = ['meta_tokens', 'attn_norm_w', 'attn_w_in', 'attn_q_norm_w', 'attn_k_norm_w', 'attn_sinks', 'attn_w_out',
           'dn_norm_w', 'dn_w_in', 'dn_conv_w', 'dn_a_log', 'dn_dt_bias', 'dn_o_norm_w', 'dn_w_out']
SMALL = ['attn_norm_w', 'attn_q_norm_w', 'attn_k_norm_w', 'attn_sinks', 'dn_a_log', 'dn_dt_bias', 'dn_o_norm_w']


def kernel(x, meta_tokens, attn_norm_w, attn_w_in, attn_q_norm_w, attn_k_norm_w, attn_sinks, attn_w_out, dn_norm_w, dn_w_in, dn_conv_w, dn_a_log, dn_dt_bias, dn_o_norm_w, dn_w_out, loss_target, m_meta_tokens, m_attn_norm_w, m_attn_w_in, m_attn_q_norm_w, m_attn_k_norm_w, m_attn_sinks, m_attn_w_out, m_dn_norm_w, m_dn_w_in, m_dn_conv_w, m_dn_a_log, m_dn_dt_bias, m_dn_o_norm_w, m_dn_w_out, v_meta_tokens, v_attn_norm_w, v_attn_w_in, v_attn_q_norm_w, v_attn_k_norm_w, v_attn_sinks, v_attn_w_out, v_dn_norm_w, v_dn_w_in, v_dn_conv_w, v_dn_a_log, v_dn_dt_bias, v_dn_o_norm_w, v_dn_w_out):
    shard = dict(meta_tokens=meta_tokens, attn_norm_w=attn_norm_w, attn_w_in=attn_w_in[0], attn_q_norm_w=attn_q_norm_w,
                 attn_k_norm_w=attn_k_norm_w, attn_sinks=attn_sinks, attn_w_out=attn_w_out[0], dn_norm_w=dn_norm_w,
                 dn_w_in=dn_w_in[0], dn_conv_w=dn_conv_w[0], dn_a_log=dn_a_log, dn_dt_bias=dn_dt_bias,
                 dn_o_norm_w=dn_o_norm_w, dn_w_out=dn_w_out[0])
    mom_m = dict(meta_tokens=m_meta_tokens, attn_norm_w=m_attn_norm_w, attn_w_in=m_attn_w_in[0], attn_q_norm_w=m_attn_q_norm_w,
                 attn_k_norm_w=m_attn_k_norm_w, attn_sinks=m_attn_sinks, attn_w_out=m_attn_w_out[0], dn_norm_w=m_dn_norm_w,
                 dn_w_in=m_dn_w_in[0], dn_conv_w=m_dn_conv_w[0], dn_a_log=m_dn_a_log, dn_dt_bias=m_dn_dt_bias,
                 dn_o_norm_w=m_dn_o_norm_w, dn_w_out=m_dn_w_out[0])
    mom_v = dict(meta_tokens=v_meta_tokens, attn_norm_w=v_attn_norm_w, attn_w_in=v_attn_w_in[0], attn_q_norm_w=v_attn_q_norm_w,
                 attn_k_norm_w=v_attn_k_norm_w, attn_sinks=v_attn_sinks, attn_w_out=v_attn_w_out[0], dn_norm_w=v_dn_norm_w,
                 dn_w_in=v_dn_w_in[0], dn_conv_w=v_dn_conv_w[0], dn_a_log=v_dn_a_log, dn_dt_bias=v_dn_dt_bias,
                 dn_o_norm_w=v_dn_o_norm_w, dn_w_out=v_dn_w_out[0])

    def rows_of(d):
        return _pack_rows(d["attn_w_in"], d["dn_w_in"], d["attn_w_out"], d["dn_w_out"], d["meta_tokens"], d["dn_conv_w"],
                          d["dn_norm_w"])

    def small_of(d, extra):
        return _pack_small(*[d[k] for k in SMALL], extra)

    wa, wb, wc = rows_of(shard)
    ga, gb, gc = _gather_two_level([wa.astype(BF16), wb.astype(BF16), wc], "gather_weights")
    full = {k: shard[k] for k in SMALL}
    full["attn_w_in"] = ga[:, :, :288].transpose(1, 0, 2).reshape(1024, 2304)
    full["dn_w_in"] = ga[:, :, 288:].transpose(1, 0, 2).reshape(1024, 6176)
    full["attn_w_out"] = gb[:, :128].reshape(1024, 1024)
    full["dn_w_out"] = gb[:, 128:].reshape(2048, 1024)
    meta_full = gc[:, :16].transpose(1, 0, 2).reshape(N_META, 1024)
    full["dn_conv_w"] = gc[:, 16:32].reshape(N_DEV, 4, 512).transpose(1, 0, 2).reshape(4, 4096)
    full["dn_norm_w"] = gc[:, 32].reshape(1, 1024)

    front = jnp.concatenate([jnp.zeros((FRONT_PAD, D_MODEL), F32), meta_full], axis=0)
    loss, grad_x, g = _local_step(x[0], front, loss_target[0], full)
    grad_x = grad_x[None]

    pa = jnp.concatenate([g["attn_w_in"].reshape(1024, N_DEV, 288), g["dn_w_in"].reshape(1024, N_DEV, 772)],
                         axis=2).transpose(1, 0, 2)
    pb = jnp.concatenate([g["attn_w_out"].reshape(N_DEV, 128, 1024), g["dn_w_out"].reshape(N_DEV, 256, 1024)], axis=1)
    dn_norm8 = jnp.concatenate([g["dn_norm_w"].reshape(N_DEV, 1, 128), jnp.zeros((N_DEV, 7, 128), F32)], axis=1)
    pc = jnp.concatenate([g["meta_tokens"].reshape(N_META, N_DEV, 128).transpose(1, 0, 2),
                          g["dn_conv_w"].reshape(4, N_DEV, 512).transpose(1, 0, 2).reshape(N_DEV, 16, 128), dn_norm8], axis=1)
    ps = small_of(g, loss)
    c = lax.axis_index("c")
    by_core = lambda p: p.astype(BF16).reshape((N_DEV // 2, 2) + p.shape[1:]).swapaxes(0, 1)
    pa2, pb2 = by_core(pa), by_core(pb)
    ra, rb_ = _swap_with_sibling([pa2, pb2], "swap_grads")
    own = lambda p2: lax.dynamic_index_in_dim(p2, c, axis=0, keepdims=False)
    flat = lambda t: t.reshape((-1,) + t.shape[2:])
    sa = _pair_sum(flat(own(pa2)), flat(ra), "pair_sum_a").reshape(ra.shape)
    sb = _pair_sum(flat(own(pb2)), flat(rb_), "pair_sum_b").reshape(rb_.shape)
    xa, xb = _exchange_chips([sa, sb], "exchange_grads")
    xc, xs = _exchange([pc, ps], [True, False], "exchange_small")

    out = {}
    ma, mb, mc = rows_of(mom_m)
    va, vb, vc = rows_of(mom_v)
    ra = _adamw(xa, wa, ma, va, "adamw_a")
    rb = _adamw(xb, wb, mb, vb, "adamw_b")
    rc = _adamw(xc, wc, mc, vc, "adamw_c")
    zero = jnp.zeros((1, 128), F32)
    rs = _adamw(xs, small_of(shard, zero), small_of(mom_m, zero), small_of(mom_v, zero), "adamw_small")
    row_names = ["attn_w_in", "dn_w_in", "attn_w_out", "dn_w_out", "meta_tokens", "dn_conv_w", "dn_norm_w"]
    lead = {"attn_w_in", "dn_w_in", "attn_w_out", "dn_w_out", "dn_conv_w"}
    for kind in range(4):
        vals = dict(zip(row_names, _unpack_rows(ra[kind], rb[kind], rc[kind])))
        small = _unpack_small(rs[kind])
        vals.update(dict(zip(SMALL, small[:7])))
        if kind == 0:
            loss_total = small[7][0, 0]
        out[kind] = [vals[k][None] if k in lead else vals[k] for k in WEIGHTS]
    return (loss_total, grad_x, *out[0], *out[1], *out[2], *out[3])
```

```python
import functools
import math

import jax
import jax.numpy as jnp
from jax import lax
from jax.experimental import pallas as pl
from jax.experimental.pallas import tpu as pltpu

F32, BF16 = jnp.float32, jnp.bfloat16

D_MODEL = 1024
N_META = 16
NORM_EPS = 1e-6
ATTN_HEADS, ATTN_KV_HEADS, ATTN_GROUPS, ATTN_HD = 16, 2, 8, 64
ATTN_BLOCK = 128
FRONT_PAD = ATTN_BLOCK - N_META
DN_HD, DN_K_HEADS, DN_V_HEADS = 128, 8, 16
DN_CHUNK = 128
TRI_BLOCK = 64
SCAN_CHUNKS = 3
SCAN_BWD_CHUNKS = 1
SCAN_FWD_GROUP = 16
SCAN_BWD_GROUP = 8
DN_KEY_W, DN_VAL_W = 1024, 2048
DN_CONV_W = 2 * DN_KEY_W + DN_VAL_W
DN_CONV_K = 4
N_DEV = 8
ROW_BLOCK = 384
WGRAD_ROWS = 1376
VMEM_LIMIT = 56 * 1024 * 1024
NEG = -1e30

ADAM_LR, ADAM_B1, ADAM_B2, ADAM_EPS, ADAM_WD, ADAM_STEP = 0.001, 0.9, 0.999, 1e-08, 0.01, 10

NT = (((1,), (1,)), ((), ()))
TN = (((0,), (0,)), ((), ()))


def _cparams(sem=("arbitrary",)):
    return pltpu.CompilerParams(dimension_semantics=sem, vmem_limit_bytes=VMEM_LIMIT)


def _rms(x, w):
    return x * lax.rsqrt(jnp.mean(x * x, axis=-1, keepdims=True) + NORM_EPS) * w


def _silu(x):
    return x * jax.nn.sigmoid(x)


def _softplus(x):
    return jnp.maximum(x, 0.0) + jnp.log(1.0 + jnp.exp(-jnp.abs(x)))


NN = (((1,), (0,)), ((), ()))


def _mm(a, b, dims):
    return lax.dot_general(a.astype(BF16), b.astype(BF16), dims, preferred_element_type=F32)


@functools.partial(jax.custom_vjp, nondiff_argnums=(2,))
def _bdot_vjp(a, b, dims):
    return _mm(a, b, dims)


def _bdot_fwd(a, b, dims):
    a16, b16 = a.astype(BF16), b.astype(BF16)
    return _mm(a16, b16, dims), (a16, b16, jnp.zeros((), a.dtype), jnp.zeros((), b.dtype))


def _bdot_bwd(dims, res, g):
    a16, b16, ta, tb = res
    g16 = g.astype(BF16)
    if dims == NN:
        da, db = _mm(g16, b16, NT), _mm(a16, g16, TN)
    elif dims == NT:
        da, db = _mm(g16, b16, NN), _mm(g16, a16, TN)
    else:
        da, db = _mm(b16, g16, NT), _mm(a16, g16, NN)
    return da.astype(ta.dtype), db.astype(tb.dtype)


_bdot_vjp.defvjp(_bdot_fwd, _bdot_bwd)


def _bdot(a, b, dims=NN):
    return _bdot_vjp(a, b, dims)


def _row_call(name, body, n_rows, rb, rows, consts, outs, accs=(), reverse=False, scratch=(), halos=()):
    n = n_rows // rb
    assert n * rb == n_rows
    idx = (lambda i: (n - 1 - i, 0)) if reverse else (lambda i: (i, 0))
    in_specs = [pl.BlockSpec((rb, a.shape[1]), idx) for a in rows]
    in_specs += [pl.BlockSpec((hr, a.shape[1]), fn) for a, hr, fn in halos]
    in_specs += [pl.BlockSpec(c.shape, functools.partial(lambda i, nd: (0,) * nd, nd=c.ndim)) for c in consts]
    out_specs = [pl.BlockSpec((rb, c), idx) for c, _ in outs]
    out_specs += [pl.BlockSpec(s, functools.partial(lambda i, nd: (0,) * nd, nd=len(s))) for s, _ in accs]
    out_shape = [jax.ShapeDtypeStruct((n_rows, c), dt) for c, dt in outs]
    out_shape += [jax.ShapeDtypeStruct(s, dt) for s, dt in accs]
    return pl.pallas_call(
        body, grid=(n,), in_specs=in_specs, out_specs=out_specs, out_shape=out_shape,
        scratch_shapes=list(scratch), name=name, compiler_params=_cparams(),
    )(*rows, *[a for a, _, _ in halos], *consts)


def _token_views(x):
    per = ROW_BLOCK // ATTN_BLOCK
    return [(x, ATTN_BLOCK, functools.partial(lambda i, k: (jnp.maximum(per * i - 1 + k, 0), 0), k=k)) for k in range(per)]


def _padded_block(i, front, views):
    first = jnp.where(i == 0, front, views[0][...]) if front is not None else views[0][...]
    return jnp.concatenate([first] + [v[...] for v in views[1:]], axis=0)


def _attn_in_fwd(x, front, norm_w, w_in):
    T = x.shape[0] + ATTN_BLOCK

    def body(xa_ref, xb_ref, xc_ref, front_ref, nw_ref, w_ref, xn_ref, q_ref, kv_ref, gate_ref):
        h = _padded_block(pl.program_id(0), front_ref[...], (xa_ref, xb_ref, xc_ref))
        xn = _rms(h, nw_ref[...]).astype(BF16)
        xn_ref[...] = xn
        q_ref[...] = jnp.dot(xn, w_ref[:, 0:1024], preferred_element_type=F32)
        kv_ref[...] = jnp.dot(xn, w_ref[:, 1024:1280], preferred_element_type=F32)
        gate_ref[...] = jnp.dot(xn, w_ref[:, 1280:2304], preferred_element_type=F32)

    return _row_call("attn_in_fwd", body, T, ROW_BLOCK, [], [front, norm_w, w_in],
                     [(1024, BF16), (1024, F32), (256, F32), (1024, F32)], halos=_token_views(x))


def _attn_bias(n, j):
    C, R = 2 * ATTN_BLOCK + N_META, ATTN_GROUPS * ATTN_BLOCK
    c = lax.broadcasted_iota(jnp.int32, (C, R), 0)
    r = lax.broadcasted_iota(jnp.int32, (C, R), 1)
    ql = r & (ATTN_BLOCK - 1)
    is_meta = c >= 2 * ATTN_BLOCK
    dist_band = ATTN_BLOCK + ql - c
    cmin = jnp.maximum(0, 2 * ATTN_BLOCK - ATTN_BLOCK * n)
    valid_band = (c >= cmin) & (dist_band >= 0) & (dist_band < ATTN_BLOCK)
    dist_meta = ATTN_BLOCK * n + ql - FRONT_PAD - (c - 2 * ATTN_BLOCK)
    valid = (is_meta & (dist_meta >= 0)) | (jnp.logical_not(is_meta) & valid_band)
    dist = jnp.minimum(jnp.where(is_meta, dist_meta, dist_band), ATTN_BLOCK).astype(F32)
    rr = lax.broadcasted_iota(jnp.int32, (1, R), 1)
    head = (rr >> 7).astype(F32) + float(ATTN_GROUPS * j + 1)
    slope = jnp.exp(head * (-0.5 * math.log(2.0)))
    return jnp.where(valid, slope * dist, -NEG)


def _attn_tables(n, refresh, bias_ref):
    @pl.when(refresh)
    def _():
        for j in range(ATTN_KV_HEADS):
            bias_ref[j] = _attn_bias(n, j)


def _attn_table_scratch():
    return [pltpu.VMEM((ATTN_KV_HEADS, 2 * ATTN_BLOCK + N_META, ATTN_GROUPS * ATTN_BLOCK), F32)]


def _attn_groups(q_t, k, v, sinkrow, qnw_col, knw, bias, late_norm=True):
    n = range(len(q_t))
    qn = [q_t[j] * (lax.rsqrt(jnp.mean(q_t[j] * q_t[j], axis=0, keepdims=True) + NORM_EPS) * (ATTN_HD ** -0.5)) * qnw_col
          for j in n]
    kn = [_rms(k[j], knw) for j in n]
    s = [_bdot(kn[j], qn[j]) - bias[j] for j in n]
    m = [lax.stop_gradient(jnp.maximum(jnp.max(s[j], axis=0, keepdims=True), sinkrow[j])) for j in n]
    e = [jnp.exp(s[j] - m[j]) for j in n]
    inv = [1.0 / (jnp.sum(e[j], axis=0, keepdims=True) + jnp.exp(sinkrow[j] - m[j])) for j in n]
    if late_norm:
        return [_bdot(v[j], e[j], TN) * inv[j] for j in n]
    return [_bdot(v[j], e[j] * inv[j], TN) for j in n]


def _sink_row(sinks_ref, j):
    rr = lax.broadcasted_iota(jnp.int32, (1, ATTN_GROUPS * ATTN_BLOCK), 1) >> 7
    row = jnp.zeros((1, ATTN_GROUPS * ATTN_BLOCK), F32)
    for hl in range(ATTN_GROUPS):
        row = jnp.where(rr == hl, sinks_ref[0, ATTN_GROUPS * j + hl], row)
    return row


def _heads_to_lanes(ref, j):
    return jnp.concatenate([ref[:, ATTN_HD * h:ATTN_HD * (h + 1)].T
                            for h in range(ATTN_GROUPS * j, ATTN_GROUPS * (j + 1))], axis=1)


def _lanes_to_heads(ref, j, x_t):
    for hl in range(ATTN_GROUPS):
        h = ATTN_GROUPS * j + hl
        ref[:, ATTN_HD * h:ATTN_HD * (h + 1)] = x_t[:, ATTN_BLOCK * hl:ATTN_BLOCK * (hl + 1)].T


def _attn_kv_tiles(kvp_ref, kvc_ref, kvm_ref, j):
    ksl = slice(ATTN_HD * j, ATTN_HD * (j + 1))
    vsl = slice(128 + ATTN_HD * j, 128 + ATTN_HD * (j + 1))
    k = jnp.concatenate([kvp_ref[:, ksl], kvc_ref[:, ksl], kvm_ref[FRONT_PAD:, ksl]], axis=0)
    v = jnp.concatenate([kvp_ref[:, vsl], kvc_ref[:, vsl], kvm_ref[FRONT_PAD:, vsl]], axis=0)
    return k, v


def _attn_core_fwd(q, kv, sinks, qnw, knw):
    T = q.shape[0]
    nb = T // ATTN_BLOCK

    def body(sinks_ref, q_ref, kvc_ref, kvp_ref, kvm_ref, qnw_ref, knw_ref, o_ref, bias_ref):
        n = pl.program_id(0)
        _attn_tables(n, n <= 2, bias_ref)
        kvh = range(ATTN_KV_HEADS)
        kv_tiles = [_attn_kv_tiles(kvp_ref, kvc_ref, kvm_ref, j) for j in kvh]
        o_t = _attn_groups([_heads_to_lanes(q_ref, j) for j in kvh], [t[0] for t in kv_tiles], [t[1] for t in kv_tiles],
                           [_sink_row(sinks_ref, j) for j in kvh], qnw_ref[...], knw_ref[...], [bias_ref[j] for j in kvh])
        for j in kvh:
            _lanes_to_heads(o_ref, j, o_t[j])

    return pl.pallas_call(
        body, grid=(nb,),
        in_specs=[pl.BlockSpec(memory_space=pltpu.SMEM),
                  pl.BlockSpec((ATTN_BLOCK, 1024), lambda i: (i, 0)),
                  pl.BlockSpec((ATTN_BLOCK, 256), lambda i: (i, 0)),
                  pl.BlockSpec((ATTN_BLOCK, 256), lambda i: (jnp.maximum(i - 1, 0), 0)),
                  pl.BlockSpec((ATTN_BLOCK, 256), lambda i: (0, 0)),
                  pl.BlockSpec((ATTN_HD, 1), lambda i: (0, 0)),
                  pl.BlockSpec((1, ATTN_HD), lambda i: (0, 0))],
        out_specs=pl.BlockSpec((ATTN_BLOCK, 1024), lambda i: (i, 0)),
        out_shape=jax.ShapeDtypeStruct((T, 1024), F32),
        scratch_shapes=_attn_table_scratch(),
        name="attn_core_fwd", compiler_params=_cparams(),
    )(sinks, q, kv, kv, kv, qnw.reshape(ATTN_HD, 1), knw)


def _attn_out_fwd(o, gate, x, front, w_out):
    T = o.shape[0]

    def body(o_ref, g_ref, xa_ref, xb_ref, xc_ref, front_ref, w_ref, h1_ref):
        h = _padded_block(pl.program_id(0), front_ref[...], (xa_ref, xb_ref, xc_ref))
        og = o_ref[...] * _silu(g_ref[...])
        h1_ref[...] = h + _bdot(og, w_ref[...])

    return _row_call("attn_out_fwd", body, T, ROW_BLOCK, [o, gate], [front, w_out], [(1024, F32)], halos=_token_views(x))[0]


def _wgrad(xn, du, cg, name):
    T, kdim = xn.shape
    cdim = du.shape[1]
    rows = WGRAD_ROWS if T % WGRAD_ROWS == 0 else ROW_BLOCK
    nr, nc = T // rows, cdim // cg
    assert nc * cg == cdim

    def body(x_ref, du_ref, dw_ref):
        @pl.when(pl.program_id(1) == 0)
        def _():
            dw_ref[...] = jnp.zeros_like(dw_ref)
        dw_ref[...] += _bdot(x_ref[...], du_ref[...], TN)

    return pl.pallas_call(
        body, grid=(nc, nr),
        in_specs=[pl.BlockSpec((rows, kdim), lambda j, i: (i, 0)),
                  pl.BlockSpec((rows, cg), lambda j, i: (i, j))],
        out_specs=pl.BlockSpec((kdim, cg), lambda j, i: (0, j)),
        out_shape=jax.ShapeDtypeStruct((kdim, cdim), F32),
        name=name, compiler_params=_cparams(("arbitrary", "arbitrary")),
    )(xn, du)


def _attn_out_bwd(dh1, o, gate, w_out):
    T = o.shape[0]

    def body(dh_ref, o_ref, g_ref, w_ref, do_ref, dg_ref, dw_ref):
        @pl.when(pl.program_id(0) == 0)
        def _():
            dw_ref[...] = jnp.zeros_like(dw_ref)
        dh = dh_ref[...]
        dog = _bdot(dh, w_ref[...], NT)
        og, vjp = jax.vjp(lambda o_, g_: o_ * _silu(g_), o_ref[...], g_ref[...])
        do, dg = vjp(dog)
        do_ref[...] = do
        dg_ref[...] = dg
        dw_ref[...] += _bdot(og, dh, TN)

    return _row_call("attn_out_bwd", body, T, ROW_BLOCK, [dh1, o, gate], [w_out],
                     [(1024, F32), (1024, F32)], [((1024, 1024), F32)])


def _attn_core_bwd(do, q, kv, sinks, qnw, knw):
    T = q.shape[0]
    nb = T // ATTN_BLOCK
    rev = lambda i: nb - 1 - i

    def body(sinks_ref, do_ref, q_ref, kvc_ref, kvp_ref, kvm_ref, qnw_ref, knw_ref,
             dq_ref, dkv_ref, dsinks_ref, dqnw_ref, dknw_ref, carry_ref, meta_ref, bias_ref):
        step = pl.program_id(0)
        n = rev(step)
        _attn_tables(n, (step == 0) | (n <= 1), bias_ref)

        @pl.when(step == 0)
        def _():
            carry_ref[...] = jnp.zeros_like(carry_ref)
            meta_ref[...] = jnp.zeros_like(meta_ref)
            dsinks_ref[...] = jnp.zeros_like(dsinks_ref)
            dqnw_ref[...] = jnp.zeros_like(dqnw_ref)
            dknw_ref[...] = jnp.zeros_like(dknw_ref)

        lane16 = lax.broadcasted_iota(jnp.int32, (1, ATTN_HEADS), 1)
        dsinks = jnp.zeros((1, ATTN_HEADS), F32)
        kvh = range(ATTN_KV_HEADS)
        kv_tiles = [_attn_kv_tiles(kvp_ref, kvc_ref, kvm_ref, j) for j in kvh]
        fn = functools.partial(_attn_groups, bias=[bias_ref[j] for j in kvh], late_norm=False)
        _, vjp = jax.vjp(fn, [_heads_to_lanes(q_ref, j) for j in kvh], [t[0] for t in kv_tiles], [t[1] for t in kv_tiles],
                         [_sink_row(sinks_ref, j) for j in kvh], qnw_ref[...], knw_ref[...])
        dq_t, dks, dvs, dsr, dqn, dkn = vjp([_heads_to_lanes(do_ref, j) for j in kvh])
        dqnw_ref[...] += dqn
        dknw_ref[...] += dkn
        for j in kvh:
            _lanes_to_heads(dq_ref, j, dq_t[j])
            for hl in range(ATTN_GROUPS):
                dsinks = dsinks + jnp.where(lane16 == ATTN_GROUPS * j + hl,
                                            jnp.sum(dsr[j][:, ATTN_BLOCK * hl:ATTN_BLOCK * (hl + 1)]), 0.0)
            ksl = slice(ATTN_HD * j, ATTN_HD * (j + 1))
            vsl = slice(128 + ATTN_HD * j, 128 + ATTN_HD * (j + 1))
            for sl, d in ((ksl, dks[j]), (vsl, dvs[j])):
                dkv_ref[:, sl] = d[ATTN_BLOCK:2 * ATTN_BLOCK, :] + carry_ref[:, sl]
                carry_ref[:, sl] = d[0:ATTN_BLOCK, :]
                meta_ref[:, sl] += d[2 * ATTN_BLOCK:, :]
        dsinks_ref[...] += dsinks

        @pl.when(n == 0)
        def _():
            dkv_ref[FRONT_PAD:, :] += meta_ref[...]

    dq, dkv, dsinks, dqnw, dknw = pl.pallas_call(
        body, grid=(nb,),
        in_specs=[pl.BlockSpec(memory_space=pltpu.SMEM),
                  pl.BlockSpec((ATTN_BLOCK, 1024), lambda i: (rev(i), 0)),
                  pl.BlockSpec((ATTN_BLOCK, 1024), lambda i: (rev(i), 0)),
                  pl.BlockSpec((ATTN_BLOCK, 256), lambda i: (rev(i), 0)),
                  pl.BlockSpec((ATTN_BLOCK, 256), lambda i: (jnp.maximum(rev(i) - 1, 0), 0)),
                  pl.BlockSpec((ATTN_BLOCK, 256), lambda i: (0, 0)),
                  pl.BlockSpec((ATTN_HD, 1), lambda i: (0, 0)),
                  pl.BlockSpec((1, ATTN_HD), lambda i: (0, 0))],
        out_specs=[pl.BlockSpec((ATTN_BLOCK, 1024), lambda i: (rev(i), 0)),
                   pl.BlockSpec((ATTN_BLOCK, 256), lambda i: (rev(i), 0)),
                   pl.BlockSpec((1, ATTN_HEADS), lambda i: (0, 0)),
                   pl.BlockSpec((ATTN_HD, 1), lambda i: (0, 0)),
                   pl.BlockSpec((1, ATTN_HD), lambda i: (0, 0))],
        out_shape=[jax.ShapeDtypeStruct((T, 1024), F32), jax.ShapeDtypeStruct((T, 256), F32),
                   jax.ShapeDtypeStruct((1, ATTN_HEADS), F32), jax.ShapeDtypeStruct((ATTN_HD, 1), F32),
                   jax.ShapeDtypeStruct((1, ATTN_HD), F32)],
        scratch_shapes=[pltpu.VMEM((ATTN_BLOCK, 256), F32), pltpu.VMEM((N_META, 256), F32)] + _attn_table_scratch(),
        name="attn_core_bwd", compiler_params=_cparams(),
    )(sinks, do, q, kv, kv, kv, qnw.reshape(ATTN_HD, 1), knw)
    return dq, dkv, dsinks, dqnw.reshape(1, ATTN_HD), dknw


def _attn_in_bwd(dq, dkv, dgate, x, front, dh1, norm_w, w_in):
    T = dq.shape[0]
    n = T // ROW_BLOCK
    per = ROW_BLOCK // ATTN_BLOCK
    assert n >= 3

    def body(dq_ref, dkv_ref, dg_ref, dh1_ref, xa_ref, xb_ref, xc_ref, front_ref, nw_ref, w_ref,
             gx_ref, dfront_ref, dnw_ref, buf_ref, sems):
        i = pl.program_id(0)
        slot = i % 2

        def piece(step, k, s):
            return pltpu.make_async_copy(buf_ref.at[s, pl.ds(ATTN_BLOCK * k, ATTN_BLOCK)],
                                         gx_ref.at[pl.ds((per * step - 1 + k) * ATTN_BLOCK, ATTN_BLOCK)], sems.at[s, k])

        @pl.when(i == 0)
        def _():
            dnw_ref[...] = jnp.zeros_like(dnw_ref)
        for k in range(per):
            @pl.when((i >= 2) & ((k > 0) | (i > 2)))
            def _():
                piece(i - 2, k, slot).wait()
        h = _padded_block(i, front_ref[...], (xa_ref, xb_ref, xc_ref))
        dxn = (_bdot(dq_ref[...], w_ref[:, 0:1024], NT) + _bdot(dkv_ref[...], w_ref[:, 1024:1280], NT)
               + _bdot(dg_ref[...], w_ref[:, 1280:2304], NT))
        _, vjp = jax.vjp(_rms, h, nw_ref[...])
        dh, dnw = vjp(dxn)
        dnw_ref[...] += dnw
        buf_ref[slot] = dh1_ref[...] + dh

        @pl.when(i == 0)
        def _():
            dfront_ref[...] = buf_ref[0, 0:ATTN_BLOCK, :]
        for k in range(per):
            @pl.when((k > 0) | (i > 0))
            def _():
                piece(i, k, slot).start()

        @pl.when(i == n - 1)
        def _():
            for k in range(per):
                piece(i, k, slot).wait()
                piece(i - 1, k, 1 - slot).wait()

    idx = lambda i: (i, 0)
    const = lambda a: pl.BlockSpec(a.shape, functools.partial(lambda i, nd: (0,) * nd, nd=a.ndim))
    rows = [dq, dkv, dgate, dh1]
    views = _token_views(x)
    return pl.pallas_call(
        body, grid=(n,),
        in_specs=[pl.BlockSpec((ROW_BLOCK, a.shape[1]), idx) for a in rows]
        + [pl.BlockSpec((hr, a.shape[1]), fn) for a, hr, fn in views] + [const(front), const(norm_w), const(w_in)],
        out_specs=[pl.BlockSpec(memory_space=pltpu.HBM), pl.BlockSpec((ATTN_BLOCK, D_MODEL), lambda i: (0, 0)),
                   pl.BlockSpec((1, D_MODEL), lambda i: (0, 0))],
        out_shape=[jax.ShapeDtypeStruct(x.shape, F32), jax.ShapeDtypeStruct((ATTN_BLOCK, D_MODEL), F32),
                   jax.ShapeDtypeStruct((1, D_MODEL), F32)],
        scratch_shapes=[pltpu.VMEM((2, ROW_BLOCK, D_MODEL), F32), pltpu.SemaphoreType.DMA((2, per))],
        name="attn_in_bwd", compiler_params=_cparams(),
    )(*rows, *[a for a, _, _ in views], front, norm_w, w_in)


def _dn_in_fwd(h1, norm_w, w_in):
    T = h1.shape[0]

    def body(h_ref, nw_ref, w_ref, xn_ref, qkv_ref, z_ref, ba_ref):
        xn = _rms(h_ref[...], nw_ref[...]).astype(BF16)
        xn_ref[...] = xn
        qkv_ref[...] = jnp.dot(xn, w_ref[:, 0:4096], preferred_element_type=F32)
        z_ref[...] = jnp.dot(xn, w_ref[:, 4096:6144], preferred_element_type=F32)
        ba_ref[...] = jnp.dot(xn, w_ref[:, 6144:6176], preferred_element_type=F32)

    return _row_call("dn_in_fwd", body, T, ROW_BLOCK, [h1], [norm_w, w_in],
                     [(1024, BF16), (4096, F32), (2048, F32), (32, F32)])


def _shift_down(cur, prev8, s):
    i8 = lax.broadcasted_iota(jnp.int32, (8, cur.shape[1]), 0)
    r = pltpu.roll(cur, s, 0)
    head = jnp.where(i8 < s, pltpu.roll(prev8, s, 0), r[0:8])
    return jnp.concatenate([head, r[8:]], axis=0)


def _shift_up(cur, next8, s):
    n = cur.shape[0]
    i8 = lax.broadcasted_iota(jnp.int32, (8, cur.shape[1]), 0)
    r = pltpu.roll(cur, n - s, 0)
    tail = jnp.where(i8 >= 8 - s, pltpu.roll(next8, 8 - s, 0), r[n - 8:])
    return jnp.concatenate([r[:n - 8], tail], axis=0)


def _conv_taps(cur, prev8):
    return [cur] + [_shift_down(cur, prev8, s) for s in range(1, DN_CONV_K)]


def _conv_tile(taps, w):
    out = w[3:4, :] * taps[0]
    for s in range(1, DN_CONV_K):
        out = out + w[3 - s:4 - s, :] * taps[s]
    return out


def _l2n(a, scale):
    return a * (lax.rsqrt(jnp.sum(a * a, axis=-1, keepdims=True) + NORM_EPS) * scale)


def _dn_post_tile(c, t):
    a = _silu(c)
    if t < DN_K_HEADS:
        return _l2n(a, DN_HD ** -0.5)
    if t < 2 * DN_K_HEADS:
        return _l2n(a, 1.0)
    return a


def _dn_beta_g(ba, a_log, dt_bias, live):
    beta = jax.nn.sigmoid(ba[:, 0:DN_V_HEADS]) * live
    g = -jnp.exp(a_log) * _softplus(ba[:, DN_V_HEADS:] + dt_bias) * live
    return beta, g


def _live_rows(i, rb):
    rows = i * rb + lax.broadcasted_iota(jnp.int32, (rb, 1), 0)
    return (rows >= FRONT_PAD).astype(F32)


def _halo_spec_args(x, rb):
    per = rb // 8
    return (x, 8, lambda i: (jnp.maximum(i * per - 1, 0), 0))


def _dn_conv_fwd(qkv, ba, conv_w, a_log, dt_bias):
    T = qkv.shape[0]

    def body(x_ref, ba_ref, halo_ref, cw_ref, al_ref, dtb_ref, q_ref, k_ref, v_ref, bg_ref, c_ref):
        i = pl.program_id(0)
        first = (i > 0).astype(F32)
        for t in range(DN_CONV_W // 128):
            cols = slice(128 * t, 128 * (t + 1))
            c = _conv_tile(_conv_taps(x_ref[:, cols], halo_ref[:, cols] * first), cw_ref[:, cols])
            c_ref[:, cols] = c.astype(BF16)
            out = _dn_post_tile(c, t)
            if t < DN_K_HEADS:
                q_ref[:, cols] = out
            elif t < 2 * DN_K_HEADS:
                k_ref[:, 128 * (t - 8):128 * (t - 7)] = out
            else:
                v_ref[:, 128 * (t - 16):128 * (t - 15)] = out
        beta, g = _dn_beta_g(ba_ref[...], al_ref[...], dtb_ref[...], _live_rows(i, ROW_BLOCK))
        bg_ref[:, 0:DN_V_HEADS] = beta
        bg_ref[:, DN_V_HEADS:] = g

    return _row_call("dn_conv_fwd", body, T, ROW_BLOCK, [qkv, ba], [conv_w, a_log, dt_bias],
                     [(1024, F32), (1024, F32), (2048, F32), (32, F32), (4096, BF16)], halos=[_halo_spec_args(qkv, ROW_BLOCK)])


def _chunk_masks():
    r = lax.broadcasted_iota(jnp.int32, (DN_CHUNK, DN_CHUNK), 0)
    c = lax.broadcasted_iota(jnp.int32, (DN_CHUNK, DN_CHUNK), 1)
    return r >= c, r > c, r == c, r <= c


def _tri_inv_block(x):
    B = TRI_BLOCK
    n = range(len(x))
    r_, c_ = lax.broadcasted_iota(jnp.int32, (B, B), 0), lax.broadcasted_iota(jnp.int32, (B, B), 1)
    ainv = [jnp.where(r_ == c_, 1.0, 0.0) + x[h] for h in n]
    p = [_bdot(x[h], x[h]) for h in n]
    for _ in range(B.bit_length() - 3):
        r = [_bdot(jnp.concatenate([p[h], ainv[h]], axis=0), p[h]) for h in n]
        ainv = [ainv[h] + r[h][B:] for h in n]
        p = [r[h][:B] for h in n]
    return [ainv[h] + _bdot(ainv[h], p[h]) for h in n]


def _tri_inv(x):
    B = TRI_BLOCK
    assert DN_CHUNK == 2 * B
    n = len(x)
    diag = _tri_inv_block([x[h][:B, :B] for h in range(n)] + [x[h][B:, B:] for h in range(n)])
    a11, a22 = diag[:n], diag[n:]
    a21 = [_bdot(_bdot(a22[h], x[h][B:, :B]), a11[h]) for h in range(n)]
    zero = jnp.zeros((B, B), F32)
    return [jnp.concatenate([jnp.concatenate([a11[h], zero], axis=1), jnp.concatenate([a21[h], a22[h]], axis=1)], axis=0)
            for h in range(n)]


@jax.custom_vjp
def _tri_inv_known(x, a):
    return a


def _tri_inv_known_fwd(x, a):
    return a, a


def _tri_inv_known_bwd(a, da):
    return [_bdot(_bdot(a[h], da[h], TN), a[h], NT) for h in range(len(a))], [jnp.zeros_like(t) for t in a]


_tri_inv_known.defvjp(_tri_inv_known_fwd, _tri_inv_known_bwd)


@jax.custom_vjp
def _known(computed, value):
    return value


def _known_fwd(computed, value):
    return value, None


def _known_bwd(_, g):
    return g, jax.tree.map(jnp.zeros_like, g)


_known.defvjp(_known_fwd, _known_bwd)


def _dn_chunk_step(S, q, k, v, beta, g, masks, known=None):
    causal, strict, eye, upper = masks
    C, W = DN_CHUNK, DN_HD
    heads = range(len(v))
    k_t = [k[j].T for j in range(len(k))]
    qk_kk = [_bdot(jnp.concatenate([q[j], k[j]], axis=0), k_t[j]) for j in range(len(q))]
    if known is not None:
        qk_kk = _known(qk_kk, known["qk_kk"])
    g_b = [jnp.broadcast_to(g[h], (C, C)) for h in heads]
    beta_b = [jnp.broadcast_to(beta[h], (C, W)) for h in heads]
    g_row = [jnp.sum(jnp.where(eye, g_b[h], 0.0), axis=0, keepdims=True) for h in heads]
    gc_col = [jnp.sum(jnp.where(causal, g_row[h], 0.0), axis=1, keepdims=True) for h in heads]
    gc_row = [jnp.sum(jnp.where(upper, g_b[h], 0.0), axis=0, keepdims=True) for h in heads]
    g_last = [jnp.sum(g_row[h], axis=1, keepdims=True) for h in heads]
    gc_b = [jnp.broadcast_to(gc_col[h], (C, W)) for h in heads]
    decay = [jnp.exp(jnp.where(causal, gc_b[h][:, :C] - gc_row[h], NEG)) for h in heads]
    eg_b = [jnp.exp(gc_b[h]) for h in heads]
    x = [jnp.where(strict, qk_kk[h // 2][C:] * beta_b[h][:, :C] * decay[h], 0.0) * -1.0 for h in heads]
    ainv = _tri_inv(x) if known is None else _tri_inv_known(x, known["inv"])
    uw = [_bdot(ainv[h], jnp.concatenate([v[h] * beta_b[h], k[h // 2] * (beta_b[h] * eg_b[h])], axis=1)) for h in heads]
    if known is not None:
        uw = _known(uw, known["uw"])
    q_eg = [q[h // 2] * eg_b[h] for h in heads]
    attn = [qk_kk[h // 2][:C] * decay[h] for h in heads]
    k_st = [k_t[h // 2] * jnp.exp(g_last[h] - gc_row[h]) for h in heads]
    s_dec = [jnp.exp(g_last[h]) for h in heads]
    prep = (uw, q_eg, attn, k_st, s_dec)
    if S is None:
        return prep, dict(inv=ainv, uw=uw, qk_kk=qk_kk)
    s_new, o, _ = _dn_chunk_tail(S, prep, None if known is None else known["v_new"])
    return s_new, o


def _dn_chunk_tail(S, prep, known_v_new=None):
    uw, q_eg, attn, k_st, s_dec = prep
    C, W = DN_CHUNK, DN_HD
    heads = range(len(uw))
    ws_qs = [_bdot(jnp.concatenate([uw[h][:, W:], q_eg[h]], axis=0), S[h]) for h in heads]
    v_new = [uw[h][:, :W] - ws_qs[h][:C] for h in heads]
    if known_v_new is not None:
        v_new = _known(v_new, known_v_new)
    o = [ws_qs[h][C:] + _bdot(attn[h], v_new[h]) for h in heads]
    s_new = [S[h] * s_dec[h] + _bdot(k_st[h], v_new[h]) for h in heads]
    return s_new, o, v_new


def _dn_chunk_tiles(q_ref, k_ref, v_ref, bg_ref, c, first, count):
    rows = slice(DN_CHUNK * c, DN_CHUNK * (c + 1))
    q = [q_ref[rows, 128 * j:128 * (j + 1)] for j in range(first // 2, (first + count) // 2)]
    k = [k_ref[rows, 128 * j:128 * (j + 1)] for j in range(first // 2, (first + count) // 2)]
    v = [v_ref[rows, 128 * h:128 * (h + 1)] for h in range(first, first + count)]
    beta = [bg_ref[rows, h:h + 1] for h in range(first, first + count)]
    g = [bg_ref[rows, DN_V_HEADS + h:DN_V_HEADS + h + 1] for h in range(first, first + count)]
    return q, k, v, beta, g


def _dn_scan_fwd(qn, kn, v, bg):
    T = qn.shape[0]
    nc = T // DN_CHUNK
    rows = SCAN_CHUNKS * DN_CHUNK
    assert nc % SCAN_CHUNKS == 0

    def body(q_ref, k_ref, v_ref, bg_ref, o_ref, ssave_ref, inv_ref, uw_ref, vn_ref, qk_ref, s_ref):
        @pl.when(pl.program_id(0) == 0)
        def _():
            s_ref[...] = jnp.zeros_like(s_ref)
        masks = _chunk_masks()
        for first in range(0, DN_V_HEADS, SCAN_FWD_GROUP):
            heads = range(first, first + SCAN_FWD_GROUP)
            preps = [_dn_chunk_step(None, *_dn_chunk_tiles(q_ref, k_ref, v_ref, bg_ref, c, first, SCAN_FWD_GROUP), masks)
                     for c in range(SCAN_CHUNKS)]
            state = [s_ref[h] for h in heads]
            for c, (prep, saved) in enumerate(preps):
                for i, h in enumerate(heads):
                    ssave_ref[c, h] = state[i]
                    inv_ref[c, h] = saved["inv"][i].astype(BF16)
                    uw_ref[c, h] = saved["uw"][i].astype(BF16)
                for i, j in enumerate(range(first // 2, (first + SCAN_FWD_GROUP) // 2)):
                    qk_ref[c, j] = saved["qk_kk"][i].astype(BF16)
                state, o, v_new = _dn_chunk_tail(state, prep)
                for i, h in enumerate(heads):
                    o_ref[DN_CHUNK * c:DN_CHUNK * (c + 1), 128 * h:128 * (h + 1)] = o[i]
                    vn_ref[c, h] = v_new[i].astype(BF16)
            for i, h in enumerate(heads):
                s_ref[h] = state[i]

    return pl.pallas_call(
        body, grid=(nc // SCAN_CHUNKS,),
        in_specs=[pl.BlockSpec((rows, 1024), lambda i: (i, 0)),
                  pl.BlockSpec((rows, 1024), lambda i: (i, 0)),
                  pl.BlockSpec((rows, 2048), lambda i: (i, 0)),
                  pl.BlockSpec((rows, 32), lambda i: (i, 0))],
        out_specs=[pl.BlockSpec((rows, 2048), lambda i: (i, 0)),
                   pl.BlockSpec((SCAN_CHUNKS, DN_V_HEADS, DN_HD, DN_HD), lambda i: (i, 0, 0, 0)),
                   pl.BlockSpec((SCAN_CHUNKS, DN_V_HEADS, DN_CHUNK, DN_CHUNK), lambda i: (i, 0, 0, 0)),
                   pl.BlockSpec((SCAN_CHUNKS, DN_V_HEADS, DN_CHUNK, 2 * DN_HD), lambda i: (i, 0, 0, 0)),
                   pl.BlockSpec((SCAN_CHUNKS, DN_V_HEADS, DN_CHUNK, DN_HD), lambda i: (i, 0, 0, 0)),
                   pl.BlockSpec((SCAN_CHUNKS, DN_K_HEADS, 2 * DN_CHUNK, DN_CHUNK), lambda i: (i, 0, 0, 0))],
        out_shape=[jax.ShapeDtypeStruct((T, 2048), F32),
                   jax.ShapeDtypeStruct((nc, DN_V_HEADS, DN_HD, DN_HD), F32),
                   jax.ShapeDtypeStruct((nc, DN_V_HEADS, DN_CHUNK, DN_CHUNK), BF16),
                   jax.ShapeDtypeStruct((nc, DN_V_HEADS, DN_CHUNK, 2 * DN_HD), BF16),
                   jax.ShapeDtypeStruct((nc, DN_V_HEADS, DN_CHUNK, DN_HD), BF16),
                   jax.ShapeDtypeStruct((nc, DN_K_HEADS, 2 * DN_CHUNK, DN_CHUNK), BF16)],
        scratch_shapes=[pltpu.VMEM((DN_V_HEADS, DN_HD, DN_HD), F32)],
        name="dn_scan_fwd", compiler_params=_cparams(),
    )(qn, kn, v, bg)


def _dn_gate_tile(o, z, onw):
    return _rms(o, onw) * _silu(z)


def _dn_out_fwd(o, z, h1, target, w_out, onw):
    T = o.shape[0]

    def body(o_ref, z_ref, h_ref, ta_ref, tb_ref, tc_ref, w_ref, onw_ref, dy_ref, og_ref, loss_ref):
        i = pl.program_id(0)

        @pl.when(i == 0)
        def _():
            loss_ref[...] = jnp.zeros_like(loss_ref)
        for h in range(DN_V_HEADS):
            cols = slice(128 * h, 128 * (h + 1))
            og_ref[:, cols] = _dn_gate_tile(o_ref[:, cols], z_ref[:, cols], onw_ref[...]).astype(BF16)
        y = h_ref[...] + jnp.dot(og_ref[...], w_ref[...], preferred_element_type=F32)
        rows = i * ROW_BLOCK + lax.broadcasted_iota(jnp.int32, (ROW_BLOCK, 1), 0)
        diff = jnp.where(rows >= FRONT_PAD + N_META, y - _padded_block(i, None, (ta_ref, tb_ref, tc_ref)), 0.0)
        dy_ref[...] = diff * (1.0 / D_MODEL)
        loss_ref[...] += jnp.sum(diff * diff) * (0.5 / D_MODEL)

    return _row_call("dn_out_fwd", body, T, ROW_BLOCK, [o, z, h1], [w_out, onw],
                     [(1024, F32), (2048, BF16)], [((1, 128), F32)], halos=_token_views(target))


def _dn_out_bwd(dy, o, z, w_out, onw):
    T = o.shape[0]

    def body(dy_ref, o_ref, z_ref, w_ref, onw_ref, do_ref, dz_ref, donw_ref, dog_ref):
        @pl.when(pl.program_id(0) == 0)
        def _():
            donw_ref[...] = jnp.zeros_like(donw_ref)
        dy = dy_ref[...].astype(BF16)
        donw = jnp.zeros((1, DN_HD), F32)
        for half in range(2):
            hcols = slice(1024 * half, 1024 * (half + 1))
            dog_ref[:, hcols] = lax.dot_general(dy, w_ref[hcols, :], NT, preferred_element_type=F32)
        for h in range(DN_V_HEADS):
            cols = slice(128 * h, 128 * (h + 1))
            _, vjp = jax.vjp(_dn_gate_tile, o_ref[:, cols], z_ref[:, cols], onw_ref[...])
            do, dz, dn = vjp(dog_ref[:, cols])
            do_ref[:, cols] = do
            dz_ref[:, cols] = dz
            donw = donw + dn
        donw_ref[...] += donw

    return _row_call("dn_out_bwd", body, T, ROW_BLOCK, [dy, o, z], [w_out, onw],
                     [(2048, F32), (2048, F32)], [((1, DN_HD), F32)], scratch=[pltpu.VMEM((ROW_BLOCK, 2048), F32)])


def _dn_scan_bwd(do, qn, kn, v, bg, ssave, saved):
    T = qn.shape[0]
    nc = T // DN_CHUNK
    chunks = SCAN_BWD_CHUNKS
    ns = nc // chunks
    rows = chunks * DN_CHUNK
    rev = lambda i: ns - 1 - i

    def body(do_ref, q_ref, k_ref, v_ref, bg_ref, ss_ref, inv_ref, uw_ref, vn_ref, qk_ref,
             dq_ref, dk_ref, dv_ref, dbg_ref, ds_ref):
        @pl.when(pl.program_id(0) == 0)
        def _():
            ds_ref[...] = jnp.zeros_like(ds_ref)
        lane32 = lax.broadcasted_iota(jnp.int32, (1, 2 * DN_V_HEADS), 1)
        masks = _chunk_masks()
        dbg = [jnp.zeros((DN_CHUNK, 2 * DN_V_HEADS), F32) for _ in range(chunks)]
        for first in range(0, DN_V_HEADS, SCAN_BWD_GROUP):
            heads = range(first, first + SCAN_BWD_GROUP)
            vjps = []
            for c in range(chunks):
                known = dict(inv=[inv_ref[c, h].astype(F32) for h in heads], uw=[uw_ref[c, h].astype(F32) for h in heads],
                             v_new=[vn_ref[c, h].astype(F32) for h in heads],
                             qk_kk=[qk_ref[c, j].astype(F32) for j in range(first // 2, (first + SCAN_BWD_GROUP) // 2)])
                fn = functools.partial(_dn_chunk_step, masks=masks, known=known)
                vjps.append(jax.vjp(fn, [ss_ref[c, h] for h in heads],
                                    *_dn_chunk_tiles(q_ref, k_ref, v_ref, bg_ref, c, first, SCAN_BWD_GROUP))[1])
            ds = [ds_ref[h] for h in heads]
            for c in reversed(range(chunks)):
                crows = slice(DN_CHUNK * c, DN_CHUNK * (c + 1))
                ds, dq, dk, dv, dbeta, dg = vjps[c]((ds, [do_ref[crows, 128 * h:128 * (h + 1)] for h in heads]))
                for i, h in enumerate(heads):
                    dv_ref[crows, 128 * h:128 * (h + 1)] = dv[i]
                    dbg[c] = dbg[c] + jnp.where(lane32 == h, dbeta[i], 0.0) + jnp.where(lane32 == DN_V_HEADS + h, dg[i], 0.0)
                for i, j in enumerate(range(first // 2, (first + SCAN_BWD_GROUP) // 2)):
                    dq_ref[crows, 128 * j:128 * (j + 1)] = dq[i]
                    dk_ref[crows, 128 * j:128 * (j + 1)] = dk[i]
            for i, h in enumerate(heads):
                ds_ref[h] = ds[i]
        for c in range(chunks):
            dbg_ref[DN_CHUNK * c:DN_CHUNK * (c + 1), :] = dbg[c]

    return pl.pallas_call(
        body, grid=(ns,),
        in_specs=[pl.BlockSpec((rows, 2048), lambda i: (rev(i), 0)),
                  pl.BlockSpec((rows, 1024), lambda i: (rev(i), 0)),
                  pl.BlockSpec((rows, 1024), lambda i: (rev(i), 0)),
                  pl.BlockSpec((rows, 2048), lambda i: (rev(i), 0)),
                  pl.BlockSpec((rows, 32), lambda i: (rev(i), 0)),
                  pl.BlockSpec((chunks, DN_V_HEADS, DN_HD, DN_HD), lambda i: (rev(i), 0, 0, 0)),
                  pl.BlockSpec((chunks, DN_V_HEADS, DN_CHUNK, DN_CHUNK), lambda i: (rev(i), 0, 0, 0)),
                  pl.BlockSpec((chunks, DN_V_HEADS, DN_CHUNK, 2 * DN_HD), lambda i: (rev(i), 0, 0, 0)),
                  pl.BlockSpec((chunks, DN_V_HEADS, DN_CHUNK, DN_HD), lambda i: (rev(i), 0, 0, 0)),
                  pl.BlockSpec((chunks, DN_K_HEADS, 2 * DN_CHUNK, DN_CHUNK), lambda i: (rev(i), 0, 0, 0))],
        out_specs=[pl.BlockSpec((rows, 1024), lambda i: (rev(i), 0)),
                   pl.BlockSpec((rows, 1024), lambda i: (rev(i), 0)),
                   pl.BlockSpec((rows, 2048), lambda i: (rev(i), 0)),
                   pl.BlockSpec((rows, 32), lambda i: (rev(i), 0))],
        out_shape=[jax.ShapeDtypeStruct((T, 1024), F32), jax.ShapeDtypeStruct((T, 1024), F32),
                   jax.ShapeDtypeStruct((T, 2048), F32), jax.ShapeDtypeStruct((T, 32), F32)],
        scratch_shapes=[pltpu.VMEM((DN_V_HEADS, DN_HD, DN_HD), F32)],
        name="dn_scan_bwd", compiler_params=_cparams(),
    )(do, qn, kn, v, bg, ssave, *saved)


def _dn_conv_bwd(dqn, dkn, dv, dbg, qkv, conv_out, ba, conv_w, a_log, dt_bias):
    T = qkv.shape[0]
    rb = ROW_BLOCK // 2
    nr = T // rb

    def body(dq_ref, dk_ref, dv_ref, dbg_ref, x_ref, c_ref, ba_ref, cw_ref, al_ref, dtb_ref,
             dx_ref, dba_ref, dcw_ref, dal_ref, ddtb_ref, carry_ref):
        step = pl.program_id(0)
        i = nr - 1 - step

        @pl.when(step == 0)
        def _():
            carry_ref[...] = jnp.zeros_like(carry_ref)
            dcw_ref[...] = jnp.zeros_like(dcw_ref)
            dal_ref[...] = jnp.zeros_like(dal_ref)
            ddtb_ref[...] = jnp.zeros_like(ddtb_ref)
        for t in range(DN_CONV_W // 128):
            cols = slice(128 * t, 128 * (t + 1))
            w, x = cw_ref[:, cols], x_ref[:, cols]
            if t < DN_K_HEADS:
                dout = dq_ref[:, cols]
            elif t < 2 * DN_K_HEADS:
                dout = dk_ref[:, 128 * (t - 8):128 * (t - 7)]
            else:
                dout = dv_ref[:, 128 * (t - 16):128 * (t - 15)]
            _, vjp = jax.vjp(functools.partial(_dn_post_tile, t=t), c_ref[:, cols].astype(F32))
            (dc,) = vjp(dout)
            nxt = carry_ref[:, cols]
            dx = w[3:4, :] * dc
            dcw_ref[3:4, cols] += jnp.sum(dc * x, axis=0, keepdims=True)
            for s in range(1, DN_CONV_K):
                up = _shift_up(dc, nxt, s)
                dx = dx + w[3 - s:4 - s, :] * up
                dcw_ref[3 - s:4 - s, cols] += jnp.sum(up * x, axis=0, keepdims=True)
            dx_ref[:, cols] = dx
            carry_ref[:, cols] = dc[0:8, :]
        fn = functools.partial(_dn_beta_g, live=_live_rows(i, rb))
        _, vjp = jax.vjp(fn, ba_ref[...], al_ref[...], dtb_ref[...])
        dba, dal, ddtb = vjp((dbg_ref[:, 0:DN_V_HEADS], dbg_ref[:, DN_V_HEADS:]))
        dba_ref[...] = dba
        dal_ref[...] += dal
        ddtb_ref[...] += ddtb

    return _row_call("dn_conv_bwd", body, T, rb, [dqn, dkn, dv, dbg, qkv, conv_out, ba], [conv_w, a_log, dt_bias],
                     [(4096, F32), (32, F32)], [((DN_CONV_K, 4096), F32), ((1, DN_V_HEADS), F32), ((1, DN_V_HEADS), F32)],
                     reverse=True, scratch=[pltpu.VMEM((8, 4096), F32)])


def _dn_in_bwd(dqkv, dz, dba, h1, dy, norm_w, w_in):
    T = h1.shape[0]

    def body(dqkv_ref, dz_ref, dba_ref, h_ref, dy_ref, nw_ref, w_ref, dh_ref, dnw_ref):
        @pl.when(pl.program_id(0) == 0)
        def _():
            dnw_ref[...] = jnp.zeros_like(dnw_ref)
        dxn = (_bdot(dqkv_ref[...], w_ref[:, 0:4096], NT) + _bdot(dz_ref[...], w_ref[:, 4096:6144], NT)
               + _bdot(dba_ref[...], w_ref[:, 6144:6176], NT))
        _, vjp = jax.vjp(_rms, h_ref[...], nw_ref[...])
        dh, dnw = vjp(dxn)
        dh_ref[...] = (dy_ref[...] + dh) * _live_rows(pl.program_id(0), ROW_BLOCK)
        dnw_ref[...] += dnw

    return _row_call("dn_in_bwd", body, T, ROW_BLOCK, [dqkv, dz, dba, h1, dy], [norm_w, w_in],
                     [(1024, F32)], [((1, 1024), F32)])


def _exchange(parts, scatter, name):
    n = len(parts)
    out_shape = [jax.ShapeDtypeStruct(p.shape if sc else (N_DEV,) + p.shape, p.dtype) for p, sc in zip(parts, scatter)]

    def body(*refs):
        ins, outs = refs[:n], refs[n:2 * n]
        send_sems, recv_sems, local_sems = refs[2 * n:]
        x, y, c = lax.axis_index("x"), lax.axis_index("y"), lax.axis_index("c")
        me = 4 * x + 2 * y + c
        peers = []
        for k in range(1, N_DEV):
            px = 1 - x if k & 4 else x
            py = 1 - y if k & 2 else y
            pc = 1 - c if k & 1 else c
            peers.append(((px, py, pc), 4 * px + 2 * py + pc))

        def src(a, idx):
            return ins[a].at[idx] if scatter[a] else ins[a]

        local = [pltpu.make_async_copy(src(a, me), outs[a].at[me], local_sems.at[a]) for a in range(n)]
        for cp in local:
            cp.start()
        for a in range(n):
            for k, (dev, idx) in enumerate(peers):
                pltpu.make_async_remote_copy(
                    src_ref=src(a, idx), dst_ref=outs[a].at[me], send_sem=send_sems.at[a, k], recv_sem=recv_sems.at[a, k],
                    device_id=dev, device_id_type=pl.DeviceIdType.MESH).start()
        for a in range(n):
            for k, (dev, idx) in enumerate(peers):
                pltpu.make_async_remote_copy(
                    src_ref=src(a, idx), dst_ref=outs[a].at[idx], send_sem=send_sems.at[a, k], recv_sem=recv_sems.at[a, k],
                    device_id=dev, device_id_type=pl.DeviceIdType.MESH).wait()
        for cp in local:
            cp.wait()

    hbm = pl.BlockSpec(memory_space=pltpu.HBM)
    return pl.pallas_call(
        body, out_shape=out_shape, in_specs=[hbm] * n, out_specs=[hbm] * n,
        scratch_shapes=[pltpu.SemaphoreType.DMA((n, N_DEV - 1)), pltpu.SemaphoreType.DMA((n, N_DEV - 1)),
                        pltpu.SemaphoreType.DMA((n,))],
        name=name,
    )(*parts)


def _gather_two_level(parts, name):
    n = len(parts)
    out_shape = [jax.ShapeDtypeStruct((N_DEV,) + p.shape, p.dtype) for p in parts]

    def body(*refs):
        ins, outs = refs[:n], refs[n:2 * n]
        send_sems, recv_sems, local_sems = refs[2 * n:]
        x, y, c = lax.axis_index("x"), lax.axis_index("y"), lax.axis_index("c")
        idx = lambda px, py, pc: 4 * px + 2 * py + pc
        me, sibling = (x, y, c), (x, y, 1 - c)
        chips = [(1 - x, y), (x, 1 - y), (1 - x, 1 - y)]

        def copy(a, k, block, to, src=None):
            slot = outs[a].at[idx(*block)]
            return pltpu.make_async_remote_copy(
                src_ref=slot if src is None else src, dst_ref=slot, send_sem=send_sems.at[a, k], recv_sem=recv_sems.at[a, k],
                device_id=to, device_id_type=pl.DeviceIdType.MESH)

        local = [pltpu.make_async_copy(ins[a], outs[a].at[idx(*me)], local_sems.at[a]) for a in range(n)]
        for cp in local:
            cp.start()
        sent = []
        for a in range(n):
            sent.append(copy(a, 0, me, sibling, src=ins[a]))
            sent += [copy(a, 1 + j, me, (*chip, c), src=ins[a]) for j, chip in enumerate(chips)]
        for cp in sent:
            cp.start()
        for a in range(n):
            for j, chip in enumerate(chips):
                copy(a, 1 + j, (*chip, c), me).wait_recv()
                passed = copy(a, 4 + j, (*chip, c), sibling)
                passed.start()
                sent.append(passed)
        for a in range(n):
            copy(a, 0, sibling, me).wait_recv()
            for j, chip in enumerate(chips):
                copy(a, 4 + j, (*chip, 1 - c), me).wait_recv()
        for cp in sent:
            cp.wait_send()
        for cp in local:
            cp.wait()

    hbm = pl.BlockSpec(memory_space=pltpu.HBM)
    return pl.pallas_call(
        body, out_shape=out_shape, in_specs=[hbm] * n, out_specs=[hbm] * n,
        scratch_shapes=[pltpu.SemaphoreType.DMA((n, N_DEV - 1)), pltpu.SemaphoreType.DMA((n, N_DEV - 1)),
                        pltpu.SemaphoreType.DMA((n,))],
        name=name,
    )(*parts)


def _swap_with_sibling(parts, name):
    n = len(parts)

    def body(*refs):
        ins, outs = refs[:n], refs[n:2 * n]
        send_sems, recv_sems = refs[2 * n:]
        x, y, c = lax.axis_index("x"), lax.axis_index("y"), lax.axis_index("c")
        copies = [pltpu.make_async_remote_copy(
            src_ref=ins[a].at[1 - c], dst_ref=outs[a], send_sem=send_sems.at[a], recv_sem=recv_sems.at[a],
            device_id=(x, y, 1 - c), device_id_type=pl.DeviceIdType.MESH) for a in range(n)]
        for cp in copies:
            cp.start()
        for cp in copies:
            cp.wait()

    hbm = pl.BlockSpec(memory_space=pltpu.HBM)
    return pl.pallas_call(
        body, out_shape=[jax.ShapeDtypeStruct(p.shape[1:], p.dtype) for p in parts], in_specs=[hbm] * n, out_specs=[hbm] * n,
        scratch_shapes=[pltpu.SemaphoreType.DMA((n,)), pltpu.SemaphoreType.DMA((n,))],
        name=name,
    )(*parts)


def _pair_sum(a, b, name):
    R, C = a.shape
    rb = _adam_rows(R)

    def body(a_ref, b_ref, o_ref):
        o_ref[...] = (a_ref[...].astype(F32) + b_ref[...].astype(F32)).astype(BF16)

    blk = pl.BlockSpec((rb, C), lambda i: (i, 0))
    return pl.pallas_call(body, grid=(R // rb,), in_specs=[blk, blk], out_specs=blk,
                          out_shape=jax.ShapeDtypeStruct((R, C), BF16), name=name, compiler_params=_cparams())(a, b)


def _exchange_chips(parts, name):
    n = len(parts)
    n_chips = N_DEV // 2

    def body(*refs):
        ins, outs = refs[:n], refs[n:2 * n]
        send_sems, recv_sems, local_sems = refs[2 * n:]
        x, y, c = lax.axis_index("x"), lax.axis_index("y"), lax.axis_index("c")
        mine = 2 * x + y
        chips = [(1 - x, y), (x, 1 - y), (1 - x, 1 - y)]
        local = [pltpu.make_async_copy(ins[a].at[mine], outs[a].at[mine], local_sems.at[a]) for a in range(n)]
        for cp in local:
            cp.start()
        for a in range(n):
            for k, (px, py) in enumerate(chips):
                pltpu.make_async_remote_copy(
                    src_ref=ins[a].at[2 * px + py], dst_ref=outs[a].at[mine], send_sem=send_sems.at[a, k],
                    recv_sem=recv_sems.at[a, k], device_id=(px, py, c), device_id_type=pl.DeviceIdType.MESH).start()
        for a in range(n):
            for k, (px, py) in enumerate(chips):
                pltpu.make_async_remote_copy(
                    src_ref=ins[a].at[2 * px + py], dst_ref=outs[a].at[2 * px + py], send_sem=send_sems.at[a, k],
                    recv_sem=recv_sems.at[a, k], device_id=(px, py, c), device_id_type=pl.DeviceIdType.MESH).wait()
        for cp in local:
            cp.wait()

    hbm = pl.BlockSpec(memory_space=pltpu.HBM)
    return pl.pallas_call(
        body, out_shape=[jax.ShapeDtypeStruct(p.shape, p.dtype) for p in parts], in_specs=[hbm] * n, out_specs=[hbm] * n,
        scratch_shapes=[pltpu.SemaphoreType.DMA((n, n_chips - 1)), pltpu.SemaphoreType.DMA((n, n_chips - 1)),
                        pltpu.SemaphoreType.DMA((n,))],
        name=name,
    )(*parts)


def _adam_rows(rows):
    for rb in (128, 64, 40, 16, 8):
        if rows % rb == 0:
            return rb
    return rows


def _adamw(stack, w, m, v, name):
    R, C = w.shape
    rb = _adam_rows(R)
    slots = stack.shape[0]

    def body(s_ref, w_ref, m_ref, v_ref, g_ref, d_ref, nm_ref, nv_ref):
        g = s_ref[0].astype(F32)
        for s in range(1, slots):
            g = g + s_ref[s].astype(F32)
        nm = ADAM_B1 * m_ref[...] + (1.0 - ADAM_B1) * g
        nv = ADAM_B2 * v_ref[...] + (1.0 - ADAM_B2) * (g * g)
        m_hat = nm / (1.0 - ADAM_B1 ** ADAM_STEP)
        v_hat = nv / (1.0 - ADAM_B2 ** ADAM_STEP)
        g_ref[...] = g
        d_ref[...] = -ADAM_LR * (m_hat / (jnp.sqrt(v_hat) + ADAM_EPS) + ADAM_WD * w_ref[...])
        nm_ref[...] = nm
        nv_ref[...] = nv

    blk = pl.BlockSpec((rb, C), lambda i: (i, 0))
    return pl.pallas_call(
        body, grid=(R // rb,),
        in_specs=[pl.BlockSpec((slots, rb, C), lambda i: (0, i, 0)), blk, blk, blk],
        out_specs=[blk] * 4, out_shape=[jax.ShapeDtypeStruct((R, C), F32)] * 4,
        name=name, compiler_params=_cparams(),
    )(stack, w, m, v)


def _pad_rows8(a):
    return jnp.concatenate([a, jnp.zeros((8 - a.shape[0], a.shape[1]), a.dtype)], axis=0) if a.shape[0] < 8 else a


def _pack_small(norm_w, qnw, knw, sinks, a_log, dt_bias, onw, extra):
    z = lambda n: jnp.zeros((1, n), F32)
    row = jnp.concatenate([norm_w, qnw, knw, sinks, a_log, dt_bias, z(80), onw, extra, z(512)], axis=1)
    return row.reshape(16, 128)


def _unpack_small(p):
    row = p.reshape(1, 2048)
    cut = lambda a, n: row[:, a:a + n]
    return (cut(0, 1024), cut(1024, 64), cut(1088, 64), cut(1152, 16), cut(1168, 16), cut(1184, 16), cut(1280, 128),
            cut(1408, 128))


def _pack_rows(w_in_a, w_in_d, w_out_a, w_out_d, meta, conv, dn_norm):
    a = jnp.concatenate([w_in_a, w_in_d], axis=1)
    b = jnp.concatenate([w_out_a, w_out_d], axis=0)
    c = jnp.concatenate([meta, conv.reshape(16, 128), _pad_rows8(dn_norm)], axis=0)
    return a, b, c


def _unpack_rows(a, b, c):
    return (a[:, :288], a[:, 288:], b[:128], b[128:], c[:16], c[16:32].reshape(4, 512), c[32:33])


def _local_step(x, front, target, w):
    xn0, q, kv, gate = _attn_in_fwd(x, front, w["attn_norm_w"], w["attn_w_in"])
    o = _attn_core_fwd(q, kv, w["attn_sinks"], w["attn_q_norm_w"], w["attn_k_norm_w"])
    h1 = _attn_out_fwd(o, gate, x, front, w["attn_w_out"])
    xn1, qkv, z, ba = _dn_in_fwd(h1, w["dn_norm_w"], w["dn_w_in"])
    qn, kn, v, bg, conv_out = _dn_conv_fwd(qkv, ba, w["dn_conv_w"], w["dn_a_log"], w["dn_dt_bias"])
    o_dn, ssave, *saved = _dn_scan_fwd(qn, kn, v, bg)
    dy, og_dn, loss = _dn_out_fwd(o_dn, z, h1, target, w["dn_w_out"], w["dn_o_norm_w"])

    g = {}
    do_dn, dz, g["dn_o_norm_w"] = _dn_out_bwd(dy, o_dn, z, w["dn_w_out"], w["dn_o_norm_w"])
    g["dn_w_out"] = _wgrad(og_dn, dy, 1024, "wgrad_dn_out")
    dqn, dkn, dv, dbg = _dn_scan_bwd(do_dn, qn, kn, v, bg, ssave, saved)
    dqkv, dba, g["dn_conv_w"], g["dn_a_log"], g["dn_dt_bias"] = _dn_conv_bwd(
        dqn, dkn, dv, dbg, qkv, conv_out, ba, w["dn_conv_w"], w["dn_a_log"], w["dn_dt_bias"])
    dh1, g["dn_norm_w"] = _dn_in_bwd(dqkv, dz, dba, h1, dy, w["dn_norm_w"], w["dn_w_in"])
    g["dn_w_in"] = jnp.concatenate([_wgrad(xn1, dqkv, 1024, "wgrad_dn_qkv"), _wgrad(xn1, dz, 1024, "wgrad_dn_z"),
                                    _wgrad(xn1, dba, 32, "wgrad_dn_ba")], axis=1)
    do, dgate, g["attn_w_out"] = _attn_out_bwd(dh1, o, gate, w["attn_w_out"])
    dq, dkv, g["attn_sinks"], g["attn_q_norm_w"], g["attn_k_norm_w"] = _attn_core_bwd(
        do, q, kv, w["attn_sinks"], w["attn_q_norm_w"], w["attn_k_norm_w"])
    grad_x, dfront, g["attn_norm_w"] = _attn_in_bwd(dq, dkv, dgate, x, front, dh1, w["attn_norm_w"], w["attn_w_in"])
    g["meta_tokens"] = dfront[FRONT_PAD:]
    g["attn_w_in"] = jnp.concatenate([_wgrad(xn0, dq, 1024, "wgrad_attn_q"), _wgrad(xn0, dkv, 256, "wgrad_attn_kv"),
                                      _wgrad(xn0, dgate, 1024, "wgrad_attn_gate")], axis=1)
    return loss, grad_x, g


WEIGHTS = ['meta_tokens', 'attn_norm_w', 'attn_w_in', 'attn_q_norm_w', 'attn_k_norm_w', 'attn_sinks', 'attn_w_out',
           'dn_norm_w', 'dn_w_in', 'dn_conv_w', 'dn_a_log', 'dn_dt_bias', 'dn_o_norm_w', 'dn_w_out']
SMALL = ['attn_norm_w', 'attn_q_norm_w', 'attn_k_norm_w', 'attn_sinks', 'dn_a_log', 'dn_dt_bias', 'dn_o_norm_w']


def kernel(x, meta_tokens, attn_norm_w, attn_w_in, attn_q_norm_w, attn_k_norm_w, attn_sinks, attn_w_out, dn_norm_w, dn_w_in, dn_conv_w, dn_a_log, dn_dt_bias, dn_o_norm_w, dn_w_out, loss_target, m_meta_tokens, m_attn_norm_w, m_attn_w_in, m_attn_q_norm_w, m_attn_k_norm_w, m_attn_sinks, m_attn_w_out, m_dn_norm_w, m_dn_w_in, m_dn_conv_w, m_dn_a_log, m_dn_dt_bias, m_dn_o_norm_w, m_dn_w_out, v_meta_tokens, v_attn_norm_w, v_attn_w_in, v_attn_q_norm_w, v_attn_k_norm_w, v_attn_sinks, v_attn_w_out, v_dn_norm_w, v_dn_w_in, v_dn_conv_w, v_dn_a_log, v_dn_dt_bias, v_dn_o_norm_w, v_dn_w_out):
    shard = dict(meta_tokens=meta_tokens, attn_norm_w=attn_norm_w, attn_w_in=attn_w_in[0], attn_q_norm_w=attn_q_norm_w,
                 attn_k_norm_w=attn_k_norm_w, attn_sinks=attn_sinks, attn_w_out=attn_w_out[0], dn_norm_w=dn_norm_w,
                 dn_w_in=dn_w_in[0], dn_conv_w=dn_conv_w[0], dn_a_log=dn_a_log, dn_dt_bias=dn_dt_bias,
                 dn_o_norm_w=dn_o_norm_w, dn_w_out=dn_w_out[0])
    mom_m = dict(meta_tokens=m_meta_tokens, attn_norm_w=m_attn_norm_w, attn_w_in=m_attn_w_in[0], attn_q_norm_w=m_attn_q_norm_w,
                 attn_k_norm_w=m_attn_k_norm_w, attn_sinks=m_attn_sinks, attn_w_out=m_attn_w_out[0], dn_norm_w=m_dn_norm_w,
                 dn_w_in=m_dn_w_in[0], dn_conv_w=m_dn_conv_w[0], dn_a_log=m_dn_a_log, dn_dt_bias=m_dn_dt_bias,
                 dn_o_norm_w=m_dn_o_norm_w, dn_w_out=m_dn_w_out[0])
    mom_v = dict(meta_tokens=v_meta_tokens, attn_norm_w=v_attn_norm_w, attn_w_in=v_attn_w_in[0], attn_q_norm_w=v_attn_q_norm_w,
                 attn_k_norm_w=v_attn_k_norm_w, attn_sinks=v_attn_sinks, attn_w_out=v_attn_w_out[0], dn_norm_w=v_dn_norm_w,
                 dn_w_in=v_dn_w_in[0], dn_conv_w=v_dn_conv_w[0], dn_a_log=v_dn_a_log, dn_dt_bias=v_dn_dt_bias,
                 dn_o_norm_w=v_dn_o_norm_w, dn_w_out=v_dn_w_out[0])

    def rows_of(d):
        return _pack_rows(d["attn_w_in"], d["dn_w_in"], d["attn_w_out"], d["dn_w_out"], d["meta_tokens"], d["dn_conv_w"],
                          d["dn_norm_w"])

    def small_of(d, extra):
        return _pack_small(*[d[k] for k in SMALL], extra)

    wa, wb, wc = rows_of(shard)
    ga, gb, gc = _gather_two_level([wa.astype(BF16), wb.astype(BF16), wc], "gather_weights")
    full = {k: shard[k] for k in SMALL}
    full["attn_w_in"] = ga[:, :, :288].transpose(1, 0, 2).reshape(1024, 2304)
    full["dn_w_in"] = ga[:, :, 288:].transpose(1, 0, 2).reshape(1024, 6176)
    full["attn_w_out"] = gb[:, :128].reshape(1024, 1024)
    full["dn_w_out"] = gb[:, 128:].reshape(2048, 1024)
    meta_full = gc[:, :16].transpose(1, 0, 2).reshape(N_META, 1024)
    full["dn_conv_w"] = gc[:, 16:32].reshape(N_DEV, 4, 512).transpose(1, 0, 2).reshape(4, 4096)
    full["dn_norm_w"] = gc[:, 32].reshape(1, 1024)

    front = jnp.concatenate([jnp.zeros((FRONT_PAD, D_MODEL), F32), meta_full], axis=0)
    loss, grad_x, g = _local_step(x[0], front, loss_target[0], full)
    grad_x = grad_x[None]

    pa = jnp.concatenate([g["attn_w_in"].reshape(1024, N_DEV, 288), g["dn_w_in"].reshape(1024, N_DEV, 772)],
                         axis=2).transpose(1, 0, 2)
    pb = jnp.concatenate([g["attn_w_out"].reshape(N_DEV, 128, 1024), g["dn_w_out"].reshape(N_DEV, 256, 1024)], axis=1)
    dn_norm8 = jnp.concatenate([g["dn_norm_w"].reshape(N_DEV, 1, 128), jnp.zeros((N_DEV, 7, 128), F32)], axis=1)
    pc = jnp.concatenate([g["meta_tokens"].reshape(N_META, N_DEV, 128).transpose(1, 0, 2),
                          g["dn_conv_w"].reshape(4, N_DEV, 512).transpose(1, 0, 2).reshape(N_DEV, 16, 128), dn_norm8], axis=1)
    ps = small_of(g, loss)
    c = lax.axis_index("c")
    by_core = lambda p: p.astype(BF16).reshape((N_DEV // 2, 2) + p.shape[1:]).swapaxes(0, 1)
    pa2, pb2 = by_core(pa), by_core(pb)
    ra, rb_ = _swap_with_sibling([pa2, pb2], "swap_grads")
    own = lambda p2: lax.dynamic_index_in_dim(p2, c, axis=0, keepdims=False)
    flat = lambda t: t.reshape((-1,) + t.shape[2:])
    sa = _pair_sum(flat(own(pa2)), flat(ra), "pair_sum_a").reshape(ra.shape)
    sb = _pair_sum(flat(own(pb2)), flat(rb_), "pair_sum_b").reshape(rb_.shape)
    xa, xb = _exchange_chips([sa, sb], "exchange_grads")
    xc, xs = _exchange([pc, ps], [True, False], "exchange_small")

    out = {}
    ma, mb, mc = rows_of(mom_m)
    va, vb, vc = rows_of(mom_v)
    ra = _adamw(xa, wa, ma, va, "adamw_a")
    rb = _adamw(xb, wb, mb, vb, "adamw_b")
    rc = _adamw(xc, wc, mc, vc, "adamw_c")
    zero = jnp.zeros((1, 128), F32)
    rs = _adamw(xs, small_of(shard, zero), small_of(mom_m, zero), small_of(mom_v, zero), "adamw_small")
    row_names = ["attn_w_in", "dn_w_in", "attn_w_out", "dn_w_out", "meta_tokens", "dn_conv_w", "dn_norm_w"]
    lead = {"attn_w_in", "dn_w_in", "attn_w_out", "dn_w_out", "dn_conv_w"}
    for kind in range(4):
        vals = dict(zip(row_names, _unpack_rows(ra[kind], rb[kind], rc[kind])))
        small = _unpack_small(rs[kind])
        vals.update(dict(zip(SMALL, small[:7])))
        if kind == 0:
            loss_total = small[7][0, 0]
        out[kind] = [vals[k][None] if k in lead else vals[k] for k in WEIGHTS]
    return (loss_total, grad_x, *out[0], *out[1], *out[2], *out[3])
```

```python
import functools
import math

import jax
import jax.numpy as jnp
from jax import lax
from jax.experimental import pallas as pl
from jax.experimental.pallas import tpu as pltpu

F32, BF16 = jnp.float32, jnp.bfloat16

D_MODEL = 1024
N_META = 16
NORM_EPS = 1e-6
ATTN_HEADS, ATTN_KV_HEADS, ATTN_GROUPS, ATTN_HD = 16, 2, 8, 64
ATTN_BLOCK = 128
ATTN_STEP_BLOCKS = 3
FRONT_PAD = ATTN_BLOCK - N_META
DN_HD, DN_K_HEADS, DN_V_HEADS = 128, 8, 16
DN_CHUNK = 128
TRI_BLOCK = 64
SCAN_CHUNKS = 3
SCAN_BWD_CHUNKS = 1
SCAN_FWD_GROUP = 16
SCAN_BWD_GROUP = 8
DN_KEY_W, DN_VAL_W = 1024, 2048
DN_CONV_W = 2 * DN_KEY_W + DN_VAL_W
DN_CONV_K = 4
N_DEV = 8
ROW_BLOCK = 384
WGRAD_ROWS = 1376
VMEM_LIMIT = 56 * 1024 * 1024
NEG = -1e30

ADAM_LR, ADAM_B1, ADAM_B2, ADAM_EPS, ADAM_WD, ADAM_STEP = 0.001, 0.9, 0.999, 1e-08, 0.01, 10

NT = (((1,), (1,)), ((), ()))
TN = (((0,), (0,)), ((), ()))


def _cparams(sem=("arbitrary",)):
    return pltpu.CompilerParams(dimension_semantics=sem, vmem_limit_bytes=VMEM_LIMIT)


def _rms(x, w):
    return x * lax.rsqrt(jnp.mean(x * x, axis=-1, keepdims=True) + NORM_EPS) * w


def _silu(x):
    return x * jax.nn.sigmoid(x)


def _softplus(x):
    return jnp.maximum(x, 0.0) + jnp.log(1.0 + jnp.exp(-jnp.abs(x)))


NN = (((1,), (0,)), ((), ()))


def _mm(a, b, dims):
    return lax.dot_general(a.astype(BF16), b.astype(BF16), dims, preferred_element_type=F32)


@functools.partial(jax.custom_vjp, nondiff_argnums=(2,))
def _bdot_vjp(a, b, dims):
    return _mm(a, b, dims)


def _bdot_fwd(a, b, dims):
    a16, b16 = a.astype(BF16), b.astype(BF16)
    return _mm(a16, b16, dims), (a16, b16, jnp.zeros((), a.dtype), jnp.zeros((), b.dtype))


def _bdot_bwd(dims, res, g):
    a16, b16, ta, tb = res
    g16 = g.astype(BF16)
    if dims == NN:
        da, db = _mm(g16, b16, NT), _mm(a16, g16, TN)
    elif dims == NT:
        da, db = _mm(g16, b16, NN), _mm(g16, a16, TN)
    else:
        da, db = _mm(b16, g16, NT), _mm(a16, g16, NN)
    return da.astype(ta.dtype), db.astype(tb.dtype)


_bdot_vjp.defvjp(_bdot_fwd, _bdot_bwd)


def _bdot(a, b, dims=NN):
    return _bdot_vjp(a, b, dims)


def _row_call(name, body, n_rows, rb, rows, consts, outs, accs=(), reverse=False, scratch=(), halos=()):
    n = n_rows // rb
    assert n * rb == n_rows
    idx = (lambda i: (n - 1 - i, 0)) if reverse else (lambda i: (i, 0))
    in_specs = [pl.BlockSpec((rb, a.shape[1]), idx) for a in rows]
    in_specs += [pl.BlockSpec((hr, a.shape[1]), fn) for a, hr, fn in halos]
    in_specs += [pl.BlockSpec(c.shape, functools.partial(lambda i, nd: (0,) * nd, nd=c.ndim)) for c in consts]
    out_specs = [pl.BlockSpec((rb, c), idx) for c, _ in outs]
    out_specs += [pl.BlockSpec(s, functools.partial(lambda i, nd: (0,) * nd, nd=len(s))) for s, _ in accs]
    out_shape = [jax.ShapeDtypeStruct((n_rows, c), dt) for c, dt in outs]
    out_shape += [jax.ShapeDtypeStruct(s, dt) for s, dt in accs]
    return pl.pallas_call(
        body, grid=(n,), in_specs=in_specs, out_specs=out_specs, out_shape=out_shape,
        scratch_shapes=list(scratch), name=name, compiler_params=_cparams(),
    )(*rows, *[a for a, _, _ in halos], *consts)


def _token_views(x):
    per = ROW_BLOCK // ATTN_BLOCK
    return [(x, ATTN_BLOCK, functools.partial(lambda i, k: (jnp.maximum(per * i - 1 + k, 0), 0), k=k)) for k in range(per)]


def _padded_block(i, front, views):
    first = jnp.where(i == 0, front, views[0][...]) if front is not None else views[0][...]
    return jnp.concatenate([first] + [v[...] for v in views[1:]], axis=0)


def _attn_in_fwd(x, front, norm_w, w_in):
    T = x.shape[0] + ATTN_BLOCK

    def body(xa_ref, xb_ref, xc_ref, front_ref, nw_ref, w_ref, xn_ref, q_ref, kv_ref, gate_ref):
        h = _padded_block(pl.program_id(0), front_ref[...], (xa_ref, xb_ref, xc_ref))
        xn = _rms(h, nw_ref[...]).astype(BF16)
        xn_ref[...] = xn
        q_ref[...] = jnp.dot(xn, w_ref[:, 0:1024], preferred_element_type=F32)
        kv_ref[...] = jnp.dot(xn, w_ref[:, 1024:1280], preferred_element_type=F32)
        gate_ref[...] = jnp.dot(xn, w_ref[:, 1280:2304], preferred_element_type=F32)

    return _row_call("attn_in_fwd", body, T, ROW_BLOCK, [], [front, norm_w, w_in],
                     [(1024, BF16), (1024, F32), (256, F32), (1024, F32)], halos=_token_views(x))


def _attn_bias(n, j):
    C, R = 2 * ATTN_BLOCK + N_META, ATTN_GROUPS * ATTN_BLOCK
    c = lax.broadcasted_iota(jnp.int32, (C, R), 0)
    r = lax.broadcasted_iota(jnp.int32, (C, R), 1)
    ql = r & (ATTN_BLOCK - 1)
    is_meta = c >= 2 * ATTN_BLOCK
    dist_band = ATTN_BLOCK + ql - c
    cmin = jnp.maximum(0, 2 * ATTN_BLOCK - ATTN_BLOCK * n)
    valid_band = (c >= cmin) & (dist_band >= 0) & (dist_band < ATTN_BLOCK)
    dist_meta = ATTN_BLOCK * n + ql - FRONT_PAD - (c - 2 * ATTN_BLOCK)
    valid = (is_meta & (dist_meta >= 0)) | (jnp.logical_not(is_meta) & valid_band)
    dist = jnp.minimum(jnp.where(is_meta, dist_meta, dist_band), ATTN_BLOCK).astype(F32)
    rr = lax.broadcasted_iota(jnp.int32, (1, R), 1)
    head = (rr >> 7).astype(F32) + float(ATTN_GROUPS * j + 1)
    slope = jnp.exp(head * (-0.5 * math.log(2.0)))
    return jnp.where(valid, slope * dist, -NEG)


def _attn_table_scratch():
    return [pltpu.VMEM((ATTN_STEP_BLOCKS, ATTN_KV_HEADS, 2 * ATTN_BLOCK + N_META, ATTN_GROUPS * ATTN_BLOCK), F32)]


def _attn_groups(q_t, k, v, sinkrow, qnw_col, knw, bias, late_norm=True):
    n = range(len(q_t))
    qn = [q_t[j] * (lax.rsqrt(jnp.mean(q_t[j] * q_t[j], axis=0, keepdims=True) + NORM_EPS) * (ATTN_HD ** -0.5)) * qnw_col
          for j in n]
    kn = [_rms(k[j], knw) for j in n]
    s = [_bdot(kn[j], qn[j]) - bias[j] for j in n]
    m = [lax.stop_gradient(jnp.maximum(jnp.max(s[j], axis=0, keepdims=True), sinkrow[j])) for j in n]
    e = [jnp.exp(s[j] - m[j]) for j in n]
    inv = [1.0 / (jnp.sum(e[j], axis=0, keepdims=True) + jnp.exp(sinkrow[j] - m[j])) for j in n]
    if late_norm:
        return [_bdot(v[j], e[j], TN) * inv[j] for j in n]
    return [_bdot(v[j], e[j] * inv[j], TN) for j in n]


def _sink_row(sinks_ref, j):
    rr = lax.broadcasted_iota(jnp.int32, (1, ATTN_GROUPS * ATTN_BLOCK), 1) >> 7
    row = jnp.zeros((1, ATTN_GROUPS * ATTN_BLOCK), F32)
    for hl in range(ATTN_GROUPS):
        row = jnp.where(rr == hl, sinks_ref[0, ATTN_GROUPS * j + hl], row)
    return row


def _heads_to_lanes(ref, b, j):
    rows = slice(ATTN_BLOCK * b, ATTN_BLOCK * (b + 1))
    return jnp.concatenate([ref[rows, ATTN_HD * h:ATTN_HD * (h + 1)].T
                            for h in range(ATTN_GROUPS * j, ATTN_GROUPS * (j + 1))], axis=1)


def _lanes_to_heads(ref, b, j, x_t):
    rows = slice(ATTN_BLOCK * b, ATTN_BLOCK * (b + 1))
    for hl in range(ATTN_GROUPS):
        h = ATTN_GROUPS * j + hl
        ref[rows, ATTN_HD * h:ATTN_HD * (h + 1)] = x_t[:, ATTN_BLOCK * hl:ATTN_BLOCK * (hl + 1)].T


def _attn_chains(sinks_ref, q_ref, kvc_ref, kvp_ref, kvm_ref, bias_ref):
    chains = [(b, j) for b in range(ATTN_STEP_BLOCKS) for j in range(ATTN_KV_HEADS)]
    q_t, ks, vs, sinkrows, biases = [], [], [], [], []
    for b, j in chains:
        rows = slice(ATTN_BLOCK * b, ATTN_BLOCK * (b + 1))
        prev = kvp_ref if b == 0 else kvc_ref
        prows = slice(0, ATTN_BLOCK) if b == 0 else slice(ATTN_BLOCK * (b - 1), ATTN_BLOCK * b)
        ksl = slice(ATTN_HD * j, ATTN_HD * (j + 1))
        vsl = slice(128 + ATTN_HD * j, 128 + ATTN_HD * (j + 1))
        ks.append(jnp.concatenate([prev[prows, ksl], kvc_ref[rows, ksl], kvm_ref[FRONT_PAD:, ksl]], axis=0))
        vs.append(jnp.concatenate([prev[prows, vsl], kvc_ref[rows, vsl], kvm_ref[FRONT_PAD:, vsl]], axis=0))
        q_t.append(_heads_to_lanes(q_ref, b, j))
        sinkrows.append(_sink_row(sinks_ref, j))
        biases.append(bias_ref[b, j])
    return chains, q_t, ks, vs, sinkrows, biases


def _attn_core_fwd(q, kv, sinks, qnw, knw):
    T = q.shape[0]
    nb = T // ATTN_BLOCK
    nbs = ATTN_STEP_BLOCKS
    assert nb % nbs == 0
    rows_all = nbs * ATTN_BLOCK

    def body(sinks_ref, q_ref, kvc_ref, kvp_ref, kvm_ref, qnw_ref, knw_ref, o_ref, bias_ref):
        i = pl.program_id(0)

        @pl.when(i <= 1)
        def _():
            for b in range(nbs):
                for j in range(ATTN_KV_HEADS):
                    bias_ref[b, j] = _attn_bias(nbs * i + b, j)
        chains, q_t, ks, vs, sinkrows, biases = _attn_chains(sinks_ref, q_ref, kvc_ref, kvp_ref, kvm_ref, bias_ref)
        o_t = _attn_groups(q_t, ks, vs, sinkrows, qnw_ref[...], knw_ref[...], biases)
        for (b, j), o in zip(chains, o_t):
            _lanes_to_heads(o_ref, b, j, o)

    return pl.pallas_call(
        body, grid=(nb // nbs,),
        in_specs=[pl.BlockSpec(memory_space=pltpu.SMEM),
                  pl.BlockSpec((rows_all, 1024), lambda i: (i, 0)),
                  pl.BlockSpec((rows_all, 256), lambda i: (i, 0)),
                  pl.BlockSpec((ATTN_BLOCK, 256), lambda i: (jnp.maximum(nbs * i - 1, 0), 0)),
                  pl.BlockSpec((ATTN_BLOCK, 256), lambda i: (0, 0)),
                  pl.BlockSpec((ATTN_HD, 1), lambda i: (0, 0)),
                  pl.BlockSpec((1, ATTN_HD), lambda i: (0, 0))],
        out_specs=pl.BlockSpec((rows_all, 1024), lambda i: (i, 0)),
        out_shape=jax.ShapeDtypeStruct((T, 1024), F32),
        scratch_shapes=_attn_table_scratch(),
        name="attn_core_fwd", compiler_params=_cparams(),
    )(sinks, q, kv, kv, kv, qnw.reshape(ATTN_HD, 1), knw)


def _attn_out_fwd(o, gate, x, front, w_out):
    T = o.shape[0]

    def body(o_ref, g_ref, xa_ref, xb_ref, xc_ref, front_ref, w_ref, h1_ref):
        h = _padded_block(pl.program_id(0), front_ref[...], (xa_ref, xb_ref, xc_ref))
        og = o_ref[...] * _silu(g_ref[...])
        h1_ref[...] = h + _bdot(og, w_ref[...])

    return _row_call("attn_out_fwd", body, T, ROW_BLOCK, [o, gate], [front, w_out], [(1024, F32)], halos=_token_views(x))[0]


def _wgrad(xn, du, cg, name):
    T, kdim = xn.shape
    cdim = du.shape[1]
    rows = WGRAD_ROWS if T % WGRAD_ROWS == 0 else ROW_BLOCK
    nr, nc = T // rows, cdim // cg
    assert nc * cg == cdim

    def body(x_ref, du_ref, dw_ref):
        @pl.when(pl.program_id(1) == 0)
        def _():
            dw_ref[...] = jnp.zeros_like(dw_ref)
        dw_ref[...] += _bdot(x_ref[...], du_ref[...], TN)

    return pl.pallas_call(
        body, grid=(nc, nr),
        in_specs=[pl.BlockSpec((rows, kdim), lambda j, i: (i, 0)),
                  pl.BlockSpec((rows, cg), lambda j, i: (i, j))],
        out_specs=pl.BlockSpec((kdim, cg), lambda j, i: (0, j)),
        out_shape=jax.ShapeDtypeStruct((kdim, cdim), F32),
        name=name, compiler_params=_cparams(("arbitrary", "arbitrary")),
    )(xn, du)


def _attn_out_bwd(dh1, o, gate, w_out):
    T = o.shape[0]

    def body(dh_ref, o_ref, g_ref, w_ref, do_ref, dg_ref, dw_ref):
        @pl.when(pl.program_id(0) == 0)
        def _():
            dw_ref[...] = jnp.zeros_like(dw_ref)
        dh = dh_ref[...]
        dog = _bdot(dh, w_ref[...], NT)
        og, vjp = jax.vjp(lambda o_, g_: o_ * _silu(g_), o_ref[...], g_ref[...])
        do, dg = vjp(dog)
        do_ref[...] = do
        dg_ref[...] = dg
        dw_ref[...] += _bdot(og, dh, TN)

    return _row_call("attn_out_bwd", body, T, ROW_BLOCK, [dh1, o, gate], [w_out],
                     [(1024, F32), (1024, F32)], [((1024, 1024), F32)])


def _attn_core_bwd(do, q, kv, sinks, qnw, knw):
    T = q.shape[0]
    nbs = ATTN_STEP_BLOCKS
    ns = T // (nbs * ATTN_BLOCK)
    rows_all = nbs * ATTN_BLOCK
    rev = lambda i: ns - 1 - i

    def body(sinks_ref, do_ref, q_ref, kvc_ref, kvp_ref, kvm_ref, qnw_ref, knw_ref,
             dq_ref, dkv_ref, dsinks_ref, dqnw_ref, dknw_ref, carry_ref, meta_ref, bias_ref):
        step = pl.program_id(0)
        i = rev(step)

        @pl.when((step == 0) | (i == 0))
        def _():
            for b in range(nbs):
                for j in range(ATTN_KV_HEADS):
                    bias_ref[b, j] = _attn_bias(nbs * i + b, j)

        @pl.when(step == 0)
        def _():
            carry_ref[...] = jnp.zeros_like(carry_ref)
            meta_ref[...] = jnp.zeros_like(meta_ref)
            dsinks_ref[...] = jnp.zeros_like(dsinks_ref)
            dqnw_ref[...] = jnp.zeros_like(dqnw_ref)
            dknw_ref[...] = jnp.zeros_like(dknw_ref)

        lane16 = lax.broadcasted_iota(jnp.int32, (1, ATTN_HEADS), 1)
        dsinks = jnp.zeros((1, ATTN_HEADS), F32)
        chains, q_t, ks, vs, sinkrows, biases = _attn_chains(sinks_ref, q_ref, kvc_ref, kvp_ref, kvm_ref, bias_ref)
        fn = functools.partial(_attn_groups, bias=biases, late_norm=False)
        _, vjp = jax.vjp(fn, q_t, ks, vs, sinkrows, qnw_ref[...], knw_ref[...])
        dq_t, dks, dvs, dsr, dqn, dkn = vjp([_heads_to_lanes(do_ref, b, j) for b, j in chains])
        dqnw_ref[...] += dqn
        dknw_ref[...] += dkn
        part = {}
        for c, (b, j) in enumerate(chains):
            _lanes_to_heads(dq_ref, b, j, dq_t[c])
            for hl in range(ATTN_GROUPS):
                dsinks = dsinks + jnp.where(lane16 == ATTN_GROUPS * j + hl,
                                            jnp.sum(dsr[c][:, ATTN_BLOCK * hl:ATTN_BLOCK * (hl + 1)]), 0.0)
            part[b, j] = (dks[c], dvs[c])
        for j in range(ATTN_KV_HEADS):
            for kind, sl in ((0, slice(ATTN_HD * j, ATTN_HD * (j + 1))), (1, slice(128 + ATTN_HD * j, 128 + ATTN_HD * (j + 1)))):
                for b in range(nbs):
                    d = part[b, j][kind]
                    nxt = part[b + 1, j][kind][0:ATTN_BLOCK, :] if b + 1 < nbs else carry_ref[:, sl]
                    dkv_ref[ATTN_BLOCK * b:ATTN_BLOCK * (b + 1), sl] = d[ATTN_BLOCK:2 * ATTN_BLOCK, :] + nxt
                    meta_ref[:, sl] += d[2 * ATTN_BLOCK:, :]
                carry_ref[:, sl] = part[0, j][kind][0:ATTN_BLOCK, :]
        dsinks_ref[...] += dsinks

        @pl.when(i == 0)
        def _():
            dkv_ref[FRONT_PAD:ATTN_BLOCK, :] += meta_ref[...]

    dq, dkv, dsinks, dqnw, dknw = pl.pallas_call(
        body, grid=(ns,),
        in_specs=[pl.BlockSpec(memory_space=pltpu.SMEM),
                  pl.BlockSpec((rows_all, 1024), lambda i: (rev(i), 0)),
                  pl.BlockSpec((rows_all, 1024), lambda i: (rev(i), 0)),
                  pl.BlockSpec((rows_all, 256), lambda i: (rev(i), 0)),
                  pl.BlockSpec((ATTN_BLOCK, 256), lambda i: (jnp.maximum(nbs * rev(i) - 1, 0), 0)),
                  pl.BlockSpec((ATTN_BLOCK, 256), lambda i: (0, 0)),
                  pl.BlockSpec((ATTN_HD, 1), lambda i: (0, 0)),
                  pl.BlockSpec((1, ATTN_HD), lambda i: (0, 0))],
        out_specs=[pl.BlockSpec((rows_all, 1024), lambda i: (rev(i), 0)),
                   pl.BlockSpec((rows_all, 256), lambda i: (rev(i), 0)),
                   pl.BlockSpec((1, ATTN_HEADS), lambda i: (0, 0)),
                   pl.BlockSpec((ATTN_HD, 1), lambda i: (0, 0)),
                   pl.BlockSpec((1, ATTN_HD), lambda i: (0, 0))],
        out_shape=[jax.ShapeDtypeStruct((T, 1024), F32), jax.ShapeDtypeStruct((T, 256), F32),
                   jax.ShapeDtypeStruct((1, ATTN_HEADS), F32), jax.ShapeDtypeStruct((ATTN_HD, 1), F32),
                   jax.ShapeDtypeStruct((1, ATTN_HD), F32)],
        scratch_shapes=[pltpu.VMEM((ATTN_BLOCK, 256), F32), pltpu.VMEM((N_META, 256), F32)] + _attn_table_scratch(),
        name="attn_core_bwd", compiler_params=_cparams(),
    )(sinks, do, q, kv, kv, kv, qnw.reshape(ATTN_HD, 1), knw)
    return dq, dkv, dsinks, dqnw.reshape(1, ATTN_HD), dknw


def _attn_in_bwd(dq, dkv, dgate, x, front, dh1, norm_w, w_in):
    T = dq.shape[0]
    n = T // ROW_BLOCK
    per = ROW_BLOCK // ATTN_BLOCK
    assert n >= 3

    def body(dq_ref, dkv_ref, dg_ref, dh1_ref, xa_ref, xb_ref, xc_ref, front_ref, nw_ref, w_ref,
             gx_ref, dfront_ref, dnw_ref, buf_ref, sems):
        i = pl.program_id(0)
        slot = i % 2

        def piece(step, k, s):
            return pltpu.make_async_copy(buf_ref.at[s, pl.ds(ATTN_BLOCK * k, ATTN_BLOCK)],
                                         gx_ref.at[pl.ds((per * step - 1 + k) * ATTN_BLOCK, ATTN_BLOCK)], sems.at[s, k])

        @pl.when(i == 0)
        def _():
            dnw_ref[...] = jnp.zeros_like(dnw_ref)
        for k in range(per):
            @pl.when((i >= 2) & ((k > 0) | (i > 2)))
            def _():
                piece(i - 2, k, slot).wait()
        h = _padded_block(i, front_ref[...], (xa_ref, xb_ref, xc_ref))
        dxn = (_bdot(dq_ref[...], w_ref[:, 0:1024], NT) + _bdot(dkv_ref[...], w_ref[:, 1024:1280], NT)
               + _bdot(dg_ref[...], w_ref[:, 1280:2304], NT))
        _, vjp = jax.vjp(_rms, h, nw_ref[...])
        dh, dnw = vjp(dxn)
        dnw_ref[...] += dnw
        buf_ref[slot] = dh1_ref[...] + dh

        @pl.when(i == 0)
        def _():
            dfront_ref[...] = buf_ref[0, 0:ATTN_BLOCK, :]
        for k in range(per):
            @pl.when((k > 0) | (i > 0))
            def _():
                piece(i, k, slot).start()

        @pl.when(i == n - 1)
        def _():
            for k in range(per):
                piece(i, k, slot).wait()
                piece(i - 1, k, 1 - slot).wait()

    idx = lambda i: (i, 0)
    const = lambda a: pl.BlockSpec(a.shape, functools.partial(lambda i, nd: (0,) * nd, nd=a.ndim))
    rows = [dq, dkv, dgate, dh1]
    views = _token_views(x)
    return pl.pallas_call(
        body, grid=(n,),
        in_specs=[pl.BlockSpec((ROW_BLOCK, a.shape[1]), idx) for a in rows]
        + [pl.BlockSpec((hr, a.shape[1]), fn) for a, hr, fn in views] + [const(front), const(norm_w), const(w_in)],
        out_specs=[pl.BlockSpec(memory_space=pltpu.HBM), pl.BlockSpec((ATTN_BLOCK, D_MODEL), lambda i: (0, 0)),
                   pl.BlockSpec((1, D_MODEL), lambda i: (0, 0))],
        out_shape=[jax.ShapeDtypeStruct(x.shape, F32), jax.ShapeDtypeStruct((ATTN_BLOCK, D_MODEL), F32),
                   jax.ShapeDtypeStruct((1, D_MODEL), F32)],
        scratch_shapes=[pltpu.VMEM((2, ROW_BLOCK, D_MODEL), F32), pltpu.SemaphoreType.DMA((2, per))],
        name="attn_in_bwd", compiler_params=_cparams(),
    )(*rows, *[a for a, _, _ in views], front, norm_w, w_in)


def _dn_in_fwd(h1, norm_w, w_in):
    T = h1.shape[0]

    def body(h_ref, nw_ref, w_ref, xn_ref, qkv_ref, z_ref, ba_ref):
        xn = _rms(h_ref[...], nw_ref[...]).astype(BF16)
        xn_ref[...] = xn
        qkv_ref[...] = jnp.dot(xn, w_ref[:, 0:4096], preferred_element_type=F32)
        z_ref[...] = jnp.dot(xn, w_ref[:, 4096:6144], preferred_element_type=F32)
        ba_ref[...] = jnp.dot(xn, w_ref[:, 6144:6176], preferred_element_type=F32)

    return _row_call("dn_in_fwd", body, T, ROW_BLOCK, [h1], [norm_w, w_in],
                     [(1024, BF16), (4096, F32), (2048, F32), (32, F32)])


def _shift_down(cur, prev8, s):
    i8 = lax.broadcasted_iota(jnp.int32, (8, cur.shape[1]), 0)
    r = pltpu.roll(cur, s, 0)
    head = jnp.where(i8 < s, pltpu.roll(prev8, s, 0), r[0:8])
    return jnp.concatenate([head, r[8:]], axis=0)


def _shift_up(cur, next8, s):
    n = cur.shape[0]
    i8 = lax.broadcasted_iota(jnp.int32, (8, cur.shape[1]), 0)
    r = pltpu.roll(cur, n - s, 0)
    tail = jnp.where(i8 >= 8 - s, pltpu.roll(next8, 8 - s, 0), r[n - 8:])
    return jnp.concatenate([r[:n - 8], tail], axis=0)


def _conv_taps(cur, prev8):
    return [cur] + [_shift_down(cur, prev8, s) for s in range(1, DN_CONV_K)]


def _conv_tile(taps, w):
    out = w[3:4, :] * taps[0]
    for s in range(1, DN_CONV_K):
        out = out + w[3 - s:4 - s, :] * taps[s]
    return out


def _l2n(a, scale):
    return a * (lax.rsqrt(jnp.sum(a * a, axis=-1, keepdims=True) + NORM_EPS) * scale)


def _dn_post_tile(c, t):
    a = _silu(c)
    if t < DN_K_HEADS:
        return _l2n(a, DN_HD ** -0.5)
    if t < 2 * DN_K_HEADS:
        return _l2n(a, 1.0)
    return a


def _dn_beta_g(ba, a_log, dt_bias, live):
    beta = jax.nn.sigmoid(ba[:, 0:DN_V_HEADS]) * live
    g = -jnp.exp(a_log) * _softplus(ba[:, DN_V_HEADS:] + dt_bias) * live
    return beta, g


def _live_rows(i, rb):
    rows = i * rb + lax.broadcasted_iota(jnp.int32, (rb, 1), 0)
    return (rows >= FRONT_PAD).astype(F32)


def _halo_spec_args(x, rb):
    per = rb // 8
    return (x, 8, lambda i: (jnp.maximum(i * per - 1, 0), 0))


def _dn_conv_fwd(qkv, ba, conv_w, a_log, dt_bias):
    T = qkv.shape[0]

    def body(x_ref, ba_ref, halo_ref, cw_ref, al_ref, dtb_ref, q_ref, k_ref, v_ref, bg_ref, c_ref):
        i = pl.program_id(0)
        first = (i > 0).astype(F32)
        for t in range(DN_CONV_W // 128):
            cols = slice(128 * t, 128 * (t + 1))
            c = _conv_tile(_conv_taps(x_ref[:, cols], halo_ref[:, cols] * first), cw_ref[:, cols])
            c_ref[:, cols] = c.astype(BF16)
            out = _dn_post_tile(c, t)
            if t < DN_K_HEADS:
                q_ref[:, cols] = out
            elif t < 2 * DN_K_HEADS:
                k_ref[:, 128 * (t - 8):128 * (t - 7)] = out
            else:
                v_ref[:, 128 * (t - 16):128 * (t - 15)] = out
        beta, g = _dn_beta_g(ba_ref[...], al_ref[...], dtb_ref[...], _live_rows(i, ROW_BLOCK))
        bg_ref[:, 0:DN_V_HEADS] = beta
        bg_ref[:, DN_V_HEADS:] = g

    return _row_call("dn_conv_fwd", body, T, ROW_BLOCK, [qkv, ba], [conv_w, a_log, dt_bias],
                     [(1024, F32), (1024, F32), (2048, F32), (32, F32), (4096, BF16)], halos=[_halo_spec_args(qkv, ROW_BLOCK)])


def _chunk_masks():
    r = lax.broadcasted_iota(jnp.int32, (DN_CHUNK, DN_CHUNK), 0)
    c = lax.broadcasted_iota(jnp.int32, (DN_CHUNK, DN_CHUNK), 1)
    return r >= c, r > c, r == c, r <= c


def _tri_inv_block(x):
    B = TRI_BLOCK
    n = range(len(x))
    r_, c_ = lax.broadcasted_iota(jnp.int32, (B, B), 0), lax.broadcasted_iota(jnp.int32, (B, B), 1)
    ainv = [jnp.where(r_ == c_, 1.0, 0.0) + x[h] for h in n]
    p = [_bdot(x[h], x[h]) for h in n]
    for _ in range(B.bit_length() - 3):
        r = [_bdot(jnp.concatenate([p[h], ainv[h]], axis=0), p[h]) for h in n]
        ainv = [ainv[h] + r[h][B:] for h in n]
        p = [r[h][:B] for h in n]
    return [ainv[h] + _bdot(ainv[h], p[h]) for h in n]


def _tri_inv(x):
    B = TRI_BLOCK
    assert DN_CHUNK == 2 * B
    n = len(x)
    diag = _tri_inv_block([x[h][:B, :B] for h in range(n)] + [x[h][B:, B:] for h in range(n)])
    a11, a22 = diag[:n], diag[n:]
    a21 = [_bdot(_bdot(a22[h], x[h][B:, :B]), a11[h]) for h in range(n)]
    zero = jnp.zeros((B, B), F32)
    return [jnp.concatenate([jnp.concatenate([a11[h], zero], axis=1), jnp.concatenate([a21[h], a22[h]], axis=1)], axis=0)
            for h in range(n)]


@jax.custom_vjp
def _tri_inv_known(x, a):
    return a


def _tri_inv_known_fwd(x, a):
    return a, a


def _tri_inv_known_bwd(a, da):
    return [_bdot(_bdot(a[h], da[h], TN), a[h], NT) for h in range(len(a))], [jnp.zeros_like(t) for t in a]


_tri_inv_known.defvjp(_tri_inv_known_fwd, _tri_inv_known_bwd)


@jax.custom_vjp
def _known(computed, value):
    return value


def _known_fwd(computed, value):
    return value, None


def _known_bwd(_, g):
    return g, jax.tree.map(jnp.zeros_like, g)


_known.defvjp(_known_fwd, _known_bwd)


def _dn_chunk_step(S, q, k, v, beta, g, masks, known=None):
    causal, strict, eye, upper = masks
    C, W = DN_CHUNK, DN_HD
    heads = range(len(v))
    k_t = [k[j].T for j in range(len(k))]
    qk_kk = [_bdot(jnp.concatenate([q[j], k[j]], axis=0), k_t[j]) for j in range(len(q))]
    if known is not None:
        qk_kk = _known(qk_kk, known["qk_kk"])
    g_b = [jnp.broadcast_to(g[h], (C, C)) for h in heads]
    beta_b = [jnp.broadcast_to(beta[h], (C, W)) for h in heads]
    g_row = [jnp.sum(jnp.where(eye, g_b[h], 0.0), axis=0, keepdims=True) for h in heads]
    gc_col = [jnp.sum(jnp.where(causal, g_row[h], 0.0), axis=1, keepdims=True) for h in heads]
    gc_row = [jnp.sum(jnp.where(upper, g_b[h], 0.0), axis=0, keepdims=True) for h in heads]
    g_last = [jnp.sum(g_row[h], axis=1, keepdims=True) for h in heads]
    gc_b = [jnp.broadcast_to(gc_col[h], (C, W)) for h in heads]
    decay = [jnp.exp(jnp.where(causal, gc_b[h][:, :C] - gc_row[h], NEG)) for h in heads]
    eg_b = [jnp.exp(gc_b[h]) for h in heads]
    x = [jnp.where(strict, qk_kk[h // 2][C:] * beta_b[h][:, :C] * decay[h], 0.0) * -1.0 for h in heads]
    ainv = _tri_inv(x) if known is None else _tri_inv_known(x, known["inv"])
    uw = [_bdot(ainv[h], jnp.concatenate([v[h] * beta_b[h], k[h // 2] * (beta_b[h] * eg_b[h])], axis=1)) for h in heads]
    if known is not None:
        uw = _known(uw, known["uw"])
    q_eg = [q[h // 2] * eg_b[h] for h in heads]
    attn = [qk_kk[h // 2][:C] * decay[h] for h in heads]
    k_st = [k_t[h // 2] * jnp.exp(g_last[h] - gc_row[h]) for h in heads]
    s_dec = [jnp.exp(g_last[h]) for h in heads]
    prep = (uw, q_eg, attn, k_st, s_dec)
    if S is None:
        return prep, dict(inv=ainv, uw=uw, qk_kk=qk_kk)
    s_new, o, _ = _dn_chunk_tail(S, prep, None if known is None else known["v_new"])
    return s_new, o


def _dn_chunk_tail(S, prep, known_v_new=None):
    uw, q_eg, attn, k_st, s_dec = prep
    C, W = DN_CHUNK, DN_HD
    heads = range(len(uw))
    ws_qs = [_bdot(jnp.concatenate([uw[h][:, W:], q_eg[h]], axis=0), S[h]) for h in heads]
    v_new = [uw[h][:, :W] - ws_qs[h][:C] for h in heads]
    if known_v_new is not None:
        v_new = _known(v_new, known_v_new)
    o = [ws_qs[h][C:] + _bdot(attn[h], v_new[h]) for h in heads]
    s_new = [S[h] * s_dec[h] + _bdot(k_st[h], v_new[h]) for h in heads]
    return s_new, o, v_new


def _dn_chunk_tiles(q_ref, k_ref, v_ref, bg_ref, c, first, count):
    rows = slice(DN_CHUNK * c, DN_CHUNK * (c + 1))
    q = [q_ref[rows, 128 * j:128 * (j + 1)] for j in range(first // 2, (first + count) // 2)]
    k = [k_ref[rows, 128 * j:128 * (j + 1)] for j in range(first // 2, (first + count) // 2)]
    v = [v_ref[rows, 128 * h:128 * (h + 1)] for h in range(first, first + count)]
    beta = [bg_ref[rows, h:h + 1] for h in range(first, first + count)]
    g = [bg_ref[rows, DN_V_HEADS + h:DN_V_HEADS + h + 1] for h in range(first, first + count)]
    return q, k, v, beta, g


def _dn_scan_fwd(qn, kn, v, bg):
    T = qn.shape[0]
    nc = T // DN_CHUNK
    rows = SCAN_CHUNKS * DN_CHUNK
    assert nc % SCAN_CHUNKS == 0

    def body(q_ref, k_ref, v_ref, bg_ref, o_ref, ssave_ref, inv_ref, uw_ref, vn_ref, qk_ref, s_ref):
        @pl.when(pl.program_id(0) == 0)
        def _():
            s_ref[...] = jnp.zeros_like(s_ref)
        masks = _chunk_masks()
        for first in range(0, DN_V_HEADS, SCAN_FWD_GROUP):
            heads = range(first, first + SCAN_FWD_GROUP)
            preps = [_dn_chunk_step(None, *_dn_chunk_tiles(q_ref, k_ref, v_ref, bg_ref, c, first, SCAN_FWD_GROUP), masks)
                     for c in range(SCAN_CHUNKS)]
            state = [s_ref[h] for h in heads]
            for c, (prep, saved) in enumerate(preps):
                for i, h in enumerate(heads):
                    ssave_ref[c, h] = state[i]
                    inv_ref[c, h] = saved["inv"][i].astype(BF16)
                    uw_ref[c, h] = saved["uw"][i].astype(BF16)
                for i, j in enumerate(range(first // 2, (first + SCAN_FWD_GROUP) // 2)):
                    qk_ref[c, j] = saved["qk_kk"][i].astype(BF16)
                state, o, v_new = _dn_chunk_tail(state, prep)
                for i, h in enumerate(heads):
                    o_ref[DN_CHUNK * c:DN_CHUNK * (c + 1), 128 * h:128 * (h + 1)] = o[i]
                    vn_ref[c, h] = v_new[i].astype(BF16)
            for i, h in enumerate(heads):
                s_ref[h] = state[i]

    return pl.pallas_call(
        body, grid=(nc // SCAN_CHUNKS,),
        in_specs=[pl.BlockSpec((rows, 1024), lambda i: (i, 0)),
                  pl.BlockSpec((rows, 1024), lambda i: (i, 0)),
                  pl.BlockSpec((rows, 2048), lambda i: (i, 0)),
                  pl.BlockSpec((rows, 32), lambda i: (i, 0))],
        out_specs=[pl.BlockSpec((rows, 2048), lambda i: (i, 0)),
                   pl.BlockSpec((SCAN_CHUNKS, DN_V_HEADS, DN_HD, DN_HD), lambda i: (i, 0, 0, 0)),
                   pl.BlockSpec((SCAN_CHUNKS, DN_V_HEADS, DN_CHUNK, DN_CHUNK), lambda i: (i, 0, 0, 0)),
                   pl.BlockSpec((SCAN_CHUNKS, DN_V_HEADS, DN_CHUNK, 2 * DN_HD), lambda i: (i, 0, 0, 0)),
                   pl.BlockSpec((SCAN_CHUNKS, DN_V_HEADS, DN_CHUNK, DN_HD), lambda i: (i, 0, 0, 0)),
                   pl.BlockSpec((SCAN_CHUNKS, DN_K_HEADS, 2 * DN_CHUNK, DN_CHUNK), lambda i: (i, 0, 0, 0))],
        out_shape=[jax.ShapeDtypeStruct((T, 2048), F32),
                   jax.ShapeDtypeStruct((nc, DN_V_HEADS, DN_HD, DN_HD), F32),
                   jax.ShapeDtypeStruct((nc, DN_V_HEADS, DN_CHUNK, DN_CHUNK), BF16),
                   jax.ShapeDtypeStruct((nc, DN_V_HEADS, DN_CHUNK, 2 * DN_HD), BF16),
                   jax.ShapeDtypeStruct((nc, DN_V_HEADS, DN_CHUNK, DN_HD), BF16),
                   jax.ShapeDtypeStruct((nc, DN_K_HEADS, 2 * DN_CHUNK, DN_CHUNK), BF16)],
        scratch_shapes=[pltpu.VMEM((DN_V_HEADS, DN_HD, DN_HD), F32)],
        name="dn_scan_fwd", compiler_params=_cparams(),
    )(qn, kn, v, bg)


def _dn_gate_tile(o, z, onw):
    return _rms(o, onw) * _silu(z)


def _dn_out_fwd(o, z, h1, target, w_out, onw):
    T = o.shape[0]

    def body(o_ref, z_ref, h_ref, ta_ref, tb_ref, tc_ref, w_ref, onw_ref, dy_ref, og_ref, loss_ref):
        i = pl.program_id(0)

        @pl.when(i == 0)
        def _():
            loss_ref[...] = jnp.zeros_like(loss_ref)
        for h in range(DN_V_HEADS):
            cols = slice(128 * h, 128 * (h + 1))
            og_ref[:, cols] = _dn_gate_tile(o_ref[:, cols], z_ref[:, cols], onw_ref[...]).astype(BF16)
        y = h_ref[...] + jnp.dot(og_ref[...], w_ref[...], preferred_element_type=F32)
        rows = i * ROW_BLOCK + lax.broadcasted_iota(jnp.int32, (ROW_BLOCK, 1), 0)
        diff = jnp.where(rows >= FRONT_PAD + N_META, y - _padded_block(i, None, (ta_ref, tb_ref, tc_ref)), 0.0)
        dy_ref[...] = diff * (1.0 / D_MODEL)
        loss_ref[...] += jnp.sum(diff * diff) * (0.5 / D_MODEL)

    return _row_call("dn_out_fwd", body, T, ROW_BLOCK, [o, z, h1], [w_out, onw],
                     [(1024, F32), (2048, BF16)], [((1, 128), F32)], halos=_token_views(target))


def _dn_out_bwd(dy, o, z, w_out, onw):
    T = o.shape[0]

    def body(dy_ref, o_ref, z_ref, w_ref, onw_ref, do_ref, dz_ref, donw_ref, dog_ref):
        @pl.when(pl.program_id(0) == 0)
        def _():
            donw_ref[...] = jnp.zeros_like(donw_ref)
        dy = dy_ref[...].astype(BF16)
        donw = jnp.zeros((1, DN_HD), F32)
        for half in range(2):
            hcols = slice(1024 * half, 1024 * (half + 1))
            dog_ref[:, hcols] = lax.dot_general(dy, w_ref[hcols, :], NT, preferred_element_type=F32)
        for h in range(DN_V_HEADS):
            cols = slice(128 * h, 128 * (h + 1))
            _, vjp = jax.vjp(_dn_gate_tile, o_ref[:, cols], z_ref[:, cols], onw_ref[...])
            do, dz, dn = vjp(dog_ref[:, cols])
            do_ref[:, cols] = do
            dz_ref[:, cols] = dz
            donw = donw + dn
        donw_ref[...] += donw

    return _row_call("dn_out_bwd", body, T, ROW_BLOCK, [dy, o, z], [w_out, onw],
                     [(2048, F32), (2048, F32)], [((1, DN_HD), F32)], scratch=[pltpu.VMEM((ROW_BLOCK, 2048), F32)])


def _dn_scan_bwd(do, qn, kn, v, bg, ssave, saved):
    T = qn.shape[0]
    nc = T // DN_CHUNK
    chunks = SCAN_BWD_CHUNKS
    ns = nc // chunks
    rows = chunks * DN_CHUNK
    rev = lambda i: ns - 1 - i

    def body(do_ref, q_ref, k_ref, v_ref, bg_ref, ss_ref, inv_ref, uw_ref, vn_ref, qk_ref,
             dq_ref, dk_ref, dv_ref, dbg_ref, ds_ref):
        @pl.when(pl.program_id(0) == 0)
        def _():
            ds_ref[...] = jnp.zeros_like(ds_ref)
        lane32 = lax.broadcasted_iota(jnp.int32, (1, 2 * DN_V_HEADS), 1)
        masks = _chunk_masks()
        dbg = [jnp.zeros((DN_CHUNK, 2 * DN_V_HEADS), F32) for _ in range(chunks)]
        for first in range(0, DN_V_HEADS, SCAN_BWD_GROUP):
            heads = range(first, first + SCAN_BWD_GROUP)
            vjps = []
            for c in range(chunks):
                known = dict(inv=[inv_ref[c, h].astype(F32) for h in heads], uw=[uw_ref[c, h].astype(F32) for h in heads],
                             v_new=[vn_ref[c, h].astype(F32) for h in heads],
                             qk_kk=[qk_ref[c, j].astype(F32) for j in range(first // 2, (first + SCAN_BWD_GROUP) // 2)])
                fn = functools.partial(_dn_chunk_step, masks=masks, known=known)
                vjps.append(jax.vjp(fn, [ss_ref[c, h] for h in heads],
                                    *_dn_chunk_tiles(q_ref, k_ref, v_ref, bg_ref, c, first, SCAN_BWD_GROUP))[1])
            ds = [ds_ref[h] for h in heads]
            for c in reversed(range(chunks)):
                crows = slice(DN_CHUNK * c, DN_CHUNK * (c + 1))
                ds, dq, dk, dv, dbeta, dg = vjps[c]((ds, [do_ref[crows, 128 * h:128 * (h + 1)] for h in heads]))
                for i, h in enumerate(heads):
                    dv_ref[crows, 128 * h:128 * (h + 1)] = dv[i]
                    dbg[c] = dbg[c] + jnp.where(lane32 == h, dbeta[i], 0.0) + jnp.where(lane32 == DN_V_HEADS + h, dg[i], 0.0)
                for i, j in enumerate(range(first // 2, (first + SCAN_BWD_GROUP) // 2)):
                    dq_ref[crows, 128 * j:128 * (j + 1)] = dq[i]
                    dk_ref[crows, 128 * j:128 * (j + 1)] = dk[i]
            for i, h in enumerate(heads):
                ds_ref[h] = ds[i]
        for c in range(chunks):
            dbg_ref[DN_CHUNK * c:DN_CHUNK * (c + 1), :] = dbg[c]

    return pl.pallas_call(
        body, grid=(ns,),
        in_specs=[pl.BlockSpec((rows, 2048), lambda i: (rev(i), 0)),
                  pl.BlockSpec((rows, 1024), lambda i: (rev(i), 0)),
                  pl.BlockSpec((rows, 1024), lambda i: (rev(i), 0)),
                  pl.BlockSpec((rows, 2048), lambda i: (rev(i), 0)),
                  pl.BlockSpec((rows, 32), lambda i: (rev(i), 0)),
                  pl.BlockSpec((chunks, DN_V_HEADS, DN_HD, DN_HD), lambda i: (rev(i), 0, 0, 0)),
                  pl.BlockSpec((chunks, DN_V_HEADS, DN_CHUNK, DN_CHUNK), lambda i: (rev(i), 0, 0, 0)),
                  pl.BlockSpec((chunks, DN_V_HEADS, DN_CHUNK, 2 * DN_HD), lambda i: (rev(i), 0, 0, 0)),
                  pl.BlockSpec((chunks, DN_V_HEADS, DN_CHUNK, DN_HD), lambda i: (rev(i), 0, 0, 0)),
                  pl.BlockSpec((chunks, DN_K_HEADS, 2 * DN_CHUNK, DN_CHUNK), lambda i: (rev(i), 0, 0, 0))],
        out_specs=[pl.BlockSpec((rows, 1024), lambda i: (rev(i), 0)),
                   pl.BlockSpec((rows, 1024), lambda i: (rev(i), 0)),
                   pl.BlockSpec((rows, 2048), lambda i: (rev(i), 0)),
                   pl.BlockSpec((rows, 32), lambda i: (rev(i), 0))],
        out_shape=[jax.ShapeDtypeStruct((T, 1024), F32), jax.ShapeDtypeStruct((T, 1024), F32),
                   jax.ShapeDtypeStruct((T, 2048), F32), jax.ShapeDtypeStruct((T, 32), F32)],
        scratch_shapes=[pltpu.VMEM((DN_V_HEADS, DN_HD, DN_HD), F32)],
        name="dn_scan_bwd", compiler_params=_cparams(),
    )(do, qn, kn, v, bg, ssave, *saved)


def _dn_conv_bwd(dqn, dkn, dv, dbg, qkv, conv_out, ba, conv_w, a_log, dt_bias):
    T = qkv.shape[0]
    rb = ROW_BLOCK // 2
    nr = T // rb

    def body(dq_ref, dk_ref, dv_ref, dbg_ref, x_ref, c_ref, ba_ref, cw_ref, al_ref, dtb_ref,
             dx_ref, dba_ref, dcw_ref, dal_ref, ddtb_ref, carry_ref):
        step = pl.program_id(0)
        i = nr - 1 - step

        @pl.when(step == 0)
        def _():
            carry_ref[...] = jnp.zeros_like(carry_ref)
            dcw_ref[...] = jnp.zeros_like(dcw_ref)
            dal_ref[...] = jnp.zeros_like(dal_ref)
            ddtb_ref[...] = jnp.zeros_like(ddtb_ref)
        for t in range(DN_CONV_W // 128):
            cols = slice(128 * t, 128 * (t + 1))
            w, x = cw_ref[:, cols], x_ref[:, cols]
            if t < DN_K_HEADS:
                dout = dq_ref[:, cols]
            elif t < 2 * DN_K_HEADS:
                dout = dk_ref[:, 128 * (t - 8):128 * (t - 7)]
            else:
                dout = dv_ref[:, 128 * (t - 16):128 * (t - 15)]
            _, vjp = jax.vjp(functools.partial(_dn_post_tile, t=t), c_ref[:, cols].astype(F32))
            (dc,) = vjp(dout)
            nxt = carry_ref[:, cols]
            dx = w[3:4, :] * dc
            dcw_ref[3:4, cols] += jnp.sum(dc * x, axis=0, keepdims=True)
            for s in range(1, DN_CONV_K):
                up = _shift_up(dc, nxt, s)
                dx = dx + w[3 - s:4 - s, :] * up
                dcw_ref[3 - s:4 - s, cols] += jnp.sum(up * x, axis=0, keepdims=True)
            dx_ref[:, cols] = dx
            carry_ref[:, cols] = dc[0:8, :]
        fn = functools.partial(_dn_beta_g, live=_live_rows(i, rb))
        _, vjp = jax.vjp(fn, ba_ref[...], al_ref[...], dtb_ref[...])
        dba, dal, ddtb = vjp((dbg_ref[:, 0:DN_V_HEADS], dbg_ref[:, DN_V_HEADS:]))
        dba_ref[...] = dba
        dal_ref[...] += dal
        ddtb_ref[...] += ddtb

    return _row_call("dn_conv_bwd", body, T, rb, [dqn, dkn, dv, dbg, qkv, conv_out, ba], [conv_w, a_log, dt_bias],
                     [(4096, F32), (32, F32)], [((DN_CONV_K, 4096), F32), ((1, DN_V_HEADS), F32), ((1, DN_V_HEADS), F32)],
                     reverse=True, scratch=[pltpu.VMEM((8, 4096), F32)])


def _dn_in_bwd(dqkv, dz, dba, h1, dy, norm_w, w_in):
    T = h1.shape[0]

    def body(dqkv_ref, dz_ref, dba_ref, h_ref, dy_ref, nw_ref, w_ref, dh_ref, dnw_ref):
        @pl.when(pl.program_id(0) == 0)
        def _():
            dnw_ref[...] = jnp.zeros_like(dnw_ref)
        dxn = (_bdot(dqkv_ref[...], w_ref[:, 0:4096], NT) + _bdot(dz_ref[...], w_ref[:, 4096:6144], NT)
               + _bdot(dba_ref[...], w_ref[:, 6144:6176], NT))
        _, vjp = jax.vjp(_rms, h_ref[...], nw_ref[...])
        dh, dnw = vjp(dxn)
        dh_ref[...] = (dy_ref[...] + dh) * _live_rows(pl.program_id(0), ROW_BLOCK)
        dnw_ref[...] += dnw

    return _row_call("dn_in_bwd", body, T, ROW_BLOCK, [dqkv, dz, dba, h1, dy], [norm_w, w_in],
                     [(1024, F32)], [((1, 1024), F32)])


def _exchange(parts, scatter, name):
    n = len(parts)
    out_shape = [jax.ShapeDtypeStruct(p.shape if sc else (N_DEV,) + p.shape, p.dtype) for p, sc in zip(parts, scatter)]

    def body(*refs):
        ins, outs = refs[:n], refs[n:2 * n]
        send_sems, recv_sems, local_sems = refs[2 * n:]
        x, y, c = lax.axis_index("x"), lax.axis_index("y"), lax.axis_index("c")
        me = 4 * x + 2 * y + c
        peers = []
        for k in range(1, N_DEV):
            px = 1 - x if k & 4 else x
            py = 1 - y if k & 2 else y
            pc = 1 - c if k & 1 else c
            peers.append(((px, py, pc), 4 * px + 2 * py + pc))

        def src(a, idx):
            return ins[a].at[idx] if scatter[a] else ins[a]

        local = [pltpu.make_async_copy(src(a, me), outs[a].at[me], local_sems.at[a]) for a in range(n)]
        for cp in local:
            cp.start()
        for a in range(n):
            for k, (dev, idx) in enumerate(peers):
                pltpu.make_async_remote_copy(
                    src_ref=src(a, idx), dst_ref=outs[a].at[me], send_sem=send_sems.at[a, k], recv_sem=recv_sems.at[a, k],
                    device_id=dev, device_id_type=pl.DeviceIdType.MESH).start()
        for a in range(n):
            for k, (dev, idx) in enumerate(peers):
                pltpu.make_async_remote_copy(
                    src_ref=src(a, idx), dst_ref=outs[a].at[idx], send_sem=send_sems.at[a, k], recv_sem=recv_sems.at[a, k],
                    device_id=dev, device_id_type=pl.DeviceIdType.MESH).wait()
        for cp in local:
            cp.wait()

    hbm = pl.BlockSpec(memory_space=pltpu.HBM)
    return pl.pallas_call(
        body, out_shape=out_shape, in_specs=[hbm] * n, out_specs=[hbm] * n,
        scratch_shapes=[pltpu.SemaphoreType.DMA((n, N_DEV - 1)), pltpu.SemaphoreType.DMA((n, N_DEV - 1)),
                        pltpu.SemaphoreType.DMA((n,))],
        name=name,
    )(*parts)


def _gather_two_level(parts, name):
    n = len(parts)
    out_shape = [jax.ShapeDtypeStruct((N_DEV,) + p.shape, p.dtype) for p in parts]

    def body(*refs):
        ins, outs = refs[:n], refs[n:2 * n]
        send_sems, recv_sems, local_sems = refs[2 * n:]
        x, y, c = lax.axis_index("x"), lax.axis_index("y"), lax.axis_index("c")
        idx = lambda px, py, pc: 4 * px + 2 * py + pc
        me, sibling = (x, y, c), (x, y, 1 - c)
        chips = [(1 - x, y), (x, 1 - y), (1 - x, 1 - y)]

        def copy(a, k, block, to, src=None):
            slot = outs[a].at[idx(*block)]
            return pltpu.make_async_remote_copy(
                src_ref=slot if src is None else src, dst_ref=slot, send_sem=send_sems.at[a, k], recv_sem=recv_sems.at[a, k],
                device_id=to, device_id_type=pl.DeviceIdType.MESH)

        local = [pltpu.make_async_copy(ins[a], outs[a].at[idx(*me)], local_sems.at[a]) for a in range(n)]
        for cp in local:
            cp.start()
        sent = []
        for a in range(n):
            sent.append(copy(a, 0, me, sibling, src=ins[a]))
            sent += [copy(a, 1 + j, me, (*chip, c), src=ins[a]) for j, chip in enumerate(chips)]
        for cp in sent:
            cp.start()
        for a in range(n):
            for j, chip in enumerate(chips):
                copy(a, 1 + j, (*chip, c), me).wait_recv()
                passed = copy(a, 4 + j, (*chip, c), sibling)
                passed.start()
                sent.append(passed)
        for a in range(n):
            copy(a, 0, sibling, me).wait_recv()
            for j, chip in enumerate(chips):
                copy(a, 4 + j, (*chip, 1 - c), me).wait_recv()
        for cp in sent:
            cp.wait_send()
        for cp in local:
            cp.wait()

    hbm = pl.BlockSpec(memory_space=pltpu.HBM)
    return pl.pallas_call(
        body, out_shape=out_shape, in_specs=[hbm] * n, out_specs=[hbm] * n,
        scratch_shapes=[pltpu.SemaphoreType.DMA((n, N_DEV - 1)), pltpu.SemaphoreType.DMA((n, N_DEV - 1)),
                        pltpu.SemaphoreType.DMA((n,))],
        name=name,
    )(*parts)


def _swap_with_sibling(parts, name):
    n = len(parts)

    def body(*refs):
        ins, outs = refs[:n], refs[n:2 * n]
        send_sems, recv_sems = refs[2 * n:]
        x, y, c = lax.axis_index("x"), lax.axis_index("y"), lax.axis_index("c")
        copies = [pltpu.make_async_remote_copy(
            src_ref=ins[a].at[1 - c], dst_ref=outs[a], send_sem=send_sems.at[a], recv_sem=recv_sems.at[a],
            device_id=(x, y, 1 - c), device_id_type=pl.DeviceIdType.MESH) for a in range(n)]
        for cp in copies:
            cp.start()
        for cp in copies:
            cp.wait()

    hbm = pl.BlockSpec(memory_space=pltpu.HBM)
    return pl.pallas_call(
        body, out_shape=[jax.ShapeDtypeStruct(p.shape[1:], p.dtype) for p in parts], in_specs=[hbm] * n, out_specs=[hbm] * n,
        scratch_shapes=[pltpu.SemaphoreType.DMA((n,)), pltpu.SemaphoreType.DMA((n,))],
        name=name,
    )(*parts)


def _pair_sum(a, b, name):
    R, C = a.shape
    rb = _adam_rows(R)

    def body(a_ref, b_ref, o_ref):
        o_ref[...] = (a_ref[...].astype(F32) + b_ref[...].astype(F32)).astype(BF16)

    blk = pl.BlockSpec((rb, C), lambda i: (i, 0))
    return pl.pallas_call(body, grid=(R // rb,), in_specs=[blk, blk], out_specs=blk,
                          out_shape=jax.ShapeDtypeStruct((R, C), BF16), name=name, compiler_params=_cparams())(a, b)


def _exchange_chips(parts, name):
    n = len(parts)
    n_chips = N_DEV // 2

    def body(*refs):
        ins, outs = refs[:n], refs[n:2 * n]
        send_sems, recv_sems, local_sems = refs[2 * n:]
        x, y, c = lax.axis_index("x"), lax.axis_index("y"), lax.axis_index("c")
        mine = 2 * x + y
        chips = [(1 - x, y), (x, 1 - y), (1 - x, 1 - y)]
        local = [pltpu.make_async_copy(ins[a].at[mine], outs[a].at[mine], local_sems.at[a]) for a in range(n)]
        for cp in local:
            cp.start()
        for a in range(n):
            for k, (px, py) in enumerate(chips):
                pltpu.make_async_remote_copy(
                    src_ref=ins[a].at[2 * px + py], dst_ref=outs[a].at[mine], send_sem=send_sems.at[a, k],
                    recv_sem=recv_sems.at[a, k], device_id=(px, py, c), device_id_type=pl.DeviceIdType.MESH).start()
        for a in range(n):
            for k, (px, py) in enumerate(chips):
                pltpu.make_async_remote_copy(
                    src_ref=ins[a].at[2 * px + py], dst_ref=outs[a].at[2 * px + py], send_sem=send_sems.at[a, k],
                    recv_sem=recv_sems.at[a, k], device_id=(px, py, c), device_id_type=pl.DeviceIdType.MESH).wait()
        for cp in local:
            cp.wait()

    hbm = pl.BlockSpec(memory_space=pltpu.HBM)
    return pl.pallas_call(
        body, out_shape=[jax.ShapeDtypeStruct(p.shape, p.dtype) for p in parts], in_specs=[hbm] * n, out_specs=[hbm] * n,
        scratch_shapes=[pltpu.SemaphoreType.DMA((n, n_chips - 1)), pltpu.SemaphoreType.DMA((n, n_chips - 1)),
                        pltpu.SemaphoreType.DMA((n,))],
        name=name,
    )(*parts)


def _adam_rows(rows):
    for rb in (128, 64, 40, 16, 8):
        if rows % rb == 0:
            return rb
    return rows


def _adamw(stack, w, m, v, name):
    R, C = w.shape
    rb = _adam_rows(R)
    slots = stack.shape[0]

    def body(s_ref, w_ref, m_ref, v_ref, g_ref, d_ref, nm_ref, nv_ref):
        g = s_ref[0].astype(F32)
        for s in range(1, slots):
            g = g + s_ref[s].astype(F32)
        nm = ADAM_B1 * m_ref[...] + (1.0 - ADAM_B1) * g
        nv = ADAM_B2 * v_ref[...] + (1.0 - ADAM_B2) * (g * g)
        m_hat = nm / (1.0 - ADAM_B1 ** ADAM_STEP)
        v_hat = nv / (1.0 - ADAM_B2 ** ADAM_STEP)
        g_ref[...] = g
        d_ref[...] = -ADAM_LR * (m_hat / (jnp.sqrt(v_hat) + ADAM_EPS) + ADAM_WD * w_ref[...])
        nm_ref[...] = nm
        nv_ref[...] = nv

    blk = pl.BlockSpec((rb, C), lambda i: (i, 0))
    return pl.pallas_call(
        body, grid=(R // rb,),
        in_specs=[pl.BlockSpec((slots, rb, C), lambda i: (0, i, 0)), blk, blk, blk],
        out_specs=[blk] * 4, out_shape=[jax.ShapeDtypeStruct((R, C), F32)] * 4,
        name=name, compiler_params=_cparams(),
    )(stack, w, m, v)


def _pad_rows8(a):
    return jnp.concatenate([a, jnp.zeros((8 - a.shape[0], a.shape[1]), a.dtype)], axis=0) if a.shape[0] < 8 else a


def _pack_small(norm_w, qnw, knw, sinks, a_log, dt_bias, onw, extra):
    z = lambda n: jnp.zeros((1, n), F32)
    row = jnp.concatenate([norm_w, qnw, knw, sinks, a_log, dt_bias, z(80), onw, extra, z(512)], axis=1)
    return row.reshape(16, 128)


def _unpack_small(p):
    row = p.reshape(1, 2048)
    cut = lambda a, n: row[:, a:a + n]
    return (cut(0, 1024), cut(1024, 64), cut(1088, 64), cut(1152, 16), cut(1168, 16), cut(1184, 16), cut(1280, 128),
            cut(1408, 128))


def _pack_rows(w_in_a, w_in_d, w_out_a, w_out_d, meta, conv, dn_norm):
    a = jnp.concatenate([w_in_a, w_in_d], axis=1)
    b = jnp.concatenate([w_out_a, w_out_d], axis=0)
    c = jnp.concatenate([meta, conv.reshape(16, 128), _pad_rows8(dn_norm)], axis=0)
    return a, b, c


def _unpack_rows(a, b, c):
    return (a[:, :288], a[:, 288:], b[:128], b[128:], c[:16], c[16:32].reshape(4, 512), c[32:33])


def _local_step(x, front, target, w):
    xn0, q, kv, gate = _attn_in_fwd(x, front, w["attn_norm_w"], w["attn_w_in"])
    o = _attn_core_fwd(q, kv, w["attn_sinks"], w["attn_q_norm_w"], w["attn_k_norm_w"])
    h1 = _attn_out_fwd(o, gate, x, front, w["attn_w_out"])
    xn1, qkv, z, ba = _dn_in_fwd(h1, w["dn_norm_w"], w["dn_w_in"])
    qn, kn, v, bg, conv_out = _dn_conv_fwd(qkv, ba, w["dn_conv_w"], w["dn_a_log"], w["dn_dt_bias"])
    o_dn, ssave, *saved = _dn_scan_fwd(qn, kn, v, bg)
    dy, og_dn, loss = _dn_out_fwd(o_dn, z, h1, target, w["dn_w_out"], w["dn_o_norm_w"])

    g = {}
    do_dn, dz, g["dn_o_norm_w"] = _dn_out_bwd(dy, o_dn, z, w["dn_w_out"], w["dn_o_norm_w"])
    g["dn_w_out"] = _wgrad(og_dn, dy, 1024, "wgrad_dn_out")
    dqn, dkn, dv, dbg = _dn_scan_bwd(do_dn, qn, kn, v, bg, ssave, saved)
    dqkv, dba, g["dn_conv_w"], g["dn_a_log"], g["dn_dt_bias"] = _dn_conv_bwd(
        dqn, dkn, dv, dbg, qkv, conv_out, ba, w["dn_conv_w"], w["dn_a_log"], w["dn_dt_bias"])
    dh1, g["dn_norm_w"] = _dn_in_bwd(dqkv, dz, dba, h1, dy, w["dn_norm_w"], w["dn_w_in"])
    g["dn_w_in"] = jnp.concatenate([_wgrad(xn1, dqkv, 1024, "wgrad_dn_qkv"), _wgrad(xn1, dz, 1024, "wgrad_dn_z"),
                                    _wgrad(xn1, dba, 32, "wgrad_dn_ba")], axis=1)
    do, dgate, g["attn_w_out"] = _attn_out_bwd(dh1, o, gate, w["attn_w_out"])
    dq, dkv, g["attn_sinks"], g["attn_q_norm_w"], g["attn_k_norm_w"] = _attn_core_bwd(
        do, q, kv, w["attn_sinks"], w["attn_q_norm_w"], w["attn_k_norm_w"])
    grad_x, dfront, g["attn_norm_w"] = _attn_in_bwd(dq, dkv, dgate, x, front, dh1, w["attn_norm_w"], w["attn_w_in"])
    g["meta_tokens"] = dfront[FRONT_PAD:]
    g["attn_w_in"] = jnp.concatenate([_wgrad(xn0, dq, 1024, "wgrad_attn_q"), _wgrad(xn0, dkv, 256, "wgrad_attn_kv"),
                                      _wgrad(xn0, dgate, 1024, "wgrad_attn_gate")], axis=1)
    return loss, grad_x, g


WEIGHTS = ['meta_tokens', 'attn_norm_w', 'attn_w_in', 'attn_q_norm_w', 'attn_k_norm_w', 'attn_sinks', 'attn_w_out',
           'dn_norm_w', 'dn_w_in', 'dn_conv_w', 'dn_a_log', 'dn_dt_bias', 'dn_o_norm_w', 'dn_w_out']
SMALL = ['attn_norm_w', 'attn_q_norm_w', 'attn_k_norm_w', 'attn_sinks', 'dn_a_log', 'dn_dt_bias', 'dn_o_norm_w']


def kernel(x, meta_tokens, attn_norm_w, attn_w_in, attn_q_norm_w, attn_k_norm_w, attn_sinks, attn_w_out, dn_norm_w, dn_w_in, dn_conv_w, dn_a_log, dn_dt_bias, dn_o_norm_w, dn_w_out, loss_target, m_meta_tokens, m_attn_norm_w, m_attn_w_in, m_attn_q_norm_w, m_attn_k_norm_w, m_attn_sinks, m_attn_w_out, m_dn_norm_w, m_dn_w_in, m_dn_conv_w, m_dn_a_log, m_dn_dt_bias, m_dn_o_norm_w, m_dn_w_out, v_meta_tokens, v_attn_norm_w, v_attn_w_in, v_attn_q_norm_w, v_attn_k_norm_w, v_attn_sinks, v_attn_w_out, v_dn_norm_w, v_dn_w_in, v_dn_conv_w, v_dn_a_log, v_dn_dt_bias, v_dn_o_norm_w, v_dn_w_out):
    shard = dict(meta_tokens=meta_tokens, attn_norm_w=attn_norm_w, attn_w_in=attn_w_in[0], attn_q_norm_w=attn_q_norm_w,
                 attn_k_norm_w=attn_k_norm_w, attn_sinks=attn_sinks, attn_w_out=attn_w_out[0], dn_norm_w=dn_norm_w,
                 dn_w_in=dn_w_in[0], dn_conv_w=dn_conv_w[0], dn_a_log=dn_a_log, dn_dt_bias=dn_dt_bias,
                 dn_o_norm_w=dn_o_norm_w, dn_w_out=dn_w_out[0])
    mom_m = dict(meta_tokens=m_meta_tokens, attn_norm_w=m_attn_norm_w, attn_w_in=m_attn_w_in[0], attn_q_norm_w=m_attn_q_norm_w,
                 attn_k_norm_w=m_attn_k_norm_w, attn_sinks=m_attn_sinks, attn_w_out=m_attn_w_out[0], dn_norm_w=m_dn_norm_w,
                 dn_w_in=m_dn_w_in[0], dn_conv_w=m_dn_conv_w[0], dn_a_log=m_dn_a_log, dn_dt_bias=m_dn_dt_bias,
                 dn_o_norm_w=m_dn_o_norm_w, dn_w_out=m_dn_w_out[0])
    mom_v = dict(meta_tokens=v_meta_tokens, attn_norm_w=v_attn_norm_w, attn_w_in=v_attn_w_in[0], attn_q_norm_w=v_attn_q_norm_w,
                 attn_k_norm_w=v_attn_k_norm_w, attn_sinks=v_attn_sinks, attn_w_out=v_attn_w_out[0], dn_norm_w=v_dn_norm_w,
                 dn_w_in=v_dn_w_in[0], dn_conv_w=v_dn_conv_w[0], dn_a_log=v_dn_a_log, dn_dt_bias=v_dn_dt_bias,
                 dn_o_norm_w=v_dn_o_norm_w, dn_w_out=v_dn_w_out[0])

    def rows_of(d):
        return _pack_rows(d["attn_w_in"], d["dn_w_in"], d["attn_w_out"], d["dn_w_out"], d["meta_tokens"], d["dn_conv_w"],
                          d["dn_norm_w"])

    def small_of(d, extra):
        return _pack_small(*[d[k] for k in SMALL], extra)

    wa, wb, wc = rows_of(shard)
    ga, gb, gc = _gather_two_level([wa.astype(BF16), wb.astype(BF16), wc], "gather_weights")
    full = {k: shard[k] for k in SMALL}
    full["attn_w_in"] = ga[:, :, :288].transpose(1, 0, 2).reshape(1024, 2304)
    full["dn_w_in"] = ga[:, :, 288:].transpose(1, 0, 2).reshape(1024, 6176)
    full["attn_w_out"] = gb[:, :128].reshape(1024, 1024)
    full["dn_w_out"] = gb[:, 128:].reshape(2048, 1024)
    meta_full = gc[:, :16].transpose(1, 0, 2).reshape(N_META, 1024)
    full["dn_conv_w"] = gc[:, 16:32].reshape(N_DEV, 4, 512).transpose(1, 0, 2).reshape(4, 4096)
    full["dn_norm_w"] = gc[:, 32].reshape(1, 1024)

    front = jnp.concatenate([jnp.zeros((FRONT_PAD, D_MODEL), F32), meta_full], axis=0)
    loss, grad_x, g = _local_step(x[0], front, loss_target[0], full)
    grad_x = grad_x[None]

    pa = jnp.concatenate([g["attn_w_in"].reshape(1024, N_DEV, 288), g["dn_w_in"].reshape(1024, N_DEV, 772)],
                         axis=2).transpose(1, 0, 2)
    pb = jnp.concatenate([g["attn_w_out"].reshape(N_DEV, 128, 1024), g["dn_w_out"].reshape(N_DEV, 256, 1024)], axis=1)
    dn_norm8 = jnp.concatenate([g["dn_norm_w"].reshape(N_DEV, 1, 128), jnp.zeros((N_DEV, 7, 128), F32)], axis=1)
    pc = jnp.concatenate([g["meta_tokens"].reshape(N_META, N_DEV, 128).transpose(1, 0, 2),
                          g["dn_conv_w"].reshape(4, N_DEV, 512).transpose(1, 0, 2).reshape(N_DEV, 16, 128), dn_norm8], axis=1)
    ps = small_of(g, loss)
    c = lax.axis_index("c")
    by_core = lambda p: p.astype(BF16).reshape((N_DEV // 2, 2) + p.shape[1:]).swapaxes(0, 1)
    pa2, pb2 = by_core(pa), by_core(pb)
    ra, rb_ = _swap_with_sibling([pa2, pb2], "swap_grads")
    own = lambda p2: lax.dynamic_index_in_dim(p2, c, axis=0, keepdims=False)
    flat = lambda t: t.reshape((-1,) + t.shape[2:])
    sa = _pair_sum(flat(own(pa2)), flat(ra), "pair_sum_a").reshape(ra.shape)
    sb = _pair_sum(flat(own(pb2)), flat(rb_), "pair_sum_b").reshape(rb_.shape)
    xa, xb = _exchange_chips([sa, sb], "exchange_grads")
    xc, xs = _exchange([pc, ps], [True, False], "exchange_small")

    out = {}
    ma, mb, mc = rows_of(mom_m)
    va, vb, vc = rows_of(mom_v)
    ra = _adamw(xa, wa, ma, va, "adamw_a")
    rb = _adamw(xb, wb, mb, vb, "adamw_b")
    rc = _adamw(xc, wc, mc, vc, "adamw_c")
    zero = jnp.zeros((1, 128), F32)
    rs = _adamw(xs, small_of(shard, zero), small_of(mom_m, zero), small_of(mom_v, zero), "adamw_small")
    row_names = ["attn_w_in", "dn_w_in", "attn_w_out", "dn_w_out", "meta_tokens", "dn_conv_w", "dn_norm_w"]
    lead = {"attn_w_in", "dn_w_in", "attn_w_out", "dn_w_out", "dn_conv_w"}
    for kind in range(4):
        vals = dict(zip(row_names, _unpack_rows(ra[kind], rb[kind], rc[kind])))
        small = _unpack_small(rs[kind])
        vals.update(dict(zip(SMALL, small[:7])))
        if kind == 0:
            loss_total = small[7][0, 0]
        out[kind] = [vals[k][None] if k in lead else vals[k] for k in WEIGHTS]
    return (loss_total, grad_x, *out[0], *out[1], *out[2], *out[3])
```

```python
import functools
import math

import jax
import jax.numpy as jnp
from jax import lax
from jax.experimental import pallas as pl
from jax.experimental.pallas import tpu as pltpu

F32, BF16 = jnp.float32, jnp.bfloat16

D_MODEL = 1024
N_META = 16
NORM_EPS = 1e-6
ATTN_HEADS, ATTN_KV_HEADS, ATTN_GROUPS, ATTN_HD = 16, 2, 8, 64
ATTN_BLOCK = 128
ATTN_STEP_BLOCKS = 3
FRONT_PAD = ATTN_BLOCK - N_META
DN_HD, DN_K_HEADS, DN_V_HEADS = 128, 8, 16
DN_CHUNK = 128
TRI_BLOCK = 64
SCAN_CHUNKS = 3
SCAN_BWD_CHUNKS = 1
SCAN_FWD_GROUP = 16
SCAN_BWD_GROUP = 8
DN_KEY_W, DN_VAL_W = 1024, 2048
DN_CONV_W = 2 * DN_KEY_W + DN_VAL_W
DN_CONV_K = 4
N_DEV = 8
ROW_BLOCK = 384
WGRAD_ROWS = 1376
VMEM_LIMIT = 56 * 1024 * 1024
NEG = -1e30

ADAM_LR, ADAM_B1, ADAM_B2, ADAM_EPS, ADAM_WD, ADAM_STEP = 0.001, 0.9, 0.999, 1e-08, 0.01, 10

NT = (((1,), (1,)), ((), ()))
TN = (((0,), (0,)), ((), ()))


def _cparams(sem=("arbitrary",)):
    return pltpu.CompilerParams(dimension_semantics=sem, vmem_limit_bytes=VMEM_LIMIT)


def _rms(x, w):
    return x * lax.rsqrt(jnp.mean(x * x, axis=-1, keepdims=True) + NORM_EPS) * w


def _silu(x):
    return x * jax.nn.sigmoid(x)


def _softplus(x):
    return jnp.maximum(x, 0.0) + jnp.log(1.0 + jnp.exp(-jnp.abs(x)))


NN = (((1,), (0,)), ((), ()))


def _mm(a, b, dims):
    return lax.dot_general(a.astype(BF16), b.astype(BF16), dims, preferred_element_type=F32)


@functools.partial(jax.custom_vjp, nondiff_argnums=(2,))
def _bdot_vjp(a, b, dims):
    return _mm(a, b, dims)


def _bdot_fwd(a, b, dims):
    a16, b16 = a.astype(BF16), b.astype(BF16)
    return _mm(a16, b16, dims), (a16, b16, jnp.zeros((), a.dtype), jnp.zeros((), b.dtype))


def _bdot_bwd(dims, res, g):
    a16, b16, ta, tb = res
    g16 = g.astype(BF16)
    if dims == NN:
        da, db = _mm(g16, b16, NT), _mm(a16, g16, TN)
    elif dims == NT:
        da, db = _mm(g16, b16, NN), _mm(g16, a16, TN)
    else:
        da, db = _mm(b16, g16, NT), _mm(a16, g16, NN)
    return da.astype(ta.dtype), db.astype(tb.dtype)


_bdot_vjp.defvjp(_bdot_fwd, _bdot_bwd)


def _bdot(a, b, dims=NN):
    return _bdot_vjp(a, b, dims)


def _row_call(name, body, n_rows, rb, rows, consts, outs, accs=(), reverse=False, scratch=(), halos=()):
    n = n_rows // rb
    assert n * rb == n_rows
    idx = (lambda i: (n - 1 - i, 0)) if reverse else (lambda i: (i, 0))
    in_specs = [pl.BlockSpec((rb, a.shape[1]), idx) for a in rows]
    in_specs += [pl.BlockSpec((hr, a.shape[1]), fn) for a, hr, fn in halos]
    in_specs += [pl.BlockSpec(c.shape, functools.partial(lambda i, nd: (0,) * nd, nd=c.ndim)) for c in consts]
    out_specs = [pl.BlockSpec((rb, c), idx) for c, _ in outs]
    out_specs += [pl.BlockSpec(s, functools.partial(lambda i, nd: (0,) * nd, nd=len(s))) for s, _ in accs]
    out_shape = [jax.ShapeDtypeStruct((n_rows, c), dt) for c, dt in outs]
    out_shape += [jax.ShapeDtypeStruct(s, dt) for s, dt in accs]
    return pl.pallas_call(
        body, grid=(n,), in_specs=in_specs, out_specs=out_specs, out_shape=out_shape,
        scratch_shapes=list(scratch), name=name, compiler_params=_cparams(),
    )(*rows, *[a for a, _, _ in halos], *consts)


def _token_views(x):
    per = ROW_BLOCK // ATTN_BLOCK
    return [(x, ATTN_BLOCK, functools.partial(lambda i, k: (jnp.maximum(per * i - 1 + k, 0), 0), k=k)) for k in range(per)]


def _padded_block(i, front, views):
    first = jnp.where(i == 0, front, views[0][...]) if front is not None else views[0][...]
    return jnp.concatenate([first] + [v[...] for v in views[1:]], axis=0)


def _attn_in_fwd(x, front, norm_w, w_in):
    T = x.shape[0] + ATTN_BLOCK

    def body(xa_ref, xb_ref, xc_ref, front_ref, nw_ref, w_ref, xn_ref, q_ref, kv_ref, gate_ref):
        h = _padded_block(pl.program_id(0), front_ref[...], (xa_ref, xb_ref, xc_ref))
        xn = _rms(h, nw_ref[...]).astype(BF16)
        xn_ref[...] = xn
        q_ref[...] = jnp.dot(xn, w_ref[:, 0:1024], preferred_element_type=F32)
        kv_ref[...] = jnp.dot(xn, w_ref[:, 1024:1280], preferred_element_type=F32)
        gate_ref[...] = jnp.dot(xn, w_ref[:, 1280:2304], preferred_element_type=F32)

    return _row_call("attn_in_fwd", body, T, ROW_BLOCK, [], [front, norm_w, w_in],
                     [(1024, BF16), (1024, F32), (256, F32), (1024, F32)], halos=_token_views(x))


def _attn_bias(n, j):
    C, R = 2 * ATTN_BLOCK + N_META, ATTN_GROUPS * ATTN_BLOCK
    c = lax.broadcasted_iota(jnp.int32, (C, R), 0)
    r = lax.broadcasted_iota(jnp.int32, (C, R), 1)
    ql = r & (ATTN_BLOCK - 1)
    is_meta = c >= 2 * ATTN_BLOCK
    dist_band = ATTN_BLOCK + ql - c
    cmin = jnp.maximum(0, 2 * ATTN_BLOCK - ATTN_BLOCK * n)
    valid_band = (c >= cmin) & (dist_band >= 0) & (dist_band < ATTN_BLOCK)
    dist_meta = ATTN_BLOCK * n + ql - FRONT_PAD - (c - 2 * ATTN_BLOCK)
    valid = (is_meta & (dist_meta >= 0)) | (jnp.logical_not(is_meta) & valid_band)
    dist = jnp.minimum(jnp.where(is_meta, dist_meta, dist_band), ATTN_BLOCK).astype(F32)
    rr = lax.broadcasted_iota(jnp.int32, (1, R), 1)
    head = (rr >> 7).astype(F32) + float(ATTN_GROUPS * j + 1)
    slope = jnp.exp(head * (-0.5 * math.log(2.0)))
    return jnp.where(valid, slope * dist, -NEG)


def _attn_table_scratch():
    return [pltpu.VMEM((ATTN_STEP_BLOCKS, ATTN_KV_HEADS, 2 * ATTN_BLOCK + N_META, ATTN_GROUPS * ATTN_BLOCK), F32)]


def _attn_groups(q_t, k, v, sinkrow, qnw_col, knw, bias, late_norm=True):
    n = range(len(q_t))
    qn = [q_t[j] * (lax.rsqrt(jnp.mean(q_t[j] * q_t[j], axis=0, keepdims=True) + NORM_EPS) * (ATTN_HD ** -0.5)) * qnw_col
          for j in n]
    kn = [_rms(k[j], knw) for j in n]
    s = [_bdot(kn[j], qn[j]) - bias[j] for j in n]
    m = [lax.stop_gradient(jnp.maximum(jnp.max(s[j], axis=0, keepdims=True), sinkrow[j])) for j in n]
    e = [jnp.exp(s[j] - m[j]) for j in n]
    inv = [1.0 / (jnp.sum(e[j], axis=0, keepdims=True) + jnp.exp(sinkrow[j] - m[j])) for j in n]
    if late_norm:
        return [_bdot(v[j], e[j], TN) * inv[j] for j in n]
    return [_bdot(v[j], e[j] * inv[j], TN) for j in n]


def _sink_row(sinks_ref, j):
    rr = lax.broadcasted_iota(jnp.int32, (1, ATTN_GROUPS * ATTN_BLOCK), 1) >> 7
    row = jnp.zeros((1, ATTN_GROUPS * ATTN_BLOCK), F32)
    for hl in range(ATTN_GROUPS):
        row = jnp.where(rr == hl, sinks_ref[0, ATTN_GROUPS * j + hl], row)
    return row


def _heads_to_lanes(ref, b, j):
    rows = slice(ATTN_BLOCK * b, ATTN_BLOCK * (b + 1))
    return jnp.concatenate([ref[rows, ATTN_HD * h:ATTN_HD * (h + 1)].T
                            for h in range(ATTN_GROUPS * j, ATTN_GROUPS * (j + 1))], axis=1)


def _lanes_to_heads(ref, b, j, x_t):
    rows = slice(ATTN_BLOCK * b, ATTN_BLOCK * (b + 1))
    for hl in range(ATTN_GROUPS):
        h = ATTN_GROUPS * j + hl
        ref[rows, ATTN_HD * h:ATTN_HD * (h + 1)] = x_t[:, ATTN_BLOCK * hl:ATTN_BLOCK * (hl + 1)].T


def _attn_chains(sinks_ref, q_ref, kvc_ref, kvp_ref, kvm_ref, bias_ref):
    chains = [(b, j) for b in range(ATTN_STEP_BLOCKS) for j in range(ATTN_KV_HEADS)]
    q_t, ks, vs, sinkrows, biases = [], [], [], [], []
    for b, j in chains:
        rows = slice(ATTN_BLOCK * b, ATTN_BLOCK * (b + 1))
        prev = kvp_ref if b == 0 else kvc_ref
        prows = slice(0, ATTN_BLOCK) if b == 0 else slice(ATTN_BLOCK * (b - 1), ATTN_BLOCK * b)
        ksl = slice(ATTN_HD * j, ATTN_HD * (j + 1))
        vsl = slice(128 + ATTN_HD * j, 128 + ATTN_HD * (j + 1))
        ks.append(jnp.concatenate([prev[prows, ksl], kvc_ref[rows, ksl], kvm_ref[FRONT_PAD:, ksl]], axis=0))
        vs.append(jnp.concatenate([prev[prows, vsl], kvc_ref[rows, vsl], kvm_ref[FRONT_PAD:, vsl]], axis=0))
        q_t.append(_heads_to_lanes(q_ref, b, j))
        sinkrows.append(_sink_row(sinks_ref, j))
        biases.append(bias_ref[b, j])
    return chains, q_t, ks, vs, sinkrows, biases


def _attn_core_fwd(q, kv, sinks, qnw, knw):
    T = q.shape[0]
    nb = T // ATTN_BLOCK
    nbs = ATTN_STEP_BLOCKS
    assert nb % nbs == 0
    rows_all = nbs * ATTN_BLOCK

    def body(sinks_ref, q_ref, kvc_ref, kvp_ref, kvm_ref, qnw_ref, knw_ref, o_ref, bias_ref):
        i = pl.program_id(0)

        @pl.when(i <= 1)
        def _():
            for b in range(nbs):
                for j in range(ATTN_KV_HEADS):
                    bias_ref[b, j] = _attn_bias(nbs * i + b, j)
        chains, q_t, ks, vs, sinkrows, biases = _attn_chains(sinks_ref, q_ref, kvc_ref, kvp_ref, kvm_ref, bias_ref)
        o_t = _attn_groups(q_t, ks, vs, sinkrows, qnw_ref[...], knw_ref[...], biases)
        for (b, j), o in zip(chains, o_t):
            _lanes_to_heads(o_ref, b, j, o)

    return pl.pallas_call(
        body, grid=(nb // nbs,),
        in_specs=[pl.BlockSpec(memory_space=pltpu.SMEM),
                  pl.BlockSpec((rows_all, 1024), lambda i: (i, 0)),
                  pl.BlockSpec((rows_all, 256), lambda i: (i, 0)),
                  pl.BlockSpec((ATTN_BLOCK, 256), lambda i: (jnp.maximum(nbs * i - 1, 0), 0)),
                  pl.BlockSpec((ATTN_BLOCK, 256), lambda i: (0, 0)),
                  pl.BlockSpec((ATTN_HD, 1), lambda i: (0, 0)),
                  pl.BlockSpec((1, ATTN_HD), lambda i: (0, 0))],
        out_specs=pl.BlockSpec((rows_all, 1024), lambda i: (i, 0)),
        out_shape=jax.ShapeDtypeStruct((T, 1024), F32),
        scratch_shapes=_attn_table_scratch(),
        name="attn_core_fwd", compiler_params=_cparams(),
    )(sinks, q, kv, kv, kv, qnw.reshape(ATTN_HD, 1), knw)


def _attn_out_fwd(o, gate, x, front, w_out):
    T = o.shape[0]

    def body(o_ref, g_ref, xa_ref, xb_ref, xc_ref, front_ref, w_ref, h1_ref):
        h = _padded_block(pl.program_id(0), front_ref[...], (xa_ref, xb_ref, xc_ref))
        og = o_ref[...] * _silu(g_ref[...])
        h1_ref[...] = h + _bdot(og, w_ref[...])

    return _row_call("attn_out_fwd", body, T, ROW_BLOCK, [o, gate], [front, w_out], [(1024, F32)], halos=_token_views(x))[0]


def _wgrad(xn, du, cg, name):
    T, kdim = xn.shape
    cdim = du.shape[1]
    rows = next(r for r in ((2 * WGRAD_ROWS,) if kdim <= 1024 else ()) + (WGRAD_ROWS, ROW_BLOCK) if T % r == 0)
    nr, nc = T // rows, cdim // cg
    assert nc * cg == cdim

    def body(x_ref, du_ref, dw_ref):
        @pl.when(pl.program_id(1) == 0)
        def _():
            dw_ref[...] = jnp.zeros_like(dw_ref)
        dw_ref[...] += _bdot(x_ref[...], du_ref[...], TN)

    return pl.pallas_call(
        body, grid=(nc, nr),
        in_specs=[pl.BlockSpec((rows, kdim), lambda j, i: (i, 0)),
                  pl.BlockSpec((rows, cg), lambda j, i: (i, j))],
        out_specs=pl.BlockSpec((kdim, cg), lambda j, i: (0, j)),
        out_shape=jax.ShapeDtypeStruct((kdim, cdim), F32),
        name=name, compiler_params=_cparams(("arbitrary", "arbitrary")),
    )(xn, du)


def _attn_out_bwd(dh1, o, gate, w_out):
    T = o.shape[0]

    def body(dh_ref, o_ref, g_ref, w_ref, do_ref, dg_ref, dw_ref):
        @pl.when(pl.program_id(0) == 0)
        def _():
            dw_ref[...] = jnp.zeros_like(dw_ref)
        dh = dh_ref[...]
        dog = _bdot(dh, w_ref[...], NT)
        og, vjp = jax.vjp(lambda o_, g_: o_ * _silu(g_), o_ref[...], g_ref[...])
        do, dg = vjp(dog)
        do_ref[...] = do
        dg_ref[...] = dg
        dw_ref[...] += _bdot(og, dh, TN)

    return _row_call("attn_out_bwd", body, T, ROW_BLOCK, [dh1, o, gate], [w_out],
                     [(1024, F32), (1024, F32)], [((1024, 1024), F32)])


def _attn_core_bwd(do, q, kv, sinks, qnw, knw):
    T = q.shape[0]
    nbs = ATTN_STEP_BLOCKS
    ns = T // (nbs * ATTN_BLOCK)
    rows_all = nbs * ATTN_BLOCK
    rev = lambda i: ns - 1 - i

    def body(sinks_ref, do_ref, q_ref, kvc_ref, kvp_ref, kvm_ref, qnw_ref, knw_ref,
             dq_ref, dkv_ref, dsinks_ref, dqnw_ref, dknw_ref, carry_ref, meta_ref, bias_ref):
        step = pl.program_id(0)
        i = rev(step)

        @pl.when((step == 0) | (i == 0))
        def _():
            for b in range(nbs):
                for j in range(ATTN_KV_HEADS):
                    bias_ref[b, j] = _attn_bias(nbs * i + b, j)

        @pl.when(step == 0)
        def _():
            carry_ref[...] = jnp.zeros_like(carry_ref)
            meta_ref[...] = jnp.zeros_like(meta_ref)
            dsinks_ref[...] = jnp.zeros_like(dsinks_ref)
            dqnw_ref[...] = jnp.zeros_like(dqnw_ref)
            dknw_ref[...] = jnp.zeros_like(dknw_ref)

        lane16 = lax.broadcasted_iota(jnp.int32, (1, ATTN_HEADS), 1)
        dsinks = jnp.zeros((1, ATTN_HEADS), F32)
        chains, q_t, ks, vs, sinkrows, biases = _attn_chains(sinks_ref, q_ref, kvc_ref, kvp_ref, kvm_ref, bias_ref)
        fn = functools.partial(_attn_groups, bias=biases, late_norm=False)
        _, vjp = jax.vjp(fn, q_t, ks, vs, sinkrows, qnw_ref[...], knw_ref[...])
        dq_t, dks, dvs, dsr, dqn, dkn = vjp([_heads_to_lanes(do_ref, b, j) for b, j in chains])
        dqnw_ref[...] += dqn
        dknw_ref[...] += dkn
        part = {}
        for c, (b, j) in enumerate(chains):
            _lanes_to_heads(dq_ref, b, j, dq_t[c])
            for hl in range(ATTN_GROUPS):
                dsinks = dsinks + jnp.where(lane16 == ATTN_GROUPS * j + hl,
                                            jnp.sum(dsr[c][:, ATTN_BLOCK * hl:ATTN_BLOCK * (hl + 1)]), 0.0)
            part[b, j] = (dks[c], dvs[c])
        for j in range(ATTN_KV_HEADS):
            for kind, sl in ((0, slice(ATTN_HD * j, ATTN_HD * (j + 1))), (1, slice(128 + ATTN_HD * j, 128 + ATTN_HD * (j + 1)))):
                for b in range(nbs):
                    d = part[b, j][kind]
                    nxt = part[b + 1, j][kind][0:ATTN_BLOCK, :] if b + 1 < nbs else carry_ref[:, sl]
                    dkv_ref[ATTN_BLOCK * b:ATTN_BLOCK * (b + 1), sl] = d[ATTN_BLOCK:2 * ATTN_BLOCK, :] + nxt
                    meta_ref[:, sl] += d[2 * ATTN_BLOCK:, :]
                carry_ref[:, sl] = part[0, j][kind][0:ATTN_BLOCK, :]
        dsinks_ref[...] += dsinks

        @pl.when(i == 0)
        def _():
            dkv_ref[FRONT_PAD:ATTN_BLOCK, :] += meta_ref[...]

    dq, dkv, dsinks, dqnw, dknw = pl.pallas_call(
        body, grid=(ns,),
        in_specs=[pl.BlockSpec(memory_space=pltpu.SMEM),
                  pl.BlockSpec((rows_all, 1024), lambda i: (rev(i), 0)),
                  pl.BlockSpec((rows_all, 1024), lambda i: (rev(i), 0)),
                  pl.BlockSpec((rows_all, 256), lambda i: (rev(i), 0)),
                  pl.BlockSpec((ATTN_BLOCK, 256), lambda i: (jnp.maximum(nbs * rev(i) - 1, 0), 0)),
                  pl.BlockSpec((ATTN_BLOCK, 256), lambda i: (0, 0)),
                  pl.BlockSpec((ATTN_HD, 1), lambda i: (0, 0)),
                  pl.BlockSpec((1, ATTN_HD), lambda i: (0, 0))],
        out_specs=[pl.BlockSpec((rows_all, 1024), lambda i: (rev(i), 0)),
                   pl.BlockSpec((rows_all, 256), lambda i: (rev(i), 0)),
                   pl.BlockSpec((1, ATTN_HEADS), lambda i: (0, 0)),
                   pl.BlockSpec((ATTN_HD, 1), lambda i: (0, 0)),
                   pl.BlockSpec((1, ATTN_HD), lambda i: (0, 0))],
        out_shape=[jax.ShapeDtypeStruct((T, 1024), F32), jax.ShapeDtypeStruct((T, 256), F32),
                   jax.ShapeDtypeStruct((1, ATTN_HEADS), F32), jax.ShapeDtypeStruct((ATTN_HD, 1), F32),
                   jax.ShapeDtypeStruct((1, ATTN_HD), F32)],
        scratch_shapes=[pltpu.VMEM((ATTN_BLOCK, 256), F32), pltpu.VMEM((N_META, 256), F32)] + _attn_table_scratch(),
        name="attn_core_bwd", compiler_params=_cparams(),
    )(sinks, do, q, kv, kv, kv, qnw.reshape(ATTN_HD, 1), knw)
    return dq, dkv, dsinks, dqnw.reshape(1, ATTN_HD), dknw


def _attn_in_bwd(dq, dkv, dgate, x, front, dh1, norm_w, w_in):
    T = dq.shape[0]
    n = T // ROW_BLOCK
    per = ROW_BLOCK // ATTN_BLOCK
    assert n >= 3

    def body(dq_ref, dkv_ref, dg_ref, dh1_ref, xa_ref, xb_ref, xc_ref, front_ref, nw_ref, w_ref,
             gx_ref, dfront_ref, dnw_ref, buf_ref, sems):
        i = pl.program_id(0)
        slot = i % 2

        def piece(step, k, s):
            return pltpu.make_async_copy(buf_ref.at[s, pl.ds(ATTN_BLOCK * k, ATTN_BLOCK)],
                                         gx_ref.at[pl.ds((per * step - 1 + k) * ATTN_BLOCK, ATTN_BLOCK)], sems.at[s, k])

        @pl.when(i == 0)
        def _():
            dnw_ref[...] = jnp.zeros_like(dnw_ref)
        for k in range(per):
            @pl.when((i >= 2) & ((k > 0) | (i > 2)))
            def _():
                piece(i - 2, k, slot).wait()
        h = _padded_block(i, front_ref[...], (xa_ref, xb_ref, xc_ref))
        dxn = (_bdot(dq_ref[...], w_ref[:, 0:1024], NT) + _bdot(dkv_ref[...], w_ref[:, 1024:1280], NT)
               + _bdot(dg_ref[...], w_ref[:, 1280:2304], NT))
        _, vjp = jax.vjp(_rms, h, nw_ref[...])
        dh, dnw = vjp(dxn)
        dnw_ref[...] += dnw
        buf_ref[slot] = dh1_ref[...] + dh

        @pl.when(i == 0)
        def _():
            dfront_ref[...] = buf_ref[0, 0:ATTN_BLOCK, :]
        for k in range(per):
            @pl.when((k > 0) | (i > 0))
            def _():
                piece(i, k, slot).start()

        @pl.when(i == n - 1)
        def _():
            for k in range(per):
                piece(i, k, slot).wait()
                piece(i - 1, k, 1 - slot).wait()

    idx = lambda i: (i, 0)
    const = lambda a: pl.BlockSpec(a.shape, functools.partial(lambda i, nd: (0,) * nd, nd=a.ndim))
    rows = [dq, dkv, dgate, dh1]
    views = _token_views(x)
    return pl.pallas_call(
        body, grid=(n,),
        in_specs=[pl.BlockSpec((ROW_BLOCK, a.shape[1]), idx) for a in rows]
        + [pl.BlockSpec((hr, a.shape[1]), fn) for a, hr, fn in views] + [const(front), const(norm_w), const(w_in)],
        out_specs=[pl.BlockSpec(memory_space=pltpu.HBM), pl.BlockSpec((ATTN_BLOCK, D_MODEL), lambda i: (0, 0)),
                   pl.BlockSpec((1, D_MODEL), lambda i: (0, 0))],
        out_shape=[jax.ShapeDtypeStruct(x.shape, F32), jax.ShapeDtypeStruct((ATTN_BLOCK, D_MODEL), F32),
                   jax.ShapeDtypeStruct((1, D_MODEL), F32)],
        scratch_shapes=[pltpu.VMEM((2, ROW_BLOCK, D_MODEL), F32), pltpu.SemaphoreType.DMA((2, per))],
        name="attn_in_bwd", compiler_params=_cparams(),
    )(*rows, *[a for a, _, _ in views], front, norm_w, w_in)


def _dn_in_fwd(h1, norm_w, w_in):
    T = h1.shape[0]

    def body(h_ref, nw_ref, w_ref, xn_ref, qkv_ref, z_ref, ba_ref):
        xn = _rms(h_ref[...], nw_ref[...]).astype(BF16)
        xn_ref[...] = xn
        qkv_ref[...] = jnp.dot(xn, w_ref[:, 0:4096], preferred_element_type=F32)
        z_ref[...] = jnp.dot(xn, w_ref[:, 4096:6144], preferred_element_type=F32)
        ba_ref[...] = jnp.dot(xn, w_ref[:, 6144:6176], preferred_element_type=F32)

    return _row_call("dn_in_fwd", body, T, ROW_BLOCK, [h1], [norm_w, w_in],
                     [(1024, BF16), (4096, F32), (2048, F32), (32, F32)])


def _shift_down(cur, prev8, s):
    i8 = lax.broadcasted_iota(jnp.int32, (8, cur.shape[1]), 0)
    r = pltpu.roll(cur, s, 0)
    head = jnp.where(i8 < s, pltpu.roll(prev8, s, 0), r[0:8])
    return jnp.concatenate([head, r[8:]], axis=0)


def _shift_up(cur, next8, s):
    n = cur.shape[0]
    i8 = lax.broadcasted_iota(jnp.int32, (8, cur.shape[1]), 0)
    r = pltpu.roll(cur, n - s, 0)
    tail = jnp.where(i8 >= 8 - s, pltpu.roll(next8, 8 - s, 0), r[n - 8:])
    return jnp.concatenate([r[:n - 8], tail], axis=0)


def _conv_taps(cur, prev8):
    return [cur] + [_shift_down(cur, prev8, s) for s in range(1, DN_CONV_K)]


def _conv_tile(taps, w):
    out = w[3:4, :] * taps[0]
    for s in range(1, DN_CONV_K):
        out = out + w[3 - s:4 - s, :] * taps[s]
    return out


def _l2n(a, scale):
    return a * (lax.rsqrt(jnp.sum(a * a, axis=-1, keepdims=True) + NORM_EPS) * scale)


def _dn_post_tile(c, t):
    a = _silu(c)
    if t < DN_K_HEADS:
        return _l2n(a, DN_HD ** -0.5)
    if t < 2 * DN_K_HEADS:
        return _l2n(a, 1.0)
    return a


def _dn_beta_g(ba, a_log, dt_bias, live):
    beta = jax.nn.sigmoid(ba[:, 0:DN_V_HEADS]) * live
    g = -jnp.exp(a_log) * _softplus(ba[:, DN_V_HEADS:] + dt_bias) * live
    return beta, g


def _live_rows(i, rb):
    rows = i * rb + lax.broadcasted_iota(jnp.int32, (rb, 1), 0)
    return (rows >= FRONT_PAD).astype(F32)


def _halo_spec_args(x, rb):
    per = rb // 8
    return (x, 8, lambda i: (jnp.maximum(i * per - 1, 0), 0))


def _dn_conv_fwd(qkv, ba, conv_w, a_log, dt_bias):
    T = qkv.shape[0]

    def body(x_ref, ba_ref, halo_ref, cw_ref, al_ref, dtb_ref, q_ref, k_ref, v_ref, bg_ref, c_ref):
        i = pl.program_id(0)
        first = (i > 0).astype(F32)
        for t in range(DN_CONV_W // 128):
            cols = slice(128 * t, 128 * (t + 1))
            c = _conv_tile(_conv_taps(x_ref[:, cols], halo_ref[:, cols] * first), cw_ref[:, cols])
            c_ref[:, cols] = c.astype(BF16)
            out = _dn_post_tile(c, t)
            if t < DN_K_HEADS:
                q_ref[:, cols] = out
            elif t < 2 * DN_K_HEADS:
                k_ref[:, 128 * (t - 8):128 * (t - 7)] = out
            else:
                v_ref[:, 128 * (t - 16):128 * (t - 15)] = out
        beta, g = _dn_beta_g(ba_ref[...], al_ref[...], dtb_ref[...], _live_rows(i, ROW_BLOCK))
        bg_ref[:, 0:DN_V_HEADS] = beta
        bg_ref[:, DN_V_HEADS:] = g

    return _row_call("dn_conv_fwd", body, T, ROW_BLOCK, [qkv, ba], [conv_w, a_log, dt_bias],
                     [(1024, F32), (1024, F32), (2048, F32), (32, F32), (4096, BF16)], halos=[_halo_spec_args(qkv, ROW_BLOCK)])


def _chunk_masks():
    r = lax.broadcasted_iota(jnp.int32, (DN_CHUNK, DN_CHUNK), 0)
    c = lax.broadcasted_iota(jnp.int32, (DN_CHUNK, DN_CHUNK), 1)
    return r >= c, r > c, r == c, r <= c


def _tri_inv_block(x):
    B = TRI_BLOCK
    n = range(len(x))
    r_, c_ = lax.broadcasted_iota(jnp.int32, (B, B), 0), lax.broadcasted_iota(jnp.int32, (B, B), 1)
    ainv = [jnp.where(r_ == c_, 1.0, 0.0) + x[h] for h in n]
    p = [_bdot(x[h], x[h]) for h in n]
    for _ in range(B.bit_length() - 3):
        r = [_bdot(jnp.concatenate([p[h], ainv[h]], axis=0), p[h]) for h in n]
        ainv = [ainv[h] + r[h][B:] for h in n]
        p = [r[h][:B] for h in n]
    return [ainv[h] + _bdot(ainv[h], p[h]) for h in n]


def _tri_inv(x):
    B = TRI_BLOCK
    assert DN_CHUNK == 2 * B
    n = len(x)
    diag = _tri_inv_block([x[h][:B, :B] for h in range(n)] + [x[h][B:, B:] for h in range(n)])
    a11, a22 = diag[:n], diag[n:]
    a21 = [_bdot(_bdot(a22[h], x[h][B:, :B]), a11[h]) for h in range(n)]
    zero = jnp.zeros((B, B), F32)
    return [jnp.concatenate([jnp.concatenate([a11[h], zero], axis=1), jnp.concatenate([a21[h], a22[h]], axis=1)], axis=0)
            for h in range(n)]


@jax.custom_vjp
def _tri_inv_known(x, a):
    return a


def _tri_inv_known_fwd(x, a):
    return a, a


def _tri_inv_known_bwd(a, da):
    return [_bdot(_bdot(a[h], da[h], TN), a[h], NT) for h in range(len(a))], [jnp.zeros_like(t) for t in a]


_tri_inv_known.defvjp(_tri_inv_known_fwd, _tri_inv_known_bwd)


@jax.custom_vjp
def _known(computed, value):
    return value


def _known_fwd(computed, value):
    return value, None


def _known_bwd(_, g):
    return g, jax.tree.map(jnp.zeros_like, g)


_known.defvjp(_known_fwd, _known_bwd)


def _dn_chunk_step(S, q, k, v, beta, g, masks, known=None):
    causal, strict, eye, upper = masks
    C, W = DN_CHUNK, DN_HD
    heads = range(len(v))
    k_t = [k[j].T for j in range(len(k))]
    qk_kk = [_bdot(jnp.concatenate([q[j], k[j]], axis=0), k_t[j]) for j in range(len(q))]
    if known is not None:
        qk_kk = _known(qk_kk, known["qk_kk"])
    g_b = [jnp.broadcast_to(g[h], (C, C)) for h in heads]
    beta_b = [jnp.broadcast_to(beta[h], (C, W)) for h in heads]
    g_row = [jnp.sum(jnp.where(eye, g_b[h], 0.0), axis=0, keepdims=True) for h in heads]
    gc_col = [jnp.sum(jnp.where(causal, g_row[h], 0.0), axis=1, keepdims=True) for h in heads]
    gc_row = [jnp.sum(jnp.where(upper, g_b[h], 0.0), axis=0, keepdims=True) for h in heads]
    g_last = [jnp.sum(g_row[h], axis=1, keepdims=True) for h in heads]
    gc_b = [jnp.broadcast_to(gc_col[h], (C, W)) for h in heads]
    decay = [jnp.exp(jnp.where(causal, gc_b[h][:, :C] - gc_row[h], NEG)) for h in heads]
    eg_b = [jnp.exp(gc_b[h]) for h in heads]
    x = [jnp.where(strict, qk_kk[h // 2][C:] * beta_b[h][:, :C] * decay[h], 0.0) * -1.0 for h in heads]
    ainv = _tri_inv(x) if known is None else _tri_inv_known(x, known["inv"])
    uw = [_bdot(ainv[h], jnp.concatenate([v[h] * beta_b[h], k[h // 2] * (beta_b[h] * eg_b[h])], axis=1)) for h in heads]
    if known is not None:
        uw = _known(uw, known["uw"])
    q_eg = [q[h // 2] * eg_b[h] for h in heads]
    attn = [qk_kk[h // 2][:C] * decay[h] for h in heads]
    k_st = [k_t[h // 2] * jnp.exp(g_last[h] - gc_row[h]) for h in heads]
    s_dec = [jnp.exp(g_last[h]) for h in heads]
    prep = (uw, q_eg, attn, k_st, s_dec)
    if S is None:
        return prep, dict(inv=ainv, uw=uw, qk_kk=qk_kk)
    s_new, o, _ = _dn_chunk_tail(S, prep, None if known is None else known["v_new"])
    return s_new, o


def _dn_chunk_tail(S, prep, known_v_new=None):
    uw, q_eg, attn, k_st, s_dec = prep
    C, W = DN_CHUNK, DN_HD
    heads = range(len(uw))
    ws_qs = [_bdot(jnp.concatenate([uw[h][:, W:], q_eg[h]], axis=0), S[h]) for h in heads]
    v_new = [uw[h][:, :W] - ws_qs[h][:C] for h in heads]
    if known_v_new is not None:
        v_new = _known(v_new, known_v_new)
    o = [ws_qs[h][C:] + _bdot(attn[h], v_new[h]) for h in heads]
    s_new = [S[h] * s_dec[h] + _bdot(k_st[h], v_new[h]) for h in heads]
    return s_new, o, v_new


def _dn_chunk_tiles(q_ref, k_ref, v_ref, bg_ref, c, first, count):
    rows = slice(DN_CHUNK * c, DN_CHUNK * (c + 1))
    q = [q_ref[rows, 128 * j:128 * (j + 1)] for j in range(first // 2, (first + count) // 2)]
    k = [k_ref[rows, 128 * j:128 * (j + 1)] for j in range(first // 2, (first + count) // 2)]
    v = [v_ref[rows, 128 * h:128 * (h + 1)] for h in range(first, first + count)]
    beta = [bg_ref[rows, h:h + 1] for h in range(first, first + count)]
    g = [bg_ref[rows, DN_V_HEADS + h:DN_V_HEADS + h + 1] for h in range(first, first + count)]
    return q, k, v, beta, g


def _dn_scan_fwd(qn, kn, v, bg):
    T = qn.shape[0]
    nc = T // DN_CHUNK
    rows = SCAN_CHUNKS * DN_CHUNK
    assert nc % SCAN_CHUNKS == 0

    def body(q_ref, k_ref, v_ref, bg_ref, o_ref, ssave_ref, inv_ref, uw_ref, vn_ref, qk_ref, s_ref):
        @pl.when(pl.program_id(0) == 0)
        def _():
            s_ref[...] = jnp.zeros_like(s_ref)
        masks = _chunk_masks()
        for first in range(0, DN_V_HEADS, SCAN_FWD_GROUP):
            heads = range(first, first + SCAN_FWD_GROUP)
            preps = [_dn_chunk_step(None, *_dn_chunk_tiles(q_ref, k_ref, v_ref, bg_ref, c, first, SCAN_FWD_GROUP), masks)
                     for c in range(SCAN_CHUNKS)]
            state = [s_ref[h] for h in heads]
            for c, (prep, saved) in enumerate(preps):
                for i, h in enumerate(heads):
                    ssave_ref[c, h] = state[i]
                    inv_ref[c, h] = saved["inv"][i].astype(BF16)
                    uw_ref[c, h] = saved["uw"][i].astype(BF16)
                for i, j in enumerate(range(first // 2, (first + SCAN_FWD_GROUP) // 2)):
                    qk_ref[c, j] = saved["qk_kk"][i].astype(BF16)
                state, o, v_new = _dn_chunk_tail(state, prep)
                for i, h in enumerate(heads):
                    o_ref[DN_CHUNK * c:DN_CHUNK * (c + 1), 128 * h:128 * (h + 1)] = o[i]
                    vn_ref[c, h] = v_new[i].astype(BF16)
            for i, h in enumerate(heads):
                s_ref[h] = state[i]

    return pl.pallas_call(
        body, grid=(nc // SCAN_CHUNKS,),
        in_specs=[pl.BlockSpec((rows, 1024), lambda i: (i, 0)),
                  pl.BlockSpec((rows, 1024), lambda i: (i, 0)),
                  pl.BlockSpec((rows, 2048), lambda i: (i, 0)),
                  pl.BlockSpec((rows, 32), lambda i: (i, 0))],
        out_specs=[pl.BlockSpec((rows, 2048), lambda i: (i, 0)),
                   pl.BlockSpec((SCAN_CHUNKS, DN_V_HEADS, DN_HD, DN_HD), lambda i: (i, 0, 0, 0)),
                   pl.BlockSpec((SCAN_CHUNKS, DN_V_HEADS, DN_CHUNK, DN_CHUNK), lambda i: (i, 0, 0, 0)),
                   pl.BlockSpec((SCAN_CHUNKS, DN_V_HEADS, DN_CHUNK, 2 * DN_HD), lambda i: (i, 0, 0, 0)),
                   pl.BlockSpec((SCAN_CHUNKS, DN_V_HEADS, DN_CHUNK, DN_HD), lambda i: (i, 0, 0, 0)),
                   pl.BlockSpec((SCAN_CHUNKS, DN_K_HEADS, 2 * DN_CHUNK, DN_CHUNK), lambda i: (i, 0, 0, 0))],
        out_shape=[jax.ShapeDtypeStruct((T, 2048), F32),
                   jax.ShapeDtypeStruct((nc, DN_V_HEADS, DN_HD, DN_HD), F32),
                   jax.ShapeDtypeStruct((nc, DN_V_HEADS, DN_CHUNK, DN_CHUNK), BF16),
                   jax.ShapeDtypeStruct((nc, DN_V_HEADS, DN_CHUNK, 2 * DN_HD), BF16),
                   jax.ShapeDtypeStruct((nc, DN_V_HEADS, DN_CHUNK, DN_HD), BF16),
                   jax.ShapeDtypeStruct((nc, DN_K_HEADS, 2 * DN_CHUNK, DN_CHUNK), BF16)],
        scratch_shapes=[pltpu.VMEM((DN_V_HEADS, DN_HD, DN_HD), F32)],
        name="dn_scan_fwd", compiler_params=_cparams(),
    )(qn, kn, v, bg)


def _dn_gate_tile(o, z, onw):
    return _rms(o, onw) * _silu(z)


def _dn_out_fwd(o, z, h1, target, w_out, onw):
    T = o.shape[0]

    def body(o_ref, z_ref, h_ref, ta_ref, tb_ref, tc_ref, w_ref, onw_ref, dy_ref, og_ref, loss_ref):
        i = pl.program_id(0)

        @pl.when(i == 0)
        def _():
            loss_ref[...] = jnp.zeros_like(loss_ref)
        for h in range(DN_V_HEADS):
            cols = slice(128 * h, 128 * (h + 1))
            og_ref[:, cols] = _dn_gate_tile(o_ref[:, cols], z_ref[:, cols], onw_ref[...]).astype(BF16)
        y = h_ref[...] + jnp.dot(og_ref[...], w_ref[...], preferred_element_type=F32)
        rows = i * ROW_BLOCK + lax.broadcasted_iota(jnp.int32, (ROW_BLOCK, 1), 0)
        diff = jnp.where(rows >= FRONT_PAD + N_META, y - _padded_block(i, None, (ta_ref, tb_ref, tc_ref)), 0.0)
        dy_ref[...] = diff * (1.0 / D_MODEL)
        loss_ref[...] += jnp.sum(diff * diff) * (0.5 / D_MODEL)

    return _row_call("dn_out_fwd", body, T, ROW_BLOCK, [o, z, h1], [w_out, onw],
                     [(1024, F32), (2048, BF16)], [((1, 128), F32)], halos=_token_views(target))


def _dn_out_bwd(dy, o, z, w_out, onw):
    T = o.shape[0]

    def body(dy_ref, o_ref, z_ref, w_ref, onw_ref, do_ref, dz_ref, donw_ref, dog_ref):
        @pl.when(pl.program_id(0) == 0)
        def _():
            donw_ref[...] = jnp.zeros_like(donw_ref)
        dy = dy_ref[...].astype(BF16)
        donw = jnp.zeros((1, DN_HD), F32)
        for half in range(2):
            hcols = slice(1024 * half, 1024 * (half + 1))
            dog_ref[:, hcols] = lax.dot_general(dy, w_ref[hcols, :], NT, preferred_element_type=F32)
        for h in range(DN_V_HEADS):
            cols = slice(128 * h, 128 * (h + 1))
            _, vjp = jax.vjp(_dn_gate_tile, o_ref[:, cols], z_ref[:, cols], onw_ref[...])
            do, dz, dn = vjp(dog_ref[:, cols])
            do_ref[:, cols] = do
            dz_ref[:, cols] = dz
            donw = donw + dn
        donw_ref[...] += donw

    return _row_call("dn_out_bwd", body, T, ROW_BLOCK, [dy, o, z], [w_out, onw],
                     [(2048, F32), (2048, F32)], [((1, DN_HD), F32)], scratch=[pltpu.VMEM((ROW_BLOCK, 2048), F32)])


def _dn_scan_bwd(do, qn, kn, v, bg, ssave, saved):
    T = qn.shape[0]
    nc = T // DN_CHUNK
    chunks = SCAN_BWD_CHUNKS
    ns = nc // chunks
    rows = chunks * DN_CHUNK
    rev = lambda i: ns - 1 - i

    def body(do_ref, q_ref, k_ref, v_ref, bg_ref, ss_ref, inv_ref, uw_ref, vn_ref, qk_ref,
             dq_ref, dk_ref, dv_ref, dbg_ref, ds_ref):
        @pl.when(pl.program_id(0) == 0)
        def _():
            ds_ref[...] = jnp.zeros_like(ds_ref)
        lane32 = lax.broadcasted_iota(jnp.int32, (1, 2 * DN_V_HEADS), 1)
        masks = _chunk_masks()
        dbg = [jnp.zeros((DN_CHUNK, 2 * DN_V_HEADS), F32) for _ in range(chunks)]
        for first in range(0, DN_V_HEADS, SCAN_BWD_GROUP):
            heads = range(first, first + SCAN_BWD_GROUP)
            vjps = []
            for c in range(chunks):
                known = dict(inv=[inv_ref[c, h].astype(F32) for h in heads], uw=[uw_ref[c, h].astype(F32) for h in heads],
                             v_new=[vn_ref[c, h].astype(F32) for h in heads],
                             qk_kk=[qk_ref[c, j].astype(F32) for j in range(first // 2, (first + SCAN_BWD_GROUP) // 2)])
                fn = functools.partial(_dn_chunk_step, masks=masks, known=known)
                vjps.append(jax.vjp(fn, [ss_ref[c, h] for h in heads],
                                    *_dn_chunk_tiles(q_ref, k_ref, v_ref, bg_ref, c, first, SCAN_BWD_GROUP))[1])
            ds = [ds_ref[h] for h in heads]
            for c in reversed(range(chunks)):
                crows = slice(DN_CHUNK * c, DN_CHUNK * (c + 1))
                ds, dq, dk, dv, dbeta, dg = vjps[c]((ds, [do_ref[crows, 128 * h:128 * (h + 1)] for h in heads]))
                for i, h in enumerate(heads):
                    dv_ref[crows, 128 * h:128 * (h + 1)] = dv[i]
                    dbg[c] = dbg[c] + jnp.where(lane32 == h, dbeta[i], 0.0) + jnp.where(lane32 == DN_V_HEADS + h, dg[i], 0.0)
                for i, j in enumerate(range(first // 2, (first + SCAN_BWD_GROUP) // 2)):
                    dq_ref[crows, 128 * j:128 * (j + 1)] = dq[i]
                    dk_ref[crows, 128 * j:128 * (j + 1)] = dk[i]
            for i, h in enumerate(heads):
                ds_ref[h] = ds[i]
        for c in range(chunks):
            dbg_ref[DN_CHUNK * c:DN_CHUNK * (c + 1), :] = dbg[c]

    return pl.pallas_call(
        body, grid=(ns,),
        in_specs=[pl.BlockSpec((rows, 2048), lambda i: (rev(i), 0)),
                  pl.BlockSpec((rows, 1024), lambda i: (rev(i), 0)),
                  pl.BlockSpec((rows, 1024), lambda i: (rev(i), 0)),
                  pl.BlockSpec((rows, 2048), lambda i: (rev(i), 0)),
                  pl.BlockSpec((rows, 32), lambda i: (rev(i), 0)),
                  pl.BlockSpec((chunks, DN_V_HEADS, DN_HD, DN_HD), lambda i: (rev(i), 0, 0, 0)),
                  pl.BlockSpec((chunks, DN_V_HEADS, DN_CHUNK, DN_CHUNK), lambda i: (rev(i), 0, 0, 0)),
                  pl.BlockSpec((chunks, DN_V_HEADS, DN_CHUNK, 2 * DN_HD), lambda i: (rev(i), 0, 0, 0)),
                  pl.BlockSpec((chunks, DN_V_HEADS, DN_CHUNK, DN_HD), lambda i: (rev(i), 0, 0, 0)),
                  pl.BlockSpec((chunks, DN_K_HEADS, 2 * DN_CHUNK, DN_CHUNK), lambda i: (rev(i), 0, 0, 0))],
        out_specs=[pl.BlockSpec((rows, 1024), lambda i: (rev(i), 0)),
                   pl.BlockSpec((rows, 1024), lambda i: (rev(i), 0)),
                   pl.BlockSpec((rows, 2048), lambda i: (rev(i), 0)),
                   pl.BlockSpec((rows, 32), lambda i: (rev(i), 0))],
        out_shape=[jax.ShapeDtypeStruct((T, 1024), F32), jax.ShapeDtypeStruct((T, 1024), F32),
                   jax.ShapeDtypeStruct((T, 2048), F32), jax.ShapeDtypeStruct((T, 32), F32)],
        scratch_shapes=[pltpu.VMEM((DN_V_HEADS, DN_HD, DN_HD), F32)],
        name="dn_scan_bwd", compiler_params=_cparams(),
    )(do, qn, kn, v, bg, ssave, *saved)


def _dn_conv_bwd(dqn, dkn, dv, dbg, qkv, conv_out, ba, conv_w, a_log, dt_bias):
    T = qkv.shape[0]
    rb = ROW_BLOCK // 2
    nr = T // rb

    def body(dq_ref, dk_ref, dv_ref, dbg_ref, x_ref, c_ref, ba_ref, cw_ref, al_ref, dtb_ref,
             dx_ref, dba_ref, dcw_ref, dal_ref, ddtb_ref, carry_ref):
        step = pl.program_id(0)
        i = nr - 1 - step

        @pl.when(step == 0)
        def _():
            carry_ref[...] = jnp.zeros_like(carry_ref)
            dcw_ref[...] = jnp.zeros_like(dcw_ref)
            dal_ref[...] = jnp.zeros_like(dal_ref)
            ddtb_ref[...] = jnp.zeros_like(ddtb_ref)
        for t in range(DN_CONV_W // 128):
            cols = slice(128 * t, 128 * (t + 1))
            w, x = cw_ref[:, cols], x_ref[:, cols]
            if t < DN_K_HEADS:
                dout = dq_ref[:, cols]
            elif t < 2 * DN_K_HEADS:
                dout = dk_ref[:, 128 * (t - 8):128 * (t - 7)]
            else:
                dout = dv_ref[:, 128 * (t - 16):128 * (t - 15)]
            _, vjp = jax.vjp(functools.partial(_dn_post_tile, t=t), c_ref[:, cols].astype(F32))
            (dc,) = vjp(dout)
            nxt = carry_ref[:, cols]
            dx = w[3:4, :] * dc
            dcw_ref[3:4, cols] += jnp.sum(dc * x, axis=0, keepdims=True)
            for s in range(1, DN_CONV_K):
                up = _shift_up(dc, nxt, s)
                dx = dx + w[3 - s:4 - s, :] * up
                dcw_ref[3 - s:4 - s, cols] += jnp.sum(up * x, axis=0, keepdims=True)
            dx_ref[:, cols] = dx
            carry_ref[:, cols] = dc[0:8, :]
        fn = functools.partial(_dn_beta_g, live=_live_rows(i, rb))
        _, vjp = jax.vjp(fn, ba_ref[...], al_ref[...], dtb_ref[...])
        dba, dal, ddtb = vjp((dbg_ref[:, 0:DN_V_HEADS], dbg_ref[:, DN_V_HEADS:]))
        dba_ref[...] = dba
        dal_ref[...] += dal
        ddtb_ref[...] += ddtb

    return _row_call("dn_conv_bwd", body, T, rb, [dqn, dkn, dv, dbg, qkv, conv_out, ba], [conv_w, a_log, dt_bias],
                     [(4096, F32), (32, F32)], [((DN_CONV_K, 4096), F32), ((1, DN_V_HEADS), F32), ((1, DN_V_HEADS), F32)],
                     reverse=True, scratch=[pltpu.VMEM((8, 4096), F32)])


def _dn_in_bwd(dqkv, dz, dba, h1, dy, norm_w, w_in):
    T = h1.shape[0]

    def body(dqkv_ref, dz_ref, dba_ref, h_ref, dy_ref, nw_ref, w_ref, dh_ref, dnw_ref):
        @pl.when(pl.program_id(0) == 0)
        def _():
            dnw_ref[...] = jnp.zeros_like(dnw_ref)
        dxn = (_bdot(dqkv_ref[...], w_ref[:, 0:4096], NT) + _bdot(dz_ref[...], w_ref[:, 4096:6144], NT)
               + _bdot(dba_ref[...], w_ref[:, 6144:6176], NT))
        _, vjp = jax.vjp(_rms, h_ref[...], nw_ref[...])
        dh, dnw = vjp(dxn)
        dh_ref[...] = (dy_ref[...] + dh) * _live_rows(pl.program_id(0), ROW_BLOCK)
        dnw_ref[...] += dnw

    return _row_call("dn_in_bwd", body, T, ROW_BLOCK, [dqkv, dz, dba, h1, dy], [norm_w, w_in],
                     [(1024, F32)], [((1, 1024), F32)])


def _exchange(parts, scatter, name):
    n = len(parts)
    out_shape = [jax.ShapeDtypeStruct(p.shape if sc else (N_DEV,) + p.shape, p.dtype) for p, sc in zip(parts, scatter)]

    def body(*refs):
        ins, outs = refs[:n], refs[n:2 * n]
        send_sems, recv_sems, local_sems = refs[2 * n:]
        x, y, c = lax.axis_index("x"), lax.axis_index("y"), lax.axis_index("c")
        me = 4 * x + 2 * y + c
        peers = []
        for k in range(1, N_DEV):
            px = 1 - x if k & 4 else x
            py = 1 - y if k & 2 else y
            pc = 1 - c if k & 1 else c
            peers.append(((px, py, pc), 4 * px + 2 * py + pc))

        def src(a, idx):
            return ins[a].at[idx] if scatter[a] else ins[a]

        local = [pltpu.make_async_copy(src(a, me), outs[a].at[me], local_sems.at[a]) for a in range(n)]
        for cp in local:
            cp.start()
        for a in range(n):
            for k, (dev, idx) in enumerate(peers):
                pltpu.make_async_remote_copy(
                    src_ref=src(a, idx), dst_ref=outs[a].at[me], send_sem=send_sems.at[a, k], recv_sem=recv_sems.at[a, k],
                    device_id=dev, device_id_type=pl.DeviceIdType.MESH).start()
        for a in range(n):
            for k, (dev, idx) in enumerate(peers):
                pltpu.make_async_remote_copy(
                    src_ref=src(a, idx), dst_ref=outs[a].at[idx], send_sem=send_sems.at[a, k], recv_sem=recv_sems.at[a, k],
                    device_id=dev, device_id_type=pl.DeviceIdType.MESH).wait()
        for cp in local:
            cp.wait()

    hbm = pl.BlockSpec(memory_space=pltpu.HBM)
    return pl.pallas_call(
        body, out_shape=out_shape, in_specs=[hbm] * n, out_specs=[hbm] * n,
        scratch_shapes=[pltpu.SemaphoreType.DMA((n, N_DEV - 1)), pltpu.SemaphoreType.DMA((n, N_DEV - 1)),
                        pltpu.SemaphoreType.DMA((n,))],
        name=name,
    )(*parts)


def _gather_two_level(parts, name):
    n = len(parts)
    out_shape = [jax.ShapeDtypeStruct((N_DEV,) + p.shape, p.dtype) for p in parts]

    def body(*refs):
        ins, outs = refs[:n], refs[n:2 * n]
        send_sems, recv_sems, local_sems = refs[2 * n:]
        x, y, c = lax.axis_index("x"), lax.axis_index("y"), lax.axis_index("c")
        idx = lambda px, py, pc: 4 * px + 2 * py + pc
        me, sibling = (x, y, c), (x, y, 1 - c)
        chips = [(1 - x, y), (x, 1 - y), (1 - x, 1 - y)]

        def copy(a, k, block, to, src=None):
            slot = outs[a].at[idx(*block)]
            return pltpu.make_async_remote_copy(
                src_ref=slot if src is None else src, dst_ref=slot, send_sem=send_sems.at[a, k], recv_sem=recv_sems.at[a, k],
                device_id=to, device_id_type=pl.DeviceIdType.MESH)

        local = [pltpu.make_async_copy(ins[a], outs[a].at[idx(*me)], local_sems.at[a]) for a in range(n)]
        for cp in local:
            cp.start()
        sent = []
        for a in range(n):
            sent.append(copy(a, 0, me, sibling, src=ins[a]))
            sent += [copy(a, 1 + j, me, (*chip, c), src=ins[a]) for j, chip in enumerate(chips)]
        for cp in sent:
            cp.start()
        for a in range(n):
            for j, chip in enumerate(chips):
                copy(a, 1 + j, (*chip, c), me).wait_recv()
                passed = copy(a, 4 + j, (*chip, c), sibling)
                passed.start()
                sent.append(passed)
        for a in range(n):
            copy(a, 0, sibling, me).wait_recv()
            for j, chip in enumerate(chips):
                copy(a, 4 + j, (*chip, 1 - c), me).wait_recv()
        for cp in sent:
            cp.wait_send()
        for cp in local:
            cp.wait()

    hbm = pl.BlockSpec(memory_space=pltpu.HBM)
    return pl.pallas_call(
        body, out_shape=out_shape, in_specs=[hbm] * n, out_specs=[hbm] * n,
        scratch_shapes=[pltpu.SemaphoreType.DMA((n, N_DEV - 1)), pltpu.SemaphoreType.DMA((n, N_DEV - 1)),
                        pltpu.SemaphoreType.DMA((n,))],
        name=name,
    )(*parts)


def _swap_with_sibling(parts, name):
    n = len(parts)

    def body(*refs):
        ins, outs = refs[:n], refs[n:2 * n]
        send_sems, recv_sems = refs[2 * n:]
        x, y, c = lax.axis_index("x"), lax.axis_index("y"), lax.axis_index("c")
        copies = [pltpu.make_async_remote_copy(
            src_ref=ins[a].at[1 - c], dst_ref=outs[a], send_sem=send_sems.at[a], recv_sem=recv_sems.at[a],
            device_id=(x, y, 1 - c), device_id_type=pl.DeviceIdType.MESH) for a in range(n)]
        for cp in copies:
            cp.start()
        for cp in copies:
            cp.wait()

    hbm = pl.BlockSpec(memory_space=pltpu.HBM)
    return pl.pallas_call(
        body, out_shape=[jax.ShapeDtypeStruct(p.shape[1:], p.dtype) for p in parts], in_specs=[hbm] * n, out_specs=[hbm] * n,
        scratch_shapes=[pltpu.SemaphoreType.DMA((n,)), pltpu.SemaphoreType.DMA((n,))],
        name=name,
    )(*parts)


def _pair_sum(a, b, name):
    R, C = a.shape
    rb = _adam_rows(R)

    def body(a_ref, b_ref, o_ref):
        o_ref[...] = (a_ref[...].astype(F32) + b_ref[...].astype(F32)).astype(BF16)

    blk = pl.BlockSpec((rb, C), lambda i: (i, 0))
    return pl.pallas_call(body, grid=(R // rb,), in_specs=[blk, blk], out_specs=blk,
                          out_shape=jax.ShapeDtypeStruct((R, C), BF16), name=name, compiler_params=_cparams())(a, b)


def _exchange_chips(parts, name):
    n = len(parts)
    n_chips = N_DEV // 2

    def body(*refs):
        ins, outs = refs[:n], refs[n:2 * n]
        send_sems, recv_sems, local_sems = refs[2 * n:]
        x, y, c = lax.axis_index("x"), lax.axis_index("y"), lax.axis_index("c")
        mine = 2 * x + y
        chips = [(1 - x, y), (x, 1 - y), (1 - x, 1 - y)]
        local = [pltpu.make_async_copy(ins[a].at[mine], outs[a].at[mine], local_sems.at[a]) for a in range(n)]
        for cp in local:
            cp.start()
        for a in range(n):
            for k, (px, py) in enumerate(chips):
                pltpu.make_async_remote_copy(
                    src_ref=ins[a].at[2 * px + py], dst_ref=outs[a].at[mine], send_sem=send_sems.at[a, k],
                    recv_sem=recv_sems.at[a, k], device_id=(px, py, c), device_id_type=pl.DeviceIdType.MESH).start()
        for a in range(n):
            for k, (px, py) in enumerate(chips):
                pltpu.make_async_remote_copy(
                    src_ref=ins[a].at[2 * px + py], dst_ref=outs[a].at[2 * px + py], send_sem=send_sems.at[a, k],
                    recv_sem=recv_sems.at[a, k], device_id=(px, py, c), device_id_type=pl.DeviceIdType.MESH).wait()
        for cp in local:
            cp.wait()

    hbm = pl.BlockSpec(memory_space=pltpu.HBM)
    return pl.pallas_call(
        body, out_shape=[jax.ShapeDtypeStruct(p.shape, p.dtype) for p in parts], in_specs=[hbm] * n, out_specs=[hbm] * n,
        scratch_shapes=[pltpu.SemaphoreType.DMA((n, n_chips - 1)), pltpu.SemaphoreType.DMA((n, n_chips - 1)),
                        pltpu.SemaphoreType.DMA((n,))],
        name=name,
    )(*parts)


def _adam_rows(rows):
    for rb in (128, 64, 40, 16, 8):
        if rows % rb == 0:
            return rb
    return rows


def _adamw(stack, w, m, v, name):
    R, C = w.shape
    rb = _adam_rows(R)
    slots = stack.shape[0]

    def body(s_ref, w_ref, m_ref, v_ref, g_ref, d_ref, nm_ref, nv_ref):
        g = s_ref[0].astype(F32)
        for s in range(1, slots):
            g = g + s_ref[s].astype(F32)
        nm = ADAM_B1 * m_ref[...] + (1.0 - ADAM_B1) * g
        nv = ADAM_B2 * v_ref[...] + (1.0 - ADAM_B2) * (g * g)
        m_hat = nm / (1.0 - ADAM_B1 ** ADAM_STEP)
        v_hat = nv / (1.0 - ADAM_B2 ** ADAM_STEP)
        g_ref[...] = g
        d_ref[...] = -ADAM_LR * (m_hat / (jnp.sqrt(v_hat) + ADAM_EPS) + ADAM_WD * w_ref[...])
        nm_ref[...] = nm
        nv_ref[...] = nv

    blk = pl.BlockSpec((rb, C), lambda i: (i, 0))
    return pl.pallas_call(
        body, grid=(R // rb,),
        in_specs=[pl.BlockSpec((slots, rb, C), lambda i: (0, i, 0)), blk, blk, blk],
        out_specs=[blk] * 4, out_shape=[jax.ShapeDtypeStruct((R, C), F32)] * 4,
        name=name, compiler_params=_cparams(),
    )(stack, w, m, v)


def _pad_rows8(a):
    return jnp.concatenate([a, jnp.zeros((8 - a.shape[0], a.shape[1]), a.dtype)], axis=0) if a.shape[0] < 8 else a


def _pack_small(norm_w, qnw, knw, sinks, a_log, dt_bias, onw, extra):
    z = lambda n: jnp.zeros((1, n), F32)
    row = jnp.concatenate([norm_w, qnw, knw, sinks, a_log, dt_bias, z(80), onw, extra, z(512)], axis=1)
    return row.reshape(16, 128)


def _unpack_small(p):
    row = p.reshape(1, 2048)
    cut = lambda a, n: row[:, a:a + n]
    return (cut(0, 1024), cut(1024, 64), cut(1088, 64), cut(1152, 16), cut(1168, 16), cut(1184, 16), cut(1280, 128),
            cut(1408, 128))


def _pack_rows(w_in_a, w_in_d, w_out_a, w_out_d, meta, conv, dn_norm):
    a = jnp.concatenate([w_in_a, w_in_d], axis=1)
    b = jnp.concatenate([w_out_a, w_out_d], axis=0)
    c = jnp.concatenate([meta, conv.reshape(16, 128), _pad_rows8(dn_norm)], axis=0)
    return a, b, c


def _unpack_rows(a, b, c):
    return (a[:, :288], a[:, 288:], b[:128], b[128:], c[:16], c[16:32].reshape(4, 512), c[32:33])


def _local_step(x, front, target, w):
    xn0, q, kv, gate = _attn_in_fwd(x, front, w["attn_norm_w"], w["attn_w_in"])
    o = _attn_core_fwd(q, kv, w["attn_sinks"], w["attn_q_norm_w"], w["attn_k_norm_w"])
    h1 = _attn_out_fwd(o, gate, x, front, w["attn_w_out"])
    xn1, qkv, z, ba = _dn_in_fwd(h1, w["dn_norm_w"], w["dn_w_in"])
    qn, kn, v, bg, conv_out = _dn_conv_fwd(qkv, ba, w["dn_conv_w"], w["dn_a_log"], w["dn_dt_bias"])
    o_dn, ssave, *saved = _dn_scan_fwd(qn, kn, v, bg)
    dy, og_dn, loss = _dn_out_fwd(o_dn, z, h1, target, w["dn_w_out"], w["dn_o_norm_w"])

    g = {}
    do_dn, dz, g["dn_o_norm_w"] = _dn_out_bwd(dy, o_dn, z, w["dn_w_out"], w["dn_o_norm_w"])
    g["dn_w_out"] = _wgrad(og_dn, dy, 1024, "wgrad_dn_out")
    dqn, dkn, dv, dbg = _dn_scan_bwd(do_dn, qn, kn, v, bg, ssave, saved)
    dqkv, dba, g["dn_conv_w"], g["dn_a_log"], g["dn_dt_bias"] = _dn_conv_bwd(
        dqn, dkn, dv, dbg, qkv, conv_out, ba, w["dn_conv_w"], w["dn_a_log"], w["dn_dt_bias"])
    dh1, g["dn_norm_w"] = _dn_in_bwd(dqkv, dz, dba, h1, dy, w["dn_norm_w"], w["dn_w_in"])
    g["dn_w_in"] = jnp.concatenate([_wgrad(xn1, dqkv, 1024, "wgrad_dn_qkv"), _wgrad(xn1, dz, 1024, "wgrad_dn_z"),
                                    _wgrad(xn1, dba, 32, "wgrad_dn_ba")], axis=1)
    do, dgate, g["attn_w_out"] = _attn_out_bwd(dh1, o, gate, w["attn_w_out"])
    dq, dkv, g["attn_sinks"], g["attn_q_norm_w"], g["attn_k_norm_w"] = _attn_core_bwd(
        do, q, kv, w["attn_sinks"], w["attn_q_norm_w"], w["attn_k_norm_w"])
    grad_x, dfront, g["attn_norm_w"] = _attn_in_bwd(dq, dkv, dgate, x, front, dh1, w["attn_norm_w"], w["attn_w_in"])
    g["meta_tokens"] = dfront[FRONT_PAD:]
    g["attn_w_in"] = jnp.concatenate([_wgrad(xn0, dq, 1024, "wgrad_attn_q"), _wgrad(xn0, dkv, 256, "wgrad_attn_kv"),
                                      _wgrad(xn0, dgate, 1024, "wgrad_attn_gate")], axis=1)
    return loss, grad_x, g


WEIGHTS = ['meta_tokens', 'attn_norm_w', 'attn_w_in', 'attn_q_norm_w', 'attn_k_norm_w', 'attn_sinks', 'attn_w_out',
           'dn_norm_w', 'dn_w_in', 'dn_conv_w', 'dn_a_log', 'dn_dt_bias', 'dn_o_norm_w', 'dn_w_out']
SMALL = ['attn_norm_w', 'attn_q_norm_w', 'attn_k_norm_w', 'attn_sinks', 'dn_a_log', 'dn_dt_bias', 'dn_o_norm_w']


def kernel(x, meta_tokens, attn_norm_w, attn_w_in, attn_q_norm_w, attn_k_norm_w, attn_sinks, attn_w_out, dn_norm_w, dn_w_in, dn_conv_w, dn_a_log, dn_dt_bias, dn_o_norm_w, dn_w_out, loss_target, m_meta_tokens, m_attn_norm_w, m_attn_w_in, m_attn_q_norm_w, m_attn_k_norm_w, m_attn_sinks, m_attn_w_out, m_dn_norm_w, m_dn_w_in, m_dn_conv_w, m_dn_a_log, m_dn_dt_bias, m_dn_o_norm_w, m_dn_w_out, v_meta_tokens, v_attn_norm_w, v_attn_w_in, v_attn_q_norm_w, v_attn_k_norm_w, v_attn_sinks, v_attn_w_out, v_dn_norm_w, v_dn_w_in, v_dn_conv_w, v_dn_a_log, v_dn_dt_bias, v_dn_o_norm_w, v_dn_w_out):
    shard = dict(meta_tokens=meta_tokens, attn_norm_w=attn_norm_w, attn_w_in=attn_w_in[0], attn_q_norm_w=attn_q_norm_w,
                 attn_k_norm_w=attn_k_norm_w, attn_sinks=attn_sinks, attn_w_out=attn_w_out[0], dn_norm_w=dn_norm_w,
                 dn_w_in=dn_w_in[0], dn_conv_w=dn_conv_w[0], dn_a_log=dn_a_log, dn_dt_bias=dn_dt_bias,
                 dn_o_norm_w=dn_o_norm_w, dn_w_out=dn_w_out[0])
    mom_m = dict(meta_tokens=m_meta_tokens, attn_norm_w=m_attn_norm_w, attn_w_in=m_attn_w_in[0], attn_q_norm_w=m_attn_q_norm_w,
                 attn_k_norm_w=m_attn_k_norm_w, attn_sinks=m_attn_sinks, attn_w_out=m_attn_w_out[0], dn_norm_w=m_dn_norm_w,
                 dn_w_in=m_dn_w_in[0], dn_conv_w=m_dn_conv_w[0], dn_a_log=m_dn_a_log, dn_dt_bias=m_dn_dt_bias,
                 dn_o_norm_w=m_dn_o_norm_w, dn_w_out=m_dn_w_out[0])
    mom_v = dict(meta_tokens=v_meta_tokens, attn_norm_w=v_attn_norm_w, attn_w_in=v_attn_w_in[0], attn_q_norm_w=v_attn_q_norm_w,
                 attn_k_norm_w=v_attn_k_norm_w, attn_sinks=v_attn_sinks, attn_w_out=v_attn_w_out[0], dn_norm_w=v_dn_norm_w,
                 dn_w_in=v_dn_w_in[0], dn_conv_w=v_dn_conv_w[0], dn_a_log=v_dn_a_log, dn_dt_bias=v_dn_dt_bias,
                 dn_o_norm_w=v_dn_o_norm_w, dn_w_out=v_dn_w_out[0])

    def rows_of(d):
        return _pack_rows(d["attn_w_in"], d["dn_w_in"], d["attn_w_out"], d["dn_w_out"], d["meta_tokens"], d["dn_conv_w"],
                          d["dn_norm_w"])

    def small_of(d, extra):
        return _pack_small(*[d[k] for k in SMALL], extra)

    wa, wb, wc = rows_of(shard)
    ga, gb, gc = _gather_two_level([wa.astype(BF16), wb.astype(BF16), wc], "gather_weights")
    full = {k: shard[k] for k in SMALL}
    full["attn_w_in"] = ga[:, :, :288].transpose(1, 0, 2).reshape(1024, 2304)
    full["dn_w_in"] = ga[:, :, 288:].transpose(1, 0, 2).reshape(1024, 6176)
    full["attn_w_out"] = gb[:, :128].reshape(1024, 1024)
    full["dn_w_out"] = gb[:, 128:].reshape(2048, 1024)
    meta_full = gc[:, :16].transpose(1, 0, 2).reshape(N_META, 1024)
    full["dn_conv_w"] = gc[:, 16:32].reshape(N_DEV, 4, 512).transpose(1, 0, 2).reshape(4, 4096)
    full["dn_norm_w"] = gc[:, 32].reshape(1, 1024)

    front = jnp.concatenate([jnp.zeros((FRONT_PAD, D_MODEL), F32), meta_full], axis=0)
    loss, grad_x, g = _local_step(x[0], front, loss_target[0], full)
    grad_x = grad_x[None]

    pa = jnp.concatenate([g["attn_w_in"].reshape(1024, N_DEV, 288), g["dn_w_in"].reshape(1024, N_DEV, 772)],
                         axis=2).transpose(1, 0, 2)
    pb = jnp.concatenate([g["attn_w_out"].reshape(N_DEV, 128, 1024), g["dn_w_out"].reshape(N_DEV, 256, 1024)], axis=1)
    dn_norm8 = jnp.concatenate([g["dn_norm_w"].reshape(N_DEV, 1, 128), jnp.zeros((N_DEV, 7, 128), F32)], axis=1)
    pc = jnp.concatenate([g["meta_tokens"].reshape(N_META, N_DEV, 128).transpose(1, 0, 2),
                          g["dn_conv_w"].reshape(4, N_DEV, 512).transpose(1, 0, 2).reshape(N_DEV, 16, 128), dn_norm8], axis=1)
    ps = small_of(g, loss)
    c = lax.axis_index("c")
    by_core = lambda p: p.astype(BF16).reshape((N_DEV // 2, 2) + p.shape[1:]).swapaxes(0, 1)
    pa2, pb2 = by_core(pa), by_core(pb)
    ra, rb_ = _swap_with_sibling([pa2, pb2], "swap_grads")
    own = lambda p2: lax.dynamic_index_in_dim(p2, c, axis=0, keepdims=False)
    flat = lambda t: t.reshape((-1,) + t.shape[2:])
    sa = _pair_sum(flat(own(pa2)), flat(ra), "pair_sum_a").reshape(ra.shape)
    sb = _pair_sum(flat(own(pb2)), flat(rb_), "pair_sum_b").reshape(rb_.shape)
    xa, xb = _exchange_chips([sa, sb], "exchange_grads")
    xc, xs = _exchange([pc, ps], [True, False], "exchange_small")

    out = {}
    ma, mb, mc = rows_of(mom_m)
    va, vb, vc = rows_of(mom_v)
    ra = _adamw(xa, wa, ma, va, "adamw_a")
    rb = _adamw(xb, wb, mb, vb, "adamw_b")
    rc = _adamw(xc, wc, mc, vc, "adamw_c")
    zero = jnp.zeros((1, 128), F32)
    rs = _adamw(xs, small_of(shard, zero), small_of(mom_m, zero), small_of(mom_v, zero), "adamw_small")
    row_names = ["attn_w_in", "dn_w_in", "attn_w_out", "dn_w_out", "meta_tokens", "dn_conv_w", "dn_norm_w"]
    lead = {"attn_w_in", "dn_w_in", "attn_w_out", "dn_w_out", "dn_conv_w"}
    for kind in range(4):
        vals = dict(zip(row_names, _unpack_rows(ra[kind], rb[kind], rc[kind])))
        small = _unpack_small(rs[kind])
        vals.update(dict(zip(SMALL, small[:7])))
        if kind == 0:
            loss_total = small[7][0, 0]
        out[kind] = [vals[k][None] if k in lead else vals[k] for k in WEIGHTS]
    return (loss_total, grad_x, *out[0], *out[1], *out[2], *out[3])
```

```python
import functools
import math

import jax
import jax.numpy as jnp
from jax import lax
from jax.experimental import pallas as pl
from jax.experimental.pallas import tpu as pltpu

F32, BF16 = jnp.float32, jnp.bfloat16

D_MODEL = 1024
N_META = 16
NORM_EPS = 1e-6
ATTN_HEADS, ATTN_KV_HEADS, ATTN_GROUPS, ATTN_HD = 16, 2, 8, 64
ATTN_BLOCK = 128
ATTN_STEP_BLOCKS = 3
FRONT_PAD = ATTN_BLOCK - N_META
DN_HD, DN_K_HEADS, DN_V_HEADS = 128, 8, 16
DN_CHUNK = 128
TRI_BLOCK = 64
SCAN_CHUNKS = 3
SCAN_BWD_CHUNKS = 1
SCAN_FWD_GROUP = 16
SCAN_BWD_GROUP = 8
DN_KEY_W, DN_VAL_W = 1024, 2048
DN_CONV_W = 2 * DN_KEY_W + DN_VAL_W
DN_CONV_K = 4
N_DEV = 8
ROW_BLOCK = 384
WGRAD_ROWS = 1376
VMEM_LIMIT = 56 * 1024 * 1024
NEG = -1e30

ADAM_LR, ADAM_B1, ADAM_B2, ADAM_EPS, ADAM_WD, ADAM_STEP = 0.001, 0.9, 0.999, 1e-08, 0.01, 10

NT = (((1,), (1,)), ((), ()))
TN = (((0,), (0,)), ((), ()))


def _cparams(sem=("arbitrary",)):
    return pltpu.CompilerParams(dimension_semantics=sem, vmem_limit_bytes=VMEM_LIMIT)


def _rms(x, w):
    return x * lax.rsqrt(jnp.mean(x * x, axis=-1, keepdims=True) + NORM_EPS) * w


def _silu(x):
    return x * jax.nn.sigmoid(x)


def _softplus(x):
    return jnp.maximum(x, 0.0) + jnp.log(1.0 + jnp.exp(-jnp.abs(x)))


NN = (((1,), (0,)), ((), ()))


def _mm(a, b, dims):
    return lax.dot_general(a.astype(BF16), b.astype(BF16), dims, preferred_element_type=F32)


@functools.partial(jax.custom_vjp, nondiff_argnums=(2,))
def _bdot_vjp(a, b, dims):
    return _mm(a, b, dims)


def _bdot_fwd(a, b, dims):
    a16, b16 = a.astype(BF16), b.astype(BF16)
    return _mm(a16, b16, dims), (a16, b16, jnp.zeros((), a.dtype), jnp.zeros((), b.dtype))


def _bdot_bwd(dims, res, g):
    a16, b16, ta, tb = res
    g16 = g.astype(BF16)
    if dims == NN:
        da, db = _mm(g16, b16, NT), _mm(a16, g16, TN)
    elif dims == NT:
        da, db = _mm(g16, b16, NN), _mm(g16, a16, TN)
    else:
        da, db = _mm(b16, g16, NT), _mm(a16, g16, NN)
    return da.astype(ta.dtype), db.astype(tb.dtype)


_bdot_vjp.defvjp(_bdot_fwd, _bdot_bwd)


def _bdot(a, b, dims=NN):
    return _bdot_vjp(a, b, dims)


def _row_call(name, body, n_rows, rb, rows, consts, outs, accs=(), reverse=False, scratch=(), halos=()):
    n = n_rows // rb
    assert n * rb == n_rows
    idx = (lambda i: (n - 1 - i, 0)) if reverse else (lambda i: (i, 0))
    in_specs = [pl.BlockSpec((rb, a.shape[1]), idx) for a in rows]
    in_specs += [pl.BlockSpec((hr, a.shape[1]), fn) for a, hr, fn in halos]
    in_specs += [pl.BlockSpec(c.shape, functools.partial(lambda i, nd: (0,) * nd, nd=c.ndim)) for c in consts]
    out_specs = [pl.BlockSpec((rb, c), idx) for c, _ in outs]
    out_specs += [pl.BlockSpec(s, functools.partial(lambda i, nd: (0,) * nd, nd=len(s))) for s, _ in accs]
    out_shape = [jax.ShapeDtypeStruct((n_rows, c), dt) for c, dt in outs]
    out_shape += [jax.ShapeDtypeStruct(s, dt) for s, dt in accs]
    return pl.pallas_call(
        body, grid=(n,), in_specs=in_specs, out_specs=out_specs, out_shape=out_shape,
        scratch_shapes=list(scratch), name=name, compiler_params=_cparams(),
    )(*rows, *[a for a, _, _ in halos], *consts)


def _token_views(x):
    per = ROW_BLOCK // ATTN_BLOCK
    return [(x, ATTN_BLOCK, functools.partial(lambda i, k: (jnp.maximum(per * i - 1 + k, 0), 0), k=k)) for k in range(per)]


def _padded_block(i, front, views):
    first = jnp.where(i == 0, front, views[0][...]) if front is not None else views[0][...]
    return jnp.concatenate([first] + [v[...] for v in views[1:]], axis=0)


def _attn_in_fwd(x, front, norm_w, w_in):
    T = x.shape[0] + ATTN_BLOCK

    def body(xa_ref, xb_ref, xc_ref, front_ref, nw_ref, w_ref, xn_ref, q_ref, kv_ref, gate_ref):
        h = _padded_block(pl.program_id(0), front_ref[...], (xa_ref, xb_ref, xc_ref))
        xn = _rms(h, nw_ref[...]).astype(BF16)
        xn_ref[...] = xn
        q_ref[...] = jnp.dot(xn, w_ref[:, 0:1024], preferred_element_type=F32)
        kv_ref[...] = jnp.dot(xn, w_ref[:, 1024:1280], preferred_element_type=F32)
        gate_ref[...] = jnp.dot(xn, w_ref[:, 1280:2304], preferred_element_type=F32)

    return _row_call("attn_in_fwd", body, T, ROW_BLOCK, [], [front, norm_w, w_in],
                     [(1024, BF16), (1024, F32), (256, F32), (1024, F32)], halos=_token_views(x))


def _attn_bias(n, j):
    C, R = 2 * ATTN_BLOCK + N_META, ATTN_GROUPS * ATTN_BLOCK
    c = lax.broadcasted_iota(jnp.int32, (C, R), 0)
    r = lax.broadcasted_iota(jnp.int32, (C, R), 1)
    ql = r & (ATTN_BLOCK - 1)
    is_meta = c >= 2 * ATTN_BLOCK
    dist_band = ATTN_BLOCK + ql - c
    cmin = jnp.maximum(0, 2 * ATTN_BLOCK - ATTN_BLOCK * n)
    valid_band = (c >= cmin) & (dist_band >= 0) & (dist_band < ATTN_BLOCK)
    dist_meta = ATTN_BLOCK * n + ql - FRONT_PAD - (c - 2 * ATTN_BLOCK)
    valid = (is_meta & (dist_meta >= 0)) | (jnp.logical_not(is_meta) & valid_band)
    dist = jnp.minimum(jnp.where(is_meta, dist_meta, dist_band), ATTN_BLOCK).astype(F32)
    rr = lax.broadcasted_iota(jnp.int32, (1, R), 1)
    head = (rr >> 7).astype(F32) + float(ATTN_GROUPS * j + 1)
    slope = jnp.exp(head * (-0.5 * math.log(2.0)))
    return jnp.where(valid, slope * dist, -NEG)


def _attn_table_scratch():
    return [pltpu.VMEM((ATTN_STEP_BLOCKS, ATTN_KV_HEADS, 2 * ATTN_BLOCK + N_META, ATTN_GROUPS * ATTN_BLOCK), F32)]


def _attn_groups(q_t, k, v, sinkrow, qnw_col, knw, bias, late_norm=True):
    n = range(len(q_t))
    qn = [q_t[j] * (lax.rsqrt(jnp.mean(q_t[j] * q_t[j], axis=0, keepdims=True) + NORM_EPS) * (ATTN_HD ** -0.5)) * qnw_col
          for j in n]
    kn = [_rms(k[j], knw) for j in n]
    s = [_bdot(kn[j], qn[j]) - bias[j] for j in n]
    m = [lax.stop_gradient(jnp.maximum(jnp.max(s[j], axis=0, keepdims=True), sinkrow[j])) for j in n]
    e = [jnp.exp(s[j] - m[j]) for j in n]
    inv = [1.0 / (jnp.sum(e[j], axis=0, keepdims=True) + jnp.exp(sinkrow[j] - m[j])) for j in n]
    if late_norm:
        return [_bdot(v[j], e[j], TN) * inv[j] for j in n]
    return [_bdot(v[j], e[j] * inv[j], TN) for j in n]


def _sink_row(sinks_ref, j):
    rr = lax.broadcasted_iota(jnp.int32, (1, ATTN_GROUPS * ATTN_BLOCK), 1) >> 7
    row = jnp.zeros((1, ATTN_GROUPS * ATTN_BLOCK), F32)
    for hl in range(ATTN_GROUPS):
        row = jnp.where(rr == hl, sinks_ref[0, ATTN_GROUPS * j + hl], row)
    return row


def _heads_to_lanes(ref, b, j):
    rows = slice(ATTN_BLOCK * b, ATTN_BLOCK * (b + 1))
    return jnp.concatenate([ref[rows, ATTN_HD * h:ATTN_HD * (h + 1)].T
                            for h in range(ATTN_GROUPS * j, ATTN_GROUPS * (j + 1))], axis=1)


def _lanes_to_heads(ref, b, j, x_t):
    rows = slice(ATTN_BLOCK * b, ATTN_BLOCK * (b + 1))
    for hl in range(ATTN_GROUPS):
        h = ATTN_GROUPS * j + hl
        ref[rows, ATTN_HD * h:ATTN_HD * (h + 1)] = x_t[:, ATTN_BLOCK * hl:ATTN_BLOCK * (hl + 1)].T


def _attn_chains(sinks_ref, q_ref, kvc_ref, kvp_ref, kvm_ref, bias_ref):
    chains = [(b, j) for b in range(ATTN_STEP_BLOCKS) for j in range(ATTN_KV_HEADS)]
    q_t, ks, vs, sinkrows, biases = [], [], [], [], []
    for b, j in chains:
        rows = slice(ATTN_BLOCK * b, ATTN_BLOCK * (b + 1))
        prev = kvp_ref if b == 0 else kvc_ref
        prows = slice(0, ATTN_BLOCK) if b == 0 else slice(ATTN_BLOCK * (b - 1), ATTN_BLOCK * b)
        ksl = slice(ATTN_HD * j, ATTN_HD * (j + 1))
        vsl = slice(128 + ATTN_HD * j, 128 + ATTN_HD * (j + 1))
        ks.append(jnp.concatenate([prev[prows, ksl], kvc_ref[rows, ksl], kvm_ref[FRONT_PAD:, ksl]], axis=0))
        vs.append(jnp.concatenate([prev[prows, vsl], kvc_ref[rows, vsl], kvm_ref[FRONT_PAD:, vsl]], axis=0))
        q_t.append(_heads_to_lanes(q_ref, b, j))
        sinkrows.append(_sink_row(sinks_ref, j))
        biases.append(bias_ref[b, j])
    return chains, q_t, ks, vs, sinkrows, biases


def _attn_core_fwd(q, kv, sinks, qnw, knw):
    T = q.shape[0]
    nb = T // ATTN_BLOCK
    nbs = ATTN_STEP_BLOCKS
    assert nb % nbs == 0
    rows_all = nbs * ATTN_BLOCK

    def body(sinks_ref, q_ref, kvc_ref, kvp_ref, kvm_ref, qnw_ref, knw_ref, o_ref, bias_ref):
        i = pl.program_id(0)

        @pl.when(i <= 1)
        def _():
            for b in range(nbs):
                for j in range(ATTN_KV_HEADS):
                    bias_ref[b, j] = _attn_bias(nbs * i + b, j)
        chains, q_t, ks, vs, sinkrows, biases = _attn_chains(sinks_ref, q_ref, kvc_ref, kvp_ref, kvm_ref, bias_ref)
        o_t = _attn_groups(q_t, ks, vs, sinkrows, qnw_ref[...], knw_ref[...], biases)
        for (b, j), o in zip(chains, o_t):
            _lanes_to_heads(o_ref, b, j, o)

    return pl.pallas_call(
        body, grid=(nb // nbs,),
        in_specs=[pl.BlockSpec(memory_space=pltpu.SMEM),
                  pl.BlockSpec((rows_all, 1024), lambda i: (i, 0)),
                  pl.BlockSpec((rows_all, 256), lambda i: (i, 0)),
                  pl.BlockSpec((ATTN_BLOCK, 256), lambda i: (jnp.maximum(nbs * i - 1, 0), 0)),
                  pl.BlockSpec((ATTN_BLOCK, 256), lambda i: (0, 0)),
                  pl.BlockSpec((ATTN_HD, 1), lambda i: (0, 0)),
                  pl.BlockSpec((1, ATTN_HD), lambda i: (0, 0))],
        out_specs=pl.BlockSpec((rows_all, 1024), lambda i: (i, 0)),
        out_shape=jax.ShapeDtypeStruct((T, 1024), F32),
        scratch_shapes=_attn_table_scratch(),
        name="attn_core_fwd", compiler_params=_cparams(),
    )(sinks, q, kv, kv, kv, qnw.reshape(ATTN_HD, 1), knw)


def _attn_out_fwd(o, gate, x, front, w_out):
    T = o.shape[0]

    def body(o_ref, g_ref, xa_ref, xb_ref, xc_ref, front_ref, w_ref, h1_ref):
        h = _padded_block(pl.program_id(0), front_ref[...], (xa_ref, xb_ref, xc_ref))
        og = o_ref[...] * _silu(g_ref[...])
        h1_ref[...] = h + _bdot(og, w_ref[...])

    return _row_call("attn_out_fwd", body, T, ROW_BLOCK, [o, gate], [front, w_out], [(1024, F32)], halos=_token_views(x))[0]


def _wgrad(xn, du, cg, name):
    T, kdim = xn.shape
    cdim = du.shape[1]
    rows = next(r for r in ((2 * WGRAD_ROWS,) if kdim * cg <= 1024 * 1024 else ()) + (WGRAD_ROWS, ROW_BLOCK) if T % r == 0)
    nr, nc = T // rows, cdim // cg
    assert nc * cg == cdim

    def body(x_ref, du_ref, dw_ref):
        @pl.when(pl.program_id(1) == 0)
        def _():
            dw_ref[...] = jnp.zeros_like(dw_ref)
        dw_ref[...] += _bdot(x_ref[...], du_ref[...], TN)

    return pl.pallas_call(
        body, grid=(nc, nr),
        in_specs=[pl.BlockSpec((rows, kdim), lambda j, i: (i, 0)),
                  pl.BlockSpec((rows, cg), lambda j, i: (i, j))],
        out_specs=pl.BlockSpec((kdim, cg), lambda j, i: (0, j)),
        out_shape=jax.ShapeDtypeStruct((kdim, cdim), F32),
        name=name, compiler_params=_cparams(("arbitrary", "arbitrary")),
    )(xn, du)


def _attn_out_bwd(dh1, o, gate, w_out):
    T = o.shape[0]

    def body(dh_ref, o_ref, g_ref, w_ref, do_ref, dg_ref, dw_ref):
        @pl.when(pl.program_id(0) == 0)
        def _():
            dw_ref[...] = jnp.zeros_like(dw_ref)
        dh = dh_ref[...]
        dog = _bdot(dh, w_ref[...], NT)
        og, vjp = jax.vjp(lambda o_, g_: o_ * _silu(g_), o_ref[...], g_ref[...])
        do, dg = vjp(dog)
        do_ref[...] = do
        dg_ref[...] = dg
        dw_ref[...] += _bdot(og, dh, TN)

    return _row_call("attn_out_bwd", body, T, ROW_BLOCK, [dh1, o, gate], [w_out],
                     [(1024, F32), (1024, F32)], [((1024, 1024), F32)])


def _attn_core_bwd(do, q, kv, sinks, qnw, knw):
    T = q.shape[0]
    nbs = ATTN_STEP_BLOCKS
    ns = T // (nbs * ATTN_BLOCK)
    rows_all = nbs * ATTN_BLOCK
    rev = lambda i: ns - 1 - i

    def body(sinks_ref, do_ref, q_ref, kvc_ref, kvp_ref, kvm_ref, qnw_ref, knw_ref,
             dq_ref, dkv_ref, dsinks_ref, dqnw_ref, dknw_ref, carry_ref, meta_ref, bias_ref):
        step = pl.program_id(0)
        i = rev(step)

        @pl.when((step == 0) | (i == 0))
        def _():
            for b in range(nbs):
                for j in range(ATTN_KV_HEADS):
                    bias_ref[b, j] = _attn_bias(nbs * i + b, j)

        @pl.when(step == 0)
        def _():
            carry_ref[...] = jnp.zeros_like(carry_ref)
            meta_ref[...] = jnp.zeros_like(meta_ref)
            dsinks_ref[...] = jnp.zeros_like(dsinks_ref)
            dqnw_ref[...] = jnp.zeros_like(dqnw_ref)
            dknw_ref[...] = jnp.zeros_like(dknw_ref)

        lane16 = lax.broadcasted_iota(jnp.int32, (1, ATTN_HEADS), 1)
        dsinks = jnp.zeros((1, ATTN_HEADS), F32)
        chains, q_t, ks, vs, sinkrows, biases = _attn_chains(sinks_ref, q_ref, kvc_ref, kvp_ref, kvm_ref, bias_ref)
        fn = functools.partial(_attn_groups, bias=biases, late_norm=False)
        _, vjp = jax.vjp(fn, q_t, ks, vs, sinkrows, qnw_ref[...], knw_ref[...])
        dq_t, dks, dvs, dsr, dqn, dkn = vjp([_heads_to_lanes(do_ref, b, j) for b, j in chains])
        dqnw_ref[...] += dqn
        dknw_ref[...] += dkn
        part = {}
        for c, (b, j) in enumerate(chains):
            _lanes_to_heads(dq_ref, b, j, dq_t[c])
            for hl in range(ATTN_GROUPS):
                dsinks = dsinks + jnp.where(lane16 == ATTN_GROUPS * j + hl,
                                            jnp.sum(dsr[c][:, ATTN_BLOCK * hl:ATTN_BLOCK * (hl + 1)]), 0.0)
            part[b, j] = (dks[c], dvs[c])
        for j in range(ATTN_KV_HEADS):
            for kind, sl in ((0, slice(ATTN_HD * j, ATTN_HD * (j + 1))), (1, slice(128 + ATTN_HD * j, 128 + ATTN_HD * (j + 1)))):
                for b in range(nbs):
                    d = part[b, j][kind]
                    nxt = part[b + 1, j][kind][0:ATTN_BLOCK, :] if b + 1 < nbs else carry_ref[:, sl]
                    dkv_ref[ATTN_BLOCK * b:ATTN_BLOCK * (b + 1), sl] = d[ATTN_BLOCK:2 * ATTN_BLOCK, :] + nxt
                    meta_ref[:, sl] += d[2 * ATTN_BLOCK:, :]
                carry_ref[:, sl] = part[0, j][kind][0:ATTN_BLOCK, :]
        dsinks_ref[...] += dsinks

        @pl.when(i == 0)
        def _():
            dkv_ref[FRONT_PAD:ATTN_BLOCK, :] += meta_ref[...]

    dq, dkv, dsinks, dqnw, dknw = pl.pallas_call(
        body, grid=(ns,),
        in_specs=[pl.BlockSpec(memory_space=pltpu.SMEM),
                  pl.BlockSpec((rows_all, 1024), lambda i: (rev(i), 0)),
                  pl.BlockSpec((rows_all, 1024), lambda i: (rev(i), 0)),
                  pl.BlockSpec((rows_all, 256), lambda i: (rev(i), 0)),
                  pl.BlockSpec((ATTN_BLOCK, 256), lambda i: (jnp.maximum(nbs * rev(i) - 1, 0), 0)),
                  pl.BlockSpec((ATTN_BLOCK, 256), lambda i: (0, 0)),
                  pl.BlockSpec((ATTN_HD, 1), lambda i: (0, 0)),
                  pl.BlockSpec((1, ATTN_HD), lambda i: (0, 0))],
        out_specs=[pl.BlockSpec((rows_all, 1024), lambda i: (rev(i), 0)),
                   pl.BlockSpec((rows_all, 256), lambda i: (rev(i), 0)),
                   pl.BlockSpec((1, ATTN_HEADS), lambda i: (0, 0)),
                   pl.BlockSpec((ATTN_HD, 1), lambda i: (0, 0)),
                   pl.BlockSpec((1, ATTN_HD), lambda i: (0, 0))],
        out_shape=[jax.ShapeDtypeStruct((T, 1024), F32), jax.ShapeDtypeStruct((T, 256), F32),
                   jax.ShapeDtypeStruct((1, ATTN_HEADS), F32), jax.ShapeDtypeStruct((ATTN_HD, 1), F32),
                   jax.ShapeDtypeStruct((1, ATTN_HD), F32)],
        scratch_shapes=[pltpu.VMEM((ATTN_BLOCK, 256), F32), pltpu.VMEM((N_META, 256), F32)] + _attn_table_scratch(),
        name="attn_core_bwd", compiler_params=_cparams(),
    )(sinks, do, q, kv, kv, kv, qnw.reshape(ATTN_HD, 1), knw)
    return dq, dkv, dsinks, dqnw.reshape(1, ATTN_HD), dknw


def _attn_in_bwd(dq, dkv, dgate, x, front, dh1, norm_w, w_in):
    T = dq.shape[0]
    n = T // ROW_BLOCK
    per = ROW_BLOCK // ATTN_BLOCK
    assert n >= 3

    def body(dq_ref, dkv_ref, dg_ref, dh1_ref, xa_ref, xb_ref, xc_ref, front_ref, nw_ref, w_ref,
             gx_ref, dfront_ref, dnw_ref, buf_ref, sems):
        i = pl.program_id(0)
        slot = i % 2

        def piece(step, k, s):
            return pltpu.make_async_copy(buf_ref.at[s, pl.ds(ATTN_BLOCK * k, ATTN_BLOCK)],
                                         gx_ref.at[pl.ds((per * step - 1 + k) * ATTN_BLOCK, ATTN_BLOCK)], sems.at[s, k])

        @pl.when(i == 0)
        def _():
            dnw_ref[...] = jnp.zeros_like(dnw_ref)
        for k in range(per):
            @pl.when((i >= 2) & ((k > 0) | (i > 2)))
            def _():
                piece(i - 2, k, slot).wait()
        h = _padded_block(i, front_ref[...], (xa_ref, xb_ref, xc_ref))
        dxn = (_bdot(dq_ref[...], w_ref[:, 0:1024], NT) + _bdot(dkv_ref[...], w_ref[:, 1024:1280], NT)
               + _bdot(dg_ref[...], w_ref[:, 1280:2304], NT))
        _, vjp = jax.vjp(_rms, h, nw_ref[...])
        dh, dnw = vjp(dxn)
        dnw_ref[...] += dnw
        buf_ref[slot] = dh1_ref[...] + dh

        @pl.when(i == 0)
        def _():
            dfront_ref[...] = buf_ref[0, 0:ATTN_BLOCK, :]
        for k in range(per):
            @pl.when((k > 0) | (i > 0))
            def _():
                piece(i, k, slot).start()

        @pl.when(i == n - 1)
        def _():
            for k in range(per):
                piece(i, k, slot).wait()
                piece(i - 1, k, 1 - slot).wait()

    idx = lambda i: (i, 0)
    const = lambda a: pl.BlockSpec(a.shape, functools.partial(lambda i, nd: (0,) * nd, nd=a.ndim))
    rows = [dq, dkv, dgate, dh1]
    views = _token_views(x)
    return pl.pallas_call(
        body, grid=(n,),
        in_specs=[pl.BlockSpec((ROW_BLOCK, a.shape[1]), idx) for a in rows]
        + [pl.BlockSpec((hr, a.shape[1]), fn) for a, hr, fn in views] + [const(front), const(norm_w), const(w_in)],
        out_specs=[pl.BlockSpec(memory_space=pltpu.HBM), pl.BlockSpec((ATTN_BLOCK, D_MODEL), lambda i: (0, 0)),
                   pl.BlockSpec((1, D_MODEL), lambda i: (0, 0))],
        out_shape=[jax.ShapeDtypeStruct(x.shape, F32), jax.ShapeDtypeStruct((ATTN_BLOCK, D_MODEL), F32),
                   jax.ShapeDtypeStruct((1, D_MODEL), F32)],
        scratch_shapes=[pltpu.VMEM((2, ROW_BLOCK, D_MODEL), F32), pltpu.SemaphoreType.DMA((2, per))],
        name="attn_in_bwd", compiler_params=_cparams(),
    )(*rows, *[a for a, _, _ in views], front, norm_w, w_in)


def _dn_in_fwd(h1, norm_w, w_in):
    T = h1.shape[0]

    def body(h_ref, nw_ref, w_ref, xn_ref, qkv_ref, z_ref, ba_ref):
        xn = _rms(h_ref[...], nw_ref[...]).astype(BF16)
        xn_ref[...] = xn
        qkv_ref[...] = jnp.dot(xn, w_ref[:, 0:4096], preferred_element_type=F32)
        z_ref[...] = jnp.dot(xn, w_ref[:, 4096:6144], preferred_element_type=F32)
        ba_ref[...] = jnp.dot(xn, w_ref[:, 6144:6176], preferred_element_type=F32)

    return _row_call("dn_in_fwd", body, T, ROW_BLOCK, [h1], [norm_w, w_in],
                     [(1024, BF16), (4096, F32), (2048, F32), (32, F32)])


def _shift_down(cur, prev8, s):
    i8 = lax.broadcasted_iota(jnp.int32, (8, cur.shape[1]), 0)
    r = pltpu.roll(cur, s, 0)
    head = jnp.where(i8 < s, pltpu.roll(prev8, s, 0), r[0:8])
    return jnp.concatenate([head, r[8:]], axis=0)


def _shift_up(cur, next8, s):
    n = cur.shape[0]
    i8 = lax.broadcasted_iota(jnp.int32, (8, cur.shape[1]), 0)
    r = pltpu.roll(cur, n - s, 0)
    tail = jnp.where(i8 >= 8 - s, pltpu.roll(next8, 8 - s, 0), r[n - 8:])
    return jnp.concatenate([r[:n - 8], tail], axis=0)


def _conv_taps(cur, prev8):
    return [cur] + [_shift_down(cur, prev8, s) for s in range(1, DN_CONV_K)]


def _conv_tile(taps, w):
    out = w[3:4, :] * taps[0]
    for s in range(1, DN_CONV_K):
        out = out + w[3 - s:4 - s, :] * taps[s]
    return out


def _l2n(a, scale):
    return a * (lax.rsqrt(jnp.sum(a * a, axis=-1, keepdims=True) + NORM_EPS) * scale)


def _dn_post_tile(c, t):
    a = _silu(c)
    if t < DN_K_HEADS:
        return _l2n(a, DN_HD ** -0.5)
    if t < 2 * DN_K_HEADS:
        return _l2n(a, 1.0)
    return a


def _dn_beta_g(ba, a_log, dt_bias, live):
    beta = jax.nn.sigmoid(ba[:, 0:DN_V_HEADS]) * live
    g = -jnp.exp(a_log) * _softplus(ba[:, DN_V_HEADS:] + dt_bias) * live
    return beta, g


def _live_rows(i, rb):
    rows = i * rb + lax.broadcasted_iota(jnp.int32, (rb, 1), 0)
    return (rows >= FRONT_PAD).astype(F32)


def _halo_spec_args(x, rb):
    per = rb // 8
    return (x, 8, lambda i: (jnp.maximum(i * per - 1, 0), 0))


def _dn_conv_fwd(qkv, ba, conv_w, a_log, dt_bias):
    T = qkv.shape[0]

    def body(x_ref, ba_ref, halo_ref, cw_ref, al_ref, dtb_ref, q_ref, k_ref, v_ref, bg_ref, c_ref):
        i = pl.program_id(0)
        first = (i > 0).astype(F32)
        for t in range(DN_CONV_W // 128):
            cols = slice(128 * t, 128 * (t + 1))
            c = _conv_tile(_conv_taps(x_ref[:, cols], halo_ref[:, cols] * first), cw_ref[:, cols])
            c_ref[:, cols] = c.astype(BF16)
            out = _dn_post_tile(c, t)
            if t < DN_K_HEADS:
                q_ref[:, cols] = out
            elif t < 2 * DN_K_HEADS:
                k_ref[:, 128 * (t - 8):128 * (t - 7)] = out
            else:
                v_ref[:, 128 * (t - 16):128 * (t - 15)] = out
        beta, g = _dn_beta_g(ba_ref[...], al_ref[...], dtb_ref[...], _live_rows(i, ROW_BLOCK))
        bg_ref[:, 0:DN_V_HEADS] = beta
        bg_ref[:, DN_V_HEADS:] = g

    return _row_call("dn_conv_fwd", body, T, ROW_BLOCK, [qkv, ba], [conv_w, a_log, dt_bias],
                     [(1024, F32), (1024, F32), (2048, F32), (32, F32), (4096, BF16)], halos=[_halo_spec_args(qkv, ROW_BLOCK)])


def _chunk_masks():
    r = lax.broadcasted_iota(jnp.int32, (DN_CHUNK, DN_CHUNK), 0)
    c = lax.broadcasted_iota(jnp.int32, (DN_CHUNK, DN_CHUNK), 1)
    return r >= c, r > c, r == c, r <= c


def _tri_inv_block(x):
    B = TRI_BLOCK
    n = range(len(x))
    r_, c_ = lax.broadcasted_iota(jnp.int32, (B, B), 0), lax.broadcasted_iota(jnp.int32, (B, B), 1)
    ainv = [jnp.where(r_ == c_, 1.0, 0.0) + x[h] for h in n]
    p = [_bdot(x[h], x[h]) for h in n]
    for _ in range(B.bit_length() - 3):
        r = [_bdot(jnp.concatenate([p[h], ainv[h]], axis=0), p[h]) for h in n]
        ainv = [ainv[h] + r[h][B:] for h in n]
        p = [r[h][:B] for h in n]
    return [ainv[h] + _bdot(ainv[h], p[h]) for h in n]


def _tri_inv(x):
    B = TRI_BLOCK
    assert DN_CHUNK == 2 * B
    n = len(x)
    diag = _tri_inv_block([x[h][:B, :B] for h in range(n)] + [x[h][B:, B:] for h in range(n)])
    a11, a22 = diag[:n], diag[n:]
    a21 = [_bdot(_bdot(a22[h], x[h][B:, :B]), a11[h]) for h in range(n)]
    zero = jnp.zeros((B, B), F32)
    return [jnp.concatenate([jnp.concatenate([a11[h], zero], axis=1), jnp.concatenate([a21[h], a22[h]], axis=1)], axis=0)
            for h in range(n)]


@jax.custom_vjp
def _tri_inv_known(x, a):
    return a


def _tri_inv_known_fwd(x, a):
    return a, a


def _tri_inv_known_bwd(a, da):
    return [_bdot(_bdot(a[h], da[h], TN), a[h], NT) for h in range(len(a))], [jnp.zeros_like(t) for t in a]


_tri_inv_known.defvjp(_tri_inv_known_fwd, _tri_inv_known_bwd)


@jax.custom_vjp
def _known(computed, value):
    return value


def _known_fwd(computed, value):
    return value, None


def _known_bwd(_, g):
    return g, jax.tree.map(jnp.zeros_like, g)


_known.defvjp(_known_fwd, _known_bwd)


def _dn_chunk_step(S, q, k, v, beta, g, masks, known=None):
    causal, strict, eye, upper = masks
    C, W = DN_CHUNK, DN_HD
    heads = range(len(v))
    k_t = [k[j].T for j in range(len(k))]
    qk_kk = [_bdot(jnp.concatenate([q[j], k[j]], axis=0), k_t[j]) for j in range(len(q))]
    if known is not None:
        qk_kk = _known(qk_kk, known["qk_kk"])
    g_b = [jnp.broadcast_to(g[h], (C, C)) for h in heads]
    beta_b = [jnp.broadcast_to(beta[h], (C, W)) for h in heads]
    g_row = [jnp.sum(jnp.where(eye, g_b[h], 0.0), axis=0, keepdims=True) for h in heads]
    gc_col = [jnp.sum(jnp.where(causal, g_row[h], 0.0), axis=1, keepdims=True) for h in heads]
    gc_row = [jnp.sum(jnp.where(upper, g_b[h], 0.0), axis=0, keepdims=True) for h in heads]
    g_last = [jnp.sum(g_row[h], axis=1, keepdims=True) for h in heads]
    gc_b = [jnp.broadcast_to(gc_col[h], (C, W)) for h in heads]
    decay = [jnp.exp(jnp.where(causal, gc_b[h][:, :C] - gc_row[h], NEG)) for h in heads]
    eg_b = [jnp.exp(gc_b[h]) for h in heads]
    x = [jnp.where(strict, qk_kk[h // 2][C:] * beta_b[h][:, :C] * decay[h], 0.0) * -1.0 for h in heads]
    ainv = _tri_inv(x) if known is None else _tri_inv_known(x, known["inv"])
    uw = [_bdot(ainv[h], jnp.concatenate([v[h] * beta_b[h], k[h // 2] * (beta_b[h] * eg_b[h])], axis=1)) for h in heads]
    if known is not None:
        uw = _known(uw, known["uw"])
    q_eg = [q[h // 2] * eg_b[h] for h in heads]
    attn = [qk_kk[h // 2][:C] * decay[h] for h in heads]
    k_st = [k_t[h // 2] * jnp.exp(g_last[h] - gc_row[h]) for h in heads]
    s_dec = [jnp.exp(g_last[h]) for h in heads]
    prep = (uw, q_eg, attn, k_st, s_dec)
    if S is None:
        return prep, dict(inv=ainv, uw=uw, qk_kk=qk_kk)
    s_new, o, _ = _dn_chunk_tail(S, prep, None if known is None else known["v_new"])
    return s_new, o


def _dn_chunk_tail(S, prep, known_v_new=None):
    uw, q_eg, attn, k_st, s_dec = prep
    C, W = DN_CHUNK, DN_HD
    heads = range(len(uw))
    ws_qs = [_bdot(jnp.concatenate([uw[h][:, W:], q_eg[h]], axis=0), S[h]) for h in heads]
    v_new = [uw[h][:, :W] - ws_qs[h][:C] for h in heads]
    if known_v_new is not None:
        v_new = _known(v_new, known_v_new)
    o = [ws_qs[h][C:] + _bdot(attn[h], v_new[h]) for h in heads]
    s_new = [S[h] * s_dec[h] + _bdot(k_st[h], v_new[h]) for h in heads]
    return s_new, o, v_new


def _dn_chunk_tiles(q_ref, k_ref, v_ref, bg_ref, c, first, count):
    rows = slice(DN_CHUNK * c, DN_CHUNK * (c + 1))
    q = [q_ref[rows, 128 * j:128 * (j + 1)] for j in range(first // 2, (first + count) // 2)]
    k = [k_ref[rows, 128 * j:128 * (j + 1)] for j in range(first // 2, (first + count) // 2)]
    v = [v_ref[rows, 128 * h:128 * (h + 1)] for h in range(first, first + count)]
    beta = [bg_ref[rows, h:h + 1] for h in range(first, first + count)]
    g = [bg_ref[rows, DN_V_HEADS + h:DN_V_HEADS + h + 1] for h in range(first, first + count)]
    return q, k, v, beta, g


def _dn_scan_fwd(qn, kn, v, bg):
    T = qn.shape[0]
    nc = T // DN_CHUNK
    rows = SCAN_CHUNKS * DN_CHUNK
    assert nc % SCAN_CHUNKS == 0

    def body(q_ref, k_ref, v_ref, bg_ref, o_ref, ssave_ref, inv_ref, uw_ref, vn_ref, qk_ref, s_ref):
        @pl.when(pl.program_id(0) == 0)
        def _():
            s_ref[...] = jnp.zeros_like(s_ref)
        masks = _chunk_masks()
        for first in range(0, DN_V_HEADS, SCAN_FWD_GROUP):
            heads = range(first, first + SCAN_FWD_GROUP)
            preps = [_dn_chunk_step(None, *_dn_chunk_tiles(q_ref, k_ref, v_ref, bg_ref, c, first, SCAN_FWD_GROUP), masks)
                     for c in range(SCAN_CHUNKS)]
            state = [s_ref[h] for h in heads]
            for c, (prep, saved) in enumerate(preps):
                for i, h in enumerate(heads):
                    ssave_ref[c, h] = state[i]
                    inv_ref[c, h] = saved["inv"][i].astype(BF16)
                    uw_ref[c, h] = saved["uw"][i].astype(BF16)
                for i, j in enumerate(range(first // 2, (first + SCAN_FWD_GROUP) // 2)):
                    qk_ref[c, j] = saved["qk_kk"][i].astype(BF16)
                state, o, v_new = _dn_chunk_tail(state, prep)
                for i, h in enumerate(heads):
                    o_ref[DN_CHUNK * c:DN_CHUNK * (c + 1), 128 * h:128 * (h + 1)] = o[i]
                    vn_ref[c, h] = v_new[i].astype(BF16)
            for i, h in enumerate(heads):
                s_ref[h] = state[i]

    return pl.pallas_call(
        body, grid=(nc // SCAN_CHUNKS,),
        in_specs=[pl.BlockSpec((rows, 1024), lambda i: (i, 0)),
                  pl.BlockSpec((rows, 1024), lambda i: (i, 0)),
                  pl.BlockSpec((rows, 2048), lambda i: (i, 0)),
                  pl.BlockSpec((rows, 32), lambda i: (i, 0))],
        out_specs=[pl.BlockSpec((rows, 2048), lambda i: (i, 0)),
                   pl.BlockSpec((SCAN_CHUNKS, DN_V_HEADS, DN_HD, DN_HD), lambda i: (i, 0, 0, 0)),
                   pl.BlockSpec((SCAN_CHUNKS, DN_V_HEADS, DN_CHUNK, DN_CHUNK), lambda i: (i, 0, 0, 0)),
                   pl.BlockSpec((SCAN_CHUNKS, DN_V_HEADS, DN_CHUNK, 2 * DN_HD), lambda i: (i, 0, 0, 0)),
                   pl.BlockSpec((SCAN_CHUNKS, DN_V_HEADS, DN_CHUNK, DN_HD), lambda i: (i, 0, 0, 0)),
                   pl.BlockSpec((SCAN_CHUNKS, DN_K_HEADS, 2 * DN_CHUNK, DN_CHUNK), lambda i: (i, 0, 0, 0))],
        out_shape=[jax.ShapeDtypeStruct((T, 2048), F32),
                   jax.ShapeDtypeStruct((nc, DN_V_HEADS, DN_HD, DN_HD), F32),
                   jax.ShapeDtypeStruct((nc, DN_V_HEADS, DN_CHUNK, DN_CHUNK), BF16),
                   jax.ShapeDtypeStruct((nc, DN_V_HEADS, DN_CHUNK, 2 * DN_HD), BF16),
                   jax.ShapeDtypeStruct((nc, DN_V_HEADS, DN_CHUNK, DN_HD), BF16),
                   jax.ShapeDtypeStruct((nc, DN_K_HEADS, 2 * DN_CHUNK, DN_CHUNK), BF16)],
        scratch_shapes=[pltpu.VMEM((DN_V_HEADS, DN_HD, DN_HD), F32)],
        name="dn_scan_fwd", compiler_params=_cparams(),
    )(qn, kn, v, bg)


def _dn_gate_tile(o, z, onw):
    return _rms(o, onw) * _silu(z)


def _dn_out_fwd(o, z, h1, target, w_out, onw):
    T = o.shape[0]

    def body(o_ref, z_ref, h_ref, ta_ref, tb_ref, tc_ref, w_ref, onw_ref, dy_ref, og_ref, loss_ref):
        i = pl.program_id(0)

        @pl.when(i == 0)
        def _():
            loss_ref[...] = jnp.zeros_like(loss_ref)
        for h in range(DN_V_HEADS):
            cols = slice(128 * h, 128 * (h + 1))
            og_ref[:, cols] = _dn_gate_tile(o_ref[:, cols], z_ref[:, cols], onw_ref[...]).astype(BF16)
        y = h_ref[...] + jnp.dot(og_ref[...], w_ref[...], preferred_element_type=F32)
        rows = i * ROW_BLOCK + lax.broadcasted_iota(jnp.int32, (ROW_BLOCK, 1), 0)
        diff = jnp.where(rows >= FRONT_PAD + N_META, y - _padded_block(i, None, (ta_ref, tb_ref, tc_ref)), 0.0)
        dy_ref[...] = diff * (1.0 / D_MODEL)
        loss_ref[...] += jnp.sum(diff * diff) * (0.5 / D_MODEL)

    return _row_call("dn_out_fwd", body, T, ROW_BLOCK, [o, z, h1], [w_out, onw],
                     [(1024, F32), (2048, BF16)], [((1, 128), F32)], halos=_token_views(target))


def _dn_out_bwd(dy, o, z, w_out, onw):
    T = o.shape[0]

    def body(dy_ref, o_ref, z_ref, w_ref, onw_ref, do_ref, dz_ref, donw_ref, dog_ref):
        @pl.when(pl.program_id(0) == 0)
        def _():
            donw_ref[...] = jnp.zeros_like(donw_ref)
        dy = dy_ref[...].astype(BF16)
        donw = jnp.zeros((1, DN_HD), F32)
        for half in range(2):
            hcols = slice(1024 * half, 1024 * (half + 1))
            dog_ref[:, hcols] = lax.dot_general(dy, w_ref[hcols, :], NT, preferred_element_type=F32)
        for h in range(DN_V_HEADS):
            cols = slice(128 * h, 128 * (h + 1))
            _, vjp = jax.vjp(_dn_gate_tile, o_ref[:, cols], z_ref[:, cols], onw_ref[...])
            do, dz, dn = vjp(dog_ref[:, cols])
            do_ref[:, cols] = do
            dz_ref[:, cols] = dz
            donw = donw + dn
        donw_ref[...] += donw

    return _row_call("dn_out_bwd", body, T, ROW_BLOCK, [dy, o, z], [w_out, onw],
                     [(2048, F32), (2048, F32)], [((1, DN_HD), F32)], scratch=[pltpu.VMEM((ROW_BLOCK, 2048), F32)])


def _dn_scan_bwd(do, qn, kn, v, bg, ssave, saved):
    T = qn.shape[0]
    nc = T // DN_CHUNK
    chunks = SCAN_BWD_CHUNKS
    ns = nc // chunks
    rows = chunks * DN_CHUNK
    rev = lambda i: ns - 1 - i

    def body(do_ref, q_ref, k_ref, v_ref, bg_ref, ss_ref, inv_ref, uw_ref, vn_ref, qk_ref,
             dq_ref, dk_ref, dv_ref, dbg_ref, ds_ref):
        @pl.when(pl.program_id(0) == 0)
        def _():
            ds_ref[...] = jnp.zeros_like(ds_ref)
        lane32 = lax.broadcasted_iota(jnp.int32, (1, 2 * DN_V_HEADS), 1)
        masks = _chunk_masks()
        dbg = [jnp.zeros((DN_CHUNK, 2 * DN_V_HEADS), F32) for _ in range(chunks)]
        for first in range(0, DN_V_HEADS, SCAN_BWD_GROUP):
            heads = range(first, first + SCAN_BWD_GROUP)
            vjps = []
            for c in range(chunks):
                known = dict(inv=[inv_ref[c, h].astype(F32) for h in heads], uw=[uw_ref[c, h].astype(F32) for h in heads],
                             v_new=[vn_ref[c, h].astype(F32) for h in heads],
                             qk_kk=[qk_ref[c, j].astype(F32) for j in range(first // 2, (first + SCAN_BWD_GROUP) // 2)])
                fn = functools.partial(_dn_chunk_step, masks=masks, known=known)
                vjps.append(jax.vjp(fn, [ss_ref[c, h] for h in heads],
                                    *_dn_chunk_tiles(q_ref, k_ref, v_ref, bg_ref, c, first, SCAN_BWD_GROUP))[1])
            ds = [ds_ref[h] for h in heads]
            for c in reversed(range(chunks)):
                crows = slice(DN_CHUNK * c, DN_CHUNK * (c + 1))
                ds, dq, dk, dv, dbeta, dg = vjps[c]((ds, [do_ref[crows, 128 * h:128 * (h + 1)] for h in heads]))
                for i, h in enumerate(heads):
                    dv_ref[crows, 128 * h:128 * (h + 1)] = dv[i]
                    dbg[c] = dbg[c] + jnp.where(lane32 == h, dbeta[i], 0.0) + jnp.where(lane32 == DN_V_HEADS + h, dg[i], 0.0)
                for i, j in enumerate(range(first // 2, (first + SCAN_BWD_GROUP) // 2)):
                    dq_ref[crows, 128 * j:128 * (j + 1)] = dq[i]
                    dk_ref[crows, 128 * j:128 * (j + 1)] = dk[i]
            for i, h in enumerate(heads):
                ds_ref[h] = ds[i]
        for c in range(chunks):
            dbg_ref[DN_CHUNK * c:DN_CHUNK * (c + 1), :] = dbg[c]

    return pl.pallas_call(
        body, grid=(ns,),
        in_specs=[pl.BlockSpec((rows, 2048), lambda i: (rev(i), 0)),
                  pl.BlockSpec((rows, 1024), lambda i: (rev(i), 0)),
                  pl.BlockSpec((rows, 1024), lambda i: (rev(i), 0)),
                  pl.BlockSpec((rows, 2048), lambda i: (rev(i), 0)),
                  pl.BlockSpec((rows, 32), lambda i: (rev(i), 0)),
                  pl.BlockSpec((chunks, DN_V_HEADS, DN_HD, DN_HD), lambda i: (rev(i), 0, 0, 0)),
                  pl.BlockSpec((chunks, DN_V_HEADS, DN_CHUNK, DN_CHUNK), lambda i: (rev(i), 0, 0, 0)),
                  pl.BlockSpec((chunks, DN_V_HEADS, DN_CHUNK, 2 * DN_HD), lambda i: (rev(i), 0, 0, 0)),
                  pl.BlockSpec((chunks, DN_V_HEADS, DN_CHUNK, DN_HD), lambda i: (rev(i), 0, 0, 0)),
                  pl.BlockSpec((chunks, DN_K_HEADS, 2 * DN_CHUNK, DN_CHUNK), lambda i: (rev(i), 0, 0, 0))],
        out_specs=[pl.BlockSpec((rows, 1024), lambda i: (rev(i), 0)),
                   pl.BlockSpec((rows, 1024), lambda i: (rev(i), 0)),
                   pl.BlockSpec((rows, 2048), lambda i: (rev(i), 0)),
                   pl.BlockSpec((rows, 32), lambda i: (rev(i), 0))],
        out_shape=[jax.ShapeDtypeStruct((T, 1024), F32), jax.ShapeDtypeStruct((T, 1024), F32),
                   jax.ShapeDtypeStruct((T, 2048), F32), jax.ShapeDtypeStruct((T, 32), F32)],
        scratch_shapes=[pltpu.VMEM((DN_V_HEADS, DN_HD, DN_HD), F32)],
        name="dn_scan_bwd", compiler_params=_cparams(),
    )(do, qn, kn, v, bg, ssave, *saved)


def _dn_conv_bwd(dqn, dkn, dv, dbg, qkv, conv_out, ba, conv_w, a_log, dt_bias):
    T = qkv.shape[0]
    rb = ROW_BLOCK // 2
    nr = T // rb

    def body(dq_ref, dk_ref, dv_ref, dbg_ref, x_ref, c_ref, ba_ref, cw_ref, al_ref, dtb_ref,
             dx_ref, dba_ref, dcw_ref, dal_ref, ddtb_ref, carry_ref):
        step = pl.program_id(0)
        i = nr - 1 - step

        @pl.when(step == 0)
        def _():
            carry_ref[...] = jnp.zeros_like(carry_ref)
            dcw_ref[...] = jnp.zeros_like(dcw_ref)
            dal_ref[...] = jnp.zeros_like(dal_ref)
            ddtb_ref[...] = jnp.zeros_like(ddtb_ref)
        for t in range(DN_CONV_W // 128):
            cols = slice(128 * t, 128 * (t + 1))
            w, x = cw_ref[:, cols], x_ref[:, cols]
            if t < DN_K_HEADS:
                dout = dq_ref[:, cols]
            elif t < 2 * DN_K_HEADS:
                dout = dk_ref[:, 128 * (t - 8):128 * (t - 7)]
            else:
                dout = dv_ref[:, 128 * (t - 16):128 * (t - 15)]
            _, vjp = jax.vjp(functools.partial(_dn_post_tile, t=t), c_ref[:, cols].astype(F32))
            (dc,) = vjp(dout)
            nxt = carry_ref[:, cols]
            dx = w[3:4, :] * dc
            dcw_ref[3:4, cols] += jnp.sum(dc * x, axis=0, keepdims=True)
            for s in range(1, DN_CONV_K):
                up = _shift_up(dc, nxt, s)
                dx = dx + w[3 - s:4 - s, :] * up
                dcw_ref[3 - s:4 - s, cols] += jnp.sum(up * x, axis=0, keepdims=True)
            dx_ref[:, cols] = dx
            carry_ref[:, cols] = dc[0:8, :]
        fn = functools.partial(_dn_beta_g, live=_live_rows(i, rb))
        _, vjp = jax.vjp(fn, ba_ref[...], al_ref[...], dtb_ref[...])
        dba, dal, ddtb = vjp((dbg_ref[:, 0:DN_V_HEADS], dbg_ref[:, DN_V_HEADS:]))
        dba_ref[...] = dba
        dal_ref[...] += dal
        ddtb_ref[...] += ddtb

    return _row_call("dn_conv_bwd", body, T, rb, [dqn, dkn, dv, dbg, qkv, conv_out, ba], [conv_w, a_log, dt_bias],
                     [(4096, F32), (32, F32)], [((DN_CONV_K, 4096), F32), ((1, DN_V_HEADS), F32), ((1, DN_V_HEADS), F32)],
                     reverse=True, scratch=[pltpu.VMEM((8, 4096), F32)])


def _dn_in_bwd(dqkv, dz, dba, h1, dy, norm_w, w_in):
    T = h1.shape[0]

    def body(dqkv_ref, dz_ref, dba_ref, h_ref, dy_ref, nw_ref, w_ref, dh_ref, dnw_ref):
        @pl.when(pl.program_id(0) == 0)
        def _():
            dnw_ref[...] = jnp.zeros_like(dnw_ref)
        dxn = (_bdot(dqkv_ref[...], w_ref[:, 0:4096], NT) + _bdot(dz_ref[...], w_ref[:, 4096:6144], NT)
               + _bdot(dba_ref[...], w_ref[:, 6144:6176], NT))
        _, vjp = jax.vjp(_rms, h_ref[...], nw_ref[...])
        dh, dnw = vjp(dxn)
        dh_ref[...] = (dy_ref[...] + dh) * _live_rows(pl.program_id(0), ROW_BLOCK)
        dnw_ref[...] += dnw

    return _row_call("dn_in_bwd", body, T, ROW_BLOCK, [dqkv, dz, dba, h1, dy], [norm_w, w_in],
                     [(1024, F32)], [((1, 1024), F32)])


def _exchange(parts, scatter, name):
    n = len(parts)
    out_shape = [jax.ShapeDtypeStruct(p.shape if sc else (N_DEV,) + p.shape, p.dtype) for p, sc in zip(parts, scatter)]

    def body(*refs):
        ins, outs = refs[:n], refs[n:2 * n]
        send_sems, recv_sems, local_sems = refs[2 * n:]
        x, y, c = lax.axis_index("x"), lax.axis_index("y"), lax.axis_index("c")
        me = 4 * x + 2 * y + c
        peers = []
        for k in range(1, N_DEV):
            px = 1 - x if k & 4 else x
            py = 1 - y if k & 2 else y
            pc = 1 - c if k & 1 else c
            peers.append(((px, py, pc), 4 * px + 2 * py + pc))

        def src(a, idx):
            return ins[a].at[idx] if scatter[a] else ins[a]

        local = [pltpu.make_async_copy(src(a, me), outs[a].at[me], local_sems.at[a]) for a in range(n)]
        for cp in local:
            cp.start()
        for a in range(n):
            for k, (dev, idx) in enumerate(peers):
                pltpu.make_async_remote_copy(
                    src_ref=src(a, idx), dst_ref=outs[a].at[me], send_sem=send_sems.at[a, k], recv_sem=recv_sems.at[a, k],
                    device_id=dev, device_id_type=pl.DeviceIdType.MESH).start()
        for a in range(n):
            for k, (dev, idx) in enumerate(peers):
                pltpu.make_async_remote_copy(
                    src_ref=src(a, idx), dst_ref=outs[a].at[idx], send_sem=send_sems.at[a, k], recv_sem=recv_sems.at[a, k],
                    device_id=dev, device_id_type=pl.DeviceIdType.MESH).wait()
        for cp in local:
            cp.wait()

    hbm = pl.BlockSpec(memory_space=pltpu.HBM)
    return pl.pallas_call(
        body, out_shape=out_shape, in_specs=[hbm] * n, out_specs=[hbm] * n,
        scratch_shapes=[pltpu.SemaphoreType.DMA((n, N_DEV - 1)), pltpu.SemaphoreType.DMA((n, N_DEV - 1)),
                        pltpu.SemaphoreType.DMA((n,))],
        name=name,
    )(*parts)


def _gather_two_level(parts, name):
    n = len(parts)
    out_shape = [jax.ShapeDtypeStruct((N_DEV,) + p.shape, p.dtype) for p in parts]

    def body(*refs):
        ins, outs = refs[:n], refs[n:2 * n]
        send_sems, recv_sems, local_sems = refs[2 * n:]
        x, y, c = lax.axis_index("x"), lax.axis_index("y"), lax.axis_index("c")
        idx = lambda px, py, pc: 4 * px + 2 * py + pc
        me, sibling = (x, y, c), (x, y, 1 - c)
        chips = [(1 - x, y), (x, 1 - y), (1 - x, 1 - y)]

        def copy(a, k, block, to, src=None):
            slot = outs[a].at[idx(*block)]
            return pltpu.make_async_remote_copy(
                src_ref=slot if src is None else src, dst_ref=slot, send_sem=send_sems.at[a, k], recv_sem=recv_sems.at[a, k],
                device_id=to, device_id_type=pl.DeviceIdType.MESH)

        local = [pltpu.make_async_copy(ins[a], outs[a].at[idx(*me)], local_sems.at[a]) for a in range(n)]
        for cp in local:
            cp.start()
        sent = []
        for a in range(n):
            sent.append(copy(a, 0, me, sibling, src=ins[a]))
            sent += [copy(a, 1 + j, me, (*chip, c), src=ins[a]) for j, chip in enumerate(chips)]
        for cp in sent:
            cp.start()
        for a in range(n):
            for j, chip in enumerate(chips):
                copy(a, 1 + j, (*chip, c), me).wait_recv()
                passed = copy(a, 4 + j, (*chip, c), sibling)
                passed.start()
                sent.append(passed)
        for a in range(n):
            copy(a, 0, sibling, me).wait_recv()
            for j, chip in enumerate(chips):
                copy(a, 4 + j, (*chip, 1 - c), me).wait_recv()
        for cp in sent:
            cp.wait_send()
        for cp in local:
            cp.wait()

    hbm = pl.BlockSpec(memory_space=pltpu.HBM)
    return pl.pallas_call(
        body, out_shape=out_shape, in_specs=[hbm] * n, out_specs=[hbm] * n,
        scratch_shapes=[pltpu.SemaphoreType.DMA((n, N_DEV - 1)), pltpu.SemaphoreType.DMA((n, N_DEV - 1)),
                        pltpu.SemaphoreType.DMA((n,))],
        name=name,
    )(*parts)


def _swap_with_sibling(parts, name):
    n = len(parts)

    def body(*refs):
        ins, outs = refs[:n], refs[n:2 * n]
        send_sems, recv_sems = refs[2 * n:]
        x, y, c = lax.axis_index("x"), lax.axis_index("y"), lax.axis_index("c")
        copies = [pltpu.make_async_remote_copy(
            src_ref=ins[a].at[1 - c], dst_ref=outs[a], send_sem=send_sems.at[a], recv_sem=recv_sems.at[a],
            device_id=(x, y, 1 - c), device_id_type=pl.DeviceIdType.MESH) for a in range(n)]
        for cp in copies:
            cp.start()
        for cp in copies:
            cp.wait()

    hbm = pl.BlockSpec(memory_space=pltpu.HBM)
    return pl.pallas_call(
        body, out_shape=[jax.ShapeDtypeStruct(p.shape[1:], p.dtype) for p in parts], in_specs=[hbm] * n, out_specs=[hbm] * n,
        scratch_shapes=[pltpu.SemaphoreType.DMA((n,)), pltpu.SemaphoreType.DMA((n,))],
        name=name,
    )(*parts)


def _pair_sum(a, b, name):
    R, C = a.shape
    rb = _adam_rows(R)

    def body(a_ref, b_ref, o_ref):
        o_ref[...] = (a_ref[...].astype(F32) + b_ref[...].astype(F32)).astype(BF16)

    blk = pl.BlockSpec((rb, C), lambda i: (i, 0))
    return pl.pallas_call(body, grid=(R // rb,), in_specs=[blk, blk], out_specs=blk,
                          out_shape=jax.ShapeDtypeStruct((R, C), BF16), name=name, compiler_params=_cparams())(a, b)


def _exchange_chips(parts, name):
    n = len(parts)
    n_chips = N_DEV // 2

    def body(*refs):
        ins, outs = refs[:n], refs[n:2 * n]
        send_sems, recv_sems, local_sems = refs[2 * n:]
        x, y, c = lax.axis_index("x"), lax.axis_index("y"), lax.axis_index("c")
        mine = 2 * x + y
        chips = [(1 - x, y), (x, 1 - y), (1 - x, 1 - y)]
        local = [pltpu.make_async_copy(ins[a].at[mine], outs[a].at[mine], local_sems.at[a]) for a in range(n)]
        for cp in local:
            cp.start()
        for a in range(n):
            for k, (px, py) in enumerate(chips):
                pltpu.make_async_remote_copy(
                    src_ref=ins[a].at[2 * px + py], dst_ref=outs[a].at[mine], send_sem=send_sems.at[a, k],
                    recv_sem=recv_sems.at[a, k], device_id=(px, py, c), device_id_type=pl.DeviceIdType.MESH).start()
        for a in range(n):
            for k, (px, py) in enumerate(chips):
                pltpu.make_async_remote_copy(
                    src_ref=ins[a].at[2 * px + py], dst_ref=outs[a].at[2 * px + py], send_sem=send_sems.at[a, k],
                    recv_sem=recv_sems.at[a, k], device_id=(px, py, c), device_id_type=pl.DeviceIdType.MESH).wait()
        for cp in local:
            cp.wait()

    hbm = pl.BlockSpec(memory_space=pltpu.HBM)
    return pl.pallas_call(
        body, out_shape=[jax.ShapeDtypeStruct(p.shape, p.dtype) for p in parts], in_specs=[hbm] * n, out_specs=[hbm] * n,
        scratch_shapes=[pltpu.SemaphoreType.DMA((n, n_chips - 1)), pltpu.SemaphoreType.DMA((n, n_chips - 1)),
                        pltpu.SemaphoreType.DMA((n,))],
        name=name,
    )(*parts)


def _adam_rows(rows):
    for rb in (128, 64, 40, 16, 8):
        if rows % rb == 0:
            return rb
    return rows


def _adamw(stack, w, m, v, name):
    R, C = w.shape
    rb = _adam_rows(R)
    slots = stack.shape[0]

    def body(s_ref, w_ref, m_ref, v_ref, g_ref, d_ref, nm_ref, nv_ref):
        g = s_ref[0].astype(F32)
        for s in range(1, slots):
            g = g + s_ref[s].astype(F32)
        nm = ADAM_B1 * m_ref[...] + (1.0 - ADAM_B1) * g
        nv = ADAM_B2 * v_ref[...] + (1.0 - ADAM_B2) * (g * g)
        m_hat = nm / (1.0 - ADAM_B1 ** ADAM_STEP)
        v_hat = nv / (1.0 - ADAM_B2 ** ADAM_STEP)
        g_ref[...] = g
        d_ref[...] = -ADAM_LR * (m_hat / (jnp.sqrt(v_hat) + ADAM_EPS) + ADAM_WD * w_ref[...])
        nm_ref[...] = nm
        nv_ref[...] = nv

    blk = pl.BlockSpec((rb, C), lambda i: (i, 0))
    return pl.pallas_call(
        body, grid=(R // rb,),
        in_specs=[pl.BlockSpec((slots, rb, C), lambda i: (0, i, 0)), blk, blk, blk],
        out_specs=[blk] * 4, out_shape=[jax.ShapeDtypeStruct((R, C), F32)] * 4,
        name=name, compiler_params=_cparams(),
    )(stack, w, m, v)


def _pad_rows8(a):
    return jnp.concatenate([a, jnp.zeros((8 - a.shape[0], a.shape[1]), a.dtype)], axis=0) if a.shape[0] < 8 else a


def _pack_small(norm_w, qnw, knw, sinks, a_log, dt_bias, onw, extra):
    z = lambda n: jnp.zeros((1, n), F32)
    row = jnp.concatenate([norm_w, qnw, knw, sinks, a_log, dt_bias, z(80), onw, extra, z(512)], axis=1)
    return row.reshape(16, 128)


def _unpack_small(p):
    row = p.reshape(1, 2048)
    cut = lambda a, n: row[:, a:a + n]
    return (cut(0, 1024), cut(1024, 64), cut(1088, 64), cut(1152, 16), cut(1168, 16), cut(1184, 16), cut(1280, 128),
            cut(1408, 128))


def _pack_rows(w_in_a, w_in_d, w_out_a, w_out_d, meta, conv, dn_norm):
    a = jnp.concatenate([w_in_a, w_in_d], axis=1)
    b = jnp.concatenate([w_out_a, w_out_d], axis=0)
    c = jnp.concatenate([meta, conv.reshape(16, 128), _pad_rows8(dn_norm)], axis=0)
    return a, b, c


def _unpack_rows(a, b, c):
    return (a[:, :288], a[:, 288:], b[:128], b[128:], c[:16], c[16:32].reshape(4, 512), c[32:33])


def _local_step(x, front, target, w):
    xn0, q, kv, gate = _attn_in_fwd(x, front, w["attn_norm_w"], w["attn_w_in"])
    o = _attn_core_fwd(q, kv, w["attn_sinks"], w["attn_q_norm_w"], w["attn_k_norm_w"])
    h1 = _attn_out_fwd(o, gate, x, front, w["attn_w_out"])
    xn1, qkv, z, ba = _dn_in_fwd(h1, w["dn_norm_w"], w["dn_w_in"])
    qn, kn, v, bg, conv_out = _dn_conv_fwd(qkv, ba, w["dn_conv_w"], w["dn_a_log"], w["dn_dt_bias"])
    o_dn, ssave, *saved = _dn_scan_fwd(qn, kn, v, bg)
    dy, og_dn, loss = _dn_out_fwd(o_dn, z, h1, target, w["dn_w_out"], w["dn_o_norm_w"])

    g = {}
    do_dn, dz, g["dn_o_norm_w"] = _dn_out_bwd(dy, o_dn, z, w["dn_w_out"], w["dn_o_norm_w"])
    g["dn_w_out"] = _wgrad(og_dn, dy, 512, "wgrad_dn_out")
    dqn, dkn, dv, dbg = _dn_scan_bwd(do_dn, qn, kn, v, bg, ssave, saved)
    dqkv, dba, g["dn_conv_w"], g["dn_a_log"], g["dn_dt_bias"] = _dn_conv_bwd(
        dqn, dkn, dv, dbg, qkv, conv_out, ba, w["dn_conv_w"], w["dn_a_log"], w["dn_dt_bias"])
    dh1, g["dn_norm_w"] = _dn_in_bwd(dqkv, dz, dba, h1, dy, w["dn_norm_w"], w["dn_w_in"])
    g["dn_w_in"] = jnp.concatenate([_wgrad(xn1, dqkv, 1024, "wgrad_dn_qkv"), _wgrad(xn1, dz, 1024, "wgrad_dn_z"),
                                    _wgrad(xn1, dba, 32, "wgrad_dn_ba")], axis=1)
    do, dgate, g["attn_w_out"] = _attn_out_bwd(dh1, o, gate, w["attn_w_out"])
    dq, dkv, g["attn_sinks"], g["attn_q_norm_w"], g["attn_k_norm_w"] = _attn_core_bwd(
        do, q, kv, w["attn_sinks"], w["attn_q_norm_w"], w["attn_k_norm_w"])
    grad_x, dfront, g["attn_norm_w"] = _attn_in_bwd(dq, dkv, dgate, x, front, dh1, w["attn_norm_w"], w["attn_w_in"])
    g["meta_tokens"] = dfront[FRONT_PAD:]
    g["attn_w_in"] = jnp.concatenate([_wgrad(xn0, dq, 1024, "wgrad_attn_q"), _wgrad(xn0, dkv, 256, "wgrad_attn_kv"),
                                      _wgrad(xn0, dgate, 1024, "wgrad_attn_gate")], axis=1)
    return loss, grad_x, g


WEIGHTS = ['meta_tokens', 'attn_norm_w', 'attn_w_in', 'attn_q_norm_w', 'attn_k_norm_w', 'attn_sinks', 'attn_w_out',
           'dn_norm_w', 'dn_w_in', 'dn_conv_w', 'dn_a_log', 'dn_dt_bias', 'dn_o_norm_w', 'dn_w_out']
SMALL = ['attn_norm_w', 'attn_q_norm_w', 'attn_k_norm_w', 'attn_sinks', 'dn_a_log', 'dn_dt_bias', 'dn_o_norm_w']


def kernel(x, meta_tokens, attn_norm_w, attn_w_in, attn_q_norm_w, attn_k_norm_w, attn_sinks, attn_w_out, dn_norm_w, dn_w_in, dn_conv_w, dn_a_log, dn_dt_bias, dn_o_norm_w, dn_w_out, loss_target, m_meta_tokens, m_attn_norm_w, m_attn_w_in, m_attn_q_norm_w, m_attn_k_norm_w, m_attn_sinks, m_attn_w_out, m_dn_norm_w, m_dn_w_in, m_dn_conv_w, m_dn_a_log, m_dn_dt_bias, m_dn_o_norm_w, m_dn_w_out, v_meta_tokens, v_attn_norm_w, v_attn_w_in, v_attn_q_norm_w, v_attn_k_norm_w, v_attn_sinks, v_attn_w_out, v_dn_norm_w, v_dn_w_in, v_dn_conv_w, v_dn_a_log, v_dn_dt_bias, v_dn_o_norm_w, v_dn_w_out):
    shard = dict(meta_tokens=meta_tokens, attn_norm_w=attn_norm_w, attn_w_in=attn_w_in[0], attn_q_norm_w=attn_q_norm_w,
                 attn_k_norm_w=attn_k_norm_w, attn_sinks=attn_sinks, attn_w_out=attn_w_out[0], dn_norm_w=dn_norm_w,
                 dn_w_in=dn_w_in[0], dn_conv_w=dn_conv_w[0], dn_a_log=dn_a_log, dn_dt_bias=dn_dt_bias,
                 dn_o_norm_w=dn_o_norm_w, dn_w_out=dn_w_out[0])
    mom_m = dict(meta_tokens=m_meta_tokens, attn_norm_w=m_attn_norm_w, attn_w_in=m_attn_w_in[0], attn_q_norm_w=m_attn_q_norm_w,
                 attn_k_norm_w=m_attn_k_norm_w, attn_sinks=m_attn_sinks, attn_w_out=m_attn_w_out[0], dn_norm_w=m_dn_norm_w,
                 dn_w_in=m_dn_w_in[0], dn_conv_w=m_dn_conv_w[0], dn_a_log=m_dn_a_log, dn_dt_bias=m_dn_dt_bias,
                 dn_o_norm_w=m_dn_o_norm_w, dn_w_out=m_dn_w_out[0])
    mom_v = dict(meta_tokens=v_meta_tokens, attn_norm_w=v_attn_norm_w, attn_w_in=v_attn_w_in[0], attn_q_norm_w=v_attn_q_norm_w,
                 attn_k_norm_w=v_attn_k_norm_w, attn_sinks=v_attn_sinks, attn_w_out=v_attn_w_out[0], dn_norm_w=v_dn_norm_w,
                 dn_w_in=v_dn_w_in[0], dn_conv_w=v_dn_conv_w[0], dn_a_log=v_dn_a_log, dn_dt_bias=v_dn_dt_bias,
                 dn_o_norm_w=v_dn_o_norm_w, dn_w_out=v_dn_w_out[0])

    def rows_of(d):
        return _pack_rows(d["attn_w_in"], d["dn_w_in"], d["attn_w_out"], d["dn_w_out"], d["meta_tokens"], d["dn_conv_w"],
                          d["dn_norm_w"])

    def small_of(d, extra):
        return _pack_small(*[d[k] for k in SMALL], extra)

    wa, wb, wc = rows_of(shard)
    ga, gb, gc = _gather_two_level([wa.astype(BF16), wb.astype(BF16), wc], "gather_weights")
    full = {k: shard[k] for k in SMALL}
    full["attn_w_in"] = ga[:, :, :288].transpose(1, 0, 2).reshape(1024, 2304)
    full["dn_w_in"] = ga[:, :, 288:].transpose(1, 0, 2).reshape(1024, 6176)
    full["attn_w_out"] = gb[:, :128].reshape(1024, 1024)
    full["dn_w_out"] = gb[:, 128:].reshape(2048, 1024)
    meta_full = gc[:, :16].transpose(1, 0, 2).reshape(N_META, 1024)
    full["dn_conv_w"] = gc[:, 16:32].reshape(N_DEV, 4, 512).transpose(1, 0, 2).reshape(4, 4096)
    full["dn_norm_w"] = gc[:, 32].reshape(1, 1024)

    front = jnp.concatenate([jnp.zeros((FRONT_PAD, D_MODEL), F32), meta_full], axis=0)
    loss, grad_x, g = _local_step(x[0], front, loss_target[0], full)
    grad_x = grad_x[None]

    pa = jnp.concatenate([g["attn_w_in"].reshape(1024, N_DEV, 288), g["dn_w_in"].reshape(1024, N_DEV, 772)],
                         axis=2).transpose(1, 0, 2)
    pb = jnp.concatenate([g["attn_w_out"].reshape(N_DEV, 128, 1024), g["dn_w_out"].reshape(N_DEV, 256, 1024)], axis=1)
    dn_norm8 = jnp.concatenate([g["dn_norm_w"].reshape(N_DEV, 1, 128), jnp.zeros((N_DEV, 7, 128), F32)], axis=1)
    pc = jnp.concatenate([g["meta_tokens"].reshape(N_META, N_DEV, 128).transpose(1, 0, 2),
                          g["dn_conv_w"].reshape(4, N_DEV, 512).transpose(1, 0, 2).reshape(N_DEV, 16, 128), dn_norm8], axis=1)
    ps = small_of(g, loss)
    c = lax.axis_index("c")
    by_core = lambda p: p.astype(BF16).reshape((N_DEV // 2, 2) + p.shape[1:]).swapaxes(0, 1)
    pa2, pb2 = by_core(pa), by_core(pb)
    ra, rb_ = _swap_with_sibling([pa2, pb2], "swap_grads")
    own = lambda p2: lax.dynamic_index_in_dim(p2, c, axis=0, keepdims=False)
    flat = lambda t: t.reshape((-1,) + t.shape[2:])
    sa = _pair_sum(flat(own(pa2)), flat(ra), "pair_sum_a").reshape(ra.shape)
    sb = _pair_sum(flat(own(pb2)), flat(rb_), "pair_sum_b").reshape(rb_.shape)
    xa, xb = _exchange_chips([sa, sb], "exchange_grads")
    xc, xs = _exchange([pc, ps], [True, False], "exchange_small")

    out = {}
    ma, mb, mc = rows_of(mom_m)
    va, vb, vc = rows_of(mom_v)
    ra = _adamw(xa, wa, ma, va, "adamw_a")
    rb = _adamw(xb, wb, mb, vb, "adamw_b")
    rc = _adamw(xc, wc, mc, vc, "adamw_c")
    zero = jnp.zeros((1, 128), F32)
    rs = _adamw(xs, small_of(shard, zero), small_of(mom_m, zero), small_of(mom_v, zero), "adamw_small")
    row_names = ["attn_w_in", "dn_w_in", "attn_w_out", "dn_w_out", "meta_tokens", "dn_conv_w", "dn_norm_w"]
    lead = {"attn_w_in", "dn_w_in", "attn_w_out", "dn_w_out", "dn_conv_w"}
    for kind in range(4):
        vals = dict(zip(row_names, _unpack_rows(ra[kind], rb[kind], rc[kind])))
        small = _unpack_small(rs[kind])
        vals.update(dict(zip(SMALL, small[:7])))
        if kind == 0:
            loss_total = small[7][0, 0]
        out[kind] = [vals[k][None] if k in lead else vals[k] for k in WEIGHTS]
    return (loss_total, grad_x, *out[0], *out[1], *out[2], *out[3])
```

```python
import functools
import math

import jax
import jax.numpy as jnp
from jax import lax
from jax.experimental import pallas as pl
from jax.experimental.pallas import tpu as pltpu

F32, BF16 = jnp.float32, jnp.bfloat16

D_MODEL = 1024
N_META = 16
NORM_EPS = 1e-6
ATTN_HEADS, ATTN_KV_HEADS, ATTN_GROUPS, ATTN_HD = 16, 2, 8, 64
ATTN_BLOCK = 128
ATTN_STEP_BLOCKS = 3
FRONT_PAD = ATTN_BLOCK - N_META
DN_HD, DN_K_HEADS, DN_V_HEADS = 128, 8, 16
DN_CHUNK = 128
TRI_BLOCK = 64
SCAN_CHUNKS = 3
SCAN_BWD_CHUNKS = 1
SCAN_FWD_GROUP = 16
SCAN_BWD_GROUP = 8
DN_KEY_W, DN_VAL_W = 1024, 2048
DN_CONV_W = 2 * DN_KEY_W + DN_VAL_W
DN_CONV_K = 4
N_DEV = 8
ROW_BLOCK = 384
WGRAD_ROWS = 1376
VMEM_LIMIT = 56 * 1024 * 1024
NEG = -1e30

ADAM_LR, ADAM_B1, ADAM_B2, ADAM_EPS, ADAM_WD, ADAM_STEP = 0.001, 0.9, 0.999, 1e-08, 0.01, 10

NT = (((1,), (1,)), ((), ()))
TN = (((0,), (0,)), ((), ()))


def _cparams(sem=("arbitrary",)):
    return pltpu.CompilerParams(dimension_semantics=sem, vmem_limit_bytes=VMEM_LIMIT)


def _rms(x, w):
    return x * lax.rsqrt(jnp.mean(x * x, axis=-1, keepdims=True) + NORM_EPS) * w


def _silu(x):
    return x * jax.nn.sigmoid(x)


def _softplus(x):
    return jnp.maximum(x, 0.0) + jnp.log(1.0 + jnp.exp(-jnp.abs(x)))


NN = (((1,), (0,)), ((), ()))


def _mm(a, b, dims):
    return lax.dot_general(a.astype(BF16), b.astype(BF16), dims, preferred_element_type=F32)


@functools.partial(jax.custom_vjp, nondiff_argnums=(2,))
def _bdot_vjp(a, b, dims):
    return _mm(a, b, dims)


def _bdot_fwd(a, b, dims):
    a16, b16 = a.astype(BF16), b.astype(BF16)
    return _mm(a16, b16, dims), (a16, b16, jnp.zeros((), a.dtype), jnp.zeros((), b.dtype))


def _bdot_bwd(dims, res, g):
    a16, b16, ta, tb = res
    g16 = g.astype(BF16)
    if dims == NN:
        da, db = _mm(g16, b16, NT), _mm(a16, g16, TN)
    elif dims == NT:
        da, db = _mm(g16, b16, NN), _mm(g16, a16, TN)
    else:
        da, db = _mm(b16, g16, NT), _mm(a16, g16, NN)
    return da.astype(ta.dtype), db.astype(tb.dtype)


_bdot_vjp.defvjp(_bdot_fwd, _bdot_bwd)


def _bdot(a, b, dims=NN):
    return _bdot_vjp(a, b, dims)


def _row_call(name, body, n_rows, rb, rows, consts, outs, accs=(), reverse=False, scratch=(), halos=(),
              single_buffer_consts=False):
    n = n_rows // rb
    assert n * rb == n_rows
    idx = (lambda i: (n - 1 - i, 0)) if reverse else (lambda i: (i, 0))
    in_specs = [pl.BlockSpec((rb, a.shape[1]), idx) for a in rows]
    in_specs += [pl.BlockSpec((hr, a.shape[1]), fn) for a, hr, fn in halos]
    mode = dict(pipeline_mode=pl.Buffered(1)) if single_buffer_consts else {}
    in_specs += [pl.BlockSpec(c.shape, functools.partial(lambda i, nd: (0,) * nd, nd=c.ndim), **mode) for c in consts]
    out_specs = [pl.BlockSpec((rb, c), idx) for c, _ in outs]
    out_specs += [pl.BlockSpec(s, functools.partial(lambda i, nd: (0,) * nd, nd=len(s))) for s, _ in accs]
    out_shape = [jax.ShapeDtypeStruct((n_rows, c), dt) for c, dt in outs]
    out_shape += [jax.ShapeDtypeStruct(s, dt) for s, dt in accs]
    return pl.pallas_call(
        body, grid=(n,), in_specs=in_specs, out_specs=out_specs, out_shape=out_shape,
        scratch_shapes=list(scratch), name=name, compiler_params=_cparams(),
    )(*rows, *[a for a, _, _ in halos], *consts)


def _token_views(x):
    per = ROW_BLOCK // ATTN_BLOCK
    return [(x, ATTN_BLOCK, functools.partial(lambda i, k: (jnp.maximum(per * i - 1 + k, 0), 0), k=k)) for k in range(per)]


def _padded_block(i, front, views):
    first = jnp.where(i == 0, front, views[0][...]) if front is not None else views[0][...]
    return jnp.concatenate([first] + [v[...] for v in views[1:]], axis=0)


def _attn_in_fwd(x, front, norm_w, w_in):
    T = x.shape[0] + ATTN_BLOCK

    def body(xa_ref, xb_ref, xc_ref, front_ref, nw_ref, w_ref, xn_ref, q_ref, kv_ref, gate_ref):
        h = _padded_block(pl.program_id(0), front_ref[...], (xa_ref, xb_ref, xc_ref))
        xn = _rms(h, nw_ref[...]).astype(BF16)
        xn_ref[...] = xn
        q_ref[...] = jnp.dot(xn, w_ref[:, 0:1024], preferred_element_type=F32)
        kv_ref[...] = jnp.dot(xn, w_ref[:, 1024:1280], preferred_element_type=F32)
        gate_ref[...] = jnp.dot(xn, w_ref[:, 1280:2304], preferred_element_type=F32)

    return _row_call("attn_in_fwd", body, T, ROW_BLOCK, [], [front, norm_w, w_in],
                     [(1024, BF16), (1024, F32), (256, F32), (1024, F32)], halos=_token_views(x))


def _attn_bias(n, j):
    C, R = 2 * ATTN_BLOCK + N_META, ATTN_GROUPS * ATTN_BLOCK
    c = lax.broadcasted_iota(jnp.int32, (C, R), 0)
    r = lax.broadcasted_iota(jnp.int32, (C, R), 1)
    ql = r & (ATTN_BLOCK - 1)
    is_meta = c >= 2 * ATTN_BLOCK
    dist_band = ATTN_BLOCK + ql - c
    cmin = jnp.maximum(0, 2 * ATTN_BLOCK - ATTN_BLOCK * n)
    valid_band = (c >= cmin) & (dist_band >= 0) & (dist_band < ATTN_BLOCK)
    dist_meta = ATTN_BLOCK * n + ql - FRONT_PAD - (c - 2 * ATTN_BLOCK)
    valid = (is_meta & (dist_meta >= 0)) | (jnp.logical_not(is_meta) & valid_band)
    dist = jnp.minimum(jnp.where(is_meta, dist_meta, dist_band), ATTN_BLOCK).astype(F32)
    rr = lax.broadcasted_iota(jnp.int32, (1, R), 1)
    head = (rr >> 7).astype(F32) + float(ATTN_GROUPS * j + 1)
    slope = jnp.exp(head * (-0.5 * math.log(2.0)))
    return jnp.where(valid, slope * dist, -NEG)


def _attn_table_scratch():
    return [pltpu.VMEM((ATTN_STEP_BLOCKS, ATTN_KV_HEADS, 2 * ATTN_BLOCK + N_META, ATTN_GROUPS * ATTN_BLOCK), F32)]


def _attn_groups(q_t, k, v, sinkrow, qnw_col, knw, bias, late_norm=True):
    n = range(len(q_t))
    qn = [q_t[j] * (lax.rsqrt(jnp.mean(q_t[j] * q_t[j], axis=0, keepdims=True) + NORM_EPS) * (ATTN_HD ** -0.5)) * qnw_col
          for j in n]
    kn = [_rms(k[j], knw) for j in n]
    s = [_bdot(kn[j], qn[j]) - bias[j] for j in n]
    m = [lax.stop_gradient(jnp.maximum(jnp.max(s[j], axis=0, keepdims=True), sinkrow[j])) for j in n]
    e = [jnp.exp(s[j] - m[j]) for j in n]
    inv = [1.0 / (jnp.sum(e[j], axis=0, keepdims=True) + jnp.exp(sinkrow[j] - m[j])) for j in n]
    if late_norm:
        return [_bdot(v[j], e[j], TN) * inv[j] for j in n]
    return [_bdot(v[j], e[j] * inv[j], TN) for j in n]


def _sink_row(sinks_ref, j):
    rr = lax.broadcasted_iota(jnp.int32, (1, ATTN_GROUPS * ATTN_BLOCK), 1) >> 7
    row = jnp.zeros((1, ATTN_GROUPS * ATTN_BLOCK), F32)
    for hl in range(ATTN_GROUPS):
        row = jnp.where(rr == hl, sinks_ref[0, ATTN_GROUPS * j + hl], row)
    return row


def _heads_to_lanes(ref, b, j):
    rows = slice(ATTN_BLOCK * b, ATTN_BLOCK * (b + 1))
    return jnp.concatenate([ref[rows, ATTN_HD * h:ATTN_HD * (h + 1)].T
                            for h in range(ATTN_GROUPS * j, ATTN_GROUPS * (j + 1))], axis=1)


def _lanes_to_heads(ref, b, j, x_t):
    rows = slice(ATTN_BLOCK * b, ATTN_BLOCK * (b + 1))
    for hl in range(ATTN_GROUPS):
        h = ATTN_GROUPS * j + hl
        ref[rows, ATTN_HD * h:ATTN_HD * (h + 1)] = x_t[:, ATTN_BLOCK * hl:ATTN_BLOCK * (hl + 1)].T


def _attn_chains(sinks_ref, q_ref, kvc_ref, kvp_ref, kvm_ref, bias_ref):
    chains = [(b, j) for b in range(ATTN_STEP_BLOCKS) for j in range(ATTN_KV_HEADS)]
    q_t, ks, vs, sinkrows, biases = [], [], [], [], []
    for b, j in chains:
        rows = slice(ATTN_BLOCK * b, ATTN_BLOCK * (b + 1))
        prev = kvp_ref if b == 0 else kvc_ref
        prows = slice(0, ATTN_BLOCK) if b == 0 else slice(ATTN_BLOCK * (b - 1), ATTN_BLOCK * b)
        ksl = slice(ATTN_HD * j, ATTN_HD * (j + 1))
        vsl = slice(128 + ATTN_HD * j, 128 + ATTN_HD * (j + 1))
        ks.append(jnp.concatenate([prev[prows, ksl], kvc_ref[rows, ksl], kvm_ref[FRONT_PAD:, ksl]], axis=0))
        vs.append(jnp.concatenate([prev[prows, vsl], kvc_ref[rows, vsl], kvm_ref[FRONT_PAD:, vsl]], axis=0))
        q_t.append(_heads_to_lanes(q_ref, b, j))
        sinkrows.append(_sink_row(sinks_ref, j))
        biases.append(bias_ref[b, j])
    return chains, q_t, ks, vs, sinkrows, biases


def _attn_core_fwd(q, kv, sinks, qnw, knw):
    T = q.shape[0]
    nb = T // ATTN_BLOCK
    nbs = ATTN_STEP_BLOCKS
    assert nb % nbs == 0
    rows_all = nbs * ATTN_BLOCK

    def body(sinks_ref, q_ref, kvc_ref, kvp_ref, kvm_ref, qnw_ref, knw_ref, o_ref, bias_ref):
        i = pl.program_id(0)

        @pl.when(i <= 1)
        def _():
            for b in range(nbs):
                for j in range(ATTN_KV_HEADS):
                    bias_ref[b, j] = _attn_bias(nbs * i + b, j)
        chains, q_t, ks, vs, sinkrows, biases = _attn_chains(sinks_ref, q_ref, kvc_ref, kvp_ref, kvm_ref, bias_ref)
        o_t = _attn_groups(q_t, ks, vs, sinkrows, qnw_ref[...], knw_ref[...], biases)
        for (b, j), o in zip(chains, o_t):
            _lanes_to_heads(o_ref, b, j, o)

    return pl.pallas_call(
        body, grid=(nb // nbs,),
        in_specs=[pl.BlockSpec(memory_space=pltpu.SMEM),
                  pl.BlockSpec((rows_all, 1024), lambda i: (i, 0)),
                  pl.BlockSpec((rows_all, 256), lambda i: (i, 0)),
                  pl.BlockSpec((ATTN_BLOCK, 256), lambda i: (jnp.maximum(nbs * i - 1, 0), 0)),
                  pl.BlockSpec((ATTN_BLOCK, 256), lambda i: (0, 0)),
                  pl.BlockSpec((ATTN_HD, 1), lambda i: (0, 0)),
                  pl.BlockSpec((1, ATTN_HD), lambda i: (0, 0))],
        out_specs=pl.BlockSpec((rows_all, 1024), lambda i: (i, 0)),
        out_shape=jax.ShapeDtypeStruct((T, 1024), F32),
        scratch_shapes=_attn_table_scratch(),
        name="attn_core_fwd", compiler_params=_cparams(),
    )(sinks, q, kv, kv, kv, qnw.reshape(ATTN_HD, 1), knw)


def _attn_out_fwd(o, gate, x, front, w_out):
    T = o.shape[0]

    def body(o_ref, g_ref, xa_ref, xb_ref, xc_ref, front_ref, w_ref, h1_ref):
        h = _padded_block(pl.program_id(0), front_ref[...], (xa_ref, xb_ref, xc_ref))
        og = o_ref[...] * _silu(g_ref[...])
        h1_ref[...] = h + _bdot(og, w_ref[...])

    return _row_call("attn_out_fwd", body, T, ROW_BLOCK, [o, gate], [front, w_out], [(1024, F32)], halos=_token_views(x))[0]


def _wgrad(xn, du, cg, name):
    T, kdim = xn.shape
    cdim = du.shape[1]
    rows = next(r for r in ((2 * WGRAD_ROWS,) if kdim <= 1024 else ()) + (WGRAD_ROWS, ROW_BLOCK) if T % r == 0)
    nr, nc = T // rows, cdim // cg
    assert nc * cg == cdim

    def body(x_ref, du_ref, dw_ref):
        @pl.when(pl.program_id(1) == 0)
        def _():
            dw_ref[...] = jnp.zeros_like(dw_ref)
        dw_ref[...] += _bdot(x_ref[...], du_ref[...], TN)

    return pl.pallas_call(
        body, grid=(nc, nr),
        in_specs=[pl.BlockSpec((rows, kdim), lambda j, i: (i, 0)),
                  pl.BlockSpec((rows, cg), lambda j, i: (i, j))],
        out_specs=pl.BlockSpec((kdim, cg), lambda j, i: (0, j)),
        out_shape=jax.ShapeDtypeStruct((kdim, cdim), F32),
        name=name, compiler_params=_cparams(("arbitrary", "arbitrary")),
    )(xn, du)


def _attn_out_bwd(dh1, o, gate, w_out):
    T = o.shape[0]

    def body(dh_ref, o_ref, g_ref, w_ref, do_ref, dg_ref, dw_ref):
        @pl.when(pl.program_id(0) == 0)
        def _():
            dw_ref[...] = jnp.zeros_like(dw_ref)
        dh = dh_ref[...]
        dog = _bdot(dh, w_ref[...], NT)
        og, vjp = jax.vjp(lambda o_, g_: o_ * _silu(g_), o_ref[...], g_ref[...])
        do, dg = vjp(dog)
        do_ref[...] = do
        dg_ref[...] = dg
        dw_ref[...] += _bdot(og, dh, TN)

    return _row_call("attn_out_bwd", body, T, ROW_BLOCK, [dh1, o, gate], [w_out],
                     [(1024, F32), (1024, F32)], [((1024, 1024), F32)])


def _attn_core_bwd(do, q, kv, sinks, qnw, knw):
    T = q.shape[0]
    nbs = ATTN_STEP_BLOCKS
    ns = T // (nbs * ATTN_BLOCK)
    rows_all = nbs * ATTN_BLOCK
    rev = lambda i: ns - 1 - i

    def body(sinks_ref, do_ref, q_ref, kvc_ref, kvp_ref, kvm_ref, qnw_ref, knw_ref,
             dq_ref, dkv_ref, dsinks_ref, dqnw_ref, dknw_ref, carry_ref, meta_ref, bias_ref):
        step = pl.program_id(0)
        i = rev(step)

        @pl.when((step == 0) | (i == 0))
        def _():
            for b in range(nbs):
                for j in range(ATTN_KV_HEADS):
                    bias_ref[b, j] = _attn_bias(nbs * i + b, j)

        @pl.when(step == 0)
        def _():
            carry_ref[...] = jnp.zeros_like(carry_ref)
            meta_ref[...] = jnp.zeros_like(meta_ref)
            dsinks_ref[...] = jnp.zeros_like(dsinks_ref)
            dqnw_ref[...] = jnp.zeros_like(dqnw_ref)
            dknw_ref[...] = jnp.zeros_like(dknw_ref)

        lane16 = lax.broadcasted_iota(jnp.int32, (1, ATTN_HEADS), 1)
        dsinks = jnp.zeros((1, ATTN_HEADS), F32)
        chains, q_t, ks, vs, sinkrows, biases = _attn_chains(sinks_ref, q_ref, kvc_ref, kvp_ref, kvm_ref, bias_ref)
        fn = functools.partial(_attn_groups, bias=biases, late_norm=False)
        _, vjp = jax.vjp(fn, q_t, ks, vs, sinkrows, qnw_ref[...], knw_ref[...])
        dq_t, dks, dvs, dsr, dqn, dkn = vjp([_heads_to_lanes(do_ref, b, j) for b, j in chains])
        dqnw_ref[...] += dqn
        dknw_ref[...] += dkn
        part = {}
        for c, (b, j) in enumerate(chains):
            _lanes_to_heads(dq_ref, b, j, dq_t[c])
            for hl in range(ATTN_GROUPS):
                dsinks = dsinks + jnp.where(lane16 == ATTN_GROUPS * j + hl,
                                            jnp.sum(dsr[c][:, ATTN_BLOCK * hl:ATTN_BLOCK * (hl + 1)]), 0.0)
            part[b, j] = (dks[c], dvs[c])
        for j in range(ATTN_KV_HEADS):
            for kind, sl in ((0, slice(ATTN_HD * j, ATTN_HD * (j + 1))), (1, slice(128 + ATTN_HD * j, 128 + ATTN_HD * (j + 1)))):
                for b in range(nbs):
                    d = part[b, j][kind]
                    nxt = part[b + 1, j][kind][0:ATTN_BLOCK, :] if b + 1 < nbs else carry_ref[:, sl]
                    dkv_ref[ATTN_BLOCK * b:ATTN_BLOCK * (b + 1), sl] = d[ATTN_BLOCK:2 * ATTN_BLOCK, :] + nxt
                    meta_ref[:, sl] += d[2 * ATTN_BLOCK:, :]
                carry_ref[:, sl] = part[0, j][kind][0:ATTN_BLOCK, :]
        dsinks_ref[...] += dsinks

        @pl.when(i == 0)
        def _():
            dkv_ref[FRONT_PAD:ATTN_BLOCK, :] += meta_ref[...]

    dq, dkv, dsinks, dqnw, dknw = pl.pallas_call(
        body, grid=(ns,),
        in_specs=[pl.BlockSpec(memory_space=pltpu.SMEM),
                  pl.BlockSpec((rows_all, 1024), lambda i: (rev(i), 0)),
                  pl.BlockSpec((rows_all, 1024), lambda i: (rev(i), 0)),
                  pl.BlockSpec((rows_all, 256), lambda i: (rev(i), 0)),
                  pl.BlockSpec((ATTN_BLOCK, 256), lambda i: (jnp.maximum(nbs * rev(i) - 1, 0), 0)),
                  pl.BlockSpec((ATTN_BLOCK, 256), lambda i: (0, 0)),
                  pl.BlockSpec((ATTN_HD, 1), lambda i: (0, 0)),
                  pl.BlockSpec((1, ATTN_HD), lambda i: (0, 0))],
        out_specs=[pl.BlockSpec((rows_all, 1024), lambda i: (rev(i), 0)),
                   pl.BlockSpec((rows_all, 256), lambda i: (rev(i), 0)),
                   pl.BlockSpec((1, ATTN_HEADS), lambda i: (0, 0)),
                   pl.BlockSpec((ATTN_HD, 1), lambda i: (0, 0)),
                   pl.BlockSpec((1, ATTN_HD), lambda i: (0, 0))],
        out_shape=[jax.ShapeDtypeStruct((T, 1024), F32), jax.ShapeDtypeStruct((T, 256), F32),
                   jax.ShapeDtypeStruct((1, ATTN_HEADS), F32), jax.ShapeDtypeStruct((ATTN_HD, 1), F32),
                   jax.ShapeDtypeStruct((1, ATTN_HD), F32)],
        scratch_shapes=[pltpu.VMEM((ATTN_BLOCK, 256), F32), pltpu.VMEM((N_META, 256), F32)] + _attn_table_scratch(),
        name="attn_core_bwd", compiler_params=_cparams(),
    )(sinks, do, q, kv, kv, kv, qnw.reshape(ATTN_HD, 1), knw)
    return dq, dkv, dsinks, dqnw.reshape(1, ATTN_HD), dknw


def _attn_in_bwd(dq, dkv, dgate, x, front, dh1, norm_w, w_in):
    T = dq.shape[0]
    n = T // ROW_BLOCK
    per = ROW_BLOCK // ATTN_BLOCK
    assert n >= 3

    def body(dq_ref, dkv_ref, dg_ref, dh1_ref, xa_ref, xb_ref, xc_ref, front_ref, nw_ref, w_ref,
             gx_ref, dfront_ref, dnw_ref, buf_ref, sems):
        i = pl.program_id(0)
        slot = i % 2

        def piece(step, k, s):
            return pltpu.make_async_copy(buf_ref.at[s, pl.ds(ATTN_BLOCK * k, ATTN_BLOCK)],
                                         gx_ref.at[pl.ds((per * step - 1 + k) * ATTN_BLOCK, ATTN_BLOCK)], sems.at[s, k])

        @pl.when(i == 0)
        def _():
            dnw_ref[...] = jnp.zeros_like(dnw_ref)
        for k in range(per):
            @pl.when((i >= 2) & ((k > 0) | (i > 2)))
            def _():
                piece(i - 2, k, slot).wait()
        h = _padded_block(i, front_ref[...], (xa_ref, xb_ref, xc_ref))
        dxn = (_bdot(dq_ref[...], w_ref[:, 0:1024], NT) + _bdot(dkv_ref[...], w_ref[:, 1024:1280], NT)
               + _bdot(dg_ref[...], w_ref[:, 1280:2304], NT))
        _, vjp = jax.vjp(_rms, h, nw_ref[...])
        dh, dnw = vjp(dxn)
        dnw_ref[...] += dnw
        buf_ref[slot] = dh1_ref[...] + dh

        @pl.when(i == 0)
        def _():
            dfront_ref[...] = buf_ref[0, 0:ATTN_BLOCK, :]
        for k in range(per):
            @pl.when((k > 0) | (i > 0))
            def _():
                piece(i, k, slot).start()

        @pl.when(i == n - 1)
        def _():
            for k in range(per):
                piece(i, k, slot).wait()
                piece(i - 1, k, 1 - slot).wait()

    idx = lambda i: (i, 0)
    const = lambda a: pl.BlockSpec(a.shape, functools.partial(lambda i, nd: (0,) * nd, nd=a.ndim))
    rows = [dq, dkv, dgate, dh1]
    views = _token_views(x)
    return pl.pallas_call(
        body, grid=(n,),
        in_specs=[pl.BlockSpec((ROW_BLOCK, a.shape[1]), idx) for a in rows]
        + [pl.BlockSpec((hr, a.shape[1]), fn) for a, hr, fn in views] + [const(front), const(norm_w), const(w_in)],
        out_specs=[pl.BlockSpec(memory_space=pltpu.HBM), pl.BlockSpec((ATTN_BLOCK, D_MODEL), lambda i: (0, 0)),
                   pl.BlockSpec((1, D_MODEL), lambda i: (0, 0))],
        out_shape=[jax.ShapeDtypeStruct(x.shape, F32), jax.ShapeDtypeStruct((ATTN_BLOCK, D_MODEL), F32),
                   jax.ShapeDtypeStruct((1, D_MODEL), F32)],
        scratch_shapes=[pltpu.VMEM((2, ROW_BLOCK, D_MODEL), F32), pltpu.SemaphoreType.DMA((2, per))],
        name="attn_in_bwd", compiler_params=_cparams(),
    )(*rows, *[a for a, _, _ in views], front, norm_w, w_in)


def _dn_in_fwd(h1, norm_w, w_in):
    T = h1.shape[0]

    def body(h_ref, nw_ref, w_ref, xn_ref, qkv_ref, z_ref, ba_ref):
        xn = _rms(h_ref[...], nw_ref[...]).astype(BF16)
        xn_ref[...] = xn
        qkv_ref[...] = jnp.dot(xn, w_ref[:, 0:4096], preferred_element_type=F32)
        z_ref[...] = jnp.dot(xn, w_ref[:, 4096:6144], preferred_element_type=F32)
        ba_ref[...] = jnp.dot(xn, w_ref[:, 6144:6176], preferred_element_type=F32)

    return _row_call("dn_in_fwd", body, T, ROW_BLOCK, [h1], [norm_w, w_in],
                     [(1024, BF16), (4096, F32), (2048, F32), (32, F32)])


def _attn_out_dn_in_fwd(o, gate, x, front, w_out, dn_norm_w, w_in):
    T = o.shape[0]

    def body(o_ref, g_ref, xa_ref, xb_ref, xc_ref, front_ref, wo_ref, nw_ref, wi_ref,
             h1_ref, xn_ref, qkv_ref, z_ref, ba_ref):
        h = _padded_block(pl.program_id(0), front_ref[...], (xa_ref, xb_ref, xc_ref))
        h1 = h + _bdot(o_ref[...] * _silu(g_ref[...]), wo_ref[...])
        h1_ref[...] = h1
        xn = _rms(h1, nw_ref[...]).astype(BF16)
        xn_ref[...] = xn
        qkv_ref[...] = jnp.dot(xn, wi_ref[:, 0:4096], preferred_element_type=F32)
        z_ref[...] = jnp.dot(xn, wi_ref[:, 4096:6144], preferred_element_type=F32)
        ba_ref[...] = jnp.dot(xn, wi_ref[:, 6144:6176], preferred_element_type=F32)

    return _row_call("attn_out_dn_in_fwd", body, T, ROW_BLOCK, [o, gate], [front, w_out, dn_norm_w, w_in],
                     [(1024, F32), (1024, BF16), (4096, F32), (2048, F32), (32, F32)], halos=_token_views(x),
                     single_buffer_consts=True)


def _shift_down(cur, prev8, s):
    i8 = lax.broadcasted_iota(jnp.int32, (8, cur.shape[1]), 0)
    r = pltpu.roll(cur, s, 0)
    head = jnp.where(i8 < s, pltpu.roll(prev8, s, 0), r[0:8])
    return jnp.concatenate([head, r[8:]], axis=0)


def _shift_up(cur, next8, s):
    n = cur.shape[0]
    i8 = lax.broadcasted_iota(jnp.int32, (8, cur.shape[1]), 0)
    r = pltpu.roll(cur, n - s, 0)
    tail = jnp.where(i8 >= 8 - s, pltpu.roll(next8, 8 - s, 0), r[n - 8:])
    return jnp.concatenate([r[:n - 8], tail], axis=0)


def _conv_taps(cur, prev8):
    return [cur] + [_shift_down(cur, prev8, s) for s in range(1, DN_CONV_K)]


def _conv_tile(taps, w):
    out = w[3:4, :] * taps[0]
    for s in range(1, DN_CONV_K):
        out = out + w[3 - s:4 - s, :] * taps[s]
    return out


def _l2n(a, scale):
    return a * (lax.rsqrt(jnp.sum(a * a, axis=-1, keepdims=True) + NORM_EPS) * scale)


def _dn_post_tile(c, t):
    a = _silu(c)
    if t < DN_K_HEADS:
        return _l2n(a, DN_HD ** -0.5)
    if t < 2 * DN_K_HEADS:
        return _l2n(a, 1.0)
    return a


def _dn_beta_g(ba, a_log, dt_bias, live):
    beta = jax.nn.sigmoid(ba[:, 0:DN_V_HEADS]) * live
    g = -jnp.exp(a_log) * _softplus(ba[:, DN_V_HEADS:] + dt_bias) * live
    return beta, g


def _live_rows(i, rb):
    rows = i * rb + lax.broadcasted_iota(jnp.int32, (rb, 1), 0)
    return (rows >= FRONT_PAD).astype(F32)


def _halo_spec_args(x, rb):
    per = rb // 8
    return (x, 8, lambda i: (jnp.maximum(i * per - 1, 0), 0))


def _dn_conv_fwd(qkv, ba, conv_w, a_log, dt_bias):
    T = qkv.shape[0]

    def body(x_ref, ba_ref, halo_ref, cw_ref, al_ref, dtb_ref, q_ref, k_ref, v_ref, bg_ref, c_ref):
        i = pl.program_id(0)
        first = (i > 0).astype(F32)
        for t in range(DN_CONV_W // 128):
            cols = slice(128 * t, 128 * (t + 1))
            c = _conv_tile(_conv_taps(x_ref[:, cols], halo_ref[:, cols] * first), cw_ref[:, cols])
            c_ref[:, cols] = c.astype(BF16)
            out = _dn_post_tile(c, t)
            if t < DN_K_HEADS:
                q_ref[:, cols] = out
            elif t < 2 * DN_K_HEADS:
                k_ref[:, 128 * (t - 8):128 * (t - 7)] = out
            else:
                v_ref[:, 128 * (t - 16):128 * (t - 15)] = out
        beta, g = _dn_beta_g(ba_ref[...], al_ref[...], dtb_ref[...], _live_rows(i, ROW_BLOCK))
        bg_ref[:, 0:DN_V_HEADS] = beta
        bg_ref[:, DN_V_HEADS:] = g

    return _row_call("dn_conv_fwd", body, T, ROW_BLOCK, [qkv, ba], [conv_w, a_log, dt_bias],
                     [(1024, F32), (1024, F32), (2048, F32), (32, F32), (4096, BF16)], halos=[_halo_spec_args(qkv, ROW_BLOCK)])


def _chunk_masks():
    r = lax.broadcasted_iota(jnp.int32, (DN_CHUNK, DN_CHUNK), 0)
    c = lax.broadcasted_iota(jnp.int32, (DN_CHUNK, DN_CHUNK), 1)
    return r >= c, r > c, r == c, r <= c


def _tri_inv_block(x):
    B = TRI_BLOCK
    n = range(len(x))
    r_, c_ = lax.broadcasted_iota(jnp.int32, (B, B), 0), lax.broadcasted_iota(jnp.int32, (B, B), 1)
    ainv = [jnp.where(r_ == c_, 1.0, 0.0) + x[h] for h in n]
    p = [_bdot(x[h], x[h]) for h in n]
    for _ in range(B.bit_length() - 3):
        r = [_bdot(jnp.concatenate([p[h], ainv[h]], axis=0), p[h]) for h in n]
        ainv = [ainv[h] + r[h][B:] for h in n]
        p = [r[h][:B] for h in n]
    return [ainv[h] + _bdot(ainv[h], p[h]) for h in n]


def _tri_inv(x):
    B = TRI_BLOCK
    assert DN_CHUNK == 2 * B
    n = len(x)
    diag = _tri_inv_block([x[h][:B, :B] for h in range(n)] + [x[h][B:, B:] for h in range(n)])
    a11, a22 = diag[:n], diag[n:]
    a21 = [_bdot(_bdot(a22[h], x[h][B:, :B]), a11[h]) for h in range(n)]
    zero = jnp.zeros((B, B), F32)
    return [jnp.concatenate([jnp.concatenate([a11[h], zero], axis=1), jnp.concatenate([a21[h], a22[h]], axis=1)], axis=0)
            for h in range(n)]


@jax.custom_vjp
def _tri_inv_known(x, a):
    return a


def _tri_inv_known_fwd(x, a):
    return a, a


def _tri_inv_known_bwd(a, da):
    return [_bdot(_bdot(a[h], da[h], TN), a[h], NT) for h in range(len(a))], [jnp.zeros_like(t) for t in a]


_tri_inv_known.defvjp(_tri_inv_known_fwd, _tri_inv_known_bwd)


@jax.custom_vjp
def _known(computed, value):
    return value


def _known_fwd(computed, value):
    return value, None


def _known_bwd(_, g):
    return g, jax.tree.map(jnp.zeros_like, g)


_known.defvjp(_known_fwd, _known_bwd)


def _dn_chunk_step(S, q, k, v, beta, g, masks, known=None):
    causal, strict, eye, upper = masks
    C, W = DN_CHUNK, DN_HD
    heads = range(len(v))
    k_t = [k[j].T for j in range(len(k))]
    qk_kk = [_bdot(jnp.concatenate([q[j], k[j]], axis=0), k_t[j]) for j in range(len(q))]
    if known is not None:
        qk_kk = _known(qk_kk, known["qk_kk"])
    g_b = [jnp.broadcast_to(g[h], (C, C)) for h in heads]
    beta_b = [jnp.broadcast_to(beta[h], (C, W)) for h in heads]
    g_row = [jnp.sum(jnp.where(eye, g_b[h], 0.0), axis=0, keepdims=True) for h in heads]
    gc_col = [jnp.sum(jnp.where(causal, g_row[h], 0.0), axis=1, keepdims=True) for h in heads]
    gc_row = [jnp.sum(jnp.where(upper, g_b[h], 0.0), axis=0, keepdims=True) for h in heads]
    g_last = [jnp.sum(g_row[h], axis=1, keepdims=True) for h in heads]
    gc_b = [jnp.broadcast_to(gc_col[h], (C, W)) for h in heads]
    decay = [jnp.exp(jnp.where(causal, gc_b[h][:, :C] - gc_row[h], NEG)) for h in heads]
    eg_b = [jnp.exp(gc_b[h]) for h in heads]
    x = [jnp.where(strict, qk_kk[h // 2][C:] * beta_b[h][:, :C] * decay[h], 0.0) * -1.0 for h in heads]
    ainv = _tri_inv(x) if known is None else _tri_inv_known(x, known["inv"])
    uw = [_bdot(ainv[h], jnp.concatenate([v[h] * beta_b[h], k[h // 2] * (beta_b[h] * eg_b[h])], axis=1)) for h in heads]
    if known is not None:
        uw = _known(uw, known["uw"])
    q_eg = [q[h // 2] * eg_b[h] for h in heads]
    attn = [qk_kk[h // 2][:C] * decay[h] for h in heads]
    k_st = [k_t[h // 2] * jnp.exp(g_last[h] - gc_row[h]) for h in heads]
    s_dec = [jnp.exp(g_last[h]) for h in heads]
    prep = (uw, q_eg, attn, k_st, s_dec)
    if S is None:
        return prep, dict(inv=ainv, uw=uw, qk_kk=qk_kk)
    s_new, o, _ = _dn_chunk_tail(S, prep, None if known is None else known["v_new"])
    return s_new, o


def _dn_chunk_tail(S, prep, known_v_new=None):
    uw, q_eg, attn, k_st, s_dec = prep
    C, W = DN_CHUNK, DN_HD
    heads = range(len(uw))
    ws_qs = [_bdot(jnp.concatenate([uw[h][:, W:], q_eg[h]], axis=0), S[h]) for h in heads]
    v_new = [uw[h][:, :W] - ws_qs[h][:C] for h in heads]
    if known_v_new is not None:
        v_new = _known(v_new, known_v_new)
    o = [ws_qs[h][C:] + _bdot(attn[h], v_new[h]) for h in heads]
    s_new = [S[h] * s_dec[h] + _bdot(k_st[h], v_new[h]) for h in heads]
    return s_new, o, v_new


def _dn_chunk_tiles(q_ref, k_ref, v_ref, bg_ref, c, first, count):
    rows = slice(DN_CHUNK * c, DN_CHUNK * (c + 1))
    q = [q_ref[rows, 128 * j:128 * (j + 1)] for j in range(first // 2, (first + count) // 2)]
    k = [k_ref[rows, 128 * j:128 * (j + 1)] for j in range(first // 2, (first + count) // 2)]
    v = [v_ref[rows, 128 * h:128 * (h + 1)] for h in range(first, first + count)]
    beta = [bg_ref[rows, h:h + 1] for h in range(first, first + count)]
    g = [bg_ref[rows, DN_V_HEADS + h:DN_V_HEADS + h + 1] for h in range(first, first + count)]
    return q, k, v, beta, g


def _dn_scan_fwd(qn, kn, v, bg):
    T = qn.shape[0]
    nc = T // DN_CHUNK
    rows = SCAN_CHUNKS * DN_CHUNK
    assert nc % SCAN_CHUNKS == 0

    def body(q_ref, k_ref, v_ref, bg_ref, o_ref, ssave_ref, inv_ref, uw_ref, vn_ref, qk_ref, s_ref):
        @pl.when(pl.program_id(0) == 0)
        def _():
            s_ref[...] = jnp.zeros_like(s_ref)
        masks = _chunk_masks()
        for first in range(0, DN_V_HEADS, SCAN_FWD_GROUP):
            heads = range(first, first + SCAN_FWD_GROUP)
            preps = [_dn_chunk_step(None, *_dn_chunk_tiles(q_ref, k_ref, v_ref, bg_ref, c, first, SCAN_FWD_GROUP), masks)
                     for c in range(SCAN_CHUNKS)]
            state = [s_ref[h] for h in heads]
            for c, (prep, saved) in enumerate(preps):
                for i, h in enumerate(heads):
                    ssave_ref[c, h] = state[i]
                    inv_ref[c, h] = saved["inv"][i].astype(BF16)
                    uw_ref[c, h] = saved["uw"][i].astype(BF16)
                for i, j in enumerate(range(first // 2, (first + SCAN_FWD_GROUP) // 2)):
                    qk_ref[c, j] = saved["qk_kk"][i].astype(BF16)
                state, o, v_new = _dn_chunk_tail(state, prep)
                for i, h in enumerate(heads):
                    o_ref[DN_CHUNK * c:DN_CHUNK * (c + 1), 128 * h:128 * (h + 1)] = o[i]
                    vn_ref[c, h] = v_new[i].astype(BF16)
            for i, h in enumerate(heads):
                s_ref[h] = state[i]

    return pl.pallas_call(
        body, grid=(nc // SCAN_CHUNKS,),
        in_specs=[pl.BlockSpec((rows, 1024), lambda i: (i, 0)),
                  pl.BlockSpec((rows, 1024), lambda i: (i, 0)),
                  pl.BlockSpec((rows, 2048), lambda i: (i, 0)),
                  pl.BlockSpec((rows, 32), lambda i: (i, 0))],
        out_specs=[pl.BlockSpec((rows, 2048), lambda i: (i, 0)),
                   pl.BlockSpec((SCAN_CHUNKS, DN_V_HEADS, DN_HD, DN_HD), lambda i: (i, 0, 0, 0)),
                   pl.BlockSpec((SCAN_CHUNKS, DN_V_HEADS, DN_CHUNK, DN_CHUNK), lambda i: (i, 0, 0, 0)),
                   pl.BlockSpec((SCAN_CHUNKS, DN_V_HEADS, DN_CHUNK, 2 * DN_HD), lambda i: (i, 0, 0, 0)),
                   pl.BlockSpec((SCAN_CHUNKS, DN_V_HEADS, DN_CHUNK, DN_HD), lambda i: (i, 0, 0, 0)),
                   pl.BlockSpec((SCAN_CHUNKS, DN_K_HEADS, 2 * DN_CHUNK, DN_CHUNK), lambda i: (i, 0, 0, 0))],
        out_shape=[jax.ShapeDtypeStruct((T, 2048), F32),
                   jax.ShapeDtypeStruct((nc, DN_V_HEADS, DN_HD, DN_HD), F32),
                   jax.ShapeDtypeStruct((nc, DN_V_HEADS, DN_CHUNK, DN_CHUNK), BF16),
                   jax.ShapeDtypeStruct((nc, DN_V_HEADS, DN_CHUNK, 2 * DN_HD), BF16),
                   jax.ShapeDtypeStruct((nc, DN_V_HEADS, DN_CHUNK, DN_HD), BF16),
                   jax.ShapeDtypeStruct((nc, DN_K_HEADS, 2 * DN_CHUNK, DN_CHUNK), BF16)],
        scratch_shapes=[pltpu.VMEM((DN_V_HEADS, DN_HD, DN_HD), F32)],
        name="dn_scan_fwd", compiler_params=_cparams(),
    )(qn, kn, v, bg)


def _dn_gate_tile(o, z, onw):
    return _rms(o, onw) * _silu(z)


def _dn_out_fwd(o, z, h1, target, w_out, onw):
    T = o.shape[0]

    def body(o_ref, z_ref, h_ref, ta_ref, tb_ref, tc_ref, w_ref, onw_ref, dy_ref, og_ref, loss_ref):
        i = pl.program_id(0)

        @pl.when(i == 0)
        def _():
            loss_ref[...] = jnp.zeros_like(loss_ref)
        for h in range(DN_V_HEADS):
            cols = slice(128 * h, 128 * (h + 1))
            og_ref[:, cols] = _dn_gate_tile(o_ref[:, cols], z_ref[:, cols], onw_ref[...]).astype(BF16)
        y = h_ref[...] + jnp.dot(og_ref[...], w_ref[...], preferred_element_type=F32)
        rows = i * ROW_BLOCK + lax.broadcasted_iota(jnp.int32, (ROW_BLOCK, 1), 0)
        diff = jnp.where(rows >= FRONT_PAD + N_META, y - _padded_block(i, None, (ta_ref, tb_ref, tc_ref)), 0.0)
        dy_ref[...] = diff * (1.0 / D_MODEL)
        loss_ref[...] += jnp.sum(diff * diff) * (0.5 / D_MODEL)

    return _row_call("dn_out_fwd", body, T, ROW_BLOCK, [o, z, h1], [w_out, onw],
                     [(1024, F32), (2048, BF16)], [((1, 128), F32)], halos=_token_views(target))


def _dn_out_bwd(dy, o, z, w_out, onw):
    T = o.shape[0]

    def body(dy_ref, o_ref, z_ref, w_ref, onw_ref, do_ref, dz_ref, donw_ref, dog_ref):
        @pl.when(pl.program_id(0) == 0)
        def _():
            donw_ref[...] = jnp.zeros_like(donw_ref)
        dy = dy_ref[...].astype(BF16)
        donw = jnp.zeros((1, DN_HD), F32)
        for half in range(2):
            hcols = slice(1024 * half, 1024 * (half + 1))
            dog_ref[:, hcols] = lax.dot_general(dy, w_ref[hcols, :], NT, preferred_element_type=F32)
        for h in range(DN_V_HEADS):
            cols = slice(128 * h, 128 * (h + 1))
            _, vjp = jax.vjp(_dn_gate_tile, o_ref[:, cols], z_ref[:, cols], onw_ref[...])
            do, dz, dn = vjp(dog_ref[:, cols])
            do_ref[:, cols] = do
            dz_ref[:, cols] = dz
            donw = donw + dn
        donw_ref[...] += donw

    return _row_call("dn_out_bwd", body, T, ROW_BLOCK, [dy, o, z], [w_out, onw],
                     [(2048, F32), (2048, F32)], [((1, DN_HD), F32)], scratch=[pltpu.VMEM((ROW_BLOCK, 2048), F32)])


def _dn_scan_bwd(do, qn, kn, v, bg, ssave, saved):
    T = qn.shape[0]
    nc = T // DN_CHUNK
    chunks = SCAN_BWD_CHUNKS
    ns = nc // chunks
    rows = chunks * DN_CHUNK
    rev = lambda i: ns - 1 - i

    def body(do_ref, q_ref, k_ref, v_ref, bg_ref, ss_ref, inv_ref, uw_ref, vn_ref, qk_ref,
             dq_ref, dk_ref, dv_ref, dbg_ref, ds_ref):
        @pl.when(pl.program_id(0) == 0)
        def _():
            ds_ref[...] = jnp.zeros_like(ds_ref)
        lane32 = lax.broadcasted_iota(jnp.int32, (1, 2 * DN_V_HEADS), 1)
        masks = _chunk_masks()
        dbg = [jnp.zeros((DN_CHUNK, 2 * DN_V_HEADS), F32) for _ in range(chunks)]
        for first in range(0, DN_V_HEADS, SCAN_BWD_GROUP):
            heads = range(first, first + SCAN_BWD_GROUP)
            vjps = []
            for c in range(chunks):
                known = dict(inv=[inv_ref[c, h].astype(F32) for h in heads], uw=[uw_ref[c, h].astype(F32) for h in heads],
                             v_new=[vn_ref[c, h].astype(F32) for h in heads],
                             qk_kk=[qk_ref[c, j].astype(F32) for j in range(first // 2, (first + SCAN_BWD_GROUP) // 2)])
                fn = functools.partial(_dn_chunk_step, masks=masks, known=known)
                vjps.append(jax.vjp(fn, [ss_ref[c, h] for h in heads],
                                    *_dn_chunk_tiles(q_ref, k_ref, v_ref, bg_ref, c, first, SCAN_BWD_GROUP))[1])
            ds = [ds_ref[h] for h in heads]
            for c in reversed(range(chunks)):
                crows = slice(DN_CHUNK * c, DN_CHUNK * (c + 1))
                ds, dq, dk, dv, dbeta, dg = vjps[c]((ds, [do_ref[crows, 128 * h:128 * (h + 1)] for h in heads]))
                for i, h in enumerate(heads):
                    dv_ref[crows, 128 * h:128 * (h + 1)] = dv[i]
                    dbg[c] = dbg[c] + jnp.where(lane32 == h, dbeta[i], 0.0) + jnp.where(lane32 == DN_V_HEADS + h, dg[i], 0.0)
                for i, j in enumerate(range(first // 2, (first + SCAN_BWD_GROUP) // 2)):
                    dq_ref[crows, 128 * j:128 * (j + 1)] = dq[i]
                    dk_ref[crows, 128 * j:128 * (j + 1)] = dk[i]
            for i, h in enumerate(heads):
                ds_ref[h] = ds[i]
        for c in range(chunks):
            dbg_ref[DN_CHUNK * c:DN_CHUNK * (c + 1), :] = dbg[c]

    return pl.pallas_call(
        body, grid=(ns,),
        in_specs=[pl.BlockSpec((rows, 2048), lambda i: (rev(i), 0)),
                  pl.BlockSpec((rows, 1024), lambda i: (rev(i), 0)),
                  pl.BlockSpec((rows, 1024), lambda i: (rev(i), 0)),
                  pl.BlockSpec((rows, 2048), lambda i: (rev(i), 0)),
                  pl.BlockSpec((rows, 32), lambda i: (rev(i), 0)),
                  pl.BlockSpec((chunks, DN_V_HEADS, DN_HD, DN_HD), lambda i: (rev(i), 0, 0, 0)),
                  pl.BlockSpec((chunks, DN_V_HEADS, DN_CHUNK, DN_CHUNK), lambda i: (rev(i), 0, 0, 0)),
                  pl.BlockSpec((chunks, DN_V_HEADS, DN_CHUNK, 2 * DN_HD), lambda i: (rev(i), 0, 0, 0)),
                  pl.BlockSpec((chunks, DN_V_HEADS, DN_CHUNK, DN_HD), lambda i: (rev(i), 0, 0, 0)),
                  pl.BlockSpec((chunks, DN_K_HEADS, 2 * DN_CHUNK, DN_CHUNK), lambda i: (rev(i), 0, 0, 0))],
        out_specs=[pl.BlockSpec((rows, 1024), lambda i: (rev(i), 0)),
                   pl.BlockSpec((rows, 1024), lambda i: (rev(i), 0)),
                   pl.BlockSpec((rows, 2048), lambda i: (rev(i), 0)),
                   pl.BlockSpec((rows, 32), lambda i: (rev(i), 0))],
        out_shape=[jax.ShapeDtypeStruct((T, 1024), F32), jax.ShapeDtypeStruct((T, 1024), F32),
                   jax.ShapeDtypeStruct((T, 2048), F32), jax.ShapeDtypeStruct((T, 32), F32)],
        scratch_shapes=[pltpu.VMEM((DN_V_HEADS, DN_HD, DN_HD), F32)],
        name="dn_scan_bwd", compiler_params=_cparams(),
    )(do, qn, kn, v, bg, ssave, *saved)


def _dn_conv_bwd(dqn, dkn, dv, dbg, qkv, conv_out, ba, conv_w, a_log, dt_bias):
    T = qkv.shape[0]
    rb = ROW_BLOCK // 2
    nr = T // rb

    def body(dq_ref, dk_ref, dv_ref, dbg_ref, x_ref, c_ref, ba_ref, cw_ref, al_ref, dtb_ref,
             dx_ref, dba_ref, dcw_ref, dal_ref, ddtb_ref, carry_ref):
        step = pl.program_id(0)
        i = nr - 1 - step

        @pl.when(step == 0)
        def _():
            carry_ref[...] = jnp.zeros_like(carry_ref)
            dcw_ref[...] = jnp.zeros_like(dcw_ref)
            dal_ref[...] = jnp.zeros_like(dal_ref)
            ddtb_ref[...] = jnp.zeros_like(ddtb_ref)
        for t in range(DN_CONV_W // 128):
            cols = slice(128 * t, 128 * (t + 1))
            w, x = cw_ref[:, cols], x_ref[:, cols]
            if t < DN_K_HEADS:
                dout = dq_ref[:, cols]
            elif t < 2 * DN_K_HEADS:
                dout = dk_ref[:, 128 * (t - 8):128 * (t - 7)]
            else:
                dout = dv_ref[:, 128 * (t - 16):128 * (t - 15)]
            _, vjp = jax.vjp(functools.partial(_dn_post_tile, t=t), c_ref[:, cols].astype(F32))
            (dc,) = vjp(dout)
            nxt = carry_ref[:, cols]
            dx = w[3:4, :] * dc
            dcw_ref[3:4, cols] += jnp.sum(dc * x, axis=0, keepdims=True)
            for s in range(1, DN_CONV_K):
                up = _shift_up(dc, nxt, s)
                dx = dx + w[3 - s:4 - s, :] * up
                dcw_ref[3 - s:4 - s, cols] += jnp.sum(up * x, axis=0, keepdims=True)
            dx_ref[:, cols] = dx
            carry_ref[:, cols] = dc[0:8, :]
        fn = functools.partial(_dn_beta_g, live=_live_rows(i, rb))
        _, vjp = jax.vjp(fn, ba_ref[...], al_ref[...], dtb_ref[...])
        dba, dal, ddtb = vjp((dbg_ref[:, 0:DN_V_HEADS], dbg_ref[:, DN_V_HEADS:]))
        dba_ref[...] = dba
        dal_ref[...] += dal
        ddtb_ref[...] += ddtb

    return _row_call("dn_conv_bwd", body, T, rb, [dqn, dkn, dv, dbg, qkv, conv_out, ba], [conv_w, a_log, dt_bias],
                     [(4096, F32), (32, F32)], [((DN_CONV_K, 4096), F32), ((1, DN_V_HEADS), F32), ((1, DN_V_HEADS), F32)],
                     reverse=True, scratch=[pltpu.VMEM((8, 4096), F32)])


def _dn_in_bwd(dqkv, dz, dba, h1, dy, norm_w, w_in):
    T = h1.shape[0]

    def body(dqkv_ref, dz_ref, dba_ref, h_ref, dy_ref, nw_ref, w_ref, dh_ref, dnw_ref):
        @pl.when(pl.program_id(0) == 0)
        def _():
            dnw_ref[...] = jnp.zeros_like(dnw_ref)
        dxn = (_bdot(dqkv_ref[...], w_ref[:, 0:4096], NT) + _bdot(dz_ref[...], w_ref[:, 4096:6144], NT)
               + _bdot(dba_ref[...], w_ref[:, 6144:6176], NT))
        _, vjp = jax.vjp(_rms, h_ref[...], nw_ref[...])
        dh, dnw = vjp(dxn)
        dh_ref[...] = (dy_ref[...] + dh) * _live_rows(pl.program_id(0), ROW_BLOCK)
        dnw_ref[...] += dnw

    return _row_call("dn_in_bwd", body, T, ROW_BLOCK, [dqkv, dz, dba, h1, dy], [norm_w, w_in],
                     [(1024, F32)], [((1, 1024), F32)])


def _exchange(parts, scatter, name):
    n = len(parts)
    out_shape = [jax.ShapeDtypeStruct(p.shape if sc else (N_DEV,) + p.shape, p.dtype) for p, sc in zip(parts, scatter)]

    def body(*refs):
        ins, outs = refs[:n], refs[n:2 * n]
        send_sems, recv_sems, local_sems = refs[2 * n:]
        x, y, c = lax.axis_index("x"), lax.axis_index("y"), lax.axis_index("c")
        me = 4 * x + 2 * y + c
        peers = []
        for k in range(1, N_DEV):
            px = 1 - x if k & 4 else x
            py = 1 - y if k & 2 else y
            pc = 1 - c if k & 1 else c
            peers.append(((px, py, pc), 4 * px + 2 * py + pc))

        def src(a, idx):
            return ins[a].at[idx] if scatter[a] else ins[a]

        local = [pltpu.make_async_copy(src(a, me), outs[a].at[me], local_sems.at[a]) for a in range(n)]
        for cp in local:
            cp.start()
        for a in range(n):
            for k, (dev, idx) in enumerate(peers):
                pltpu.make_async_remote_copy(
                    src_ref=src(a, idx), dst_ref=outs[a].at[me], send_sem=send_sems.at[a, k], recv_sem=recv_sems.at[a, k],
                    device_id=dev, device_id_type=pl.DeviceIdType.MESH).start()
        for a in range(n):
            for k, (dev, idx) in enumerate(peers):
                pltpu.make_async_remote_copy(
                    src_ref=src(a, idx), dst_ref=outs[a].at[idx], send_sem=send_sems.at[a, k], recv_sem=recv_sems.at[a, k],
                    device_id=dev, device_id_type=pl.DeviceIdType.MESH).wait()
        for cp in local:
            cp.wait()

    hbm = pl.BlockSpec(memory_space=pltpu.HBM)
    return pl.pallas_call(
        body, out_shape=out_shape, in_specs=[hbm] * n, out_specs=[hbm] * n,
        scratch_shapes=[pltpu.SemaphoreType.DMA((n, N_DEV - 1)), pltpu.SemaphoreType.DMA((n, N_DEV - 1)),
                        pltpu.SemaphoreType.DMA((n,))],
        name=name,
    )(*parts)


def _gather_two_level(parts, name):
    n = len(parts)
    out_shape = [jax.ShapeDtypeStruct((N_DEV,) + p.shape, p.dtype) for p in parts]

    def body(*refs):
        ins, outs = refs[:n], refs[n:2 * n]
        send_sems, recv_sems, local_sems = refs[2 * n:]
        x, y, c = lax.axis_index("x"), lax.axis_index("y"), lax.axis_index("c")
        idx = lambda px, py, pc: 4 * px + 2 * py + pc
        me, sibling = (x, y, c), (x, y, 1 - c)
        chips = [(1 - x, y), (x, 1 - y), (1 - x, 1 - y)]

        def copy(a, k, block, to, src=None):
            slot = outs[a].at[idx(*block)]
            return pltpu.make_async_remote_copy(
                src_ref=slot if src is None else src, dst_ref=slot, send_sem=send_sems.at[a, k], recv_sem=recv_sems.at[a, k],
                device_id=to, device_id_type=pl.DeviceIdType.MESH)

        local = [pltpu.make_async_copy(ins[a], outs[a].at[idx(*me)], local_sems.at[a]) for a in range(n)]
        for cp in local:
            cp.start()
        sent = []
        for a in range(n):
            sent.append(copy(a, 0, me, sibling, src=ins[a]))
            sent += [copy(a, 1 + j, me, (*chip, c), src=ins[a]) for j, chip in enumerate(chips)]
        for cp in sent:
            cp.start()
        for a in range(n):
            for j, chip in enumerate(chips):
                copy(a, 1 + j, (*chip, c), me).wait_recv()
                passed = copy(a, 4 + j, (*chip, c), sibling)
                passed.start()
                sent.append(passed)
        for a in range(n):
            copy(a, 0, sibling, me).wait_recv()
            for j, chip in enumerate(chips):
                copy(a, 4 + j, (*chip, 1 - c), me).wait_recv()
        for cp in sent:
            cp.wait_send()
        for cp in local:
            cp.wait()

    hbm = pl.BlockSpec(memory_space=pltpu.HBM)
    return pl.pallas_call(
        body, out_shape=out_shape, in_specs=[hbm] * n, out_specs=[hbm] * n,
        scratch_shapes=[pltpu.SemaphoreType.DMA((n, N_DEV - 1)), pltpu.SemaphoreType.DMA((n, N_DEV - 1)),
                        pltpu.SemaphoreType.DMA((n,))],
        name=name,
    )(*parts)


def _swap_with_sibling(parts, name):
    n = len(parts)

    def body(*refs):
        ins, outs = refs[:n], refs[n:2 * n]
        send_sems, recv_sems = refs[2 * n:]
        x, y, c = lax.axis_index("x"), lax.axis_index("y"), lax.axis_index("c")
        copies = [pltpu.make_async_remote_copy(
            src_ref=ins[a].at[1 - c], dst_ref=outs[a], send_sem=send_sems.at[a], recv_sem=recv_sems.at[a],
            device_id=(x, y, 1 - c), device_id_type=pl.DeviceIdType.MESH) for a in range(n)]
        for cp in copies:
            cp.start()
        for cp in copies:
            cp.wait()

    hbm = pl.BlockSpec(memory_space=pltpu.HBM)
    return pl.pallas_call(
        body, out_shape=[jax.ShapeDtypeStruct(p.shape[1:], p.dtype) for p in parts], in_specs=[hbm] * n, out_specs=[hbm] * n,
        scratch_shapes=[pltpu.SemaphoreType.DMA((n,)), pltpu.SemaphoreType.DMA((n,))],
        name=name,
    )(*parts)


def _pair_sum(a, b, name):
    R, C = a.shape
    rb = _adam_rows(R)

    def body(a_ref, b_ref, o_ref):
        o_ref[...] = (a_ref[...].astype(F32) + b_ref[...].astype(F32)).astype(BF16)

    blk = pl.BlockSpec((rb, C), lambda i: (i, 0))
    return pl.pallas_call(body, grid=(R // rb,), in_specs=[blk, blk], out_specs=blk,
                          out_shape=jax.ShapeDtypeStruct((R, C), BF16), name=name, compiler_params=_cparams())(a, b)


def _exchange_chips(parts, name):
    n = len(parts)
    n_chips = N_DEV // 2

    def body(*refs):
        ins, outs = refs[:n], refs[n:2 * n]
        send_sems, recv_sems, local_sems = refs[2 * n:]
        x, y, c = lax.axis_index("x"), lax.axis_index("y"), lax.axis_index("c")
        mine = 2 * x + y
        chips = [(1 - x, y), (x, 1 - y), (1 - x, 1 - y)]
        local = [pltpu.make_async_copy(ins[a].at[mine], outs[a].at[mine], local_sems.at[a]) for a in range(n)]
        for cp in local:
            cp.start()
        for a in range(n):
            for k, (px, py) in enumerate(chips):
                pltpu.make_async_remote_copy(
                    src_ref=ins[a].at[2 * px + py], dst_ref=outs[a].at[mine], send_sem=send_sems.at[a, k],
                    recv_sem=recv_sems.at[a, k], device_id=(px, py, c), device_id_type=pl.DeviceIdType.MESH).start()
        for a in range(n):
            for k, (px, py) in enumerate(chips):
                pltpu.make_async_remote_copy(
                    src_ref=ins[a].at[2 * px + py], dst_ref=outs[a].at[2 * px + py], send_sem=send_sems.at[a, k],
                    recv_sem=recv_sems.at[a, k], device_id=(px, py, c), device_id_type=pl.DeviceIdType.MESH).wait()
        for cp in local:
            cp.wait()

    hbm = pl.BlockSpec(memory_space=pltpu.HBM)
    return pl.pallas_call(
        body, out_shape=[jax.ShapeDtypeStruct(p.shape, p.dtype) for p in parts], in_specs=[hbm] * n, out_specs=[hbm] * n,
        scratch_shapes=[pltpu.SemaphoreType.DMA((n, n_chips - 1)), pltpu.SemaphoreType.DMA((n, n_chips - 1)),
                        pltpu.SemaphoreType.DMA((n,))],
        name=name,
    )(*parts)


def _adam_rows(rows):
    for rb in (128, 64, 40, 16, 8):
        if rows % rb == 0:
            return rb
    return rows


def _adamw(stack, w, m, v, name):
    R, C = w.shape
    rb = _adam_rows(R)
    slots = stack.shape[0]

    def body(s_ref, w_ref, m_ref, v_ref, g_ref, d_ref, nm_ref, nv_ref):
        g = s_ref[0].astype(F32)
        for s in range(1, slots):
            g = g + s_ref[s].astype(F32)
        nm = ADAM_B1 * m_ref[...] + (1.0 - ADAM_B1) * g
        nv = ADAM_B2 * v_ref[...] + (1.0 - ADAM_B2) * (g * g)
        m_hat = nm / (1.0 - ADAM_B1 ** ADAM_STEP)
        v_hat = nv / (1.0 - ADAM_B2 ** ADAM_STEP)
        g_ref[...] = g
        d_ref[...] = -ADAM_LR * (m_hat / (jnp.sqrt(v_hat) + ADAM_EPS) + ADAM_WD * w_ref[...])
        nm_ref[...] = nm
        nv_ref[...] = nv

    blk = pl.BlockSpec((rb, C), lambda i: (i, 0))
    return pl.pallas_call(
        body, grid=(R // rb,),
        in_specs=[pl.BlockSpec((slots, rb, C), lambda i: (0, i, 0)), blk, blk, blk],
        out_specs=[blk] * 4, out_shape=[jax.ShapeDtypeStruct((R, C), F32)] * 4,
        name=name, compiler_params=_cparams(),
    )(stack, w, m, v)


def _pad_rows8(a):
    return jnp.concatenate([a, jnp.zeros((8 - a.shape[0], a.shape[1]), a.dtype)], axis=0) if a.shape[0] < 8 else a


def _pack_small(norm_w, qnw, knw, sinks, a_log, dt_bias, onw, extra):
    z = lambda n: jnp.zeros((1, n), F32)
    row = jnp.concatenate([norm_w, qnw, knw, sinks, a_log, dt_bias, z(80), onw, extra, z(512)], axis=1)
    return row.reshape(16, 128)


def _unpack_small(p):
    row = p.reshape(1, 2048)
    cut = lambda a, n: row[:, a:a + n]
    return (cut(0, 1024), cut(1024, 64), cut(1088, 64), cut(1152, 16), cut(1168, 16), cut(1184, 16), cut(1280, 128),
            cut(1408, 128))


def _pack_rows(w_in_a, w_in_d, w_out_a, w_out_d, meta, conv, dn_norm):
    a = jnp.concatenate([w_in_a, w_in_d], axis=1)
    b = jnp.concatenate([w_out_a, w_out_d], axis=0)
    c = jnp.concatenate([meta, conv.reshape(16, 128), _pad_rows8(dn_norm)], axis=0)
    return a, b, c


def _unpack_rows(a, b, c):
    return (a[:, :288], a[:, 288:], b[:128], b[128:], c[:16], c[16:32].reshape(4, 512), c[32:33])


def _local_step(x, front, target, w):
    xn0, q, kv, gate = _attn_in_fwd(x, front, w["attn_norm_w"], w["attn_w_in"])
    o = _attn_core_fwd(q, kv, w["attn_sinks"], w["attn_q_norm_w"], w["attn_k_norm_w"])
    h1, xn1, qkv, z, ba = _attn_out_dn_in_fwd(o, gate, x, front, w["attn_w_out"], w["dn_norm_w"], w["dn_w_in"])
    qn, kn, v, bg, conv_out = _dn_conv_fwd(qkv, ba, w["dn_conv_w"], w["dn_a_log"], w["dn_dt_bias"])
    o_dn, ssave, *saved = _dn_scan_fwd(qn, kn, v, bg)
    dy, og_dn, loss = _dn_out_fwd(o_dn, z, h1, target, w["dn_w_out"], w["dn_o_norm_w"])

    g = {}
    do_dn, dz, g["dn_o_norm_w"] = _dn_out_bwd(dy, o_dn, z, w["dn_w_out"], w["dn_o_norm_w"])
    g["dn_w_out"] = _wgrad(og_dn, dy, 1024, "wgrad_dn_out")
    dqn, dkn, dv, dbg = _dn_scan_bwd(do_dn, qn, kn, v, bg, ssave, saved)
    dqkv, dba, g["dn_conv_w"], g["dn_a_log"], g["dn_dt_bias"] = _dn_conv_bwd(
        dqn, dkn, dv, dbg, qkv, conv_out, ba, w["dn_conv_w"], w["dn_a_log"], w["dn_dt_bias"])
    dh1, g["dn_norm_w"] = _dn_in_bwd(dqkv, dz, dba, h1, dy, w["dn_norm_w"], w["dn_w_in"])
    g["dn_w_in"] = jnp.concatenate([_wgrad(xn1, dqkv, 1024, "wgrad_dn_qkv"), _wgrad(xn1, dz, 1024, "wgrad_dn_z"),
                                    _wgrad(xn1, dba, 32, "wgrad_dn_ba")], axis=1)
    do, dgate, g["attn_w_out"] = _attn_out_bwd(dh1, o, gate, w["attn_w_out"])
    dq, dkv, g["attn_sinks"], g["attn_q_norm_w"], g["attn_k_norm_w"] = _attn_core_bwd(
        do, q, kv, w["attn_sinks"], w["attn_q_norm_w"], w["attn_k_norm_w"])
    grad_x, dfront, g["attn_norm_w"] = _attn_in_bwd(dq, dkv, dgate, x, front, dh1, w["attn_norm_w"], w["attn_w_in"])
    g["meta_tokens"] = dfront[FRONT_PAD:]
    g["attn_w_in"] = jnp.concatenate([_wgrad(xn0, dq, 1024, "wgrad_attn_q"), _wgrad(xn0, dkv, 256, "wgrad_attn_kv"),
                                      _wgrad(xn0, dgate, 1024, "wgrad_attn_gate")], axis=1)
    return loss, grad_x, g


WEIGHTS = ['meta_tokens', 'attn_norm_w', 'attn_w_in', 'attn_q_norm_w', 'attn_k_norm_w', 'attn_sinks', 'attn_w_out',
           'dn_norm_w', 'dn_w_in', 'dn_conv_w', 'dn_a_log', 'dn_dt_bias', 'dn_o_norm_w', 'dn_w_out']
SMALL = ['attn_norm_w', 'attn_q_norm_w', 'attn_k_norm_w', 'attn_sinks', 'dn_a_log', 'dn_dt_bias', 'dn_o_norm_w']


def kernel(x, meta_tokens, attn_norm_w, attn_w_in, attn_q_norm_w, attn_k_norm_w, attn_sinks, attn_w_out, dn_norm_w, dn_w_in, dn_conv_w, dn_a_log, dn_dt_bias, dn_o_norm_w, dn_w_out, loss_target, m_meta_tokens, m_attn_norm_w, m_attn_w_in, m_attn_q_norm_w, m_attn_k_norm_w, m_attn_sinks, m_attn_w_out, m_dn_norm_w, m_dn_w_in, m_dn_conv_w, m_dn_a_log, m_dn_dt_bias, m_dn_o_norm_w, m_dn_w_out, v_meta_tokens, v_attn_norm_w, v_attn_w_in, v_attn_q_norm_w, v_attn_k_norm_w, v_attn_sinks, v_attn_w_out, v_dn_norm_w, v_dn_w_in, v_dn_conv_w, v_dn_a_log, v_dn_dt_bias, v_dn_o_norm_w, v_dn_w_out):
    shard = dict(meta_tokens=meta_tokens, attn_norm_w=attn_norm_w, attn_w_in=attn_w_in[0], attn_q_norm_w=attn_q_norm_w,
                 attn_k_norm_w=attn_k_norm_w, attn_sinks=attn_sinks, attn_w_out=attn_w_out[0], dn_norm_w=dn_norm_w,
                 dn_w_in=dn_w_in[0], dn_conv_w=dn_conv_w[0], dn_a_log=dn_a_log, dn_dt_bias=dn_dt_bias,
                 dn_o_norm_w=dn_o_norm_w, dn_w_out=dn_w_out[0])
    mom_m = dict(meta_tokens=m_meta_tokens, attn_norm_w=m_attn_norm_w, attn_w_in=m_attn_w_in[0], attn_q_norm_w=m_attn_q_norm_w,
                 attn_k_norm_w=m_attn_k_norm_w, attn_sinks=m_attn_sinks, attn_w_out=m_attn_w_out[0], dn_norm_w=m_dn_norm_w,
                 dn_w_in=m_dn_w_in[0], dn_conv_w=m_dn_conv_w[0], dn_a_log=m_dn_a_log, dn_dt_bias=m_dn_dt_bias,
                 dn_o_norm_w=m_dn_o_norm_w, dn_w_out=m_dn_w_out[0])
    mom_v = dict(meta_tokens=v_meta_tokens, attn_norm_w=v_attn_norm_w, attn_w_in=v_attn_w_in[0], attn_q_norm_w=v_attn_q_norm_w,
                 attn_k_norm_w=v_attn_k_norm_w, attn_sinks=v_attn_sinks, attn_w_out=v_attn_w_out[0], dn_norm_w=v_dn_norm_w,
                 dn_w_in=v_dn_w_in[0], dn_conv_w=v_dn_conv_w[0], dn_a_log=v_dn_a_log, dn_dt_bias=v_dn_dt_bias,
                 dn_o_norm_w=v_dn_o_norm_w, dn_w_out=v_dn_w_out[0])

    def rows_of(d):
        return _pack_rows(d["attn_w_in"], d["dn_w_in"], d["attn_w_out"], d["dn_w_out"], d["meta_tokens"], d["dn_conv_w"],
                          d["dn_norm_w"])

    def small_of(d, extra):
        return _pack_small(*[d[k] for k in SMALL], extra)

    wa, wb, wc = rows_of(shard)
    ga, gb, gc = _gather_two_level([wa.astype(BF16), wb.astype(BF16), wc], "gather_weights")
    full = {k: shard[k] for k in SMALL}
    full["attn_w_in"] = ga[:, :, :288].transpose(1, 0, 2).reshape(1024, 2304)
    full["dn_w_in"] = ga[:, :, 288:].transpose(1, 0, 2).reshape(1024, 6176)
    full["attn_w_out"] = gb[:, :128].reshape(1024, 1024)
    full["dn_w_out"] = gb[:, 128:].reshape(2048, 1024)
    meta_full = gc[:, :16].transpose(1, 0, 2).reshape(N_META, 1024)
    full["dn_conv_w"] = gc[:, 16:32].reshape(N_DEV, 4, 512).transpose(1, 0, 2).reshape(4, 4096)
    full["dn_norm_w"] = gc[:, 32].reshape(1, 1024)

    front = jnp.concatenate([jnp.zeros((FRONT_PAD, D_MODEL), F32), meta_full], axis=0)
    loss, grad_x, g = _local_step(x[0], front, loss_target[0], full)
    grad_x = grad_x[None]

    pa = jnp.concatenate([g["attn_w_in"].reshape(1024, N_DEV, 288), g["dn_w_in"].reshape(1024, N_DEV, 772)],
                         axis=2).transpose(1, 0, 2)
    pb = jnp.concatenate([g["attn_w_out"].reshape(N_DEV, 128, 1024), g["dn_w_out"].reshape(N_DEV, 256, 1024)], axis=1)
    dn_norm8 = jnp.concatenate([g["dn_norm_w"].reshape(N_DEV, 1, 128), jnp.zeros((N_DEV, 7, 128), F32)], axis=1)
    pc = jnp.concatenate([g["meta_tokens"].reshape(N_META, N_DEV, 128).transpose(1, 0, 2),
                          g["dn_conv_w"].reshape(4, N_DEV, 512).transpose(1, 0, 2).reshape(N_DEV, 16, 128), dn_norm8], axis=1)
    ps = small_of(g, loss)
    c = lax.axis_index("c")
    by_core = lambda p: p.astype(BF16).reshape((N_DEV // 2, 2) + p.shape[1:]).swapaxes(0, 1)
    pa2, pb2 = by_core(pa), by_core(pb)
    ra, rb_ = _swap_with_sibling([pa2, pb2], "swap_grads")
    own = lambda p2: lax.dynamic_index_in_dim(p2, c, axis=0, keepdims=False)
    flat = lambda t: t.reshape((-1,) + t.shape[2:])
    sa = _pair_sum(flat(own(pa2)), flat(ra), "pair_sum_a").reshape(ra.shape)
    sb = _pair_sum(flat(own(pb2)), flat(rb_), "pair_sum_b").reshape(rb_.shape)
    xa, xb = _exchange_chips([sa, sb], "exchange_grads")
    xc, xs = _exchange([pc, ps], [True, False], "exchange_small")

    out = {}
    ma, mb, mc = rows_of(mom_m)
    va, vb, vc = rows_of(mom_v)
    ra = _adamw(xa, wa, ma, va, "adamw_a")
    rb = _adamw(xb, wb, mb, vb, "adamw_b")
    rc = _adamw(xc, wc, mc, vc, "adamw_c")
    zero = jnp.zeros((1, 128), F32)
    rs = _adamw(xs, small_of(shard, zero), small_of(mom_m, zero), small_of(mom_v, zero), "adamw_small")
    row_names = ["attn_w_in", "dn_w_in", "attn_w_out", "dn_w_out", "meta_tokens", "dn_conv_w", "dn_norm_w"]
    lead = {"attn_w_in", "dn_w_in", "attn_w_out", "dn_w_out", "dn_conv_w"}
    for kind in range(4):
        vals = dict(zip(row_names, _unpack_rows(ra[kind], rb[kind], rc[kind])))
        small = _unpack_small(rs[kind])
        vals.update(dict(zip(SMALL, small[:7])))
        if kind == 0:
            loss_total = small[7][0, 0]
        out[kind] = [vals[k][None] if k in lead else vals[k] for k in WEIGHTS]
    return (loss_total, grad_x, *out[0], *out[1], *out[2], *out[3])
```
